```python
import math
import jax, jax.numpy as jnp
from jax import lax
import numpy as np

D_MODEL = 2048
BATCH = 8
SEQ = 4096
DEPTH = 2

MEM_LEN = 256
CHUNK = 128
GMLP_WIDTH = 1024
GMLP_GROUPS = 8
GMLP_GROUP_DIM = GMLP_WIDTH // GMLP_GROUPS
N_Q_HEADS = 16
N_KV_HEADS = 4
HEAD_DIM = 64
ATTN_WIDTH = N_Q_HEADS * HEAD_DIM
KV_WIDTH = N_KV_HEADS * HEAD_DIM
WINDOW = 128
ROPE_THETA = 10000.0
X_HEADS = 4
X_HEAD_DIM = 128
X_WIDTH = X_HEADS * X_HEAD_DIM
D_FF = 4 * D_MODEL
LN_EPS = 1e-5
ALPHA = (2 * DEPTH) ** 0.25
BETA = (8 * DEPTH) ** -0.25

OFF_U = GMLP_WIDTH
OFF_V = 2 * GMLP_WIDTH
OFF_Q = OFF_V + ATTN_WIDTH
OFF_K = OFF_Q + KV_WIDTH
OFF_VA = OFF_K + KV_WIDTH
OFF_GA = OFF_VA + D_MODEL
IN_WIDTH = OFF_GA + D_MODEL

kernel_name = "hybrid_gmlp_swa_sink_deepnorm_decoder"


def layer_norm(x, g, b):
    xf = x.astype(jnp.float32)
    mu = jnp.mean(xf, axis=-1, keepdims=True)
    var = jnp.mean(jnp.square(xf - mu), axis=-1, keepdims=True)
    y = (xf - mu) * lax.rsqrt(var + LN_EPS)
    return (y * g.astype(jnp.float32) + b.astype(jnp.float32)).astype(x.dtype)


def rope_tables(seq):
    inv = 1.0 / (ROPE_THETA ** (jnp.arange(0, HEAD_DIM, 2, dtype=jnp.float32) / HEAD_DIM))
    pos = jnp.arange(seq, dtype=jnp.float32)
    ang = pos[:, None] * inv[None, :]
    return jnp.cos(ang), jnp.sin(ang)


def apply_rope(x, cos, sin):
    xf = x.astype(jnp.float32)
    x1, x2 = jnp.split(xf, 2, axis=-1)
    c = cos[None, :, None, :]
    s = sin[None, :, None, :]
    return jnp.concatenate([x1 * c - x2 * s, x2 * c + x1 * s], axis=-1).astype(x.dtype)


def chunked_spatial_gating(u, v, ln_g, ln_b, w_s, b_s):
    bsz, seq, _ = u.shape
    n_chunks = seq // CHUNK
    v = layer_norm(v, ln_g, ln_b)
    vc = v.reshape(bsz, n_chunks, CHUNK, GMLP_GROUPS, GMLP_GROUP_DIM)
    causal = jnp.tril(jnp.ones((CHUNK, CHUNK), dtype=bool))
    w = jnp.where(causal[None], w_s, 0.0)
    mixed = jnp.einsum('gts,bnsgc->bntgc', w, vc) + b_s.T[None, None, :, :, None]
    return u * mixed.reshape(bsz, seq, GMLP_WIDTH)


def sliding_window_attention(q, k, v, sinks):
    bsz, seq, _, _ = q.shape
    n_blk = seq // WINDOW
    grp = N_Q_HEADS // N_KV_HEADS
    qb = q.reshape(bsz, n_blk, WINDOW, N_KV_HEADS, grp, HEAD_DIM)
    kb = k.reshape(bsz, n_blk, WINDOW, N_KV_HEADS, HEAD_DIM)
    vb = v.reshape(bsz, n_blk, WINDOW, N_KV_HEADS, HEAD_DIM)
    pad = ((0, 0), (1, 0), (0, 0), (0, 0), (0, 0))
    kk = jnp.concatenate([jnp.pad(kb, pad)[:, :-1], kb], axis=2)
    vv = jnp.concatenate([jnp.pad(vb, pad)[:, :-1], vb], axis=2)
    scores = jnp.einsum('bnqhgd,bnkhd->bnhgqk', qb, kk).astype(jnp.float32) * (HEAD_DIM ** -0.5)
    q_loc = jnp.arange(WINDOW)[:, None]
    k_loc = jnp.arange(2 * WINDOW)[None, :]
    band = (k_loc <= q_loc + WINDOW) & (k_loc > q_loc)
    blk = jnp.arange(n_blk)[:, None, None]
    valid = band[None] & (blk * WINDOW + k_loc[None] - WINDOW >= 0)
    scores = jnp.where(valid[None, :, None, None], scores, -jnp.inf)
    sink = sinks.astype(jnp.float32).reshape(N_KV_HEADS, grp)[None, None, :, :, None, None]
    m = jnp.maximum(jnp.max(scores, axis=-1, keepdims=True), sink)
    p = jnp.exp(scores - m)
    probs = (p / (jnp.sum(p, axis=-1, keepdims=True) + jnp.exp(sink - m))).astype(v.dtype)
    out = jnp.einsum('bnhgqk,bnkhd->bnqhgd', probs, vv)
    return out.reshape(bsz, seq, ATTN_WIDTH)


def hybrid_mixer(x, w_in, b_gate, ln_v_g, ln_v_b, w_s, b_s, sinks, w_br_a, w_br_b, w_o, cos, sin):
    bsz, seq, _ = x.shape
    proj = x @ w_in
    u, v, q, k, va, ga, gb = jnp.split(proj, [OFF_U, OFF_V, OFF_Q, OFF_K, OFF_VA, OFF_GA], axis=-1)
    ya = chunked_spatial_gating(jax.nn.gelu(u), jax.nn.gelu(v), ln_v_g, ln_v_b, w_s, b_s) @ w_br_a
    q = apply_rope(q.reshape(bsz, seq, N_Q_HEADS, HEAD_DIM), cos, sin)
    k = apply_rope(k.reshape(bsz, seq, N_KV_HEADS, HEAD_DIM), cos, sin)
    va = va.reshape(bsz, seq, N_KV_HEADS, HEAD_DIM)
    yb = sliding_window_attention(q, k, va, sinks) @ w_br_b
    merged = jax.nn.sigmoid(ga + b_gate[:D_MODEL]) * ya + jax.nn.sigmoid(gb + b_gate[D_MODEL:]) * yb
    return merged @ w_o


def memory_cross_attention(x, mem, w_xq, w_xkv, w_xo):
    bsz, seq, _ = x.shape
    q = (x @ w_xq).reshape(bsz, seq, X_HEADS, X_HEAD_DIM)
    k, v = jnp.split(mem @ w_xkv, 2, axis=-1)
    k = k.reshape(bsz, MEM_LEN, X_HEADS, X_HEAD_DIM)
    v = v.reshape(bsz, MEM_LEN, X_HEADS, X_HEAD_DIM)
    s = jnp.einsum('bqhd,bkhd->bhqk', q, k).astype(jnp.float32) * (X_HEAD_DIM ** -0.5)
    p = jax.nn.softmax(s, axis=-1).astype(v.dtype)
    o = jnp.einsum('bhqk,bkhd->bqhd', p, v).reshape(bsz, seq, X_WIDTH)
    return o @ w_xo


def squared_relu_mlp(x, w_up, w_down):
    return jnp.square(jax.nn.relu(x @ w_up)) @ w_down


def _fwd_setup_inputs(seed: int = 0) -> dict:
    key = jax.random.key(seed)
    ks = jax.random.split(key, 26)
    f32 = jnp.float32
    L, D = DEPTH, D_MODEL

    def nrm(k, shape, scale):
        return jax.random.normal(k, shape, f32) * scale

    return {
        "x": nrm(ks[0], (BATCH, SEQ, D), 1.0),
        "mem": nrm(ks[1], (BATCH, MEM_LEN, D), 1.0),
        "w_in": nrm(ks[2], (L, D, IN_WIDTH), D ** -0.5),
        "b_gate": nrm(ks[3], (L, 2 * D), 0.1),
        "ln_v_g": 1.0 + nrm(ks[4], (L, GMLP_WIDTH), 0.02),
        "ln_v_b": nrm(ks[5], (L, GMLP_WIDTH), 0.02),
        "w_s": nrm(ks[6], (L, GMLP_GROUPS, CHUNK, CHUNK), 0.05),
        "b_s": 1.0 + nrm(ks[7], (L, GMLP_GROUPS, CHUNK), 0.1),
        "sinks": nrm(ks[8], (L, N_Q_HEADS), 0.5),
        "w_br_a": nrm(ks[9], (L, GMLP_WIDTH, D), GMLP_WIDTH ** -0.5),
        "w_br_b": nrm(ks[10], (L, ATTN_WIDTH, D), ATTN_WIDTH ** -0.5),
        "w_o": nrm(ks[11], (L, D, D), BETA * D ** -0.5),
        "ln1_g": 1.0 + nrm(ks[12], (L, D), 0.02),
        "ln1_b": nrm(ks[13], (L, D), 0.02),
        "w_xq": nrm(ks[14], (L, D, X_WIDTH), D ** -0.5),
        "w_xkv": nrm(ks[15], (L, D, 2 * X_WIDTH), D ** -0.5),
        "w_xo": nrm(ks[16], (L, X_WIDTH, D), BETA * X_WIDTH ** -0.5),
        "ln2_g": 1.0 + nrm(ks[17], (L, D), 0.02),
        "ln2_b": nrm(ks[18], (L, D), 0.02),
        "w_up": nrm(ks[19], (L, D, D_FF), D ** -0.5),
        "w_down": nrm(ks[20], (L, D_FF, D), BETA * D_FF ** -0.5),
        "ln3_g": 1.0 + nrm(ks[21], (L, D), 0.02),
        "ln3_b": nrm(ks[22], (L, D), 0.02),
    }


def _fwd_reference(x, mem, w_in, b_gate, ln_v_g, ln_v_b, w_s, b_s, sinks, w_br_a, w_br_b, w_o,
              ln1_g, ln1_b, w_xq, w_xkv, w_xo, ln2_g, ln2_b, w_up, w_down, ln3_g, ln3_b):
    cos, sin = rope_tables(x.shape[1])
    for l in range(DEPTH):
        y = hybrid_mixer(x, w_in[l], b_gate[l], ln_v_g[l], ln_v_b[l], w_s[l], b_s[l], sinks[l],
                         w_br_a[l], w_br_b[l], w_o[l], cos, sin)
        x = layer_norm(ALPHA * x + y, ln1_g[l], ln1_b[l])
        y = memory_cross_attention(x, mem, w_xq[l], w_xkv[l], w_xo[l])
        x = layer_norm(ALPHA * x + y, ln2_g[l], ln2_b[l])
        y = squared_relu_mlp(x, w_up[l], w_down[l])
        x = layer_norm(ALPHA * x + y, ln3_g[l], ln3_b[l])
    return x


import jax as _jax
import jax.numpy as _jnp

TWIN_FORMAT = 'train_step'
FWD_PARAMS = ['x', 'mem', 'w_in', 'b_gate', 'ln_v_g', 'ln_v_b', 'w_s', 'b_s', 'sinks', 'w_br_a', 'w_br_b', 'w_o', 'ln1_g', 'ln1_b', 'w_xq', 'w_xkv', 'w_xo', 'ln2_g', 'ln2_b', 'w_up', 'w_down', 'ln3_g', 'ln3_b']
TWIN_WEIGHTS = ['w_in', 'b_gate', 'ln_v_g', 'ln_v_b', 'w_s', 'b_s', 'sinks', 'w_br_a', 'w_br_b', 'w_o', 'ln1_g', 'ln1_b', 'w_xq', 'w_xkv', 'w_xo', 'ln2_g', 'ln2_b', 'w_up', 'w_down', 'ln3_g', 'ln3_b']
TWIN_DIFF_INPUT = 'x'
TWIN_INPUTS = ['x', 'mem', 'w_in', 'b_gate', 'ln_v_g', 'ln_v_b', 'w_s', 'b_s', 'sinks', 'w_br_a', 'w_br_b', 'w_o', 'ln1_g', 'ln1_b', 'w_xq', 'w_xkv', 'w_xo', 'ln2_g', 'ln2_b', 'w_up', 'w_down', 'ln3_g', 'ln3_b', 'loss_target', 'm_w_in', 'm_b_gate', 'm_ln_v_g', 'm_ln_v_b', 'm_w_s', 'm_b_s', 'm_sinks', 'm_w_br_a', 'm_w_br_b', 'm_w_o', 'm_ln1_g', 'm_ln1_b', 'm_w_xq', 'm_w_xkv', 'm_w_xo', 'm_ln2_g', 'm_ln2_b', 'm_w_up', 'm_w_down', 'm_ln3_g', 'm_ln3_b', 'v_w_in', 'v_b_gate', 'v_ln_v_g', 'v_ln_v_b', 'v_w_s', 'v_b_s', 'v_sinks', 'v_w_br_a', 'v_w_br_b', 'v_w_o', 'v_ln1_g', 'v_ln1_b', 'v_w_xq', 'v_w_xkv', 'v_w_xo', 'v_ln2_g', 'v_ln2_b', 'v_w_up', 'v_w_down', 'v_ln3_g', 'v_ln3_b']
TWIN_OUTPUTS = ['loss', 'grad_x', 'grad_w_in', 'grad_b_gate', 'grad_ln_v_g', 'grad_ln_v_b', 'grad_w_s', 'grad_b_s', 'grad_sinks', 'grad_w_br_a', 'grad_w_br_b', 'grad_w_o', 'grad_ln1_g', 'grad_ln1_b', 'grad_w_xq', 'grad_w_xkv', 'grad_w_xo', 'grad_ln2_g', 'grad_ln2_b', 'grad_w_up', 'grad_w_down', 'grad_ln3_g', 'grad_ln3_b', 'delta_w_in', 'delta_b_gate', 'delta_ln_v_g', 'delta_ln_v_b', 'delta_w_s', 'delta_b_s', 'delta_sinks', 'delta_w_br_a', 'delta_w_br_b', 'delta_w_o', 'delta_ln1_g', 'delta_ln1_b', 'delta_w_xq', 'delta_w_xkv', 'delta_w_xo', 'delta_ln2_g', 'delta_ln2_b', 'delta_w_up', 'delta_w_down', 'delta_ln3_g', 'delta_ln3_b', 'new_m_w_in', 'new_m_b_gate', 'new_m_ln_v_g', 'new_m_ln_v_b', 'new_m_w_s', 'new_m_b_s', 'new_m_sinks', 'new_m_w_br_a', 'new_m_w_br_b', 'new_m_w_o', 'new_m_ln1_g', 'new_m_ln1_b', 'new_m_w_xq', 'new_m_w_xkv', 'new_m_w_xo', 'new_m_ln2_g', 'new_m_ln2_b', 'new_m_w_up', 'new_m_w_down', 'new_m_ln3_g', 'new_m_ln3_b', 'new_v_w_in', 'new_v_b_gate', 'new_v_ln_v_g', 'new_v_ln_v_b', 'new_v_w_s', 'new_v_b_s', 'new_v_sinks', 'new_v_w_br_a', 'new_v_w_br_b', 'new_v_w_o', 'new_v_ln1_g', 'new_v_ln1_b', 'new_v_w_xq', 'new_v_w_xkv', 'new_v_w_xo', 'new_v_ln2_g', 'new_v_ln2_b', 'new_v_w_up', 'new_v_w_down', 'new_v_ln3_g', 'new_v_ln3_b']
TWIN_LEAF_KINDS = {'loss': 'loss', 'grad_x': 'grad_x', 'grad_w_in': 'grad_w', 'grad_b_gate': 'grad_w', 'grad_ln_v_g': 'grad_w', 'grad_ln_v_b': 'grad_w', 'grad_w_s': 'grad_w', 'grad_b_s': 'grad_w', 'grad_sinks': 'grad_w', 'grad_w_br_a': 'grad_w', 'grad_w_br_b': 'grad_w', 'grad_w_o': 'grad_w', 'grad_ln1_g': 'grad_w', 'grad_ln1_b': 'grad_w', 'grad_w_xq': 'grad_w', 'grad_w_xkv': 'grad_w', 'grad_w_xo': 'grad_w', 'grad_ln2_g': 'grad_w', 'grad_ln2_b': 'grad_w', 'grad_w_up': 'grad_w', 'grad_w_down': 'grad_w', 'grad_ln3_g': 'grad_w', 'grad_ln3_b': 'grad_w', 'delta_w_in': 'delta_w', 'delta_b_gate': 'delta_w', 'delta_ln_v_g': 'delta_w', 'delta_ln_v_b': 'delta_w', 'delta_w_s': 'delta_w', 'delta_b_s': 'delta_w', 'delta_sinks': 'delta_w', 'delta_w_br_a': 'delta_w', 'delta_w_br_b': 'delta_w', 'delta_w_o': 'delta_w', 'delta_ln1_g': 'delta_w', 'delta_ln1_b': 'delta_w', 'delta_w_xq': 'delta_w', 'delta_w_xkv': 'delta_w', 'delta_w_xo': 'delta_w', 'delta_ln2_g': 'delta_w', 'delta_ln2_b': 'delta_w', 'delta_w_up': 'delta_w', 'delta_w_down': 'delta_w', 'delta_ln3_g': 'delta_w', 'delta_ln3_b': 'delta_w', 'new_m_w_in': 'new_m', 'new_m_b_gate': 'new_m', 'new_m_ln_v_g': 'new_m', 'new_m_ln_v_b': 'new_m', 'new_m_w_s': 'new_m', 'new_m_b_s': 'new_m', 'new_m_sinks': 'new_m', 'new_m_w_br_a': 'new_m', 'new_m_w_br_b': 'new_m', 'new_m_w_o': 'new_m', 'new_m_ln1_g': 'new_m', 'new_m_ln1_b': 'new_m', 'new_m_w_xq': 'new_m', 'new_m_w_xkv': 'new_m', 'new_m_w_xo': 'new_m', 'new_m_ln2_g': 'new_m', 'new_m_ln2_b': 'new_m', 'new_m_w_up': 'new_m', 'new_m_w_down': 'new_m', 'new_m_ln3_g': 'new_m', 'new_m_ln3_b': 'new_m', 'new_v_w_in': 'new_v', 'new_v_b_gate': 'new_v', 'new_v_ln_v_g': 'new_v', 'new_v_ln_v_b': 'new_v', 'new_v_w_s': 'new_v', 'new_v_b_s': 'new_v', 'new_v_sinks': 'new_v', 'new_v_w_br_a': 'new_v', 'new_v_w_br_b': 'new_v', 'new_v_w_o': 'new_v', 'new_v_ln1_g': 'new_v', 'new_v_ln1_b': 'new_v', 'new_v_w_xq': 'new_v', 'new_v_w_xkv': 'new_v', 'new_v_w_xo': 'new_v', 'new_v_ln2_g': 'new_v', 'new_v_ln2_b': 'new_v', 'new_v_w_up': 'new_v', 'new_v_w_down': 'new_v', 'new_v_ln3_g': 'new_v', 'new_v_ln3_b': 'new_v'}


def _forward(args):
    return _fwd_reference(*[args[k] for k in FWD_PARAMS])


def _output_shape():
    def fwd():
        inp = _fwd_setup_inputs(0)
        return _fwd_reference(*[inp[k] for k in FWD_PARAMS])
    out = _jax.eval_shape(fwd)
    return out.shape, out.dtype

N_MICROBATCH = 1
ADAM_LR = 0.001
ADAM_B1 = 0.9
ADAM_B2 = 0.999
ADAM_EPS = 1e-08
ADAM_WD = 0.01
ADAM_STEP = 10
PER_EXAMPLE_BATCH_AXIS = {'x': 0, 'mem': 0, 'loss_target': 0}
SHARED_INPUTS = []
_WEIGHT_DTYPES = {'w_in': _jnp.float32, 'b_gate': _jnp.float32, 'ln_v_g': _jnp.float32, 'ln_v_b': _jnp.float32, 'w_s': _jnp.float32, 'b_s': _jnp.float32, 'sinks': _jnp.float32, 'w_br_a': _jnp.float32, 'w_br_b': _jnp.float32, 'w_o': _jnp.float32, 'ln1_g': _jnp.float32, 'ln1_b': _jnp.float32, 'w_xq': _jnp.float32, 'w_xkv': _jnp.float32, 'w_xo': _jnp.float32, 'ln2_g': _jnp.float32, 'ln2_b': _jnp.float32, 'w_up': _jnp.float32, 'w_down': _jnp.float32, 'ln3_g': _jnp.float32, 'ln3_b': _jnp.float32}
MOMENT_SCALE = {'w_in': 8.328513e-03, 'b_gate': 5.309313e-03, 'ln_v_g': 6.500416e-03, 'ln_v_b': 6.577306e-03, 'w_s': 1.144777e-02, 'b_s': 1.621943e-02, 'sinks': 4.504710e-03, 'w_br_a': 1.697786e-02, 'w_br_b': 5.399014e-03, 'w_o': 3.536885e-02, 'ln1_g': 4.675808e-01, 'ln1_b': 2.853899e-01, 'w_xq': 6.742789e-03, 'w_xkv': 7.181964e-03, 'w_xo': 7.696344e-03, 'ln2_g': 4.675405e-01, 'ln2_b': 2.855069e-01, 'w_up': 2.098710e-02, 'w_down': 1.079966e-01, 'ln3_g': 1.137210e+01, 'ln3_b': 2.633542e+00}


def _to_microbatches(a, axis):
    t = _jnp.moveaxis(a, axis, 0)
    t = t.reshape((N_MICROBATCH, t.shape[0] // N_MICROBATCH) + t.shape[1:])
    return _jnp.moveaxis(t, 1, axis + 1)


def setup_inputs(seed: int = 0) -> dict:
    inp = _fwd_setup_inputs(seed)
    key = _jax.random.fold_in(_jax.random.key(seed), 7919)
    shape, _ = _output_shape()
    out = dict(inp)
    out["loss_target"] = _jax.random.normal(_jax.random.fold_in(key, 0), shape, _jnp.float32)
    for i, name in enumerate(TWIN_WEIGHTS):
        w = inp[name].astype(_jnp.float32)
        if MOMENT_SCALE is None:
            s = _jnp.sqrt(_jnp.mean(_jnp.square(w)) + 1e-30)
        else:
            s = MOMENT_SCALE[name]
        km, kv = _jax.random.split(_jax.random.fold_in(key, i + 1))
        out[name] = w
        out["m_" + name] = s * _jax.random.normal(km, w.shape, _jnp.float32)
        out["v_" + name] = (s * s) * _jax.random.uniform(kv, w.shape, _jnp.float32, 0.5, 1.5)
    if N_MICROBATCH > 1:
        for name, axis in PER_EXAMPLE_BATCH_AXIS.items():
            out[name] = _to_microbatches(out[name], axis)
    return {'x': out['x'], 'mem': out['mem'], 'w_in': out['w_in'], 'b_gate': out['b_gate'], 'ln_v_g': out['ln_v_g'], 'ln_v_b': out['ln_v_b'], 'w_s': out['w_s'], 'b_s': out['b_s'], 'sinks': out['sinks'], 'w_br_a': out['w_br_a'], 'w_br_b': out['w_br_b'], 'w_o': out['w_o'], 'ln1_g': out['ln1_g'], 'ln1_b': out['ln1_b'], 'w_xq': out['w_xq'], 'w_xkv': out['w_xkv'], 'w_xo': out['w_xo'], 'ln2_g': out['ln2_g'], 'ln2_b': out['ln2_b'], 'w_up': out['w_up'], 'w_down': out['w_down'], 'ln3_g': out['ln3_g'], 'ln3_b': out['ln3_b'], 'loss_target': out['loss_target'], 'm_w_in': out['m_w_in'], 'm_b_gate': out['m_b_gate'], 'm_ln_v_g': out['m_ln_v_g'], 'm_ln_v_b': out['m_ln_v_b'], 'm_w_s': out['m_w_s'], 'm_b_s': out['m_b_s'], 'm_sinks': out['m_sinks'], 'm_w_br_a': out['m_w_br_a'], 'm_w_br_b': out['m_w_br_b'], 'm_w_o': out['m_w_o'], 'm_ln1_g': out['m_ln1_g'], 'm_ln1_b': out['m_ln1_b'], 'm_w_xq': out['m_w_xq'], 'm_w_xkv': out['m_w_xkv'], 'm_w_xo': out['m_w_xo'], 'm_ln2_g': out['m_ln2_g'], 'm_ln2_b': out['m_ln2_b'], 'm_w_up': out['m_w_up'], 'm_w_down': out['m_w_down'], 'm_ln3_g': out['m_ln3_g'], 'm_ln3_b': out['m_ln3_b'], 'v_w_in': out['v_w_in'], 'v_b_gate': out['v_b_gate'], 'v_ln_v_g': out['v_ln_v_g'], 'v_ln_v_b': out['v_ln_v_b'], 'v_w_s': out['v_w_s'], 'v_b_s': out['v_b_s'], 'v_sinks': out['v_sinks'], 'v_w_br_a': out['v_w_br_a'], 'v_w_br_b': out['v_w_br_b'], 'v_w_o': out['v_w_o'], 'v_ln1_g': out['v_ln1_g'], 'v_ln1_b': out['v_ln1_b'], 'v_w_xq': out['v_w_xq'], 'v_w_xkv': out['v_w_xkv'], 'v_w_xo': out['v_w_xo'], 'v_ln2_g': out['v_ln2_g'], 'v_ln2_b': out['v_ln2_b'], 'v_w_up': out['v_w_up'], 'v_w_down': out['v_w_down'], 'v_ln3_g': out['v_ln3_g'], 'v_ln3_b': out['v_ln3_b']}


def _loss(weights, diff, rest, loss_target):
    with _jax.named_scope("forward"):
        args = {**rest, TWIN_DIFF_INPUT: diff, **{k: w.astype(_WEIGHT_DTYPES[k]) for k, w in weights.items()}}
        y = _forward(args)
    with _jax.named_scope("loss_head"):
        err = _jnp.square(y.astype(_jnp.float32) - loss_target)
        return 0.5 * _jnp.sum(_jnp.mean(err, axis=-1)) if err.ndim else 0.5 * err


def _adamw(w, g, m, v):
    m = ADAM_B1 * m + (1.0 - ADAM_B1) * g
    v = ADAM_B2 * v + (1.0 - ADAM_B2) * _jnp.square(g)
    m_hat = m / (1.0 - ADAM_B1 ** ADAM_STEP)
    v_hat = v / (1.0 - ADAM_B2 ** ADAM_STEP)
    delta = -ADAM_LR * (m_hat / (_jnp.sqrt(v_hat) + ADAM_EPS) + ADAM_WD * w)
    return delta, m, v


def reference(x, mem, w_in, b_gate, ln_v_g, ln_v_b, w_s, b_s, sinks, w_br_a, w_br_b, w_o, ln1_g, ln1_b, w_xq, w_xkv, w_xo, ln2_g, ln2_b, w_up, w_down, ln3_g, ln3_b, loss_target, m_w_in, m_b_gate, m_ln_v_g, m_ln_v_b, m_w_s, m_b_s, m_sinks, m_w_br_a, m_w_br_b, m_w_o, m_ln1_g, m_ln1_b, m_w_xq, m_w_xkv, m_w_xo, m_ln2_g, m_ln2_b, m_w_up, m_w_down, m_ln3_g, m_ln3_b, v_w_in, v_b_gate, v_ln_v_g, v_ln_v_b, v_w_s, v_b_s, v_sinks, v_w_br_a, v_w_br_b, v_w_o, v_ln1_g, v_ln1_b, v_w_xq, v_w_xkv, v_w_xo, v_ln2_g, v_ln2_b, v_w_up, v_w_down, v_ln3_g, v_ln3_b):
    given = dict(x=x, mem=mem, w_in=w_in, b_gate=b_gate, ln_v_g=ln_v_g, ln_v_b=ln_v_b, w_s=w_s, b_s=b_s, sinks=sinks, w_br_a=w_br_a, w_br_b=w_br_b, w_o=w_o, ln1_g=ln1_g, ln1_b=ln1_b, w_xq=w_xq, w_xkv=w_xkv, w_xo=w_xo, ln2_g=ln2_g, ln2_b=ln2_b, w_up=w_up, w_down=w_down, ln3_g=ln3_g, ln3_b=ln3_b, loss_target=loss_target, m_w_in=m_w_in, m_b_gate=m_b_gate, m_ln_v_g=m_ln_v_g, m_ln_v_b=m_ln_v_b, m_w_s=m_w_s, m_b_s=m_b_s, m_sinks=m_sinks, m_w_br_a=m_w_br_a, m_w_br_b=m_w_br_b, m_w_o=m_w_o, m_ln1_g=m_ln1_g, m_ln1_b=m_ln1_b, m_w_xq=m_w_xq, m_w_xkv=m_w_xkv, m_w_xo=m_w_xo, m_ln2_g=m_ln2_g, m_ln2_b=m_ln2_b, m_w_up=m_w_up, m_w_down=m_w_down, m_ln3_g=m_ln3_g, m_ln3_b=m_ln3_b, v_w_in=v_w_in, v_b_gate=v_b_gate, v_ln_v_g=v_ln_v_g, v_ln_v_b=v_ln_v_b, v_w_s=v_w_s, v_b_s=v_b_s, v_sinks=v_sinks, v_w_br_a=v_w_br_a, v_w_br_b=v_w_br_b, v_w_o=v_w_o, v_ln1_g=v_ln1_g, v_ln1_b=v_ln1_b, v_w_xq=v_w_xq, v_w_xkv=v_w_xkv, v_w_xo=v_w_xo, v_ln2_g=v_ln2_g, v_ln2_b=v_ln2_b, v_w_up=v_w_up, v_w_down=v_w_down, v_ln3_g=v_ln3_g, v_ln3_b=v_ln3_b)
    weights = {n: given[n] for n in TWIN_WEIGHTS}
    shared = {n: given[n] for n in SHARED_INPUTS}
    per_example = {n: given[n] for n in ['x', 'mem']}
    grad_fn = _jax.value_and_grad(_loss, argnums=(0, 1))

    def one_microbatch(ex, loss_target):
        ex = dict(ex)
        diff = ex.pop(TWIN_DIFF_INPUT)
        return grad_fn(weights, diff, {**shared, **ex}, loss_target)

    if N_MICROBATCH == 1:
        loss, (grad_w, grad_x) = one_microbatch(per_example, given["loss_target"])
    else:
        def body(carry, xs):
            loss_sum, grad_sum = carry
            l_k, (gw_k, gx_k) = one_microbatch(xs[0], xs[1])
            with _jax.named_scope("update"):
                return (loss_sum + l_k, _jax.tree.map(_jnp.add, grad_sum, gw_k)), gx_k

        init = (_jnp.zeros((), _jnp.float32), _jax.tree.map(_jnp.zeros_like, weights))
        (loss, grad_w), grad_x = _jax.lax.scan(body, init, (per_example, given["loss_target"]))
    with _jax.named_scope("update"):
        delta_w, new_m, new_v = {}, {}, {}
        for n in TWIN_WEIGHTS:
            delta_w[n], new_m[n], new_v[n] = _adamw(weights[n], grad_w[n], given["m_" + n], given["v_" + n])
    return (loss, grad_x, *[grad_w[n] for n in TWIN_WEIGHTS], *[delta_w[n] for n in TWIN_WEIGHTS],
            *[new_m[n] for n in TWIN_WEIGHTS], *[new_v[n] for n in TWIN_WEIGHTS])
```

```python
import functools
import math

import jax
import jax.numpy as jnp
from jax import lax
from jax.experimental import pallas as pl
from jax.experimental.pallas import tpu as pltpu

F32 = jnp.float32
BF16 = jnp.bfloat16
MESH = pl.DeviceIdType.MESH
ANY = pl.BlockSpec(memory_space=pl.ANY)
VMEM_SPEC = pl.BlockSpec(memory_space=pltpu.VMEM)

DEPTH = 2
CHUNK = 128
GMLP_W = 1024
GROUPS = 8
NQ, NKV, HD = 16, 4, 64
ATT_W = NQ * HD
KV_W = NKV * HD
XH, XHD = 4, 128
X_W = XH * XHD
LN_EPS = 1e-5
ALPHA = (2 * DEPTH) ** 0.25
OFF_Q = 2 * GMLP_W
OFF_K = OFF_Q + ATT_W
OFF_VA = OFF_K + KV_W
OFF_GA = OFF_VA + KV_W
NEG = -1e30

ADAM_LR, ADAM_B1, ADAM_B2, ADAM_EPS, ADAM_WD, ADAM_STEP = 0.001, 0.9, 0.999, 1e-08, 0.01, 10

V7X_VMEM_BYTES = 64 * 1024 * 1024
VMEM_LIMIT = V7X_VMEM_BYTES - 12 * 1024 * 1024
LANE = 128

BIG = ("w_in", "w_br_a", "w_br_b", "w_o", "w_xq", "w_xkv", "w_xo", "w_up", "w_down")
SHARD_AXIS = {"w_in": 1, "w_br_a": 1, "w_br_b": 1, "w_o": 0, "w_xq": 0, "w_xkv": 0, "w_xo": 1,
              "w_up": 1, "w_down": 0}
SMALL = ("b_gate", "ln_v_g", "ln_v_b", "w_s", "b_s", "sinks", "ln1_g", "ln1_b", "ln2_g", "ln2_b",
         "ln3_g", "ln3_b")
WEIGHTS = ("w_in", "b_gate", "ln_v_g", "ln_v_b", "w_s", "b_s", "sinks", "w_br_a", "w_br_b", "w_o",
           "ln1_g", "ln1_b", "w_xq", "w_xkv", "w_xo", "ln2_g", "ln2_b", "w_up", "w_down", "ln3_g", "ln3_b")


def _pallas(body, **kw):
    return pl.pallas_call(body, **kw)


def _params(**kw):
    return pltpu.CompilerParams(vmem_limit_bytes=VMEM_LIMIT, **kw)


def _tile(dim, pref, unit=LANE):
    best = None
    t = unit
    while t <= min(dim, pref):
        if dim % t == 0:
            best = t
        t += unit
    return best if best is not None else dim


def _dot(a, b, dims):
    return lax.dot_general(a, b, (dims, ((), ())), preferred_element_type=F32)


NN = ((1,), (0,))
NT = ((1,), (1,))
TN = ((0,), (0,))


def _bf(x):
    return x if x.dtype == BF16 else x.astype(BF16)


def _mm(name, a, b, *, dims, grid, a_spec, b_spec, out_shape, out_specs, epilogue,
        extras=(), extra_specs=(), acc_shape=None, aliases=None):
    nk = grid[2]
    n_ex, n_out = len(extras), len(out_shape)

    def body(*refs):
        a_ref, b_ref = refs[0], refs[1]
        ex = refs[2:2 + n_ex]
        outs = refs[2 + n_ex:2 + n_ex + n_out]
        part = _dot(_bf(a_ref[...]), _bf(b_ref[...]), dims)
        if nk == 1:
            epilogue(part, ex, outs)
        else:
            acc = refs[-1]
            k = pl.program_id(2)

            @pl.when(k == 0)
            def _():
                acc[...] = part

            @pl.when(k > 0)
            def _():
                acc[...] += part

            @pl.when(k == nk - 1)
            def _():
                epilogue(acc[...], ex, outs)

    scratch = [pltpu.VMEM(acc_shape, F32)] if nk > 1 else []
    return _pallas(
        body, name=name, grid=grid, in_specs=[a_spec, b_spec, *extra_specs], out_specs=list(out_specs),
        out_shape=list(out_shape), scratch_shapes=scratch, input_output_aliases=aliases or {},
        compiler_params=_params(dimension_semantics=("arbitrary",) * 3),
    )(a, b, *extras)


def _store(dtypes):
    def ep(acc, ex, outs):
        for o in outs:
            o[...] = acc.astype(o.dtype)
    return ep


def _ln_rows(r, g, b):
    mu = jnp.mean(r, axis=-1, keepdims=True)
    xc = r - mu
    var = jnp.mean(xc * xc, axis=-1, keepdims=True)
    rstd = lax.rsqrt(var + LN_EPS)
    xhat = xc * rstd
    return xhat * g + b, xhat, rstd


def _ep_residual_ln(acc, ex, outs):
    x_ref, g_ref, b_ref = ex
    r_ref, y_ref, yb_ref = outs
    r = ALPHA * x_ref[...] + acc
    y, _, _ = _ln_rows(r, g_ref[...], b_ref[...])
    r_ref[...] = r
    y_ref[...] = y
    yb_ref[...] = y.astype(BF16)


def _ep_add_scaled(acc, ex, outs):
    outs[0][...] = acc + ALPHA * ex[0][...]


def _row_spec(bm, width):
    return pl.BlockSpec((bm, width), lambda i, j, k: (i, 0))


def _vec_spec(width):
    return pl.BlockSpec((1, width), lambda i, j, k: (0, 0))


def _sds(shape, dtype):
    return jax.ShapeDtypeStruct(shape, dtype)


_GC = math.sqrt(2.0 / math.pi)


def _gelu(x):
    t = jnp.tanh(_GC * (x + 0.044715 * (x * x * x)))
    return 0.5 * x * (1.0 + t), t


def _gelu_grad(x, t):
    return 0.5 * (1.0 + t) + 0.5 * x * (1.0 - t * t) * (_GC * (1.0 + 3.0 * 0.044715 * x * x))


def _sigmoid(x):
    return 1.0 / (1.0 + jnp.exp(-x))


def _rope(x, cos, sin_signed):
    w = x.shape[-1]
    lane = lax.broadcasted_iota(jnp.int32, x.shape, 1)
    first = (lane % HD) < (HD // 2)
    partner = jnp.where(first, pltpu.roll(x, w - HD // 2, 1), pltpu.roll(x, HD // 2, 1))
    reps = w // LANE
    return x * jnp.tile(cos, (1, reps)) + partner * jnp.tile(sin_signed, (1, reps))


def _cast_bf16(name, w):
    _, r, c = w.shape
    br = _tile(r, 512, 8)

    def body(w_ref, o_ref):
        o_ref[...] = w_ref[...].astype(BF16)

    spec = pl.BlockSpec((None, br, c), lambda l, i: (l, i, 0))
    return _pallas(body, name=name, grid=(2, r // br), in_specs=[spec], out_specs=spec,
                   out_shape=_sds(w.shape, BF16), compiler_params=_params())(w)


def _cast2d(name, x):
    s, d = x.shape
    bm = _tile(s, 512, 8)

    def body(x_ref, o_ref):
        o_ref[...] = x_ref[...].astype(BF16)

    spec = pl.BlockSpec((bm, d), lambda i: (i, 0))
    return _pallas(body, name=name, grid=(s // bm,), in_specs=[spec], out_specs=spec,
                   out_shape=_sds(x.shape, BF16), compiler_params=_params())(x)


def _place():
    x, y, c = lax.axis_index("x"), lax.axis_index("y"), lax.axis_index("c")
    chips = [(1 - x, y), (x, 1 - y), (1 - x, 1 - y)]
    return x, y, c, chips


def _window(ref, layer, chip, axis, rows, cols):
    if axis == 0:
        return ref.at[layer, pl.ds(pl.multiple_of(chip * rows, 8), rows), :]
    return ref.at[layer, :, pl.ds(pl.multiple_of(chip * cols, LANE), cols)]


def _gather_weights(name, shards, axes):
    n = len(shards)
    fulls = []
    for s, ax in zip(shards, axes):
        _, r, c = s.shape
        fulls.append(_sds((2, 4 * r, c) if ax == 0 else (2, r, 4 * c), BF16))

    def body(*refs):
        ins, outs = refs[:n], refs[n:2 * n]
        loc, isend, irecv, dsend, drecv = refs[2 * n:]
        x, y, c, chips = _place()
        me = 2 * x + y
        sib = (x, y, 1 - c)
        started = []
        for w in range(n):
            _, r, cc = ins[w].shape
            for l in range(2):
                cp = pltpu.make_async_copy(ins[w].at[l], _window(outs[w], l, me, axes[w], r, cc), loc.at[2 * w + l])
                cp.start()
                started.append(cp)
        sends = []
        for w in range(n):
            _, r, cc = ins[w].shape
            for j, (px, py) in enumerate(chips):
                cp = pltpu.make_async_remote_copy(
                    src_ref=ins[w].at[c], dst_ref=_window(outs[w], c, me, axes[w], r, cc),
                    send_sem=isend.at[3 * w + j], recv_sem=irecv.at[3 * w + j],
                    device_id=(px, py, c), device_id_type=MESH)
                cp.start()
                sends.append(cp)
        for w in range(n):
            _, r, cc = ins[w].shape
            for j, (px, py) in enumerate(chips):
                src_chip = 2 * px + py
                landed = _window(outs[w], c, src_chip, axes[w], r, cc)
                pltpu.make_async_remote_copy(
                    src_ref=landed, dst_ref=landed, send_sem=isend.at[3 * w + j], recv_sem=irecv.at[3 * w + j],
                    device_id=(px, py, c), device_id_type=MESH).wait_recv()
                fw = pltpu.make_async_remote_copy(
                    src_ref=landed, dst_ref=landed, send_sem=dsend.at[3 * w + j], recv_sem=drecv.at[3 * w + j],
                    device_id=sib, device_id_type=MESH)
                fw.start()
                sends.append(fw)
        for w in range(n):
            _, r, cc = ins[w].shape
            for j, (px, py) in enumerate(chips):
                src_chip = 2 * px + py
                got = _window(outs[w], 1 - c, src_chip, axes[w], r, cc)
                pltpu.make_async_remote_copy(
                    src_ref=got, dst_ref=got, send_sem=dsend.at[3 * w + j], recv_sem=drecv.at[3 * w + j],
                    device_id=sib, device_id_type=MESH).wait_recv()
        for cp in sends:
            cp.wait_send()
        for cp in started:
            cp.wait()

    return _pallas(
        body, name=name, in_specs=[ANY] * n, out_specs=[ANY] * n, out_shape=fulls,
        scratch_shapes=[pltpu.SemaphoreType.DMA((2 * n,)), pltpu.SemaphoreType.DMA((3 * n,)),
                        pltpu.SemaphoreType.DMA((3 * n,)), pltpu.SemaphoreType.DMA((3 * n,)),
                        pltpu.SemaphoreType.DMA((3 * n,))],
        compiler_params=_params(has_side_effects=True),
    )(*shards)


def _swap_layers(name, fulls):
    n = len(fulls)

    def body(*refs):
        ins, outs = refs[:n], refs[n:2 * n]
        ssem, rsem = refs[2 * n:]
        x, y, c, _ = _place()
        sib = (x, y, 1 - c)
        cps = []
        for w in range(n):
            cp = pltpu.make_async_remote_copy(
                src_ref=ins[w].at[1 - c], dst_ref=outs[w], send_sem=ssem.at[w], recv_sem=rsem.at[w],
                device_id=sib, device_id_type=MESH)
            cp.start()
            cps.append(cp)
        for cp in cps:
            cp.wait()

    return _pallas(
        body, name=name, in_specs=[ANY] * n, out_specs=[ANY] * n,
        out_shape=[_sds(f.shape[1:], BF16) for f in fulls],
        scratch_shapes=[pltpu.SemaphoreType.DMA((n,)), pltpu.SemaphoreType.DMA((n,))],
        compiler_params=_params(has_side_effects=True),
    )(*fulls)


def _add_pair(name, full, got, cidx):
    _, k, n = full.shape
    bm = _tile(k, 512, 8)

    def body(c_ref, a_ref, b_ref, o_ref):
        o_ref[...] = (a_ref[...].astype(F32) + b_ref[...].astype(F32)).astype(BF16)

    return _pallas(
        body, name=name,
        grid_spec=pltpu.PrefetchScalarGridSpec(
            num_scalar_prefetch=1, grid=(k // bm,),
            in_specs=[pl.BlockSpec((None, bm, n), lambda i, c: (c[0], i, 0)),
                      pl.BlockSpec((bm, n), lambda i, c: (i, 0))],
            out_specs=pl.BlockSpec((bm, n), lambda i, c: (i, 0))),
        out_shape=_sds((k, n), BF16), compiler_params=_params(),
    )(cidx, full, got)


def _scatter_partials(name, sums, axes):
    n = len(sums)
    shapes = []
    for s, ax in zip(sums, axes):
        k, nn = s.shape
        shapes.append((4, k // 4, nn) if ax == 0 else (4, k, nn // 4))

    def body(*refs):
        ins, outs = refs[:n], refs[n:2 * n]
        loc, ssem, rsem = refs[2 * n:]
        x, y, c, chips = _place()
        me = 2 * x + y

        def win(w, chip):
            _, r, cc = outs[w].shape
            if axes[w] == 0:
                return ins[w].at[pl.ds(pl.multiple_of(chip * r, 8), r), :]
            return ins[w].at[:, pl.ds(pl.multiple_of(chip * cc, LANE), cc)]

        cps, locs = [], []
        for w in range(n):
            cp = pltpu.make_async_copy(win(w, me), outs[w].at[me], loc.at[w])
            cp.start()
            locs.append(cp)
            for j, (px, py) in enumerate(chips):
                cp = pltpu.make_async_remote_copy(
                    src_ref=win(w, 2 * px + py), dst_ref=outs[w].at[me],
                    send_sem=ssem.at[3 * w + j], recv_sem=rsem.at[3 * w + j],
                    device_id=(px, py, c), device_id_type=MESH)
                cp.start()
                cps.append(cp)
        for w in range(n):
            for j, (px, py) in enumerate(chips):
                slot = outs[w].at[2 * px + py]
                pltpu.make_async_remote_copy(
                    src_ref=slot, dst_ref=slot, send_sem=ssem.at[3 * w + j], recv_sem=rsem.at[3 * w + j],
                    device_id=(px, py, c), device_id_type=MESH).wait_recv()
        for cp in cps:
            cp.wait_send()
        for cp in locs:
            cp.wait()

    return _pallas(
        body, name=name, in_specs=[ANY] * n, out_specs=[ANY] * n,
        out_shape=[_sds(s, BF16) for s in shapes],
        scratch_shapes=[pltpu.SemaphoreType.DMA((n,)), pltpu.SemaphoreType.DMA((3 * n,)),
                        pltpu.SemaphoreType.DMA((3 * n,))],
        compiler_params=_params(has_side_effects=True),
    )(*sums)


def _sum_slots(name, slots, cidx):
    _, r, cc = slots.shape
    br = _tile(r, 256, 8)

    def body(c_ref, s_ref, o_ref):
        acc = s_ref[0].astype(F32)
        for i in range(1, 4):
            acc = acc + s_ref[i].astype(F32)
        o_ref[...] = acc

    return _pallas(
        body, name=name,
        grid_spec=pltpu.PrefetchScalarGridSpec(
            num_scalar_prefetch=1, grid=(r // br,),
            in_specs=[pl.BlockSpec((4, br, cc), lambda i, c: (0, i, 0))],
            out_specs=pl.BlockSpec((None, br, cc), lambda i, c: (c[0], i, 0))),
        out_shape=_sds((2, r, cc), F32), compiler_params=_params(),
    )(cidx, slots)


def _exchange_layers(name, grads):
    n = len(grads)

    def body(*refs):
        ins, outs = refs[:n], refs[n:2 * n]
        ssem, rsem = refs[2 * n:]
        x, y, c, _ = _place()
        sib = (x, y, 1 - c)
        cps = []
        for w in range(n):
            cp = pltpu.make_async_remote_copy(
                src_ref=outs[w].at[c], dst_ref=outs[w].at[c], send_sem=ssem.at[w], recv_sem=rsem.at[w],
                device_id=sib, device_id_type=MESH)
            cp.start()
            cps.append(cp)
        for w in range(n):
            got = outs[w].at[1 - c]
            pltpu.make_async_remote_copy(
                src_ref=got, dst_ref=got, send_sem=ssem.at[w], recv_sem=rsem.at[w],
                device_id=sib, device_id_type=MESH).wait_recv()
        for cp in cps:
            cp.wait_send()

    return _pallas(
        body, name=name, in_specs=[ANY] * n, out_specs=[ANY] * n,
        out_shape=[_sds(g.shape, F32) for g in grads],
        input_output_aliases={i: i for i in range(n)},
        scratch_shapes=[pltpu.SemaphoreType.DMA((n,)), pltpu.SemaphoreType.DMA((n,))],
        compiler_params=_params(has_side_effects=True),
    )(*grads)


def _allreduce_small(name, packed):
    rows, lanes = packed.shape

    def body(in_ref, out_ref, slots, loc, ssem, rsem):
        x, y, c, _ = _place()
        me = 4 * x + 2 * y + c
        mine = pltpu.make_async_copy(in_ref, slots.at[me], loc)
        mine.start()
        cps = []
        k = 0
        for dx in range(2):
            for dy in range(2):
                for dc in range(2):
                    if dx == 0 and dy == 0 and dc == 0:
                        continue
                    peer = (1 - x if dx else x, 1 - y if dy else y, 1 - c if dc else c)
                    cp = pltpu.make_async_remote_copy(
                        src_ref=in_ref, dst_ref=slots.at[me], send_sem=ssem.at[k], recv_sem=rsem.at[k],
                        device_id=peer, device_id_type=MESH)
                    cp.start()
                    cps.append((cp, peer))
                    k += 1
        for k, (cp, peer) in enumerate(cps):
            src = 4 * peer[0] + 2 * peer[1] + peer[2]
            slot = slots.at[src]
            pltpu.make_async_remote_copy(
                src_ref=slot, dst_ref=slot, send_sem=ssem.at[k], recv_sem=rsem.at[k],
                device_id=peer, device_id_type=MESH).wait_recv()
        for cp, _ in cps:
            cp.wait_send()
        mine.wait()
        acc = slots[0]
        for i in range(1, 8):
            acc = acc + slots[i]
        out_ref[...] = acc

    return _pallas(
        body, name=name, in_specs=[VMEM_SPEC], out_specs=VMEM_SPEC, out_shape=_sds(packed.shape, F32),
        scratch_shapes=[pltpu.VMEM((8, rows, lanes), F32), pltpu.SemaphoreType.DMA,
                        pltpu.SemaphoreType.DMA((7,)), pltpu.SemaphoreType.DMA((7,))],
        compiler_params=_params(has_side_effects=True),
    )(packed)


def _adamw_math(w, g, m, v):
    m2 = ADAM_B1 * m + (1.0 - ADAM_B1) * g
    v2 = ADAM_B2 * v + (1.0 - ADAM_B2) * (g * g)
    m_hat = m2 / (1.0 - ADAM_B1 ** ADAM_STEP)
    v_hat = v2 / (1.0 - ADAM_B2 ** ADAM_STEP)
    delta = -ADAM_LR * (m_hat / (jnp.sqrt(v_hat) + ADAM_EPS) + ADAM_WD * w)
    return delta, m2, v2


def _adamw(name, w, g, m, v):
    l, r, c = w.shape
    br = _tile(r, 256, 8)

    def body(w_ref, g_ref, m_ref, v_ref, go_ref, d_ref, mo_ref, vo_ref):
        gg = g_ref[...]
        delta, m2, v2 = _adamw_math(w_ref[...], gg, m_ref[...], v_ref[...])
        go_ref[...] = gg
        d_ref[...] = delta
        mo_ref[...] = m2
        vo_ref[...] = v2

    spec = pl.BlockSpec((None, br, c), lambda a, i: (a, i, 0))
    return _pallas(body, name=name, grid=(l, r // br), in_specs=[spec] * 4, out_specs=[spec] * 4,
                   out_shape=[_sds(w.shape, F32)] * 4, compiler_params=_params())(w, g, m, v)


def _gmlp_fwd(name, proj, ln_g, ln_b, w_s, b_st):
    s = proj.shape[0]

    def body(u_ref, v_ref, g_ref, b_ref, ws_ref, bst_ref, sg_ref):
        gu, _ = _gelu(u_ref[...])
        gv, _ = _gelu(v_ref[...])
        vn, _, _ = _ln_rows(gv, g_ref[...], b_ref[...])
        vn = vn.astype(BF16)
        row = lax.broadcasted_iota(jnp.int32, (CHUNK, CHUNK), 0)
        col = lax.broadcasted_iota(jnp.int32, (CHUNK, CHUNK), 1)
        tril = col <= row
        outs = []
        for g in range(GROUPS):
            sl = slice(g * LANE, (g + 1) * LANE)
            w = jnp.where(tril, ws_ref[g], 0.0).astype(BF16)
            mixed = _dot(w, vn[:, sl], NN) + bst_ref[:, g:g + 1]
            outs.append(gu[:, sl] * mixed)
        sg_ref[...] = jnp.concatenate(outs, axis=-1).astype(BF16)

    return _pallas(
        body, name=name, grid=(s // CHUNK,),
        in_specs=[pl.BlockSpec((CHUNK, GMLP_W), lambda n: (n, 0)), pl.BlockSpec((CHUNK, GMLP_W), lambda n: (n, 1)),
                  pl.BlockSpec((1, GMLP_W), lambda n: (0, 0)), pl.BlockSpec((1, GMLP_W), lambda n: (0, 0)),
                  pl.BlockSpec((GROUPS, CHUNK, CHUNK), lambda n: (0, 0, 0)),
                  pl.BlockSpec((CHUNK, GROUPS), lambda n: (0, 0))],
        out_specs=pl.BlockSpec((CHUNK, GMLP_W), lambda n: (n, 0)),
        out_shape=_sds((s, GMLP_W), BF16), compiler_params=_params(),
    )(proj, proj, ln_g, ln_b, w_s, b_st)


def _swa_fwd(name, proj, cos4, sin4, sinks):
    s = proj.shape[0]
    w = CHUNK
    scale = HD ** -0.5

    def body(q_ref, k_ref, v_ref, cos_ref, sin_ref, sink_ref, o_ref, qr_ref, kr_ref, lse_ref, kprev, vprev):
        n = pl.program_id(0)

        @pl.when(n == 0)
        def _():
            kprev[...] = jnp.zeros_like(kprev)
            vprev[...] = jnp.zeros_like(vprev)

        cos, sin = cos_ref[...], sin_ref[...]
        qr = _rope(q_ref[...], cos, sin).astype(BF16)
        kr = _rope(k_ref[...], cos, sin).astype(BF16)
        vb = v_ref[...].astype(BF16)
        kp, vp = kprev[...], vprev[...]
        row = lax.broadcasted_iota(jnp.int32, (w, w), 0)
        col = lax.broadcasted_iota(jnp.int32, (w, w), 1)
        m_cur = col <= row
        m_prev = jnp.logical_and(col > row, n > 0)
        outs, lses = [], []
        for hq in range(NQ):
            hk = hq // (NQ // NKV)
            qs, ks = slice(hq * HD, (hq + 1) * HD), slice(hk * HD, (hk + 1) * HD)
            qh = qr[:, qs]
            s_c = jnp.where(m_cur, _dot(qh, kr[:, ks], NT) * scale, NEG)
            s_p = jnp.where(m_prev, _dot(qh, kp[:, ks], NT) * scale, NEG)
            sink = sink_ref[:, hq:hq + 1]
            mx = jnp.maximum(jnp.maximum(jnp.max(s_c, axis=-1, keepdims=True),
                                         jnp.max(s_p, axis=-1, keepdims=True)), sink)
            p_c = jnp.exp(s_c - mx)
            p_p = jnp.exp(s_p - mx)
            den = jnp.sum(p_c, axis=-1, keepdims=True) + jnp.sum(p_p, axis=-1, keepdims=True) + jnp.exp(sink - mx)
            inv = 1.0 / den
            oh = _dot((p_c * inv).astype(BF16), vb[:, ks], NN) + _dot((p_p * inv).astype(BF16), vp[:, ks], NN)
            outs.append(oh)
            lses.append(mx + jnp.log(den))
        o_ref[...] = jnp.concatenate(outs, axis=-1).astype(BF16)
        lse_ref[...] = jnp.concatenate(lses, axis=-1)
        qr_ref[...] = qr
        kr_ref[...] = kr
        kprev[...] = kr
        vprev[...] = vb

    return _pallas(
        body, name=name, grid=(s // w,),
        in_specs=[pl.BlockSpec((w, ATT_W), lambda n: (n, OFF_Q // ATT_W)),
                  pl.BlockSpec((w, KV_W), lambda n: (n, OFF_K // KV_W)),
                  pl.BlockSpec((w, KV_W), lambda n: (n, OFF_VA // KV_W)),
                  pl.BlockSpec((w, LANE), lambda n: (n, 0)), pl.BlockSpec((w, LANE), lambda n: (n, 0)),
                  pl.BlockSpec((1, NQ), lambda n: (0, 0))],
        out_specs=[pl.BlockSpec((w, ATT_W), lambda n: (n, 0)), pl.BlockSpec((w, ATT_W), lambda n: (n, 0)),
                   pl.BlockSpec((w, KV_W), lambda n: (n, 0)), pl.BlockSpec((w, NQ), lambda n: (n, 0))],
        out_shape=[_sds((s, ATT_W), BF16), _sds((s, ATT_W), BF16), _sds((s, KV_W), BF16), _sds((s, NQ), F32)],
        scratch_shapes=[pltpu.VMEM((w, KV_W), BF16), pltpu.VMEM((w, KV_W), BF16)],
        compiler_params=_params(dimension_semantics=("arbitrary",)),
    )(proj, proj, proj, cos4, sin4, sinks)


def _gate_fwd(name, sg, attn, wa, wb, layer, proj, b_gate, d):
    s = sg.shape[0]
    bm, bn = _tile(s, 1024), _tile(d, 512)
    off_a, off_b = OFF_GA // bn, (OFF_GA + d) // bn

    def body(sg_ref, at_ref, wa_ref, wb_ref, ga_ref, gb_ref, ba_ref, bb_ref, m_ref, ya_ref, yb_ref):
        ya = _dot(sg_ref[...], wa_ref[...], NN)
        yb = _dot(at_ref[...], wb_ref[...], NN)
        sa = _sigmoid(ga_ref[...] + ba_ref[...])
        sb = _sigmoid(gb_ref[...] + bb_ref[...])
        m_ref[...] = (sa * ya + sb * yb).astype(BF16)
        ya_ref[...] = ya.astype(BF16)
        yb_ref[...] = yb.astype(BF16)

    tile = pl.BlockSpec((bm, bn), lambda i, j: (i, j))
    return _pallas(
        body, name=name, grid=(s // bm, d // bn),
        in_specs=[pl.BlockSpec((bm, GMLP_W), lambda i, j: (i, 0)), pl.BlockSpec((bm, ATT_W), lambda i, j: (i, 0)),
                  pl.BlockSpec((None, GMLP_W, bn), lambda i, j: (layer, 0, j)),
                  pl.BlockSpec((None, ATT_W, bn), lambda i, j: (layer, 0, j)),
                  pl.BlockSpec((bm, bn), lambda i, j: (i, off_a + j)),
                  pl.BlockSpec((bm, bn), lambda i, j: (i, off_b + j)),
                  pl.BlockSpec((1, bn), lambda i, j: (0, j)), pl.BlockSpec((1, bn), lambda i, j: (0, d // bn + j))],
        out_specs=[tile, tile, tile], out_shape=[_sds((s, d), BF16)] * 3,
        compiler_params=_params(),
    )(sg, attn, wa, wb, proj, proj, b_gate, b_gate)


def _xattn_fwd(name, xb, xf, wq, kv, wo, layer, ln_g, ln_b):
    s, d = xf.shape
    mem = kv.shape[0]
    bm = _tile(s, 512)
    scale = XHD ** -0.5

    def body(xb_ref, xf_ref, wq_ref, kv_ref, wo_ref, g_ref, b_ref, q_out, o_out, r_out, y_out, yb_out):
        qb = _dot(xb_ref[...], wq_ref[...], NN).astype(BF16)
        kvv = kv_ref[...]
        outs = []
        for h in range(XH):
            hs = slice(h * XHD, (h + 1) * XHD)
            vs = slice(X_W + h * XHD, X_W + (h + 1) * XHD)
            sc = _dot(qb[:, hs], kvv[:, hs], NT) * scale
            mx = jnp.max(sc, axis=-1, keepdims=True)
            p = jnp.exp(sc - mx)
            p = p / jnp.sum(p, axis=-1, keepdims=True)
            outs.append(_dot(p.astype(BF16), kvv[:, vs], NN))
        ob = jnp.concatenate(outs, axis=-1).astype(BF16)
        yv = _dot(ob, wo_ref[...], NN)
        r = ALPHA * xf_ref[...] + yv
        yn, _, _ = _ln_rows(r, g_ref[...], b_ref[...])
        q_out[...] = qb
        o_out[...] = ob
        r_out[...] = r
        y_out[...] = yn
        yb_out[...] = yn.astype(BF16)

    row = lambda wd: pl.BlockSpec((bm, wd), lambda i: (i, 0))
    return _pallas(
        body, name=name, grid=(s // bm,),
        in_specs=[row(d), row(d), pl.BlockSpec((None, d, X_W), lambda i: (layer, 0, 0)),
                  pl.BlockSpec((mem, 2 * X_W), lambda i: (0, 0)),
                  pl.BlockSpec((None, X_W, d), lambda i: (layer, 0, 0)),
                  pl.BlockSpec((1, d), lambda i: (0, 0)), pl.BlockSpec((1, d), lambda i: (0, 0))],
        out_specs=[row(X_W), row(X_W), row(d), row(d), row(d)],
        out_shape=[_sds((s, X_W), BF16), _sds((s, X_W), BF16), _sds((s, d), F32), _sds((s, d), F32),
                   _sds((s, d), BF16)],
        compiler_params=_params(),
    )(xb, xf, wq, kv, wo, ln_g, ln_b)


def _loss_grad(name, y, tgt):
    s, d = y.shape
    bm = _tile(s, 512)

    def body(y_ref, t_ref, dy_ref, loss_ref):
        i = pl.program_id(0)
        err = y_ref[...] - t_ref[...]
        dy_ref[...] = err * (1.0 / d)
        part = 0.5 * jnp.sum(jnp.sum(err * err, axis=-1, keepdims=True), axis=0, keepdims=True) * (1.0 / d)

        @pl.when(i == 0)
        def _():
            loss_ref[...] = part

        @pl.when(i > 0)
        def _():
            loss_ref[...] += part

    row = pl.BlockSpec((bm, d), lambda i: (i, 0))
    return _pallas(
        body, name=name, grid=(s // bm,), in_specs=[row, row],
        out_specs=[row, pl.BlockSpec((1, 1), lambda i: (0, 0))],
        out_shape=[_sds((s, d), F32), _sds((1, 1), F32)],
        compiler_params=_params(dimension_semantics=("arbitrary",)),
    )(y, tgt)


def _ln_bwd(name, dy, r, g):
    s, d = r.shape
    bm = _tile(s, 256)

    def body(dy_ref, r_ref, g_ref, dr_ref, drb_ref, dg_ref, db_ref):
        i = pl.program_id(0)
        dyv = dy_ref[...]
        _, xhat, rstd = _ln_rows(r_ref[...], g_ref[...], 0.0)
        dxh = dyv * g_ref[...]
        m1 = jnp.mean(dxh, axis=-1, keepdims=True)
        m2 = jnp.mean(dxh * xhat, axis=-1, keepdims=True)
        dr = rstd * (dxh - m1 - xhat * m2)
        dr_ref[...] = dr
        drb_ref[...] = dr.astype(BF16)
        dg = jnp.sum(dyv * xhat, axis=0, keepdims=True)
        db = jnp.sum(dyv, axis=0, keepdims=True)

        @pl.when(i == 0)
        def _():
            dg_ref[...] = dg
            db_ref[...] = db

        @pl.when(i > 0)
        def _():
            dg_ref[...] += dg
            db_ref[...] += db

    row = pl.BlockSpec((bm, d), lambda i: (i, 0))
    vec = pl.BlockSpec((1, d), lambda i: (0, 0))
    return _pallas(
        body, name=name, grid=(s // bm,), in_specs=[row, row, vec], out_specs=[row, row, vec, vec],
        out_shape=[_sds((s, d), F32), _sds((s, d), BF16), _sds((1, d), F32), _sds((1, d), F32)],
        compiler_params=_params(dimension_semantics=("arbitrary",)),
    )(dy, r, g)


def _xattn_bwd(name, dyb, drf, q, kv, wo, wq, layer):
    s, d = drf.shape
    mem = kv.shape[0]
    bm = _tile(s, 512)
    scale = XHD ** -0.5

    def body(dy_ref, dr_ref, q_ref, kv_ref, wo_ref, wq_ref, dx_out, dq_out, dkv_out):
        i = pl.program_id(0)
        dob = _dot(dy_ref[...], wo_ref[...], NT).astype(BF16)
        qb = q_ref[...]
        kvv = kv_ref[...]
        dqs, dks, dvs = [], [], []
        for h in range(XH):
            hs = slice(h * XHD, (h + 1) * XHD)
            vs = slice(X_W + h * XHD, X_W + (h + 1) * XHD)
            sc = _dot(qb[:, hs], kvv[:, hs], NT) * scale
            mx = jnp.max(sc, axis=-1, keepdims=True)
            p = jnp.exp(sc - mx)
            p = p / jnp.sum(p, axis=-1, keepdims=True)
            dp = _dot(dob[:, hs], kvv[:, vs], NT)
            dsum = jnp.sum(p * dp, axis=-1, keepdims=True)
            dsb = (p * (dp - dsum) * scale).astype(BF16)
            dqs.append(_dot(dsb, kvv[:, hs], NN))
            dks.append(_dot(dsb, qb[:, hs], TN))
            dvs.append(_dot(p.astype(BF16), dob[:, hs], TN))
        dqb = jnp.concatenate(dqs, axis=-1).astype(BF16)
        dq_out[...] = dqb
        dx_out[...] = _dot(dqb, wq_ref[...], NT) + ALPHA * dr_ref[...]
        dkv = jnp.concatenate(dks + dvs, axis=-1)

        @pl.when(i == 0)
        def _():
            dkv_out[...] = dkv

        @pl.when(i > 0)
        def _():
            dkv_out[...] += dkv

    row = lambda wd: pl.BlockSpec((bm, wd), lambda i: (i, 0))
    return _pallas(
        body, name=name, grid=(s // bm,),
        in_specs=[row(d), row(d), row(X_W), pl.BlockSpec((mem, 2 * X_W), lambda i: (0, 0)),
                  pl.BlockSpec((None, X_W, d), lambda i: (layer, 0, 0)),
                  pl.BlockSpec((None, d, X_W), lambda i: (layer, 0, 0))],
        out_specs=[row(d), row(X_W), pl.BlockSpec((mem, 2 * X_W), lambda i: (0, 0))],
        out_shape=[_sds((s, d), F32), _sds((s, X_W), BF16), _sds((mem, 2 * X_W), F32)],
        compiler_params=_params(dimension_semantics=("arbitrary",)),
    )(dyb, drf, q, kv, wo, wq)


def _gate_bwd(name, dr1b, w_o, layer, proj, ya, yb, b_gate, d):
    s = dr1b.shape[0]
    bm, bn = _tile(s, 1024), _tile(d, 512)
    off_a, off_b = OFF_GA // bn, (OFF_GA + d) // bn
    nj = d // bn

    def body(a_ref, w_ref, ga_ref, gb_ref, ya_ref, yb_ref, ba_ref, bb_ref, dya_ref, dyb_ref, dg_ref, dba_ref, dbb_ref):
        i = pl.program_id(1)
        dm = _dot(a_ref[...], w_ref[...], NT)
        sa = _sigmoid(ga_ref[...] + ba_ref[...])
        sb = _sigmoid(gb_ref[...] + bb_ref[...])
        dya_ref[...] = (dm * sa).astype(BF16)
        dyb_ref[...] = (dm * sb).astype(BF16)
        dga = dm * ya_ref[...].astype(F32) * (sa * (1.0 - sa))
        dgb = dm * yb_ref[...].astype(F32) * (sb * (1.0 - sb))
        dg_ref[0] = dga.astype(BF16)
        dg_ref[1] = dgb.astype(BF16)
        sa_sum = jnp.sum(dga, axis=0, keepdims=True)
        sb_sum = jnp.sum(dgb, axis=0, keepdims=True)

        @pl.when(i == 0)
        def _():
            dba_ref[...] = sa_sum
            dbb_ref[...] = sb_sum

        @pl.when(i > 0)
        def _():
            dba_ref[...] += sa_sum
            dbb_ref[...] += sb_sum

    tile = pl.BlockSpec((bm, bn), lambda j, i: (i, j))
    return _pallas(
        body, name=name, grid=(nj, s // bm),
        in_specs=[pl.BlockSpec((bm, d), lambda j, i: (i, 0)),
                  pl.BlockSpec((None, bn, d), lambda j, i: (layer, j, 0)),
                  pl.BlockSpec((bm, bn), lambda j, i: (i, off_a + j)),
                  pl.BlockSpec((bm, bn), lambda j, i: (i, off_b + j)),
                  tile, tile,
                  pl.BlockSpec((1, bn), lambda j, i: (0, j)), pl.BlockSpec((1, bn), lambda j, i: (0, nj + j))],
        out_specs=[tile, tile, pl.BlockSpec((2, bm, bn), lambda j, i: (0, i, j)),
                   pl.BlockSpec((1, bn), lambda j, i: (0, j)), pl.BlockSpec((1, bn), lambda j, i: (0, j))],
        out_shape=[_sds((s, d), BF16), _sds((s, d), BF16), _sds((2, s, d), BF16), _sds((1, d), F32),
                   _sds((1, d), F32)],
        compiler_params=_params(dimension_semantics=("arbitrary", "arbitrary")),
    )(dr1b, w_o, proj, proj, ya, yb, b_gate, b_gate)


def _gmlp_bwd(name, proj, dsg, ln_g, ln_b, w_s, b_st):
    s = proj.shape[0]

    def body(u_ref, v_ref, dsg_ref, g_ref, b_ref, ws_ref, bst_ref, duv_ref, dws_ref, dbst_ref, dlg_ref, dlb_ref):
        n = pl.program_id(0)
        u, v = u_ref[...], v_ref[...]
        gu, tu = _gelu(u)
        gv, tv = _gelu(v)
        gam = g_ref[...]
        vn, xhat, rstd = _ln_rows(gv, gam, b_ref[...])
        vnb = vn.astype(BF16)
        dsg = dsg_ref[...].astype(F32)
        row = lax.broadcasted_iota(jnp.int32, (CHUNK, CHUNK), 0)
        col = lax.broadcasted_iota(jnp.int32, (CHUNK, CHUNK), 1)
        tril = col <= row
        dgu, dvn, dws, dbs = [], [], [], []
        for g in range(GROUPS):
            sl = slice(g * LANE, (g + 1) * LANE)
            w = jnp.where(tril, ws_ref[g], 0.0).astype(BF16)
            mixed = _dot(w, vnb[:, sl], NN) + bst_ref[:, g:g + 1]
            dgu.append(dsg[:, sl] * mixed)
            dmx = dsg[:, sl] * gu[:, sl]
            dmxb = dmx.astype(BF16)
            dbs.append(jnp.sum(dmx, axis=-1, keepdims=True))
            dws.append(jnp.where(tril, _dot(dmxb, vnb[:, sl], NT), 0.0))
            dvn.append(_dot(w, dmxb, TN))
        dvn = jnp.concatenate(dvn, axis=-1)
        dgu = jnp.concatenate(dgu, axis=-1)
        dxh = dvn * gam
        m1 = jnp.mean(dxh, axis=-1, keepdims=True)
        m2 = jnp.mean(dxh * xhat, axis=-1, keepdims=True)
        dgv = rstd * (dxh - m1 - xhat * m2)
        du = dgu * _gelu_grad(u, tu)
        dv = dgv * _gelu_grad(v, tv)
        duv_ref[...] = jnp.concatenate([du, dv], axis=-1).astype(BF16)
        dlg = jnp.sum(dvn * xhat, axis=0, keepdims=True)
        dlb = jnp.sum(dvn, axis=0, keepdims=True)
        dbst = jnp.concatenate(dbs, axis=-1)

        @pl.when(n == 0)
        def _():
            for g in range(GROUPS):
                dws_ref[g] = dws[g]
            dbst_ref[...] = dbst
            dlg_ref[...] = dlg
            dlb_ref[...] = dlb

        @pl.when(n > 0)
        def _():
            for g in range(GROUPS):
                dws_ref[g] += dws[g]
            dbst_ref[...] += dbst
            dlg_ref[...] += dlg
            dlb_ref[...] += dlb

    vec = pl.BlockSpec((1, GMLP_W), lambda n: (0, 0))
    return _pallas(
        body, name=name, grid=(s // CHUNK,),
        in_specs=[pl.BlockSpec((CHUNK, GMLP_W), lambda n: (n, 0)), pl.BlockSpec((CHUNK, GMLP_W), lambda n: (n, 1)),
                  pl.BlockSpec((CHUNK, GMLP_W), lambda n: (n, 0)), vec, vec,
                  pl.BlockSpec((GROUPS, CHUNK, CHUNK), lambda n: (0, 0, 0)),
                  pl.BlockSpec((CHUNK, GROUPS), lambda n: (0, 0))],
        out_specs=[pl.BlockSpec((CHUNK, 2 * GMLP_W), lambda n: (n, 0)),
                   pl.BlockSpec((GROUPS, CHUNK, CHUNK), lambda n: (0, 0, 0)),
                   pl.BlockSpec((CHUNK, GROUPS), lambda n: (0, 0)), vec, vec],
        out_shape=[_sds((s, 2 * GMLP_W), BF16), _sds((GROUPS, CHUNK, CHUNK), F32), _sds((CHUNK, GROUPS), F32),
                   _sds((1, GMLP_W), F32), _sds((1, GMLP_W), F32)],
        compiler_params=_params(dimension_semantics=("arbitrary",)),
    )(proj, proj, dsg, ln_g, ln_b, w_s, b_st)


def _swa_bwd(name, qr, kr, proj, do, o, lse, sinks, cos4, nsin4):
    s = qr.shape[0]
    w = CHUNK
    nblk = s // w
    scale = HD ** -0.5
    grp = NQ // NKV

    def body(qj_ref, qn_ref, kj_ref, kp_ref, vj_ref, vp_ref, doj_ref, don_ref, oj_ref, on_ref, lj_ref, ln_ref,
             sink_ref, cos_ref, sin_ref, out_ref, dsink_ref):
        j = pl.program_id(0)
        qj, qn, kj, kp = qj_ref[...], qn_ref[...], kj_ref[...], kp_ref[...]
        vj, vp = vj_ref[...].astype(BF16), vp_ref[...].astype(BF16)
        doj, don = doj_ref[...], don_ref[...]
        lj, lnx = lj_ref[...], ln_ref[...]
        prod_j = doj.astype(F32) * oj_ref[...].astype(F32)
        prod_n = don.astype(F32) * on_ref[...].astype(F32)
        row = lax.broadcasted_iota(jnp.int32, (w, w), 0)
        col = lax.broadcasted_iota(jnp.int32, (w, w), 1)
        m_diag = col <= row
        m_next = jnp.logical_and(col > row, j + 1 < nblk)
        m_prev = jnp.logical_and(col > row, j > 0)
        dqs, dsk = [], []
        dks = [None] * NKV
        dvs = [None] * NKV

        def add(lst, idx, val):
            lst[idx] = val if lst[idx] is None else lst[idx] + val

        for hq in range(NQ):
            hk = hq // grp
            qs, ks = slice(hq * HD, (hq + 1) * HD), slice(hk * HD, (hk + 1) * HD)
            l_j, l_n = lj[:, hq:hq + 1], lnx[:, hq:hq + 1]
            d_j = jnp.sum(prod_j[:, qs], axis=-1, keepdims=True)
            d_n = jnp.sum(prod_n[:, qs], axis=-1, keepdims=True)
            p_a = jnp.where(m_diag, jnp.exp(_dot(qj[:, qs], kj[:, ks], NT) * scale - l_j), 0.0)
            ds_a = (p_a * (_dot(doj[:, qs], vj[:, ks], NT) - d_j) * scale).astype(BF16)
            dq = _dot(ds_a, kj[:, ks], NN)
            add(dks, hk, _dot(ds_a, qj[:, qs], TN))
            add(dvs, hk, _dot(p_a.astype(BF16), doj[:, qs], TN))
            p_b = jnp.where(m_next, jnp.exp(_dot(qn[:, qs], kj[:, ks], NT) * scale - l_n), 0.0)
            ds_b = (p_b * (_dot(don[:, qs], vj[:, ks], NT) - d_n) * scale).astype(BF16)
            add(dks, hk, _dot(ds_b, qn[:, qs], TN))
            add(dvs, hk, _dot(p_b.astype(BF16), don[:, qs], TN))
            p_c = jnp.where(m_prev, jnp.exp(_dot(qj[:, qs], kp[:, ks], NT) * scale - l_j), 0.0)
            ds_c = (p_c * (_dot(doj[:, qs], vp[:, ks], NT) - d_j) * scale).astype(BF16)
            dq = dq + _dot(ds_c, kp[:, ks], NN)
            dqs.append(dq)
            p_sink = jnp.exp(sink_ref[:, hq:hq + 1] - l_j)
            dsk.append(-jnp.sum(p_sink * d_j, axis=0, keepdims=True))
        cos, nsin = cos_ref[...], sin_ref[...]
        dq = _rope(jnp.concatenate(dqs, axis=-1), cos, nsin)
        dk = _rope(jnp.concatenate(dks, axis=-1), cos, nsin)
        dv = jnp.concatenate(dvs, axis=-1)
        out_ref[...] = jnp.concatenate([dq, dk, dv], axis=-1).astype(BF16)
        dsink = jnp.concatenate(dsk, axis=-1)

        @pl.when(j == 0)
        def _():
            dsink_ref[...] = dsink

        @pl.when(j > 0)
        def _():
            dsink_ref[...] += dsink

    nxt = lambda j: jnp.minimum(j + 1, nblk - 1)
    prv = lambda j: jnp.maximum(j - 1, 0)
    va = OFF_VA // KV_W
    return _pallas(
        body, name=name, grid=(nblk,),
        in_specs=[pl.BlockSpec((w, ATT_W), lambda j: (j, 0)), pl.BlockSpec((w, ATT_W), lambda j: (nxt(j), 0)),
                  pl.BlockSpec((w, KV_W), lambda j: (j, 0)), pl.BlockSpec((w, KV_W), lambda j: (prv(j), 0)),
                  pl.BlockSpec((w, KV_W), lambda j: (j, va)), pl.BlockSpec((w, KV_W), lambda j: (prv(j), va)),
                  pl.BlockSpec((w, ATT_W), lambda j: (j, 0)), pl.BlockSpec((w, ATT_W), lambda j: (nxt(j), 0)),
                  pl.BlockSpec((w, ATT_W), lambda j: (j, 0)), pl.BlockSpec((w, ATT_W), lambda j: (nxt(j), 0)),
                  pl.BlockSpec((w, NQ), lambda j: (j, 0)), pl.BlockSpec((w, NQ), lambda j: (nxt(j), 0)),
                  pl.BlockSpec((1, NQ), lambda j: (0, 0)),
                  pl.BlockSpec((w, LANE), lambda j: (j, 0)), pl.BlockSpec((w, LANE), lambda j: (j, 0))],
        out_specs=[pl.BlockSpec((w, ATT_W + 2 * KV_W), lambda j: (j, 0)), pl.BlockSpec((1, NQ), lambda j: (0, 0))],
        out_shape=[_sds((s, ATT_W + 2 * KV_W), BF16), _sds((1, NQ), F32)],
        compiler_params=_params(dimension_semantics=("arbitrary",)),
    )(qr, qr, kr, kr, proj, proj, do, do, o, o, lse, lse, sinks, cos4, nsin4)


def _mm_nn(name, a, w, layer, *, out_dtypes, epilogue=None, bm_pref=1024, bn_pref=1024):
    m, k = a.shape
    n = w.shape[-1]
    bm, bn = _tile(m, bm_pref), _tile(n, bn_pref)
    tile = pl.BlockSpec((bm, bn), lambda i, j, kk: (i, j))
    return _mm(name, a, w, dims=NN, grid=(m // bm, n // bn, 1),
               a_spec=pl.BlockSpec((bm, k), lambda i, j, kk: (i, 0)),
               b_spec=pl.BlockSpec((None, k, bn), lambda i, j, kk: (layer, 0, j)),
               out_shape=[_sds((m, n), dt) for dt in out_dtypes], out_specs=[tile] * len(out_dtypes),
               epilogue=epilogue or _store(out_dtypes))


def _mm_tn(name, a, b, *, out=None, layer=0, bm_pref=1024, bn_pref=1024):
    s, m = a.shape
    n = b.shape[-1]
    bm, bn = _tile(m, bm_pref), _tile(n, bn_pref)
    kw = {}
    extras, extra_specs = (), ()
    if out is not None:
        extras, extra_specs = (out,), (ANY,)
        kw["aliases"] = {2: 0}

    def ep(acc, ex, outs):
        outs[0][...] = acc.astype(BF16)

    return _mm(name, a, b, dims=TN, grid=(m // bm, n // bn, 1),
               a_spec=pl.BlockSpec((s, bm), lambda i, j, kk: (0, i)),
               b_spec=pl.BlockSpec((s, bn), lambda i, j, kk: (0, j)),
               extras=extras, extra_specs=extra_specs,
               out_shape=[_sds((2, m, n), BF16)],
               out_specs=[pl.BlockSpec((None, bm, bn), lambda i, j, kk: (layer, i, j))],
               epilogue=ep, **kw)[0]


def _layer_fwd(l, xf, xb, memb, full, small, tabs):
    s, d = xf.shape
    dff = full["w_up"].shape[-1]
    iw = full["w_in"].shape[-1]
    cos4, sin4, _ = tabs
    sv = {"xf": xf, "xb": xb}
    t = f"l{l}_"

    proj = _mm_nn(t + "proj", xb, full["w_in"], l, out_dtypes=[F32], bn_pref=1280)[0]
    sg = _gmlp_fwd(t + "gmlp_fwd", proj, small["ln_v_g"][l], small["ln_v_b"][l], small["w_s"][l], small["b_st"][l])
    attn, qr, kr, lse = _swa_fwd(t + "swa_fwd", proj, cos4, sin4, small["sinks"][l])
    merged, ya, yb = _gate_fwd(t + "gate_fwd", sg, attn, full["w_br_a"], full["w_br_b"], l, proj,
                               small["b_gate"][l], d)
    bm = _tile(s, 512)
    row = pl.BlockSpec((bm, d), lambda i, j, k: (i, 0))
    r1, x1, x1b = _mm(
        t + "o_ln", merged, full["w_o"], dims=NN, grid=(s // bm, 1, 1),
        a_spec=row, b_spec=pl.BlockSpec((None, d, d), lambda i, j, k: (l, 0, 0)),
        extras=(xf, small["ln1_g"][l], small["ln1_b"][l]), extra_specs=(row, _vec_spec(d), _vec_spec(d)),
        out_shape=[_sds((s, d), F32), _sds((s, d), F32), _sds((s, d), BF16)], out_specs=[row] * 3,
        epilogue=_ep_residual_ln)
    kv = _mm_nn(t + "xkv", memb, full["w_xkv"], l, out_dtypes=[BF16])[0]
    q, o, r2, x2, x2b = _xattn_fwd(t + "xattn_fwd", x1b, x1, full["w_xq"], kv, full["w_xo"], l,
                                   small["ln2_g"][l], small["ln2_b"][l])

    def ep_up(acc, ex, outs):
        outs[0][...] = acc.astype(BF16)
        rl = jnp.maximum(acc, 0.0)
        outs[1][...] = (rl * rl).astype(BF16)

    h, a = _mm_nn(t + "up", x2b, full["w_up"], l, out_dtypes=[BF16, BF16], epilogue=ep_up)
    bk = _tile(dff, 1024)
    nk = dff // bk
    r3, x3, x3b = _mm(
        t + "down_ln", a, full["w_down"], dims=NN, grid=(s // bm, 1, nk),
        a_spec=pl.BlockSpec((bm, bk), lambda i, j, k: (i, k)),
        b_spec=pl.BlockSpec((None, bk, d), lambda i, j, k: (l, k, 0)),
        extras=(x2, small["ln3_g"][l], small["ln3_b"][l]), extra_specs=(row, _vec_spec(d), _vec_spec(d)),
        out_shape=[_sds((s, d), F32), _sds((s, d), F32), _sds((s, d), BF16)], out_specs=[row] * 3,
        epilogue=_ep_residual_ln, acc_shape=(bm, d))
    sv.update(proj=proj, sg=sg, attn=attn, qr=qr, kr=kr, lse=lse, merged=merged, ya=ya, yb=yb, r1=r1, x1=x1,
              x1b=x1b, kv=kv, q=q, o=o, r2=r2, x2b=x2b, h=h, a=a, r3=r3)
    return x3, x3b, sv


def _layer_bwd(l, dx3, sv, memb, full, small, tabs, dw, first):
    s, d = dx3.shape
    dff = full["w_up"].shape[-1]
    iw = full["w_in"].shape[-1]
    cos4, _, nsin4 = tabs
    t = f"l{l}_"
    sg_out = {}

    def put(name, a, b):
        dw[name] = _mm_tn(t + "d" + name, a, b, out=None if first else dw[name], layer=l)

    dr3, dr3b, sg_out["ln3_g"], sg_out["ln3_b"] = _ln_bwd(t + "ln3_bwd", dx3, sv["r3"], small["ln3_g"][l])

    bm, bn = _tile(s, 1024), _tile(dff, 1024)

    def ep_dh(acc, ex, outs):
        outs[0][...] = (acc * (2.0 * jnp.maximum(ex[0][...].astype(F32), 0.0))).astype(BF16)

    tile = pl.BlockSpec((bm, bn), lambda i, j, k: (i, j))
    dh = _mm(t + "dh", dr3b, full["w_down"], dims=NT, grid=(s // bm, dff // bn, 1),
             a_spec=pl.BlockSpec((bm, d), lambda i, j, k: (i, 0)),
             b_spec=pl.BlockSpec((None, bn, d), lambda i, j, k: (l, j, 0)),
             extras=(sv["h"],), extra_specs=(tile,), out_shape=[_sds((s, dff), BF16)], out_specs=[tile],
             epilogue=ep_dh)[0]
    put("w_down", sv["a"], dr3b)
    put("w_up", sv["x2b"], dh)
    bm2 = _tile(s, 512)
    bk = _tile(dff, 1024)
    row2 = pl.BlockSpec((bm2, d), lambda i, j, k: (i, 0))
    dx2 = _mm(t + "dx2", dh, full["w_up"], dims=NT, grid=(s // bm2, 1, dff // bk),
              a_spec=pl.BlockSpec((bm2, bk), lambda i, j, k: (i, k)),
              b_spec=pl.BlockSpec((None, d, bk), lambda i, j, k: (l, 0, k)),
              extras=(dr3,), extra_specs=(row2,), out_shape=[_sds((s, d), F32)], out_specs=[row2],
              epilogue=_ep_add_scaled, acc_shape=(bm2, d))[0]

    dr2, dr2b, sg_out["ln2_g"], sg_out["ln2_b"] = _ln_bwd(t + "ln2_bwd", dx2, sv["r2"], small["ln2_g"][l])
    dx1, dq, dkv = _xattn_bwd(t + "xattn_bwd", dr2b, dr2, sv["q"], sv["kv"], full["w_xo"], full["w_xq"], l)
    put("w_xo", sv["o"], dr2b)
    put("w_xq", sv["x1b"], dq)
    put("w_xkv", memb, _cast2d(t + "dkv_cast", dkv))

    dr1, dr1b, sg_out["ln1_g"], sg_out["ln1_b"] = _ln_bwd(t + "ln1_bwd", dx1, sv["r1"], small["ln1_g"][l])
    dya, dyb, dgate, dba, dbb = _gate_bwd(t + "gate_bwd", dr1b, full["w_o"], l, sv["proj"], sv["ya"], sv["yb"],
                                          small["b_gate"][l], d)
    sg_out["b_gate"] = jnp.concatenate([dba, dbb], axis=-1)
    put("w_o", sv["merged"], dr1b)
    put("w_br_a", sv["sg"], dya)
    put("w_br_b", sv["attn"], dyb)

    def dbranch(name, dy, w):
        bk2 = _tile(d, 1024)
        return _mm(name, dy, w, dims=NT, grid=(s // bm, 1, d // bk2),
                   a_spec=pl.BlockSpec((bm, bk2), lambda i, j, k: (i, k)),
                   b_spec=pl.BlockSpec((None, w.shape[1], bk2), lambda i, j, k: (l, 0, k)),
                   out_shape=[_sds((s, w.shape[1]), BF16)],
                   out_specs=[pl.BlockSpec((bm, w.shape[1]), lambda i, j, k: (i, 0))],
                   epilogue=_store([BF16]), acc_shape=(bm, w.shape[1]))[0]

    dsg = dbranch(t + "dsg", dya, full["w_br_a"])
    dattn = dbranch(t + "dattn", dyb, full["w_br_b"])
    duv, sg_out["w_s"], dbst, dlg, dlb = _gmlp_bwd(t + "gmlp_bwd", sv["proj"], dsg, small["ln_v_g"][l],
                                                   small["ln_v_b"][l], small["w_s"][l], small["b_st"][l])
    sg_out["b_s"] = dbst.T
    sg_out["ln_v_g"], sg_out["ln_v_b"] = dlg, dlb
    dqkv, sg_out["sinks"] = _swa_bwd(t + "swa_bwd", sv["qr"], sv["kr"], sv["proj"], dattn, sv["attn"], sv["lse"],
                                     small["sinks"][l], cos4, nsin4)
    dproj = jnp.concatenate([duv, dqkv, dgate[0], dgate[1]], axis=-1)
    put("w_in", sv["xb"], dproj)
    bk3 = _tile(iw, 1280)
    dx0 = _mm(t + "dx0", dproj, full["w_in"], dims=NT, grid=(s // bm2, 1, iw // bk3),
              a_spec=pl.BlockSpec((bm2, bk3), lambda i, j, k: (i, k)),
              b_spec=pl.BlockSpec((None, d, bk3), lambda i, j, k: (l, 0, k)),
              extras=(dr1,), extra_specs=(row2,), out_shape=[_sds((s, d), F32)], out_specs=[row2],
              epilogue=_ep_add_scaled, acc_shape=(bm2, d))[0]
    return dx0, sg_out


def _pack(arrs):
    flat = jnp.concatenate([a.reshape(-1) for a in arrs])
    n = flat.shape[0]
    pad = (-n) % (8 * LANE)
    return jnp.pad(flat, (0, pad)).reshape(-1, LANE)


def _unpack(packed, shapes):
    flat = packed.reshape(-1)
    out, off = [], 0
    for sh in shapes:
        n = math.prod(sh)
        out.append(flat[off:off + n].reshape(sh))
        off += n
    return out


def kernel(x, mem, w_in, b_gate, ln_v_g, ln_v_b, w_s, b_s, sinks, w_br_a, w_br_b, w_o, ln1_g, ln1_b, w_xq, w_xkv, w_xo, ln2_g, ln2_b, w_up, w_down, ln3_g, ln3_b, loss_target, m_w_in, m_b_gate, m_ln_v_g, m_ln_v_b, m_w_s, m_b_s, m_sinks, m_w_br_a, m_w_br_b, m_w_o, m_ln1_g, m_ln1_b, m_w_xq, m_w_xkv, m_w_xo, m_ln2_g, m_ln2_b, m_w_up, m_w_down, m_ln3_g, m_ln3_b, v_w_in, v_b_gate, v_ln_v_g, v_ln_v_b, v_w_s, v_b_s, v_sinks, v_w_br_a, v_w_br_b, v_w_o, v_ln1_g, v_ln1_b, v_w_xq, v_w_xkv, v_w_xo, v_ln2_g, v_ln2_b, v_w_up, v_w_down, v_ln3_g, v_ln3_b):
    env = dict(locals())
    wts = {n: env[n] for n in WEIGHTS}
    mom_m = {n: env["m_" + n] for n in WEIGHTS}
    mom_v = {n: env["v_" + n] for n in WEIGHTS}
    s, d = x.shape[1], x.shape[2]
    xf = x.reshape(s, d)
    tgt = loss_target.reshape(s, d)
    memf = mem.reshape(mem.shape[1], d)

    inv = 1.0 / (10000.0 ** (jnp.arange(0, HD, 2, dtype=F32) / HD))
    ang = jnp.arange(s, dtype=F32)[:, None] * inv[None, :]
    cos, sin = jnp.cos(ang), jnp.sin(ang)
    cos4 = jnp.tile(cos, (1, 4))
    sin4 = jnp.concatenate([-sin, sin, -sin, sin], axis=-1)
    tabs = (cos4, sin4, -sin4)

    small = {}
    for n in SMALL:
        w = wts[n]
        if n == "w_s":
            small[n] = [w[l] for l in range(DEPTH)]
        elif n == "b_s":
            small["b_st"] = [w[l].T for l in range(DEPTH)]
        else:
            small[n] = [w[l][None, :] for l in range(DEPTH)]

    shards = [_cast_bf16("cast_" + n, wts[n]) for n in BIG]
    axes = [SHARD_AXIS[n] for n in BIG]
    fulls = _gather_weights("gather_weights", shards, axes)
    full = dict(zip(BIG, fulls))

    xb = _cast2d("cast_x", xf)
    memb = _cast2d("cast_mem", memf)
    saved = []
    hf, hb = xf, xb
    for l in range(DEPTH):
        hf, hb, sv = _layer_fwd(l, hf, hb, memb, full, small, tabs)
        saved.append(sv)
    dy, loss11 = _loss_grad("loss", hf, tgt)
    loss = lax.psum(loss11[0, 0], ("x", "y", "c"))

    dw = {}
    small_g = [None] * DEPTH
    g = dy
    for l in reversed(range(DEPTH)):
        g, small_g[l] = _layer_bwd(l, g, saved[l], memb, full, small, tabs, dw, first=(l == DEPTH - 1))
    grad_x = g.reshape(x.shape)

    c = lax.axis_index("c")
    cidx = jnp.reshape(c, (1,)).astype(jnp.int32)
    partials = [dw[n] for n in BIG]
    got = _swap_layers("rs_swap", partials)
    sums = [_add_pair("rs_add_" + n, p, gt, cidx) for n, p, gt in zip(BIG, partials, got)]
    slots = _scatter_partials("rs_scatter", sums, axes)
    mine = [_sum_slots("rs_sum_" + n, sl, cidx) for n, sl in zip(BIG, slots)]
    grads_big = dict(zip(BIG, _exchange_layers("rs_exchange", mine)))

    shapes = [wts[n].shape for n in SMALL]
    packed_g = _pack([jnp.stack([small_g[l][n].reshape(wts[n].shape[1:]) for l in range(DEPTH)]) for n in SMALL])
    packed_g = _allreduce_small("ar_small", packed_g)
    pw, pm, pv = (_pack([src[n] for n in SMALL]) for src in (wts, mom_m, mom_v))
    sg, sd, sm2, sv2 = _adamw("adamw_small", pw[None], packed_g[None], pm[None], pv[None])
    small_out = {k: dict(zip(SMALL, _unpack(a[0], shapes))) for k, a in
                 (("g", sg), ("d", sd), ("m", sm2), ("v", sv2))}
    big_out = {"g": {}, "d": {}, "m": {}, "v": {}}
    for n in BIG:
        go, dl, m2, v2 = _adamw("adamw_" + n, wts[n], grads_big[n], mom_m[n], mom_v[n])
        big_out["g"][n], big_out["d"][n], big_out["m"][n], big_out["v"][n] = go, dl, m2, v2

    def pick(kind, n):
        return big_out[kind][n] if n in big_out[kind] else small_out[kind][n]

    return (loss, grad_x, *[pick("g", n) for n in WEIGHTS], *[pick("d", n) for n in WEIGHTS],
            *[pick("m", n) for n in WEIGHTS], *[pick("v", n) for n in WEIGHTS])
```

```python
import math

import jax
import jax.numpy as jnp
from jax import lax
from jax.experimental import pallas as pl
from jax.experimental.pallas import tpu as pltpu

F32 = jnp.float32
BF16 = jnp.bfloat16
MESH = pl.DeviceIdType.MESH
ANY = pl.BlockSpec(memory_space=pl.ANY)
HBM = pl.BlockSpec(memory_space=pltpu.HBM)
SEM = pl.BlockSpec(memory_space=pltpu.SEMAPHORE)
VMEM_SPEC = pl.BlockSpec(memory_space=pltpu.VMEM)
EFFECT = pltpu.SideEffectType.DATAFLOW_SIDE_EFFECTING

DEPTH = 2
CHUNK = 128
GMLP_W = 1024
GROUPS = 8
NQ, NKV, HD = 16, 4, 64
ATT_W = NQ * HD
KV_W = NKV * HD
XH, XHD = 4, 128
X_W = XH * XHD
LN_EPS = 1e-5
ALPHA = (2 * DEPTH) ** 0.25
OFF_Q = 2 * GMLP_W
OFF_K = OFF_Q + ATT_W
OFF_VA = OFF_K + KV_W
OFF_GA = OFF_VA + KV_W
NEG = -1e30

ADAM_LR, ADAM_B1, ADAM_B2, ADAM_EPS, ADAM_WD, ADAM_STEP = 0.001, 0.9, 0.999, 1e-08, 0.01, 10

V7X_VMEM_BYTES = 64 * 1024 * 1024
VMEM_LIMIT = V7X_VMEM_BYTES - 12 * 1024 * 1024
LANE = 128

BIG = ("w_in", "w_br_a", "w_br_b", "w_o", "w_xq", "w_xkv", "w_xo", "w_up", "w_down")
SHARD_AXIS = {"w_in": 1, "w_br_a": 1, "w_br_b": 1, "w_o": 0, "w_xq": 0, "w_xkv": 0, "w_xo": 1,
              "w_up": 1, "w_down": 0}
GROUPS_FWD = (("w_in",), ("w_br_a", "w_br_b", "w_o", "w_xq", "w_xkv", "w_xo"), ("w_up", "w_down"))
SMALL = ("b_gate", "ln_v_g", "ln_v_b", "w_s", "b_s", "sinks", "ln1_g", "ln1_b", "ln2_g", "ln2_b",
         "ln3_g", "ln3_b")
WEIGHTS = ("w_in", "b_gate", "ln_v_g", "ln_v_b", "w_s", "b_s", "sinks", "w_br_a", "w_br_b", "w_o",
           "ln1_g", "ln1_b", "w_xq", "w_xkv", "w_xo", "ln2_g", "ln2_b", "w_up", "w_down", "ln3_g", "ln3_b")


def _pallas(body, after=(), **kw):
    n_after = len(after)
    if not n_after:
        return pl.pallas_call(body, **kw)
    n_in = len(kw["in_specs"])
    kw["in_specs"] = list(kw["in_specs"]) + [ANY] * n_after

    def tied(*refs):
        return body(*refs[:n_in], *refs[n_in + n_after:])

    call = pl.pallas_call(tied, **kw)
    return lambda *ops: call(*ops, *after)


def _params(**kw):
    return pltpu.CompilerParams(vmem_limit_bytes=VMEM_LIMIT, **kw)


def _tile(dim, pref, unit=LANE):
    best = None
    t = unit
    while t <= min(dim, pref):
        if dim % t == 0:
            best = t
        t += unit
    return best if best is not None else dim


def _dot(a, b, dims):
    return lax.dot_general(a, b, (dims, ((), ())), preferred_element_type=F32)


NN = ((1,), (0,))
NT = ((1,), (1,))
TN = ((0,), (0,))


def _bf(x):
    return x if x.dtype == BF16 else x.astype(BF16)


def _sds(shape, dtype):
    return jax.ShapeDtypeStruct(shape, dtype)


def _mm(name, a, b, *, dims, grid, a_spec, b_spec, out_shape, out_specs, epilogue,
        extras=(), extra_specs=(), acc_shape=None, after=()):
    nk = grid[2]
    n_ex, n_out = len(extras), len(out_shape)

    def body(*refs):
        a_ref, b_ref = refs[0], refs[1]
        ex = refs[2:2 + n_ex]
        outs = refs[2 + n_ex:2 + n_ex + n_out]
        part = _dot(_bf(a_ref[...]), _bf(b_ref[...]), dims)
        if nk == 1:
            epilogue(part, ex, outs)
        else:
            acc = refs[-1]
            k = pl.program_id(2)

            @pl.when(k == 0)
            def _():
                acc[...] = part

            @pl.when(k > 0)
            def _():
                acc[...] += part

            @pl.when(k == nk - 1)
            def _():
                epilogue(acc[...], ex, outs)

    scratch = [pltpu.VMEM(acc_shape, F32)] if nk > 1 else []
    return _pallas(
        body, after=after, name=name, grid=grid, in_specs=[a_spec, b_spec, *extra_specs], out_specs=list(out_specs),
        out_shape=list(out_shape), scratch_shapes=scratch,
        compiler_params=_params(dimension_semantics=("arbitrary",) * 3),
    )(a, b, *extras)


def _store(acc, ex, outs):
    for o in outs:
        o[...] = acc.astype(o.dtype)


def _ln_rows(r, g, b):
    mu = jnp.mean(r, axis=-1, keepdims=True)
    xc = r - mu
    var = jnp.mean(xc * xc, axis=-1, keepdims=True)
    rstd = lax.rsqrt(var + LN_EPS)
    xhat = xc * rstd
    return xhat * g + b, xhat, rstd


def _ep_residual_ln(acc, ex, outs):
    x_ref, g_ref, b_ref = ex
    r_ref, y_ref, yb_ref = outs
    r = ALPHA * x_ref[...] + acc
    y, _, _ = _ln_rows(r, g_ref[...], b_ref[...])
    r_ref[...] = r
    y_ref[...] = y
    yb_ref[...] = y.astype(BF16)


def _ep_add_scaled(acc, ex, outs):
    outs[0][...] = acc + ALPHA * ex[0][...]


def _vec_spec(width):
    return pl.BlockSpec((1, width), lambda i, j, k: (0, 0))


_GC = math.sqrt(2.0 / math.pi)


def _gelu(x):
    t = jnp.tanh(_GC * (x + 0.044715 * (x * x * x)))
    return 0.5 * x * (1.0 + t), t


def _gelu_grad(x, t):
    return 0.5 * (1.0 + t) + 0.5 * x * (1.0 - t * t) * (_GC * (1.0 + 3.0 * 0.044715 * x * x))


def _sigmoid(x):
    return 1.0 / (1.0 + jnp.exp(-x))


def _rope(x, cos, sin_signed):
    w = x.shape[-1]
    lane = lax.broadcasted_iota(jnp.int32, x.shape, 1)
    first = (lane % HD) < (HD // 2)
    partner = jnp.where(first, pltpu.roll(x, w - HD // 2, 1), pltpu.roll(x, HD // 2, 1))
    reps = w // LANE
    return x * jnp.tile(cos, (1, reps)) + partner * jnp.tile(sin_signed, (1, reps))


def _cast_bf16(name, w, after=()):
    _, r, c = w.shape
    br = _tile(r, 512, 8)

    def body(w_ref, o_ref):
        o_ref[...] = w_ref[...].astype(BF16)

    spec = pl.BlockSpec((None, br, c), lambda l, i: (l, i, 0))
    return _pallas(body, after=after, name=name, grid=(2, r // br), in_specs=[spec], out_specs=spec,
                   out_shape=_sds(w.shape, BF16), compiler_params=_params())(w)


def _cast2d(name, x):
    s, d = x.shape
    bm = _tile(s, 512, 8)

    def body(x_ref, o_ref):
        o_ref[...] = x_ref[...].astype(BF16)

    spec = pl.BlockSpec((bm, d), lambda i: (i, 0))
    return _pallas(body, name=name, grid=(s // bm,), in_specs=[spec], out_specs=spec,
                   out_shape=_sds(x.shape, BF16), compiler_params=_params())(x)


def _place():
    x, y, c = lax.axis_index("x"), lax.axis_index("y"), lax.axis_index("c")
    chips = [(1 - x, y), (x, 1 - y), (1 - x, 1 - y)]
    return x, y, c, chips


def _cut(ref, axis, chip=None, half=None, lead=()):
    k, n = ref.shape[-2], ref.shape[-1]
    rows, cols = slice(None), slice(None)
    if chip is not None:
        if axis == 0:
            rows = pl.ds(pl.multiple_of(chip * (k // 4), 8), k // 4)
        else:
            cols = pl.ds(pl.multiple_of(chip * (n // 4), LANE), n // 4)
    if half is not None:
        if axis == 0:
            cols = pl.ds(pl.multiple_of(half * (n // 2), LANE), n // 2)
        else:
            rows = pl.ds(pl.multiple_of(half * (k // 2), 8), k // 2)
    return ref.at[(*lead, rows, cols)]


def _split_start(name, srcs, lands, make, n_sem, after=()):
    ns, nl, na = len(srcs), len(lands), len(after)

    def body(*refs):
        src, land = refs[:ns], refs[ns:ns + nl]
        outs = refs[ns + nl + na:]
        for out_cp, _ in make(src, land, outs[0], outs[1]):
            out_cp.start()
        outs[-1][...] = jnp.zeros_like(outs[-1])

    res = pl.pallas_call(
        body, name=name, in_specs=[HBM] * (ns + nl) + [ANY] * na,
        out_specs=[SEM, SEM] + [HBM] * nl + [VMEM_SPEC],
        out_shape=[pltpu.SemaphoreType.DMA((n_sem,)), pltpu.SemaphoreType.DMA((n_sem,))]
        + [pltpu.HBM(a.shape, a.dtype) for a in lands] + [_sds((8, LANE), F32)],
        input_output_aliases={ns + i: 2 + i for i in range(nl)},
        compiler_params=pltpu.CompilerParams(has_side_effects=EFFECT),
    )(*[pltpu.with_memory_space_constraint(a, pltpu.HBM) for a in (*srcs, *lands)], *after)
    return res[0], res[1], list(res[2:2 + nl]), res[-1]


def _split_wait(name, srcs, lands, ssem, rsem, make, after=()):
    ns, nl, na = len(srcs), len(lands), len(after)

    def body(*refs):
        src, land = refs[:ns], refs[ns:ns + nl]
        s_ref, r_ref = refs[ns + nl], refs[ns + nl + 1]
        pairs = make(src, land, s_ref, r_ref)
        for _, in_cp in pairs:
            in_cp.wait_recv()
        for out_cp, _ in pairs:
            out_cp.wait_send()

    res = pl.pallas_call(
        body, name=name, in_specs=[HBM] * (ns + nl) + [SEM, SEM] + [ANY] * na,
        out_specs=[HBM] * nl, out_shape=[pltpu.HBM(a.shape, a.dtype) for a in lands],
        input_output_aliases={ns + i: i for i in range(nl)},
        compiler_params=pltpu.CompilerParams(has_side_effects=EFFECT),
    )(*srcs, *lands, ssem, rsem, *after)
    return list(res)


def _rcopy(src, dst, ssem, rsem, k, dev):
    return pltpu.make_async_remote_copy(src_ref=src, dst_ref=dst, send_sem=ssem.at[k], recv_sem=rsem.at[k],
                                        device_id=dev, device_id_type=MESH)


def _mk_gather_ici(layer, axes):
    def make(src, land, ssem, rsem):
        x, y, c, chips = _place()
        me = 2 * x + y
        pairs = []
        for w, ax in enumerate(axes):
            mine = _cut(src[w], ax, half=c, lead=(layer,))
            for j, (px, py) in enumerate(chips):
                dev = (px, py, c)
                out_cp = _rcopy(mine, _cut(land[w], ax, chip=me, half=c), ssem, rsem, 3 * w + j, dev)
                got = _cut(land[w], ax, chip=2 * px + py, half=c)
                pairs.append((out_cp, _rcopy(got, got, ssem, rsem, 3 * w + j, dev)))
        return pairs
    return make


def _mk_gather_d2d(axes):
    def make(src, land, ssem, rsem):
        x, y, c, chips = _place()
        sib = (x, y, 1 - c)
        pairs = []
        for w, ax in enumerate(axes):
            for j, (px, py) in enumerate(chips):
                have = _cut(land[w], ax, chip=2 * px + py, half=c)
                want = _cut(land[w], ax, chip=2 * px + py, half=1 - c)
                pairs.append((_rcopy(have, have, ssem, rsem, 3 * w + j, sib),
                              _rcopy(want, want, ssem, rsem, 3 * w + j, sib)))
        return pairs
    return make


def _mk_swap(axes):
    def make(src, land, ssem, rsem):
        x, y, c, _ = _place()
        sib = (x, y, 1 - c)
        pairs = []
        for w, ax in enumerate(axes):
            cp = _rcopy(_cut(src[w], ax, half=1 - c), land[w], ssem, rsem, w, sib)
            pairs.append((cp, cp))
        return pairs
    return make


def _mk_scatter(axes):
    def make(src, land, ssem, rsem):
        x, y, c, chips = _place()
        pairs = []
        for w, ax in enumerate(axes):
            for j, (px, py) in enumerate(chips):
                cp = _rcopy(_cut(src[w], ax, chip=2 * px + py), land[w].at[j], ssem, rsem, 3 * w + j, (px, py, c))
                pairs.append((cp, cp))
        return pairs
    return make


def _mk_exchange(axes):
    def make(src, land, ssem, rsem):
        x, y, c, _ = _place()
        sib = (x, y, 1 - c)
        pairs = []
        for w, ax in enumerate(axes):
            have = _cut(land[w], ax, half=c)
            want = _cut(land[w], ax, half=1 - c)
            pairs.append((_rcopy(have, have, ssem, rsem, w, sib), _rcopy(want, want, ssem, rsem, w, sib)))
        return pairs
    return make


def _place_own(name, shard, axis, meidx):
    _, r, c = shard.shape
    full = (4 * r, c) if axis == 0 else (r, 4 * c)
    br = _tile(r, 512, 8)
    nb = r // br
    if axis == 0:
        ospec = pl.BlockSpec((br, c), lambda i, me: (me[0] * nb + i, 0))
    else:
        ospec = pl.BlockSpec((br, c), lambda i, me: (i, me[0]))

    def body(me_ref, s_ref, o0_ref, o1_ref):
        o0_ref[...] = s_ref[0]
        o1_ref[...] = s_ref[1]

    return pl.pallas_call(
        body, name=name,
        grid_spec=pltpu.PrefetchScalarGridSpec(
            num_scalar_prefetch=1, grid=(nb,),
            in_specs=[pl.BlockSpec((2, br, c), lambda i, me: (0, i, 0))], out_specs=[ospec, ospec]),
        out_shape=[_sds(full, BF16)] * 2, compiler_params=_params(),
    )(meidx, shard)


def _add_half(name, part, got, axis, cidx):
    k, n = got.shape
    bm = _tile(k, 512, 8)
    nb = k // bm
    if axis == 0:
        pspec = pl.BlockSpec((bm, n), lambda i, c: (i, c[0]))
    else:
        pspec = pl.BlockSpec((bm, n), lambda i, c: (c[0] * nb + i, 0))

    def body(c_ref, a_ref, b_ref, o_ref):
        o_ref[...] = (a_ref[...].astype(F32) + b_ref[...].astype(F32)).astype(BF16)

    return pl.pallas_call(
        body, name=name,
        grid_spec=pltpu.PrefetchScalarGridSpec(
            num_scalar_prefetch=1, grid=(nb,), in_specs=[pspec, pl.BlockSpec((bm, n), lambda i, c: (i, 0))],
            out_specs=pl.BlockSpec((bm, n), lambda i, c: (i, 0))),
        out_shape=_sds((k, n), BF16), compiler_params=_params(),
    )(cidx, part, got)


def _sum_half(name, own, slots, axis, mc):
    _, r, cc = slots.shape
    br = _tile(r, 256, 8)
    nb = r // br
    if axis == 0:
        own_spec = pl.BlockSpec((br, cc), lambda i, mc: (mc[0] * nb + i, 0))
        out_spec = pl.BlockSpec((br, cc), lambda i, mc: (i, mc[1]))
        shape = (r, 2 * cc)
    else:
        own_spec = pl.BlockSpec((br, cc), lambda i, mc: (i, mc[0]))
        out_spec = pl.BlockSpec((br, cc), lambda i, mc: (mc[1] * nb + i, 0))
        shape = (2 * r, cc)

    def body(mc_ref, own_ref, s_ref, o_ref):
        acc = own_ref[...].astype(F32)
        for i in range(3):
            acc = acc + s_ref[i].astype(F32)
        o_ref[...] = acc

    return pl.pallas_call(
        body, name=name,
        grid_spec=pltpu.PrefetchScalarGridSpec(
            num_scalar_prefetch=1, grid=(nb,),
            in_specs=[own_spec, pl.BlockSpec((3, br, cc), lambda i, mc: (0, i, 0))], out_specs=out_spec),
        out_shape=_sds(shape, F32), compiler_params=_params(),
    )(mc, own, slots)


def _allreduce_small(name, packed, after=()):
    rows, lanes = packed.shape

    def body(in_ref, out_ref, slots, loc, ssem, rsem):
        x, y, c, _ = _place()
        me = 4 * x + 2 * y + c
        mine = pltpu.make_async_copy(in_ref, slots.at[me], loc)
        mine.start()
        cps = []
        k = 0
        for dx in range(2):
            for dy in range(2):
                for dc in range(2):
                    if dx == 0 and dy == 0 and dc == 0:
                        continue
                    peer = (1 - x if dx else x, 1 - y if dy else y, 1 - c if dc else c)
                    cp = _rcopy(in_ref, slots.at[me], ssem, rsem, k, peer)
                    cp.start()
                    cps.append((cp, peer))
                    k += 1
        for k, (cp, peer) in enumerate(cps):
            slot = slots.at[4 * peer[0] + 2 * peer[1] + peer[2]]
            _rcopy(slot, slot, ssem, rsem, k, peer).wait_recv()
        for cp, _ in cps:
            cp.wait_send()
        mine.wait()
        acc = slots[0]
        for i in range(1, 8):
            acc = acc + slots[i]
        out_ref[...] = acc

    return _pallas(
        body, after=after, name=name, in_specs=[VMEM_SPEC], out_specs=VMEM_SPEC, out_shape=_sds(packed.shape, F32),
        scratch_shapes=[pltpu.VMEM((8, rows, lanes), F32), pltpu.SemaphoreType.DMA,
                        pltpu.SemaphoreType.DMA((7,)), pltpu.SemaphoreType.DMA((7,))],
        compiler_params=_params(has_side_effects=True),
    )(packed)


def _adamw_math(w, g, m, v):
    m2 = ADAM_B1 * m + (1.0 - ADAM_B1) * g
    v2 = ADAM_B2 * v + (1.0 - ADAM_B2) * (g * g)
    m_hat = m2 / (1.0 - ADAM_B1 ** ADAM_STEP)
    v_hat = v2 / (1.0 - ADAM_B2 ** ADAM_STEP)
    delta = -ADAM_LR * (m_hat / (jnp.sqrt(v_hat) + ADAM_EPS) + ADAM_WD * w)
    return delta, m2, v2


def _adamw(name, w, g, m, v, layer, prev=None):
    _, r, c = w.shape
    br = _tile(r, 256, 8)
    n_prev = 0 if prev is None else 4

    def body(*refs):
        w_ref, g_ref, m_ref, v_ref = refs[:4]
        go_ref, d_ref, mo_ref, vo_ref = refs[4 + n_prev:]
        gg = g_ref[...]
        delta, m2, v2 = _adamw_math(w_ref[...], gg, m_ref[...], v_ref[...])
        go_ref[...] = gg
        d_ref[...] = delta
        mo_ref[...] = m2
        vo_ref[...] = v2

    spec = pl.BlockSpec((None, br, c), lambda i: (layer, i, 0))
    return pl.pallas_call(
        body, name=name, grid=(r // br,),
        in_specs=[spec, pl.BlockSpec((br, c), lambda i: (i, 0)), spec, spec] + [ANY] * n_prev,
        out_specs=[spec] * 4, out_shape=[_sds(w.shape, F32)] * 4,
        input_output_aliases={4 + i: i for i in range(n_prev)}, compiler_params=_params(),
    )(w, g, m, v, *(prev or ()))


def _gmlp_fwd(name, proj, ln_g, ln_b, w_s, b_st):
    s = proj.shape[0]

    def body(u_ref, v_ref, g_ref, b_ref, ws_ref, bst_ref, sg_ref):
        gu, _ = _gelu(u_ref[...])
        gv, _ = _gelu(v_ref[...])
        vn, _, _ = _ln_rows(gv, g_ref[...], b_ref[...])
        vn = vn.astype(BF16)
        row = lax.broadcasted_iota(jnp.int32, (CHUNK, CHUNK), 0)
        col = lax.broadcasted_iota(jnp.int32, (CHUNK, CHUNK), 1)
        tril = col <= row
        outs = []
        for g in range(GROUPS):
            sl = slice(g * LANE, (g + 1) * LANE)
            w = jnp.where(tril, ws_ref[g], 0.0).astype(BF16)
            mixed = _dot(w, vn[:, sl], NN) + bst_ref[:, g:g + 1]
            outs.append(gu[:, sl] * mixed)
        sg_ref[...] = jnp.concatenate(outs, axis=-1).astype(BF16)

    return _pallas(
        body, name=name, grid=(s // CHUNK,),
        in_specs=[pl.BlockSpec((CHUNK, GMLP_W), lambda n: (n, 0)), pl.BlockSpec((CHUNK, GMLP_W), lambda n: (n, 1)),
                  pl.BlockSpec((1, GMLP_W), lambda n: (0, 0)), pl.BlockSpec((1, GMLP_W), lambda n: (0, 0)),
                  pl.BlockSpec((GROUPS, CHUNK, CHUNK), lambda n: (0, 0, 0)),
                  pl.BlockSpec((CHUNK, GROUPS), lambda n: (0, 0))],
        out_specs=pl.BlockSpec((CHUNK, GMLP_W), lambda n: (n, 0)),
        out_shape=_sds((s, GMLP_W), BF16), compiler_params=_params(),
    )(proj, proj, ln_g, ln_b, w_s, b_st)


def _swa_fwd(name, proj, cos4, sin4, sinks, after=()):
    s = proj.shape[0]
    w = CHUNK
    scale = HD ** -0.5

    def body(q_ref, k_ref, v_ref, cos_ref, sin_ref, sink_ref, o_ref, qr_ref, kr_ref, lse_ref, kprev, vprev):
        n = pl.program_id(0)

        @pl.when(n == 0)
        def _():
            kprev[...] = jnp.zeros_like(kprev)
            vprev[...] = jnp.zeros_like(vprev)

        cos, sin = cos_ref[...], sin_ref[...]
        qr = _rope(q_ref[...], cos, sin).astype(BF16)
        kr = _rope(k_ref[...], cos, sin).astype(BF16)
        vb = v_ref[...].astype(BF16)
        kp, vp = kprev[...], vprev[...]
        row = lax.broadcasted_iota(jnp.int32, (w, w), 0)
        col = lax.broadcasted_iota(jnp.int32, (w, w), 1)
        m_cur = col <= row
        m_prev = jnp.logical_and(col > row, n > 0)
        outs, lses = [], []
        for hq in range(NQ):
            hk = hq // (NQ // NKV)
            qs, ks = slice(hq * HD, (hq + 1) * HD), slice(hk * HD, (hk + 1) * HD)
            qh = qr[:, qs]
            s_c = jnp.where(m_cur, _dot(qh, kr[:, ks], NT) * scale, NEG)
            s_p = jnp.where(m_prev, _dot(qh, kp[:, ks], NT) * scale, NEG)
            sink = sink_ref[:, hq:hq + 1]
            mx = jnp.maximum(jnp.maximum(jnp.max(s_c, axis=-1, keepdims=True),
                                         jnp.max(s_p, axis=-1, keepdims=True)), sink)
            p_c = jnp.exp(s_c - mx)
            p_p = jnp.exp(s_p - mx)
            den = jnp.sum(p_c, axis=-1, keepdims=True) + jnp.sum(p_p, axis=-1, keepdims=True) + jnp.exp(sink - mx)
            inv = 1.0 / den
            oh = _dot((p_c * inv).astype(BF16), vb[:, ks], NN) + _dot((p_p * inv).astype(BF16), vp[:, ks], NN)
            outs.append(oh)
            lses.append(mx + jnp.log(den))
        o_ref[...] = jnp.concatenate(outs, axis=-1).astype(BF16)
        lse_ref[...] = jnp.concatenate(lses, axis=-1)
        qr_ref[...] = qr
        kr_ref[...] = kr
        kprev[...] = kr
        vprev[...] = vb

    return _pallas(
        body, after=after, name=name, grid=(s // w,),
        in_specs=[pl.BlockSpec((w, ATT_W), lambda n: (n, OFF_Q // ATT_W)),
                  pl.BlockSpec((w, KV_W), lambda n: (n, OFF_K // KV_W)),
                  pl.BlockSpec((w, KV_W), lambda n: (n, OFF_VA // KV_W)),
                  pl.BlockSpec((w, LANE), lambda n: (n, 0)), pl.BlockSpec((w, LANE), lambda n: (n, 0)),
                  pl.BlockSpec((1, NQ), lambda n: (0, 0))],
        out_specs=[pl.BlockSpec((w, ATT_W), lambda n: (n, 0)), pl.BlockSpec((w, ATT_W), lambda n: (n, 0)),
                   pl.BlockSpec((w, KV_W), lambda n: (n, 0)), pl.BlockSpec((w, NQ), lambda n: (n, 0))],
        out_shape=[_sds((s, ATT_W), BF16), _sds((s, ATT_W), BF16), _sds((s, KV_W), BF16), _sds((s, NQ), F32)],
        scratch_shapes=[pltpu.VMEM((w, KV_W), BF16), pltpu.VMEM((w, KV_W), BF16)],
        compiler_params=_params(dimension_semantics=("arbitrary",)),
    )(proj, proj, proj, cos4, sin4, sinks)


def _gate_fwd(name, sg, attn, wa, wb, proj, b_gate, d):
    s = sg.shape[0]
    bm, bn = _tile(s, 1024), _tile(d, 512)
    off_a, off_b = OFF_GA // bn, (OFF_GA + d) // bn

    def body(sg_ref, at_ref, wa_ref, wb_ref, ga_ref, gb_ref, ba_ref, bb_ref, m_ref, ya_ref, yb_ref):
        ya = _dot(sg_ref[...], wa_ref[...], NN)
        yb = _dot(at_ref[...], wb_ref[...], NN)
        sa = _sigmoid(ga_ref[...] + ba_ref[...])
        sb = _sigmoid(gb_ref[...] + bb_ref[...])
        m_ref[...] = (sa * ya + sb * yb).astype(BF16)
        ya_ref[...] = ya.astype(BF16)
        yb_ref[...] = yb.astype(BF16)

    tile = pl.BlockSpec((bm, bn), lambda i, j: (i, j))
    return _pallas(
        body, name=name, grid=(s // bm, d // bn),
        in_specs=[pl.BlockSpec((bm, GMLP_W), lambda i, j: (i, 0)), pl.BlockSpec((bm, ATT_W), lambda i, j: (i, 0)),
                  pl.BlockSpec((GMLP_W, bn), lambda i, j: (0, j)), pl.BlockSpec((ATT_W, bn), lambda i, j: (0, j)),
                  pl.BlockSpec((bm, bn), lambda i, j: (i, off_a + j)),
                  pl.BlockSpec((bm, bn), lambda i, j: (i, off_b + j)),
                  pl.BlockSpec((1, bn), lambda i, j: (0, j)), pl.BlockSpec((1, bn), lambda i, j: (0, d // bn + j))],
        out_specs=[tile, tile, tile], out_shape=[_sds((s, d), BF16)] * 3,
        compiler_params=_params(),
    )(sg, attn, wa, wb, proj, proj, b_gate, b_gate)


def _xattn_fwd(name, xb, xf, wq, kv, wo, ln_g, ln_b, after=()):
    s, d = xf.shape
    mem = kv.shape[0]
    bm = _tile(s, 512)
    scale = XHD ** -0.5

    def body(xb_ref, xf_ref, wq_ref, kv_ref, wo_ref, g_ref, b_ref, q_out, o_out, r_out, y_out, yb_out):
        qb = _dot(xb_ref[...], wq_ref[...], NN).astype(BF16)
        kvv = kv_ref[...]
        outs = []
        for h in range(XH):
            hs = slice(h * XHD, (h + 1) * XHD)
            vs = slice(X_W + h * XHD, X_W + (h + 1) * XHD)
            sc = _dot(qb[:, hs], kvv[:, hs], NT) * scale
            mx = jnp.max(sc, axis=-1, keepdims=True)
            p = jnp.exp(sc - mx)
            p = p / jnp.sum(p, axis=-1, keepdims=True)
            outs.append(_dot(p.astype(BF16), kvv[:, vs], NN))
        ob = jnp.concatenate(outs, axis=-1).astype(BF16)
        yv = _dot(ob, wo_ref[...], NN)
        r = ALPHA * xf_ref[...] + yv
        yn, _, _ = _ln_rows(r, g_ref[...], b_ref[...])
        q_out[...] = qb
        o_out[...] = ob
        r_out[...] = r
        y_out[...] = yn
        yb_out[...] = yn.astype(BF16)

    row = lambda wd: pl.BlockSpec((bm, wd), lambda i: (i, 0))
    return _pallas(
        body, after=after, name=name, grid=(s // bm,),
        in_specs=[row(d), row(d), pl.BlockSpec((d, X_W), lambda i: (0, 0)),
                  pl.BlockSpec((mem, 2 * X_W), lambda i: (0, 0)), pl.BlockSpec((X_W, d), lambda i: (0, 0)),
                  pl.BlockSpec((1, d), lambda i: (0, 0)), pl.BlockSpec((1, d), lambda i: (0, 0))],
        out_specs=[row(X_W), row(X_W), row(d), row(d), row(d)],
        out_shape=[_sds((s, X_W), BF16), _sds((s, X_W), BF16), _sds((s, d), F32), _sds((s, d), F32),
                   _sds((s, d), BF16)],
        compiler_params=_params(),
    )(xb, xf, wq, kv, wo, ln_g, ln_b)


def _loss_grad(name, y, tgt):
    s, d = y.shape
    bm = _tile(s, 512)

    def body(y_ref, t_ref, dy_ref, loss_ref):
        i = pl.program_id(0)
        err = y_ref[...] - t_ref[...]
        dy_ref[...] = err * (1.0 / d)
        part = 0.5 * jnp.sum(jnp.sum(err * err, axis=-1, keepdims=True), axis=0, keepdims=True) * (1.0 / d)

        @pl.when(i == 0)
        def _():
            loss_ref[...] = part

        @pl.when(i > 0)
        def _():
            loss_ref[...] += part

    row = pl.BlockSpec((bm, d), lambda i: (i, 0))
    return _pallas(
        body, name=name, grid=(s // bm,), in_specs=[row, row],
        out_specs=[row, pl.BlockSpec((1, 1), lambda i: (0, 0))],
        out_shape=[_sds((s, d), F32), _sds((1, 1), F32)],
        compiler_params=_params(dimension_semantics=("arbitrary",)),
    )(y, tgt)


def _ln_bwd(name, dy, r, g, after=()):
    s, d = r.shape
    bm = _tile(s, 256)

    def body(dy_ref, r_ref, g_ref, dr_ref, drb_ref, dg_ref, db_ref):
        i = pl.program_id(0)
        dyv = dy_ref[...]
        _, xhat, rstd = _ln_rows(r_ref[...], g_ref[...], 0.0)
        dxh = dyv * g_ref[...]
        m1 = jnp.mean(dxh, axis=-1, keepdims=True)
        m2 = jnp.mean(dxh * xhat, axis=-1, keepdims=True)
        dr = rstd * (dxh - m1 - xhat * m2)
        dr_ref[...] = dr
        drb_ref[...] = dr.astype(BF16)
        dg = jnp.sum(dyv * xhat, axis=0, keepdims=True)
        db = jnp.sum(dyv, axis=0, keepdims=True)

        @pl.when(i == 0)
        def _():
            dg_ref[...] = dg
            db_ref[...] = db

        @pl.when(i > 0)
        def _():
            dg_ref[...] += dg
            db_ref[...] += db

    row = pl.BlockSpec((bm, d), lambda i: (i, 0))
    vec = pl.BlockSpec((1, d), lambda i: (0, 0))
    return _pallas(
        body, after=after, name=name, grid=(s // bm,), in_specs=[row, row, vec], out_specs=[row, row, vec, vec],
        out_shape=[_sds((s, d), F32), _sds((s, d), BF16), _sds((1, d), F32), _sds((1, d), F32)],
        compiler_params=_params(dimension_semantics=("arbitrary",)),
    )(dy, r, g)


def _xattn_bwd(name, dyb, drf, q, kv, wo, wq):
    s, d = drf.shape
    mem = kv.shape[0]
    bm = _tile(s, 512)
    scale = XHD ** -0.5

    def body(dy_ref, dr_ref, q_ref, kv_ref, wo_ref, wq_ref, dx_out, dq_out, dkv_out):
        i = pl.program_id(0)
        dob = _dot(dy_ref[...], wo_ref[...], NT).astype(BF16)
        qb = q_ref[...]
        kvv = kv_ref[...]
        dqs, dks, dvs = [], [], []
        for h in range(XH):
            hs = slice(h * XHD, (h + 1) * XHD)
            vs = slice(X_W + h * XHD, X_W + (h + 1) * XHD)
            sc = _dot(qb[:, hs], kvv[:, hs], NT) * scale
            mx = jnp.max(sc, axis=-1, keepdims=True)
            p = jnp.exp(sc - mx)
            p = p / jnp.sum(p, axis=-1, keepdims=True)
            dp = _dot(dob[:, hs], kvv[:, vs], NT)
            dsum = jnp.sum(p * dp, axis=-1, keepdims=True)
            dsb = (p * (dp - dsum) * scale).astype(BF16)
            dqs.append(_dot(dsb, kvv[:, hs], NN))
            dks.append(_dot(dsb, qb[:, hs], TN))
            dvs.append(_dot(p.astype(BF16), dob[:, hs], TN))
        dqb = jnp.concatenate(dqs, axis=-1).astype(BF16)
        dq_out[...] = dqb
        dx_out[...] = _dot(dqb, wq_ref[...], NT) + ALPHA * dr_ref[...]
        dkv = jnp.concatenate(dks + dvs, axis=-1)

        @pl.when(i == 0)
        def _():
            dkv_out[...] = dkv

        @pl.when(i > 0)
        def _():
            dkv_out[...] += dkv

    row = lambda wd: pl.BlockSpec((bm, wd), lambda i: (i, 0))
    return _pallas(
        body, name=name, grid=(s // bm,),
        in_specs=[row(d), row(d), row(X_W), pl.BlockSpec((mem, 2 * X_W), lambda i: (0, 0)),
                  pl.BlockSpec((X_W, d), lambda i: (0, 0)), pl.BlockSpec((d, X_W), lambda i: (0, 0))],
        out_specs=[row(d), row(X_W), pl.BlockSpec((mem, 2 * X_W), lambda i: (0, 0))],
        out_shape=[_sds((s, d), F32), _sds((s, X_W), BF16), _sds((mem, 2 * X_W), F32)],
        compiler_params=_params(dimension_semantics=("arbitrary",)),
    )(dyb, drf, q, kv, wo, wq)


def _gate_bwd(name, dr1b, w_o, proj, ya, yb, b_gate, d, after=()):
    s = dr1b.shape[0]
    bm, bn = _tile(s, 1024), _tile(d, 512)
    off_a, off_b = OFF_GA // bn, (OFF_GA + d) // bn
    nj = d // bn

    def body(a_ref, w_ref, ga_ref, gb_ref, ya_ref, yb_ref, ba_ref, bb_ref, dya_ref, dyb_ref, dg_ref, dba_ref, dbb_ref):
        i = pl.program_id(1)
        dm = _dot(a_ref[...], w_ref[...], NT)
        sa = _sigmoid(ga_ref[...] + ba_ref[...])
        sb = _sigmoid(gb_ref[...] + bb_ref[...])
        dya_ref[...] = (dm * sa).astype(BF16)
        dyb_ref[...] = (dm * sb).astype(BF16)
        dga = dm * ya_ref[...].astype(F32) * (sa * (1.0 - sa))
        dgb = dm * yb_ref[...].astype(F32) * (sb * (1.0 - sb))
        dg_ref[0] = dga.astype(BF16)
        dg_ref[1] = dgb.astype(BF16)
        sa_sum = jnp.sum(dga, axis=0, keepdims=True)
        sb_sum = jnp.sum(dgb, axis=0, keepdims=True)

        @pl.when(i == 0)
        def _():
            dba_ref[...] = sa_sum
            dbb_ref[...] = sb_sum

        @pl.when(i > 0)
        def _():
            dba_ref[...] += sa_sum
            dbb_ref[...] += sb_sum

    tile = pl.BlockSpec((bm, bn), lambda j, i: (i, j))
    return _pallas(
        body, after=after, name=name, grid=(nj, s // bm),
        in_specs=[pl.BlockSpec((bm, d), lambda j, i: (i, 0)),
                  pl.BlockSpec((bn, d), lambda j, i: (j, 0)),
                  pl.BlockSpec((bm, bn), lambda j, i: (i, off_a + j)),
                  pl.BlockSpec((bm, bn), lambda j, i: (i, off_b + j)),
                  tile, tile,
                  pl.BlockSpec((1, bn), lambda j, i: (0, j)), pl.BlockSpec((1, bn), lambda j, i: (0, nj + j))],
        out_specs=[tile, tile, pl.BlockSpec((2, bm, bn), lambda j, i: (0, i, j)),
                   pl.BlockSpec((1, bn), lambda j, i: (0, j)), pl.BlockSpec((1, bn), lambda j, i: (0, j))],
        out_shape=[_sds((s, d), BF16), _sds((s, d), BF16), _sds((2, s, d), BF16), _sds((1, d), F32),
                   _sds((1, d), F32)],
        compiler_params=_params(dimension_semantics=("arbitrary", "arbitrary")),
    )(dr1b, w_o, proj, proj, ya, yb, b_gate, b_gate)


def _gmlp_bwd(name, proj, dsg, ln_g, ln_b, w_s, b_st):
    s = proj.shape[0]

    def body(u_ref, v_ref, dsg_ref, g_ref, b_ref, ws_ref, bst_ref, duv_ref, dws_ref, dbst_ref, dlg_ref, dlb_ref):
        n = pl.program_id(0)
        u, v = u_ref[...], v_ref[...]
        gu, tu = _gelu(u)
        gv, tv = _gelu(v)
        gam = g_ref[...]
        vn, xhat, rstd = _ln_rows(gv, gam, b_ref[...])
        vnb = vn.astype(BF16)
        dsg = dsg_ref[...].astype(F32)
        row = lax.broadcasted_iota(jnp.int32, (CHUNK, CHUNK), 0)
        col = lax.broadcasted_iota(jnp.int32, (CHUNK, CHUNK), 1)
        tril = col <= row
        dgu, dvn, dws, dbs = [], [], [], []
        for g in range(GROUPS):
            sl = slice(g * LANE, (g + 1) * LANE)
            w = jnp.where(tril, ws_ref[g], 0.0).astype(BF16)
            mixed = _dot(w, vnb[:, sl], NN) + bst_ref[:, g:g + 1]
            dgu.append(dsg[:, sl] * mixed)
            dmx = dsg[:, sl] * gu[:, sl]
            dmxb = dmx.astype(BF16)
            dbs.append(jnp.sum(dmx, axis=-1, keepdims=True))
            dws.append(jnp.where(tril, _dot(dmxb, vnb[:, sl], NT), 0.0))
            dvn.append(_dot(w, dmxb, TN))
        dvn = jnp.concatenate(dvn, axis=-1)
        dgu = jnp.concatenate(dgu, axis=-1)
        dxh = dvn * gam
        m1 = jnp.mean(dxh, axis=-1, keepdims=True)
        m2 = jnp.mean(dxh * xhat, axis=-1, keepdims=True)
        dgv = rstd * (dxh - m1 - xhat * m2)
        du = dgu * _gelu_grad(u, tu)
        dv = dgv * _gelu_grad(v, tv)
        duv_ref[...] = jnp.concatenate([du, dv], axis=-1).astype(BF16)
        dlg = jnp.sum(dvn * xhat, axis=0, keepdims=True)
        dlb = jnp.sum(dvn, axis=0, keepdims=True)
        dbst = jnp.concatenate(dbs, axis=-1)

        @pl.when(n == 0)
        def _():
            for g in range(GROUPS):
                dws_ref[g] = dws[g]
            dbst_ref[...] = dbst
            dlg_ref[...] = dlg
            dlb_ref[...] = dlb

        @pl.when(n > 0)
        def _():
            for g in range(GROUPS):
                dws_ref[g] += dws[g]
            dbst_ref[...] += dbst
            dlg_ref[...] += dlg
            dlb_ref[...] += dlb

    vec = pl.BlockSpec((1, GMLP_W), lambda n: (0, 0))
    return _pallas(
        body, name=name, grid=(s // CHUNK,),
        in_specs=[pl.BlockSpec((CHUNK, GMLP_W), lambda n: (n, 0)), pl.BlockSpec((CHUNK, GMLP_W), lambda n: (n, 1)),
                  pl.BlockSpec((CHUNK, GMLP_W), lambda n: (n, 0)), vec, vec,
                  pl.BlockSpec((GROUPS, CHUNK, CHUNK), lambda n: (0, 0, 0)),
                  pl.BlockSpec((CHUNK, GROUPS), lambda n: (0, 0))],
        out_specs=[pl.BlockSpec((CHUNK, 2 * GMLP_W), lambda n: (n, 0)),
                   pl.BlockSpec((GROUPS, CHUNK, CHUNK), lambda n: (0, 0, 0)),
                   pl.BlockSpec((CHUNK, GROUPS), lambda n: (0, 0)), vec, vec],
        out_shape=[_sds((s, 2 * GMLP_W), BF16), _sds((GROUPS, CHUNK, CHUNK), F32), _sds((CHUNK, GROUPS), F32),
                   _sds((1, GMLP_W), F32), _sds((1, GMLP_W), F32)],
        compiler_params=_params(dimension_semantics=("arbitrary",)),
    )(proj, proj, dsg, ln_g, ln_b, w_s, b_st)


def _swa_bwd(name, qr, kr, proj, do, o, lse, sinks, cos4, nsin4, after=()):
    s = qr.shape[0]
    w = CHUNK
    nblk = s // w
    scale = HD ** -0.5
    grp = NQ // NKV

    def body(qj_ref, qn_ref, kj_ref, kp_ref, vj_ref, vp_ref, doj_ref, don_ref, oj_ref, on_ref, lj_ref, ln_ref,
             sink_ref, cos_ref, sin_ref, out_ref, dsink_ref):
        j = pl.program_id(0)
        qj, qn, kj, kp = qj_ref[...], qn_ref[...], kj_ref[...], kp_ref[...]
        vj, vp = vj_ref[...].astype(BF16), vp_ref[...].astype(BF16)
        doj, don = doj_ref[...], don_ref[...]
        lj, lnx = lj_ref[...], ln_ref[...]
        prod_j = doj.astype(F32) * oj_ref[...].astype(F32)
        prod_n = don.astype(F32) * on_ref[...].astype(F32)
        row = lax.broadcasted_iota(jnp.int32, (w, w), 0)
        col = lax.broadcasted_iota(jnp.int32, (w, w), 1)
        m_diag = col <= row
        m_next = jnp.logical_and(col > row, j + 1 < nblk)
        m_prev = jnp.logical_and(col > row, j > 0)
        dqs, dsk = [], []
        dks = [None] * NKV
        dvs = [None] * NKV

        def add(lst, idx, val):
            lst[idx] = val if lst[idx] is None else lst[idx] + val

        for hq in range(NQ):
            hk = hq // grp
            qs, ks = slice(hq * HD, (hq + 1) * HD), slice(hk * HD, (hk + 1) * HD)
            l_j, l_n = lj[:, hq:hq + 1], lnx[:, hq:hq + 1]
            d_j = jnp.sum(prod_j[:, qs], axis=-1, keepdims=True)
            d_n = jnp.sum(prod_n[:, qs], axis=-1, keepdims=True)
            p_a = jnp.where(m_diag, jnp.exp(_dot(qj[:, qs], kj[:, ks], NT) * scale - l_j), 0.0)
            ds_a = (p_a * (_dot(doj[:, qs], vj[:, ks], NT) - d_j) * scale).astype(BF16)
            dq = _dot(ds_a, kj[:, ks], NN)
            add(dks, hk, _dot(ds_a, qj[:, qs], TN))
            add(dvs, hk, _dot(p_a.astype(BF16), doj[:, qs], TN))
            p_b = jnp.where(m_next, jnp.exp(_dot(qn[:, qs], kj[:, ks], NT) * scale - l_n), 0.0)
            ds_b = (p_b * (_dot(don[:, qs], vj[:, ks], NT) - d_n) * scale).astype(BF16)
            add(dks, hk, _dot(ds_b, qn[:, qs], TN))
            add(dvs, hk, _dot(p_b.astype(BF16), don[:, qs], TN))
            p_c = jnp.where(m_prev, jnp.exp(_dot(qj[:, qs], kp[:, ks], NT) * scale - l_j), 0.0)
            ds_c = (p_c * (_dot(doj[:, qs], vp[:, ks], NT) - d_j) * scale).astype(BF16)
            dq = dq + _dot(ds_c, kp[:, ks], NN)
            dqs.append(dq)
            p_sink = jnp.exp(sink_ref[:, hq:hq + 1] - l_j)
            dsk.append(-jnp.sum(p_sink * d_j, axis=0, keepdims=True))
        cos, nsin = cos_ref[...], sin_ref[...]
        dq = _rope(jnp.concatenate(dqs, axis=-1), cos, nsin)
        dk = _rope(jnp.concatenate(dks, axis=-1), cos, nsin)
        dv = jnp.concatenate(dvs, axis=-1)
        out_ref[...] = jnp.concatenate([dq, dk, dv], axis=-1).astype(BF16)
        dsink = jnp.concatenate(dsk, axis=-1)

        @pl.when(j == 0)
        def _():
            dsink_ref[...] = dsink

        @pl.when(j > 0)
        def _():
            dsink_ref[...] += dsink

    nxt = lambda j: jnp.minimum(j + 1, nblk - 1)
    prv = lambda j: jnp.maximum(j - 1, 0)
    va = OFF_VA // KV_W
    return _pallas(
        body, after=after, name=name, grid=(nblk,),
        in_specs=[pl.BlockSpec((w, ATT_W), lambda j: (j, 0)), pl.BlockSpec((w, ATT_W), lambda j: (nxt(j), 0)),
                  pl.BlockSpec((w, KV_W), lambda j: (j, 0)), pl.BlockSpec((w, KV_W), lambda j: (prv(j), 0)),
                  pl.BlockSpec((w, KV_W), lambda j: (j, va)), pl.BlockSpec((w, KV_W), lambda j: (prv(j), va)),
                  pl.BlockSpec((w, ATT_W), lambda j: (j, 0)), pl.BlockSpec((w, ATT_W), lambda j: (nxt(j), 0)),
                  pl.BlockSpec((w, ATT_W), lambda j: (j, 0)), pl.BlockSpec((w, ATT_W), lambda j: (nxt(j), 0)),
                  pl.BlockSpec((w, NQ), lambda j: (j, 0)), pl.BlockSpec((w, NQ), lambda j: (nxt(j), 0)),
                  pl.BlockSpec((1, NQ), lambda j: (0, 0)),
                  pl.BlockSpec((w, LANE), lambda j: (j, 0)), pl.BlockSpec((w, LANE), lambda j: (j, 0))],
        out_specs=[pl.BlockSpec((w, ATT_W + 2 * KV_W), lambda j: (j, 0)), pl.BlockSpec((1, NQ), lambda j: (0, 0))],
        out_shape=[_sds((s, ATT_W + 2 * KV_W), BF16), _sds((1, NQ), F32)],
        compiler_params=_params(dimension_semantics=("arbitrary",)),
    )(qr, qr, kr, kr, proj, proj, do, do, o, o, lse, lse, sinks, cos4, nsin4)


def _mm_nn(name, a, w, *, out_dtypes, epilogue=_store, bm_pref=1024, bn_pref=1024, after=()):
    m, k = a.shape
    n = w.shape[-1]
    bm, bn = _tile(m, bm_pref), _tile(n, bn_pref)
    tile = pl.BlockSpec((bm, bn), lambda i, j, kk: (i, j))
    return _mm(name, a, w, dims=NN, grid=(m // bm, n // bn, 1),
               a_spec=pl.BlockSpec((bm, k), lambda i, j, kk: (i, 0)),
               b_spec=pl.BlockSpec((k, bn), lambda i, j, kk: (0, j)),
               out_shape=[_sds((m, n), dt) for dt in out_dtypes], out_specs=[tile] * len(out_dtypes),
               epilogue=epilogue, after=after)


def _mm_tn(name, a, b, *, bm_pref=1024, bn_pref=1024, after=()):
    s, m = a.shape
    n = b.shape[-1]
    bm, bn = _tile(m, bm_pref), _tile(n, bn_pref)
    return _mm(name, a, b, dims=TN, grid=(m // bm, n // bn, 1),
               a_spec=pl.BlockSpec((s, bm), lambda i, j, kk: (0, i)),
               b_spec=pl.BlockSpec((s, bn), lambda i, j, kk: (0, j)),
               out_shape=[_sds((m, n), BF16)], out_specs=[pl.BlockSpec((bm, bn), lambda i, j, kk: (i, j))],
               epilogue=_store, after=after)[0]


class _Gather:
    def __init__(self, tag, layer, names, shards, fulls, after):
        self.tag, self.names = tag, names
        self.axes = [SHARD_AXIS[n] for n in names]
        self.srcs = [shards[n] for n in names]
        self.mk1 = _mk_gather_ici(layer, self.axes)
        self.mk2 = _mk_gather_d2d(self.axes)
        self.n_sem = 3 * len(names)
        self.s1, self.r1, self.lands, self.token = _split_start(
            tag + "_ici_start", self.srcs, [fulls[n] for n in names], self.mk1, self.n_sem, after)

    def forward(self, after=()):
        lands = _split_wait(self.tag + "_ici_wait", self.srcs, self.lands, self.s1, self.r1, self.mk1, after)
        self.s2, self.r2, self.lands, tok = _split_start(self.tag + "_d2d_start", [], lands, self.mk2, self.n_sem)
        return tok

    def done(self, after=()):
        lands = _split_wait(self.tag + "_d2d_wait", [], self.lands, self.s2, self.r2, self.mk2, after)
        return dict(zip(self.names, lands))


class _Reduce:
    def __init__(self, tag, names, parts, cidx, mcidx, after=()):
        self.tag, self.names, self.cidx, self.mcidx = tag, names, cidx, mcidx
        self.axes = [SHARD_AXIS[n] for n in names]
        self.parts = [parts[n] for n in names]
        self.mk = _mk_swap(self.axes)
        lands = []
        for p, ax in zip(self.parts, self.axes):
            k, n = p.shape
            lands.append(lax.empty((k, n // 2) if ax == 0 else (k // 2, n), BF16))
        self.s, self.r, self.lands, self.token = _split_start(
            tag + "_swap_start", self.parts, lands, self.mk, len(names), after)

    def scatter(self, after=()):
        got = _split_wait(self.tag + "_swap_wait", self.parts, self.lands, self.s, self.r, self.mk, after)
        self.sums = [_add_half(f"{self.tag}_add_{n}", p, g, ax, self.cidx)
                     for n, p, g, ax in zip(self.names, self.parts, got, self.axes)]
        self.mk = _mk_scatter(self.axes)
        lands = []
        for q, ax in zip(self.sums, self.axes):
            k, n = q.shape
            lands.append(lax.empty((3, k // 4, n) if ax == 0 else (3, k, n // 4), BF16))
        self.s, self.r, self.lands, tok = _split_start(
            self.tag + "_scatter_start", self.sums, lands, self.mk, 3 * len(self.names))
        return tok

    def exchange(self, after=()):
        slots = _split_wait(self.tag + "_scatter_wait", self.sums, self.lands, self.s, self.r, self.mk, after)
        halves = [_sum_half(f"{self.tag}_sum_{n}", q, sl, ax, self.mcidx)
                  for n, q, sl, ax in zip(self.names, self.sums, slots, self.axes)]
        self.mk = _mk_exchange(self.axes)
        self.s, self.r, self.lands, tok = _split_start(
            self.tag + "_exchange_start", [], halves, self.mk, len(self.names))
        return tok

    def done(self, after=()):
        grads = _split_wait(self.tag + "_exchange_wait", [], self.lands, self.s, self.r, self.mk, after)
        return dict(zip(self.names, grads))


def _pack(arrs):
    flat = jnp.concatenate([a.reshape(-1) for a in arrs])
    n = flat.shape[0]
    pad = (-n) % (8 * LANE)
    return jnp.pad(flat, (0, pad)).reshape(-1, LANE)


def _unpack(packed, shapes):
    flat = packed.reshape(-1)
    out, off = [], 0
    for sh in shapes:
        n = math.prod(sh)
        out.append(flat[off:off + n].reshape(sh))
        off += n
    return out


def kernel(x, mem, w_in, b_gate, ln_v_g, ln_v_b, w_s, b_s, sinks, w_br_a, w_br_b, w_o, ln1_g, ln1_b, w_xq, w_xkv, w_xo, ln2_g, ln2_b, w_up, w_down, ln3_g, ln3_b, loss_target, m_w_in, m_b_gate, m_ln_v_g, m_ln_v_b, m_w_s, m_b_s, m_sinks, m_w_br_a, m_w_br_b, m_w_o, m_ln1_g, m_ln1_b, m_w_xq, m_w_xkv, m_w_xo, m_ln2_g, m_ln2_b, m_w_up, m_w_down, m_ln3_g, m_ln3_b, v_w_in, v_b_gate, v_ln_v_g, v_ln_v_b, v_w_s, v_b_s, v_sinks, v_w_br_a, v_w_br_b, v_w_o, v_ln1_g, v_ln1_b, v_w_xq, v_w_xkv, v_w_xo, v_ln2_g, v_ln2_b, v_w_up, v_w_down, v_ln3_g, v_ln3_b):
    env = dict(locals())
    wts = {n: env[n] for n in WEIGHTS}
    mom_m = {n: env["m_" + n] for n in WEIGHTS}
    mom_v = {n: env["v_" + n] for n in WEIGHTS}
    s, d = x.shape[1], x.shape[2]
    dff = 4 * w_up.shape[-1]
    iw = 4 * w_in.shape[-1]
    xf = x.reshape(s, d)
    tgt = loss_target.reshape(s, d)
    memf = mem.reshape(mem.shape[1], d)
    ax_x, ax_y, ax_c = lax.axis_index("x"), lax.axis_index("y"), lax.axis_index("c")
    meidx = jnp.reshape(2 * ax_x + ax_y, (1,)).astype(jnp.int32)
    cidx = jnp.reshape(ax_c, (1,)).astype(jnp.int32)
    mcidx = jnp.concatenate([meidx, cidx])

    inv = 1.0 / (10000.0 ** (jnp.arange(0, HD, 2, dtype=F32) / HD))
    ang = jnp.arange(s, dtype=F32)[:, None] * inv[None, :]
    cos, sin = jnp.cos(ang), jnp.sin(ang)
    cos4 = jnp.tile(cos, (1, 4))
    sin4 = jnp.concatenate([-sin, sin, -sin, sin], axis=-1)
    nsin4 = -sin4

    small = {}
    for n in SMALL:
        w = wts[n]
        if n == "w_s":
            small[n] = [w[l] for l in range(DEPTH)]
        elif n == "b_s":
            small["b_st"] = [w[l].T for l in range(DEPTH)]
        else:
            small[n] = [w[l][None, :] for l in range(DEPTH)]

    shards, fulls = {}, [{}, {}]
    tok = ()
    gathers = [[None] * len(GROUPS_FWD) for _ in range(DEPTH)]
    for gi, names in enumerate(GROUPS_FWD):
        for n in names:
            shards[n] = _cast_bf16("cast_" + n, wts[n], after=tok)
            fulls[0][n], fulls[1][n] = _place_own("place_" + n, shards[n], SHARD_AXIS[n], meidx)
        gathers[0][gi] = _Gather(f"ag0_{gi}", 0, names, shards, fulls[0], tok)
        tok = (gathers[0][gi].token,)
    for gi, names in enumerate(GROUPS_FWD):
        gathers[1][gi] = _Gather(f"ag1_{gi}", 1, names, shards, fulls[1], tok)
        tok = (gathers[1][gi].token,)

    xb = _cast2d("cast_x", xf)
    memb = _cast2d("cast_mem", memf)

    saved = []
    hf, hb = xf, xb
    nxt_tok = gathers[0][0].forward(after=tok)
    for l in range(DEPTH):
        t = f"l{l}_"
        ga, gb, gc = gathers[l]
        full = ga.done(after=(nxt_tok, hb))
        sv = {"xf": hf, "xb": hb}
        proj = _mm_nn(t + "proj", hb, full["w_in"], out_dtypes=[F32], bn_pref=1280)[0]
        tok_b = gb.forward(after=(proj,))
        sg = _gmlp_fwd(t + "gmlp_fwd", proj, small["ln_v_g"][l], small["ln_v_b"][l], small["w_s"][l],
                       small["b_st"][l])
        attn, qr, kr, lse = _swa_fwd(t + "swa_fwd", proj, cos4, sin4, small["sinks"][l], after=(tok_b,))
        full.update(gb.done(after=(attn,)))
        merged, ya, yb = _gate_fwd(t + "gate_fwd", sg, attn, full["w_br_a"], full["w_br_b"], proj,
                                   small["b_gate"][l], d)
        tok_c = gc.forward(after=(merged,))
        bm = _tile(s, 512)
        row = pl.BlockSpec((bm, d), lambda i, j, k: (i, 0))
        r1, x1, x1b = _mm(
            t + "o_ln", merged, full["w_o"], dims=NN, grid=(s // bm, 1, 1),
            a_spec=row, b_spec=pl.BlockSpec((d, d), lambda i, j, k: (0, 0)),
            extras=(hf, small["ln1_g"][l], small["ln1_b"][l]), extra_specs=(row, _vec_spec(d), _vec_spec(d)),
            out_shape=[_sds((s, d), F32), _sds((s, d), F32), _sds((s, d), BF16)], out_specs=[row] * 3,
            epilogue=_ep_residual_ln, after=(tok_c,))
        kv = _mm_nn(t + "xkv", memb, full["w_xkv"], out_dtypes=[BF16])[0]
        q, o, r2, x2, x2b = _xattn_fwd(t + "xattn_fwd", x1b, x1, full["w_xq"], kv, full["w_xo"],
                                       small["ln2_g"][l], small["ln2_b"][l])
        full.update(gc.done(after=(x2b,)))

        def ep_up(acc, ex, outs):
            outs[0][...] = acc.astype(BF16)
            rl = jnp.maximum(acc, 0.0)
            outs[1][...] = (rl * rl).astype(BF16)

        h, a = _mm_nn(t + "up", x2b, full["w_up"], out_dtypes=[BF16, BF16], epilogue=ep_up)
        nxt_tok = gathers[l + 1][0].forward(after=(h,)) if l + 1 < DEPTH else None
        bk = _tile(dff, 1024)
        r3, x3, x3b = _mm(
            t + "down_ln", a, full["w_down"], dims=NN, grid=(s // bm, 1, dff // bk),
            a_spec=pl.BlockSpec((bm, bk), lambda i, j, k: (i, k)),
            b_spec=pl.BlockSpec((bk, d), lambda i, j, k: (k, 0)),
            extras=(x2, small["ln3_g"][l], small["ln3_b"][l]), extra_specs=(row, _vec_spec(d), _vec_spec(d)),
            out_shape=[_sds((s, d), F32), _sds((s, d), F32), _sds((s, d), BF16)], out_specs=[row] * 3,
            epilogue=_ep_residual_ln, acc_shape=(bm, d), after=() if nxt_tok is None else (nxt_tok,))
        sv.update(proj=proj, sg=sg, attn=attn, qr=qr, kr=kr, lse=lse, merged=merged, ya=ya, yb=yb, r1=r1, x1=x1,
                  x1b=x1b, kv=kv, q=q, o=o, r2=r2, x2b=x2b, h=h, a=a, r3=r3, full=full)
        saved.append(sv)
        hf, hb = x3, x3b
    dy, loss11 = _loss_grad("loss", hf, tgt)
    loss = lax.psum(loss11[0, 0], ("x", "y", "c"))

    small_g = [None] * DEPTH
    grads = [{}, {}]
    pend_a = None
    pend_b = None
    g = dy
    for l in reversed(range(DEPTH)):
        t = f"l{l}_"
        sv = saved[l]
        full = sv["full"]
        dw, sgo = {}, {}
        dr3, dr3b, sgo["ln3_g"], sgo["ln3_b"] = _ln_bwd(t + "ln3_bwd", g, sv["r3"], small["ln3_g"][l])
        bm, bn = _tile(s, 1024), _tile(dff, 1024)

        def ep_dh(acc, ex, outs):
            outs[0][...] = (acc * (2.0 * jnp.maximum(ex[0][...].astype(F32), 0.0))).astype(BF16)

        tile = pl.BlockSpec((bm, bn), lambda i, j, k: (i, j))
        dh = _mm(t + "dh", dr3b, full["w_down"], dims=NT, grid=(s // bm, dff // bn, 1),
                 a_spec=pl.BlockSpec((bm, d), lambda i, j, k: (i, 0)),
                 b_spec=pl.BlockSpec((bn, d), lambda i, j, k: (j, 0)),
                 extras=(sv["h"],), extra_specs=(tile,), out_shape=[_sds((s, dff), BF16)], out_specs=[tile],
                 epilogue=ep_dh)[0]
        if pend_a is not None:
            tok_pa = pend_a.exchange(after=(dh,))
            grads[l + 1].update(pend_b.done(after=(dh,)))
        dw["w_down"] = _mm_tn(t + "dw_down", sv["a"], dr3b, after=() if pend_a is None else (tok_pa,))
        dw["w_up"] = _mm_tn(t + "dw_up", sv["x2b"], dh)
        red_c = _Reduce(t + "rs_c", GROUPS_FWD[2], dw, cidx, mcidx)
        bm2 = _tile(s, 512)
        bk = _tile(dff, 1024)
        row2 = pl.BlockSpec((bm2, d), lambda i, j, k: (i, 0))
        dx2 = _mm(t + "dx2", dh, full["w_up"], dims=NT, grid=(s // bm2, 1, dff // bk),
                  a_spec=pl.BlockSpec((bm2, bk), lambda i, j, k: (i, k)),
                  b_spec=pl.BlockSpec((d, bk), lambda i, j, k: (0, k)),
                  extras=(dr3,), extra_specs=(row2,), out_shape=[_sds((s, d), F32)], out_specs=[row2],
                  epilogue=_ep_add_scaled, acc_shape=(bm2, d), after=(red_c.token,))[0]
        tok_c = red_c.scatter(after=(dx2,))
        if pend_a is not None:
            grads[l + 1].update(pend_a.done(after=(dx2,)))
            pend_a = None

        dr2, dr2b, sgo["ln2_g"], sgo["ln2_b"] = _ln_bwd(t + "ln2_bwd", dx2, sv["r2"], small["ln2_g"][l],
                                                        after=(tok_c,))
        dx1, dq, dkv = _xattn_bwd(t + "xattn_bwd", dr2b, dr2, sv["q"], sv["kv"], full["w_xo"], full["w_xq"])
        dw["w_xo"] = _mm_tn(t + "dw_xo", sv["o"], dr2b)
        dw["w_xq"] = _mm_tn(t + "dw_xq", sv["x1b"], dq)
        dw["w_xkv"] = _mm_tn(t + "dw_xkv", memb, _cast2d(t + "dkv_cast", dkv))

        dr1, dr1b, sgo["ln1_g"], sgo["ln1_b"] = _ln_bwd(t + "ln1_bwd", dx1, sv["r1"], small["ln1_g"][l])
        dya, dyb, dgate, dba, dbb = _gate_bwd(t + "gate_bwd", dr1b, full["w_o"], sv["proj"], sv["ya"], sv["yb"],
                                              small["b_gate"][l], d)
        sgo["b_gate"] = jnp.concatenate([dba, dbb], axis=-1)
        tok_c = red_c.exchange(after=(dya,))
        dw["w_o"] = _mm_tn(t + "dw_o", sv["merged"], dr1b, after=(tok_c,))
        dw["w_br_a"] = _mm_tn(t + "dw_br_a", sv["sg"], dya)
        dw["w_br_b"] = _mm_tn(t + "dw_br_b", sv["attn"], dyb)
        red_b = _Reduce(t + "rs_b", GROUPS_FWD[1], dw, cidx, mcidx)

        def dbranch(name, dyx, w, after):
            bk2 = _tile(d, 1024)
            return _mm(name, dyx, w, dims=NT, grid=(s // bm, 1, d // bk2),
                       a_spec=pl.BlockSpec((bm, bk2), lambda i, j, k: (i, k)),
                       b_spec=pl.BlockSpec((w.shape[0], bk2), lambda i, j, k: (0, k)),
                       out_shape=[_sds((s, w.shape[0]), BF16)],
                       out_specs=[pl.BlockSpec((bm, w.shape[0]), lambda i, j, k: (i, 0))],
                       epilogue=_store, acc_shape=(bm, w.shape[0]), after=after)[0]

        dsg = dbranch(t + "dsg", dya, full["w_br_a"], (red_b.token,))
        dattn = dbranch(t + "dattn", dyb, full["w_br_b"], ())
        tok_b = red_b.scatter(after=(dattn,))
        grads[l].update(red_c.done(after=(dattn,)))
        duv, sgo["w_s"], dbst, dlg, dlb = _gmlp_bwd(t + "gmlp_bwd", sv["proj"], dsg, small["ln_v_g"][l],
                                                    small["ln_v_b"][l], small["w_s"][l], small["b_st"][l])
        sgo["b_s"] = dbst.T
        sgo["ln_v_g"], sgo["ln_v_b"] = dlg, dlb
        dqkv, sgo["sinks"] = _swa_bwd(t + "swa_bwd", sv["qr"], sv["kr"], sv["proj"], dattn, sv["attn"], sv["lse"],
                                      small["sinks"][l], cos4, nsin4, after=(tok_b,))
        dproj = jnp.concatenate([duv, dqkv, dgate[0], dgate[1]], axis=-1)
        tok_b = red_b.exchange(after=(dproj,))
        dw["w_in"] = _mm_tn(t + "dw_in", sv["xb"], dproj, after=(tok_b,))
        red_a = _Reduce(t + "rs_a", GROUPS_FWD[0], dw, cidx, mcidx)
        bk3 = _tile(iw, 1280)
        g = _mm(t + "dx0", dproj, full["w_in"], dims=NT, grid=(s // bm2, 1, iw // bk3),
                a_spec=pl.BlockSpec((bm2, bk3), lambda i, j, k: (i, k)),
                b_spec=pl.BlockSpec((d, bk3), lambda i, j, k: (0, k)),
                extras=(dr1,), extra_specs=(row2,), out_shape=[_sds((s, d), F32)], out_specs=[row2],
                epilogue=_ep_add_scaled, acc_shape=(bm2, d), after=(red_a.token,))[0]
        tok_a = red_a.scatter(after=(g,))
        pend_a, pend_b = red_a, red_b
        small_g[l] = sgo
    grad_x = g.reshape(x.shape)

    big_out = {}

    def adam_layer(l, names):
        for n in names:
            prev = big_out.get(n)
            big_out[n] = _adamw(f"adamw{l}_{n}", wts[n], grads[l][n], mom_m[n], mom_v[n], l, prev)

    shapes = [wts[n].shape for n in SMALL]
    packed_g = _pack([jnp.stack([small_g[l][n].reshape(wts[n].shape[1:]) for l in range(DEPTH)]) for n in SMALL])
    packed_g = _allreduce_small("ar_small", packed_g, after=(tok_a,))
    grads[0].update(pend_b.done(after=(packed_g,)))
    for l in reversed(range(DEPTH)):
        adam_layer(l, GROUPS_FWD[2])
    adam_layer(1, GROUPS_FWD[1])
    tok_a = pend_a.exchange(after=(big_out["w_o"][0],))
    adam_layer(1, GROUPS_FWD[0])
    adam_layer(0, GROUPS_FWD[1])
    pw, pm, pv = (_pack([src[n] for n in SMALL]) for src in (wts, mom_m, mom_v))
    small4 = _adamw("adamw_small", pw[None], packed_g, pm[None], pv[None], 0)
    small_out = [dict(zip(SMALL, _unpack(a[0], shapes))) for a in small4]
    grads[0].update(pend_a.done(after=(small4[0],)))
    adam_layer(0, GROUPS_FWD[0])

    def pick(kind, n):
        return big_out[n][kind] if n in big_out else small_out[kind][n]

    return (loss, grad_x, *[pick(0, n) for n in WEIGHTS], *[pick(1, n) for n in WEIGHTS],
            *[pick(2, n) for n in WEIGHTS], *[pick(3, n) for n in WEIGHTS])
```

```python
import math

import jax
import jax.numpy as jnp
from jax import lax
from jax.experimental import pallas as pl
from jax.experimental.pallas import tpu as pltpu

F32 = jnp.float32
BF16 = jnp.bfloat16
MESH = pl.DeviceIdType.MESH
ANY = pl.BlockSpec(memory_space=pl.ANY)
HBM = pl.BlockSpec(memory_space=pltpu.HBM)
SEM = pl.BlockSpec(memory_space=pltpu.SEMAPHORE)
VMEM_SPEC = pl.BlockSpec(memory_space=pltpu.VMEM)
EFFECT = pltpu.SideEffectType.DATAFLOW_SIDE_EFFECTING

DEPTH = 2
CHUNK = 128
GMLP_W = 1024
GROUPS = 8
NQ, NKV, HD = 16, 4, 64
ATT_W = NQ * HD
KV_W = NKV * HD
XH, XHD = 4, 128
X_W = XH * XHD
LN_EPS = 1e-5
ALPHA = (2 * DEPTH) ** 0.25
OFF_Q = 2 * GMLP_W
OFF_K = OFF_Q + ATT_W
OFF_VA = OFF_K + KV_W
OFF_GA = OFF_VA + KV_W
NEG = -1e30

ADAM_LR, ADAM_B1, ADAM_B2, ADAM_EPS, ADAM_WD, ADAM_STEP = 0.001, 0.9, 0.999, 1e-08, 0.01, 10

V7X_VMEM_BYTES = 64 * 1024 * 1024
VMEM_LIMIT = V7X_VMEM_BYTES - 12 * 1024 * 1024
LANE = 128

BIG = ("w_in", "w_br_a", "w_br_b", "w_o", "w_xq", "w_xkv", "w_xo", "w_up", "w_down")
SHARD_AXIS = {"w_in": 1, "w_br_a": 1, "w_br_b": 1, "w_o": 0, "w_xq": 0, "w_xkv": 0, "w_xo": 1,
              "w_up": 1, "w_down": 0}
GROUPS_FWD = (("w_in",), ("w_br_a", "w_br_b", "w_o", "w_xq", "w_xkv", "w_xo"), ("w_up", "w_down"))
SMALL = ("b_gate", "ln_v_g", "ln_v_b", "w_s", "b_s", "sinks", "ln1_g", "ln1_b", "ln2_g", "ln2_b",
         "ln3_g", "ln3_b")
WEIGHTS = ("w_in", "b_gate", "ln_v_g", "ln_v_b", "w_s", "b_s", "sinks", "w_br_a", "w_br_b", "w_o",
           "ln1_g", "ln1_b", "w_xq", "w_xkv", "w_xo", "ln2_g", "ln2_b", "w_up", "w_down", "ln3_g", "ln3_b")


def _pallas(body, after=(), **kw):
    n_after = len(after)
    if not n_after:
        return pl.pallas_call(body, **kw)
    n_in = len(kw["in_specs"])
    kw["in_specs"] = list(kw["in_specs"]) + [ANY] * n_after

    def tied(*refs):
        return body(*refs[:n_in], *refs[n_in + n_after:])

    call = pl.pallas_call(tied, **kw)
    return lambda *ops: call(*ops, *after)


def _params(**kw):
    return pltpu.CompilerParams(vmem_limit_bytes=VMEM_LIMIT, **kw)


def _tile(dim, pref, unit=LANE):
    best = None
    t = unit
    while t <= min(dim, pref):
        if dim % t == 0:
            best = t
        t += unit
    return best if best is not None else dim


def _dot(a, b, dims):
    return lax.dot_general(a, b, (dims, ((), ())), preferred_element_type=F32)


NN = ((1,), (0,))
NT = ((1,), (1,))
TN = ((0,), (0,))


def _bf(x):
    return x if x.dtype == BF16 else x.astype(BF16)


def _sds(shape, dtype):
    return jax.ShapeDtypeStruct(shape, dtype)


def _mm(name, a, b, *, dims, grid, a_spec, b_spec, out_shape, out_specs, epilogue,
        extras=(), extra_specs=(), acc_shape=None, after=()):
    nk = grid[2]
    n_ex, n_out = len(extras), len(out_shape)

    def body(*refs):
        a_ref, b_ref = refs[0], refs[1]
        ex = refs[2:2 + n_ex]
        outs = refs[2 + n_ex:2 + n_ex + n_out]
        part = _dot(_bf(a_ref[...]), _bf(b_ref[...]), dims)
        if nk == 1:
            epilogue(part, ex, outs)
        else:
            acc = refs[-1]
            k = pl.program_id(2)

            @pl.when(k == 0)
            def _():
                acc[...] = part

            @pl.when(k > 0)
            def _():
                acc[...] += part

            @pl.when(k == nk - 1)
            def _():
                epilogue(acc[...], ex, outs)

    scratch = [pltpu.VMEM(acc_shape, F32)] if nk > 1 else []
    return _pallas(
        body, after=after, name=name, grid=grid, in_specs=[a_spec, b_spec, *extra_specs], out_specs=list(out_specs),
        out_shape=list(out_shape), scratch_shapes=scratch,
        compiler_params=_params(dimension_semantics=("arbitrary",) * 3),
    )(a, b, *extras)


def _store(acc, ex, outs):
    for o in outs:
        o[...] = acc.astype(o.dtype)


def _ln_rows(r, g, b):
    mu = jnp.mean(r, axis=-1, keepdims=True)
    xc = r - mu
    var = jnp.mean(xc * xc, axis=-1, keepdims=True)
    rstd = lax.rsqrt(var + LN_EPS)
    xhat = xc * rstd
    return xhat * g + b, xhat, rstd


def _ep_residual_ln(acc, ex, outs):
    x_ref, g_ref, b_ref = ex
    r_ref, y_ref, yb_ref = outs
    r = ALPHA * x_ref[...] + acc
    y, _, _ = _ln_rows(r, g_ref[...], b_ref[...])
    r_ref[...] = r
    y_ref[...] = y
    yb_ref[...] = y.astype(BF16)


def _ep_add_scaled(acc, ex, outs):
    outs[0][...] = acc + ALPHA * ex[0][...]


def _vec_spec(width):
    return pl.BlockSpec((1, width), lambda i, j, k: (0, 0))


_GC = math.sqrt(2.0 / math.pi)


def _gelu(x):
    t = jnp.tanh(_GC * (x + 0.044715 * (x * x * x)))
    return 0.5 * x * (1.0 + t), t


def _gelu_grad(x, t):
    return 0.5 * (1.0 + t) + 0.5 * x * (1.0 - t * t) * (_GC * (1.0 + 3.0 * 0.044715 * x * x))


def _sigmoid(x):
    return 1.0 / (1.0 + jnp.exp(-x))


GRP = NQ // NKV


def _band_mask(prev_ok, prev_only=False):
    cols = CHUNK if prev_only else 2 * CHUNK
    row = jnp.bitwise_and(lax.broadcasted_iota(jnp.int32, (GRP * CHUNK, cols), 0), CHUNK - 1)
    col = lax.broadcasted_iota(jnp.int32, (GRP * CHUNK, cols), 1)
    prev = jnp.logical_and(jnp.logical_and(col < CHUNK, col > row), prev_ok)
    if prev_only:
        return prev
    return jnp.logical_or(prev, jnp.logical_and(col >= CHUNK, col - CHUNK <= row))


def _pair(x, g):
    return x[:, (g // 2) * LANE:(g // 2 + 1) * LANE]


def _own_head(x, g):
    xp = _pair(x, g)
    lane = lax.broadcasted_iota(jnp.int32, xp.shape, 1)
    lo = (g % 2) * HD
    return jnp.where(jnp.logical_and(lane >= lo, lane < lo + HD), xp, jnp.zeros_like(xp))


def _stack_heads(x, g):
    a = x[:, g * GRP * HD:g * GRP * HD + LANE]
    b = x[:, g * GRP * HD + LANE:(g + 1) * GRP * HD]
    ar, br = pltpu.roll(a, HD, 1), pltpu.roll(b, HD, 1)
    parts = [a, ar, b, br] if g % 2 == 0 else [ar, a, br, b]
    return jnp.concatenate(parts, axis=0).astype(BF16)


def _unstack_heads(og, g):
    o = [og[h * CHUNK:(h + 1) * CHUNK] for h in range(GRP)]
    lo = lax.broadcasted_iota(jnp.int32, (CHUNK, LANE), 1) < HD
    if g % 2 == 0:
        x0, x1, x2, x3 = o[0], pltpu.roll(o[1], HD, 1), o[2], pltpu.roll(o[3], HD, 1)
    else:
        x0, x1, x2, x3 = pltpu.roll(o[0], HD, 1), o[1], pltpu.roll(o[2], HD, 1), o[3]
    return [jnp.where(lo, x0, x1), jnp.where(lo, x2, x3)]


def _stack_cols(x, g):
    return jnp.concatenate([x[:, g * GRP + h:g * GRP + h + 1] for h in range(GRP)], axis=0)


def _head_sums(x):
    lo = lax.broadcasted_iota(jnp.int32, (x.shape[0], LANE), 1) < HD
    cols = []
    for p in range(NQ // 2):
        xp = x[:, p * LANE:(p + 1) * LANE]
        cols.append(jnp.sum(jnp.where(lo, xp, 0.0), axis=-1, keepdims=True))
        cols.append(jnp.sum(jnp.where(lo, 0.0, xp), axis=-1, keepdims=True))
    return jnp.concatenate(cols, axis=-1)


def _rope(x, cos, sin_signed):
    w = x.shape[-1]
    lane = lax.broadcasted_iota(jnp.int32, x.shape, 1)
    first = (lane % HD) < (HD // 2)
    partner = jnp.where(first, pltpu.roll(x, w - HD // 2, 1), pltpu.roll(x, HD // 2, 1))
    reps = w // LANE
    return x * jnp.tile(cos, (1, reps)) + partner * jnp.tile(sin_signed, (1, reps))


def _cast_bf16(name, w, after=()):
    _, r, c = w.shape
    br = _tile(r, 512, 8)

    def body(w_ref, o_ref):
        o_ref[...] = w_ref[...].astype(BF16)

    spec = pl.BlockSpec((None, br, c), lambda l, i: (l, i, 0))
    return _pallas(body, after=after, name=name, grid=(2, r // br), in_specs=[spec], out_specs=spec,
                   out_shape=_sds(w.shape, BF16), compiler_params=_params())(w)


def _cast2d(name, x):
    s, d = x.shape
    bm = _tile(s, 512, 8)

    def body(x_ref, o_ref):
        o_ref[...] = x_ref[...].astype(BF16)

    spec = pl.BlockSpec((bm, d), lambda i: (i, 0))
    return _pallas(body, name=name, grid=(s // bm,), in_specs=[spec], out_specs=spec,
                   out_shape=_sds(x.shape, BF16), compiler_params=_params())(x)


def _place():
    x, y, c = lax.axis_index("x"), lax.axis_index("y"), lax.axis_index("c")
    chips = [(1 - x, y), (x, 1 - y), (1 - x, 1 - y)]
    return x, y, c, chips


def _cut(ref, axis, chip=None, half=None, lead=()):
    k, n = ref.shape[-2], ref.shape[-1]
    rows, cols = slice(None), slice(None)
    if chip is not None:
        if axis == 0:
            rows = pl.ds(pl.multiple_of(chip * (k // 4), 8), k // 4)
        else:
            cols = pl.ds(pl.multiple_of(chip * (n // 4), LANE), n // 4)
    if half is not None:
        if axis == 0:
            cols = pl.ds(pl.multiple_of(half * (n // 2), LANE), n // 2)
        else:
            rows = pl.ds(pl.multiple_of(half * (k // 2), 8), k // 2)
    return ref.at[(*lead, rows, cols)]


def _split_start(name, srcs, lands, make, n_sem, after=()):
    ns, nl, na = len(srcs), len(lands), len(after)

    def body(*refs):
        src, land = refs[:ns], refs[ns:ns + nl]
        outs = refs[ns + nl + na:]
        for out_cp, _ in make(src, land, outs[0], outs[1]):
            out_cp.start()
        outs[-1][...] = jnp.zeros_like(outs[-1])

    res = pl.pallas_call(
        body, name=name, in_specs=[HBM] * (ns + nl) + [ANY] * na,
        out_specs=[SEM, SEM] + [HBM] * nl + [VMEM_SPEC],
        out_shape=[pltpu.SemaphoreType.DMA((n_sem,)), pltpu.SemaphoreType.DMA((n_sem,))]
        + [pltpu.HBM(a.shape, a.dtype) for a in lands] + [_sds((8, LANE), F32)],
        input_output_aliases={ns + i: 2 + i for i in range(nl)},
        compiler_params=pltpu.CompilerParams(has_side_effects=EFFECT),
    )(*[pltpu.with_memory_space_constraint(a, pltpu.HBM) for a in (*srcs, *lands)], *after)
    return res[0], res[1], list(res[2:2 + nl]), res[-1]


def _split_wait(name, srcs, lands, ssem, rsem, make, after=()):
    ns, nl, na = len(srcs), len(lands), len(after)

    def body(*refs):
        src, land = refs[:ns], refs[ns:ns + nl]
        s_ref, r_ref = refs[ns + nl], refs[ns + nl + 1]
        pairs = make(src, land, s_ref, r_ref)
        for _, in_cp in pairs:
            in_cp.wait_recv()
        for out_cp, _ in pairs:
            out_cp.wait_send()

    res = pl.pallas_call(
        body, name=name, in_specs=[HBM] * (ns + nl) + [SEM, SEM] + [ANY] * na,
        out_specs=[HBM] * nl, out_shape=[pltpu.HBM(a.shape, a.dtype) for a in lands],
        input_output_aliases={ns + i: i for i in range(nl)},
        compiler_params=pltpu.CompilerParams(has_side_effects=EFFECT),
    )(*srcs, *lands, ssem, rsem, *after)
    return list(res)


def _rcopy(src, dst, ssem, rsem, k, dev):
    return pltpu.make_async_remote_copy(src_ref=src, dst_ref=dst, send_sem=ssem.at[k], recv_sem=rsem.at[k],
                                        device_id=dev, device_id_type=MESH)


def _mk_gather_ici(layer, axes):
    def make(src, land, ssem, rsem):
        x, y, c, chips = _place()
        me = 2 * x + y
        pairs = []
        for w, ax in enumerate(axes):
            mine = _cut(src[w], ax, half=c, lead=(layer,))
            for j, (px, py) in enumerate(chips):
                dev = (px, py, c)
                out_cp = _rcopy(mine, _cut(land[w], ax, chip=me, half=c), ssem, rsem, 3 * w + j, dev)
                got = _cut(land[w], ax, chip=2 * px + py, half=c)
                pairs.append((out_cp, _rcopy(got, got, ssem, rsem, 3 * w + j, dev)))
        return pairs
    return make


def _mk_gather_d2d(axes):
    def make(src, land, ssem, rsem):
        x, y, c, chips = _place()
        sib = (x, y, 1 - c)
        pairs = []
        for w, ax in enumerate(axes):
            for j, (px, py) in enumerate(chips):
                have = _cut(land[w], ax, chip=2 * px + py, half=c)
                want = _cut(land[w], ax, chip=2 * px + py, half=1 - c)
                pairs.append((_rcopy(have, have, ssem, rsem, 3 * w + j, sib),
                              _rcopy(want, want, ssem, rsem, 3 * w + j, sib)))
        return pairs
    return make


def _mk_swap(axes):
    def make(src, land, ssem, rsem):
        x, y, c, _ = _place()
        sib = (x, y, 1 - c)
        pairs = []
        for w, ax in enumerate(axes):
            cp = _rcopy(_cut(src[w], ax, half=1 - c), land[w], ssem, rsem, w, sib)
            pairs.append((cp, cp))
        return pairs
    return make


def _mk_scatter(axes):
    def make(src, land, ssem, rsem):
        x, y, c, chips = _place()
        pairs = []
        for w, ax in enumerate(axes):
            for j, (px, py) in enumerate(chips):
                cp = _rcopy(_cut(src[w], ax, chip=2 * px + py), land[w].at[j], ssem, rsem, 3 * w + j, (px, py, c))
                pairs.append((cp, cp))
        return pairs
    return make


def _mk_exchange(axes):
    def make(src, land, ssem, rsem):
        x, y, c, _ = _place()
        sib = (x, y, 1 - c)
        pairs = []
        for w, ax in enumerate(axes):
            have = _cut(land[w], ax, half=c)
            want = _cut(land[w], ax, half=1 - c)
            pairs.append((_rcopy(have, have, ssem, rsem, w, sib), _rcopy(want, want, ssem, rsem, w, sib)))
        return pairs
    return make


def _place_own(name, shard, axis, meidx):
    _, r, c = shard.shape
    full = (4 * r, c) if axis == 0 else (r, 4 * c)
    br = _tile(r, 512, 8)
    nb = r // br
    if axis == 0:
        ospec = pl.BlockSpec((br, c), lambda i, me: (me[0] * nb + i, 0))
    else:
        ospec = pl.BlockSpec((br, c), lambda i, me: (i, me[0]))

    def body(me_ref, s_ref, o0_ref, o1_ref):
        o0_ref[...] = s_ref[0]
        o1_ref[...] = s_ref[1]

    return pl.pallas_call(
        body, name=name,
        grid_spec=pltpu.PrefetchScalarGridSpec(
            num_scalar_prefetch=1, grid=(nb,),
            in_specs=[pl.BlockSpec((2, br, c), lambda i, me: (0, i, 0))], out_specs=[ospec, ospec]),
        out_shape=[_sds(full, BF16)] * 2, compiler_params=_params(),
    )(meidx, shard)


def _add_half(name, part, got, axis, cidx):
    k, n = got.shape
    bm = _tile(k, 512, 8)
    nb = k // bm
    if axis == 0:
        pspec = pl.BlockSpec((bm, n), lambda i, c: (i, c[0]))
    else:
        pspec = pl.BlockSpec((bm, n), lambda i, c: (c[0] * nb + i, 0))

    def body(c_ref, a_ref, b_ref, o_ref):
        o_ref[...] = (a_ref[...].astype(F32) + b_ref[...].astype(F32)).astype(BF16)

    return pl.pallas_call(
        body, name=name,
        grid_spec=pltpu.PrefetchScalarGridSpec(
            num_scalar_prefetch=1, grid=(nb,), in_specs=[pspec, pl.BlockSpec((bm, n), lambda i, c: (i, 0))],
            out_specs=pl.BlockSpec((bm, n), lambda i, c: (i, 0))),
        out_shape=_sds((k, n), BF16), compiler_params=_params(),
    )(cidx, part, got)


def _sum_half(name, own, slots, axis, mc):
    _, r, cc = slots.shape
    br = _tile(r, 256, 8)
    nb = r // br
    if axis == 0:
        own_spec = pl.BlockSpec((br, cc), lambda i, mc: (mc[0] * nb + i, 0))
        out_spec = pl.BlockSpec((br, cc), lambda i, mc: (i, mc[1]))
        shape = (r, 2 * cc)
    else:
        own_spec = pl.BlockSpec((br, cc), lambda i, mc: (i, mc[0]))
        out_spec = pl.BlockSpec((br, cc), lambda i, mc: (mc[1] * nb + i, 0))
        shape = (2 * r, cc)

    def body(mc_ref, own_ref, s_ref, o_ref):
        acc = own_ref[...].astype(F32)
        for i in range(3):
            acc = acc + s_ref[i].astype(F32)
        o_ref[...] = acc

    return pl.pallas_call(
        body, name=name,
        grid_spec=pltpu.PrefetchScalarGridSpec(
            num_scalar_prefetch=1, grid=(nb,),
            in_specs=[own_spec, pl.BlockSpec((3, br, cc), lambda i, mc: (0, i, 0))], out_specs=out_spec),
        out_shape=_sds(shape, F32), compiler_params=_params(),
    )(mc, own, slots)


def _allreduce_small(name, packed, after=()):
    rows, lanes = packed.shape

    def body(in_ref, out_ref, slots, loc, ssem, rsem):
        x, y, c, _ = _place()
        me = 4 * x + 2 * y + c
        mine = pltpu.make_async_copy(in_ref, slots.at[me], loc)
        mine.start()
        cps = []
        k = 0
        for dx in range(2):
            for dy in range(2):
                for dc in range(2):
                    if dx == 0 and dy == 0 and dc == 0:
                        continue
                    peer = (1 - x if dx else x, 1 - y if dy else y, 1 - c if dc else c)
                    cp = _rcopy(in_ref, slots.at[me], ssem, rsem, k, peer)
                    cp.start()
                    cps.append((cp, peer))
                    k += 1
        for k, (cp, peer) in enumerate(cps):
            slot = slots.at[4 * peer[0] + 2 * peer[1] + peer[2]]
            _rcopy(slot, slot, ssem, rsem, k, peer).wait_recv()
        for cp, _ in cps:
            cp.wait_send()
        mine.wait()
        acc = slots[0]
        for i in range(1, 8):
            acc = acc + slots[i]
        out_ref[...] = acc

    return _pallas(
        body, after=after, name=name, in_specs=[VMEM_SPEC], out_specs=VMEM_SPEC, out_shape=_sds(packed.shape, F32),
        scratch_shapes=[pltpu.VMEM((8, rows, lanes), F32), pltpu.SemaphoreType.DMA,
                        pltpu.SemaphoreType.DMA((7,)), pltpu.SemaphoreType.DMA((7,))],
        compiler_params=_params(has_side_effects=True),
    )(packed)


def _adamw_math(w, g, m, v):
    m2 = ADAM_B1 * m + (1.0 - ADAM_B1) * g
    v2 = ADAM_B2 * v + (1.0 - ADAM_B2) * (g * g)
    m_hat = m2 / (1.0 - ADAM_B1 ** ADAM_STEP)
    v_hat = v2 / (1.0 - ADAM_B2 ** ADAM_STEP)
    delta = -ADAM_LR * (m_hat / (jnp.sqrt(v_hat) + ADAM_EPS) + ADAM_WD * w)
    return delta, m2, v2


def _adamw(name, w, g, m, v, layer, prev=None, after=()):
    _, r, c = w.shape
    br = _tile(r, 256, 8)
    n_prev = 0 if prev is None else 4

    def body(*refs):
        w_ref, g_ref, m_ref, v_ref = refs[:4]
        go_ref, d_ref, mo_ref, vo_ref = refs[4 + n_prev:]
        gg = g_ref[...]
        delta, m2, v2 = _adamw_math(w_ref[...], gg, m_ref[...], v_ref[...])
        go_ref[...] = gg
        d_ref[...] = delta
        mo_ref[...] = m2
        vo_ref[...] = v2

    spec = pl.BlockSpec((None, br, c), lambda i: (layer, i, 0))
    return _pallas(
        body, after=after, name=name, grid=(r // br,),
        in_specs=[spec, pl.BlockSpec((br, c), lambda i: (i, 0)), spec, spec] + [ANY] * n_prev,
        out_specs=[spec] * 4, out_shape=[_sds(w.shape, F32)] * 4,
        input_output_aliases={4 + i: i for i in range(n_prev)}, compiler_params=_params(),
    )(w, g, m, v, *(prev or ()))


def _gmlp_fwd(name, proj, ln_g, ln_b, w_s, b_st):
    s = proj.shape[0]

    def body(u_ref, v_ref, g_ref, b_ref, ws_ref, bst_ref, sg_ref):
        gu, _ = _gelu(u_ref[...])
        gv, _ = _gelu(v_ref[...])
        vn, _, _ = _ln_rows(gv, g_ref[...], b_ref[...])
        vn = vn.astype(BF16)
        row = lax.broadcasted_iota(jnp.int32, (CHUNK, CHUNK), 0)
        col = lax.broadcasted_iota(jnp.int32, (CHUNK, CHUNK), 1)
        tril = col <= row
        outs = []
        for g in range(GROUPS):
            sl = slice(g * LANE, (g + 1) * LANE)
            w = jnp.where(tril, ws_ref[g], 0.0).astype(BF16)
            mixed = _dot(w, vn[:, sl], NN) + bst_ref[:, g:g + 1]
            outs.append(gu[:, sl] * mixed)
        sg_ref[...] = jnp.concatenate(outs, axis=-1).astype(BF16)

    return _pallas(
        body, name=name, grid=(s // CHUNK,),
        in_specs=[pl.BlockSpec((CHUNK, GMLP_W), lambda n: (n, 0)), pl.BlockSpec((CHUNK, GMLP_W), lambda n: (n, 1)),
                  pl.BlockSpec((1, GMLP_W), lambda n: (0, 0)), pl.BlockSpec((1, GMLP_W), lambda n: (0, 0)),
                  pl.BlockSpec((GROUPS, CHUNK, CHUNK), lambda n: (0, 0, 0)),
                  pl.BlockSpec((CHUNK, GROUPS), lambda n: (0, 0))],
        out_specs=pl.BlockSpec((CHUNK, GMLP_W), lambda n: (n, 0)),
        out_shape=_sds((s, GMLP_W), BF16), compiler_params=_params(),
    )(proj, proj, ln_g, ln_b, w_s, b_st)


def _swa_fwd(name, proj, cos4, sin4, sinks, after=()):
    s = proj.shape[0]
    w = CHUNK
    scale = HD ** -0.5

    def body(q_ref, k_ref, v_ref, cos_ref, sin_ref, sink_ref, o_ref, qr_ref, kr_ref, lse_ref, kprev, vprev):
        n = pl.program_id(0)

        @pl.when(n == 0)
        def _():
            kprev[...] = jnp.zeros_like(kprev)
            vprev[...] = jnp.zeros_like(vprev)

        cos, sin = cos_ref[...], sin_ref[...]
        qr = _rope(q_ref[...], cos, sin)
        kr = _rope(k_ref[...], cos, sin).astype(BF16)
        vb = v_ref[...].astype(BF16)
        kk = jnp.concatenate([kprev[...], kr], axis=0)
        vv = jnp.concatenate([vprev[...], vb], axis=0)
        valid = _band_mask(n > 0)
        outs, lses = [], []
        for g in range(NKV):
            qg = _stack_heads(qr, g)
            kz = _own_head(kk, g)
            sc = jnp.where(valid, _dot(qg, kz, NT) * scale, NEG)
            sink = sink_ref[g]
            mx = jnp.maximum(jnp.max(sc, axis=-1, keepdims=True), sink)
            p = jnp.exp(sc - mx)
            den = jnp.sum(p, axis=-1, keepdims=True) + jnp.exp(sink - mx)
            og = _dot((p * (1.0 / den)).astype(BF16), _pair(vv, g), NN)
            outs.extend(_unstack_heads(og, g))
            lg = mx + jnp.log(den)
            lses.extend([lg[h * w:(h + 1) * w] for h in range(GRP)])
        o_ref[...] = jnp.concatenate(outs, axis=-1).astype(BF16)
        lse_ref[...] = jnp.concatenate(lses, axis=-1)
        qr_ref[...] = qr.astype(BF16)
        kr_ref[...] = kr
        kprev[...] = kr
        vprev[...] = vb

    return _pallas(
        body, after=after, name=name, grid=(s // w,),
        in_specs=[pl.BlockSpec((w, ATT_W), lambda n: (n, OFF_Q // ATT_W)),
                  pl.BlockSpec((w, KV_W), lambda n: (n, OFF_K // KV_W)),
                  pl.BlockSpec((w, KV_W), lambda n: (n, OFF_VA // KV_W)),
                  pl.BlockSpec((w, LANE), lambda n: (n, 0)), pl.BlockSpec((w, LANE), lambda n: (n, 0)),
                  pl.BlockSpec((NKV, GRP * w, 1), lambda n: (0, 0, 0))],
        out_specs=[pl.BlockSpec((w, ATT_W), lambda n: (n, 0)), pl.BlockSpec((w, ATT_W), lambda n: (n, 0)),
                   pl.BlockSpec((w, KV_W), lambda n: (n, 0)), pl.BlockSpec((w, NQ), lambda n: (n, 0))],
        out_shape=[_sds((s, ATT_W), BF16), _sds((s, ATT_W), BF16), _sds((s, KV_W), BF16), _sds((s, NQ), F32)],
        scratch_shapes=[pltpu.VMEM((w, KV_W), BF16), pltpu.VMEM((w, KV_W), BF16)],
        compiler_params=_params(dimension_semantics=("arbitrary",)),
    )(proj, proj, proj, cos4, sin4, sinks)


def _gate_fwd(name, sg, attn, wa, wb, proj, b_gate, d):
    s = sg.shape[0]
    bm, bn = _tile(s, 1024), _tile(d, 512)
    off_a, off_b = OFF_GA // bn, (OFF_GA + d) // bn

    def body(sg_ref, at_ref, wa_ref, wb_ref, ga_ref, gb_ref, ba_ref, bb_ref, m_ref, ya_ref, yb_ref):
        ya = _dot(sg_ref[...], wa_ref[...], NN)
        yb = _dot(at_ref[...], wb_ref[...], NN)
        sa = _sigmoid(ga_ref[...] + ba_ref[...])
        sb = _sigmoid(gb_ref[...] + bb_ref[...])
        m_ref[...] = (sa * ya + sb * yb).astype(BF16)
        ya_ref[...] = ya.astype(BF16)
        yb_ref[...] = yb.astype(BF16)

    tile = pl.BlockSpec((bm, bn), lambda i, j: (i, j))
    return _pallas(
        body, name=name, grid=(s // bm, d // bn),
        in_specs=[pl.BlockSpec((bm, GMLP_W), lambda i, j: (i, 0)), pl.BlockSpec((bm, ATT_W), lambda i, j: (i, 0)),
                  pl.BlockSpec((GMLP_W, bn), lambda i, j: (0, j)), pl.BlockSpec((ATT_W, bn), lambda i, j: (0, j)),
                  pl.BlockSpec((bm, bn), lambda i, j: (i, off_a + j)),
                  pl.BlockSpec((bm, bn), lambda i, j: (i, off_b + j)),
                  pl.BlockSpec((1, bn), lambda i, j: (0, j)), pl.BlockSpec((1, bn), lambda i, j: (0, d // bn + j))],
        out_specs=[tile, tile, tile], out_shape=[_sds((s, d), BF16)] * 3,
        compiler_params=_params(),
    )(sg, attn, wa, wb, proj, proj, b_gate, b_gate)


def _xattn_fwd(name, xb, xf, wq, kv, wo, ln_g, ln_b, after=()):
    s, d = xf.shape
    mem = kv.shape[0]
    bm = _tile(s, 512)
    scale = XHD ** -0.5

    def body(xb_ref, xf_ref, wq_ref, kv_ref, wo_ref, g_ref, b_ref, q_out, o_out, r_out, y_out, yb_out):
        qb = _dot(xb_ref[...], wq_ref[...], NN).astype(BF16)
        kvv = kv_ref[...]
        outs = []
        for h in range(XH):
            hs = slice(h * XHD, (h + 1) * XHD)
            vs = slice(X_W + h * XHD, X_W + (h + 1) * XHD)
            sc = _dot(qb[:, hs], kvv[:, hs], NT) * scale
            mx = jnp.max(sc, axis=-1, keepdims=True)
            p = jnp.exp(sc - mx)
            p = p / jnp.sum(p, axis=-1, keepdims=True)
            outs.append(_dot(p.astype(BF16), kvv[:, vs], NN))
        ob = jnp.concatenate(outs, axis=-1).astype(BF16)
        yv = _dot(ob, wo_ref[...], NN)
        r = ALPHA * xf_ref[...] + yv
        yn, _, _ = _ln_rows(r, g_ref[...], b_ref[...])
        q_out[...] = qb
        o_out[...] = ob
        r_out[...] = r
        y_out[...] = yn
        yb_out[...] = yn.astype(BF16)

    row = lambda wd: pl.BlockSpec((bm, wd), lambda i: (i, 0))
    return _pallas(
        body, after=after, name=name, grid=(s // bm,),
        in_specs=[row(d), row(d), pl.BlockSpec((d, X_W), lambda i: (0, 0)),
                  pl.BlockSpec((mem, 2 * X_W), lambda i: (0, 0)), pl.BlockSpec((X_W, d), lambda i: (0, 0)),
                  pl.BlockSpec((1, d), lambda i: (0, 0)), pl.BlockSpec((1, d), lambda i: (0, 0))],
        out_specs=[row(X_W), row(X_W), row(d), row(d), row(d)],
        out_shape=[_sds((s, X_W), BF16), _sds((s, X_W), BF16), _sds((s, d), F32), _sds((s, d), F32),
                   _sds((s, d), BF16)],
        compiler_params=_params(),
    )(xb, xf, wq, kv, wo, ln_g, ln_b)


def _loss_grad(name, y, tgt):
    s, d = y.shape
    bm = _tile(s, 512)

    def body(y_ref, t_ref, dy_ref, loss_ref):
        i = pl.program_id(0)
        err = y_ref[...] - t_ref[...]
        dy_ref[...] = err * (1.0 / d)
        part = 0.5 * jnp.sum(jnp.sum(err * err, axis=-1, keepdims=True), axis=0, keepdims=True) * (1.0 / d)

        @pl.when(i == 0)
        def _():
            loss_ref[...] = part

        @pl.when(i > 0)
        def _():
            loss_ref[...] += part

    row = pl.BlockSpec((bm, d), lambda i: (i, 0))
    return _pallas(
        body, name=name, grid=(s // bm,), in_specs=[row, row],
        out_specs=[row, pl.BlockSpec((1, 1), lambda i: (0, 0))],
        out_shape=[_sds((s, d), F32), _sds((1, 1), F32)],
        compiler_params=_params(dimension_semantics=("arbitrary",)),
    )(y, tgt)


def _ln_bwd(name, dy, r, g, after=()):
    s, d = r.shape
    bm = _tile(s, 256)

    def body(dy_ref, r_ref, g_ref, dr_ref, drb_ref, dg_ref, db_ref):
        i = pl.program_id(0)
        dyv = dy_ref[...]
        _, xhat, rstd = _ln_rows(r_ref[...], g_ref[...], 0.0)
        dxh = dyv * g_ref[...]
        m1 = jnp.mean(dxh, axis=-1, keepdims=True)
        m2 = jnp.mean(dxh * xhat, axis=-1, keepdims=True)
        dr = rstd * (dxh - m1 - xhat * m2)
        dr_ref[...] = dr
        drb_ref[...] = dr.astype(BF16)
        dg = jnp.sum(dyv * xhat, axis=0, keepdims=True)
        db = jnp.sum(dyv, axis=0, keepdims=True)

        @pl.when(i == 0)
        def _():
            dg_ref[...] = dg
            db_ref[...] = db

        @pl.when(i > 0)
        def _():
            dg_ref[...] += dg
            db_ref[...] += db

    row = pl.BlockSpec((bm, d), lambda i: (i, 0))
    vec = pl.BlockSpec((1, d), lambda i: (0, 0))
    return _pallas(
        body, after=after, name=name, grid=(s // bm,), in_specs=[row, row, vec], out_specs=[row, row, vec, vec],
        out_shape=[_sds((s, d), F32), _sds((s, d), BF16), _sds((1, d), F32), _sds((1, d), F32)],
        compiler_params=_params(dimension_semantics=("arbitrary",)),
    )(dy, r, g)


def _xattn_bwd(name, dyb, drf, q, kv, wo, wq):
    s, d = drf.shape
    mem = kv.shape[0]
    bm = _tile(s, 512)
    scale = XHD ** -0.5

    def body(dy_ref, dr_ref, q_ref, kv_ref, wo_ref, wq_ref, dx_out, dq_out, dkv_out):
        i = pl.program_id(0)
        dob = _dot(dy_ref[...], wo_ref[...], NT).astype(BF16)
        qb = q_ref[...]
        kvv = kv_ref[...]
        dqs, dks, dvs = [], [], []
        for h in range(XH):
            hs = slice(h * XHD, (h + 1) * XHD)
            vs = slice(X_W + h * XHD, X_W + (h + 1) * XHD)
            sc = _dot(qb[:, hs], kvv[:, hs], NT) * scale
            mx = jnp.max(sc, axis=-1, keepdims=True)
            p = jnp.exp(sc - mx)
            p = p / jnp.sum(p, axis=-1, keepdims=True)
            dp = _dot(dob[:, hs], kvv[:, vs], NT)
            dsum = jnp.sum(p * dp, axis=-1, keepdims=True)
            dsb = (p * (dp - dsum) * scale).astype(BF16)
            dqs.append(_dot(dsb, kvv[:, hs], NN))
            dks.append(_dot(dsb, qb[:, hs], TN))
            dvs.append(_dot(p.astype(BF16), dob[:, hs], TN))
        dqb = jnp.concatenate(dqs, axis=-1).astype(BF16)
        dq_out[...] = dqb
        dx_out[...] = _dot(dqb, wq_ref[...], NT) + ALPHA * dr_ref[...]
        dkv = jnp.concatenate(dks + dvs, axis=-1)

        @pl.when(i == 0)
        def _():
            dkv_out[...] = dkv

        @pl.when(i > 0)
        def _():
            dkv_out[...] += dkv

    row = lambda wd: pl.BlockSpec((bm, wd), lambda i: (i, 0))
    return _pallas(
        body, name=name, grid=(s // bm,),
        in_specs=[row(d), row(d), row(X_W), pl.BlockSpec((mem, 2 * X_W), lambda i: (0, 0)),
                  pl.BlockSpec((X_W, d), lambda i: (0, 0)), pl.BlockSpec((d, X_W), lambda i: (0, 0))],
        out_specs=[row(d), row(X_W), pl.BlockSpec((mem, 2 * X_W), lambda i: (0, 0))],
        out_shape=[_sds((s, d), F32), _sds((s, X_W), BF16), _sds((mem, 2 * X_W), F32)],
        compiler_params=_params(dimension_semantics=("arbitrary",)),
    )(dyb, drf, q, kv, wo, wq)


def _gate_bwd(name, dr1b, w_o, proj, ya, yb, b_gate, d, after=()):
    s = dr1b.shape[0]
    bm, bn = _tile(s, 1024), _tile(d, 512)
    off_a, off_b = OFF_GA // bn, (OFF_GA + d) // bn
    nj = d // bn

    def body(a_ref, w_ref, ga_ref, gb_ref, ya_ref, yb_ref, ba_ref, bb_ref, dya_ref, dyb_ref, dg_ref, dba_ref, dbb_ref):
        i = pl.program_id(1)
        dm = _dot(a_ref[...], w_ref[...], NT)
        sa = _sigmoid(ga_ref[...] + ba_ref[...])
        sb = _sigmoid(gb_ref[...] + bb_ref[...])
        dya_ref[...] = (dm * sa).astype(BF16)
        dyb_ref[...] = (dm * sb).astype(BF16)
        dga = dm * ya_ref[...].astype(F32) * (sa * (1.0 - sa))
        dgb = dm * yb_ref[...].astype(F32) * (sb * (1.0 - sb))
        dg_ref[0] = dga.astype(BF16)
        dg_ref[1] = dgb.astype(BF16)
        sa_sum = jnp.sum(dga, axis=0, keepdims=True)
        sb_sum = jnp.sum(dgb, axis=0, keepdims=True)

        @pl.when(i == 0)
        def _():
            dba_ref[...] = sa_sum
            dbb_ref[...] = sb_sum

        @pl.when(i > 0)
        def _():
            dba_ref[...] += sa_sum
            dbb_ref[...] += sb_sum

    tile = pl.BlockSpec((bm, bn), lambda j, i: (i, j))
    return _pallas(
        body, after=after, name=name, grid=(nj, s // bm),
        in_specs=[pl.BlockSpec((bm, d), lambda j, i: (i, 0)),
                  pl.BlockSpec((bn, d), lambda j, i: (j, 0)),
                  pl.BlockSpec((bm, bn), lambda j, i: (i, off_a + j)),
                  pl.BlockSpec((bm, bn), lambda j, i: (i, off_b + j)),
                  tile, tile,
                  pl.BlockSpec((1, bn), lambda j, i: (0, j)), pl.BlockSpec((1, bn), lambda j, i: (0, nj + j))],
        out_specs=[tile, tile, pl.BlockSpec((2, bm, bn), lambda j, i: (0, i, j)),
                   pl.BlockSpec((1, bn), lambda j, i: (0, j)), pl.BlockSpec((1, bn), lambda j, i: (0, j))],
        out_shape=[_sds((s, d), BF16), _sds((s, d), BF16), _sds((2, s, d), BF16), _sds((1, d), F32),
                   _sds((1, d), F32)],
        compiler_params=_params(dimension_semantics=("arbitrary", "arbitrary")),
    )(dr1b, w_o, proj, proj, ya, yb, b_gate, b_gate)


def _gmlp_bwd(name, proj, dsg, ln_g, ln_b, w_s, b_st):
    s = proj.shape[0]

    def body(u_ref, v_ref, dsg_ref, g_ref, b_ref, ws_ref, bst_ref, duv_ref, dws_ref, dbst_ref, dlg_ref, dlb_ref):
        n = pl.program_id(0)
        u, v = u_ref[...], v_ref[...]
        gu, tu = _gelu(u)
        gv, tv = _gelu(v)
        gam = g_ref[...]
        vn, xhat, rstd = _ln_rows(gv, gam, b_ref[...])
        vnb = vn.astype(BF16)
        dsg = dsg_ref[...].astype(F32)
        row = lax.broadcasted_iota(jnp.int32, (CHUNK, CHUNK), 0)
        col = lax.broadcasted_iota(jnp.int32, (CHUNK, CHUNK), 1)
        tril = col <= row
        dgu, dvn, dws, dbs = [], [], [], []
        for g in range(GROUPS):
            sl = slice(g * LANE, (g + 1) * LANE)
            w = jnp.where(tril, ws_ref[g], 0.0).astype(BF16)
            mixed = _dot(w, vnb[:, sl], NN) + bst_ref[:, g:g + 1]
            dgu.append(dsg[:, sl] * mixed)
            dmx = dsg[:, sl] * gu[:, sl]
            dmxb = dmx.astype(BF16)
            dbs.append(jnp.sum(dmx, axis=-1, keepdims=True))
            dws.append(jnp.where(tril, _dot(dmxb, vnb[:, sl], NT), 0.0))
            dvn.append(_dot(w, dmxb, TN))
        dvn = jnp.concatenate(dvn, axis=-1)
        dgu = jnp.concatenate(dgu, axis=-1)
        dxh = dvn * gam
        m1 = jnp.mean(dxh, axis=-1, keepdims=True)
        m2 = jnp.mean(dxh * xhat, axis=-1, keepdims=True)
        dgv = rstd * (dxh - m1 - xhat * m2)
        du = dgu * _gelu_grad(u, tu)
        dv = dgv * _gelu_grad(v, tv)
        duv_ref[...] = jnp.concatenate([du, dv], axis=-1).astype(BF16)
        dlg = jnp.sum(dvn * xhat, axis=0, keepdims=True)
        dlb = jnp.sum(dvn, axis=0, keepdims=True)
        dbst = jnp.concatenate(dbs, axis=-1)

        @pl.when(n == 0)
        def _():
            for g in range(GROUPS):
                dws_ref[g] = dws[g]
            dbst_ref[...] = dbst
            dlg_ref[...] = dlg
            dlb_ref[...] = dlb

        @pl.when(n > 0)
        def _():
            for g in range(GROUPS):
                dws_ref[g] += dws[g]
            dbst_ref[...] += dbst
            dlg_ref[...] += dlg
            dlb_ref[...] += dlb

    vec = pl.BlockSpec((1, GMLP_W), lambda n: (0, 0))
    return _pallas(
        body, name=name, grid=(s // CHUNK,),
        in_specs=[pl.BlockSpec((CHUNK, GMLP_W), lambda n: (n, 0)), pl.BlockSpec((CHUNK, GMLP_W), lambda n: (n, 1)),
                  pl.BlockSpec((CHUNK, GMLP_W), lambda n: (n, 0)), vec, vec,
                  pl.BlockSpec((GROUPS, CHUNK, CHUNK), lambda n: (0, 0, 0)),
                  pl.BlockSpec((CHUNK, GROUPS), lambda n: (0, 0))],
        out_specs=[pl.BlockSpec((CHUNK, 2 * GMLP_W), lambda n: (n, 0)),
                   pl.BlockSpec((GROUPS, CHUNK, CHUNK), lambda n: (0, 0, 0)),
                   pl.BlockSpec((CHUNK, GROUPS), lambda n: (0, 0)), vec, vec],
        out_shape=[_sds((s, 2 * GMLP_W), BF16), _sds((GROUPS, CHUNK, CHUNK), F32), _sds((CHUNK, GROUPS), F32),
                   _sds((1, GMLP_W), F32), _sds((1, GMLP_W), F32)],
        compiler_params=_params(dimension_semantics=("arbitrary",)),
    )(proj, proj, dsg, ln_g, ln_b, w_s, b_st)


def _swa_bwd(name, qr, kr, proj, do, o, lse, sinks, cos4, nsin4, after=()):
    s = qr.shape[0]
    w = CHUNK
    nblk = s // w
    scale = HD ** -0.5
    grp = NQ // NKV

    def body(qj_ref, qn_ref, kj_ref, kp_ref, vj_ref, vp_ref, doj_ref, don_ref, oj_ref, on_ref, lj_ref, ln_ref,
             sink_ref, cos_ref, sin_ref, out_ref, dsink_ref):
        j = pl.program_id(0)
        qj, qn = qj_ref[...].astype(F32), qn_ref[...].astype(F32)
        doj, don = doj_ref[...].astype(F32), don_ref[...].astype(F32)
        kk = jnp.concatenate([kp_ref[...], kj_ref[...]], axis=0)
        vv = jnp.concatenate([vp_ref[...], vj_ref[...]], axis=0).astype(BF16)
        lj, lnx = lj_ref[...], ln_ref[...]
        dsum_j = _head_sums(doj * oj_ref[...].astype(F32))
        dsum_n = _head_sums(don * on_ref[...].astype(F32))
        valid_j = _band_mask(j > 0)
        valid_n = _band_mask(j + 1 < nblk, prev_only=True)
        lo = lax.broadcasted_iota(jnp.int32, (w, LANE), 1) < HD
        dqs, dsk, dk_g, dv_g = [], [], [], []
        for g in range(NKV):
            kz, vz = _own_head(kk, g), _own_head(vv, g)
            kz_c, vz_c = kz[w:], vz[w:]
            qg_j, qg_n = _stack_heads(qj, g), _stack_heads(qn, g)
            dog_j, dog_n = _stack_heads(doj, g), _stack_heads(don, g)
            l_j, l_n = _stack_cols(lj, g), _stack_cols(lnx, g)
            d_j, d_n = _stack_cols(dsum_j, g), _stack_cols(dsum_n, g)
            p = jnp.where(valid_j, jnp.exp(_dot(qg_j, kz, NT) * scale - l_j), 0.0)
            ds = (p * (_dot(dog_j, vz, NT) - d_j) * scale).astype(BF16)
            dqs.extend(_unstack_heads(_dot(ds, kz, NN), g))
            p2 = jnp.where(valid_n, jnp.exp(_dot(qg_n, kz_c, NT) * scale - l_n), 0.0)
            ds2 = (p2 * (_dot(dog_n, vz_c, NT) - d_n) * scale).astype(BF16)
            dk_g.append(_dot(ds[:, w:], qg_j, TN) + _dot(ds2, qg_n, TN))
            dv_g.append(_dot(p[:, w:].astype(BF16), dog_j, TN) + _dot(p2.astype(BF16), dog_n, TN))
            t = jnp.exp(sink_ref[g] - l_j) * d_j
            dsk.extend([-jnp.sum(t[h * w:(h + 1) * w], axis=0, keepdims=True) for h in range(GRP)])
        cos, nsin = cos_ref[...], sin_ref[...]
        dq = _rope(jnp.concatenate(dqs, axis=-1), cos, nsin)
        dk = _rope(jnp.concatenate([jnp.where(lo, dk_g[2 * m], dk_g[2 * m + 1]) for m in range(NKV // 2)], axis=-1),
                   cos, nsin)
        dv = jnp.concatenate([jnp.where(lo, dv_g[2 * m], dv_g[2 * m + 1]) for m in range(NKV // 2)], axis=-1)
        out_ref[...] = jnp.concatenate([dq, dk, dv], axis=-1).astype(BF16)
        dsink = jnp.concatenate(dsk, axis=-1)

        @pl.when(j == 0)
        def _():
            dsink_ref[...] = dsink

        @pl.when(j > 0)
        def _():
            dsink_ref[...] += dsink

    nxt = lambda j: jnp.minimum(j + 1, nblk - 1)
    prv = lambda j: jnp.maximum(j - 1, 0)
    va = OFF_VA // KV_W
    return _pallas(
        body, after=after, name=name, grid=(nblk,),
        in_specs=[pl.BlockSpec((w, ATT_W), lambda j: (j, 0)), pl.BlockSpec((w, ATT_W), lambda j: (nxt(j), 0)),
                  pl.BlockSpec((w, KV_W), lambda j: (j, 0)), pl.BlockSpec((w, KV_W), lambda j: (prv(j), 0)),
                  pl.BlockSpec((w, KV_W), lambda j: (j, va)), pl.BlockSpec((w, KV_W), lambda j: (prv(j), va)),
                  pl.BlockSpec((w, ATT_W), lambda j: (j, 0)), pl.BlockSpec((w, ATT_W), lambda j: (nxt(j), 0)),
                  pl.BlockSpec((w, ATT_W), lambda j: (j, 0)), pl.BlockSpec((w, ATT_W), lambda j: (nxt(j), 0)),
                  pl.BlockSpec((w, NQ), lambda j: (j, 0)), pl.BlockSpec((w, NQ), lambda j: (nxt(j), 0)),
                  pl.BlockSpec((NKV, GRP * w, 1), lambda j: (0, 0, 0)),
                  pl.BlockSpec((w, LANE), lambda j: (j, 0)), pl.BlockSpec((w, LANE), lambda j: (j, 0))],
        out_specs=[pl.BlockSpec((w, ATT_W + 2 * KV_W), lambda j: (j, 0)), pl.BlockSpec((1, NQ), lambda j: (0, 0))],
        out_shape=[_sds((s, ATT_W + 2 * KV_W), BF16), _sds((1, NQ), F32)],
        compiler_params=_params(dimension_semantics=("arbitrary",)),
    )(qr, qr, kr, kr, proj, proj, do, do, o, o, lse, lse, sinks, cos4, nsin4)


def _mm_nn(name, a, w, *, out_dtypes, epilogue=_store, bm_pref=1024, bn_pref=1024, after=()):
    m, k = a.shape
    n = w.shape[-1]
    bm, bn = _tile(m, bm_pref), _tile(n, bn_pref)
    tile = pl.BlockSpec((bm, bn), lambda i, j, kk: (i, j))
    return _mm(name, a, w, dims=NN, grid=(m // bm, n // bn, 1),
               a_spec=pl.BlockSpec((bm, k), lambda i, j, kk: (i, 0)),
               b_spec=pl.BlockSpec((k, bn), lambda i, j, kk: (0, j)),
               out_shape=[_sds((m, n), dt) for dt in out_dtypes], out_specs=[tile] * len(out_dtypes),
               epilogue=epilogue, after=after)


def _mm_tn(name, a, b, *, bm_pref=1024, bn_pref=1024, after=()):
    s, m = a.shape
    n = b.shape[-1]
    bm, bn = _tile(m, bm_pref), _tile(n, bn_pref)
    return _mm(name, a, b, dims=TN, grid=(m // bm, n // bn, 1),
               a_spec=pl.BlockSpec((s, bm), lambda i, j, kk: (0, i)),
               b_spec=pl.BlockSpec((s, bn), lambda i, j, kk: (0, j)),
               out_shape=[_sds((m, n), BF16)], out_specs=[pl.BlockSpec((bm, bn), lambda i, j, kk: (i, j))],
               epilogue=_store, after=after)[0]


class _Gather:
    def __init__(self, tag, layer, names, shards, fulls, after):
        self.tag, self.names = tag, names
        self.axes = [SHARD_AXIS[n] for n in names]
        self.srcs = [shards[n] for n in names]
        self.mk1 = _mk_gather_ici(layer, self.axes)
        self.mk2 = _mk_gather_d2d(self.axes)
        self.n_sem = 3 * len(names)
        self.s1, self.r1, self.lands, self.token = _split_start(
            tag + "_ici_start", self.srcs, [fulls[n] for n in names], self.mk1, self.n_sem, after)

    def forward(self, after=()):
        lands = _split_wait(self.tag + "_ici_wait", self.srcs, self.lands, self.s1, self.r1, self.mk1, after)
        self.s2, self.r2, self.lands, tok = _split_start(self.tag + "_d2d_start", [], lands, self.mk2, self.n_sem)
        return tok

    def done(self, after=()):
        lands = _split_wait(self.tag + "_d2d_wait", [], self.lands, self.s2, self.r2, self.mk2, after)
        return dict(zip(self.names, lands))


class _Reduce:
    def __init__(self, tag, names, parts, cidx, mcidx, after=()):
        self.tag, self.names, self.cidx, self.mcidx = tag, names, cidx, mcidx
        self.axes = [SHARD_AXIS[n] for n in names]
        self.parts = [parts[n] for n in names]
        self.mk = _mk_swap(self.axes)
        lands = []
        for p, ax in zip(self.parts, self.axes):
            k, n = p.shape
            lands.append(lax.empty((k, n // 2) if ax == 0 else (k // 2, n), BF16))
        self.s, self.r, self.lands, self.token = _split_start(
            tag + "_swap_start", self.parts, lands, self.mk, len(names), after)

    def scatter(self, after=()):
        got = _split_wait(self.tag + "_swap_wait", self.parts, self.lands, self.s, self.r, self.mk, after)
        self.sums = [_add_half(f"{self.tag}_add_{n}", p, g, ax, self.cidx)
                     for n, p, g, ax in zip(self.names, self.parts, got, self.axes)]
        self.mk = _mk_scatter(self.axes)
        lands = []
        for q, ax in zip(self.sums, self.axes):
            k, n = q.shape
            lands.append(lax.empty((3, k // 4, n) if ax == 0 else (3, k, n // 4), BF16))
        self.s, self.r, self.lands, tok = _split_start(
            self.tag + "_scatter_start", self.sums, lands, self.mk, 3 * len(self.names))
        return tok

    def exchange(self, after=()):
        slots = _split_wait(self.tag + "_scatter_wait", self.sums, self.lands, self.s, self.r, self.mk, after)
        halves = [_sum_half(f"{self.tag}_sum_{n}", q, sl, ax, self.mcidx)
                  for n, q, sl, ax in zip(self.names, self.sums, slots, self.axes)]
        self.mk = _mk_exchange(self.axes)
        self.s, self.r, self.lands, tok = _split_start(
            self.tag + "_exchange_start", [], halves, self.mk, len(self.names))
        return tok

    def done(self, after=()):
        grads = _split_wait(self.tag + "_exchange_wait", [], self.lands, self.s, self.r, self.mk, after)
        return dict(zip(self.names, grads))


def _pack(arrs):
    flat = jnp.concatenate([a.reshape(-1) for a in arrs])
    n = flat.shape[0]
    pad = (-n) % (8 * LANE)
    return jnp.pad(flat, (0, pad)).reshape(-1, LANE)


def _unpack(packed, shapes):
    flat = packed.reshape(-1)
    out, off = [], 0
    for sh in shapes:
        n = math.prod(sh)
        out.append(flat[off:off + n].reshape(sh))
        off += n
    return out


def kernel(x, mem, w_in, b_gate, ln_v_g, ln_v_b, w_s, b_s, sinks, w_br_a, w_br_b, w_o, ln1_g, ln1_b, w_xq, w_xkv, w_xo, ln2_g, ln2_b, w_up, w_down, ln3_g, ln3_b, loss_target, m_w_in, m_b_gate, m_ln_v_g, m_ln_v_b, m_w_s, m_b_s, m_sinks, m_w_br_a, m_w_br_b, m_w_o, m_ln1_g, m_ln1_b, m_w_xq, m_w_xkv, m_w_xo, m_ln2_g, m_ln2_b, m_w_up, m_w_down, m_ln3_g, m_ln3_b, v_w_in, v_b_gate, v_ln_v_g, v_ln_v_b, v_w_s, v_b_s, v_sinks, v_w_br_a, v_w_br_b, v_w_o, v_ln1_g, v_ln1_b, v_w_xq, v_w_xkv, v_w_xo, v_ln2_g, v_ln2_b, v_w_up, v_w_down, v_ln3_g, v_ln3_b):
    env = dict(locals())
    wts = {n: env[n] for n in WEIGHTS}
    mom_m = {n: env["m_" + n] for n in WEIGHTS}
    mom_v = {n: env["v_" + n] for n in WEIGHTS}
    s, d = x.shape[1], x.shape[2]
    dff = 4 * w_up.shape[-1]
    iw = 4 * w_in.shape[-1]
    xf = x.reshape(s, d)
    tgt = loss_target.reshape(s, d)
    memf = mem.reshape(mem.shape[1], d)
    ax_x, ax_y, ax_c = lax.axis_index("x"), lax.axis_index("y"), lax.axis_index("c")
    meidx = jnp.reshape(2 * ax_x + ax_y, (1,)).astype(jnp.int32)
    cidx = jnp.reshape(ax_c, (1,)).astype(jnp.int32)
    mcidx = jnp.concatenate([meidx, cidx])

    inv = 1.0 / (10000.0 ** (jnp.arange(0, HD, 2, dtype=F32) / HD))
    ang = jnp.arange(s, dtype=F32)[:, None] * inv[None, :]
    cos, sin = jnp.cos(ang), jnp.sin(ang)
    cos4 = jnp.tile(cos, (1, 4))
    sin4 = jnp.concatenate([-sin, sin, -sin, sin], axis=-1)
    nsin4 = -sin4

    small = {}
    for n in SMALL:
        w = wts[n]
        if n == "w_s":
            small[n] = [w[l] for l in range(DEPTH)]
        elif n == "b_s":
            small["b_st"] = [w[l].T for l in range(DEPTH)]
        else:
            small[n] = [w[l][None, :] for l in range(DEPTH)]
    small["sink_rows"] = [jnp.repeat(sinks[l].reshape(NKV, GRP), CHUNK, axis=1)[..., None] for l in range(DEPTH)]

    shards, fulls = {}, [{}, {}]
    tok = ()
    gathers = [[None] * len(GROUPS_FWD) for _ in range(DEPTH)]
    for gi, names in enumerate(GROUPS_FWD):
        for n in names:
            shards[n] = _cast_bf16("cast_" + n, wts[n], after=tok)
            fulls[0][n], fulls[1][n] = _place_own("place_" + n, shards[n], SHARD_AXIS[n], meidx)
        gathers[0][gi] = _Gather(f"ag0_{gi}", 0, names, shards, fulls[0], tok)
        tok = (gathers[0][gi].token,)
    for gi, names in enumerate(GROUPS_FWD):
        gathers[1][gi] = _Gather(f"ag1_{gi}", 1, names, shards, fulls[1], tok)
        tok = (gathers[1][gi].token,)

    xb = _cast2d("cast_x", xf)
    memb = _cast2d("cast_mem", memf)

    saved = []
    hf, hb = xf, xb
    nxt_tok = gathers[0][0].forward(after=tok)
    for l in range(DEPTH):
        t = f"l{l}_"
        ga, gb, gc = gathers[l]
        full = ga.done(after=(nxt_tok, hb))
        sv = {"xf": hf, "xb": hb}
        proj = _mm_nn(t + "proj", hb, full["w_in"], out_dtypes=[F32], bn_pref=1280)[0]
        tok_b = gb.forward(after=(proj,))
        sg = _gmlp_fwd(t + "gmlp_fwd", proj, small["ln_v_g"][l], small["ln_v_b"][l], small["w_s"][l],
                       small["b_st"][l])
        attn, qr, kr, lse = _swa_fwd(t + "swa_fwd", proj, cos4, sin4, small["sink_rows"][l], after=(tok_b,))
        full.update(gb.done(after=(attn,)))
        merged, ya, yb = _gate_fwd(t + "gate_fwd", sg, attn, full["w_br_a"], full["w_br_b"], proj,
                                   small["b_gate"][l], d)
        tok_c = gc.forward(after=(merged,))
        bm = _tile(s, 512)
        row = pl.BlockSpec((bm, d), lambda i, j, k: (i, 0))
        r1, x1, x1b = _mm(
            t + "o_ln", merged, full["w_o"], dims=NN, grid=(s // bm, 1, 1),
            a_spec=row, b_spec=pl.BlockSpec((d, d), lambda i, j, k: (0, 0)),
            extras=(hf, small["ln1_g"][l], small["ln1_b"][l]), extra_specs=(row, _vec_spec(d), _vec_spec(d)),
            out_shape=[_sds((s, d), F32), _sds((s, d), F32), _sds((s, d), BF16)], out_specs=[row] * 3,
            epilogue=_ep_residual_ln, after=(tok_c,))
        kv = _mm_nn(t + "xkv", memb, full["w_xkv"], out_dtypes=[BF16])[0]
        q, o, r2, x2, x2b = _xattn_fwd(t + "xattn_fwd", x1b, x1, full["w_xq"], kv, full["w_xo"],
                                       small["ln2_g"][l], small["ln2_b"][l])
        full.update(gc.done(after=(x2b,)))

        def ep_up(acc, ex, outs):
            outs[0][...] = acc.astype(BF16)
            rl = jnp.maximum(acc, 0.0)
            outs[1][...] = (rl * rl).astype(BF16)

        h, a = _mm_nn(t + "up", x2b, full["w_up"], out_dtypes=[BF16, BF16], epilogue=ep_up)
        nxt_tok = gathers[l + 1][0].forward(after=(h,)) if l + 1 < DEPTH else None
        bk = _tile(dff, 1024)
        r3, x3, x3b = _mm(
            t + "down_ln", a, full["w_down"], dims=NN, grid=(s // bm, 1, dff // bk),
            a_spec=pl.BlockSpec((bm, bk), lambda i, j, k: (i, k)),
            b_spec=pl.BlockSpec((bk, d), lambda i, j, k: (k, 0)),
            extras=(x2, small["ln3_g"][l], small["ln3_b"][l]), extra_specs=(row, _vec_spec(d), _vec_spec(d)),
            out_shape=[_sds((s, d), F32), _sds((s, d), F32), _sds((s, d), BF16)], out_specs=[row] * 3,
            epilogue=_ep_residual_ln, acc_shape=(bm, d), after=() if nxt_tok is None else (nxt_tok,))
        sv.update(proj=proj, sg=sg, attn=attn, qr=qr, kr=kr, lse=lse, merged=merged, ya=ya, yb=yb, r1=r1, x1=x1,
                  x1b=x1b, kv=kv, q=q, o=o, r2=r2, x2b=x2b, h=h, a=a, r3=r3, full=full)
        saved.append(sv)
        hf, hb = x3, x3b
    dy, loss11 = _loss_grad("loss", hf, tgt)
    loss = lax.psum(loss11[0, 0], ("x", "y", "c"))

    small_g = [None] * DEPTH
    grads = [{}, {}]
    pend_a = None
    pend_b = None
    g = dy
    for l in reversed(range(DEPTH)):
        t = f"l{l}_"
        sv = saved[l]
        full = sv["full"]
        dw, sgo = {}, {}
        dr3, dr3b, sgo["ln3_g"], sgo["ln3_b"] = _ln_bwd(t + "ln3_bwd", g, sv["r3"], small["ln3_g"][l],
                                                        after=() if pend_a is None else (tok_a,))
        bm, bn = _tile(s, 1024), _tile(dff, 1024)

        def ep_dh(acc, ex, outs):
            outs[0][...] = (acc * (2.0 * jnp.maximum(ex[0][...].astype(F32), 0.0))).astype(BF16)

        tile = pl.BlockSpec((bm, bn), lambda i, j, k: (i, j))
        dh = _mm(t + "dh", dr3b, full["w_down"], dims=NT, grid=(s // bm, dff // bn, 1),
                 a_spec=pl.BlockSpec((bm, d), lambda i, j, k: (i, 0)),
                 b_spec=pl.BlockSpec((bn, d), lambda i, j, k: (j, 0)),
                 extras=(sv["h"],), extra_specs=(tile,), out_shape=[_sds((s, dff), BF16)], out_specs=[tile],
                 epilogue=ep_dh)[0]
        if pend_a is not None:
            tok_pa = pend_a.exchange(after=(dh,))
            grads[l + 1].update(pend_b.done(after=(dh,)))
        dw["w_down"] = _mm_tn(t + "dw_down", sv["a"], dr3b, after=() if pend_a is None else (tok_pa,))
        dw["w_up"] = _mm_tn(t + "dw_up", sv["x2b"], dh)
        red_c = _Reduce(t + "rs_c", GROUPS_FWD[2], dw, cidx, mcidx)
        bm2, bn2 = _tile(s, 512), _tile(d, 512)
        tile2 = pl.BlockSpec((bm2, bn2), lambda i, j, k: (i, j))
        dx2 = _mm(t + "dx2", dh, full["w_up"], dims=NT, grid=(s // bm2, d // bn2, 1),
                  a_spec=pl.BlockSpec((bm2, dff), lambda i, j, k: (i, 0)),
                  b_spec=pl.BlockSpec((bn2, dff), lambda i, j, k: (j, 0)),
                  extras=(dr3,), extra_specs=(tile2,), out_shape=[_sds((s, d), F32)], out_specs=[tile2],
                  epilogue=_ep_add_scaled, after=(red_c.token,))[0]
        tok_c = red_c.scatter(after=(dx2,))
        if pend_a is not None:
            grads[l + 1].update(pend_a.done(after=(dx2,)))
            pend_a = None

        dr2, dr2b, sgo["ln2_g"], sgo["ln2_b"] = _ln_bwd(t + "ln2_bwd", dx2, sv["r2"], small["ln2_g"][l],
                                                        after=(tok_c,))
        dx1, dq, dkv = _xattn_bwd(t + "xattn_bwd", dr2b, dr2, sv["q"], sv["kv"], full["w_xo"], full["w_xq"])
        dw["w_xo"] = _mm_tn(t + "dw_xo", sv["o"], dr2b)
        dw["w_xq"] = _mm_tn(t + "dw_xq", sv["x1b"], dq)
        dw["w_xkv"] = _mm_tn(t + "dw_xkv", memb, _cast2d(t + "dkv_cast", dkv))

        dr1, dr1b, sgo["ln1_g"], sgo["ln1_b"] = _ln_bwd(t + "ln1_bwd", dx1, sv["r1"], small["ln1_g"][l])
        dya, dyb, dgate, dba, dbb = _gate_bwd(t + "gate_bwd", dr1b, full["w_o"], sv["proj"], sv["ya"], sv["yb"],
                                              small["b_gate"][l], d)
        sgo["b_gate"] = jnp.concatenate([dba, dbb], axis=-1)
        tok_c = red_c.exchange(after=(dya,))
        dw["w_o"] = _mm_tn(t + "dw_o", sv["merged"], dr1b, after=(tok_c,))
        dw["w_br_a"] = _mm_tn(t + "dw_br_a", sv["sg"], dya)
        dw["w_br_b"] = _mm_tn(t + "dw_br_b", sv["attn"], dyb)
        red_b = _Reduce(t + "rs_b", GROUPS_FWD[1], dw, cidx, mcidx)

        def dbranch(name, dyx, w, after):
            bk2 = _tile(d, 1024)
            return _mm(name, dyx, w, dims=NT, grid=(s // bm, 1, d // bk2),
                       a_spec=pl.BlockSpec((bm, bk2), lambda i, j, k: (i, k)),
                       b_spec=pl.BlockSpec((w.shape[0], bk2), lambda i, j, k: (0, k)),
                       out_shape=[_sds((s, w.shape[0]), BF16)],
                       out_specs=[pl.BlockSpec((bm, w.shape[0]), lambda i, j, k: (i, 0))],
                       epilogue=_store, acc_shape=(bm, w.shape[0]), after=after)[0]

        dsg = dbranch(t + "dsg", dya, full["w_br_a"], (red_b.token,))
        dattn = dbranch(t + "dattn", dyb, full["w_br_b"], ())
        tok_b = red_b.scatter(after=(dattn,))
        grads[l].update(red_c.done(after=(dattn,)))
        duv, sgo["w_s"], dbst, dlg, dlb = _gmlp_bwd(t + "gmlp_bwd", sv["proj"], dsg, small["ln_v_g"][l],
                                                    small["ln_v_b"][l], small["w_s"][l], small["b_st"][l])
        sgo["b_s"] = dbst.T
        sgo["ln_v_g"], sgo["ln_v_b"] = dlg, dlb
        dqkv, sgo["sinks"] = _swa_bwd(t + "swa_bwd", sv["qr"], sv["kr"], sv["proj"], dattn, sv["attn"], sv["lse"],
                                      small["sink_rows"][l], cos4, nsin4, after=(tok_b,))
        dproj = jnp.concatenate([duv, dqkv, dgate[0], dgate[1]], axis=-1)
        tok_b = red_b.exchange(after=(dproj,))
        dw["w_in"] = _mm_tn(t + "dw_in", sv["xb"], dproj, after=(tok_b,))
        red_a = _Reduce(t + "rs_a", GROUPS_FWD[0], dw, cidx, mcidx)
        g = _mm(t + "dx0", dproj, full["w_in"], dims=NT, grid=(s // bm2, d // bn2, 1),
                a_spec=pl.BlockSpec((bm2, iw), lambda i, j, k: (i, 0)),
                b_spec=pl.BlockSpec((bn2, iw), lambda i, j, k: (j, 0)),
                extras=(dr1,), extra_specs=(tile2,), out_shape=[_sds((s, d), F32)], out_specs=[tile2],
                epilogue=_ep_add_scaled, after=(red_a.token,))[0]
        tok_a = red_a.scatter(after=(g,))
        pend_a, pend_b = red_a, red_b
        small_g[l] = sgo
    grad_x = g.reshape(x.shape)

    big_out = {}

    def adam_layer(l, names, after):
        done = []
        for n in names:
            prev = big_out.get(n)
            big_out[n] = _adamw(f"adamw{l}_{n}", wts[n], grads[l][n], mom_m[n], mom_v[n], l, prev, after=after)
            done.append(big_out[n][0])
        return done

    shapes = [wts[n].shape for n in SMALL]
    packed_g = _pack([jnp.stack([small_g[l][n].reshape(wts[n].shape[1:]) for l in range(DEPTH)]) for n in SMALL])
    packed_g = _allreduce_small("ar_small", packed_g, after=(tok_a,))
    grads[0].update(pend_b.done(after=(packed_g,)))
    pw, pm, pv = (_pack([src[n] for n in SMALL]) for src in (wts, mom_m, mom_v))
    small4 = _adamw("adamw_small", pw[None], packed_g, pm[None], pv[None], 0, after=(tok_a,))
    small_out = [dict(zip(SMALL, _unpack(a[0], shapes))) for a in small4]
    fill = [small4[0]]
    for names in GROUPS_FWD:
        fill += adam_layer(1, names, (tok_a,))
    fill += adam_layer(0, GROUPS_FWD[1], (tok_a,))
    tok_a = pend_a.exchange(after=tuple(fill))
    fill = adam_layer(0, GROUPS_FWD[2], (tok_a,))
    grads[0].update(pend_a.done(after=tuple(fill)))
    adam_layer(0, GROUPS_FWD[0], ())

    def pick(kind, n):
        return big_out[n][kind] if n in big_out else small_out[kind][n]

    return (loss, grad_x, *[pick(0, n) for n in WEIGHTS], *[pick(1, n) for n in WEIGHTS],
            *[pick(2, n) for n in WEIGHTS], *[pick(3, n) for n in WEIGHTS])
```

```python
import math

import jax
import jax.numpy as jnp
from jax import lax
from jax.experimental import pallas as pl
from jax.experimental.pallas import tpu as pltpu

F32 = jnp.float32
BF16 = jnp.bfloat16
MESH = pl.DeviceIdType.MESH
ANY = pl.BlockSpec(memory_space=pl.ANY)
HBM = pl.BlockSpec(memory_space=pltpu.HBM)
SEM = pl.BlockSpec(memory_space=pltpu.SEMAPHORE)
VMEM_SPEC = pl.BlockSpec(memory_space=pltpu.VMEM)
EFFECT = pltpu.SideEffectType.DATAFLOW_SIDE_EFFECTING

DEPTH = 2
CHUNK = 128
GMLP_W = 1024
GROUPS = 8
NQ, NKV, HD = 16, 4, 64
ATT_W = NQ * HD
KV_W = NKV * HD
XH, XHD = 4, 128
X_W = XH * XHD
LN_EPS = 1e-5
ALPHA = (2 * DEPTH) ** 0.25
OFF_Q = 2 * GMLP_W
OFF_K = OFF_Q + ATT_W
OFF_VA = OFF_K + KV_W
OFF_GA = OFF_VA + KV_W
NEG = -1e30

ADAM_LR, ADAM_B1, ADAM_B2, ADAM_EPS, ADAM_WD, ADAM_STEP = 0.001, 0.9, 0.999, 1e-08, 0.01, 10

V7X_VMEM_BYTES = 64 * 1024 * 1024
VMEM_LIMIT = V7X_VMEM_BYTES - 12 * 1024 * 1024
LANE = 128

BIG = ("w_in", "w_br_a", "w_br_b", "w_o", "w_xq", "w_xkv", "w_xo", "w_up", "w_down")
SHARD_AXIS = {"w_in": 1, "w_br_a": 1, "w_br_b": 1, "w_o": 0, "w_xq": 0, "w_xkv": 0, "w_xo": 1,
              "w_up": 1, "w_down": 0}
GROUPS_FWD = (("w_in",), ("w_br_a", "w_br_b", "w_o", "w_xq", "w_xkv", "w_xo"), ("w_up", "w_down"))
SMALL = ("b_gate", "ln_v_g", "ln_v_b", "w_s", "b_s", "sinks", "ln1_g", "ln1_b", "ln2_g", "ln2_b",
         "ln3_g", "ln3_b")
WEIGHTS = ("w_in", "b_gate", "ln_v_g", "ln_v_b", "w_s", "b_s", "sinks", "w_br_a", "w_br_b", "w_o",
           "ln1_g", "ln1_b", "w_xq", "w_xkv", "w_xo", "ln2_g", "ln2_b", "w_up", "w_down", "ln3_g", "ln3_b")


def _pallas(body, after=(), **kw):
    n_after = len(after)
    if not n_after:
        return pl.pallas_call(body, **kw)
    n_in = len(kw["in_specs"])
    kw["in_specs"] = list(kw["in_specs"]) + [ANY] * n_after

    def tied(*refs):
        return body(*refs[:n_in], *refs[n_in + n_after:])

    call = pl.pallas_call(tied, **kw)
    return lambda *ops: call(*ops, *after)


def _params(**kw):
    return pltpu.CompilerParams(vmem_limit_bytes=VMEM_LIMIT, **kw)


def _tile(dim, pref, unit=LANE):
    best = None
    t = unit
    while t <= min(dim, pref):
        if dim % t == 0:
            best = t
        t += unit
    return best if best is not None else dim


def _dot(a, b, dims):
    return lax.dot_general(a, b, (dims, ((), ())), preferred_element_type=F32)


NN = ((1,), (0,))
NT = ((1,), (1,))
TN = ((0,), (0,))


def _bf(x):
    return x if x.dtype == BF16 else x.astype(BF16)


def _sds(shape, dtype):
    return jax.ShapeDtypeStruct(shape, dtype)


def _mm(name, a, b, *, dims, grid, a_spec, b_spec, out_shape, out_specs, epilogue,
        extras=(), extra_specs=(), acc_shape=None, after=()):
    nk = grid[2]
    n_ex, n_out = len(extras), len(out_shape)

    def body(*refs):
        a_ref, b_ref = refs[0], refs[1]
        ex = refs[2:2 + n_ex]
        outs = refs[2 + n_ex:2 + n_ex + n_out]
        part = _dot(_bf(a_ref[...]), _bf(b_ref[...]), dims)
        if nk == 1:
            epilogue(part, ex, outs)
        else:
            acc = refs[-1]
            k = pl.program_id(2)

            @pl.when(k == 0)
            def _():
                acc[...] = part

            @pl.when(k > 0)
            def _():
                acc[...] += part

            @pl.when(k == nk - 1)
            def _():
                epilogue(acc[...], ex, outs)

    scratch = [pltpu.VMEM(acc_shape, F32)] if nk > 1 else []
    return _pallas(
        body, after=after, name=name, grid=grid, in_specs=[a_spec, b_spec, *extra_specs], out_specs=list(out_specs),
        out_shape=list(out_shape), scratch_shapes=scratch,
        compiler_params=_params(dimension_semantics=("arbitrary",) * 3),
    )(a, b, *extras)


def _store(acc, ex, outs):
    for o in outs:
        o[...] = acc.astype(o.dtype)


def _ln_rows(r, g, b):
    mu = jnp.mean(r, axis=-1, keepdims=True)
    xc = r - mu
    var = jnp.mean(xc * xc, axis=-1, keepdims=True)
    rstd = lax.rsqrt(var + LN_EPS)
    xhat = xc * rstd
    return xhat * g + b, xhat, rstd


def _ep_residual_ln(acc, ex, outs):
    x_ref, g_ref, b_ref = ex
    r_ref, y_ref, yb_ref = outs
    r = ALPHA * x_ref[...] + acc
    y, _, _ = _ln_rows(r, g_ref[...], b_ref[...])
    r_ref[...] = r
    y_ref[...] = y
    yb_ref[...] = y.astype(BF16)


def _ep_add_scaled(acc, ex, outs):
    outs[0][...] = acc + ALPHA * ex[0][...]


def _vec_spec(width):
    return pl.BlockSpec((1, width), lambda i, j, k: (0, 0))


_GC = math.sqrt(2.0 / math.pi)


def _gelu(x):
    t = jnp.tanh(_GC * (x + 0.044715 * (x * x * x)))
    return 0.5 * x * (1.0 + t), t


def _gelu_grad(x, t):
    return 0.5 * (1.0 + t) + 0.5 * x * (1.0 - t * t) * (_GC * (1.0 + 3.0 * 0.044715 * x * x))


def _sigmoid(x):
    return 1.0 / (1.0 + jnp.exp(-x))


GRP = NQ // NKV


def _band_mask(prev_ok, prev_only=False):
    rows = CHUNK if prev_only else 2 * CHUNK
    key = lax.broadcasted_iota(jnp.int32, (rows, GRP * CHUNK), 0)
    qry = jnp.bitwise_and(lax.broadcasted_iota(jnp.int32, (rows, GRP * CHUNK), 1), CHUNK - 1)
    prev = jnp.logical_and(jnp.logical_and(key < CHUNK, key > qry), prev_ok)
    if prev_only:
        return prev
    return jnp.logical_or(prev, jnp.logical_and(key >= CHUNK, key - CHUNK <= qry))


def _pair(x, g):
    return x[:, (g // 2) * LANE:(g // 2 + 1) * LANE]


def _own_head(x, g):
    xp = _pair(x, g)
    lane = lax.broadcasted_iota(jnp.int32, xp.shape, 1)
    lo = (g % 2) * HD
    return jnp.where(jnp.logical_and(lane >= lo, lane < lo + HD), xp, jnp.zeros_like(xp))


def _stack_heads(x, g, dtype=BF16):
    a = x[:, g * GRP * HD:g * GRP * HD + LANE]
    b = x[:, g * GRP * HD + LANE:(g + 1) * GRP * HD]
    ar, br = pltpu.roll(a, HD, 1), pltpu.roll(b, HD, 1)
    parts = [a, ar, b, br] if g % 2 == 0 else [ar, a, br, b]
    return jnp.concatenate(parts, axis=0).astype(dtype)


def _unstack_heads(og, g):
    o = [og[h * CHUNK:(h + 1) * CHUNK] for h in range(GRP)]
    lo = lax.broadcasted_iota(jnp.int32, (CHUNK, LANE), 1) < HD
    if g % 2 == 0:
        x0, x1, x2, x3 = o[0], pltpu.roll(o[1], HD, 1), o[2], pltpu.roll(o[3], HD, 1)
    else:
        x0, x1, x2, x3 = pltpu.roll(o[0], HD, 1), o[1], pltpu.roll(o[2], HD, 1), o[3]
    return [jnp.where(lo, x0, x1), jnp.where(lo, x2, x3)]


def _stack_rows(x, g):
    return jnp.concatenate([x[g * GRP + h:g * GRP + h + 1] for h in range(GRP)], axis=-1)


def _head_lane_sums(x, g):
    lane = lax.broadcasted_iota(jnp.int32, (8, LANE), 1)
    lo_lane = (g % 2) * HD
    sel = jnp.where(jnp.logical_and(lane >= lo_lane, lane < lo_lane + HD), 1.0, 0.0).astype(BF16)
    hi = x.astype(BF16)
    lo = (x - hi.astype(F32)).astype(BF16)
    return (_dot(sel, hi, NT) + _dot(sel, lo, NT))[0:1]


def _rope(x, cos, sin_signed):
    w = x.shape[-1]
    lane = lax.broadcasted_iota(jnp.int32, x.shape, 1)
    first = (lane % HD) < (HD // 2)
    partner = jnp.where(first, pltpu.roll(x, w - HD // 2, 1), pltpu.roll(x, HD // 2, 1))
    reps = w // LANE
    return x * jnp.tile(cos, (1, reps)) + partner * jnp.tile(sin_signed, (1, reps))


def _cast_bf16(name, w, after=()):
    _, r, c = w.shape
    br = _tile(r, 512, 8)

    def body(w_ref, o_ref):
        o_ref[...] = w_ref[...].astype(BF16)

    spec = pl.BlockSpec((None, br, c), lambda l, i: (l, i, 0))
    return _pallas(body, after=after, name=name, grid=(2, r // br), in_specs=[spec], out_specs=spec,
                   out_shape=_sds(w.shape, BF16), compiler_params=_params())(w)


def _cast2d(name, x, after=()):
    s, d = x.shape
    bm = _tile(s, 512, 8)

    def body(x_ref, o_ref):
        o_ref[...] = x_ref[...].astype(BF16)

    spec = pl.BlockSpec((bm, d), lambda i: (i, 0))
    return _pallas(body, after=after, name=name, grid=(s // bm,), in_specs=[spec], out_specs=spec,
                   out_shape=_sds(x.shape, BF16), compiler_params=_params())(x)


def _place():
    x, y, c = lax.axis_index("x"), lax.axis_index("y"), lax.axis_index("c")
    chips = [(1 - x, y), (x, 1 - y), (1 - x, 1 - y)]
    return x, y, c, chips


def _cut(ref, axis, chip=None, half=None, lead=()):
    k, n = ref.shape[-2], ref.shape[-1]
    rows, cols = slice(None), slice(None)
    if chip is not None:
        if axis == 0:
            rows = pl.ds(pl.multiple_of(chip * (k // 4), 8), k // 4)
        else:
            cols = pl.ds(pl.multiple_of(chip * (n // 4), LANE), n // 4)
    if half is not None:
        if axis == 0:
            cols = pl.ds(pl.multiple_of(half * (n // 2), LANE), n // 2)
        else:
            rows = pl.ds(pl.multiple_of(half * (k // 2), 8), k // 2)
    return ref.at[(*lead, rows, cols)]


def _split_start(name, srcs, lands, make, n_sem, after=()):
    ns, nl, na = len(srcs), len(lands), len(after)

    def body(*refs):
        src, land = refs[:ns], refs[ns:ns + nl]
        outs = refs[ns + nl + na:]
        for out_cp, _ in make(src, land, outs[0], outs[1]):
            out_cp.start()
        outs[-1][...] = jnp.zeros_like(outs[-1])

    res = pl.pallas_call(
        body, name=name, in_specs=[HBM] * (ns + nl) + [ANY] * na,
        out_specs=[SEM, SEM] + [HBM] * nl + [VMEM_SPEC],
        out_shape=[pltpu.SemaphoreType.DMA((n_sem,)), pltpu.SemaphoreType.DMA((n_sem,))]
        + [pltpu.HBM(a.shape, a.dtype) for a in lands] + [_sds((8, LANE), F32)],
        input_output_aliases={ns + i: 2 + i for i in range(nl)},
        compiler_params=pltpu.CompilerParams(has_side_effects=EFFECT),
    )(*[pltpu.with_memory_space_constraint(a, pltpu.HBM) for a in (*srcs, *lands)], *after)
    return res[0], res[1], list(res[2:2 + nl]), res[-1]


def _split_wait(name, srcs, lands, ssem, rsem, make, after=()):
    ns, nl, na = len(srcs), len(lands), len(after)

    def body(*refs):
        src, land = refs[:ns], refs[ns:ns + nl]
        s_ref, r_ref = refs[ns + nl], refs[ns + nl + 1]
        pairs = make(src, land, s_ref, r_ref)
        for _, in_cp in pairs:
            in_cp.wait_recv()
        for out_cp, _ in pairs:
            out_cp.wait_send()

    res = pl.pallas_call(
        body, name=name, in_specs=[HBM] * (ns + nl) + [SEM, SEM] + [ANY] * na,
        out_specs=[HBM] * nl, out_shape=[pltpu.HBM(a.shape, a.dtype) for a in lands],
        input_output_aliases={ns + i: i for i in range(nl)},
        compiler_params=pltpu.CompilerParams(has_side_effects=EFFECT),
    )(*srcs, *lands, ssem, rsem, *after)
    return list(res)


def _rcopy(src, dst, ssem, rsem, k, dev):
    return pltpu.make_async_remote_copy(src_ref=src, dst_ref=dst, send_sem=ssem.at[k], recv_sem=rsem.at[k],
                                        device_id=dev, device_id_type=MESH)


def _mk_gather_ici(layer, axes):
    def make(src, land, ssem, rsem):
        x, y, c, chips = _place()
        me = 2 * x + y
        pairs = []
        for w, ax in enumerate(axes):
            mine = _cut(src[w], ax, half=c, lead=(layer,))
            for j, (px, py) in enumerate(chips):
                dev = (px, py, c)
                out_cp = _rcopy(mine, _cut(land[w], ax, chip=me, half=c), ssem, rsem, 3 * w + j, dev)
                got = _cut(land[w], ax, chip=2 * px + py, half=c)
                pairs.append((out_cp, _rcopy(got, got, ssem, rsem, 3 * w + j, dev)))
        return pairs
    return make


def _mk_gather_d2d(axes):
    def make(src, land, ssem, rsem):
        x, y, c, chips = _place()
        sib = (x, y, 1 - c)
        pairs = []
        for w, ax in enumerate(axes):
            for j, (px, py) in enumerate(chips):
                have = _cut(land[w], ax, chip=2 * px + py, half=c)
                want = _cut(land[w], ax, chip=2 * px + py, half=1 - c)
                pairs.append((_rcopy(have, have, ssem, rsem, 3 * w + j, sib),
                              _rcopy(want, want, ssem, rsem, 3 * w + j, sib)))
        return pairs
    return make


def _mk_swap(axes):
    def make(src, land, ssem, rsem):
        x, y, c, _ = _place()
        sib = (x, y, 1 - c)
        pairs = []
        for w, ax in enumerate(axes):
            cp = _rcopy(_cut(src[w], ax, half=1 - c), land[w], ssem, rsem, w, sib)
            pairs.append((cp, cp))
        return pairs
    return make


def _mk_scatter(axes):
    def make(src, land, ssem, rsem):
        x, y, c, chips = _place()
        pairs = []
        for w, ax in enumerate(axes):
            for j, (px, py) in enumerate(chips):
                cp = _rcopy(_cut(src[w], ax, chip=2 * px + py), land[w].at[j], ssem, rsem, 3 * w + j, (px, py, c))
                pairs.append((cp, cp))
        return pairs
    return make


def _mk_exchange(axes):
    def make(src, land, ssem, rsem):
        x, y, c, _ = _place()
        sib = (x, y, 1 - c)
        pairs = []
        for w, ax in enumerate(axes):
            have = _cut(land[w], ax, half=c)
            want = _cut(land[w], ax, half=1 - c)
            pairs.append((_rcopy(have, have, ssem, rsem, w, sib), _rcopy(want, want, ssem, rsem, w, sib)))
        return pairs
    return make


def _place_own(name, shard, axis, meidx):
    _, r, c = shard.shape
    full = (4 * r, c) if axis == 0 else (r, 4 * c)
    br = _tile(r, 512, 8)
    nb = r // br
    if axis == 0:
        ospec = pl.BlockSpec((br, c), lambda i, me: (me[0] * nb + i, 0))
    else:
        ospec = pl.BlockSpec((br, c), lambda i, me: (i, me[0]))

    def body(me_ref, s_ref, o0_ref, o1_ref):
        o0_ref[...] = s_ref[0]
        o1_ref[...] = s_ref[1]

    return pl.pallas_call(
        body, name=name,
        grid_spec=pltpu.PrefetchScalarGridSpec(
            num_scalar_prefetch=1, grid=(nb,),
            in_specs=[pl.BlockSpec((2, br, c), lambda i, me: (0, i, 0))], out_specs=[ospec, ospec]),
        out_shape=[_sds(full, BF16)] * 2, compiler_params=_params(),
    )(meidx, shard)


def _add_half(name, part, got, axis, cidx):
    k, n = got.shape
    bm = _tile(k, 512, 8)
    nb = k // bm
    if axis == 0:
        pspec = pl.BlockSpec((bm, n), lambda i, c: (i, c[0]))
    else:
        pspec = pl.BlockSpec((bm, n), lambda i, c: (c[0] * nb + i, 0))

    def body(c_ref, a_ref, b_ref, o_ref):
        o_ref[...] = (a_ref[...].astype(F32) + b_ref[...].astype(F32)).astype(BF16)

    return pl.pallas_call(
        body, name=name,
        grid_spec=pltpu.PrefetchScalarGridSpec(
            num_scalar_prefetch=1, grid=(nb,), in_specs=[pspec, pl.BlockSpec((bm, n), lambda i, c: (i, 0))],
            out_specs=pl.BlockSpec((bm, n), lambda i, c: (i, 0))),
        out_shape=_sds((k, n), BF16), compiler_params=_params(),
    )(cidx, part, got)


def _sum_half(name, own, slots, axis, mc):
    _, r, cc = slots.shape
    br = _tile(r, 256, 8)
    nb = r // br
    if axis == 0:
        own_spec = pl.BlockSpec((br, cc), lambda i, mc: (mc[0] * nb + i, 0))
        out_spec = pl.BlockSpec((br, cc), lambda i, mc: (i, mc[1]))
        shape = (r, 2 * cc)
    else:
        own_spec = pl.BlockSpec((br, cc), lambda i, mc: (i, mc[0]))
        out_spec = pl.BlockSpec((br, cc), lambda i, mc: (mc[1] * nb + i, 0))
        shape = (2 * r, cc)

    def body(mc_ref, own_ref, s_ref, o_ref):
        acc = own_ref[...].astype(F32)
        for i in range(3):
            acc = acc + s_ref[i].astype(F32)
        o_ref[...] = acc

    return pl.pallas_call(
        body, name=name,
        grid_spec=pltpu.PrefetchScalarGridSpec(
            num_scalar_prefetch=1, grid=(nb,),
            in_specs=[own_spec, pl.BlockSpec((3, br, cc), lambda i, mc: (0, i, 0))], out_specs=out_spec),
        out_shape=_sds(shape, F32), compiler_params=_params(),
    )(mc, own, slots)


def _mk_small(src, land, ssem, rsem):
    x, y, c, _ = _place()
    me = 4 * x + 2 * y + c
    pairs = []
    for k in range(1, 8):
        peer = (1 - x if k & 4 else x, 1 - y if k & 2 else y, 1 - c if k & 1 else c)
        got = land[0].at[4 * peer[0] + 2 * peer[1] + peer[2]]
        pairs.append((_rcopy(src[0], land[0].at[me], ssem, rsem, k - 1, peer),
                      _rcopy(got, got, ssem, rsem, k - 1, peer)))
    return pairs


def _place_slot(name, packed, me8):
    rows, lanes = packed.shape
    br = _tile(rows, 512, 8)

    def body(me_ref, p_ref, o_ref):
        o_ref[...] = p_ref[...]

    return pl.pallas_call(
        body, name=name,
        grid_spec=pltpu.PrefetchScalarGridSpec(
            num_scalar_prefetch=1, grid=(rows // br,),
            in_specs=[pl.BlockSpec((br, lanes), lambda i, me: (i, 0))],
            out_specs=pl.BlockSpec((None, br, lanes), lambda i, me: (me[0], i, 0))),
        out_shape=_sds((8, rows, lanes), F32), compiler_params=_params(),
    )(me8, packed)


def _sum_slots(name, slots):
    _, rows, lanes = slots.shape
    br = _tile(rows, 512, 8)

    def body(s_ref, o_ref):
        acc = s_ref[0]
        for i in range(1, 8):
            acc = acc + s_ref[i]
        o_ref[...] = acc

    return pl.pallas_call(
        body, name=name, grid=(rows // br,), in_specs=[pl.BlockSpec((8, br, lanes), lambda i: (0, i, 0))],
        out_specs=pl.BlockSpec((br, lanes), lambda i: (i, 0)), out_shape=_sds((rows, lanes), F32),
        compiler_params=_params(),
    )(slots)


def _adamw_math(w, g, m, v):
    m2 = ADAM_B1 * m + (1.0 - ADAM_B1) * g
    v2 = ADAM_B2 * v + (1.0 - ADAM_B2) * (g * g)
    m_hat = m2 / (1.0 - ADAM_B1 ** ADAM_STEP)
    v_hat = v2 / (1.0 - ADAM_B2 ** ADAM_STEP)
    delta = -ADAM_LR * (m_hat / (jnp.sqrt(v_hat) + ADAM_EPS) + ADAM_WD * w)
    return delta, m2, v2


def _adamw(name, w, g, m, v, layer, prev=None, after=()):
    _, r, c = w.shape
    br = _tile(r, 256, 8)
    n_prev = 0 if prev is None else 4

    def body(*refs):
        w_ref, g_ref, m_ref, v_ref = refs[:4]
        go_ref, d_ref, mo_ref, vo_ref = refs[4 + n_prev:]
        gg = g_ref[...]
        delta, m2, v2 = _adamw_math(w_ref[...], gg, m_ref[...], v_ref[...])
        go_ref[...] = gg
        d_ref[...] = delta
        mo_ref[...] = m2
        vo_ref[...] = v2

    spec = pl.BlockSpec((None, br, c), lambda i: (layer, i, 0))
    return _pallas(
        body, after=after, name=name, grid=(r // br,),
        in_specs=[spec, pl.BlockSpec((br, c), lambda i: (i, 0)), spec, spec] + [ANY] * n_prev,
        out_specs=[spec] * 4, out_shape=[_sds(w.shape, F32)] * 4,
        input_output_aliases={4 + i: i for i in range(n_prev)}, compiler_params=_params(),
    )(w, g, m, v, *(prev or ()))


def _gmlp_fwd(name, proj, ln_g, ln_b, w_s, b_st):
    s = proj.shape[0]

    def body(u_ref, v_ref, g_ref, b_ref, ws_ref, bst_ref, sg_ref):
        gu, _ = _gelu(u_ref[...])
        gv, _ = _gelu(v_ref[...])
        vn, _, _ = _ln_rows(gv, g_ref[...], b_ref[...])
        vn = vn.astype(BF16)
        row = lax.broadcasted_iota(jnp.int32, (CHUNK, CHUNK), 0)
        col = lax.broadcasted_iota(jnp.int32, (CHUNK, CHUNK), 1)
        tril = col <= row
        outs = []
        for g in range(GROUPS):
            sl = slice(g * LANE, (g + 1) * LANE)
            w = jnp.where(tril, ws_ref[g], 0.0).astype(BF16)
            mixed = _dot(w, vn[:, sl], NN) + bst_ref[:, g:g + 1]
            outs.append(gu[:, sl] * mixed)
        sg_ref[...] = jnp.concatenate(outs, axis=-1).astype(BF16)

    return _pallas(
        body, name=name, grid=(s // CHUNK,),
        in_specs=[pl.BlockSpec((CHUNK, GMLP_W), lambda n: (n, 0)), pl.BlockSpec((CHUNK, GMLP_W), lambda n: (n, 1)),
                  pl.BlockSpec((1, GMLP_W), lambda n: (0, 0)), pl.BlockSpec((1, GMLP_W), lambda n: (0, 0)),
                  pl.BlockSpec((GROUPS, CHUNK, CHUNK), lambda n: (0, 0, 0)),
                  pl.BlockSpec((CHUNK, GROUPS), lambda n: (0, 0))],
        out_specs=pl.BlockSpec((CHUNK, GMLP_W), lambda n: (n, 0)),
        out_shape=_sds((s, GMLP_W), BF16), compiler_params=_params(),
    )(proj, proj, ln_g, ln_b, w_s, b_st)


def _swa_fwd(name, proj, cos4, sin4, sinks, after=()):
    s = proj.shape[0]
    w = CHUNK
    scale = HD ** -0.5

    def body(q_ref, k_ref, v_ref, cos_ref, sin_ref, sink_ref, o_ref, qr_ref, kr_ref, lse_ref, kprev, vprev):
        n = pl.program_id(0)

        @pl.when(n == 0)
        def _():
            kprev[...] = jnp.zeros_like(kprev)
            vprev[...] = jnp.zeros_like(vprev)

        cos, sin = cos_ref[...], sin_ref[...]
        qr = _rope(q_ref[...], cos, sin)
        kr = _rope(k_ref[...], cos, sin).astype(BF16)
        vb = v_ref[...].astype(BF16)
        kk = jnp.concatenate([kprev[...], kr], axis=0)
        vv = jnp.concatenate([vprev[...], vb], axis=0)
        valid = _band_mask(n > 0)
        outs, lses = [], []
        for g in range(NKV):
            sc = jnp.where(valid, _dot(_own_head(kk, g), _stack_heads(qr, g), NT) * scale, NEG)
            sink = sink_ref[g]
            mx = jnp.maximum(jnp.max(sc, axis=0, keepdims=True), sink)
            p = jnp.exp(sc - mx)
            den = jnp.sum(p, axis=0, keepdims=True) + jnp.exp(sink - mx)
            og = _dot((p * (1.0 / den)).astype(BF16), _pair(vv, g), TN)
            outs.extend(_unstack_heads(og, g))
            lg = mx + jnp.log(den)
            lses.extend([lg[:, h * w:(h + 1) * w] for h in range(GRP)])
        o_ref[...] = jnp.concatenate(outs, axis=-1).astype(BF16)
        lse_ref[...] = jnp.concatenate(lses, axis=0)
        qr_ref[...] = qr.astype(BF16)
        kr_ref[...] = kr
        kprev[...] = kr
        vprev[...] = vb

    return _pallas(
        body, after=after, name=name, grid=(s // w,),
        in_specs=[pl.BlockSpec((w, ATT_W), lambda n: (n, OFF_Q // ATT_W)),
                  pl.BlockSpec((w, KV_W), lambda n: (n, OFF_K // KV_W)),
                  pl.BlockSpec((w, KV_W), lambda n: (n, OFF_VA // KV_W)),
                  pl.BlockSpec((w, LANE), lambda n: (n, 0)), pl.BlockSpec((w, LANE), lambda n: (n, 0)),
                  pl.BlockSpec((NKV, 1, GRP * w), lambda n: (0, 0, 0))],
        out_specs=[pl.BlockSpec((w, ATT_W), lambda n: (n, 0)), pl.BlockSpec((w, ATT_W), lambda n: (n, 0)),
                   pl.BlockSpec((w, KV_W), lambda n: (n, 0)), pl.BlockSpec((None, NQ, w), lambda n: (n, 0, 0))],
        out_shape=[_sds((s, ATT_W), BF16), _sds((s, ATT_W), BF16), _sds((s, KV_W), BF16),
                   _sds((s // w, NQ, w), F32)],
        scratch_shapes=[pltpu.VMEM((w, KV_W), BF16), pltpu.VMEM((w, KV_W), BF16)],
        compiler_params=_params(dimension_semantics=("arbitrary",)),
    )(proj, proj, proj, cos4, sin4, sinks)


def _gate_fwd(name, sg, attn, wa, wb, proj, b_gate, d):
    s = sg.shape[0]
    bm, bn = _tile(s, 1024), _tile(d, 512)
    off_a, off_b = OFF_GA // bn, (OFF_GA + d) // bn

    def body(sg_ref, at_ref, wa_ref, wb_ref, ga_ref, gb_ref, ba_ref, bb_ref, m_ref, ya_ref, yb_ref):
        ya = _dot(sg_ref[...], wa_ref[...], NN)
        yb = _dot(at_ref[...], wb_ref[...], NN)
        sa = _sigmoid(ga_ref[...] + ba_ref[...])
        sb = _sigmoid(gb_ref[...] + bb_ref[...])
        m_ref[...] = (sa * ya + sb * yb).astype(BF16)
        ya_ref[...] = ya.astype(BF16)
        yb_ref[...] = yb.astype(BF16)

    tile = pl.BlockSpec((bm, bn), lambda i, j: (i, j))
    return _pallas(
        body, name=name, grid=(s // bm, d // bn),
        in_specs=[pl.BlockSpec((bm, GMLP_W), lambda i, j: (i, 0)), pl.BlockSpec((bm, ATT_W), lambda i, j: (i, 0)),
                  pl.BlockSpec((GMLP_W, bn), lambda i, j: (0, j)), pl.BlockSpec((ATT_W, bn), lambda i, j: (0, j)),
                  pl.BlockSpec((bm, bn), lambda i, j: (i, off_a + j)),
                  pl.BlockSpec((bm, bn), lambda i, j: (i, off_b + j)),
                  pl.BlockSpec((1, bn), lambda i, j: (0, j)), pl.BlockSpec((1, bn), lambda i, j: (0, d // bn + j))],
        out_specs=[tile, tile, tile], out_shape=[_sds((s, d), BF16)] * 3,
        compiler_params=_params(),
    )(sg, attn, wa, wb, proj, proj, b_gate, b_gate)


def _xattn_fwd(name, xb, xf, wq, kv, wo, ln_g, ln_b, after=()):
    s, d = xf.shape
    mem = kv.shape[0]
    bm = _tile(s, 512)
    scale = XHD ** -0.5

    def body(xb_ref, xf_ref, wq_ref, kv_ref, wo_ref, g_ref, b_ref, q_out, o_out, r_out, y_out, yb_out):
        qb = _dot(xb_ref[...], wq_ref[...], NN).astype(BF16)
        kvv = kv_ref[...]
        outs = []
        for h in range(XH):
            hs = slice(h * XHD, (h + 1) * XHD)
            vs = slice(X_W + h * XHD, X_W + (h + 1) * XHD)
            sc = _dot(qb[:, hs], kvv[:, hs], NT) * scale
            mx = jnp.max(sc, axis=-1, keepdims=True)
            p = jnp.exp(sc - mx)
            p = p / jnp.sum(p, axis=-1, keepdims=True)
            outs.append(_dot(p.astype(BF16), kvv[:, vs], NN))
        ob = jnp.concatenate(outs, axis=-1).astype(BF16)
        yv = _dot(ob, wo_ref[...], NN)
        r = ALPHA * xf_ref[...] + yv
        yn, _, _ = _ln_rows(r, g_ref[...], b_ref[...])
        q_out[...] = qb
        o_out[...] = ob
        r_out[...] = r
        y_out[...] = yn
        yb_out[...] = yn.astype(BF16)

    row = lambda wd: pl.BlockSpec((bm, wd), lambda i: (i, 0))
    return _pallas(
        body, after=after, name=name, grid=(s // bm,),
        in_specs=[row(d), row(d), pl.BlockSpec((d, X_W), lambda i: (0, 0)),
                  pl.BlockSpec((mem, 2 * X_W), lambda i: (0, 0)), pl.BlockSpec((X_W, d), lambda i: (0, 0)),
                  pl.BlockSpec((1, d), lambda i: (0, 0)), pl.BlockSpec((1, d), lambda i: (0, 0))],
        out_specs=[row(X_W), row(X_W), row(d), row(d), row(d)],
        out_shape=[_sds((s, X_W), BF16), _sds((s, X_W), BF16), _sds((s, d), F32), _sds((s, d), F32),
                   _sds((s, d), BF16)],
        compiler_params=_params(),
    )(xb, xf, wq, kv, wo, ln_g, ln_b)


def _loss_grad(name, y, tgt):
    s, d = y.shape
    bm = _tile(s, 512)

    def body(y_ref, t_ref, dy_ref, loss_ref):
        i = pl.program_id(0)
        err = y_ref[...] - t_ref[...]
        dy_ref[...] = err * (1.0 / d)
        part = 0.5 * jnp.sum(jnp.sum(err * err, axis=-1, keepdims=True), axis=0, keepdims=True) * (1.0 / d)

        @pl.when(i == 0)
        def _():
            loss_ref[...] = part

        @pl.when(i > 0)
        def _():
            loss_ref[...] += part

    row = pl.BlockSpec((bm, d), lambda i: (i, 0))
    return _pallas(
        body, name=name, grid=(s // bm,), in_specs=[row, row],
        out_specs=[row, pl.BlockSpec((1, 1), lambda i: (0, 0))],
        out_shape=[_sds((s, d), F32), _sds((1, 1), F32)],
        compiler_params=_params(dimension_semantics=("arbitrary",)),
    )(y, tgt)


def _ln_bwd(name, dy, r, g, after=()):
    s, d = r.shape
    bm = _tile(s, 256)

    def body(dy_ref, r_ref, g_ref, dr_ref, drb_ref, dg_ref, db_ref):
        i = pl.program_id(0)
        dyv = dy_ref[...]
        _, xhat, rstd = _ln_rows(r_ref[...], g_ref[...], 0.0)
        dxh = dyv * g_ref[...]
        m1 = jnp.mean(dxh, axis=-1, keepdims=True)
        m2 = jnp.mean(dxh * xhat, axis=-1, keepdims=True)
        dr = rstd * (dxh - m1 - xhat * m2)
        dr_ref[...] = dr
        drb_ref[...] = dr.astype(BF16)
        dg = jnp.sum(dyv * xhat, axis=0, keepdims=True)
        db = jnp.sum(dyv, axis=0, keepdims=True)

        @pl.when(i == 0)
        def _():
            dg_ref[...] = dg
            db_ref[...] = db

        @pl.when(i > 0)
        def _():
            dg_ref[...] += dg
            db_ref[...] += db

    row = pl.BlockSpec((bm, d), lambda i: (i, 0))
    vec = pl.BlockSpec((1, d), lambda i: (0, 0))
    return _pallas(
        body, after=after, name=name, grid=(s // bm,), in_specs=[row, row, vec], out_specs=[row, row, vec, vec],
        out_shape=[_sds((s, d), F32), _sds((s, d), BF16), _sds((1, d), F32), _sds((1, d), F32)],
        compiler_params=_params(dimension_semantics=("arbitrary",)),
    )(dy, r, g)


def _xattn_bwd(name, dyb, drf, q, kv, wo, wq):
    s, d = drf.shape
    mem = kv.shape[0]
    bm = _tile(s, 512)
    scale = XHD ** -0.5

    def body(dy_ref, dr_ref, q_ref, kv_ref, wo_ref, wq_ref, dx_out, dq_out, dkv_out):
        i = pl.program_id(0)
        dob = _dot(dy_ref[...], wo_ref[...], NT).astype(BF16)
        qb = q_ref[...]
        kvv = kv_ref[...]
        dqs, dks, dvs = [], [], []
        for h in range(XH):
            hs = slice(h * XHD, (h + 1) * XHD)
            vs = slice(X_W + h * XHD, X_W + (h + 1) * XHD)
            sc = _dot(qb[:, hs], kvv[:, hs], NT) * scale
            mx = jnp.max(sc, axis=-1, keepdims=True)
            p = jnp.exp(sc - mx)
            p = p / jnp.sum(p, axis=-1, keepdims=True)
            dp = _dot(dob[:, hs], kvv[:, vs], NT)
            dsum = jnp.sum(p * dp, axis=-1, keepdims=True)
            dsb = (p * (dp - dsum) * scale).astype(BF16)
            dqs.append(_dot(dsb, kvv[:, hs], NN))
            dks.append(_dot(dsb, qb[:, hs], TN))
            dvs.append(_dot(p.astype(BF16), dob[:, hs], TN))
        dqb = jnp.concatenate(dqs, axis=-1).astype(BF16)
        dq_out[...] = dqb
        dx_out[...] = _dot(dqb, wq_ref[...], NT) + ALPHA * dr_ref[...]
        dkv = jnp.concatenate(dks + dvs, axis=-1)

        @pl.when(i == 0)
        def _():
            dkv_out[...] = dkv

        @pl.when(i > 0)
        def _():
            dkv_out[...] += dkv

    row = lambda wd: pl.BlockSpec((bm, wd), lambda i: (i, 0))
    return _pallas(
        body, name=name, grid=(s // bm,),
        in_specs=[row(d), row(d), row(X_W), pl.BlockSpec((mem, 2 * X_W), lambda i: (0, 0)),
                  pl.BlockSpec((X_W, d), lambda i: (0, 0)), pl.BlockSpec((d, X_W), lambda i: (0, 0))],
        out_specs=[row(d), row(X_W), pl.BlockSpec((mem, 2 * X_W), lambda i: (0, 0))],
        out_shape=[_sds((s, d), F32), _sds((s, X_W), BF16), _sds((mem, 2 * X_W), F32)],
        compiler_params=_params(dimension_semantics=("arbitrary",)),
    )(dyb, drf, q, kv, wo, wq)


def _gate_bwd(name, dr1b, w_o, proj, ya, yb, b_gate, d, after=()):
    s = dr1b.shape[0]
    bm, bn = _tile(s, 1024), _tile(d, 512)
    off_a, off_b = OFF_GA // bn, (OFF_GA + d) // bn
    nj = d // bn

    def body(a_ref, w_ref, ga_ref, gb_ref, ya_ref, yb_ref, ba_ref, bb_ref, dya_ref, dyb_ref, dg_ref, dba_ref, dbb_ref):
        i = pl.program_id(1)
        dm = _dot(a_ref[...], w_ref[...], NT)
        sa = _sigmoid(ga_ref[...] + ba_ref[...])
        sb = _sigmoid(gb_ref[...] + bb_ref[...])
        dya_ref[...] = (dm * sa).astype(BF16)
        dyb_ref[...] = (dm * sb).astype(BF16)
        dga = dm * ya_ref[...].astype(F32) * (sa * (1.0 - sa))
        dgb = dm * yb_ref[...].astype(F32) * (sb * (1.0 - sb))
        dg_ref[0] = dga.astype(BF16)
        dg_ref[1] = dgb.astype(BF16)
        sa_sum = jnp.sum(dga, axis=0, keepdims=True)
        sb_sum = jnp.sum(dgb, axis=0, keepdims=True)

        @pl.when(i == 0)
        def _():
            dba_ref[...] = sa_sum
            dbb_ref[...] = sb_sum

        @pl.when(i > 0)
        def _():
            dba_ref[...] += sa_sum
            dbb_ref[...] += sb_sum

    tile = pl.BlockSpec((bm, bn), lambda j, i: (i, j))
    return _pallas(
        body, after=after, name=name, grid=(nj, s // bm),
        in_specs=[pl.BlockSpec((bm, d), lambda j, i: (i, 0)),
                  pl.BlockSpec((bn, d), lambda j, i: (j, 0)),
                  pl.BlockSpec((bm, bn), lambda j, i: (i, off_a + j)),
                  pl.BlockSpec((bm, bn), lambda j, i: (i, off_b + j)),
                  tile, tile,
                  pl.BlockSpec((1, bn), lambda j, i: (0, j)), pl.BlockSpec((1, bn), lambda j, i: (0, nj + j))],
        out_specs=[tile, tile, pl.BlockSpec((2, bm, bn), lambda j, i: (0, i, j)),
                   pl.BlockSpec((1, bn), lambda j, i: (0, j)), pl.BlockSpec((1, bn), lambda j, i: (0, j))],
        out_shape=[_sds((s, d), BF16), _sds((s, d), BF16), _sds((2, s, d), BF16), _sds((1, d), F32),
                   _sds((1, d), F32)],
        compiler_params=_params(dimension_semantics=("arbitrary", "arbitrary")),
    )(dr1b, w_o, proj, proj, ya, yb, b_gate, b_gate)


def _gmlp_bwd(name, proj, dsg, ln_g, ln_b, w_s, b_st):
    s = proj.shape[0]

    def body(u_ref, v_ref, dsg_ref, g_ref, b_ref, ws_ref, bst_ref, duv_ref, dws_ref, dbst_ref, dlg_ref, dlb_ref):
        n = pl.program_id(0)
        u, v = u_ref[...], v_ref[...]
        gu, tu = _gelu(u)
        gv, tv = _gelu(v)
        gam = g_ref[...]
        vn, xhat, rstd = _ln_rows(gv, gam, b_ref[...])
        vnb = vn.astype(BF16)
        dsg = dsg_ref[...].astype(F32)
        row = lax.broadcasted_iota(jnp.int32, (CHUNK, CHUNK), 0)
        col = lax.broadcasted_iota(jnp.int32, (CHUNK, CHUNK), 1)
        tril = col <= row
        dgu, dvn, dws, dbs = [], [], [], []
        for g in range(GROUPS):
            sl = slice(g * LANE, (g + 1) * LANE)
            w = jnp.where(tril, ws_ref[g], 0.0).astype(BF16)
            mixed = _dot(w, vnb[:, sl], NN) + bst_ref[:, g:g + 1]
            dgu.append(dsg[:, sl] * mixed)
            dmx = dsg[:, sl] * gu[:, sl]
            dmxb = dmx.astype(BF16)
            dbs.append(jnp.sum(dmx, axis=-1, keepdims=True))
            dws.append(jnp.where(tril, _dot(dmxb, vnb[:, sl], NT), 0.0))
            dvn.append(_dot(w, dmxb, TN))
        dvn = jnp.concatenate(dvn, axis=-1)
        dgu = jnp.concatenate(dgu, axis=-1)
        dxh = dvn * gam
        m1 = jnp.mean(dxh, axis=-1, keepdims=True)
        m2 = jnp.mean(dxh * xhat, axis=-1, keepdims=True)
        dgv = rstd * (dxh - m1 - xhat * m2)
        du = dgu * _gelu_grad(u, tu)
        dv = dgv * _gelu_grad(v, tv)
        duv_ref[...] = jnp.concatenate([du, dv], axis=-1).astype(BF16)
        dlg = jnp.sum(dvn * xhat, axis=0, keepdims=True)
        dlb = jnp.sum(dvn, axis=0, keepdims=True)
        dbst = jnp.concatenate(dbs, axis=-1)

        @pl.when(n == 0)
        def _():
            for g in range(GROUPS):
                dws_ref[g] = dws[g]
            dbst_ref[...] = dbst
            dlg_ref[...] = dlg
            dlb_ref[...] = dlb

        @pl.when(n > 0)
        def _():
            for g in range(GROUPS):
                dws_ref[g] += dws[g]
            dbst_ref[...] += dbst
            dlg_ref[...] += dlg
            dlb_ref[...] += dlb

    vec = pl.BlockSpec((1, GMLP_W), lambda n: (0, 0))
    return _pallas(
        body, name=name, grid=(s // CHUNK,),
        in_specs=[pl.BlockSpec((CHUNK, GMLP_W), lambda n: (n, 0)), pl.BlockSpec((CHUNK, GMLP_W), lambda n: (n, 1)),
                  pl.BlockSpec((CHUNK, GMLP_W), lambda n: (n, 0)), vec, vec,
                  pl.BlockSpec((GROUPS, CHUNK, CHUNK), lambda n: (0, 0, 0)),
                  pl.BlockSpec((CHUNK, GROUPS), lambda n: (0, 0))],
        out_specs=[pl.BlockSpec((CHUNK, 2 * GMLP_W), lambda n: (n, 0)),
                   pl.BlockSpec((GROUPS, CHUNK, CHUNK), lambda n: (0, 0, 0)),
                   pl.BlockSpec((CHUNK, GROUPS), lambda n: (0, 0)), vec, vec],
        out_shape=[_sds((s, 2 * GMLP_W), BF16), _sds((GROUPS, CHUNK, CHUNK), F32), _sds((CHUNK, GROUPS), F32),
                   _sds((1, GMLP_W), F32), _sds((1, GMLP_W), F32)],
        compiler_params=_params(dimension_semantics=("arbitrary",)),
    )(proj, proj, dsg, ln_g, ln_b, w_s, b_st)


def _swa_bwd(name, qr, kr, proj, do, o, lse, sinks, cos4, nsin4, after=()):
    s = qr.shape[0]
    w = CHUNK
    nblk = s // w
    scale = HD ** -0.5
    grp = NQ // NKV

    def body(qj_ref, qn_ref, kj_ref, kp_ref, vj_ref, vp_ref, doj_ref, don_ref, oj_ref, on_ref, lj_ref, ln_ref,
             sink_ref, cos_ref, sin_ref, out_ref, dsink_ref):
        j = pl.program_id(0)
        qj, qn = qj_ref[...].astype(F32), qn_ref[...].astype(F32)
        doj, don = doj_ref[...].astype(F32), don_ref[...].astype(F32)
        kk = jnp.concatenate([kp_ref[...], kj_ref[...]], axis=0)
        vv = jnp.concatenate([vp_ref[...], vj_ref[...]], axis=0).astype(BF16)
        lj, lnx = lj_ref[...], ln_ref[...]
        prod_j = doj * oj_ref[...].astype(F32)
        prod_n = don * on_ref[...].astype(F32)
        valid_j = _band_mask(j > 0)
        valid_n = _band_mask(j + 1 < nblk, prev_only=True)
        lo = lax.broadcasted_iota(jnp.int32, (w, LANE), 1) < HD
        dqs, dsk, dk_g, dv_g = [], [], [], []
        for g in range(NKV):
            kz, vz = _own_head(kk, g), _own_head(vv, g)
            kz_c, vz_c = kz[w:], vz[w:]
            qg_j, qg_n = _stack_heads(qj, g), _stack_heads(qn, g)
            dog_j, dog_n = _stack_heads(doj, g), _stack_heads(don, g)
            l_j, l_n = _stack_rows(lj, g), _stack_rows(lnx, g)
            d_j = _head_lane_sums(_stack_heads(prod_j, g, F32), g)
            d_n = _head_lane_sums(_stack_heads(prod_n, g, F32), g)
            p = jnp.where(valid_j, jnp.exp(_dot(kz, qg_j, NT) * scale - l_j), 0.0)
            ds = (p * (_dot(vz, dog_j, NT) - d_j) * scale).astype(BF16)
            dqs.extend(_unstack_heads(_dot(ds, kz, TN), g))
            p2 = jnp.where(valid_n, jnp.exp(_dot(kz_c, qg_n, NT) * scale - l_n), 0.0)
            ds2 = (p2 * (_dot(vz_c, dog_n, NT) - d_n) * scale).astype(BF16)
            dk_g.append(_dot(ds[w:], qg_j, NN) + _dot(ds2, qg_n, NN))
            dv_g.append(_dot(p[w:].astype(BF16), dog_j, NN) + _dot(p2.astype(BF16), dog_n, NN))
            t = jnp.exp(sink_ref[g] - l_j) * d_j
            dsk.extend([-jnp.sum(t[:, h * w:(h + 1) * w], axis=-1, keepdims=True) for h in range(GRP)])
        cos, nsin = cos_ref[...], sin_ref[...]
        dq = _rope(jnp.concatenate(dqs, axis=-1), cos, nsin)
        dk = _rope(jnp.concatenate([jnp.where(lo, dk_g[2 * m], dk_g[2 * m + 1]) for m in range(NKV // 2)], axis=-1),
                   cos, nsin)
        dv = jnp.concatenate([jnp.where(lo, dv_g[2 * m], dv_g[2 * m + 1]) for m in range(NKV // 2)], axis=-1)
        out_ref[...] = jnp.concatenate([dq, dk, dv], axis=-1).astype(BF16)
        dsink = jnp.concatenate(dsk, axis=-1)

        @pl.when(j == 0)
        def _():
            dsink_ref[...] = dsink

        @pl.when(j > 0)
        def _():
            dsink_ref[...] += dsink

    nxt = lambda j: jnp.minimum(j + 1, nblk - 1)
    prv = lambda j: jnp.maximum(j - 1, 0)
    va = OFF_VA // KV_W
    return _pallas(
        body, after=after, name=name, grid=(nblk,),
        in_specs=[pl.BlockSpec((w, ATT_W), lambda j: (j, 0)), pl.BlockSpec((w, ATT_W), lambda j: (nxt(j), 0)),
                  pl.BlockSpec((w, KV_W), lambda j: (j, 0)), pl.BlockSpec((w, KV_W), lambda j: (prv(j), 0)),
                  pl.BlockSpec((w, KV_W), lambda j: (j, va)), pl.BlockSpec((w, KV_W), lambda j: (prv(j), va)),
                  pl.BlockSpec((w, ATT_W), lambda j: (j, 0)), pl.BlockSpec((w, ATT_W), lambda j: (nxt(j), 0)),
                  pl.BlockSpec((w, ATT_W), lambda j: (j, 0)), pl.BlockSpec((w, ATT_W), lambda j: (nxt(j), 0)),
                  pl.BlockSpec((None, NQ, w), lambda j: (j, 0, 0)),
                  pl.BlockSpec((None, NQ, w), lambda j: (nxt(j), 0, 0)),
                  pl.BlockSpec((NKV, 1, GRP * w), lambda j: (0, 0, 0)),
                  pl.BlockSpec((w, LANE), lambda j: (j, 0)), pl.BlockSpec((w, LANE), lambda j: (j, 0))],
        out_specs=[pl.BlockSpec((w, ATT_W + 2 * KV_W), lambda j: (j, 0)), pl.BlockSpec((1, NQ), lambda j: (0, 0))],
        out_shape=[_sds((s, ATT_W + 2 * KV_W), BF16), _sds((1, NQ), F32)],
        compiler_params=_params(dimension_semantics=("arbitrary",)),
    )(qr, qr, kr, kr, proj, proj, do, do, o, o, lse, lse, sinks, cos4, nsin4)


def _mm_nn(name, a, w, *, out_dtypes, epilogue=_store, bm_pref=1024, bn_pref=1024, after=()):
    m, k = a.shape
    n = w.shape[-1]
    bm, bn = _tile(m, bm_pref), _tile(n, bn_pref)
    tile = pl.BlockSpec((bm, bn), lambda i, j, kk: (i, j))
    return _mm(name, a, w, dims=NN, grid=(m // bm, n // bn, 1),
               a_spec=pl.BlockSpec((bm, k), lambda i, j, kk: (i, 0)),
               b_spec=pl.BlockSpec((k, bn), lambda i, j, kk: (0, j)),
               out_shape=[_sds((m, n), dt) for dt in out_dtypes], out_specs=[tile] * len(out_dtypes),
               epilogue=epilogue, after=after)


def _mm_tn(name, a, b, *, bm_pref=1024, bn_pref=1024, after=()):
    s, m = a.shape
    n = b.shape[-1]
    bm, bn = _tile(m, bm_pref), _tile(n, bn_pref)
    return _mm(name, a, b, dims=TN, grid=(m // bm, n // bn, 1),
               a_spec=pl.BlockSpec((s, bm), lambda i, j, kk: (0, i)),
               b_spec=pl.BlockSpec((s, bn), lambda i, j, kk: (0, j)),
               out_shape=[_sds((m, n), BF16)], out_specs=[pl.BlockSpec((bm, bn), lambda i, j, kk: (i, j))],
               epilogue=_store, after=after)[0]


class _Gather:
    def __init__(self, tag, layer, names, shards, fulls, after):
        self.tag, self.names = tag, names
        self.axes = [SHARD_AXIS[n] for n in names]
        self.srcs = [shards[n] for n in names]
        self.mk1 = _mk_gather_ici(layer, self.axes)
        self.mk2 = _mk_gather_d2d(self.axes)
        self.n_sem = 3 * len(names)
        self.s1, self.r1, self.lands, self.token = _split_start(
            tag + "_ici_start", self.srcs, [fulls[n] for n in names], self.mk1, self.n_sem, after)

    def forward(self, after=()):
        lands = _split_wait(self.tag + "_ici_wait", self.srcs, self.lands, self.s1, self.r1, self.mk1, after)
        self.s2, self.r2, self.lands, tok = _split_start(self.tag + "_d2d_start", [], lands, self.mk2, self.n_sem)
        return tok

    def done(self, after=()):
        lands = _split_wait(self.tag + "_d2d_wait", [], self.lands, self.s2, self.r2, self.mk2, after)
        return dict(zip(self.names, lands))


class _Reduce:
    def __init__(self, tag, names, parts, cidx, mcidx, after=()):
        self.tag, self.names, self.cidx, self.mcidx = tag, names, cidx, mcidx
        self.axes = [SHARD_AXIS[n] for n in names]
        self.parts = [parts[n] for n in names]
        self.mk = _mk_swap(self.axes)
        lands = []
        for p, ax in zip(self.parts, self.axes):
            k, n = p.shape
            lands.append(lax.empty((k, n // 2) if ax == 0 else (k // 2, n), BF16))
        self.s, self.r, self.lands, self.token = _split_start(
            tag + "_swap_start", self.parts, lands, self.mk, len(names), after)

    def scatter(self, after=()):
        got = _split_wait(self.tag + "_swap_wait", self.parts, self.lands, self.s, self.r, self.mk, after)
        self.sums = [_add_half(f"{self.tag}_add_{n}", p, g, ax, self.cidx)
                     for n, p, g, ax in zip(self.names, self.parts, got, self.axes)]
        self.mk = _mk_scatter(self.axes)
        lands = []
        for q, ax in zip(self.sums, self.axes):
            k, n = q.shape
            lands.append(lax.empty((3, k // 4, n) if ax == 0 else (3, k, n // 4), BF16))
        self.s, self.r, self.lands, tok = _split_start(
            self.tag + "_scatter_start", self.sums, lands, self.mk, 3 * len(self.names))
        return tok

    def exchange(self, after=()):
        slots = _split_wait(self.tag + "_scatter_wait", self.sums, self.lands, self.s, self.r, self.mk, after)
        halves = [_sum_half(f"{self.tag}_sum_{n}", q, sl, ax, self.mcidx)
                  for n, q, sl, ax in zip(self.names, self.sums, slots, self.axes)]
        self.mk = _mk_exchange(self.axes)
        self.s, self.r, self.lands, tok = _split_start(
            self.tag + "_exchange_start", [], halves, self.mk, len(self.names))
        return tok

    def done(self, after=()):
        grads = _split_wait(self.tag + "_exchange_wait", [], self.lands, self.s, self.r, self.mk, after)
        return dict(zip(self.names, grads))


def _pack(arrs):
    flat = jnp.concatenate([a.reshape(-1) for a in arrs])
    n = flat.shape[0]
    pad = (-n) % (8 * LANE)
    return jnp.pad(flat, (0, pad)).reshape(-1, LANE)


def _unpack(packed, shapes):
    flat = packed.reshape(-1)
    out, off = [], 0
    for sh in shapes:
        n = math.prod(sh)
        out.append(flat[off:off + n].reshape(sh))
        off += n
    return out


def kernel(x, mem, w_in, b_gate, ln_v_g, ln_v_b, w_s, b_s, sinks, w_br_a, w_br_b, w_o, ln1_g, ln1_b, w_xq, w_xkv, w_xo, ln2_g, ln2_b, w_up, w_down, ln3_g, ln3_b, loss_target, m_w_in, m_b_gate, m_ln_v_g, m_ln_v_b, m_w_s, m_b_s, m_sinks, m_w_br_a, m_w_br_b, m_w_o, m_ln1_g, m_ln1_b, m_w_xq, m_w_xkv, m_w_xo, m_ln2_g, m_ln2_b, m_w_up, m_w_down, m_ln3_g, m_ln3_b, v_w_in, v_b_gate, v_ln_v_g, v_ln_v_b, v_w_s, v_b_s, v_sinks, v_w_br_a, v_w_br_b, v_w_o, v_ln1_g, v_ln1_b, v_w_xq, v_w_xkv, v_w_xo, v_ln2_g, v_ln2_b, v_w_up, v_w_down, v_ln3_g, v_ln3_b):
    env = dict(locals())
    wts = {n: env[n] for n in WEIGHTS}
    mom_m = {n: env["m_" + n] for n in WEIGHTS}
    mom_v = {n: env["v_" + n] for n in WEIGHTS}
    s, d = x.shape[1], x.shape[2]
    dff = 4 * w_up.shape[-1]
    iw = 4 * w_in.shape[-1]
    xf = x.reshape(s, d)
    tgt = loss_target.reshape(s, d)
    memf = mem.reshape(mem.shape[1], d)
    ax_x, ax_y, ax_c = lax.axis_index("x"), lax.axis_index("y"), lax.axis_index("c")
    meidx = jnp.reshape(2 * ax_x + ax_y, (1,)).astype(jnp.int32)
    cidx = jnp.reshape(ax_c, (1,)).astype(jnp.int32)
    mcidx = jnp.concatenate([meidx, cidx])

    inv = 1.0 / (10000.0 ** (jnp.arange(0, HD, 2, dtype=F32) / HD))
    ang = jnp.arange(s, dtype=F32)[:, None] * inv[None, :]
    cos, sin = jnp.cos(ang), jnp.sin(ang)
    cos4 = jnp.tile(cos, (1, 4))
    sin4 = jnp.concatenate([-sin, sin, -sin, sin], axis=-1)
    nsin4 = -sin4

    small = {}
    for n in SMALL:
        w = wts[n]
        if n == "w_s":
            small[n] = [w[l] for l in range(DEPTH)]
        elif n == "b_s":
            small["b_st"] = [w[l].T for l in range(DEPTH)]
        else:
            small[n] = [w[l][None, :] for l in range(DEPTH)]
    small["sink_rows"] = [jnp.repeat(sinks[l].reshape(NKV, GRP), CHUNK, axis=1)[:, None, :] for l in range(DEPTH)]

    shards, fulls = {}, [{}, {}]
    tok = ()
    gathers = [[None] * len(GROUPS_FWD) for _ in range(DEPTH)]
    for gi, names in enumerate(GROUPS_FWD):
        for n in names:
            shards[n] = _cast_bf16("cast_" + n, wts[n], after=tok)
            fulls[0][n], fulls[1][n] = _place_own("place_" + n, shards[n], SHARD_AXIS[n], meidx)
        gathers[0][gi] = _Gather(f"ag0_{gi}", 0, names, shards, fulls[0], tok)
        tok = (gathers[0][gi].token,)
    for gi, names in enumerate(GROUPS_FWD):
        gathers[1][gi] = _Gather(f"ag1_{gi}", 1, names, shards, fulls[1], tok)
        tok = (gathers[1][gi].token,)

    xb = _cast2d("cast_x", xf, after=tok)
    memb = _cast2d("cast_mem", memf, after=tok)

    saved = []
    hf, hb = xf, xb
    nxt_tok = gathers[0][0].forward(after=tok)
    for l in range(DEPTH):
        t = f"l{l}_"
        ga, gb, gc = gathers[l]
        full = ga.done(after=(nxt_tok, hb))
        sv = {"xf": hf, "xb": hb}
        proj = _mm_nn(t + "proj", hb, full["w_in"], out_dtypes=[F32], bn_pref=1280)[0]
        tok_b = gb.forward(after=(proj,))
        sg = _gmlp_fwd(t + "gmlp_fwd", proj, small["ln_v_g"][l], small["ln_v_b"][l], small["w_s"][l],
                       small["b_st"][l])
        attn, qr, kr, lse = _swa_fwd(t + "swa_fwd", proj, cos4, sin4, small["sink_rows"][l], after=(tok_b,))
        full.update(gb.done(after=(attn,)))
        merged, ya, yb = _gate_fwd(t + "gate_fwd", sg, attn, full["w_br_a"], full["w_br_b"], proj,
                                   small["b_gate"][l], d)
        tok_c = gc.forward(after=(merged,))
        bm = _tile(s, 512)
        row = pl.BlockSpec((bm, d), lambda i, j, k: (i, 0))
        r1, x1, x1b = _mm(
            t + "o_ln", merged, full["w_o"], dims=NN, grid=(s // bm, 1, 1),
            a_spec=row, b_spec=pl.BlockSpec((d, d), lambda i, j, k: (0, 0)),
            extras=(hf, small["ln1_g"][l], small["ln1_b"][l]), extra_specs=(row, _vec_spec(d), _vec_spec(d)),
            out_shape=[_sds((s, d), F32), _sds((s, d), F32), _sds((s, d), BF16)], out_specs=[row] * 3,
            epilogue=_ep_residual_ln, after=(tok_c,))
        kv = _mm_nn(t + "xkv", memb, full["w_xkv"], out_dtypes=[BF16])[0]
        q, o, r2, x2, x2b = _xattn_fwd(t + "xattn_fwd", x1b, x1, full["w_xq"], kv, full["w_xo"],
                                       small["ln2_g"][l], small["ln2_b"][l])
        full.update(gc.done(after=(x2b,)))

        def ep_up(acc, ex, outs):
            outs[0][...] = acc.astype(BF16)
            rl = jnp.maximum(acc, 0.0)
            outs[1][...] = (rl * rl).astype(BF16)

        h, a = _mm_nn(t + "up", x2b, full["w_up"], out_dtypes=[BF16, BF16], epilogue=ep_up)
        nxt_tok = gathers[l + 1][0].forward(after=(h,)) if l + 1 < DEPTH else None
        bk = _tile(dff, 1024)
        r3, x3, x3b = _mm(
            t + "down_ln", a, full["w_down"], dims=NN, grid=(s // bm, 1, dff // bk),
            a_spec=pl.BlockSpec((bm, bk), lambda i, j, k: (i, k)),
            b_spec=pl.BlockSpec((bk, d), lambda i, j, k: (k, 0)),
            extras=(x2, small["ln3_g"][l], small["ln3_b"][l]), extra_specs=(row, _vec_spec(d), _vec_spec(d)),
            out_shape=[_sds((s, d), F32), _sds((s, d), F32), _sds((s, d), BF16)], out_specs=[row] * 3,
            epilogue=_ep_residual_ln, acc_shape=(bm, d), after=() if nxt_tok is None else (nxt_tok,))
        sv.update(proj=proj, sg=sg, attn=attn, qr=qr, kr=kr, lse=lse, merged=merged, ya=ya, yb=yb, r1=r1, x1=x1,
                  x1b=x1b, kv=kv, q=q, o=o, r2=r2, x2b=x2b, h=h, a=a, r3=r3, full=full)
        saved.append(sv)
        hf, hb = x3, x3b
    dy, loss11 = _loss_grad("loss", hf, tgt)
    loss = lax.psum(loss11[0, 0], ("x", "y", "c"))

    small_g = [None] * DEPTH
    grads = [{}, {}]
    pend_a = None
    pend_b = None
    g = dy
    for l in reversed(range(DEPTH)):
        t = f"l{l}_"
        sv = saved[l]
        full = sv["full"]
        dw, sgo = {}, {}
        dr3, dr3b, sgo["ln3_g"], sgo["ln3_b"] = _ln_bwd(t + "ln3_bwd", g, sv["r3"], small["ln3_g"][l],
                                                        after=() if pend_a is None else (tok_a,))
        bm, bn = _tile(s, 1024), _tile(dff, 1024)

        def ep_dh(acc, ex, outs):
            outs[0][...] = (acc * (2.0 * jnp.maximum(ex[0][...].astype(F32), 0.0))).astype(BF16)

        tile = pl.BlockSpec((bm, bn), lambda i, j, k: (i, j))
        dh = _mm(t + "dh", dr3b, full["w_down"], dims=NT, grid=(s // bm, dff // bn, 1),
                 a_spec=pl.BlockSpec((bm, d), lambda i, j, k: (i, 0)),
                 b_spec=pl.BlockSpec((bn, d), lambda i, j, k: (j, 0)),
                 extras=(sv["h"],), extra_specs=(tile,), out_shape=[_sds((s, dff), BF16)], out_specs=[tile],
                 epilogue=ep_dh)[0]
        if pend_a is not None:
            tok_pa = pend_a.exchange(after=(dh,))
            grads[l + 1].update(pend_b.done(after=(dh,)))
        dw["w_down"] = _mm_tn(t + "dw_down", sv["a"], dr3b, after=() if pend_a is None else (tok_pa,))
        dw["w_up"] = _mm_tn(t + "dw_up", sv["x2b"], dh)
        red_c = _Reduce(t + "rs_c", GROUPS_FWD[2], dw, cidx, mcidx)
        bm2, bn2 = _tile(s, 512), _tile(d, 512)
        tile2 = pl.BlockSpec((bm2, bn2), lambda i, j, k: (i, j))
        dx2 = _mm(t + "dx2", dh, full["w_up"], dims=NT, grid=(s // bm2, d // bn2, 1),
                  a_spec=pl.BlockSpec((bm2, dff), lambda i, j, k: (i, 0)),
                  b_spec=pl.BlockSpec((bn2, dff), lambda i, j, k: (j, 0)),
                  extras=(dr3,), extra_specs=(tile2,), out_shape=[_sds((s, d), F32)], out_specs=[tile2],
                  epilogue=_ep_add_scaled, after=(red_c.token,))[0]
        tok_c = red_c.scatter(after=(dx2,))
        if pend_a is not None:
            grads[l + 1].update(pend_a.done(after=(dx2,)))
            pend_a = None

        dr2, dr2b, sgo["ln2_g"], sgo["ln2_b"] = _ln_bwd(t + "ln2_bwd", dx2, sv["r2"], small["ln2_g"][l],
                                                        after=(tok_c,))
        dx1, dq, dkv = _xattn_bwd(t + "xattn_bwd", dr2b, dr2, sv["q"], sv["kv"], full["w_xo"], full["w_xq"])
        dw["w_xo"] = _mm_tn(t + "dw_xo", sv["o"], dr2b)
        dw["w_xq"] = _mm_tn(t + "dw_xq", sv["x1b"], dq)
        dw["w_xkv"] = _mm_tn(t + "dw_xkv", memb, _cast2d(t + "dkv_cast", dkv))

        dr1, dr1b, sgo["ln1_g"], sgo["ln1_b"] = _ln_bwd(t + "ln1_bwd", dx1, sv["r1"], small["ln1_g"][l])
        dya, dyb, dgate, dba, dbb = _gate_bwd(t + "gate_bwd", dr1b, full["w_o"], sv["proj"], sv["ya"], sv["yb"],
                                              small["b_gate"][l], d)
        sgo["b_gate"] = jnp.concatenate([dba, dbb], axis=-1)
        tok_c = red_c.exchange(after=(dya,))
        dw["w_o"] = _mm_tn(t + "dw_o", sv["merged"], dr1b, after=(tok_c,))
        dw["w_br_a"] = _mm_tn(t + "dw_br_a", sv["sg"], dya)
        dw["w_br_b"] = _mm_tn(t + "dw_br_b", sv["attn"], dyb)
        red_b = _Reduce(t + "rs_b", GROUPS_FWD[1], dw, cidx, mcidx)

        def dbranch(name, dyx, w, after):
            bk2 = _tile(d, 1024)
            return _mm(name, dyx, w, dims=NT, grid=(s // bm, 1, d // bk2),
                       a_spec=pl.BlockSpec((bm, bk2), lambda i, j, k: (i, k)),
                       b_spec=pl.BlockSpec((w.shape[0], bk2), lambda i, j, k: (0, k)),
                       out_shape=[_sds((s, w.shape[0]), BF16)],
                       out_specs=[pl.BlockSpec((bm, w.shape[0]), lambda i, j, k: (i, 0))],
                       epilogue=_store, acc_shape=(bm, w.shape[0]), after=after)[0]

        dsg = dbranch(t + "dsg", dya, full["w_br_a"], (red_b.token,))
        dattn = dbranch(t + "dattn", dyb, full["w_br_b"], ())
        tok_b = red_b.scatter(after=(dattn,))
        grads[l].update(red_c.done(after=(dattn,)))
        duv, sgo["w_s"], dbst, dlg, dlb = _gmlp_bwd(t + "gmlp_bwd", sv["proj"], dsg, small["ln_v_g"][l],
                                                    small["ln_v_b"][l], small["w_s"][l], small["b_st"][l])
        sgo["b_s"] = dbst.T
        sgo["ln_v_g"], sgo["ln_v_b"] = dlg, dlb
        dqkv, sgo["sinks"] = _swa_bwd(t + "swa_bwd", sv["qr"], sv["kr"], sv["proj"], dattn, sv["attn"], sv["lse"],
                                      small["sink_rows"][l], cos4, nsin4, after=(tok_b,))
        dproj = jnp.concatenate([duv, dqkv, dgate[0], dgate[1]], axis=-1)
        tok_b = red_b.exchange(after=(dproj,))
        dw["w_in"] = _mm_tn(t + "dw_in", sv["xb"], dproj, after=(tok_b,))
        red_a = _Reduce(t + "rs_a", GROUPS_FWD[0], dw, cidx, mcidx)
        g = _mm(t + "dx0", dproj, full["w_in"], dims=NT, grid=(s // bm2, d // bn2, 1),
                a_spec=pl.BlockSpec((bm2, iw), lambda i, j, k: (i, 0)),
                b_spec=pl.BlockSpec((bn2, iw), lambda i, j, k: (j, 0)),
                extras=(dr1,), extra_specs=(tile2,), out_shape=[_sds((s, d), F32)], out_specs=[tile2],
                epilogue=_ep_add_scaled, after=(red_a.token,))[0]
        tok_a = red_a.scatter(after=(g,))
        pend_a, pend_b = red_a, red_b
        small_g[l] = sgo
    grad_x = g.reshape(x.shape)

    big_out = {}

    def adam_layer(l, names, after):
        done = []
        for n in names:
            prev = big_out.get(n)
            big_out[n] = _adamw(f"adamw{l}_{n}", wts[n], grads[l][n], mom_m[n], mom_v[n], l, prev, after=after)
            done.append(big_out[n][0])
        return done

    shapes = [wts[n].shape for n in SMALL]
    packed_g = _pack([jnp.stack([small_g[l][n].reshape(wts[n].shape[1:]) for l in range(DEPTH)]) for n in SMALL])
    me8 = jnp.reshape(4 * ax_x + 2 * ax_y + ax_c, (1,)).astype(jnp.int32)
    ar_s, ar_r, ar_land, tok_ar = _split_start("ar_start", [packed_g], [_place_slot("ar_place", packed_g, me8)],
                                               _mk_small, 7, after=(tok_a,))
    fill = []
    for names in GROUPS_FWD:
        fill += adam_layer(1, names, (tok_ar,))
    grads[0].update(pend_b.done(after=tuple(fill)))
    fill += adam_layer(0, GROUPS_FWD[1], (tok_ar,))
    ar_land = _split_wait("ar_wait", [packed_g], ar_land, ar_s, ar_r, _mk_small, after=tuple(fill))
    packed_g = _sum_slots("ar_sum", ar_land[0])
    pw, pm, pv = (_pack([src[n] for n in SMALL]) for src in (wts, mom_m, mom_v))
    small4 = _adamw("adamw_small", pw[None], packed_g, pm[None], pv[None], 0)
    small_out = [dict(zip(SMALL, _unpack(a[0], shapes))) for a in small4]
    tok_a = pend_a.exchange(after=(small4[0],))
    fill = adam_layer(0, GROUPS_FWD[2], (tok_a,))
    grads[0].update(pend_a.done(after=tuple(fill)))
    adam_layer(0, GROUPS_FWD[0], ())

    def pick(kind, n):
        return big_out[n][kind] if n in big_out else small_out[kind][n]

    return (loss, grad_x, *[pick(0, n) for n in WEIGHTS], *[pick(1, n) for n in WEIGHTS],
            *[pick(2, n) for n in WEIGHTS], *[pick(3, n) for n in WEIGHTS])
```

```python
import math

import jax
import jax.numpy as jnp
from jax import lax
from jax.experimental import pallas as pl
from jax.experimental.pallas import tpu as pltpu

F32 = jnp.float32
BF16 = jnp.bfloat16
MESH = pl.DeviceIdType.MESH
ANY = pl.BlockSpec(memory_space=pl.ANY)
HBM = pl.BlockSpec(memory_space=pltpu.HBM)
SEM = pl.BlockSpec(memory_space=pltpu.SEMAPHORE)
VMEM_SPEC = pl.BlockSpec(memory_space=pltpu.VMEM)
EFFECT = pltpu.SideEffectType.DATAFLOW_SIDE_EFFECTING

DEPTH = 2
CHUNK = 128
GMLP_W = 1024
GROUPS = 8
NQ, NKV, HD = 16, 4, 64
ATT_W = NQ * HD
KV_W = NKV * HD
XH, XHD = 4, 128
X_W = XH * XHD
LN_EPS = 1e-5
ALPHA = (2 * DEPTH) ** 0.25
OFF_Q = 2 * GMLP_W
OFF_K = OFF_Q + ATT_W
OFF_VA = OFF_K + KV_W
OFF_GA = OFF_VA + KV_W
NEG = -1e30

ADAM_LR, ADAM_B1, ADAM_B2, ADAM_EPS, ADAM_WD, ADAM_STEP = 0.001, 0.9, 0.999, 1e-08, 0.01, 10

V7X_VMEM_BYTES = 64 * 1024 * 1024
VMEM_LIMIT = V7X_VMEM_BYTES - 12 * 1024 * 1024
LANE = 128

BIG = ("w_in", "w_br_a", "w_br_b", "w_o", "w_xq", "w_xkv", "w_xo", "w_up", "w_down")
SHARD_AXIS = {"w_in": 1, "w_br_a": 1, "w_br_b": 1, "w_o": 0, "w_xq": 0, "w_xkv": 0, "w_xo": 1,
              "w_up": 1, "w_down": 0}
GROUPS_GATHER = (("w_in",), ("w_br_a", "w_br_b", "w_o", "w_xq", "w_xkv", "w_xo"), ("w_up",), ("w_down",))
GROUPS_FWD = (("w_in",), ("w_br_a", "w_br_b", "w_o", "w_xq", "w_xkv", "w_xo"), ("w_up", "w_down"))
SMALL = ("b_gate", "ln_v_g", "ln_v_b", "w_s", "b_s", "sinks", "ln1_g", "ln1_b", "ln2_g", "ln2_b",
         "ln3_g", "ln3_b")
WEIGHTS = ("w_in", "b_gate", "ln_v_g", "ln_v_b", "w_s", "b_s", "sinks", "w_br_a", "w_br_b", "w_o",
           "ln1_g", "ln1_b", "w_xq", "w_xkv", "w_xo", "ln2_g", "ln2_b", "w_up", "w_down", "ln3_g", "ln3_b")


def _pallas(body, after=(), **kw):
    n_after = len(after)
    if not n_after:
        return pl.pallas_call(body, **kw)
    n_in = len(kw["in_specs"])
    kw["in_specs"] = list(kw["in_specs"]) + [ANY] * n_after

    def tied(*refs):
        return body(*refs[:n_in], *refs[n_in + n_after:])

    call = pl.pallas_call(tied, **kw)
    return lambda *ops: call(*ops, *after)


def _params(**kw):
    return pltpu.CompilerParams(vmem_limit_bytes=VMEM_LIMIT, **kw)


def _tile(dim, pref, unit=LANE):
    best = None
    t = unit
    while t <= min(dim, pref):
        if dim % t == 0:
            best = t
        t += unit
    return best if best is not None else dim


def _dot(a, b, dims):
    return lax.dot_general(a, b, (dims, ((), ())), preferred_element_type=F32)


NN = ((1,), (0,))
NT = ((1,), (1,))
TN = ((0,), (0,))


def _bf(x):
    return x if x.dtype == BF16 else x.astype(BF16)


def _sds(shape, dtype):
    return jax.ShapeDtypeStruct(shape, dtype)


def _mm(name, a, b, *, dims, grid, a_spec, b_spec, out_shape, out_specs, epilogue,
        extras=(), extra_specs=(), acc_shape=None, after=()):
    nk = grid[2]
    n_ex, n_out = len(extras), len(out_shape)

    def body(*refs):
        a_ref, b_ref = refs[0], refs[1]
        ex = refs[2:2 + n_ex]
        outs = refs[2 + n_ex:2 + n_ex + n_out]
        part = _dot(_bf(a_ref[...]), _bf(b_ref[...]), dims)
        if nk == 1:
            epilogue(part, ex, outs)
        else:
            acc = refs[-1]
            k = pl.program_id(2)

            @pl.when(k == 0)
            def _():
                acc[...] = part

            @pl.when(k > 0)
            def _():
                acc[...] += part

            @pl.when(k == nk - 1)
            def _():
                epilogue(acc[...], ex, outs)

    scratch = [pltpu.VMEM(acc_shape, F32)] if nk > 1 else []
    return _pallas(
        body, after=after, name=name, grid=grid, in_specs=[a_spec, b_spec, *extra_specs], out_specs=list(out_specs),
        out_shape=list(out_shape), scratch_shapes=scratch,
        compiler_params=_params(dimension_semantics=("arbitrary",) * 3),
    )(a, b, *extras)


def _store(acc, ex, outs):
    for o in outs:
        o[...] = acc.astype(o.dtype)


def _ln_rows(r, g, b):
    mu = jnp.mean(r, axis=-1, keepdims=True)
    xc = r - mu
    var = jnp.mean(xc * xc, axis=-1, keepdims=True)
    rstd = lax.rsqrt(var + LN_EPS)
    xhat = xc * rstd
    return xhat * g + b, xhat, rstd


def _ep_add_scaled(acc, ex, outs):
    outs[0][...] = acc + ALPHA * ex[0][...]


def _ln_fwd(name, r, g, b):
    s, d = r.shape
    bm = _tile(s, 256)

    def body(r_ref, g_ref, b_ref, y_ref, yb_ref):
        y, _, _ = _ln_rows(r_ref[...], g_ref[...], b_ref[...])
        y_ref[...] = y
        yb_ref[...] = y.astype(BF16)

    row = pl.BlockSpec((bm, d), lambda i: (i, 0))
    vec = pl.BlockSpec((1, d), lambda i: (0, 0))
    return _pallas(body, name=name, grid=(s // bm,), in_specs=[row, vec, vec], out_specs=[row, row],
                   out_shape=[_sds((s, d), F32), _sds((s, d), BF16)], compiler_params=_params())(r, g, b)


_GC = math.sqrt(2.0 / math.pi)


def _gelu(x):
    t = jnp.tanh(_GC * (x + 0.044715 * (x * x * x)))
    return 0.5 * x * (1.0 + t), t


def _gelu_grad(x, t):
    return 0.5 * (1.0 + t) + 0.5 * x * (1.0 - t * t) * (_GC * (1.0 + 3.0 * 0.044715 * x * x))


def _sigmoid(x):
    return 1.0 / (1.0 + jnp.exp(-x))


GRP = NQ // NKV


def _band_mask(prev_ok, prev_only=False):
    rows = CHUNK if prev_only else 2 * CHUNK
    key = lax.broadcasted_iota(jnp.int32, (rows, GRP * CHUNK), 0)
    qry = jnp.bitwise_and(lax.broadcasted_iota(jnp.int32, (rows, GRP * CHUNK), 1), CHUNK - 1)
    prev = jnp.logical_and(jnp.logical_and(key < CHUNK, key > qry), prev_ok)
    if prev_only:
        return prev
    return jnp.logical_or(prev, jnp.logical_and(key >= CHUNK, key - CHUNK <= qry))


def _pair(x, g):
    return x[:, (g // 2) * LANE:(g // 2 + 1) * LANE]


def _own_head(x, g):
    xp = _pair(x, g)
    lane = lax.broadcasted_iota(jnp.int32, xp.shape, 1)
    lo = (g % 2) * HD
    return jnp.where(jnp.logical_and(lane >= lo, lane < lo + HD), xp, jnp.zeros_like(xp))


def _stack_heads(x, g, dtype=BF16):
    a = x[:, g * GRP * HD:g * GRP * HD + LANE]
    b = x[:, g * GRP * HD + LANE:(g + 1) * GRP * HD]
    ar, br = pltpu.roll(a, HD, 1), pltpu.roll(b, HD, 1)
    parts = [a, ar, b, br] if g % 2 == 0 else [ar, a, br, b]
    return jnp.concatenate(parts, axis=0).astype(dtype)


def _unstack_heads(og, g):
    o = [og[h * CHUNK:(h + 1) * CHUNK] for h in range(GRP)]
    lo = lax.broadcasted_iota(jnp.int32, (CHUNK, LANE), 1) < HD
    if g % 2 == 0:
        x0, x1, x2, x3 = o[0], pltpu.roll(o[1], HD, 1), o[2], pltpu.roll(o[3], HD, 1)
    else:
        x0, x1, x2, x3 = pltpu.roll(o[0], HD, 1), o[1], pltpu.roll(o[2], HD, 1), o[3]
    return [jnp.where(lo, x0, x1), jnp.where(lo, x2, x3)]


def _stack_rows(x, g):
    return jnp.concatenate([x[g * GRP + h:g * GRP + h + 1] for h in range(GRP)], axis=-1)


def _head_lane_sums(x, g):
    lane = lax.broadcasted_iota(jnp.int32, (8, LANE), 1)
    lo_lane = (g % 2) * HD
    sel = jnp.where(jnp.logical_and(lane >= lo_lane, lane < lo_lane + HD), 1.0, 0.0).astype(BF16)
    hi = x.astype(BF16)
    lo = (x - hi.astype(F32)).astype(BF16)
    return (_dot(sel, hi, NT) + _dot(sel, lo, NT))[0:1]


def _rope(x, cos, sin_signed):
    w = x.shape[-1]
    lane = lax.broadcasted_iota(jnp.int32, x.shape, 1)
    first = (lane % HD) < (HD // 2)
    partner = jnp.where(first, pltpu.roll(x, w - HD // 2, 1), pltpu.roll(x, HD // 2, 1))
    reps = w // LANE
    return x * jnp.tile(cos, (1, reps)) + partner * jnp.tile(sin_signed, (1, reps))


def _cast_bf16(name, w, after=()):
    _, r, c = w.shape
    br = _tile(r, 512, 8)

    def body(w_ref, o_ref):
        o_ref[...] = w_ref[...].astype(BF16)

    spec = pl.BlockSpec((None, br, c), lambda l, i: (l, i, 0))
    return _pallas(body, after=after, name=name, grid=(2, r // br), in_specs=[spec], out_specs=spec,
                   out_shape=_sds(w.shape, BF16), compiler_params=_params())(w)


def _cast2d(name, x, after=()):
    s, d = x.shape
    bm = _tile(s, 512, 8)

    def body(x_ref, o_ref):
        o_ref[...] = x_ref[...].astype(BF16)

    spec = pl.BlockSpec((bm, d), lambda i: (i, 0))
    return _pallas(body, after=after, name=name, grid=(s // bm,), in_specs=[spec], out_specs=spec,
                   out_shape=_sds(x.shape, BF16), compiler_params=_params())(x)


def _place():
    x, y, c = lax.axis_index("x"), lax.axis_index("y"), lax.axis_index("c")
    chips = [(1 - x, y), (x, 1 - y), (1 - x, 1 - y)]
    return x, y, c, chips


def _cut(ref, axis, chip=None, half=None, lead=()):
    k, n = ref.shape[-2], ref.shape[-1]
    rows, cols = slice(None), slice(None)
    if chip is not None:
        if axis == 0:
            rows = pl.ds(pl.multiple_of(chip * (k // 4), 8), k // 4)
        else:
            cols = pl.ds(pl.multiple_of(chip * (n // 4), LANE), n // 4)
    if half is not None:
        if axis == 0:
            cols = pl.ds(pl.multiple_of(half * (n // 2), LANE), n // 2)
        else:
            rows = pl.ds(pl.multiple_of(half * (k // 2), 8), k // 2)
    return ref.at[(*lead, rows, cols)]


def _split_start(name, srcs, lands, make, n_sem, after=()):
    ns, nl, na = len(srcs), len(lands), len(after)

    def body(*refs):
        src, land = refs[:ns], refs[ns:ns + nl]
        outs = refs[ns + nl + na:]
        for out_cp, _ in make(src, land, outs[0], outs[1]):
            out_cp.start()
        outs[-1][...] = jnp.zeros_like(outs[-1])

    res = pl.pallas_call(
        body, name=name, in_specs=[HBM] * (ns + nl) + [ANY] * na,
        out_specs=[SEM, SEM] + [HBM] * nl + [VMEM_SPEC],
        out_shape=[pltpu.SemaphoreType.DMA((n_sem,)), pltpu.SemaphoreType.DMA((n_sem,))]
        + [pltpu.HBM(a.shape, a.dtype) for a in lands] + [_sds((8, LANE), F32)],
        input_output_aliases={ns + i: 2 + i for i in range(nl)},
        compiler_params=pltpu.CompilerParams(has_side_effects=EFFECT),
    )(*[pltpu.with_memory_space_constraint(a, pltpu.HBM) for a in (*srcs, *lands)], *after)
    return res[0], res[1], list(res[2:2 + nl]), res[-1]


def _split_wait(name, srcs, lands, ssem, rsem, make, after=()):
    ns, nl, na = len(srcs), len(lands), len(after)

    def body(*refs):
        src, land = refs[:ns], refs[ns:ns + nl]
        s_ref, r_ref = refs[ns + nl], refs[ns + nl + 1]
        pairs = make(src, land, s_ref, r_ref)
        for _, in_cp in pairs:
            in_cp.wait_recv()
        for out_cp, _ in pairs:
            out_cp.wait_send()

    res = pl.pallas_call(
        body, name=name, in_specs=[HBM] * (ns + nl) + [SEM, SEM] + [ANY] * na,
        out_specs=[HBM] * nl, out_shape=[pltpu.HBM(a.shape, a.dtype) for a in lands],
        input_output_aliases={ns + i: i for i in range(nl)},
        compiler_params=pltpu.CompilerParams(has_side_effects=EFFECT),
    )(*srcs, *lands, ssem, rsem, *after)
    return list(res)


def _rcopy(src, dst, ssem, rsem, k, dev):
    return pltpu.make_async_remote_copy(src_ref=src, dst_ref=dst, send_sem=ssem.at[k], recv_sem=rsem.at[k],
                                        device_id=dev, device_id_type=MESH)


def _mk_gather_ici(layer, axes):
    def make(src, land, ssem, rsem):
        x, y, c, chips = _place()
        me = 2 * x + y
        pairs = []
        for w, ax in enumerate(axes):
            mine = _cut(src[w], ax, half=c, lead=(layer,))
            for j, (px, py) in enumerate(chips):
                dev = (px, py, c)
                out_cp = _rcopy(mine, _cut(land[w], ax, chip=me, half=c), ssem, rsem, 3 * w + j, dev)
                got = _cut(land[w], ax, chip=2 * px + py, half=c)
                pairs.append((out_cp, _rcopy(got, got, ssem, rsem, 3 * w + j, dev)))
        return pairs
    return make


def _mk_gather_d2d(axes):
    def make(src, land, ssem, rsem):
        x, y, c, chips = _place()
        sib = (x, y, 1 - c)
        pairs = []
        for w, ax in enumerate(axes):
            for j, (px, py) in enumerate(chips):
                have = _cut(land[w], ax, chip=2 * px + py, half=c)
                want = _cut(land[w], ax, chip=2 * px + py, half=1 - c)
                pairs.append((_rcopy(have, have, ssem, rsem, 3 * w + j, sib),
                              _rcopy(want, want, ssem, rsem, 3 * w + j, sib)))
        return pairs
    return make


def _mk_swap(axes):
    def make(src, land, ssem, rsem):
        x, y, c, _ = _place()
        sib = (x, y, 1 - c)
        pairs = []
        for w, ax in enumerate(axes):
            cp = _rcopy(_cut(src[w], ax, half=1 - c), land[w], ssem, rsem, w, sib)
            pairs.append((cp, cp))
        return pairs
    return make


def _mk_scatter(axes):
    def make(src, land, ssem, rsem):
        x, y, c, chips = _place()
        pairs = []
        for w, ax in enumerate(axes):
            for j, (px, py) in enumerate(chips):
                cp = _rcopy(_cut(src[w], ax, chip=2 * px + py), land[w].at[j], ssem, rsem, 3 * w + j, (px, py, c))
                pairs.append((cp, cp))
        return pairs
    return make


def _mk_exchange(axes):
    def make(src, land, ssem, rsem):
        x, y, c, _ = _place()
        sib = (x, y, 1 - c)
        pairs = []
        for w, ax in enumerate(axes):
            have = _cut(land[w], ax, half=c)
            want = _cut(land[w], ax, half=1 - c)
            pairs.append((_rcopy(have, have, ssem, rsem, w, sib), _rcopy(want, want, ssem, rsem, w, sib)))
        return pairs
    return make


def _place_own(name, shard, axis, meidx):
    _, r, c = shard.shape
    full = (4 * r, c) if axis == 0 else (r, 4 * c)
    br = _tile(r, 512, 8)
    nb = r // br
    if axis == 0:
        ospec = pl.BlockSpec((br, c), lambda i, me: (me[0] * nb + i, 0))
    else:
        ospec = pl.BlockSpec((br, c), lambda i, me: (i, me[0]))

    def body(me_ref, s_ref, o0_ref, o1_ref):
        o0_ref[...] = s_ref[0]
        o1_ref[...] = s_ref[1]

    return pl.pallas_call(
        body, name=name,
        grid_spec=pltpu.PrefetchScalarGridSpec(
            num_scalar_prefetch=1, grid=(nb,),
            in_specs=[pl.BlockSpec((2, br, c), lambda i, me: (0, i, 0))], out_specs=[ospec, ospec]),
        out_shape=[_sds(full, BF16)] * 2, compiler_params=_params(),
    )(meidx, shard)


def _add_half(name, part, got, axis, cidx):
    k, n = got.shape
    bm = _tile(k, 512, 8)
    nb = k // bm
    if axis == 0:
        pspec = pl.BlockSpec((bm, n), lambda i, c: (i, c[0]))
    else:
        pspec = pl.BlockSpec((bm, n), lambda i, c: (c[0] * nb + i, 0))

    def body(c_ref, a_ref, b_ref, o_ref):
        o_ref[...] = (a_ref[...].astype(F32) + b_ref[...].astype(F32)).astype(BF16)

    return pl.pallas_call(
        body, name=name,
        grid_spec=pltpu.PrefetchScalarGridSpec(
            num_scalar_prefetch=1, grid=(nb,), in_specs=[pspec, pl.BlockSpec((bm, n), lambda i, c: (i, 0))],
            out_specs=pl.BlockSpec((bm, n), lambda i, c: (i, 0))),
        out_shape=_sds((k, n), BF16), compiler_params=_params(),
    )(cidx, part, got)


def _sum_half(name, own, slots, axis, mc):
    _, r, cc = slots.shape
    br = _tile(r, 256, 8)
    nb = r // br
    if axis == 0:
        own_spec = pl.BlockSpec((br, cc), lambda i, mc: (mc[0] * nb + i, 0))
        out_spec = pl.BlockSpec((br, cc), lambda i, mc: (i, mc[1]))
        shape = (r, 2 * cc)
    else:
        own_spec = pl.BlockSpec((br, cc), lambda i, mc: (i, mc[0]))
        out_spec = pl.BlockSpec((br, cc), lambda i, mc: (mc[1] * nb + i, 0))
        shape = (2 * r, cc)

    def body(mc_ref, own_ref, s_ref, o_ref):
        acc = own_ref[...].astype(F32)
        for i in range(3):
            acc = acc + s_ref[i].astype(F32)
        o_ref[...] = acc

    return pl.pallas_call(
        body, name=name,
        grid_spec=pltpu.PrefetchScalarGridSpec(
            num_scalar_prefetch=1, grid=(nb,),
            in_specs=[own_spec, pl.BlockSpec((3, br, cc), lambda i, mc: (0, i, 0))], out_specs=out_spec),
        out_shape=_sds(shape, F32), compiler_params=_params(),
    )(mc, own, slots)


def _mk_small(src, land, ssem, rsem):
    x, y, c, _ = _place()
    me = 4 * x + 2 * y + c
    pairs = []
    for k in range(1, 8):
        peer = (1 - x if k & 4 else x, 1 - y if k & 2 else y, 1 - c if k & 1 else c)
        got = land[0].at[4 * peer[0] + 2 * peer[1] + peer[2]]
        pairs.append((_rcopy(src[0], land[0].at[me], ssem, rsem, k - 1, peer),
                      _rcopy(got, got, ssem, rsem, k - 1, peer)))
    return pairs


def _place_slot(name, packed, me8):
    rows, lanes = packed.shape
    br = _tile(rows, 512, 8)

    def body(me_ref, p_ref, o_ref):
        o_ref[...] = p_ref[...]

    return pl.pallas_call(
        body, name=name,
        grid_spec=pltpu.PrefetchScalarGridSpec(
            num_scalar_prefetch=1, grid=(rows // br,),
            in_specs=[pl.BlockSpec((br, lanes), lambda i, me: (i, 0))],
            out_specs=pl.BlockSpec((None, br, lanes), lambda i, me: (me[0], i, 0))),
        out_shape=_sds((8, rows, lanes), F32), compiler_params=_params(),
    )(me8, packed)


def _sum_slots(name, slots):
    _, rows, lanes = slots.shape
    br = _tile(rows, 512, 8)

    def body(s_ref, o_ref):
        acc = s_ref[0]
        for i in range(1, 8):
            acc = acc + s_ref[i]
        o_ref[...] = acc

    return pl.pallas_call(
        body, name=name, grid=(rows // br,), in_specs=[pl.BlockSpec((8, br, lanes), lambda i: (0, i, 0))],
        out_specs=pl.BlockSpec((br, lanes), lambda i: (i, 0)), out_shape=_sds((rows, lanes), F32),
        compiler_params=_params(),
    )(slots)


def _adamw_math(w, g, m, v):
    m2 = ADAM_B1 * m + (1.0 - ADAM_B1) * g
    v2 = ADAM_B2 * v + (1.0 - ADAM_B2) * (g * g)
    m_hat = m2 / (1.0 - ADAM_B1 ** ADAM_STEP)
    v_hat = v2 / (1.0 - ADAM_B2 ** ADAM_STEP)
    delta = -ADAM_LR * (m_hat / (jnp.sqrt(v_hat) + ADAM_EPS) + ADAM_WD * w)
    return delta, m2, v2


def _adamw(name, w, g, m, v, layer, prev=None, after=()):
    _, r, c = w.shape
    br = _tile(r, 256, 8)
    n_prev = 0 if prev is None else 4

    def body(*refs):
        w_ref, g_ref, m_ref, v_ref = refs[:4]
        go_ref, d_ref, mo_ref, vo_ref = refs[4 + n_prev:]
        gg = g_ref[...]
        delta, m2, v2 = _adamw_math(w_ref[...], gg, m_ref[...], v_ref[...])
        go_ref[...] = gg
        d_ref[...] = delta
        mo_ref[...] = m2
        vo_ref[...] = v2

    spec = pl.BlockSpec((None, br, c), lambda i: (layer, i, 0))
    return _pallas(
        body, after=after, name=name, grid=(r // br,),
        in_specs=[spec, pl.BlockSpec((br, c), lambda i: (i, 0)), spec, spec] + [ANY] * n_prev,
        out_specs=[spec] * 4, out_shape=[_sds(w.shape, F32)] * 4,
        input_output_aliases={4 + i: i for i in range(n_prev)}, compiler_params=_params(),
    )(w, g, m, v, *(prev or ()))


def _gmlp_fwd(name, proj, ln_g, ln_b, w_s, b_st):
    s = proj.shape[0]

    def body(u_ref, v_ref, g_ref, b_ref, ws_ref, bst_ref, sg_ref):
        gu, _ = _gelu(u_ref[...])
        gv, _ = _gelu(v_ref[...])
        vn, _, _ = _ln_rows(gv, g_ref[...], b_ref[...])
        vn = vn.astype(BF16)
        row = lax.broadcasted_iota(jnp.int32, (CHUNK, CHUNK), 0)
        col = lax.broadcasted_iota(jnp.int32, (CHUNK, CHUNK), 1)
        tril = col <= row
        outs = []
        for g in range(GROUPS):
            sl = slice(g * LANE, (g + 1) * LANE)
            w = jnp.where(tril, ws_ref[g], 0.0).astype(BF16)
            mixed = _dot(w, vn[:, sl], NN) + bst_ref[:, g:g + 1]
            outs.append(gu[:, sl] * mixed)
        sg_ref[...] = jnp.concatenate(outs, axis=-1).astype(BF16)

    return _pallas(
        body, name=name, grid=(s // CHUNK,),
        in_specs=[pl.BlockSpec((CHUNK, GMLP_W), lambda n: (n, 0)), pl.BlockSpec((CHUNK, GMLP_W), lambda n: (n, 1)),
                  pl.BlockSpec((1, GMLP_W), lambda n: (0, 0)), pl.BlockSpec((1, GMLP_W), lambda n: (0, 0)),
                  pl.BlockSpec((GROUPS, CHUNK, CHUNK), lambda n: (0, 0, 0)),
                  pl.BlockSpec((CHUNK, GROUPS), lambda n: (0, 0))],
        out_specs=pl.BlockSpec((CHUNK, GMLP_W), lambda n: (n, 0)),
        out_shape=_sds((s, GMLP_W), BF16), compiler_params=_params(),
    )(proj, proj, ln_g, ln_b, w_s, b_st)


def _swa_fwd(name, proj, cos4, sin4, sinks, after=()):
    s = proj.shape[0]
    w = CHUNK
    scale = HD ** -0.5

    def body(q_ref, k_ref, v_ref, cos_ref, sin_ref, sink_ref, o_ref, qr_ref, kr_ref, lse_ref, kprev, vprev):
        n = pl.program_id(0)

        @pl.when(n == 0)
        def _():
            kprev[...] = jnp.zeros_like(kprev)
            vprev[...] = jnp.zeros_like(vprev)

        cos, sin = cos_ref[...], sin_ref[...]
        qr = _rope(q_ref[...], cos, sin)
        kr = _rope(k_ref[...], cos, sin).astype(BF16)
        vb = v_ref[...].astype(BF16)
        kk = jnp.concatenate([kprev[...], kr], axis=0)
        vv = jnp.concatenate([vprev[...], vb], axis=0)
        valid = _band_mask(n > 0)
        outs, lses = [], []
        for g in range(NKV):
            sc = jnp.where(valid, _dot(_own_head(kk, g), _stack_heads(qr, g), NT) * scale, NEG)
            sink = sink_ref[g]
            mx = jnp.maximum(jnp.max(sc, axis=0, keepdims=True), sink)
            p = jnp.exp(sc - mx)
            den = jnp.sum(p, axis=0, keepdims=True) + jnp.exp(sink - mx)
            og = _dot((p * (1.0 / den)).astype(BF16), _pair(vv, g), TN)
            outs.extend(_unstack_heads(og, g))
            lg = mx + jnp.log(den)
            lses.extend([lg[:, h * w:(h + 1) * w] for h in range(GRP)])
        o_ref[...] = jnp.concatenate(outs, axis=-1).astype(BF16)
        lse_ref[...] = jnp.concatenate(lses, axis=0)
        qr_ref[...] = qr.astype(BF16)
        kr_ref[...] = kr
        kprev[...] = kr
        vprev[...] = vb

    return _pallas(
        body, after=after, name=name, grid=(s // w,),
        in_specs=[pl.BlockSpec((w, ATT_W), lambda n: (n, OFF_Q // ATT_W)),
                  pl.BlockSpec((w, KV_W), lambda n: (n, OFF_K // KV_W)),
                  pl.BlockSpec((w, KV_W), lambda n: (n, OFF_VA // KV_W)),
                  pl.BlockSpec((w, LANE), lambda n: (n, 0)), pl.BlockSpec((w, LANE), lambda n: (n, 0)),
                  pl.BlockSpec((NKV, 1, GRP * w), lambda n: (0, 0, 0))],
        out_specs=[pl.BlockSpec((w, ATT_W), lambda n: (n, 0)), pl.BlockSpec((w, ATT_W), lambda n: (n, 0)),
                   pl.BlockSpec((w, KV_W), lambda n: (n, 0)), pl.BlockSpec((None, NQ, w), lambda n: (n, 0, 0))],
        out_shape=[_sds((s, ATT_W), BF16), _sds((s, ATT_W), BF16), _sds((s, KV_W), BF16),
                   _sds((s // w, NQ, w), F32)],
        scratch_shapes=[pltpu.VMEM((w, KV_W), BF16), pltpu.VMEM((w, KV_W), BF16)],
        compiler_params=_params(dimension_semantics=("arbitrary",)),
    )(proj, proj, proj, cos4, sin4, sinks)


def _gate_fwd(name, sg, attn, wa, wb, proj, b_gate, d):
    s = sg.shape[0]
    bm, bn = _tile(s, 1024), _tile(d, 512)
    off_a, off_b = OFF_GA // bn, (OFF_GA + d) // bn

    def body(sg_ref, at_ref, wa_ref, wb_ref, ga_ref, gb_ref, ba_ref, bb_ref, m_ref, ya_ref, yb_ref):
        ya = _dot(sg_ref[...], wa_ref[...], NN)
        yb = _dot(at_ref[...], wb_ref[...], NN)
        sa = _sigmoid(ga_ref[...] + ba_ref[...])
        sb = _sigmoid(gb_ref[...] + bb_ref[...])
        m_ref[...] = (sa * ya + sb * yb).astype(BF16)
        ya_ref[...] = ya.astype(BF16)
        yb_ref[...] = yb.astype(BF16)

    tile = pl.BlockSpec((bm, bn), lambda i, j: (i, j))
    return _pallas(
        body, name=name, grid=(s // bm, d // bn),
        in_specs=[pl.BlockSpec((bm, GMLP_W), lambda i, j: (i, 0)), pl.BlockSpec((bm, ATT_W), lambda i, j: (i, 0)),
                  pl.BlockSpec((GMLP_W, bn), lambda i, j: (0, j)), pl.BlockSpec((ATT_W, bn), lambda i, j: (0, j)),
                  pl.BlockSpec((bm, bn), lambda i, j: (i, off_a + j)),
                  pl.BlockSpec((bm, bn), lambda i, j: (i, off_b + j)),
                  pl.BlockSpec((1, bn), lambda i, j: (0, j)), pl.BlockSpec((1, bn), lambda i, j: (0, d // bn + j))],
        out_specs=[tile, tile, tile], out_shape=[_sds((s, d), BF16)] * 3,
        compiler_params=_params(),
    )(sg, attn, wa, wb, proj, proj, b_gate, b_gate)


def _xattn_fwd(name, xb, xf, wq, kv, wo, ln_g, ln_b, after=()):
    s, d = xf.shape
    mem = kv.shape[0]
    bm = _tile(s, 512)
    scale = XHD ** -0.5

    def body(xb_ref, xf_ref, wq_ref, kv_ref, wo_ref, g_ref, b_ref, q_out, o_out, r_out, y_out, yb_out):
        qb = _dot(xb_ref[...], wq_ref[...], NN).astype(BF16)
        kvv = kv_ref[...]
        outs = []
        for h in range(XH):
            hs = slice(h * XHD, (h + 1) * XHD)
            vs = slice(X_W + h * XHD, X_W + (h + 1) * XHD)
            sc = _dot(qb[:, hs], kvv[:, hs], NT) * scale
            mx = jnp.max(sc, axis=-1, keepdims=True)
            p = jnp.exp(sc - mx)
            p = p / jnp.sum(p, axis=-1, keepdims=True)
            outs.append(_dot(p.astype(BF16), kvv[:, vs], NN))
        ob = jnp.concatenate(outs, axis=-1).astype(BF16)
        yv = _dot(ob, wo_ref[...], NN)
        r = ALPHA * xf_ref[...] + yv
        yn, _, _ = _ln_rows(r, g_ref[...], b_ref[...])
        q_out[...] = qb
        o_out[...] = ob
        r_out[...] = r
        y_out[...] = yn
        yb_out[...] = yn.astype(BF16)

    row = lambda wd: pl.BlockSpec((bm, wd), lambda i: (i, 0))
    return _pallas(
        body, after=after, name=name, grid=(s // bm,),
        in_specs=[row(d), row(d), pl.BlockSpec((d, X_W), lambda i: (0, 0)),
                  pl.BlockSpec((mem, 2 * X_W), lambda i: (0, 0)), pl.BlockSpec((X_W, d), lambda i: (0, 0)),
                  pl.BlockSpec((1, d), lambda i: (0, 0)), pl.BlockSpec((1, d), lambda i: (0, 0))],
        out_specs=[row(X_W), row(X_W), row(d), row(d), row(d)],
        out_shape=[_sds((s, X_W), BF16), _sds((s, X_W), BF16), _sds((s, d), F32), _sds((s, d), F32),
                   _sds((s, d), BF16)],
        compiler_params=_params(),
    )(xb, xf, wq, kv, wo, ln_g, ln_b)


def _loss_grad(name, y, tgt):
    s, d = y.shape
    bm = _tile(s, 512)

    def body(y_ref, t_ref, dy_ref, loss_ref):
        i = pl.program_id(0)
        err = y_ref[...] - t_ref[...]
        dy_ref[...] = err * (1.0 / d)
        part = 0.5 * jnp.sum(jnp.sum(err * err, axis=-1, keepdims=True), axis=0, keepdims=True) * (1.0 / d)

        @pl.when(i == 0)
        def _():
            loss_ref[...] = part

        @pl.when(i > 0)
        def _():
            loss_ref[...] += part

    row = pl.BlockSpec((bm, d), lambda i: (i, 0))
    return _pallas(
        body, name=name, grid=(s // bm,), in_specs=[row, row],
        out_specs=[row, pl.BlockSpec((1, 1), lambda i: (0, 0))],
        out_shape=[_sds((s, d), F32), _sds((1, 1), F32)],
        compiler_params=_params(dimension_semantics=("arbitrary",)),
    )(y, tgt)


def _ln_bwd(name, dy, r, g, after=()):
    s, d = r.shape
    bm = _tile(s, 256)

    def body(dy_ref, r_ref, g_ref, dr_ref, drb_ref, dg_ref, db_ref):
        i = pl.program_id(0)
        dyv = dy_ref[...]
        _, xhat, rstd = _ln_rows(r_ref[...], g_ref[...], 0.0)
        dxh = dyv * g_ref[...]
        m1 = jnp.mean(dxh, axis=-1, keepdims=True)
        m2 = jnp.mean(dxh * xhat, axis=-1, keepdims=True)
        dr = rstd * (dxh - m1 - xhat * m2)
        dr_ref[...] = dr
        drb_ref[...] = dr.astype(BF16)
        dg = jnp.sum(dyv * xhat, axis=0, keepdims=True)
        db = jnp.sum(dyv, axis=0, keepdims=True)

        @pl.when(i == 0)
        def _():
            dg_ref[...] = dg
            db_ref[...] = db

        @pl.when(i > 0)
        def _():
            dg_ref[...] += dg
            db_ref[...] += db

    row = pl.BlockSpec((bm, d), lambda i: (i, 0))
    vec = pl.BlockSpec((1, d), lambda i: (0, 0))
    return _pallas(
        body, after=after, name=name, grid=(s // bm,), in_specs=[row, row, vec], out_specs=[row, row, vec, vec],
        out_shape=[_sds((s, d), F32), _sds((s, d), BF16), _sds((1, d), F32), _sds((1, d), F32)],
        compiler_params=_params(dimension_semantics=("arbitrary",)),
    )(dy, r, g)


def _xattn_bwd(name, dyb, drf, q, kv, wo, wq):
    s, d = drf.shape
    mem = kv.shape[0]
    bm = _tile(s, 512)
    scale = XHD ** -0.5

    def body(dy_ref, dr_ref, q_ref, kv_ref, wo_ref, wq_ref, dx_out, dq_out, dkv_out):
        i = pl.program_id(0)
        dob = _dot(dy_ref[...], wo_ref[...], NT).astype(BF16)
        qb = q_ref[...]
        kvv = kv_ref[...]
        dqs, dks, dvs = [], [], []
        for h in range(XH):
            hs = slice(h * XHD, (h + 1) * XHD)
            vs = slice(X_W + h * XHD, X_W + (h + 1) * XHD)
            sc = _dot(qb[:, hs], kvv[:, hs], NT) * scale
            mx = jnp.max(sc, axis=-1, keepdims=True)
            p = jnp.exp(sc - mx)
            p = p / jnp.sum(p, axis=-1, keepdims=True)
            dp = _dot(dob[:, hs], kvv[:, vs], NT)
            dsum = jnp.sum(p * dp, axis=-1, keepdims=True)
            dsb = (p * (dp - dsum) * scale).astype(BF16)
            dqs.append(_dot(dsb, kvv[:, hs], NN))
            dks.append(_dot(dsb, qb[:, hs], TN))
            dvs.append(_dot(p.astype(BF16), dob[:, hs], TN))
        dqb = jnp.concatenate(dqs, axis=-1).astype(BF16)
        dq_out[...] = dqb
        dx_out[...] = _dot(dqb, wq_ref[...], NT) + ALPHA * dr_ref[...]
        dkv = jnp.concatenate(dks + dvs, axis=-1)

        @pl.when(i == 0)
        def _():
            dkv_out[...] = dkv

        @pl.when(i > 0)
        def _():
            dkv_out[...] += dkv

    row = lambda wd: pl.BlockSpec((bm, wd), lambda i: (i, 0))
    return _pallas(
        body, name=name, grid=(s // bm,),
        in_specs=[row(d), row(d), row(X_W), pl.BlockSpec((mem, 2 * X_W), lambda i: (0, 0)),
                  pl.BlockSpec((X_W, d), lambda i: (0, 0)), pl.BlockSpec((d, X_W), lambda i: (0, 0))],
        out_specs=[row(d), row(X_W), pl.BlockSpec((mem, 2 * X_W), lambda i: (0, 0))],
        out_shape=[_sds((s, d), F32), _sds((s, X_W), BF16), _sds((mem, 2 * X_W), F32)],
        compiler_params=_params(dimension_semantics=("arbitrary",)),
    )(dyb, drf, q, kv, wo, wq)


def _gate_bwd(name, dr1b, w_o, proj, ya, yb, b_gate, d, after=()):
    s = dr1b.shape[0]
    bm, bn = _tile(s, 1024), _tile(d, 512)
    off_a, off_b = OFF_GA // bn, (OFF_GA + d) // bn
    nj = d // bn

    def body(a_ref, w_ref, ga_ref, gb_ref, ya_ref, yb_ref, ba_ref, bb_ref, dya_ref, dyb_ref, dg_ref, dba_ref, dbb_ref):
        i = pl.program_id(1)
        dm = _dot(a_ref[...], w_ref[...], NT)
        sa = _sigmoid(ga_ref[...] + ba_ref[...])
        sb = _sigmoid(gb_ref[...] + bb_ref[...])
        dya_ref[...] = (dm * sa).astype(BF16)
        dyb_ref[...] = (dm * sb).astype(BF16)
        dga = dm * ya_ref[...].astype(F32) * (sa * (1.0 - sa))
        dgb = dm * yb_ref[...].astype(F32) * (sb * (1.0 - sb))
        dg_ref[0] = dga.astype(BF16)
        dg_ref[1] = dgb.astype(BF16)
        sa_sum = jnp.sum(dga, axis=0, keepdims=True)
        sb_sum = jnp.sum(dgb, axis=0, keepdims=True)

        @pl.when(i == 0)
        def _():
            dba_ref[...] = sa_sum
            dbb_ref[...] = sb_sum

        @pl.when(i > 0)
        def _():
            dba_ref[...] += sa_sum
            dbb_ref[...] += sb_sum

    tile = pl.BlockSpec((bm, bn), lambda j, i: (i, j))
    return _pallas(
        body, after=after, name=name, grid=(nj, s // bm),
        in_specs=[pl.BlockSpec((bm, d), lambda j, i: (i, 0)),
                  pl.BlockSpec((bn, d), lambda j, i: (j, 0)),
                  pl.BlockSpec((bm, bn), lambda j, i: (i, off_a + j)),
                  pl.BlockSpec((bm, bn), lambda j, i: (i, off_b + j)),
                  tile, tile,
                  pl.BlockSpec((1, bn), lambda j, i: (0, j)), pl.BlockSpec((1, bn), lambda j, i: (0, nj + j))],
        out_specs=[tile, tile, pl.BlockSpec((2, bm, bn), lambda j, i: (0, i, j)),
                   pl.BlockSpec((1, bn), lambda j, i: (0, j)), pl.BlockSpec((1, bn), lambda j, i: (0, j))],
        out_shape=[_sds((s, d), BF16), _sds((s, d), BF16), _sds((2, s, d), BF16), _sds((1, d), F32),
                   _sds((1, d), F32)],
        compiler_params=_params(dimension_semantics=("arbitrary", "arbitrary")),
    )(dr1b, w_o, proj, proj, ya, yb, b_gate, b_gate)


def _gmlp_bwd(name, proj, dsg, ln_g, ln_b, w_s, b_st):
    s = proj.shape[0]

    def body(u_ref, v_ref, dsg_ref, g_ref, b_ref, ws_ref, bst_ref, duv_ref, dws_ref, dbst_ref, dlg_ref, dlb_ref):
        n = pl.program_id(0)
        u, v = u_ref[...], v_ref[...]
        gu, tu = _gelu(u)
        gv, tv = _gelu(v)
        gam = g_ref[...]
        vn, xhat, rstd = _ln_rows(gv, gam, b_ref[...])
        vnb = vn.astype(BF16)
        dsg = dsg_ref[...].astype(F32)
        row = lax.broadcasted_iota(jnp.int32, (CHUNK, CHUNK), 0)
        col = lax.broadcasted_iota(jnp.int32, (CHUNK, CHUNK), 1)
        tril = col <= row
        dgu, dvn, dws, dbs = [], [], [], []
        for g in range(GROUPS):
            sl = slice(g * LANE, (g + 1) * LANE)
            w = jnp.where(tril, ws_ref[g], 0.0).astype(BF16)
            mixed = _dot(w, vnb[:, sl], NN) + bst_ref[:, g:g + 1]
            dgu.append(dsg[:, sl] * mixed)
            dmx = dsg[:, sl] * gu[:, sl]
            dmxb = dmx.astype(BF16)
            dbs.append(jnp.sum(dmx, axis=-1, keepdims=True))
            dws.append(jnp.where(tril, _dot(dmxb, vnb[:, sl], NT), 0.0))
            dvn.append(_dot(w, dmxb, TN))
        dvn = jnp.concatenate(dvn, axis=-1)
        dgu = jnp.concatenate(dgu, axis=-1)
        dxh = dvn * gam
        m1 = jnp.mean(dxh, axis=-1, keepdims=True)
        m2 = jnp.mean(dxh * xhat, axis=-1, keepdims=True)
        dgv = rstd * (dxh - m1 - xhat * m2)
        du = dgu * _gelu_grad(u, tu)
        dv = dgv * _gelu_grad(v, tv)
        duv_ref[...] = jnp.concatenate([du, dv], axis=-1).astype(BF16)
        dlg = jnp.sum(dvn * xhat, axis=0, keepdims=True)
        dlb = jnp.sum(dvn, axis=0, keepdims=True)
        dbst = jnp.concatenate(dbs, axis=-1)

        @pl.when(n == 0)
        def _():
            for g in range(GROUPS):
                dws_ref[g] = dws[g]
            dbst_ref[...] = dbst
            dlg_ref[...] = dlg
            dlb_ref[...] = dlb

        @pl.when(n > 0)
        def _():
            for g in range(GROUPS):
                dws_ref[g] += dws[g]
            dbst_ref[...] += dbst
            dlg_ref[...] += dlg
            dlb_ref[...] += dlb

    vec = pl.BlockSpec((1, GMLP_W), lambda n: (0, 0))
    return _pallas(
        body, name=name, grid=(s // CHUNK,),
        in_specs=[pl.BlockSpec((CHUNK, GMLP_W), lambda n: (n, 0)), pl.BlockSpec((CHUNK, GMLP_W), lambda n: (n, 1)),
                  pl.BlockSpec((CHUNK, GMLP_W), lambda n: (n, 0)), vec, vec,
                  pl.BlockSpec((GROUPS, CHUNK, CHUNK), lambda n: (0, 0, 0)),
                  pl.BlockSpec((CHUNK, GROUPS), lambda n: (0, 0))],
        out_specs=[pl.BlockSpec((CHUNK, 2 * GMLP_W), lambda n: (n, 0)),
                   pl.BlockSpec((GROUPS, CHUNK, CHUNK), lambda n: (0, 0, 0)),
                   pl.BlockSpec((CHUNK, GROUPS), lambda n: (0, 0)), vec, vec],
        out_shape=[_sds((s, 2 * GMLP_W), BF16), _sds((GROUPS, CHUNK, CHUNK), F32), _sds((CHUNK, GROUPS), F32),
                   _sds((1, GMLP_W), F32), _sds((1, GMLP_W), F32)],
        compiler_params=_params(dimension_semantics=("arbitrary",)),
    )(proj, proj, dsg, ln_g, ln_b, w_s, b_st)


def _swa_bwd(name, qr, kr, proj, do, o, lse, sinks, cos4, nsin4, after=()):
    s = qr.shape[0]
    w = CHUNK
    nblk = s // w
    scale = HD ** -0.5
    grp = NQ // NKV

    def body(qj_ref, qn_ref, kj_ref, kp_ref, vj_ref, vp_ref, doj_ref, don_ref, oj_ref, on_ref, lj_ref, ln_ref,
             sink_ref, cos_ref, sin_ref, out_ref, dsink_ref):
        j = pl.program_id(0)
        qj, qn = qj_ref[...].astype(F32), qn_ref[...].astype(F32)
        doj, don = doj_ref[...].astype(F32), don_ref[...].astype(F32)
        kk = jnp.concatenate([kp_ref[...], kj_ref[...]], axis=0)
        vv = jnp.concatenate([vp_ref[...], vj_ref[...]], axis=0).astype(BF16)
        lj, lnx = lj_ref[...], ln_ref[...]
        prod_j = doj * oj_ref[...].astype(F32)
        prod_n = don * on_ref[...].astype(F32)
        valid_j = _band_mask(j > 0)
        valid_n = _band_mask(j + 1 < nblk, prev_only=True)
        lo = lax.broadcasted_iota(jnp.int32, (w, LANE), 1) < HD
        dqs, dsk, dk_g, dv_g = [], [], [], []
        for g in range(NKV):
            kz, vz = _own_head(kk, g), _own_head(vv, g)
            kz_c, vz_c = kz[w:], vz[w:]
            qg_j, qg_n = _stack_heads(qj, g), _stack_heads(qn, g)
            dog_j, dog_n = _stack_heads(doj, g), _stack_heads(don, g)
            l_j, l_n = _stack_rows(lj, g), _stack_rows(lnx, g)
            d_j = _head_lane_sums(_stack_heads(prod_j, g, F32), g)
            d_n = _head_lane_sums(_stack_heads(prod_n, g, F32), g)
            p = jnp.where(valid_j, jnp.exp(_dot(kz, qg_j, NT) * scale - l_j), 0.0)
            ds = (p * (_dot(vz, dog_j, NT) - d_j) * scale).astype(BF16)
            dqs.extend(_unstack_heads(_dot(ds, kz, TN), g))
            p2 = jnp.where(valid_n, jnp.exp(_dot(kz_c, qg_n, NT) * scale - l_n), 0.0)
            ds2 = (p2 * (_dot(vz_c, dog_n, NT) - d_n) * scale).astype(BF16)
            dk_g.append(_dot(ds[w:], qg_j, NN) + _dot(ds2, qg_n, NN))
            dv_g.append(_dot(p[w:].astype(BF16), dog_j, NN) + _dot(p2.astype(BF16), dog_n, NN))
            t = jnp.exp(sink_ref[g] - l_j) * d_j
            dsk.extend([-jnp.sum(t[:, h * w:(h + 1) * w], axis=-1, keepdims=True) for h in range(GRP)])
        cos, nsin = cos_ref[...], sin_ref[...]
        dq = _rope(jnp.concatenate(dqs, axis=-1), cos, nsin)
        dk = _rope(jnp.concatenate([jnp.where(lo, dk_g[2 * m], dk_g[2 * m + 1]) for m in range(NKV // 2)], axis=-1),
                   cos, nsin)
        dv = jnp.concatenate([jnp.where(lo, dv_g[2 * m], dv_g[2 * m + 1]) for m in range(NKV // 2)], axis=-1)
        out_ref[...] = jnp.concatenate([dq, dk, dv], axis=-1).astype(BF16)
        dsink = jnp.concatenate(dsk, axis=-1)

        @pl.when(j == 0)
        def _():
            dsink_ref[...] = dsink

        @pl.when(j > 0)
        def _():
            dsink_ref[...] += dsink

    nxt = lambda j: jnp.minimum(j + 1, nblk - 1)
    prv = lambda j: jnp.maximum(j - 1, 0)
    va = OFF_VA // KV_W
    return _pallas(
        body, after=after, name=name, grid=(nblk,),
        in_specs=[pl.BlockSpec((w, ATT_W), lambda j: (j, 0)), pl.BlockSpec((w, ATT_W), lambda j: (nxt(j), 0)),
                  pl.BlockSpec((w, KV_W), lambda j: (j, 0)), pl.BlockSpec((w, KV_W), lambda j: (prv(j), 0)),
                  pl.BlockSpec((w, KV_W), lambda j: (j, va)), pl.BlockSpec((w, KV_W), lambda j: (prv(j), va)),
                  pl.BlockSpec((w, ATT_W), lambda j: (j, 0)), pl.BlockSpec((w, ATT_W), lambda j: (nxt(j), 0)),
                  pl.BlockSpec((w, ATT_W), lambda j: (j, 0)), pl.BlockSpec((w, ATT_W), lambda j: (nxt(j), 0)),
                  pl.BlockSpec((None, NQ, w), lambda j: (j, 0, 0)),
                  pl.BlockSpec((None, NQ, w), lambda j: (nxt(j), 0, 0)),
                  pl.BlockSpec((NKV, 1, GRP * w), lambda j: (0, 0, 0)),
                  pl.BlockSpec((w, LANE), lambda j: (j, 0)), pl.BlockSpec((w, LANE), lambda j: (j, 0))],
        out_specs=[pl.BlockSpec((w, ATT_W + 2 * KV_W), lambda j: (j, 0)), pl.BlockSpec((1, NQ), lambda j: (0, 0))],
        out_shape=[_sds((s, ATT_W + 2 * KV_W), BF16), _sds((1, NQ), F32)],
        compiler_params=_params(dimension_semantics=("arbitrary",)),
    )(qr, qr, kr, kr, proj, proj, do, do, o, o, lse, lse, sinks, cos4, nsin4)


def _mm_nn(name, a, w, *, out_dtypes, epilogue=_store, bm_pref=1024, bn_pref=1024, after=()):
    m, k = a.shape
    n = w.shape[-1]
    bm, bn = _tile(m, bm_pref), _tile(n, bn_pref)
    tile = pl.BlockSpec((bm, bn), lambda i, j, kk: (i, j))
    return _mm(name, a, w, dims=NN, grid=(m // bm, n // bn, 1),
               a_spec=pl.BlockSpec((bm, k), lambda i, j, kk: (i, 0)),
               b_spec=pl.BlockSpec((k, bn), lambda i, j, kk: (0, j)),
               out_shape=[_sds((m, n), dt) for dt in out_dtypes], out_specs=[tile] * len(out_dtypes),
               epilogue=epilogue, after=after)


def _mm_tn(name, a, b, *, bm_pref=1024, bn_pref=1024, after=()):
    s, m = a.shape
    n = b.shape[-1]
    bm, bn = _tile(m, bm_pref), _tile(n, bn_pref)
    return _mm(name, a, b, dims=TN, grid=(m // bm, n // bn, 1),
               a_spec=pl.BlockSpec((s, bm), lambda i, j, kk: (0, i)),
               b_spec=pl.BlockSpec((s, bn), lambda i, j, kk: (0, j)),
               out_shape=[_sds((m, n), BF16)], out_specs=[pl.BlockSpec((bm, bn), lambda i, j, kk: (i, j))],
               epilogue=_store, after=after)[0]


def _mm_residual(name, a, w, x, after=()):
    s, k = a.shape
    d = w.shape[-1]
    bm, bn = _tile(s, 512), _tile(d, 512)
    tile = pl.BlockSpec((bm, bn), lambda i, j, kk: (i, j))
    return _mm(name, a, w, dims=NN, grid=(s // bm, d // bn, 1),
               a_spec=pl.BlockSpec((bm, k), lambda i, j, kk: (i, 0)),
               b_spec=pl.BlockSpec((k, bn), lambda i, j, kk: (0, j)),
               extras=(x,), extra_specs=(tile,), out_shape=[_sds((s, d), F32)], out_specs=[tile],
               epilogue=_ep_add_scaled, after=after)[0]


def _dw_pieces(name, a, pieces, widths, after=()):
    s, m = a.shape
    total = sum(widths)
    bm, bn = _tile(m, 1024), 512
    out, off = None, 0
    for p, (piece, wd) in enumerate(zip(pieces, widths)):
        if isinstance(piece, tuple):
            arr = piece[0]
            b_spec = pl.BlockSpec((None, s, bn), (lambda ix: lambda i, j: (ix, 0, j))(piece[1]))
        else:
            arr, b_spec = piece, pl.BlockSpec((s, bn), lambda i, j: (0, j))
        prev = () if out is None else (out,)

        def body(a_ref, b_ref, *rest):
            rest[-1][...] = _dot(a_ref[...], b_ref[...], TN).astype(BF16)

        out = _pallas(
            body, after=after if out is None else (), name=f"{name}_{p}", grid=(m // bm, wd // bn),
            in_specs=[pl.BlockSpec((s, bm), lambda i, j: (0, i)), b_spec] + [ANY] * len(prev),
            out_specs=pl.BlockSpec((bm, bn), (lambda c: lambda i, j: (i, c + j))(off // bn)),
            out_shape=_sds((m, total), BF16), input_output_aliases={2: 0} if prev else {},
            compiler_params=_params(dimension_semantics=("arbitrary", "arbitrary")),
        )(a, arr, *prev)
        off += wd
    return out


def _dx_pieces(name, pieces, widths, w, dr, after=()):
    s, d = dr.shape
    iw = w.shape[-1]
    bm, bn = _tile(s, 512), _tile(d, 512)
    arrs, specs = [], []
    for piece, wd in zip(pieces, widths):
        if isinstance(piece, tuple):
            arrs.append(piece[0])
            specs.append(pl.BlockSpec((None, bm, wd), (lambda idx: lambda i, j: (idx, i, 0))(piece[1])))
        else:
            arrs.append(piece)
            specs.append(pl.BlockSpec((bm, wd), lambda i, j: (i, 0)))
    n = len(arrs)

    def body(*refs):
        w_ref, dr_ref, o_ref = refs[n], refs[n + 1], refs[n + 2]
        acc = ALPHA * dr_ref[...]
        off = 0
        for p, wd in enumerate(widths):
            acc = acc + _dot(refs[p][...], w_ref[:, off:off + wd], NT)
            off += wd
        o_ref[...] = acc

    tile = pl.BlockSpec((bm, bn), lambda i, j: (i, j))
    return _pallas(
        body, after=after, name=name, grid=(s // bm, d // bn),
        in_specs=specs + [pl.BlockSpec((bn, iw), lambda i, j: (j, 0)), tile], out_specs=tile,
        out_shape=_sds((s, d), F32), compiler_params=_params(dimension_semantics=("arbitrary", "arbitrary")),
    )(*arrs, w, dr)


class _Gather:
    def __init__(self, tag, layer, names, shards, fulls, after):
        self.tag, self.names = tag, names
        self.axes = [SHARD_AXIS[n] for n in names]
        self.srcs = [shards[n] for n in names]
        self.mk1 = _mk_gather_ici(layer, self.axes)
        self.mk2 = _mk_gather_d2d(self.axes)
        self.n_sem = 3 * len(names)
        self.s1, self.r1, self.lands, self.token = _split_start(
            tag + "_ici_start", self.srcs, [fulls[n] for n in names], self.mk1, self.n_sem, after)

    def forward(self, after=()):
        lands = _split_wait(self.tag + "_ici_wait", self.srcs, self.lands, self.s1, self.r1, self.mk1, after)
        self.s2, self.r2, self.lands, tok = _split_start(self.tag + "_d2d_start", [], lands, self.mk2, self.n_sem)
        return tok

    def done(self, after=()):
        lands = _split_wait(self.tag + "_d2d_wait", [], self.lands, self.s2, self.r2, self.mk2, after)
        return dict(zip(self.names, lands))


class _Reduce:
    def __init__(self, tag, names, parts, cidx, mcidx, after=()):
        self.tag, self.names, self.cidx, self.mcidx = tag, names, cidx, mcidx
        self.axes = [SHARD_AXIS[n] for n in names]
        self.parts = [parts[n] for n in names]
        self.mk = _mk_swap(self.axes)
        lands = []
        for p, ax in zip(self.parts, self.axes):
            k, n = p.shape
            lands.append(lax.empty((k, n // 2) if ax == 0 else (k // 2, n), BF16))
        self.s, self.r, self.lands, self.token = _split_start(
            tag + "_swap_start", self.parts, lands, self.mk, len(names), after)

    def scatter(self, after=()):
        got = _split_wait(self.tag + "_swap_wait", self.parts, self.lands, self.s, self.r, self.mk, after)
        self.sums = [_add_half(f"{self.tag}_add_{n}", p, g, ax, self.cidx)
                     for n, p, g, ax in zip(self.names, self.parts, got, self.axes)]
        self.mk = _mk_scatter(self.axes)
        lands = []
        for q, ax in zip(self.sums, self.axes):
            k, n = q.shape
            lands.append(lax.empty((3, k // 4, n) if ax == 0 else (3, k, n // 4), BF16))
        self.s, self.r, self.lands, tok = _split_start(
            self.tag + "_scatter_start", self.sums, lands, self.mk, 3 * len(self.names))
        return tok

    def exchange(self, after=()):
        slots = _split_wait(self.tag + "_scatter_wait", self.sums, self.lands, self.s, self.r, self.mk, after)
        halves = [_sum_half(f"{self.tag}_sum_{n}", q, sl, ax, self.mcidx)
                  for n, q, sl, ax in zip(self.names, self.sums, slots, self.axes)]
        self.mk = _mk_exchange(self.axes)
        self.s, self.r, self.lands, tok = _split_start(
            self.tag + "_exchange_start", [], halves, self.mk, len(self.names))
        return tok

    def done(self, after=()):
        grads = _split_wait(self.tag + "_exchange_wait", [], self.lands, self.s, self.r, self.mk, after)
        return dict(zip(self.names, grads))


def _pack(arrs):
    flat = jnp.concatenate([a.reshape(-1) for a in arrs])
    n = flat.shape[0]
    pad = (-n) % (8 * LANE)
    return jnp.pad(flat, (0, pad)).reshape(-1, LANE)


def _unpack(packed, shapes):
    flat = packed.reshape(-1)
    out, off = [], 0
    for sh in shapes:
        n = math.prod(sh)
        out.append(flat[off:off + n].reshape(sh))
        off += n
    return out


def kernel(x, mem, w_in, b_gate, ln_v_g, ln_v_b, w_s, b_s, sinks, w_br_a, w_br_b, w_o, ln1_g, ln1_b, w_xq, w_xkv, w_xo, ln2_g, ln2_b, w_up, w_down, ln3_g, ln3_b, loss_target, m_w_in, m_b_gate, m_ln_v_g, m_ln_v_b, m_w_s, m_b_s, m_sinks, m_w_br_a, m_w_br_b, m_w_o, m_ln1_g, m_ln1_b, m_w_xq, m_w_xkv, m_w_xo, m_ln2_g, m_ln2_b, m_w_up, m_w_down, m_ln3_g, m_ln3_b, v_w_in, v_b_gate, v_ln_v_g, v_ln_v_b, v_w_s, v_b_s, v_sinks, v_w_br_a, v_w_br_b, v_w_o, v_ln1_g, v_ln1_b, v_w_xq, v_w_xkv, v_w_xo, v_ln2_g, v_ln2_b, v_w_up, v_w_down, v_ln3_g, v_ln3_b):
    env = dict(locals())
    wts = {n: env[n] for n in WEIGHTS}
    mom_m = {n: env["m_" + n] for n in WEIGHTS}
    mom_v = {n: env["v_" + n] for n in WEIGHTS}
    s, d = x.shape[1], x.shape[2]
    dff = 4 * w_up.shape[-1]
    iw = 4 * w_in.shape[-1]
    xf = x.reshape(s, d)
    tgt = loss_target.reshape(s, d)
    memf = mem.reshape(mem.shape[1], d)
    ax_x, ax_y, ax_c = lax.axis_index("x"), lax.axis_index("y"), lax.axis_index("c")
    meidx = jnp.reshape(2 * ax_x + ax_y, (1,)).astype(jnp.int32)
    cidx = jnp.reshape(ax_c, (1,)).astype(jnp.int32)
    mcidx = jnp.concatenate([meidx, cidx])

    inv = 1.0 / (10000.0 ** (jnp.arange(0, HD, 2, dtype=F32) / HD))
    ang = jnp.arange(s, dtype=F32)[:, None] * inv[None, :]
    cos, sin = jnp.cos(ang), jnp.sin(ang)
    cos4 = jnp.tile(cos, (1, 4))
    sin4 = jnp.concatenate([-sin, sin, -sin, sin], axis=-1)
    nsin4 = -sin4

    small = {}
    for n in SMALL:
        w = wts[n]
        if n == "w_s":
            small[n] = [w[l] for l in range(DEPTH)]
        elif n == "b_s":
            small["b_st"] = [w[l].T for l in range(DEPTH)]
        else:
            small[n] = [w[l][None, :] for l in range(DEPTH)]
    small["sink_rows"] = [jnp.repeat(sinks[l].reshape(NKV, GRP), CHUNK, axis=1)[:, None, :] for l in range(DEPTH)]

    shards, fulls = {}, [{}, {}]
    tok = ()
    gathers = [[None] * len(GROUPS_GATHER) for _ in range(DEPTH)]
    for gi, names in enumerate(GROUPS_GATHER):
        for n in names:
            shards[n] = _cast_bf16("cast_" + n, wts[n], after=tok)
            fulls[0][n], fulls[1][n] = _place_own("place_" + n, shards[n], SHARD_AXIS[n], meidx)
        gathers[0][gi] = _Gather(f"ag0_{gi}", 0, names, shards, fulls[0], tok)
        tok = (gathers[0][gi].token,)
    for gi, names in enumerate(GROUPS_GATHER):
        gathers[1][gi] = _Gather(f"ag1_{gi}", 1, names, shards, fulls[1], tok)
        tok = (gathers[1][gi].token,)

    xb = _cast2d("cast_x", xf, after=tok)
    memb = _cast2d("cast_mem", memf, after=tok)

    saved = []
    hf, hb = xf, xb
    nxt_tok = gathers[0][0].forward(after=tok)
    for l in range(DEPTH):
        t = f"l{l}_"
        ga, gb, gc, gd = gathers[l]
        full = ga.done(after=(nxt_tok, hb))
        sv = {"xf": hf, "xb": hb}
        proj = _mm_nn(t + "proj", hb, full["w_in"], out_dtypes=[F32], bn_pref=1280)[0]
        tok_b = gb.forward(after=(proj,))
        sg = _gmlp_fwd(t + "gmlp_fwd", proj, small["ln_v_g"][l], small["ln_v_b"][l], small["w_s"][l],
                       small["b_st"][l])
        attn, qr, kr, lse = _swa_fwd(t + "swa_fwd", proj, cos4, sin4, small["sink_rows"][l], after=(tok_b,))
        full.update(gb.done(after=(attn,)))
        merged, ya, yb = _gate_fwd(t + "gate_fwd", sg, attn, full["w_br_a"], full["w_br_b"], proj,
                                   small["b_gate"][l], d)
        tok_c = gc.forward(after=(merged,))
        r1 = _mm_residual(t + "o", merged, full["w_o"], hf, after=(tok_c,))
        x1, x1b = _ln_fwd(t + "ln1", r1, small["ln1_g"][l], small["ln1_b"][l])
        kv = _mm_nn(t + "xkv", memb, full["w_xkv"], out_dtypes=[BF16])[0]
        q, o, r2, x2, x2b = _xattn_fwd(t + "xattn_fwd", x1b, x1, full["w_xq"], kv, full["w_xo"],
                                       small["ln2_g"][l], small["ln2_b"][l])
        full.update(gc.done(after=(x2b,)))
        tok_d = gd.forward(after=(x2b,))

        def ep_up(acc, ex, outs):
            outs[0][...] = acc.astype(BF16)
            rl = jnp.maximum(acc, 0.0)
            outs[1][...] = (rl * rl).astype(BF16)

        h, a = _mm_nn(t + "up", x2b, full["w_up"], out_dtypes=[BF16, BF16], epilogue=ep_up, after=(tok_d,))
        full.update(gd.done(after=(h,)))
        nxt_tok = gathers[l + 1][0].forward(after=(h,)) if l + 1 < DEPTH else None
        r3 = _mm_residual(t + "down", a, full["w_down"], x2, after=() if nxt_tok is None else (nxt_tok,))
        x3, x3b = _ln_fwd(t + "ln3", r3, small["ln3_g"][l], small["ln3_b"][l])
        sv.update(proj=proj, sg=sg, attn=attn, qr=qr, kr=kr, lse=lse, merged=merged, ya=ya, yb=yb, r1=r1, x1=x1,
                  x1b=x1b, kv=kv, q=q, o=o, r2=r2, x2b=x2b, h=h, a=a, r3=r3, full=full)
        saved.append(sv)
        hf, hb = x3, x3b
    dy, loss11 = _loss_grad("loss", hf, tgt)
    loss = lax.psum(loss11[0, 0], ("x", "y", "c"))

    small_g = [None] * DEPTH
    grads = [{}, {}]
    pend_a = None
    pend_b = None
    g = dy
    for l in reversed(range(DEPTH)):
        t = f"l{l}_"
        sv = saved[l]
        full = sv["full"]
        dw, sgo = {}, {}
        dr3, dr3b, sgo["ln3_g"], sgo["ln3_b"] = _ln_bwd(t + "ln3_bwd", g, sv["r3"], small["ln3_g"][l],
                                                        after=() if pend_a is None else (tok_a,))
        bm, bn = _tile(s, 1024), _tile(dff, 1024)

        def ep_dh(acc, ex, outs):
            outs[0][...] = (acc * (2.0 * jnp.maximum(ex[0][...].astype(F32), 0.0))).astype(BF16)

        tile = pl.BlockSpec((bm, bn), lambda i, j, k: (i, j))
        dh = _mm(t + "dh", dr3b, full["w_down"], dims=NT, grid=(s // bm, dff // bn, 1),
                 a_spec=pl.BlockSpec((bm, d), lambda i, j, k: (i, 0)),
                 b_spec=pl.BlockSpec((bn, d), lambda i, j, k: (j, 0)),
                 extras=(sv["h"],), extra_specs=(tile,), out_shape=[_sds((s, dff), BF16)], out_specs=[tile],
                 epilogue=ep_dh)[0]
        if pend_a is not None:
            tok_pa = pend_a.exchange(after=(dh,))
            grads[l + 1].update(pend_b.done(after=(dh,)))
        dw["w_down"] = _mm_tn(t + "dw_down", sv["a"], dr3b, after=() if pend_a is None else (tok_pa,))
        dw["w_up"] = _mm_tn(t + "dw_up", sv["x2b"], dh)
        red_c = _Reduce(t + "rs_c", GROUPS_FWD[2], dw, cidx, mcidx)
        bm2, bn2 = _tile(s, 512), _tile(d, 512)
        tile2 = pl.BlockSpec((bm2, bn2), lambda i, j, k: (i, j))
        dx2 = _mm(t + "dx2", dh, full["w_up"], dims=NT, grid=(s // bm2, d // bn2, 1),
                  a_spec=pl.BlockSpec((bm2, dff), lambda i, j, k: (i, 0)),
                  b_spec=pl.BlockSpec((bn2, dff), lambda i, j, k: (j, 0)),
                  extras=(dr3,), extra_specs=(tile2,), out_shape=[_sds((s, d), F32)], out_specs=[tile2],
                  epilogue=_ep_add_scaled, after=(red_c.token,))[0]
        tok_c = red_c.scatter(after=(dx2,))
        if pend_a is not None:
            grads[l + 1].update(pend_a.done(after=(dx2,)))
            pend_a = None

        dr2, dr2b, sgo["ln2_g"], sgo["ln2_b"] = _ln_bwd(t + "ln2_bwd", dx2, sv["r2"], small["ln2_g"][l],
                                                        after=(tok_c,))
        dx1, dq, dkv = _xattn_bwd(t + "xattn_bwd", dr2b, dr2, sv["q"], sv["kv"], full["w_xo"], full["w_xq"])
        dw["w_xo"] = _mm_tn(t + "dw_xo", sv["o"], dr2b)
        dw["w_xq"] = _mm_tn(t + "dw_xq", sv["x1b"], dq)
        dw["w_xkv"] = _mm_tn(t + "dw_xkv", memb, _cast2d(t + "dkv_cast", dkv))

        dr1, dr1b, sgo["ln1_g"], sgo["ln1_b"] = _ln_bwd(t + "ln1_bwd", dx1, sv["r1"], small["ln1_g"][l])
        dya, dyb, dgate, dba, dbb = _gate_bwd(t + "gate_bwd", dr1b, full["w_o"], sv["proj"], sv["ya"], sv["yb"],
                                              small["b_gate"][l], d)
        sgo["b_gate"] = jnp.concatenate([dba, dbb], axis=-1)
        tok_c = red_c.exchange(after=(dya,))
        dw["w_o"] = _mm_tn(t + "dw_o", sv["merged"], dr1b, after=(tok_c,))
        dw["w_br_a"] = _mm_tn(t + "dw_br_a", sv["sg"], dya)
        dw["w_br_b"] = _mm_tn(t + "dw_br_b", sv["attn"], dyb)
        red_b = _Reduce(t + "rs_b", GROUPS_FWD[1], dw, cidx, mcidx)

        def dbranch(name, dyx, w, after):
            bk2 = _tile(d, 1024)
            return _mm(name, dyx, w, dims=NT, grid=(s // bm, 1, d // bk2),
                       a_spec=pl.BlockSpec((bm, bk2), lambda i, j, k: (i, k)),
                       b_spec=pl.BlockSpec((w.shape[0], bk2), lambda i, j, k: (0, k)),
                       out_shape=[_sds((s, w.shape[0]), BF16)],
                       out_specs=[pl.BlockSpec((bm, w.shape[0]), lambda i, j, k: (i, 0))],
                       epilogue=_store, acc_shape=(bm, w.shape[0]), after=after)[0]

        dsg = dbranch(t + "dsg", dya, full["w_br_a"], (red_b.token,))
        dattn = dbranch(t + "dattn", dyb, full["w_br_b"], ())
        tok_b = red_b.scatter(after=(dattn,))
        grads[l].update(red_c.done(after=(dattn,)))
        duv, sgo["w_s"], dbst, dlg, dlb = _gmlp_bwd(t + "gmlp_bwd", sv["proj"], dsg, small["ln_v_g"][l],
                                                    small["ln_v_b"][l], small["w_s"][l], small["b_st"][l])
        sgo["b_s"] = dbst.T
        sgo["ln_v_g"], sgo["ln_v_b"] = dlg, dlb
        dqkv, sgo["sinks"] = _swa_bwd(t + "swa_bwd", sv["qr"], sv["kr"], sv["proj"], dattn, sv["attn"], sv["lse"],
                                      small["sink_rows"][l], cos4, nsin4, after=(tok_b,))
        pieces = (duv, dqkv, (dgate, 0), (dgate, 1))
        widths = (2 * GMLP_W, ATT_W + 2 * KV_W, d, d)
        tok_b = red_b.exchange(after=(dqkv, duv))
        dw["w_in"] = _dw_pieces(t + "dw_in", sv["xb"], pieces, widths, after=(tok_b,))
        red_a = _Reduce(t + "rs_a", GROUPS_FWD[0], dw, cidx, mcidx)
        g = _dx_pieces(t + "dx0", pieces, widths, full["w_in"], dr1, after=(red_a.token,))
        tok_a = red_a.scatter(after=(g,))
        pend_a, pend_b = red_a, red_b
        small_g[l] = sgo
    grad_x = g.reshape(x.shape)

    big_out = {}

    def adam_layer(l, names, after):
        done = []
        for n in names:
            prev = big_out.get(n)
            big_out[n] = _adamw(f"adamw{l}_{n}", wts[n], grads[l][n], mom_m[n], mom_v[n], l, prev, after=after)
            done.append(big_out[n][0])
        return done

    shapes = [wts[n].shape for n in SMALL]
    packed_g = _pack([jnp.stack([small_g[l][n].reshape(wts[n].shape[1:]) for l in range(DEPTH)]) for n in SMALL])
    me8 = jnp.reshape(4 * ax_x + 2 * ax_y + ax_c, (1,)).astype(jnp.int32)
    ar_s, ar_r, ar_land, tok_ar = _split_start("ar_start", [packed_g], [_place_slot("ar_place", packed_g, me8)],
                                               _mk_small, 7, after=(tok_a,))
    fill = []
    for names in GROUPS_FWD:
        fill += adam_layer(1, names, (tok_ar,))
    grads[0].update(pend_b.done(after=tuple(fill)))
    fill += adam_layer(0, GROUPS_FWD[1], (tok_ar,))
    ar_land = _split_wait("ar_wait", [packed_g], ar_land, ar_s, ar_r, _mk_small, after=tuple(fill))
    packed_g = _sum_slots("ar_sum", ar_land[0])
    pw, pm, pv = (_pack([src[n] for n in SMALL]) for src in (wts, mom_m, mom_v))
    small4 = _adamw("adamw_small", pw[None], packed_g, pm[None], pv[None], 0)
    small_out = [dict(zip(SMALL, _unpack(a[0], shapes))) for a in small4]
    tok_a = pend_a.exchange(after=(small4[0],))
    fill = adam_layer(0, GROUPS_FWD[2], (tok_a,))
    grads[0].update(pend_a.done(after=tuple(fill)))
    adam_layer(0, GROUPS_FWD[0], ())

    def pick(kind, n):
        return big_out[n][kind] if n in big_out else small_out[kind][n]

    return (loss, grad_x, *[pick(0, n) for n in WEIGHTS], *[pick(1, n) for n in WEIGHTS],
            *[pick(2, n) for n in WEIGHTS], *[pick(3, n) for n in WEIGHTS])
```

```python
import math

import jax
import jax.numpy as jnp
from jax import lax
from jax.experimental import pallas as pl
from jax.experimental.pallas import tpu as pltpu

F32 = jnp.float32
BF16 = jnp.bfloat16
MESH = pl.DeviceIdType.MESH
ANY = pl.BlockSpec(memory_space=pl.ANY)
HBM = pl.BlockSpec(memory_space=pltpu.HBM)
SEM = pl.BlockSpec(memory_space=pltpu.SEMAPHORE)
VMEM_SPEC = pl.BlockSpec(memory_space=pltpu.VMEM)
EFFECT = pltpu.SideEffectType.DATAFLOW_SIDE_EFFECTING

DEPTH = 2
CHUNK = 128
GMLP_W = 1024
GROUPS = 8
NQ, NKV, HD = 16, 4, 64
ATT_W = NQ * HD
KV_W = NKV * HD
XH, XHD = 4, 128
X_W = XH * XHD
LN_EPS = 1e-5
ALPHA = (2 * DEPTH) ** 0.25
OFF_Q = 2 * GMLP_W
OFF_K = OFF_Q + ATT_W
OFF_VA = OFF_K + KV_W
OFF_GA = OFF_VA + KV_W
NEG = -1e30

ADAM_LR, ADAM_B1, ADAM_B2, ADAM_EPS, ADAM_WD, ADAM_STEP = 0.001, 0.9, 0.999, 1e-08, 0.01, 10

V7X_VMEM_BYTES = 64 * 1024 * 1024
VMEM_LIMIT = V7X_VMEM_BYTES - 12 * 1024 * 1024
LANE = 128

BIG = ("w_in", "w_br_a", "w_br_b", "w_o", "w_xq", "w_xkv", "w_xo", "w_up", "w_down")
SHARD_AXIS = {"w_in": 1, "w_br_a": 1, "w_br_b": 1, "w_o": 0, "w_xq": 0, "w_xkv": 0, "w_xo": 1,
              "w_up": 1, "w_down": 0}
GROUPS_GATHER = (("w_in",), ("w_br_a", "w_br_b", "w_o", "w_xq", "w_xkv", "w_xo"), ("w_up",), ("w_down",))
GROUPS_FWD = (("w_in",), ("w_br_a", "w_br_b", "w_o", "w_xq", "w_xkv", "w_xo"), ("w_up", "w_down"))
SMALL = ("b_gate", "ln_v_g", "ln_v_b", "w_s", "b_s", "sinks", "ln1_g", "ln1_b", "ln2_g", "ln2_b",
         "ln3_g", "ln3_b")
WEIGHTS = ("w_in", "b_gate", "ln_v_g", "ln_v_b", "w_s", "b_s", "sinks", "w_br_a", "w_br_b", "w_o",
           "ln1_g", "ln1_b", "w_xq", "w_xkv", "w_xo", "ln2_g", "ln2_b", "w_up", "w_down", "ln3_g", "ln3_b")


def _pallas(body, after=(), **kw):
    n_after = len(after)
    if not n_after:
        return pl.pallas_call(body, **kw)
    n_in = len(kw["in_specs"])
    kw["in_specs"] = list(kw["in_specs"]) + [ANY] * n_after

    def tied(*refs):
        return body(*refs[:n_in], *refs[n_in + n_after:])

    call = pl.pallas_call(tied, **kw)
    return lambda *ops: call(*ops, *after)


def _params(**kw):
    return pltpu.CompilerParams(vmem_limit_bytes=VMEM_LIMIT, **kw)


def _tile(dim, pref, unit=LANE):
    best = None
    t = unit
    while t <= min(dim, pref):
        if dim % t == 0:
            best = t
        t += unit
    return best if best is not None else dim


def _dot(a, b, dims):
    return lax.dot_general(a, b, (dims, ((), ())), preferred_element_type=F32)


NN = ((1,), (0,))
NT = ((1,), (1,))
TN = ((0,), (0,))


def _bf(x):
    return x if x.dtype == BF16 else x.astype(BF16)


def _sds(shape, dtype):
    return jax.ShapeDtypeStruct(shape, dtype)


def _mm(name, a, b, *, dims, grid, a_spec, b_spec, out_shape, out_specs, epilogue,
        extras=(), extra_specs=(), acc_shape=None, after=()):
    nk = grid[2]
    n_ex, n_out = len(extras), len(out_shape)

    def body(*refs):
        a_ref, b_ref = refs[0], refs[1]
        ex = refs[2:2 + n_ex]
        outs = refs[2 + n_ex:2 + n_ex + n_out]
        part = _dot(_bf(a_ref[...]), _bf(b_ref[...]), dims)
        if nk == 1:
            epilogue(part, ex, outs)
        else:
            acc = refs[-1]
            k = pl.program_id(2)

            @pl.when(k == 0)
            def _():
                acc[...] = part

            @pl.when(k > 0)
            def _():
                acc[...] += part

            @pl.when(k == nk - 1)
            def _():
                epilogue(acc[...], ex, outs)

    scratch = [pltpu.VMEM(acc_shape, F32)] if nk > 1 else []
    return _pallas(
        body, after=after, name=name, grid=grid, in_specs=[a_spec, b_spec, *extra_specs], out_specs=list(out_specs),
        out_shape=list(out_shape), scratch_shapes=scratch,
        compiler_params=_params(dimension_semantics=("arbitrary",) * 3),
    )(a, b, *extras)


def _store(acc, ex, outs):
    for o in outs:
        o[...] = acc.astype(o.dtype)


def _ln_rows(r, g, b):
    mu = jnp.mean(r, axis=-1, keepdims=True)
    xc = r - mu
    var = jnp.mean(xc * xc, axis=-1, keepdims=True)
    rstd = lax.rsqrt(var + LN_EPS)
    xhat = xc * rstd
    return xhat * g + b, xhat, rstd


def _ep_add_scaled(acc, ex, outs):
    outs[0][...] = acc + ALPHA * ex[0][...]


def _ln_fwd(name, r, g, b):
    s, d = r.shape
    bm = _tile(s, 256)

    def body(r_ref, g_ref, b_ref, y_ref, yb_ref):
        y, _, _ = _ln_rows(r_ref[...], g_ref[...], b_ref[...])
        y_ref[...] = y
        yb_ref[...] = y.astype(BF16)

    row = pl.BlockSpec((bm, d), lambda i: (i, 0))
    vec = pl.BlockSpec((1, d), lambda i: (0, 0))
    return _pallas(body, name=name, grid=(s // bm,), in_specs=[row, vec, vec], out_specs=[row, row],
                   out_shape=[_sds((s, d), F32), _sds((s, d), BF16)], compiler_params=_params())(r, g, b)


_GC = math.sqrt(2.0 / math.pi)


def _gelu(x):
    t = jnp.tanh(_GC * (x + 0.044715 * (x * x * x)))
    return 0.5 * x * (1.0 + t), t


def _gelu_grad(x, t):
    return 0.5 * (1.0 + t) + 0.5 * x * (1.0 - t * t) * (_GC * (1.0 + 3.0 * 0.044715 * x * x))


def _sigmoid(x):
    return 1.0 / (1.0 + jnp.exp(-x))


GRP = NQ // NKV


def _band_mask(prev_ok, prev_only=False):
    rows = CHUNK if prev_only else 2 * CHUNK
    key = lax.broadcasted_iota(jnp.int32, (rows, GRP * CHUNK), 0)
    qry = jnp.bitwise_and(lax.broadcasted_iota(jnp.int32, (rows, GRP * CHUNK), 1), CHUNK - 1)
    prev = jnp.logical_and(jnp.logical_and(key < CHUNK, key > qry), prev_ok)
    if prev_only:
        return prev
    return jnp.logical_or(prev, jnp.logical_and(key >= CHUNK, key - CHUNK <= qry))


def _pair(x, g):
    return x[:, (g // 2) * LANE:(g // 2 + 1) * LANE]


def _own_head(x, g):
    xp = _pair(x, g)
    lane = lax.broadcasted_iota(jnp.int32, xp.shape, 1)
    lo = (g % 2) * HD
    return jnp.where(jnp.logical_and(lane >= lo, lane < lo + HD), xp, jnp.zeros_like(xp))


def _stack_heads(x, g, dtype=BF16):
    a = x[:, g * GRP * HD:g * GRP * HD + LANE]
    b = x[:, g * GRP * HD + LANE:(g + 1) * GRP * HD]
    ar, br = pltpu.roll(a, HD, 1), pltpu.roll(b, HD, 1)
    parts = [a, ar, b, br] if g % 2 == 0 else [ar, a, br, b]
    return jnp.concatenate(parts, axis=0).astype(dtype)


def _unstack_heads(og, g):
    o = [og[h * CHUNK:(h + 1) * CHUNK] for h in range(GRP)]
    lo = lax.broadcasted_iota(jnp.int32, (CHUNK, LANE), 1) < HD
    if g % 2 == 0:
        x0, x1, x2, x3 = o[0], pltpu.roll(o[1], HD, 1), o[2], pltpu.roll(o[3], HD, 1)
    else:
        x0, x1, x2, x3 = pltpu.roll(o[0], HD, 1), o[1], pltpu.roll(o[2], HD, 1), o[3]
    return [jnp.where(lo, x0, x1), jnp.where(lo, x2, x3)]


def _stack_rows(x, g):
    return jnp.concatenate([x[g * GRP + h:g * GRP + h + 1] for h in range(GRP)], axis=-1)


def _head_lane_sums(x, g):
    lane = lax.broadcasted_iota(jnp.int32, (8, LANE), 1)
    lo_lane = (g % 2) * HD
    sel = jnp.where(jnp.logical_and(lane >= lo_lane, lane < lo_lane + HD), 1.0, 0.0).astype(BF16)
    hi = x.astype(BF16)
    lo = (x - hi.astype(F32)).astype(BF16)
    return (_dot(sel, hi, NT) + _dot(sel, lo, NT))[0:1]


def _rope(x, cos, sin_signed):
    w = x.shape[-1]
    lane = lax.broadcasted_iota(jnp.int32, x.shape, 1)
    first = (lane % HD) < (HD // 2)
    partner = jnp.where(first, pltpu.roll(x, w - HD // 2, 1), pltpu.roll(x, HD // 2, 1))
    reps = w // LANE
    return x * jnp.tile(cos, (1, reps)) + partner * jnp.tile(sin_signed, (1, reps))


def _cast2d(name, x, after=()):
    s, d = x.shape
    bm = _tile(s, 512, 8)

    def body(x_ref, o_ref):
        o_ref[...] = x_ref[...].astype(BF16)

    spec = pl.BlockSpec((bm, d), lambda i: (i, 0))
    return _pallas(body, after=after, name=name, grid=(s // bm,), in_specs=[spec], out_specs=spec,
                   out_shape=_sds(x.shape, BF16), compiler_params=_params())(x)


def _place():
    x, y, c = lax.axis_index("x"), lax.axis_index("y"), lax.axis_index("c")
    chips = [(1 - x, y), (x, 1 - y), (1 - x, 1 - y)]
    return x, y, c, chips


def _cut(ref, axis, chip=None, half=None, lead=()):
    k, n = ref.shape[-2], ref.shape[-1]
    rows, cols = slice(None), slice(None)
    if chip is not None:
        if axis == 0:
            rows = pl.ds(pl.multiple_of(chip * (k // 4), 8), k // 4)
        else:
            cols = pl.ds(pl.multiple_of(chip * (n // 4), LANE), n // 4)
    if half is not None:
        if axis == 0:
            cols = pl.ds(pl.multiple_of(half * (n // 2), LANE), n // 2)
        else:
            rows = pl.ds(pl.multiple_of(half * (k // 2), 8), k // 2)
    return ref.at[(*lead, rows, cols)]


def _split_start(name, srcs, lands, make, n_sem, after=()):
    ns, nl, na = len(srcs), len(lands), len(after)

    def body(*refs):
        src, land = refs[:ns], refs[ns:ns + nl]
        outs = refs[ns + nl + na:]
        for out_cp, _ in make(src, land, outs[0], outs[1]):
            out_cp.start()
        outs[-1][...] = jnp.zeros_like(outs[-1])

    res = pl.pallas_call(
        body, name=name, in_specs=[HBM] * (ns + nl) + [ANY] * na,
        out_specs=[SEM, SEM] + [HBM] * nl + [VMEM_SPEC],
        out_shape=[pltpu.SemaphoreType.DMA((n_sem,)), pltpu.SemaphoreType.DMA((n_sem,))]
        + [pltpu.HBM(a.shape, a.dtype) for a in lands] + [_sds((8, LANE), F32)],
        input_output_aliases={ns + i: 2 + i for i in range(nl)},
        compiler_params=pltpu.CompilerParams(has_side_effects=EFFECT),
    )(*[pltpu.with_memory_space_constraint(a, pltpu.HBM) for a in (*srcs, *lands)], *after)
    return res[0], res[1], list(res[2:2 + nl]), res[-1]


def _split_wait(name, srcs, lands, ssem, rsem, make, after=()):
    ns, nl, na = len(srcs), len(lands), len(after)

    def body(*refs):
        src, land = refs[:ns], refs[ns:ns + nl]
        s_ref, r_ref = refs[ns + nl], refs[ns + nl + 1]
        pairs = make(src, land, s_ref, r_ref)
        for _, in_cp in pairs:
            in_cp.wait_recv()
        for out_cp, _ in pairs:
            out_cp.wait_send()

    res = pl.pallas_call(
        body, name=name, in_specs=[HBM] * (ns + nl) + [SEM, SEM] + [ANY] * na,
        out_specs=[HBM] * nl, out_shape=[pltpu.HBM(a.shape, a.dtype) for a in lands],
        input_output_aliases={ns + i: i for i in range(nl)},
        compiler_params=pltpu.CompilerParams(has_side_effects=EFFECT),
    )(*srcs, *lands, ssem, rsem, *after)
    return list(res)


def _rcopy(src, dst, ssem, rsem, k, dev):
    return pltpu.make_async_remote_copy(src_ref=src, dst_ref=dst, send_sem=ssem.at[k], recv_sem=rsem.at[k],
                                        device_id=dev, device_id_type=MESH)


def _mk_gather_ici(axes):
    def make(src, land, ssem, rsem):
        x, y, c, chips = _place()
        me = 2 * x + y
        pairs = []
        for w, ax in enumerate(axes):
            mine = _cut(land[w], ax, chip=me, half=c)
            for j, (px, py) in enumerate(chips):
                dev = (px, py, c)
                got = _cut(land[w], ax, chip=2 * px + py, half=c)
                pairs.append((_rcopy(mine, mine, ssem, rsem, 3 * w + j, dev),
                              _rcopy(got, got, ssem, rsem, 3 * w + j, dev)))
        return pairs
    return make


def _mk_gather_d2d(axes):
    def make(src, land, ssem, rsem):
        x, y, c, chips = _place()
        sib = (x, y, 1 - c)
        pairs = []
        for w, ax in enumerate(axes):
            for j, (px, py) in enumerate(chips):
                have = _cut(land[w], ax, chip=2 * px + py, half=c)
                want = _cut(land[w], ax, chip=2 * px + py, half=1 - c)
                pairs.append((_rcopy(have, have, ssem, rsem, 3 * w + j, sib),
                              _rcopy(want, want, ssem, rsem, 3 * w + j, sib)))
        return pairs
    return make


def _mk_swap(axes):
    def make(src, land, ssem, rsem):
        x, y, c, _ = _place()
        sib = (x, y, 1 - c)
        pairs = []
        for w, ax in enumerate(axes):
            cp = _rcopy(_cut(src[w], ax, half=1 - c), land[w], ssem, rsem, w, sib)
            pairs.append((cp, cp))
        return pairs
    return make


def _mk_scatter(axes):
    def make(src, land, ssem, rsem):
        x, y, c, chips = _place()
        pairs = []
        for w, ax in enumerate(axes):
            for j, (px, py) in enumerate(chips):
                cp = _rcopy(_cut(src[w], ax, chip=2 * px + py), land[w].at[j], ssem, rsem, 3 * w + j, (px, py, c))
                pairs.append((cp, cp))
        return pairs
    return make


def _mk_exchange(axes):
    def make(src, land, ssem, rsem):
        x, y, c, _ = _place()
        sib = (x, y, 1 - c)
        pairs = []
        for w, ax in enumerate(axes):
            have = _cut(land[w], ax, half=c)
            want = _cut(land[w], ax, half=1 - c)
            pairs.append((_rcopy(have, have, ssem, rsem, w, sib), _rcopy(want, want, ssem, rsem, w, sib)))
        return pairs
    return make


def _place_own(name, shard, axis, meidx, after=()):
    _, r, c = shard.shape
    full = (4 * r, c) if axis == 0 else (r, 4 * c)
    br = _tile(r, 512, 8)
    nb = r // br
    if axis == 0:
        ospec = pl.BlockSpec((br, c), lambda i, me: (me[0] * nb + i, 0))
    else:
        ospec = pl.BlockSpec((br, c), lambda i, me: (i, me[0]))
    n_after = len(after)

    def body(me_ref, s_ref, *rest):
        o0_ref, o1_ref = rest[n_after:]
        o0_ref[...] = s_ref[0].astype(BF16)
        o1_ref[...] = s_ref[1].astype(BF16)

    return pl.pallas_call(
        body, name=name,
        grid_spec=pltpu.PrefetchScalarGridSpec(
            num_scalar_prefetch=1, grid=(nb,),
            in_specs=[pl.BlockSpec((2, br, c), lambda i, me: (0, i, 0))] + [ANY] * n_after,
            out_specs=[ospec, ospec]),
        out_shape=[_sds(full, BF16)] * 2, compiler_params=_params(),
    )(meidx, shard, *after)


def _add_half(name, part, got, axis, cidx):
    k, n = got.shape
    bm = _tile(k, 512, 8)
    nb = k // bm
    if axis == 0:
        pspec = pl.BlockSpec((bm, n), lambda i, c: (i, c[0]))
    else:
        pspec = pl.BlockSpec((bm, n), lambda i, c: (c[0] * nb + i, 0))

    def body(c_ref, a_ref, b_ref, o_ref):
        o_ref[...] = (a_ref[...].astype(F32) + b_ref[...].astype(F32)).astype(BF16)

    return pl.pallas_call(
        body, name=name,
        grid_spec=pltpu.PrefetchScalarGridSpec(
            num_scalar_prefetch=1, grid=(nb,), in_specs=[pspec, pl.BlockSpec((bm, n), lambda i, c: (i, 0))],
            out_specs=pl.BlockSpec((bm, n), lambda i, c: (i, 0))),
        out_shape=_sds((k, n), BF16), compiler_params=_params(),
    )(cidx, part, got)


def _sum_half(name, own, slots, axis, mc):
    _, r, cc = slots.shape
    br = _tile(r, 256, 8)
    nb = r // br
    if axis == 0:
        own_spec = pl.BlockSpec((br, cc), lambda i, mc: (mc[0] * nb + i, 0))
        out_spec = pl.BlockSpec((br, cc), lambda i, mc: (i, mc[1]))
        shape = (r, 2 * cc)
    else:
        own_spec = pl.BlockSpec((br, cc), lambda i, mc: (i, mc[0]))
        out_spec = pl.BlockSpec((br, cc), lambda i, mc: (mc[1] * nb + i, 0))
        shape = (2 * r, cc)

    def body(mc_ref, own_ref, s_ref, o_ref):
        acc = own_ref[...].astype(F32)
        for i in range(3):
            acc = acc + s_ref[i].astype(F32)
        o_ref[...] = acc

    return pl.pallas_call(
        body, name=name,
        grid_spec=pltpu.PrefetchScalarGridSpec(
            num_scalar_prefetch=1, grid=(nb,),
            in_specs=[own_spec, pl.BlockSpec((3, br, cc), lambda i, mc: (0, i, 0))], out_specs=out_spec),
        out_shape=_sds(shape, F32), compiler_params=_params(),
    )(mc, own, slots)


def _mk_small(src, land, ssem, rsem):
    x, y, c, _ = _place()
    me = 4 * x + 2 * y + c
    pairs = []
    for k in range(1, 8):
        peer = (1 - x if k & 4 else x, 1 - y if k & 2 else y, 1 - c if k & 1 else c)
        got = land[0].at[4 * peer[0] + 2 * peer[1] + peer[2]]
        pairs.append((_rcopy(src[0], land[0].at[me], ssem, rsem, k - 1, peer),
                      _rcopy(got, got, ssem, rsem, k - 1, peer)))
    return pairs


def _place_slot(name, packed, me8):
    rows, lanes = packed.shape
    br = _tile(rows, 512, 8)

    def body(me_ref, p_ref, o_ref):
        o_ref[...] = p_ref[...]

    return pl.pallas_call(
        body, name=name,
        grid_spec=pltpu.PrefetchScalarGridSpec(
            num_scalar_prefetch=1, grid=(rows // br,),
            in_specs=[pl.BlockSpec((br, lanes), lambda i, me: (i, 0))],
            out_specs=pl.BlockSpec((None, br, lanes), lambda i, me: (me[0], i, 0))),
        out_shape=_sds((8, rows, lanes), F32), compiler_params=_params(),
    )(me8, packed)


def _sum_slots(name, slots):
    _, rows, lanes = slots.shape
    br = _tile(rows, 512, 8)

    def body(s_ref, o_ref):
        acc = s_ref[0]
        for i in range(1, 8):
            acc = acc + s_ref[i]
        o_ref[...] = acc

    return pl.pallas_call(
        body, name=name, grid=(rows // br,), in_specs=[pl.BlockSpec((8, br, lanes), lambda i: (0, i, 0))],
        out_specs=pl.BlockSpec((br, lanes), lambda i: (i, 0)), out_shape=_sds((rows, lanes), F32),
        compiler_params=_params(),
    )(slots)


def _adamw_math(w, g, m, v):
    m2 = ADAM_B1 * m + (1.0 - ADAM_B1) * g
    v2 = ADAM_B2 * v + (1.0 - ADAM_B2) * (g * g)
    m_hat = m2 / (1.0 - ADAM_B1 ** ADAM_STEP)
    v_hat = v2 / (1.0 - ADAM_B2 ** ADAM_STEP)
    delta = -ADAM_LR * (m_hat / (jnp.sqrt(v_hat) + ADAM_EPS) + ADAM_WD * w)
    return delta, m2, v2


def _adamw(name, w, g, m, v, layer, prev=None, after=()):
    _, r, c = w.shape
    br = _tile(r, 256, 8)
    n_prev = 0 if prev is None else 4

    def body(*refs):
        w_ref, g_ref, m_ref, v_ref = refs[:4]
        go_ref, d_ref, mo_ref, vo_ref = refs[4 + n_prev:]
        gg = g_ref[...]
        delta, m2, v2 = _adamw_math(w_ref[...], gg, m_ref[...], v_ref[...])
        go_ref[...] = gg
        d_ref[...] = delta
        mo_ref[...] = m2
        vo_ref[...] = v2

    spec = pl.BlockSpec((None, br, c), lambda i: (layer, i, 0))
    return _pallas(
        body, after=after, name=name, grid=(r // br,),
        in_specs=[spec, pl.BlockSpec((br, c), lambda i: (i, 0)), spec, spec] + [ANY] * n_prev,
        out_specs=[spec] * 4, out_shape=[_sds(w.shape, F32)] * 4,
        input_output_aliases={4 + i: i for i in range(n_prev)}, compiler_params=_params(),
    )(w, g, m, v, *(prev or ()))


def _gmlp_fwd(name, proj, ln_g, ln_b, w_s, b_st):
    s = proj.shape[0]

    def body(u_ref, v_ref, g_ref, b_ref, ws_ref, bst_ref, sg_ref):
        gu, _ = _gelu(u_ref[...])
        gv, _ = _gelu(v_ref[...])
        vn, _, _ = _ln_rows(gv, g_ref[...], b_ref[...])
        vn = vn.astype(BF16)
        row = lax.broadcasted_iota(jnp.int32, (CHUNK, CHUNK), 0)
        col = lax.broadcasted_iota(jnp.int32, (CHUNK, CHUNK), 1)
        tril = col <= row
        outs = []
        for g in range(GROUPS):
            sl = slice(g * LANE, (g + 1) * LANE)
            w = jnp.where(tril, ws_ref[g], 0.0).astype(BF16)
            mixed = _dot(w, vn[:, sl], NN) + bst_ref[:, g:g + 1]
            outs.append(gu[:, sl] * mixed)
        sg_ref[...] = jnp.concatenate(outs, axis=-1).astype(BF16)

    return _pallas(
        body, name=name, grid=(s // CHUNK,),
        in_specs=[pl.BlockSpec((CHUNK, GMLP_W), lambda n: (n, 0)), pl.BlockSpec((CHUNK, GMLP_W), lambda n: (n, 1)),
                  pl.BlockSpec((1, GMLP_W), lambda n: (0, 0)), pl.BlockSpec((1, GMLP_W), lambda n: (0, 0)),
                  pl.BlockSpec((GROUPS, CHUNK, CHUNK), lambda n: (0, 0, 0)),
                  pl.BlockSpec((CHUNK, GROUPS), lambda n: (0, 0))],
        out_specs=pl.BlockSpec((CHUNK, GMLP_W), lambda n: (n, 0)),
        out_shape=_sds((s, GMLP_W), BF16), compiler_params=_params(),
    )(proj, proj, ln_g, ln_b, w_s, b_st)


def _swa_fwd(name, proj, cos4, sin4, sinks, after=()):
    s = proj.shape[0]
    w = CHUNK
    scale = HD ** -0.5

    def body(q_ref, k_ref, v_ref, cos_ref, sin_ref, sink_ref, o_ref, qr_ref, kr_ref, lse_ref, kprev, vprev):
        n = pl.program_id(0)

        @pl.when(n == 0)
        def _():
            kprev[...] = jnp.zeros_like(kprev)
            vprev[...] = jnp.zeros_like(vprev)

        cos, sin = cos_ref[...], sin_ref[...]
        qr = _rope(q_ref[...], cos, sin)
        kr = _rope(k_ref[...], cos, sin).astype(BF16)
        vb = v_ref[...].astype(BF16)
        kk = jnp.concatenate([kprev[...], kr], axis=0)
        vv = jnp.concatenate([vprev[...], vb], axis=0)
        valid = _band_mask(n > 0)
        outs, lses = [], []
        for g in range(NKV):
            sc = jnp.where(valid, _dot(_own_head(kk, g), _stack_heads(qr, g), NT) * scale, NEG)
            sink = sink_ref[g]
            mx = jnp.maximum(jnp.max(sc, axis=0, keepdims=True), sink)
            p = jnp.exp(sc - mx)
            den = jnp.sum(p, axis=0, keepdims=True) + jnp.exp(sink - mx)
            og = _dot((p * (1.0 / den)).astype(BF16), _pair(vv, g), TN)
            outs.extend(_unstack_heads(og, g))
            lg = mx + jnp.log(den)
            lses.extend([lg[:, h * w:(h + 1) * w] for h in range(GRP)])
        o_ref[...] = jnp.concatenate(outs, axis=-1).astype(BF16)
        lse_ref[...] = jnp.concatenate(lses, axis=0)
        qr_ref[...] = qr.astype(BF16)
        kr_ref[...] = kr
        kprev[...] = kr
        vprev[...] = vb

    return _pallas(
        body, after=after, name=name, grid=(s // w,),
        in_specs=[pl.BlockSpec((w, ATT_W), lambda n: (n, OFF_Q // ATT_W)),
                  pl.BlockSpec((w, KV_W), lambda n: (n, OFF_K // KV_W)),
                  pl.BlockSpec((w, KV_W), lambda n: (n, OFF_VA // KV_W)),
                  pl.BlockSpec((w, LANE), lambda n: (n, 0)), pl.BlockSpec((w, LANE), lambda n: (n, 0)),
                  pl.BlockSpec((NKV, 1, GRP * w), lambda n: (0, 0, 0))],
        out_specs=[pl.BlockSpec((w, ATT_W), lambda n: (n, 0)), pl.BlockSpec((w, ATT_W), lambda n: (n, 0)),
                   pl.BlockSpec((w, KV_W), lambda n: (n, 0)), pl.BlockSpec((None, NQ, w), lambda n: (n, 0, 0))],
        out_shape=[_sds((s, ATT_W), BF16), _sds((s, ATT_W), BF16), _sds((s, KV_W), BF16),
                   _sds((s // w, NQ, w), F32)],
        scratch_shapes=[pltpu.VMEM((w, KV_W), BF16), pltpu.VMEM((w, KV_W), BF16)],
        compiler_params=_params(dimension_semantics=("arbitrary",)),
    )(proj, proj, proj, cos4, sin4, sinks)


def _gate_fwd(name, sg, attn, wa, wb, proj, b_gate, d):
    s = sg.shape[0]
    bm, bn = _tile(s, 1024), _tile(d, 512)
    off_a, off_b = OFF_GA // bn, (OFF_GA + d) // bn

    def body(sg_ref, at_ref, wa_ref, wb_ref, ga_ref, gb_ref, ba_ref, bb_ref, m_ref, ya_ref, yb_ref):
        ya = _dot(sg_ref[...], wa_ref[...], NN)
        yb = _dot(at_ref[...], wb_ref[...], NN)
        sa = _sigmoid(ga_ref[...] + ba_ref[...])
        sb = _sigmoid(gb_ref[...] + bb_ref[...])
        m_ref[...] = (sa * ya + sb * yb).astype(BF16)
        ya_ref[...] = ya.astype(BF16)
        yb_ref[...] = yb.astype(BF16)

    tile = pl.BlockSpec((bm, bn), lambda i, j: (i, j))
    return _pallas(
        body, name=name, grid=(s // bm, d // bn),
        in_specs=[pl.BlockSpec((bm, GMLP_W), lambda i, j: (i, 0)), pl.BlockSpec((bm, ATT_W), lambda i, j: (i, 0)),
                  pl.BlockSpec((GMLP_W, bn), lambda i, j: (0, j)), pl.BlockSpec((ATT_W, bn), lambda i, j: (0, j)),
                  pl.BlockSpec((bm, bn), lambda i, j: (i, off_a + j)),
                  pl.BlockSpec((bm, bn), lambda i, j: (i, off_b + j)),
                  pl.BlockSpec((1, bn), lambda i, j: (0, j)), pl.BlockSpec((1, bn), lambda i, j: (0, d // bn + j))],
        out_specs=[tile, tile, tile], out_shape=[_sds((s, d), BF16)] * 3,
        compiler_params=_params(),
    )(sg, attn, wa, wb, proj, proj, b_gate, b_gate)


def _xattn_fwd(name, xb, xf, wq, kv, wo, ln_g, ln_b, after=()):
    s, d = xf.shape
    mem = kv.shape[0]
    bm = _tile(s, 512)
    scale = XHD ** -0.5

    def body(xb_ref, xf_ref, wq_ref, kv_ref, wo_ref, g_ref, b_ref, q_out, o_out, r_out, y_out, yb_out):
        qb = _dot(xb_ref[...], wq_ref[...], NN).astype(BF16)
        kvv = kv_ref[...]
        outs = []
        for h in range(XH):
            hs = slice(h * XHD, (h + 1) * XHD)
            vs = slice(X_W + h * XHD, X_W + (h + 1) * XHD)
            sc = _dot(qb[:, hs], kvv[:, hs], NT) * scale
            mx = jnp.max(sc, axis=-1, keepdims=True)
            p = jnp.exp(sc - mx)
            p = p / jnp.sum(p, axis=-1, keepdims=True)
            outs.append(_dot(p.astype(BF16), kvv[:, vs], NN))
        ob = jnp.concatenate(outs, axis=-1).astype(BF16)
        yv = _dot(ob, wo_ref[...], NN)
        r = ALPHA * xf_ref[...] + yv
        yn, _, _ = _ln_rows(r, g_ref[...], b_ref[...])
        q_out[...] = qb
        o_out[...] = ob
        r_out[...] = r
        y_out[...] = yn
        yb_out[...] = yn.astype(BF16)

    row = lambda wd: pl.BlockSpec((bm, wd), lambda i: (i, 0))
    return _pallas(
        body, after=after, name=name, grid=(s // bm,),
        in_specs=[row(d), row(d), pl.BlockSpec((d, X_W), lambda i: (0, 0)),
                  pl.BlockSpec((mem, 2 * X_W), lambda i: (0, 0)), pl.BlockSpec((X_W, d), lambda i: (0, 0)),
                  pl.BlockSpec((1, d), lambda i: (0, 0)), pl.BlockSpec((1, d), lambda i: (0, 0))],
        out_specs=[row(X_W), row(X_W), row(d), row(d), row(d)],
        out_shape=[_sds((s, X_W), BF16), _sds((s, X_W), BF16), _sds((s, d), F32), _sds((s, d), F32),
                   _sds((s, d), BF16)],
        compiler_params=_params(),
    )(xb, xf, wq, kv, wo, ln_g, ln_b)


def _loss_grad(name, y, tgt):
    s, d = y.shape
    bm = _tile(s, 512)

    def body(y_ref, t_ref, dy_ref, loss_ref):
        i = pl.program_id(0)
        err = y_ref[...] - t_ref[...]
        dy_ref[...] = err * (1.0 / d)
        part = 0.5 * jnp.sum(jnp.sum(err * err, axis=-1, keepdims=True), axis=0, keepdims=True) * (1.0 / d)

        @pl.when(i == 0)
        def _():
            loss_ref[...] = part

        @pl.when(i > 0)
        def _():
            loss_ref[...] += part

    row = pl.BlockSpec((bm, d), lambda i: (i, 0))
    return _pallas(
        body, name=name, grid=(s // bm,), in_specs=[row, row],
        out_specs=[row, pl.BlockSpec((1, 1), lambda i: (0, 0))],
        out_shape=[_sds((s, d), F32), _sds((1, 1), F32)],
        compiler_params=_params(dimension_semantics=("arbitrary",)),
    )(y, tgt)


def _ln_bwd(name, dy, r, g, after=()):
    s, d = r.shape
    bm = _tile(s, 256)

    def body(dy_ref, r_ref, g_ref, dr_ref, drb_ref, dg_ref, db_ref):
        i = pl.program_id(0)
        dyv = dy_ref[...]
        _, xhat, rstd = _ln_rows(r_ref[...], g_ref[...], 0.0)
        dxh = dyv * g_ref[...]
        m1 = jnp.mean(dxh, axis=-1, keepdims=True)
        m2 = jnp.mean(dxh * xhat, axis=-1, keepdims=True)
        dr = rstd * (dxh - m1 - xhat * m2)
        dr_ref[...] = dr
        drb_ref[...] = dr.astype(BF16)
        dg = jnp.sum(dyv * xhat, axis=0, keepdims=True)
        db = jnp.sum(dyv, axis=0, keepdims=True)

        @pl.when(i == 0)
        def _():
            dg_ref[...] = dg
            db_ref[...] = db

        @pl.when(i > 0)
        def _():
            dg_ref[...] += dg
            db_ref[...] += db

    row = pl.BlockSpec((bm, d), lambda i: (i, 0))
    vec = pl.BlockSpec((1, d), lambda i: (0, 0))
    return _pallas(
        body, after=after, name=name, grid=(s // bm,), in_specs=[row, row, vec], out_specs=[row, row, vec, vec],
        out_shape=[_sds((s, d), F32), _sds((s, d), BF16), _sds((1, d), F32), _sds((1, d), F32)],
        compiler_params=_params(dimension_semantics=("arbitrary",)),
    )(dy, r, g)


def _xattn_bwd(name, dyb, drf, q, kv, wo, wq):
    s, d = drf.shape
    mem = kv.shape[0]
    bm = _tile(s, 512)
    scale = XHD ** -0.5

    def body(dy_ref, dr_ref, q_ref, kv_ref, wo_ref, wq_ref, dx_out, dq_out, dkv_out):
        i = pl.program_id(0)
        dob = _dot(dy_ref[...], wo_ref[...], NT).astype(BF16)
        qb = q_ref[...]
        kvv = kv_ref[...]
        dqs, dks, dvs = [], [], []
        for h in range(XH):
            hs = slice(h * XHD, (h + 1) * XHD)
            vs = slice(X_W + h * XHD, X_W + (h + 1) * XHD)
            sc = _dot(qb[:, hs], kvv[:, hs], NT) * scale
            mx = jnp.max(sc, axis=-1, keepdims=True)
            p = jnp.exp(sc - mx)
            p = p / jnp.sum(p, axis=-1, keepdims=True)
            dp = _dot(dob[:, hs], kvv[:, vs], NT)
            dsum = jnp.sum(p * dp, axis=-1, keepdims=True)
            dsb = (p * (dp - dsum) * scale).astype(BF16)
            dqs.append(_dot(dsb, kvv[:, hs], NN))
            dks.append(_dot(dsb, qb[:, hs], TN))
            dvs.append(_dot(p.astype(BF16), dob[:, hs], TN))
        dqb = jnp.concatenate(dqs, axis=-1).astype(BF16)
        dq_out[...] = dqb
        dx_out[...] = _dot(dqb, wq_ref[...], NT) + ALPHA * dr_ref[...]
        dkv = jnp.concatenate(dks + dvs, axis=-1)

        @pl.when(i == 0)
        def _():
            dkv_out[...] = dkv

        @pl.when(i > 0)
        def _():
            dkv_out[...] += dkv

    row = lambda wd: pl.BlockSpec((bm, wd), lambda i: (i, 0))
    return _pallas(
        body, name=name, grid=(s // bm,),
        in_specs=[row(d), row(d), row(X_W), pl.BlockSpec((mem, 2 * X_W), lambda i: (0, 0)),
                  pl.BlockSpec((X_W, d), lambda i: (0, 0)), pl.BlockSpec((d, X_W), lambda i: (0, 0))],
        out_specs=[row(d), row(X_W), pl.BlockSpec((mem, 2 * X_W), lambda i: (0, 0))],
        out_shape=[_sds((s, d), F32), _sds((s, X_W), BF16), _sds((mem, 2 * X_W), F32)],
        compiler_params=_params(dimension_semantics=("arbitrary",)),
    )(dyb, drf, q, kv, wo, wq)


def _gate_bwd(name, dr1b, w_o, proj, ya, yb, b_gate, d, after=()):
    s = dr1b.shape[0]
    bm, bn = _tile(s, 1024), _tile(d, 512)
    off_a, off_b = OFF_GA // bn, (OFF_GA + d) // bn
    nj = d // bn

    def body(a_ref, w_ref, ga_ref, gb_ref, ya_ref, yb_ref, ba_ref, bb_ref, dya_ref, dyb_ref, dg_ref, dba_ref, dbb_ref):
        i = pl.program_id(1)
        dm = _dot(a_ref[...], w_ref[...], NT)
        sa = _sigmoid(ga_ref[...] + ba_ref[...])
        sb = _sigmoid(gb_ref[...] + bb_ref[...])
        dya_ref[...] = (dm * sa).astype(BF16)
        dyb_ref[...] = (dm * sb).astype(BF16)
        dga = dm * ya_ref[...].astype(F32) * (sa * (1.0 - sa))
        dgb = dm * yb_ref[...].astype(F32) * (sb * (1.0 - sb))
        dg_ref[0] = dga.astype(BF16)
        dg_ref[1] = dgb.astype(BF16)
        sa_sum = jnp.sum(dga, axis=0, keepdims=True)
        sb_sum = jnp.sum(dgb, axis=0, keepdims=True)

        @pl.when(i == 0)
        def _():
            dba_ref[...] = sa_sum
            dbb_ref[...] = sb_sum

        @pl.when(i > 0)
        def _():
            dba_ref[...] += sa_sum
            dbb_ref[...] += sb_sum

    tile = pl.BlockSpec((bm, bn), lambda j, i: (i, j))
    return _pallas(
        body, after=after, name=name, grid=(nj, s // bm),
        in_specs=[pl.BlockSpec((bm, d), lambda j, i: (i, 0)),
                  pl.BlockSpec((bn, d), lambda j, i: (j, 0)),
                  pl.BlockSpec((bm, bn), lambda j, i: (i, off_a + j)),
                  pl.BlockSpec((bm, bn), lambda j, i: (i, off_b + j)),
                  tile, tile,
                  pl.BlockSpec((1, bn), lambda j, i: (0, j)), pl.BlockSpec((1, bn), lambda j, i: (0, nj + j))],
        out_specs=[tile, tile, pl.BlockSpec((2, bm, bn), lambda j, i: (0, i, j)),
                   pl.BlockSpec((1, bn), lambda j, i: (0, j)), pl.BlockSpec((1, bn), lambda j, i: (0, j))],
        out_shape=[_sds((s, d), BF16), _sds((s, d), BF16), _sds((2, s, d), BF16), _sds((1, d), F32),
                   _sds((1, d), F32)],
        compiler_params=_params(dimension_semantics=("arbitrary", "arbitrary")),
    )(dr1b, w_o, proj, proj, ya, yb, b_gate, b_gate)


def _gmlp_bwd(name, proj, dsg, ln_g, ln_b, w_s, b_st):
    s = proj.shape[0]

    def body(u_ref, v_ref, dsg_ref, g_ref, b_ref, ws_ref, bst_ref, duv_ref, dws_ref, dbst_ref, dlg_ref, dlb_ref):
        n = pl.program_id(0)
        u, v = u_ref[...], v_ref[...]
        gu, tu = _gelu(u)
        gv, tv = _gelu(v)
        gam = g_ref[...]
        vn, xhat, rstd = _ln_rows(gv, gam, b_ref[...])
        vnb = vn.astype(BF16)
        dsg = dsg_ref[...].astype(F32)
        row = lax.broadcasted_iota(jnp.int32, (CHUNK, CHUNK), 0)
        col = lax.broadcasted_iota(jnp.int32, (CHUNK, CHUNK), 1)
        tril = col <= row
        dgu, dvn, dws, dbs = [], [], [], []
        for g in range(GROUPS):
            sl = slice(g * LANE, (g + 1) * LANE)
            w = jnp.where(tril, ws_ref[g], 0.0).astype(BF16)
            mixed = _dot(w, vnb[:, sl], NN) + bst_ref[:, g:g + 1]
            dgu.append(dsg[:, sl] * mixed)
            dmx = dsg[:, sl] * gu[:, sl]
            dmxb = dmx.astype(BF16)
            dbs.append(jnp.sum(dmx, axis=-1, keepdims=True))
            dws.append(jnp.where(tril, _dot(dmxb, vnb[:, sl], NT), 0.0))
            dvn.append(_dot(w, dmxb, TN))
        dvn = jnp.concatenate(dvn, axis=-1)
        dgu = jnp.concatenate(dgu, axis=-1)
        dxh = dvn * gam
        m1 = jnp.mean(dxh, axis=-1, keepdims=True)
        m2 = jnp.mean(dxh * xhat, axis=-1, keepdims=True)
        dgv = rstd * (dxh - m1 - xhat * m2)
        du = dgu * _gelu_grad(u, tu)
        dv = dgv * _gelu_grad(v, tv)
        duv_ref[...] = jnp.concatenate([du, dv], axis=-1).astype(BF16)
        dlg = jnp.sum(dvn * xhat, axis=0, keepdims=True)
        dlb = jnp.sum(dvn, axis=0, keepdims=True)
        dbst = jnp.concatenate(dbs, axis=-1)

        @pl.when(n == 0)
        def _():
            for g in range(GROUPS):
                dws_ref[g] = dws[g]
            dbst_ref[...] = dbst
            dlg_ref[...] = dlg
            dlb_ref[...] = dlb

        @pl.when(n > 0)
        def _():
            for g in range(GROUPS):
                dws_ref[g] += dws[g]
            dbst_ref[...] += dbst
            dlg_ref[...] += dlg
            dlb_ref[...] += dlb

    vec = pl.BlockSpec((1, GMLP_W), lambda n: (0, 0))
    return _pallas(
        body, name=name, grid=(s // CHUNK,),
        in_specs=[pl.BlockSpec((CHUNK, GMLP_W), lambda n: (n, 0)), pl.BlockSpec((CHUNK, GMLP_W), lambda n: (n, 1)),
                  pl.BlockSpec((CHUNK, GMLP_W), lambda n: (n, 0)), vec, vec,
                  pl.BlockSpec((GROUPS, CHUNK, CHUNK), lambda n: (0, 0, 0)),
                  pl.BlockSpec((CHUNK, GROUPS), lambda n: (0, 0))],
        out_specs=[pl.BlockSpec((CHUNK, 2 * GMLP_W), lambda n: (n, 0)),
                   pl.BlockSpec((GROUPS, CHUNK, CHUNK), lambda n: (0, 0, 0)),
                   pl.BlockSpec((CHUNK, GROUPS), lambda n: (0, 0)), vec, vec],
        out_shape=[_sds((s, 2 * GMLP_W), BF16), _sds((GROUPS, CHUNK, CHUNK), F32), _sds((CHUNK, GROUPS), F32),
                   _sds((1, GMLP_W), F32), _sds((1, GMLP_W), F32)],
        compiler_params=_params(dimension_semantics=("arbitrary",)),
    )(proj, proj, dsg, ln_g, ln_b, w_s, b_st)


def _swa_bwd(name, qr, kr, proj, do, o, lse, sinks, cos4, nsin4, after=()):
    s = qr.shape[0]
    w = CHUNK
    nblk = s // w
    scale = HD ** -0.5
    grp = NQ // NKV

    def body(qj_ref, qn_ref, kj_ref, kp_ref, vj_ref, vp_ref, doj_ref, don_ref, oj_ref, on_ref, lj_ref, ln_ref,
             sink_ref, cos_ref, sin_ref, out_ref, dsink_ref):
        j = pl.program_id(0)
        qj, qn = qj_ref[...].astype(F32), qn_ref[...].astype(F32)
        doj, don = doj_ref[...].astype(F32), don_ref[...].astype(F32)
        kk = jnp.concatenate([kp_ref[...], kj_ref[...]], axis=0)
        vv = jnp.concatenate([vp_ref[...], vj_ref[...]], axis=0).astype(BF16)
        lj, lnx = lj_ref[...], ln_ref[...]
        prod_j = doj * oj_ref[...].astype(F32)
        prod_n = don * on_ref[...].astype(F32)
        valid_j = _band_mask(j > 0)
        valid_n = _band_mask(j + 1 < nblk, prev_only=True)
        lo = lax.broadcasted_iota(jnp.int32, (w, LANE), 1) < HD
        dqs, dsk, dk_g, dv_g = [], [], [], []
        for g in range(NKV):
            kz, vz = _own_head(kk, g), _own_head(vv, g)
            kz_c, vz_c = kz[w:], vz[w:]
            qg_j, qg_n = _stack_heads(qj, g), _stack_heads(qn, g)
            dog_j, dog_n = _stack_heads(doj, g), _stack_heads(don, g)
            l_j, l_n = _stack_rows(lj, g), _stack_rows(lnx, g)
            d_j = _head_lane_sums(_stack_heads(prod_j, g, F32), g)
            d_n = _head_lane_sums(_stack_heads(prod_n, g, F32), g)
            p = jnp.where(valid_j, jnp.exp(_dot(kz, qg_j, NT) * scale - l_j), 0.0)
            ds = (p * (_dot(vz, dog_j, NT) - d_j) * scale).astype(BF16)
            dqs.extend(_unstack_heads(_dot(ds, kz, TN), g))
            p2 = jnp.where(valid_n, jnp.exp(_dot(kz_c, qg_n, NT) * scale - l_n), 0.0)
            ds2 = (p2 * (_dot(vz_c, dog_n, NT) - d_n) * scale).astype(BF16)
            dk_g.append(_dot(ds[w:], qg_j, NN) + _dot(ds2, qg_n, NN))
            dv_g.append(_dot(p[w:].astype(BF16), dog_j, NN) + _dot(p2.astype(BF16), dog_n, NN))
            t = jnp.exp(sink_ref[g] - l_j) * d_j
            dsk.extend([-jnp.sum(t[:, h * w:(h + 1) * w], axis=-1, keepdims=True) for h in range(GRP)])
        cos, nsin = cos_ref[...], sin_ref[...]
        dq = _rope(jnp.concatenate(dqs, axis=-1), cos, nsin)
        dk = _rope(jnp.concatenate([jnp.where(lo, dk_g[2 * m], dk_g[2 * m + 1]) for m in range(NKV // 2)], axis=-1),
                   cos, nsin)
        dv = jnp.concatenate([jnp.where(lo, dv_g[2 * m], dv_g[2 * m + 1]) for m in range(NKV // 2)], axis=-1)
        out_ref[...] = jnp.concatenate([dq, dk, dv], axis=-1).astype(BF16)
        dsink = jnp.concatenate(dsk, axis=-1)

        @pl.when(j == 0)
        def _():
            dsink_ref[...] = dsink

        @pl.when(j > 0)
        def _():
            dsink_ref[...] += dsink

    nxt = lambda j: jnp.minimum(j + 1, nblk - 1)
    prv = lambda j: jnp.maximum(j - 1, 0)
    va = OFF_VA // KV_W
    return _pallas(
        body, after=after, name=name, grid=(nblk,),
        in_specs=[pl.BlockSpec((w, ATT_W), lambda j: (j, 0)), pl.BlockSpec((w, ATT_W), lambda j: (nxt(j), 0)),
                  pl.BlockSpec((w, KV_W), lambda j: (j, 0)), pl.BlockSpec((w, KV_W), lambda j: (prv(j), 0)),
                  pl.BlockSpec((w, KV_W), lambda j: (j, va)), pl.BlockSpec((w, KV_W), lambda j: (prv(j), va)),
                  pl.BlockSpec((w, ATT_W), lambda j: (j, 0)), pl.BlockSpec((w, ATT_W), lambda j: (nxt(j), 0)),
                  pl.BlockSpec((w, ATT_W), lambda j: (j, 0)), pl.BlockSpec((w, ATT_W), lambda j: (nxt(j), 0)),
                  pl.BlockSpec((None, NQ, w), lambda j: (j, 0, 0)),
                  pl.BlockSpec((None, NQ, w), lambda j: (nxt(j), 0, 0)),
                  pl.BlockSpec((NKV, 1, GRP * w), lambda j: (0, 0, 0)),
                  pl.BlockSpec((w, LANE), lambda j: (j, 0)), pl.BlockSpec((w, LANE), lambda j: (j, 0))],
        out_specs=[pl.BlockSpec((w, ATT_W + 2 * KV_W), lambda j: (j, 0)), pl.BlockSpec((1, NQ), lambda j: (0, 0))],
        out_shape=[_sds((s, ATT_W + 2 * KV_W), BF16), _sds((1, NQ), F32)],
        compiler_params=_params(dimension_semantics=("arbitrary",)),
    )(qr, qr, kr, kr, proj, proj, do, do, o, o, lse, lse, sinks, cos4, nsin4)


def _mm_nn(name, a, w, *, out_dtypes, epilogue=_store, bm_pref=1024, bn_pref=1024, after=()):
    m, k = a.shape
    n = w.shape[-1]
    bm, bn = _tile(m, bm_pref), _tile(n, bn_pref)
    tile = pl.BlockSpec((bm, bn), lambda i, j, kk: (i, j))
    return _mm(name, a, w, dims=NN, grid=(m // bm, n // bn, 1),
               a_spec=pl.BlockSpec((bm, k), lambda i, j, kk: (i, 0)),
               b_spec=pl.BlockSpec((k, bn), lambda i, j, kk: (0, j)),
               out_shape=[_sds((m, n), dt) for dt in out_dtypes], out_specs=[tile] * len(out_dtypes),
               epilogue=epilogue, after=after)


def _mm_tn(name, a, b, *, bm_pref=1024, bn_pref=1024, after=()):
    s, m = a.shape
    n = b.shape[-1]
    bm, bn = _tile(m, bm_pref), _tile(n, bn_pref)
    return _mm(name, a, b, dims=TN, grid=(m // bm, n // bn, 1),
               a_spec=pl.BlockSpec((s, bm), lambda i, j, kk: (0, i)),
               b_spec=pl.BlockSpec((s, bn), lambda i, j, kk: (0, j)),
               out_shape=[_sds((m, n), BF16)], out_specs=[pl.BlockSpec((bm, bn), lambda i, j, kk: (i, j))],
               epilogue=_store, after=after)[0]


def _mm_residual(name, a, w, x, after=()):
    s, k = a.shape
    d = w.shape[-1]
    pref = 1024 if k <= 2048 else 512
    bm, bn = _tile(s, pref), _tile(d, pref)
    tile = pl.BlockSpec((bm, bn), lambda i, j, kk: (i, j))
    return _mm(name, a, w, dims=NN, grid=(s // bm, d // bn, 1),
               a_spec=pl.BlockSpec((bm, k), lambda i, j, kk: (i, 0)),
               b_spec=pl.BlockSpec((k, bn), lambda i, j, kk: (0, j)),
               extras=(x,), extra_specs=(tile,), out_shape=[_sds((s, d), F32)], out_specs=[tile],
               epilogue=_ep_add_scaled, after=after)[0]


def _dw_pieces(name, a, pieces, widths, after=()):
    s, m = a.shape
    total = sum(widths)
    bm, bn = _tile(m, 1024), 512
    out, off = None, 0
    for p, (piece, wd) in enumerate(zip(pieces, widths)):
        if isinstance(piece, tuple):
            arr = piece[0]
            b_spec = pl.BlockSpec((None, s, bn), (lambda ix: lambda i, j: (ix, 0, j))(piece[1]))
        else:
            arr, b_spec = piece, pl.BlockSpec((s, bn), lambda i, j: (0, j))
        prev = () if out is None else (out,)

        def body(a_ref, b_ref, *rest):
            rest[-1][...] = _dot(a_ref[...], b_ref[...], TN).astype(BF16)

        out = _pallas(
            body, after=after if out is None else (), name=f"{name}_{p}", grid=(m // bm, wd // bn),
            in_specs=[pl.BlockSpec((s, bm), lambda i, j: (0, i)), b_spec] + [ANY] * len(prev),
            out_specs=pl.BlockSpec((bm, bn), (lambda c: lambda i, j: (i, c + j))(off // bn)),
            out_shape=_sds((m, total), BF16), input_output_aliases={2: 0} if prev else {},
            compiler_params=_params(dimension_semantics=("arbitrary", "arbitrary")),
        )(a, arr, *prev)
        off += wd
    return out


def _dx_pieces(name, pieces, widths, w, dr, after=()):
    s, d = dr.shape
    iw = w.shape[-1]
    bm, bn = _tile(s, 512), _tile(d, 512)
    arrs, specs = [], []
    for piece, wd in zip(pieces, widths):
        if isinstance(piece, tuple):
            arrs.append(piece[0])
            specs.append(pl.BlockSpec((None, bm, wd), (lambda idx: lambda i, j: (idx, i, 0))(piece[1])))
        else:
            arrs.append(piece)
            specs.append(pl.BlockSpec((bm, wd), lambda i, j: (i, 0)))
    n = len(arrs)

    def body(*refs):
        w_ref, dr_ref, o_ref = refs[n], refs[n + 1], refs[n + 2]
        acc = ALPHA * dr_ref[...]
        off = 0
        for p, wd in enumerate(widths):
            acc = acc + _dot(refs[p][...], w_ref[:, off:off + wd], NT)
            off += wd
        o_ref[...] = acc

    tile = pl.BlockSpec((bm, bn), lambda i, j: (i, j))
    return _pallas(
        body, after=after, name=name, grid=(s // bm, d // bn),
        in_specs=specs + [pl.BlockSpec((bn, iw), lambda i, j: (j, 0)), tile], out_specs=tile,
        out_shape=_sds((s, d), F32), compiler_params=_params(dimension_semantics=("arbitrary", "arbitrary")),
    )(*arrs, w, dr)


class _Gather:
    def __init__(self, tag, names, fulls, after):
        self.tag, self.names = tag, names
        self.axes = [SHARD_AXIS[n] for n in names]
        self.srcs = []
        self.mk1 = _mk_gather_ici(self.axes)
        self.mk2 = _mk_gather_d2d(self.axes)
        self.n_sem = 3 * len(names)
        self.s1, self.r1, self.lands, self.token = _split_start(
            tag + "_ici_start", self.srcs, [fulls[n] for n in names], self.mk1, self.n_sem, after)

    def forward(self, after=()):
        lands = _split_wait(self.tag + "_ici_wait", self.srcs, self.lands, self.s1, self.r1, self.mk1, after)
        self.s2, self.r2, self.lands, tok = _split_start(self.tag + "_d2d_start", [], lands, self.mk2, self.n_sem)
        return tok

    def done(self, after=()):
        lands = _split_wait(self.tag + "_d2d_wait", [], self.lands, self.s2, self.r2, self.mk2, after)
        return dict(zip(self.names, lands))


class _Reduce:
    def __init__(self, tag, names, parts, cidx, mcidx, after=()):
        self.tag, self.names, self.cidx, self.mcidx = tag, names, cidx, mcidx
        self.axes = [SHARD_AXIS[n] for n in names]
        self.parts = [parts[n] for n in names]
        self.mk = _mk_swap(self.axes)
        lands = []
        for p, ax in zip(self.parts, self.axes):
            k, n = p.shape
            lands.append(lax.empty((k, n // 2) if ax == 0 else (k // 2, n), BF16))
        self.s, self.r, self.lands, self.token = _split_start(
            tag + "_swap_start", self.parts, lands, self.mk, len(names), after)

    def scatter(self, after=()):
        got = _split_wait(self.tag + "_swap_wait", self.parts, self.lands, self.s, self.r, self.mk, after)
        self.sums = [_add_half(f"{self.tag}_add_{n}", p, g, ax, self.cidx)
                     for n, p, g, ax in zip(self.names, self.parts, got, self.axes)]
        self.mk = _mk_scatter(self.axes)
        lands = []
        for q, ax in zip(self.sums, self.axes):
            k, n = q.shape
            lands.append(lax.empty((3, k // 4, n) if ax == 0 else (3, k, n // 4), BF16))
        self.s, self.r, self.lands, tok = _split_start(
            self.tag + "_scatter_start", self.sums, lands, self.mk, 3 * len(self.names))
        return tok

    def exchange(self, after=()):
        slots = _split_wait(self.tag + "_scatter_wait", self.sums, self.lands, self.s, self.r, self.mk, after)
        halves = [_sum_half(f"{self.tag}_sum_{n}", q, sl, ax, self.mcidx)
                  for n, q, sl, ax in zip(self.names, self.sums, slots, self.axes)]
        self.mk = _mk_exchange(self.axes)
        self.s, self.r, self.lands, tok = _split_start(
            self.tag + "_exchange_start", [], halves, self.mk, len(self.names))
        return tok

    def done(self, after=()):
        grads = _split_wait(self.tag + "_exchange_wait", [], self.lands, self.s, self.r, self.mk, after)
        return dict(zip(self.names, grads))


def _pack(arrs):
    flat = jnp.concatenate([a.reshape(-1) for a in arrs])
    n = flat.shape[0]
    pad = (-n) % (8 * LANE)
    return jnp.pad(flat, (0, pad)).reshape(-1, LANE)


def _unpack(packed, shapes):
    flat = packed.reshape(-1)
    out, off = [], 0
    for sh in shapes:
        n = math.prod(sh)
        out.append(flat[off:off + n].reshape(sh))
        off += n
    return out


def kernel(x, mem, w_in, b_gate, ln_v_g, ln_v_b, w_s, b_s, sinks, w_br_a, w_br_b, w_o, ln1_g, ln1_b, w_xq, w_xkv, w_xo, ln2_g, ln2_b, w_up, w_down, ln3_g, ln3_b, loss_target, m_w_in, m_b_gate, m_ln_v_g, m_ln_v_b, m_w_s, m_b_s, m_sinks, m_w_br_a, m_w_br_b, m_w_o, m_ln1_g, m_ln1_b, m_w_xq, m_w_xkv, m_w_xo, m_ln2_g, m_ln2_b, m_w_up, m_w_down, m_ln3_g, m_ln3_b, v_w_in, v_b_gate, v_ln_v_g, v_ln_v_b, v_w_s, v_b_s, v_sinks, v_w_br_a, v_w_br_b, v_w_o, v_ln1_g, v_ln1_b, v_w_xq, v_w_xkv, v_w_xo, v_ln2_g, v_ln2_b, v_w_up, v_w_down, v_ln3_g, v_ln3_b):
    env = dict(locals())
    wts = {n: env[n] for n in WEIGHTS}
    mom_m = {n: env["m_" + n] for n in WEIGHTS}
    mom_v = {n: env["v_" + n] for n in WEIGHTS}
    s, d = x.shape[1], x.shape[2]
    dff = 4 * w_up.shape[-1]
    xf = x.reshape(s, d)
    tgt = loss_target.reshape(s, d)
    memf = mem.reshape(mem.shape[1], d)
    ax_x, ax_y, ax_c = lax.axis_index("x"), lax.axis_index("y"), lax.axis_index("c")
    meidx = jnp.reshape(2 * ax_x + ax_y, (1,)).astype(jnp.int32)
    cidx = jnp.reshape(ax_c, (1,)).astype(jnp.int32)
    mcidx = jnp.concatenate([meidx, cidx])

    inv = 1.0 / (10000.0 ** (jnp.arange(0, HD, 2, dtype=F32) / HD))
    ang = jnp.arange(s, dtype=F32)[:, None] * inv[None, :]
    cos, sin = jnp.cos(ang), jnp.sin(ang)
    cos4 = jnp.tile(cos, (1, 4))
    sin4 = jnp.concatenate([-sin, sin, -sin, sin], axis=-1)
    nsin4 = -sin4

    small = {}
    for n in SMALL:
        w = wts[n]
        if n == "w_s":
            small[n] = [w[l] for l in range(DEPTH)]
        elif n == "b_s":
            small["b_st"] = [w[l].T for l in range(DEPTH)]
        else:
            small[n] = [w[l][None, :] for l in range(DEPTH)]
    small["sink_rows"] = [jnp.repeat(sinks[l].reshape(NKV, GRP), CHUNK, axis=1)[:, None, :] for l in range(DEPTH)]

    fulls = [{}, {}]
    tok = ()
    gathers = [[None] * len(GROUPS_GATHER) for _ in range(DEPTH)]
    for gi, names in enumerate(GROUPS_GATHER):
        for n in names:
            fulls[0][n], fulls[1][n] = _place_own("place_" + n, wts[n], SHARD_AXIS[n], meidx, after=tok)
        gathers[0][gi] = _Gather(f"ag0_{gi}", names, fulls[0], tok)
        tok = (gathers[0][gi].token,)
    for gi, names in enumerate(GROUPS_GATHER):
        gathers[1][gi] = _Gather(f"ag1_{gi}", names, fulls[1], tok)
        tok = (gathers[1][gi].token,)

    xb = _cast2d("cast_x", xf, after=tok)
    memb = _cast2d("cast_mem", memf, after=tok)

    saved = []
    hf, hb = xf, xb
    nxt_tok = gathers[0][0].forward(after=tok)
    for l in range(DEPTH):
        t = f"l{l}_"
        ga, gb, gc, gd = gathers[l]
        full = ga.done(after=(nxt_tok, hb))
        sv = {"xf": hf, "xb": hb}
        proj = _mm_nn(t + "proj", hb, full["w_in"], out_dtypes=[F32], bn_pref=1280)[0]
        tok_b = gb.forward(after=(proj,))
        sg = _gmlp_fwd(t + "gmlp_fwd", proj, small["ln_v_g"][l], small["ln_v_b"][l], small["w_s"][l],
                       small["b_st"][l])
        attn, qr, kr, lse = _swa_fwd(t + "swa_fwd", proj, cos4, sin4, small["sink_rows"][l], after=(tok_b,))
        full.update(gb.done(after=(attn,)))
        merged, ya, yb = _gate_fwd(t + "gate_fwd", sg, attn, full["w_br_a"], full["w_br_b"], proj,
                                   small["b_gate"][l], d)
        tok_c = gc.forward(after=(merged,))
        r1 = _mm_residual(t + "o", merged, full["w_o"], hf, after=(tok_c,))
        x1, x1b = _ln_fwd(t + "ln1", r1, small["ln1_g"][l], small["ln1_b"][l])
        kv = _mm_nn(t + "xkv", memb, full["w_xkv"], out_dtypes=[BF16])[0]
        q, o, r2, x2, x2b = _xattn_fwd(t + "xattn_fwd", x1b, x1, full["w_xq"], kv, full["w_xo"],
                                       small["ln2_g"][l], small["ln2_b"][l])
        full.update(gc.done(after=(x2b,)))
        tok_d = gd.forward(after=(x2b,))

        def ep_up(acc, ex, outs):
            outs[0][...] = acc.astype(BF16)
            rl = jnp.maximum(acc, 0.0)
            outs[1][...] = (rl * rl).astype(BF16)

        h, a = _mm_nn(t + "up", x2b, full["w_up"], out_dtypes=[BF16, BF16], epilogue=ep_up, after=(tok_d,))
        full.update(gd.done(after=(h,)))
        nxt_tok = gathers[l + 1][0].forward(after=(h,)) if l + 1 < DEPTH else None
        r3 = _mm_residual(t + "down", a, full["w_down"], x2, after=() if nxt_tok is None else (nxt_tok,))
        x3, x3b = _ln_fwd(t + "ln3", r3, small["ln3_g"][l], small["ln3_b"][l])
        sv.update(proj=proj, sg=sg, attn=attn, qr=qr, kr=kr, lse=lse, merged=merged, ya=ya, yb=yb, r1=r1, x1=x1,
                  x1b=x1b, kv=kv, q=q, o=o, r2=r2, x2b=x2b, h=h, a=a, r3=r3, full=full)
        saved.append(sv)
        hf, hb = x3, x3b
    dy, loss11 = _loss_grad("loss", hf, tgt)
    loss = lax.psum(loss11[0, 0], ("x", "y", "c"))

    small_g = [None] * DEPTH
    grads = [{}, {}]
    pend_a = None
    pend_b = None
    g = dy
    for l in reversed(range(DEPTH)):
        t = f"l{l}_"
        sv = saved[l]
        full = sv["full"]
        dw, sgo = {}, {}
        dr3, dr3b, sgo["ln3_g"], sgo["ln3_b"] = _ln_bwd(t + "ln3_bwd", g, sv["r3"], small["ln3_g"][l],
                                                        after=() if pend_a is None else (tok_a,))
        bm, bn = _tile(s, 1024), _tile(dff, 1024)

        def ep_dh(acc, ex, outs):
            outs[0][...] = (acc * (2.0 * jnp.maximum(ex[0][...].astype(F32), 0.0))).astype(BF16)

        tile = pl.BlockSpec((bm, bn), lambda i, j, k: (i, j))
        dh = _mm(t + "dh", dr3b, full["w_down"], dims=NT, grid=(s // bm, dff // bn, 1),
                 a_spec=pl.BlockSpec((bm, d), lambda i, j, k: (i, 0)),
                 b_spec=pl.BlockSpec((bn, d), lambda i, j, k: (j, 0)),
                 extras=(sv["h"],), extra_specs=(tile,), out_shape=[_sds((s, dff), BF16)], out_specs=[tile],
                 epilogue=ep_dh)[0]
        if pend_a is not None:
            tok_pa = pend_a.exchange(after=(dh,))
            grads[l + 1].update(pend_b.done(after=(dh,)))
        dw["w_down"] = _mm_tn(t + "dw_down", sv["a"], dr3b, after=() if pend_a is None else (tok_pa,))
        dw["w_up"] = _mm_tn(t + "dw_up", sv["x2b"], dh)
        red_c = _Reduce(t + "rs_c", GROUPS_FWD[2], dw, cidx, mcidx)
        bm2, bn2 = _tile(s, 512), _tile(d, 512)
        tile2 = pl.BlockSpec((bm2, bn2), lambda i, j, k: (i, j))
        dx2 = _mm(t + "dx2", dh, full["w_up"], dims=NT, grid=(s // bm2, d // bn2, 1),
                  a_spec=pl.BlockSpec((bm2, dff), lambda i, j, k: (i, 0)),
                  b_spec=pl.BlockSpec((bn2, dff), lambda i, j, k: (j, 0)),
                  extras=(dr3,), extra_specs=(tile2,), out_shape=[_sds((s, d), F32)], out_specs=[tile2],
                  epilogue=_ep_add_scaled, after=(red_c.token,))[0]
        tok_c = red_c.scatter(after=(dx2,))
        if pend_a is not None:
            grads[l + 1].update(pend_a.done(after=(dx2,)))
            pend_a = None

        dr2, dr2b, sgo["ln2_g"], sgo["ln2_b"] = _ln_bwd(t + "ln2_bwd", dx2, sv["r2"], small["ln2_g"][l],
                                                        after=(tok_c,))
        dx1, dq, dkv = _xattn_bwd(t + "xattn_bwd", dr2b, dr2, sv["q"], sv["kv"], full["w_xo"], full["w_xq"])
        dw["w_xo"] = _mm_tn(t + "dw_xo", sv["o"], dr2b)
        dw["w_xq"] = _mm_tn(t + "dw_xq", sv["x1b"], dq)
        dw["w_xkv"] = _mm_tn(t + "dw_xkv", memb, _cast2d(t + "dkv_cast", dkv))

        dr1, dr1b, sgo["ln1_g"], sgo["ln1_b"] = _ln_bwd(t + "ln1_bwd", dx1, sv["r1"], small["ln1_g"][l])
        dya, dyb, dgate, dba, dbb = _gate_bwd(t + "gate_bwd", dr1b, full["w_o"], sv["proj"], sv["ya"], sv["yb"],
                                              small["b_gate"][l], d)
        sgo["b_gate"] = jnp.concatenate([dba, dbb], axis=-1)
        dw["w_o"] = _mm_tn(t + "dw_o", sv["merged"], dr1b)
        dw["w_br_a"] = _mm_tn(t + "dw_br_a", sv["sg"], dya)
        dw["w_br_b"] = _mm_tn(t + "dw_br_b", sv["attn"], dyb)
        red_b = _Reduce(t + "rs_b", GROUPS_FWD[1], dw, cidx, mcidx)

        def dbranch(name, dyx, w, after):
            bk2 = _tile(d, 1024)
            return _mm(name, dyx, w, dims=NT, grid=(s // bm, 1, d // bk2),
                       a_spec=pl.BlockSpec((bm, bk2), lambda i, j, k: (i, k)),
                       b_spec=pl.BlockSpec((w.shape[0], bk2), lambda i, j, k: (0, k)),
                       out_shape=[_sds((s, w.shape[0]), BF16)],
                       out_specs=[pl.BlockSpec((bm, w.shape[0]), lambda i, j, k: (i, 0))],
                       epilogue=_store, acc_shape=(bm, w.shape[0]), after=after)[0]

        dsg = dbranch(t + "dsg", dya, full["w_br_a"], (red_b.token,))
        dattn = dbranch(t + "dattn", dyb, full["w_br_b"], ())
        tok_c = red_c.exchange(after=(dattn, dsg))
        tok_b = red_b.scatter(after=(dattn, dsg))
        duv, sgo["w_s"], dbst, dlg, dlb = _gmlp_bwd(t + "gmlp_bwd", sv["proj"], dsg, small["ln_v_g"][l],
                                                    small["ln_v_b"][l], small["w_s"][l], small["b_st"][l])
        sgo["b_s"] = dbst.T
        sgo["ln_v_g"], sgo["ln_v_b"] = dlg, dlb
        dqkv, sgo["sinks"] = _swa_bwd(t + "swa_bwd", sv["qr"], sv["kr"], sv["proj"], dattn, sv["attn"], sv["lse"],
                                      small["sink_rows"][l], cos4, nsin4, after=(tok_b, tok_c))
        grads[l].update(red_c.done(after=(dqkv,)))
        pieces = (duv, dqkv, (dgate, 0), (dgate, 1))
        widths = (2 * GMLP_W, ATT_W + 2 * KV_W, d, d)
        tok_b = red_b.exchange(after=(dqkv, duv))
        dw["w_in"] = _dw_pieces(t + "dw_in", sv["xb"], pieces, widths, after=(tok_b,))
        red_a = _Reduce(t + "rs_a", GROUPS_FWD[0], dw, cidx, mcidx)
        g = _dx_pieces(t + "dx0", pieces, widths, full["w_in"], dr1, after=(red_a.token,))
        tok_a = red_a.scatter(after=(g,))
        pend_a, pend_b = red_a, red_b
        small_g[l] = sgo
    grad_x = g.reshape(x.shape)

    big_out = {}

    def adam_layer(l, names, after):
        done = []
        for n in names:
            prev = big_out.get(n)
            big_out[n] = _adamw(f"adamw{l}_{n}", wts[n], grads[l][n], mom_m[n], mom_v[n], l, prev, after=after)
            done.append(big_out[n][0])
        return done

    shapes = [wts[n].shape for n in SMALL]
    packed_g = _pack([jnp.stack([small_g[l][n].reshape(wts[n].shape[1:]) for l in range(DEPTH)]) for n in SMALL])
    me8 = jnp.reshape(4 * ax_x + 2 * ax_y + ax_c, (1,)).astype(jnp.int32)
    ar_s, ar_r, ar_land, tok_ar = _split_start("ar_start", [packed_g], [_place_slot("ar_place", packed_g, me8)],
                                               _mk_small, 7, after=(tok_a,))
    fill = []
    for names in GROUPS_FWD:
        fill += adam_layer(1, names, (tok_ar,))
    grads[0].update(pend_b.done(after=tuple(fill)))
    fill += adam_layer(0, GROUPS_FWD[1], (tok_ar,))
    ar_land = _split_wait("ar_wait", [packed_g], ar_land, ar_s, ar_r, _mk_small, after=tuple(fill))
    packed_g = _sum_slots("ar_sum", ar_land[0])
    pw, pm, pv = (_pack([src[n] for n in SMALL]) for src in (wts, mom_m, mom_v))
    small4 = _adamw("adamw_small", pw[None], packed_g, pm[None], pv[None], 0)
    small_out = [dict(zip(SMALL, _unpack(a[0], shapes))) for a in small4]
    tok_a = pend_a.exchange(after=(small4[0],))
    fill = adam_layer(0, GROUPS_FWD[2], (tok_a,))
    grads[0].update(pend_a.done(after=tuple(fill)))
    adam_layer(0, GROUPS_FWD[0], ())

    def pick(kind, n):
        return big_out[n][kind] if n in big_out else small_out[kind][n]

    return (loss, grad_x, *[pick(0, n) for n in WEIGHTS], *[pick(1, n) for n in WEIGHTS],
            *[pick(2, n) for n in WEIGHTS], *[pick(3, n) for n in WEIGHTS])
```

```python
import math

import jax
import jax.numpy as jnp
from jax import lax
from jax.experimental import pallas as pl
from jax.experimental.pallas import tpu as pltpu

F32 = jnp.float32
BF16 = jnp.bfloat16
MESH = pl.DeviceIdType.MESH
ANY = pl.BlockSpec(memory_space=pl.ANY)
HBM = pl.BlockSpec(memory_space=pltpu.HBM)
SEM = pl.BlockSpec(memory_space=pltpu.SEMAPHORE)
VMEM_SPEC = pl.BlockSpec(memory_space=pltpu.VMEM)
EFFECT = pltpu.SideEffectType.DATAFLOW_SIDE_EFFECTING

DEPTH = 2
CHUNK = 128
GMLP_W = 1024
GROUPS = 8
NQ, NKV, HD = 16, 4, 64
ATT_W = NQ * HD
KV_W = NKV * HD
XH, XHD = 4, 128
X_W = XH * XHD
LN_EPS = 1e-5
ALPHA = (2 * DEPTH) ** 0.25
OFF_Q = 2 * GMLP_W
OFF_K = OFF_Q + ATT_W
OFF_VA = OFF_K + KV_W
OFF_GA = OFF_VA + KV_W
NEG = -1e30

ADAM_LR, ADAM_B1, ADAM_B2, ADAM_EPS, ADAM_WD, ADAM_STEP = 0.001, 0.9, 0.999, 1e-08, 0.01, 10

V7X_VMEM_BYTES = 64 * 1024 * 1024
VMEM_LIMIT = V7X_VMEM_BYTES - 4 * 1024 * 1024
LANE = 128

BIG = ("w_in", "w_br_a", "w_br_b", "w_o", "w_xq", "w_xkv", "w_xo", "w_up", "w_down")
SHARD_AXIS = {"w_in": 1, "w_br_a": 1, "w_br_b": 1, "w_o": 0, "w_xq": 0, "w_xkv": 0, "w_xo": 1,
              "w_up": 1, "w_down": 0}
GROUPS_GATHER = (("w_in",), ("w_br_a", "w_br_b", "w_o", "w_xq", "w_xkv", "w_xo"), ("w_up",), ("w_down",))
GROUPS_FWD = (("w_in",), ("w_br_a", "w_br_b", "w_o", "w_xq", "w_xkv", "w_xo"), ("w_up", "w_down"))
SMALL = ("b_gate", "ln_v_g", "ln_v_b", "w_s", "b_s", "sinks", "ln1_g", "ln1_b", "ln2_g", "ln2_b",
         "ln3_g", "ln3_b")
WEIGHTS = ("w_in", "b_gate", "ln_v_g", "ln_v_b", "w_s", "b_s", "sinks", "w_br_a", "w_br_b", "w_o",
           "ln1_g", "ln1_b", "w_xq", "w_xkv", "w_xo", "ln2_g", "ln2_b", "w_up", "w_down", "ln3_g", "ln3_b")


def _pallas(body, after=(), **kw):
    n_after = len(after)
    if not n_after:
        return pl.pallas_call(body, **kw)
    n_in = len(kw["in_specs"])
    kw["in_specs"] = list(kw["in_specs"]) + [ANY] * n_after

    def tied(*refs):
        return body(*refs[:n_in], *refs[n_in + n_after:])

    call = pl.pallas_call(tied, **kw)
    return lambda *ops: call(*ops, *after)


def _params(**kw):
    return pltpu.CompilerParams(vmem_limit_bytes=VMEM_LIMIT, **kw)


def _tile(dim, pref, unit=LANE):
    best = None
    t = unit
    while t <= min(dim, pref):
        if dim % t == 0:
            best = t
        t += unit
    return best if best is not None else dim


def _dot(a, b, dims):
    return lax.dot_general(a, b, (dims, ((), ())), preferred_element_type=F32)


NN = ((1,), (0,))
NT = ((1,), (1,))
TN = ((0,), (0,))


def _bf(x):
    return x if x.dtype == BF16 else x.astype(BF16)


def _sds(shape, dtype):
    return jax.ShapeDtypeStruct(shape, dtype)


def _mm(name, a, b, *, dims, grid, a_spec, b_spec, out_shape, out_specs, epilogue,
        extras=(), extra_specs=(), after=()):
    assert grid[2] == 1
    n_ex, n_out = len(extras), len(out_shape)

    def body(*refs):
        ex = refs[2:2 + n_ex]
        outs = refs[2 + n_ex:2 + n_ex + n_out]
        epilogue(_dot(_bf(refs[0][...]), _bf(refs[1][...]), dims), ex, outs)

    return _pallas(
        body, after=after, name=name, grid=grid, in_specs=[a_spec, b_spec, *extra_specs], out_specs=list(out_specs),
        out_shape=list(out_shape), compiler_params=_params(dimension_semantics=("arbitrary",) * 3),
    )(a, b, *extras)


def _store(acc, ex, outs):
    for o in outs:
        o[...] = acc.astype(o.dtype)


def _ln_rows(r, g, b):
    mu = jnp.mean(r, axis=-1, keepdims=True)
    xc = r - mu
    var = jnp.mean(xc * xc, axis=-1, keepdims=True)
    rstd = lax.rsqrt(var + LN_EPS)
    xhat = xc * rstd
    return xhat * g + b, xhat, rstd


def _ep_add_scaled(acc, ex, outs):
    outs[0][...] = acc + ALPHA * ex[0][...]


def _ln_fwd(name, r, g, b):
    s, d = r.shape
    bm = _tile(s, 256)

    def body(r_ref, g_ref, b_ref, y_ref, yb_ref):
        y, _, _ = _ln_rows(r_ref[...], g_ref[...], b_ref[...])
        y_ref[...] = y
        yb_ref[...] = y.astype(BF16)

    row = pl.BlockSpec((bm, d), lambda i: (i, 0))
    vec = pl.BlockSpec((1, d), lambda i: (0, 0))
    return _pallas(body, name=name, grid=(s // bm,), in_specs=[row, vec, vec], out_specs=[row, row],
                   out_shape=[_sds((s, d), F32), _sds((s, d), BF16)], compiler_params=_params())(r, g, b)


_GC = math.sqrt(2.0 / math.pi)


def _gelu(x):
    t = jnp.tanh(_GC * (x + 0.044715 * (x * x * x)))
    return 0.5 * x * (1.0 + t), t


def _gelu_grad(x, t):
    return 0.5 * (1.0 + t) + 0.5 * x * (1.0 - t * t) * (_GC * (1.0 + 3.0 * 0.044715 * x * x))


def _sigmoid(x):
    return 1.0 / (1.0 + jnp.exp(-x))


GRP = NQ // NKV


def _band_mask(prev_ok, prev_only=False):
    rows = CHUNK if prev_only else 2 * CHUNK
    key = lax.broadcasted_iota(jnp.int32, (rows, GRP * CHUNK), 0)
    qry = jnp.bitwise_and(lax.broadcasted_iota(jnp.int32, (rows, GRP * CHUNK), 1), CHUNK - 1)
    prev = jnp.logical_and(jnp.logical_and(key < CHUNK, key > qry), prev_ok)
    if prev_only:
        return prev
    return jnp.logical_or(prev, jnp.logical_and(key >= CHUNK, key - CHUNK <= qry))


def _pair(x, g):
    return x[:, (g // 2) * LANE:(g // 2 + 1) * LANE]


def _own_head(x, g):
    xp = _pair(x, g)
    lane = lax.broadcasted_iota(jnp.int32, xp.shape, 1)
    lo = (g % 2) * HD
    return jnp.where(jnp.logical_and(lane >= lo, lane < lo + HD), xp, jnp.zeros_like(xp))


def _stack_heads(x, g, dtype=BF16):
    a = x[:, g * GRP * HD:g * GRP * HD + LANE]
    b = x[:, g * GRP * HD + LANE:(g + 1) * GRP * HD]
    ar, br = pltpu.roll(a, HD, 1), pltpu.roll(b, HD, 1)
    parts = [a, ar, b, br] if g % 2 == 0 else [ar, a, br, b]
    return jnp.concatenate(parts, axis=0).astype(dtype)


def _unstack_heads(og, g):
    o = [og[h * CHUNK:(h + 1) * CHUNK] for h in range(GRP)]
    lo = lax.broadcasted_iota(jnp.int32, (CHUNK, LANE), 1) < HD
    if g % 2 == 0:
        x0, x1, x2, x3 = o[0], pltpu.roll(o[1], HD, 1), o[2], pltpu.roll(o[3], HD, 1)
    else:
        x0, x1, x2, x3 = pltpu.roll(o[0], HD, 1), o[1], pltpu.roll(o[2], HD, 1), o[3]
    return [jnp.where(lo, x0, x1), jnp.where(lo, x2, x3)]


def _stack_rows(x, g):
    return jnp.concatenate([x[g * GRP + h:g * GRP + h + 1] for h in range(GRP)], axis=-1)


def _head_lane_sums(x, g):
    lane = lax.broadcasted_iota(jnp.int32, (8, LANE), 1)
    lo_lane = (g % 2) * HD
    sel = jnp.where(jnp.logical_and(lane >= lo_lane, lane < lo_lane + HD), 1.0, 0.0).astype(BF16)
    hi = x.astype(BF16)
    lo = (x - hi.astype(F32)).astype(BF16)
    return (_dot(sel, hi, NT) + _dot(sel, lo, NT))[0:1]


def _rope(x, cos, sin_signed):
    w = x.shape[-1]
    lane = lax.broadcasted_iota(jnp.int32, x.shape, 1)
    first = (lane % HD) < (HD // 2)
    partner = jnp.where(first, pltpu.roll(x, w - HD // 2, 1), pltpu.roll(x, HD // 2, 1))
    reps = w // LANE
    return x * jnp.tile(cos, (1, reps)) + partner * jnp.tile(sin_signed, (1, reps))


def _cast2d(name, x, after=()):
    s, d = x.shape
    bm = _tile(s, 512, 8)

    def body(x_ref, o_ref):
        o_ref[...] = x_ref[...].astype(BF16)

    spec = pl.BlockSpec((bm, d), lambda i: (i, 0))
    return _pallas(body, after=after, name=name, grid=(s // bm,), in_specs=[spec], out_specs=spec,
                   out_shape=_sds(x.shape, BF16), compiler_params=_params())(x)


def _place():
    x, y, c = lax.axis_index("x"), lax.axis_index("y"), lax.axis_index("c")
    chips = [(1 - x, y), (x, 1 - y), (1 - x, 1 - y)]
    return x, y, c, chips


def _cut(ref, axis, chip=None, half=None):
    k, n = ref.shape[-2], ref.shape[-1]
    rows, cols = slice(None), slice(None)
    if chip is not None:
        if axis == 0:
            rows = pl.ds(pl.multiple_of(chip * (k // 4), 8), k // 4)
        else:
            cols = pl.ds(pl.multiple_of(chip * (n // 4), LANE), n // 4)
    if half is not None:
        if axis == 0:
            cols = pl.ds(pl.multiple_of(half * (n // 2), LANE), n // 2)
        else:
            rows = pl.ds(pl.multiple_of(half * (k // 2), 8), k // 2)
    return ref.at[rows, cols]


def _split_start(name, srcs, lands, make, n_sem, after=()):
    ns, nl, na = len(srcs), len(lands), len(after)

    def body(*refs):
        src, land = refs[:ns], refs[ns:ns + nl]
        outs = refs[ns + nl + na:]
        for out_cp, _ in make(src, land, outs[0], outs[1]):
            out_cp.start()
        outs[-1][...] = jnp.zeros_like(outs[-1])

    res = pl.pallas_call(
        body, name=name, in_specs=[HBM] * (ns + nl) + [ANY] * na,
        out_specs=[SEM, SEM] + [HBM] * nl + [VMEM_SPEC],
        out_shape=[pltpu.SemaphoreType.DMA((n_sem,)), pltpu.SemaphoreType.DMA((n_sem,))]
        + [pltpu.HBM(a.shape, a.dtype) for a in lands] + [_sds((8, LANE), F32)],
        input_output_aliases={ns + i: 2 + i for i in range(nl)},
        compiler_params=pltpu.CompilerParams(has_side_effects=EFFECT),
    )(*[pltpu.with_memory_space_constraint(a, pltpu.HBM) for a in (*srcs, *lands)], *after)
    return res[0], res[1], list(res[2:2 + nl]), res[-1]


def _split_wait(name, srcs, lands, ssem, rsem, make, after=()):
    ns, nl, na = len(srcs), len(lands), len(after)

    def body(*refs):
        src, land = refs[:ns], refs[ns:ns + nl]
        s_ref, r_ref = refs[ns + nl], refs[ns + nl + 1]
        pairs = make(src, land, s_ref, r_ref)
        for _, in_cp in pairs:
            in_cp.wait_recv()
        for out_cp, _ in pairs:
            out_cp.wait_send()

    res = pl.pallas_call(
        body, name=name, in_specs=[HBM] * (ns + nl) + [SEM, SEM] + [ANY] * na,
        out_specs=[HBM] * nl, out_shape=[pltpu.HBM(a.shape, a.dtype) for a in lands],
        input_output_aliases={ns + i: i for i in range(nl)},
        compiler_params=pltpu.CompilerParams(has_side_effects=EFFECT),
    )(*srcs, *lands, ssem, rsem, *after)
    return list(res)


def _rcopy(src, dst, ssem, rsem, k, dev):
    return pltpu.make_async_remote_copy(src_ref=src, dst_ref=dst, send_sem=ssem.at[k], recv_sem=rsem.at[k],
                                        device_id=dev, device_id_type=MESH)


def _mk_gather_ici(axes):
    def make(src, land, ssem, rsem):
        x, y, c, chips = _place()
        me = 2 * x + y
        pairs = []
        for w, ax in enumerate(axes):
            mine = _cut(land[w], ax, chip=me, half=c)
            for j, (px, py) in enumerate(chips):
                dev = (px, py, c)
                got = _cut(land[w], ax, chip=2 * px + py, half=c)
                pairs.append((_rcopy(mine, mine, ssem, rsem, 3 * w + j, dev),
                              _rcopy(got, got, ssem, rsem, 3 * w + j, dev)))
        return pairs
    return make


def _mk_gather_d2d(axes):
    def make(src, land, ssem, rsem):
        x, y, c, chips = _place()
        sib = (x, y, 1 - c)
        pairs = []
        for w, ax in enumerate(axes):
            for j, (px, py) in enumerate(chips):
                have = _cut(land[w], ax, chip=2 * px + py, half=c)
                want = _cut(land[w], ax, chip=2 * px + py, half=1 - c)
                pairs.append((_rcopy(have, have, ssem, rsem, 3 * w + j, sib),
                              _rcopy(want, want, ssem, rsem, 3 * w + j, sib)))
        return pairs
    return make


def _mk_swap(axes):
    def make(src, land, ssem, rsem):
        x, y, c, _ = _place()
        sib = (x, y, 1 - c)
        pairs = []
        for w, ax in enumerate(axes):
            cp = _rcopy(_cut(src[w], ax, half=1 - c), land[w], ssem, rsem, w, sib)
            pairs.append((cp, cp))
        return pairs
    return make


def _mk_scatter(axes):
    def make(src, land, ssem, rsem):
        x, y, c, chips = _place()
        pairs = []
        for w, ax in enumerate(axes):
            for j, (px, py) in enumerate(chips):
                cp = _rcopy(_cut(src[w], ax, chip=2 * px + py), land[w].at[j], ssem, rsem, 3 * w + j, (px, py, c))
                pairs.append((cp, cp))
        return pairs
    return make


def _mk_exchange(axes):
    def make(src, land, ssem, rsem):
        x, y, c, _ = _place()
        sib = (x, y, 1 - c)
        pairs = []
        for w, ax in enumerate(axes):
            have = _cut(land[w], ax, half=c)
            want = _cut(land[w], ax, half=1 - c)
            pairs.append((_rcopy(have, have, ssem, rsem, w, sib), _rcopy(want, want, ssem, rsem, w, sib)))
        return pairs
    return make


def _place_own(name, shard, axis, meidx, after=()):
    _, r, c = shard.shape
    full = (4 * r, c) if axis == 0 else (r, 4 * c)
    br = _tile(r, 512, 8)
    nb = r // br
    if axis == 0:
        ospec = pl.BlockSpec((br, c), lambda i, me: (me[0] * nb + i, 0))
    else:
        ospec = pl.BlockSpec((br, c), lambda i, me: (i, me[0]))
    n_after = len(after)

    def body(me_ref, s_ref, *rest):
        o0_ref, o1_ref = rest[n_after:]
        o0_ref[...] = s_ref[0].astype(BF16)
        o1_ref[...] = s_ref[1].astype(BF16)

    return pl.pallas_call(
        body, name=name,
        grid_spec=pltpu.PrefetchScalarGridSpec(
            num_scalar_prefetch=1, grid=(nb,),
            in_specs=[pl.BlockSpec((2, br, c), lambda i, me: (0, i, 0))] + [ANY] * n_after,
            out_specs=[ospec, ospec]),
        out_shape=[_sds(full, BF16)] * 2, compiler_params=_params(),
    )(meidx, shard, *after)


def _add_half(name, part, got, axis, cidx):
    k, n = got.shape
    bm = _tile(k, 512, 8)
    nb = k // bm
    if axis == 0:
        pspec = pl.BlockSpec((bm, n), lambda i, c: (i, c[0]))
    else:
        pspec = pl.BlockSpec((bm, n), lambda i, c: (c[0] * nb + i, 0))

    def body(c_ref, a_ref, b_ref, o_ref):
        o_ref[...] = (a_ref[...].astype(F32) + b_ref[...].astype(F32)).astype(BF16)

    return pl.pallas_call(
        body, name=name,
        grid_spec=pltpu.PrefetchScalarGridSpec(
            num_scalar_prefetch=1, grid=(nb,), in_specs=[pspec, pl.BlockSpec((bm, n), lambda i, c: (i, 0))],
            out_specs=pl.BlockSpec((bm, n), lambda i, c: (i, 0))),
        out_shape=_sds((k, n), BF16), compiler_params=_params(),
    )(cidx, part, got)


def _sum_half(name, own, slots, axis, mc):
    _, r, cc = slots.shape
    br = _tile(r, 256, 8)
    nb = r // br
    if axis == 0:
        own_spec = pl.BlockSpec((br, cc), lambda i, mc: (mc[0] * nb + i, 0))
        out_spec = pl.BlockSpec((br, cc), lambda i, mc: (i, mc[1]))
        shape = (r, 2 * cc)
    else:
        own_spec = pl.BlockSpec((br, cc), lambda i, mc: (i, mc[0]))
        out_spec = pl.BlockSpec((br, cc), lambda i, mc: (mc[1] * nb + i, 0))
        shape = (2 * r, cc)

    def body(mc_ref, own_ref, s_ref, o_ref):
        acc = own_ref[...].astype(F32)
        for i in range(3):
            acc = acc + s_ref[i].astype(F32)
        o_ref[...] = acc

    return pl.pallas_call(
        body, name=name,
        grid_spec=pltpu.PrefetchScalarGridSpec(
            num_scalar_prefetch=1, grid=(nb,),
            in_specs=[own_spec, pl.BlockSpec((3, br, cc), lambda i, mc: (0, i, 0))], out_specs=out_spec),
        out_shape=_sds(shape, F32), compiler_params=_params(),
    )(mc, own, slots)


def _mk_small(src, land, ssem, rsem):
    x, y, c, _ = _place()
    me = 4 * x + 2 * y + c
    pairs = []
    for k in range(1, 8):
        peer = (1 - x if k & 4 else x, 1 - y if k & 2 else y, 1 - c if k & 1 else c)
        got = land[0].at[4 * peer[0] + 2 * peer[1] + peer[2]]
        pairs.append((_rcopy(src[0], land[0].at[me], ssem, rsem, k - 1, peer),
                      _rcopy(got, got, ssem, rsem, k - 1, peer)))
    return pairs


def _place_slot(name, packed, me8):
    rows, lanes = packed.shape
    br = _tile(rows, 512, 8)

    def body(me_ref, p_ref, o_ref):
        o_ref[...] = p_ref[...]

    return pl.pallas_call(
        body, name=name,
        grid_spec=pltpu.PrefetchScalarGridSpec(
            num_scalar_prefetch=1, grid=(rows // br,),
            in_specs=[pl.BlockSpec((br, lanes), lambda i, me: (i, 0))],
            out_specs=pl.BlockSpec((None, br, lanes), lambda i, me: (me[0], i, 0))),
        out_shape=_sds((8, rows, lanes), F32), compiler_params=_params(),
    )(me8, packed)


def _sum_slots(name, slots):
    _, rows, lanes = slots.shape
    br = _tile(rows, 512, 8)

    def body(s_ref, o_ref):
        acc = s_ref[0]
        for i in range(1, 8):
            acc = acc + s_ref[i]
        o_ref[...] = acc

    return pl.pallas_call(
        body, name=name, grid=(rows // br,), in_specs=[pl.BlockSpec((8, br, lanes), lambda i: (0, i, 0))],
        out_specs=pl.BlockSpec((br, lanes), lambda i: (i, 0)), out_shape=_sds((rows, lanes), F32),
        compiler_params=_params(),
    )(slots)


def _adamw_math(w, g, m, v):
    m2 = ADAM_B1 * m + (1.0 - ADAM_B1) * g
    v2 = ADAM_B2 * v + (1.0 - ADAM_B2) * (g * g)
    m_hat = m2 / (1.0 - ADAM_B1 ** ADAM_STEP)
    v_hat = v2 / (1.0 - ADAM_B2 ** ADAM_STEP)
    delta = -ADAM_LR * (m_hat / (jnp.sqrt(v_hat) + ADAM_EPS) + ADAM_WD * w)
    return delta, m2, v2


def _adamw(name, w, g, m, v, layer, prev=None, after=()):
    _, r, c = w.shape
    br = _tile(r, 256, 8)
    n_prev = 0 if prev is None else 4

    def body(*refs):
        w_ref, g_ref, m_ref, v_ref = refs[:4]
        go_ref, d_ref, mo_ref, vo_ref = refs[4 + n_prev:]
        gg = g_ref[...]
        delta, m2, v2 = _adamw_math(w_ref[...], gg, m_ref[...], v_ref[...])
        go_ref[...] = gg
        d_ref[...] = delta
        mo_ref[...] = m2
        vo_ref[...] = v2

    spec = pl.BlockSpec((None, br, c), lambda i: (layer, i, 0))
    return _pallas(
        body, after=after, name=name, grid=(r // br,),
        in_specs=[spec, pl.BlockSpec((br, c), lambda i: (i, 0)), spec, spec] + [ANY] * n_prev,
        out_specs=[spec] * 4, out_shape=[_sds(w.shape, F32)] * 4,
        input_output_aliases={4 + i: i for i in range(n_prev)}, compiler_params=_params(),
    )(w, g, m, v, *(prev or ()))


def _gmlp_fwd(name, proj, ln_g, ln_b, w_s, b_st):
    s = proj.shape[0]

    def body(u_ref, v_ref, g_ref, b_ref, ws_ref, bst_ref, sg_ref):
        gu, _ = _gelu(u_ref[...])
        gv, _ = _gelu(v_ref[...])
        vn, _, _ = _ln_rows(gv, g_ref[...], b_ref[...])
        vn = vn.astype(BF16)
        row = lax.broadcasted_iota(jnp.int32, (CHUNK, CHUNK), 0)
        col = lax.broadcasted_iota(jnp.int32, (CHUNK, CHUNK), 1)
        tril = col <= row
        outs = []
        for g in range(GROUPS):
            sl = slice(g * LANE, (g + 1) * LANE)
            w = jnp.where(tril, ws_ref[g], 0.0).astype(BF16)
            mixed = _dot(w, vn[:, sl], NN) + bst_ref[:, g:g + 1]
            outs.append(gu[:, sl] * mixed)
        sg_ref[...] = jnp.concatenate(outs, axis=-1).astype(BF16)

    return _pallas(
        body, name=name, grid=(s // CHUNK,),
        in_specs=[pl.BlockSpec((CHUNK, GMLP_W), lambda n: (n, 0)), pl.BlockSpec((CHUNK, GMLP_W), lambda n: (n, 1)),
                  pl.BlockSpec((1, GMLP_W), lambda n: (0, 0)), pl.BlockSpec((1, GMLP_W), lambda n: (0, 0)),
                  pl.BlockSpec((GROUPS, CHUNK, CHUNK), lambda n: (0, 0, 0)),
                  pl.BlockSpec((CHUNK, GROUPS), lambda n: (0, 0))],
        out_specs=pl.BlockSpec((CHUNK, GMLP_W), lambda n: (n, 0)),
        out_shape=_sds((s, GMLP_W), BF16), compiler_params=_params(),
    )(proj, proj, ln_g, ln_b, w_s, b_st)


def _swa_fwd(name, proj, cos4, sin4, sinks, after=()):
    s = proj.shape[0]
    w = CHUNK
    scale = HD ** -0.5

    def body(q_ref, k_ref, v_ref, cos_ref, sin_ref, sink_ref, o_ref, qr_ref, kr_ref, lse_ref, kprev, vprev):
        n = pl.program_id(0)

        @pl.when(n == 0)
        def _():
            kprev[...] = jnp.zeros_like(kprev)
            vprev[...] = jnp.zeros_like(vprev)

        cos, sin = cos_ref[...], sin_ref[...]
        qr = _rope(q_ref[...], cos, sin)
        kr = _rope(k_ref[...], cos, sin).astype(BF16)
        vb = v_ref[...].astype(BF16)
        kk = jnp.concatenate([kprev[...], kr], axis=0)
        vv = jnp.concatenate([vprev[...], vb], axis=0)
        valid = _band_mask(n > 0)
        outs, lses = [], []
        for g in range(NKV):
            sc = jnp.where(valid, _dot(_own_head(kk, g), _stack_heads(qr, g), NT) * scale, NEG)
            sink = sink_ref[g]
            mx = jnp.maximum(jnp.max(sc, axis=0, keepdims=True), sink)
            p = jnp.exp(sc - mx)
            den = jnp.sum(p, axis=0, keepdims=True) + jnp.exp(sink - mx)
            og = _dot((p * (1.0 / den)).astype(BF16), _pair(vv, g), TN)
            outs.extend(_unstack_heads(og, g))
            lg = mx + jnp.log(den)
            lses.extend([lg[:, h * w:(h + 1) * w] for h in range(GRP)])
        o_ref[...] = jnp.concatenate(outs, axis=-1).astype(BF16)
        lse_ref[...] = jnp.concatenate(lses, axis=0)
        qr_ref[...] = qr.astype(BF16)
        kr_ref[...] = kr
        kprev[...] = kr
        vprev[...] = vb

    return _pallas(
        body, after=after, name=name, grid=(s // w,),
        in_specs=[pl.BlockSpec((w, ATT_W), lambda n: (n, OFF_Q // ATT_W)),
                  pl.BlockSpec((w, KV_W), lambda n: (n, OFF_K // KV_W)),
                  pl.BlockSpec((w, KV_W), lambda n: (n, OFF_VA // KV_W)),
                  pl.BlockSpec((w, LANE), lambda n: (n, 0)), pl.BlockSpec((w, LANE), lambda n: (n, 0)),
                  pl.BlockSpec((NKV, 1, GRP * w), lambda n: (0, 0, 0))],
        out_specs=[pl.BlockSpec((w, ATT_W), lambda n: (n, 0)), pl.BlockSpec((w, ATT_W), lambda n: (n, 0)),
                   pl.BlockSpec((w, KV_W), lambda n: (n, 0)), pl.BlockSpec((None, NQ, w), lambda n: (n, 0, 0))],
        out_shape=[_sds((s, ATT_W), BF16), _sds((s, ATT_W), BF16), _sds((s, KV_W), BF16),
                   _sds((s // w, NQ, w), F32)],
        scratch_shapes=[pltpu.VMEM((w, KV_W), BF16), pltpu.VMEM((w, KV_W), BF16)],
        compiler_params=_params(dimension_semantics=("arbitrary",)),
    )(proj, proj, proj, cos4, sin4, sinks)


def _gate_fwd(name, sg, attn, wa, wb, proj, b_gate, d):
    s = sg.shape[0]
    bm, bn = _tile(s, 1024), _tile(d, 512)
    off_a, off_b = OFF_GA // bn, (OFF_GA + d) // bn

    def body(sg_ref, at_ref, wa_ref, wb_ref, ga_ref, gb_ref, ba_ref, bb_ref, m_ref, ya_ref, yb_ref):
        ya = _dot(sg_ref[...], wa_ref[...], NN)
        yb = _dot(at_ref[...], wb_ref[...], NN)
        sa = _sigmoid(ga_ref[...] + ba_ref[...])
        sb = _sigmoid(gb_ref[...] + bb_ref[...])
        m_ref[...] = (sa * ya + sb * yb).astype(BF16)
        ya_ref[...] = ya.astype(BF16)
        yb_ref[...] = yb.astype(BF16)

    tile = pl.BlockSpec((bm, bn), lambda i, j: (i, j))
    return _pallas(
        body, name=name, grid=(s // bm, d // bn),
        in_specs=[pl.BlockSpec((bm, GMLP_W), lambda i, j: (i, 0)), pl.BlockSpec((bm, ATT_W), lambda i, j: (i, 0)),
                  pl.BlockSpec((GMLP_W, bn), lambda i, j: (0, j)), pl.BlockSpec((ATT_W, bn), lambda i, j: (0, j)),
                  pl.BlockSpec((bm, bn), lambda i, j: (i, off_a + j)),
                  pl.BlockSpec((bm, bn), lambda i, j: (i, off_b + j)),
                  pl.BlockSpec((1, bn), lambda i, j: (0, j)), pl.BlockSpec((1, bn), lambda i, j: (0, d // bn + j))],
        out_specs=[tile, tile, tile], out_shape=[_sds((s, d), BF16)] * 3,
        compiler_params=_params(),
    )(sg, attn, wa, wb, proj, proj, b_gate, b_gate)


def _xattn_fwd(name, xb, xf, wq, kv, wo, ln_g, ln_b, after=()):
    s, d = xf.shape
    mem = kv.shape[0]
    bm = _tile(s, 512)
    scale = XHD ** -0.5

    def body(xb_ref, xf_ref, wq_ref, kv_ref, wo_ref, g_ref, b_ref, q_out, o_out, r_out, y_out, yb_out):
        qb = _dot(xb_ref[...], wq_ref[...], NN).astype(BF16)
        kvv = kv_ref[...]
        outs = []
        for h in range(XH):
            hs = slice(h * XHD, (h + 1) * XHD)
            vs = slice(X_W + h * XHD, X_W + (h + 1) * XHD)
            sc = _dot(qb[:, hs], kvv[:, hs], NT) * scale
            mx = jnp.max(sc, axis=-1, keepdims=True)
            p = jnp.exp(sc - mx)
            p = p / jnp.sum(p, axis=-1, keepdims=True)
            outs.append(_dot(p.astype(BF16), kvv[:, vs], NN))
        ob = jnp.concatenate(outs, axis=-1).astype(BF16)
        yv = _dot(ob, wo_ref[...], NN)
        r = ALPHA * xf_ref[...] + yv
        yn, _, _ = _ln_rows(r, g_ref[...], b_ref[...])
        q_out[...] = qb
        o_out[...] = ob
        r_out[...] = r
        y_out[...] = yn
        yb_out[...] = yn.astype(BF16)

    row = lambda wd: pl.BlockSpec((bm, wd), lambda i: (i, 0))
    return _pallas(
        body, after=after, name=name, grid=(s // bm,),
        in_specs=[row(d), row(d), pl.BlockSpec((d, X_W), lambda i: (0, 0)),
                  pl.BlockSpec((mem, 2 * X_W), lambda i: (0, 0)), pl.BlockSpec((X_W, d), lambda i: (0, 0)),
                  pl.BlockSpec((1, d), lambda i: (0, 0)), pl.BlockSpec((1, d), lambda i: (0, 0))],
        out_specs=[row(X_W), row(X_W), row(d), row(d), row(d)],
        out_shape=[_sds((s, X_W), BF16), _sds((s, X_W), BF16), _sds((s, d), F32), _sds((s, d), F32),
                   _sds((s, d), BF16)],
        compiler_params=_params(),
    )(xb, xf, wq, kv, wo, ln_g, ln_b)


def _loss_grad(name, y, tgt):
    s, d = y.shape
    bm = _tile(s, 512)

    def body(y_ref, t_ref, dy_ref, loss_ref):
        i = pl.program_id(0)
        err = y_ref[...] - t_ref[...]
        dy_ref[...] = err * (1.0 / d)
        part = 0.5 * jnp.sum(jnp.sum(err * err, axis=-1, keepdims=True), axis=0, keepdims=True) * (1.0 / d)

        @pl.when(i == 0)
        def _():
            loss_ref[...] = part

        @pl.when(i > 0)
        def _():
            loss_ref[...] += part

    row = pl.BlockSpec((bm, d), lambda i: (i, 0))
    return _pallas(
        body, name=name, grid=(s // bm,), in_specs=[row, row],
        out_specs=[row, pl.BlockSpec((1, 1), lambda i: (0, 0))],
        out_shape=[_sds((s, d), F32), _sds((1, 1), F32)],
        compiler_params=_params(dimension_semantics=("arbitrary",)),
    )(y, tgt)


def _ln_bwd(name, dy, r, g, after=()):
    s, d = r.shape
    bm = _tile(s, 256)

    def body(dy_ref, r_ref, g_ref, dr_ref, drb_ref, dg_ref, db_ref):
        i = pl.program_id(0)
        dyv = dy_ref[...]
        _, xhat, rstd = _ln_rows(r_ref[...], g_ref[...], 0.0)
        dxh = dyv * g_ref[...]
        m1 = jnp.mean(dxh, axis=-1, keepdims=True)
        m2 = jnp.mean(dxh * xhat, axis=-1, keepdims=True)
        dr = rstd * (dxh - m1 - xhat * m2)
        dr_ref[...] = dr
        drb_ref[...] = dr.astype(BF16)
        dg = jnp.sum(dyv * xhat, axis=0, keepdims=True)
        db = jnp.sum(dyv, axis=0, keepdims=True)

        @pl.when(i == 0)
        def _():
            dg_ref[...] = dg
            db_ref[...] = db

        @pl.when(i > 0)
        def _():
            dg_ref[...] += dg
            db_ref[...] += db

    row = pl.BlockSpec((bm, d), lambda i: (i, 0))
    vec = pl.BlockSpec((1, d), lambda i: (0, 0))
    return _pallas(
        body, after=after, name=name, grid=(s // bm,), in_specs=[row, row, vec], out_specs=[row, row, vec, vec],
        out_shape=[_sds((s, d), F32), _sds((s, d), BF16), _sds((1, d), F32), _sds((1, d), F32)],
        compiler_params=_params(dimension_semantics=("arbitrary",)),
    )(dy, r, g)


def _xattn_bwd(name, dyb, drf, q, kv, wo, wq):
    s, d = drf.shape
    mem = kv.shape[0]
    bm = _tile(s, 512)
    scale = XHD ** -0.5

    def body(dy_ref, dr_ref, q_ref, kv_ref, wo_ref, wq_ref, dx_out, dq_out, dkv_out):
        i = pl.program_id(0)
        dob = _dot(dy_ref[...], wo_ref[...], NT).astype(BF16)
        qb = q_ref[...]
        kvv = kv_ref[...]
        dqs, dks, dvs = [], [], []
        for h in range(XH):
            hs = slice(h * XHD, (h + 1) * XHD)
            vs = slice(X_W + h * XHD, X_W + (h + 1) * XHD)
            sc = _dot(qb[:, hs], kvv[:, hs], NT) * scale
            mx = jnp.max(sc, axis=-1, keepdims=True)
            p = jnp.exp(sc - mx)
            p = p / jnp.sum(p, axis=-1, keepdims=True)
            dp = _dot(dob[:, hs], kvv[:, vs], NT)
            dsum = jnp.sum(p * dp, axis=-1, keepdims=True)
            dsb = (p * (dp - dsum) * scale).astype(BF16)
            dqs.append(_dot(dsb, kvv[:, hs], NN))
            dks.append(_dot(dsb, qb[:, hs], TN))
            dvs.append(_dot(p.astype(BF16), dob[:, hs], TN))
        dqb = jnp.concatenate(dqs, axis=-1).astype(BF16)
        dq_out[...] = dqb
        dx_out[...] = _dot(dqb, wq_ref[...], NT) + ALPHA * dr_ref[...]
        dkv = jnp.concatenate(dks + dvs, axis=-1)

        @pl.when(i == 0)
        def _():
            dkv_out[...] = dkv

        @pl.when(i > 0)
        def _():
            dkv_out[...] += dkv

    row = lambda wd: pl.BlockSpec((bm, wd), lambda i: (i, 0))
    return _pallas(
        body, name=name, grid=(s // bm,),
        in_specs=[row(d), row(d), row(X_W), pl.BlockSpec((mem, 2 * X_W), lambda i: (0, 0)),
                  pl.BlockSpec((X_W, d), lambda i: (0, 0)), pl.BlockSpec((d, X_W), lambda i: (0, 0))],
        out_specs=[row(d), row(X_W), pl.BlockSpec((mem, 2 * X_W), lambda i: (0, 0))],
        out_shape=[_sds((s, d), F32), _sds((s, X_W), BF16), _sds((mem, 2 * X_W), F32)],
        compiler_params=_params(dimension_semantics=("arbitrary",)),
    )(dyb, drf, q, kv, wo, wq)


def _gate_bwd(name, dr1b, w_o, proj, ya, yb, b_gate, d, after=()):
    s = dr1b.shape[0]
    bm, bn = _tile(s, 1024), _tile(d, 512)
    off_a, off_b = OFF_GA // bn, (OFF_GA + d) // bn
    nj = d // bn

    def body(a_ref, w_ref, ga_ref, gb_ref, ya_ref, yb_ref, ba_ref, bb_ref, dya_ref, dyb_ref, dg_ref, dba_ref, dbb_ref):
        i = pl.program_id(1)
        dm = _dot(a_ref[...], w_ref[...], NT)
        sa = _sigmoid(ga_ref[...] + ba_ref[...])
        sb = _sigmoid(gb_ref[...] + bb_ref[...])
        dya_ref[...] = (dm * sa).astype(BF16)
        dyb_ref[...] = (dm * sb).astype(BF16)
        dga = dm * ya_ref[...].astype(F32) * (sa * (1.0 - sa))
        dgb = dm * yb_ref[...].astype(F32) * (sb * (1.0 - sb))
        dg_ref[0] = dga.astype(BF16)
        dg_ref[1] = dgb.astype(BF16)
        sa_sum = jnp.sum(dga, axis=0, keepdims=True)
        sb_sum = jnp.sum(dgb, axis=0, keepdims=True)

        @pl.when(i == 0)
        def _():
            dba_ref[...] = sa_sum
            dbb_ref[...] = sb_sum

        @pl.when(i > 0)
        def _():
            dba_ref[...] += sa_sum
            dbb_ref[...] += sb_sum

    tile = pl.BlockSpec((bm, bn), lambda j, i: (i, j))
    return _pallas(
        body, after=after, name=name, grid=(nj, s // bm),
        in_specs=[pl.BlockSpec((bm, d), lambda j, i: (i, 0)),
                  pl.BlockSpec((bn, d), lambda j, i: (j, 0)),
                  pl.BlockSpec((bm, bn), lambda j, i: (i, off_a + j)),
                  pl.BlockSpec((bm, bn), lambda j, i: (i, off_b + j)),
                  tile, tile,
                  pl.BlockSpec((1, bn), lambda j, i: (0, j)), pl.BlockSpec((1, bn), lambda j, i: (0, nj + j))],
        out_specs=[tile, tile, pl.BlockSpec((2, bm, bn), lambda j, i: (0, i, j)),
                   pl.BlockSpec((1, bn), lambda j, i: (0, j)), pl.BlockSpec((1, bn), lambda j, i: (0, j))],
        out_shape=[_sds((s, d), BF16), _sds((s, d), BF16), _sds((2, s, d), BF16), _sds((1, d), F32),
                   _sds((1, d), F32)],
        compiler_params=_params(dimension_semantics=("arbitrary", "arbitrary")),
    )(dr1b, w_o, proj, proj, ya, yb, b_gate, b_gate)


def _gmlp_bwd(name, proj, dsg, ln_g, ln_b, w_s, b_st):
    s = proj.shape[0]

    def body(u_ref, v_ref, dsg_ref, g_ref, b_ref, ws_ref, bst_ref, duv_ref, dws_ref, dbst_ref, dlg_ref, dlb_ref):
        n = pl.program_id(0)
        u, v = u_ref[...], v_ref[...]
        gu, tu = _gelu(u)
        gv, tv = _gelu(v)
        gam = g_ref[...]
        vn, xhat, rstd = _ln_rows(gv, gam, b_ref[...])
        vnb = vn.astype(BF16)
        dsg = dsg_ref[...].astype(F32)
        row = lax.broadcasted_iota(jnp.int32, (CHUNK, CHUNK), 0)
        col = lax.broadcasted_iota(jnp.int32, (CHUNK, CHUNK), 1)
        tril = col <= row
        dgu, dvn, dws, dbs = [], [], [], []
        for g in range(GROUPS):
            sl = slice(g * LANE, (g + 1) * LANE)
            w = jnp.where(tril, ws_ref[g], 0.0).astype(BF16)
            mixed = _dot(w, vnb[:, sl], NN) + bst_ref[:, g:g + 1]
            dgu.append(dsg[:, sl] * mixed)
            dmx = dsg[:, sl] * gu[:, sl]
            dmxb = dmx.astype(BF16)
            dbs.append(jnp.sum(dmx, axis=-1, keepdims=True))
            dws.append(jnp.where(tril, _dot(dmxb, vnb[:, sl], NT), 0.0))
            dvn.append(_dot(w, dmxb, TN))
        dvn = jnp.concatenate(dvn, axis=-1)
        dgu = jnp.concatenate(dgu, axis=-1)
        dxh = dvn * gam
        m1 = jnp.mean(dxh, axis=-1, keepdims=True)
        m2 = jnp.mean(dxh * xhat, axis=-1, keepdims=True)
        dgv = rstd * (dxh - m1 - xhat * m2)
        du = dgu * _gelu_grad(u, tu)
        dv = dgv * _gelu_grad(v, tv)
        duv_ref[...] = jnp.concatenate([du, dv], axis=-1).astype(BF16)
        dlg = jnp.sum(dvn * xhat, axis=0, keepdims=True)
        dlb = jnp.sum(dvn, axis=0, keepdims=True)
        dbst = jnp.concatenate(dbs, axis=-1)

        @pl.when(n == 0)
        def _():
            for g in range(GROUPS):
                dws_ref[g] = dws[g]
            dbst_ref[...] = dbst
            dlg_ref[...] = dlg
            dlb_ref[...] = dlb

        @pl.when(n > 0)
        def _():
            for g in range(GROUPS):
                dws_ref[g] += dws[g]
            dbst_ref[...] += dbst
            dlg_ref[...] += dlg
            dlb_ref[...] += dlb

    vec = pl.BlockSpec((1, GMLP_W), lambda n: (0, 0))
    return _pallas(
        body, name=name, grid=(s // CHUNK,),
        in_specs=[pl.BlockSpec((CHUNK, GMLP_W), lambda n: (n, 0)), pl.BlockSpec((CHUNK, GMLP_W), lambda n: (n, 1)),
                  pl.BlockSpec((CHUNK, GMLP_W), lambda n: (n, 0)), vec, vec,
                  pl.BlockSpec((GROUPS, CHUNK, CHUNK), lambda n: (0, 0, 0)),
                  pl.BlockSpec((CHUNK, GROUPS), lambda n: (0, 0))],
        out_specs=[pl.BlockSpec((CHUNK, 2 * GMLP_W), lambda n: (n, 0)),
                   pl.BlockSpec((GROUPS, CHUNK, CHUNK), lambda n: (0, 0, 0)),
                   pl.BlockSpec((CHUNK, GROUPS), lambda n: (0, 0)), vec, vec],
        out_shape=[_sds((s, 2 * GMLP_W), BF16), _sds((GROUPS, CHUNK, CHUNK), F32), _sds((CHUNK, GROUPS), F32),
                   _sds((1, GMLP_W), F32), _sds((1, GMLP_W), F32)],
        compiler_params=_params(dimension_semantics=("arbitrary",)),
    )(proj, proj, dsg, ln_g, ln_b, w_s, b_st)


def _swa_bwd(name, qr, kr, proj, do, o, lse, sinks, cos4, nsin4, after=()):
    s = qr.shape[0]
    w = CHUNK
    nblk = s // w
    scale = HD ** -0.5
    grp = NQ // NKV

    def body(qj_ref, qn_ref, kj_ref, kp_ref, vj_ref, vp_ref, doj_ref, don_ref, oj_ref, on_ref, lj_ref, ln_ref,
             sink_ref, cos_ref, sin_ref, out_ref, dsink_ref):
        j = pl.program_id(0)
        qj, qn = qj_ref[...].astype(F32), qn_ref[...].astype(F32)
        doj, don = doj_ref[...].astype(F32), don_ref[...].astype(F32)
        kk = jnp.concatenate([kp_ref[...], kj_ref[...]], axis=0)
        vv = jnp.concatenate([vp_ref[...], vj_ref[...]], axis=0).astype(BF16)
        lj, lnx = lj_ref[...], ln_ref[...]
        prod_j = doj * oj_ref[...].astype(F32)
        prod_n = don * on_ref[...].astype(F32)
        valid_j = _band_mask(j > 0)
        valid_n = _band_mask(j + 1 < nblk, prev_only=True)
        lo = lax.broadcasted_iota(jnp.int32, (w, LANE), 1) < HD
        dqs, dsk, dk_g, dv_g = [], [], [], []
        for g in range(NKV):
            kz, vz = _own_head(kk, g), _own_head(vv, g)
            kz_c, vz_c = kz[w:], vz[w:]
            qg_j, qg_n = _stack_heads(qj, g), _stack_heads(qn, g)
            dog_j, dog_n = _stack_heads(doj, g), _stack_heads(don, g)
            l_j, l_n = _stack_rows(lj, g), _stack_rows(lnx, g)
            d_j = _head_lane_sums(_stack_heads(prod_j, g, F32), g)
            d_n = _head_lane_sums(_stack_heads(prod_n, g, F32), g)
            p = jnp.where(valid_j, jnp.exp(_dot(kz, qg_j, NT) * scale - l_j), 0.0)
            ds = (p * (_dot(vz, dog_j, NT) - d_j) * scale).astype(BF16)
            dqs.extend(_unstack_heads(_dot(ds, kz, TN), g))
            p2 = jnp.where(valid_n, jnp.exp(_dot(kz_c, qg_n, NT) * scale - l_n), 0.0)
            ds2 = (p2 * (_dot(vz_c, dog_n, NT) - d_n) * scale).astype(BF16)
            dk_g.append(_dot(ds[w:], qg_j, NN) + _dot(ds2, qg_n, NN))
            dv_g.append(_dot(p[w:].astype(BF16), dog_j, NN) + _dot(p2.astype(BF16), dog_n, NN))
            t = jnp.exp(sink_ref[g] - l_j) * d_j
            dsk.extend([-jnp.sum(t[:, h * w:(h + 1) * w], axis=-1, keepdims=True) for h in range(GRP)])
        cos, nsin = cos_ref[...], sin_ref[...]
        dq = _rope(jnp.concatenate(dqs, axis=-1), cos, nsin)
        dk = _rope(jnp.concatenate([jnp.where(lo, dk_g[2 * m], dk_g[2 * m + 1]) for m in range(NKV // 2)], axis=-1),
                   cos, nsin)
        dv = jnp.concatenate([jnp.where(lo, dv_g[2 * m], dv_g[2 * m + 1]) for m in range(NKV // 2)], axis=-1)
        out_ref[...] = jnp.concatenate([dq, dk, dv], axis=-1).astype(BF16)
        dsink = jnp.concatenate(dsk, axis=-1)

        @pl.when(j == 0)
        def _():
            dsink_ref[...] = dsink

        @pl.when(j > 0)
        def _():
            dsink_ref[...] += dsink

    nxt = lambda j: jnp.minimum(j + 1, nblk - 1)
    prv = lambda j: jnp.maximum(j - 1, 0)
    va = OFF_VA // KV_W
    return _pallas(
        body, after=after, name=name, grid=(nblk,),
        in_specs=[pl.BlockSpec((w, ATT_W), lambda j: (j, 0)), pl.BlockSpec((w, ATT_W), lambda j: (nxt(j), 0)),
                  pl.BlockSpec((w, KV_W), lambda j: (j, 0)), pl.BlockSpec((w, KV_W), lambda j: (prv(j), 0)),
                  pl.BlockSpec((w, KV_W), lambda j: (j, va)), pl.BlockSpec((w, KV_W), lambda j: (prv(j), va)),
                  pl.BlockSpec((w, ATT_W), lambda j: (j, 0)), pl.BlockSpec((w, ATT_W), lambda j: (nxt(j), 0)),
                  pl.BlockSpec((w, ATT_W), lambda j: (j, 0)), pl.BlockSpec((w, ATT_W), lambda j: (nxt(j), 0)),
                  pl.BlockSpec((None, NQ, w), lambda j: (j, 0, 0)),
                  pl.BlockSpec((None, NQ, w), lambda j: (nxt(j), 0, 0)),
                  pl.BlockSpec((NKV, 1, GRP * w), lambda j: (0, 0, 0)),
                  pl.BlockSpec((w, LANE), lambda j: (j, 0)), pl.BlockSpec((w, LANE), lambda j: (j, 0))],
        out_specs=[pl.BlockSpec((w, ATT_W + 2 * KV_W), lambda j: (j, 0)), pl.BlockSpec((1, NQ), lambda j: (0, 0))],
        out_shape=[_sds((s, ATT_W + 2 * KV_W), BF16), _sds((1, NQ), F32)],
        compiler_params=_params(dimension_semantics=("arbitrary",)),
    )(qr, qr, kr, kr, proj, proj, do, do, o, o, lse, lse, sinks, cos4, nsin4)


def _mm_nn(name, a, w, *, out_dtypes, epilogue=_store, bm_pref=1024, bn_pref=1024, after=()):
    m, k = a.shape
    n = w.shape[-1]
    bm, bn = _tile(m, bm_pref), _tile(n, bn_pref)
    tile = pl.BlockSpec((bm, bn), lambda i, j, kk: (i, j))
    return _mm(name, a, w, dims=NN, grid=(m // bm, n // bn, 1),
               a_spec=pl.BlockSpec((bm, k), lambda i, j, kk: (i, 0)),
               b_spec=pl.BlockSpec((k, bn), lambda i, j, kk: (0, j)),
               out_shape=[_sds((m, n), dt) for dt in out_dtypes], out_specs=[tile] * len(out_dtypes),
               epilogue=epilogue, after=after)


def _mm_tn(name, a, b, *, bm_pref=1024, bn_pref=1024, after=()):
    s, m = a.shape
    n = b.shape[-1]
    bm, bn = _tile(m, bm_pref), _tile(n, bn_pref)
    return _mm(name, a, b, dims=TN, grid=(m // bm, n // bn, 1),
               a_spec=pl.BlockSpec((s, bm), lambda i, j, kk: (0, i)),
               b_spec=pl.BlockSpec((s, bn), lambda i, j, kk: (0, j)),
               out_shape=[_sds((m, n), BF16)], out_specs=[pl.BlockSpec((bm, bn), lambda i, j, kk: (i, j))],
               epilogue=_store, after=after)[0]


def _mm_residual(name, a, w, x, after=()):
    s, k = a.shape
    d = w.shape[-1]
    bm, bn = _tile(s, 1024), _tile(d, 1024 if k <= 2048 else 512)
    tile = pl.BlockSpec((bm, bn), lambda i, j, kk: (i, j))
    return _mm(name, a, w, dims=NN, grid=(s // bm, d // bn, 1),
               a_spec=pl.BlockSpec((bm, k), lambda i, j, kk: (i, 0)),
               b_spec=pl.BlockSpec((k, bn), lambda i, j, kk: (0, j)),
               extras=(x,), extra_specs=(tile,), out_shape=[_sds((s, d), F32)], out_specs=[tile],
               epilogue=_ep_add_scaled, after=after)[0]


def _dw_pieces(name, a, pieces, widths, after=()):
    s, m = a.shape
    total = sum(widths)
    bm, bn = _tile(m, 1024), 512
    out, off = None, 0
    for p, (piece, wd) in enumerate(zip(pieces, widths)):
        if isinstance(piece, tuple):
            arr = piece[0]
            b_spec = pl.BlockSpec((None, s, bn), (lambda ix: lambda i, j: (ix, 0, j))(piece[1]))
        else:
            arr, b_spec = piece, pl.BlockSpec((s, bn), lambda i, j: (0, j))
        prev = () if out is None else (out,)

        def body(a_ref, b_ref, *rest):
            rest[-1][...] = _dot(a_ref[...], b_ref[...], TN).astype(BF16)

        out = _pallas(
            body, after=after if out is None else (), name=f"{name}_{p}", grid=(m // bm, wd // bn),
            in_specs=[pl.BlockSpec((s, bm), lambda i, j: (0, i)), b_spec] + [ANY] * len(prev),
            out_specs=pl.BlockSpec((bm, bn), (lambda c: lambda i, j: (i, c + j))(off // bn)),
            out_shape=_sds((m, total), BF16), input_output_aliases={2: 0} if prev else {},
            compiler_params=_params(dimension_semantics=("arbitrary", "arbitrary")),
        )(a, arr, *prev)
        off += wd
    return out


def _dx_pieces(name, pieces, widths, w, dr, after=()):
    s, d = dr.shape
    iw = w.shape[-1]
    bm, bn = _tile(s, 1024), _tile(d, 512)
    arrs, specs = [], []
    for piece, wd in zip(pieces, widths):
        if isinstance(piece, tuple):
            arrs.append(piece[0])
            specs.append(pl.BlockSpec((None, bm, wd), (lambda idx: lambda i, j: (idx, i, 0))(piece[1])))
        else:
            arrs.append(piece)
            specs.append(pl.BlockSpec((bm, wd), lambda i, j: (i, 0)))
    n = len(arrs)

    def body(*refs):
        w_ref, dr_ref, o_ref = refs[n], refs[n + 1], refs[n + 2]
        acc = ALPHA * dr_ref[...]
        off = 0
        for p, wd in enumerate(widths):
            acc = acc + _dot(refs[p][...], w_ref[:, off:off + wd], NT)
            off += wd
        o_ref[...] = acc

    tile = pl.BlockSpec((bm, bn), lambda i, j: (i, j))
    return _pallas(
        body, after=after, name=name, grid=(s // bm, d // bn),
        in_specs=specs + [pl.BlockSpec((bn, iw), lambda i, j: (j, 0)), tile], out_specs=tile,
        out_shape=_sds((s, d), F32), compiler_params=_params(dimension_semantics=("arbitrary", "arbitrary")),
    )(*arrs, w, dr)


class _Gather:
    def __init__(self, tag, names, fulls, after):
        self.tag, self.names = tag, names
        self.axes = [SHARD_AXIS[n] for n in names]
        self.srcs = []
        self.mk1 = _mk_gather_ici(self.axes)
        self.mk2 = _mk_gather_d2d(self.axes)
        self.n_sem = 3 * len(names)
        self.s1, self.r1, self.lands, self.token = _split_start(
            tag + "_ici_start", self.srcs, [fulls[n] for n in names], self.mk1, self.n_sem, after)

    def forward(self, after=()):
        lands = _split_wait(self.tag + "_ici_wait", self.srcs, self.lands, self.s1, self.r1, self.mk1, after)
        self.s2, self.r2, self.lands, tok = _split_start(self.tag + "_d2d_start", [], lands, self.mk2, self.n_sem)
        return tok

    def done(self, after=()):
        lands = _split_wait(self.tag + "_d2d_wait", [], self.lands, self.s2, self.r2, self.mk2, after)
        return dict(zip(self.names, lands))


class _Reduce:
    def __init__(self, tag, names, parts, cidx, mcidx, after=()):
        self.tag, self.names, self.cidx, self.mcidx = tag, names, cidx, mcidx
        self.axes = [SHARD_AXIS[n] for n in names]
        self.parts = [parts[n] for n in names]
        self.mk = _mk_swap(self.axes)
        lands = []
        for p, ax in zip(self.parts, self.axes):
            k, n = p.shape
            lands.append(lax.empty((k, n // 2) if ax == 0 else (k // 2, n), BF16))
        self.s, self.r, self.lands, self.token = _split_start(
            tag + "_swap_start", self.parts, lands, self.mk, len(names), after)

    def scatter(self, after=()):
        got = _split_wait(self.tag + "_swap_wait", self.parts, self.lands, self.s, self.r, self.mk, after)
        self.sums = [_add_half(f"{self.tag}_add_{n}", p, g, ax, self.cidx)
                     for n, p, g, ax in zip(self.names, self.parts, got, self.axes)]
        self.mk = _mk_scatter(self.axes)
        lands = []
        for q, ax in zip(self.sums, self.axes):
            k, n = q.shape
            lands.append(lax.empty((3, k // 4, n) if ax == 0 else (3, k, n // 4), BF16))
        self.s, self.r, self.lands, tok = _split_start(
            self.tag + "_scatter_start", self.sums, lands, self.mk, 3 * len(self.names))
        return tok

    def exchange(self, after=()):
        slots = _split_wait(self.tag + "_scatter_wait", self.sums, self.lands, self.s, self.r, self.mk, after)
        halves = [_sum_half(f"{self.tag}_sum_{n}", q, sl, ax, self.mcidx)
                  for n, q, sl, ax in zip(self.names, self.sums, slots, self.axes)]
        self.mk = _mk_exchange(self.axes)
        self.s, self.r, self.lands, tok = _split_start(
            self.tag + "_exchange_start", [], halves, self.mk, len(self.names))
        return tok

    def done(self, after=()):
        grads = _split_wait(self.tag + "_exchange_wait", [], self.lands, self.s, self.r, self.mk, after)
        return dict(zip(self.names, grads))


def _pack(arrs):
    flat = jnp.concatenate([a.reshape(-1) for a in arrs])
    n = flat.shape[0]
    pad = (-n) % (8 * LANE)
    return jnp.pad(flat, (0, pad)).reshape(-1, LANE)


def _unpack(packed, shapes):
    flat = packed.reshape(-1)
    out, off = [], 0
    for sh in shapes:
        n = math.prod(sh)
        out.append(flat[off:off + n].reshape(sh))
        off += n
    return out


def kernel(x, mem, w_in, b_gate, ln_v_g, ln_v_b, w_s, b_s, sinks, w_br_a, w_br_b, w_o, ln1_g, ln1_b, w_xq, w_xkv, w_xo, ln2_g, ln2_b, w_up, w_down, ln3_g, ln3_b, loss_target, m_w_in, m_b_gate, m_ln_v_g, m_ln_v_b, m_w_s, m_b_s, m_sinks, m_w_br_a, m_w_br_b, m_w_o, m_ln1_g, m_ln1_b, m_w_xq, m_w_xkv, m_w_xo, m_ln2_g, m_ln2_b, m_w_up, m_w_down, m_ln3_g, m_ln3_b, v_w_in, v_b_gate, v_ln_v_g, v_ln_v_b, v_w_s, v_b_s, v_sinks, v_w_br_a, v_w_br_b, v_w_o, v_ln1_g, v_ln1_b, v_w_xq, v_w_xkv, v_w_xo, v_ln2_g, v_ln2_b, v_w_up, v_w_down, v_ln3_g, v_ln3_b):
    env = dict(locals())
    wts = {n: env[n] for n in WEIGHTS}
    mom_m = {n: env["m_" + n] for n in WEIGHTS}
    mom_v = {n: env["v_" + n] for n in WEIGHTS}
    s, d = x.shape[1], x.shape[2]
    dff = 4 * w_up.shape[-1]
    xf = x.reshape(s, d)
    tgt = loss_target.reshape(s, d)
    memf = mem.reshape(mem.shape[1], d)
    ax_x, ax_y, ax_c = lax.axis_index("x"), lax.axis_index("y"), lax.axis_index("c")
    meidx = jnp.reshape(2 * ax_x + ax_y, (1,)).astype(jnp.int32)
    cidx = jnp.reshape(ax_c, (1,)).astype(jnp.int32)
    mcidx = jnp.concatenate([meidx, cidx])

    inv = 1.0 / (10000.0 ** (jnp.arange(0, HD, 2, dtype=F32) / HD))
    ang = jnp.arange(s, dtype=F32)[:, None] * inv[None, :]
    cos, sin = jnp.cos(ang), jnp.sin(ang)
    cos4 = jnp.tile(cos, (1, 4))
    sin4 = jnp.concatenate([-sin, sin, -sin, sin], axis=-1)
    nsin4 = -sin4

    small = {}
    for n in SMALL:
        w = wts[n]
        if n == "w_s":
            small[n] = [w[l] for l in range(DEPTH)]
        elif n == "b_s":
            small["b_st"] = [w[l].T for l in range(DEPTH)]
        else:
            small[n] = [w[l][None, :] for l in range(DEPTH)]
    small["sink_rows"] = [jnp.repeat(sinks[l].reshape(NKV, GRP), CHUNK, axis=1)[:, None, :] for l in range(DEPTH)]

    fulls = [{}, {}]
    tok = ()
    gathers = [[None] * len(GROUPS_GATHER) for _ in range(DEPTH)]
    for gi, names in enumerate(GROUPS_GATHER):
        for n in names:
            fulls[0][n], fulls[1][n] = _place_own("place_" + n, wts[n], SHARD_AXIS[n], meidx, after=tok)
        gathers[0][gi] = _Gather(f"ag0_{gi}", names, fulls[0], tok)
        tok = (gathers[0][gi].token,)
    for gi, names in enumerate(GROUPS_GATHER):
        gathers[1][gi] = _Gather(f"ag1_{gi}", names, fulls[1], tok)
        tok = (gathers[1][gi].token,)

    xb = _cast2d("cast_x", xf, after=tok)
    memb = _cast2d("cast_mem", memf, after=tok)

    saved = []
    hf, hb = xf, xb
    nxt_tok = gathers[0][0].forward(after=tok)
    for l in range(DEPTH):
        t = f"l{l}_"
        ga, gb, gc, gd = gathers[l]
        full = ga.done(after=(nxt_tok, hb))
        sv = {"xf": hf, "xb": hb}
        proj = _mm_nn(t + "proj", hb, full["w_in"], out_dtypes=[F32], bn_pref=1280)[0]
        tok_b = gb.forward(after=(proj,))
        sg = _gmlp_fwd(t + "gmlp_fwd", proj, small["ln_v_g"][l], small["ln_v_b"][l], small["w_s"][l],
                       small["b_st"][l])
        attn, qr, kr, lse = _swa_fwd(t + "swa_fwd", proj, cos4, sin4, small["sink_rows"][l], after=(tok_b,))
        full.update(gb.done(after=(attn,)))
        merged, ya, yb = _gate_fwd(t + "gate_fwd", sg, attn, full["w_br_a"], full["w_br_b"], proj,
                                   small["b_gate"][l], d)
        tok_c = gc.forward(after=(merged,))
        r1 = _mm_residual(t + "o", merged, full["w_o"], hf, after=(tok_c,))
        x1, x1b = _ln_fwd(t + "ln1", r1, small["ln1_g"][l], small["ln1_b"][l])
        kv = _mm_nn(t + "xkv", memb, full["w_xkv"], out_dtypes=[BF16])[0]
        q, o, r2, x2, x2b = _xattn_fwd(t + "xattn_fwd", x1b, x1, full["w_xq"], kv, full["w_xo"],
                                       small["ln2_g"][l], small["ln2_b"][l])
        full.update(gc.done(after=(x2b,)))
        tok_d = gd.forward(after=(x2b,))

        def ep_up(acc, ex, outs):
            outs[0][...] = acc.astype(BF16)
            rl = jnp.maximum(acc, 0.0)
            outs[1][...] = (rl * rl).astype(BF16)

        h, a = _mm_nn(t + "up", x2b, full["w_up"], out_dtypes=[BF16, BF16], epilogue=ep_up, after=(tok_d,))
        full.update(gd.done(after=(h,)))
        nxt_tok = gathers[l + 1][0].forward(after=(h,)) if l + 1 < DEPTH else None
        r3 = _mm_residual(t + "down", a, full["w_down"], x2, after=() if nxt_tok is None else (nxt_tok,))
        x3, x3b = _ln_fwd(t + "ln3", r3, small["ln3_g"][l], small["ln3_b"][l])
        sv.update(proj=proj, sg=sg, attn=attn, qr=qr, kr=kr, lse=lse, merged=merged, ya=ya, yb=yb, r1=r1, x1=x1,
                  x1b=x1b, kv=kv, q=q, o=o, r2=r2, x2b=x2b, h=h, a=a, r3=r3, full=full)
        saved.append(sv)
        hf, hb = x3, x3b
    dy, loss11 = _loss_grad("loss", hf, tgt)
    loss = lax.psum(loss11[0, 0], ("x", "y", "c"))

    small_g = [None] * DEPTH
    grads = [{}, {}]
    pend_a = None
    pend_b = None
    g = dy
    for l in reversed(range(DEPTH)):
        t = f"l{l}_"
        sv = saved[l]
        full = sv["full"]
        dw, sgo = {}, {}
        dr3, dr3b, sgo["ln3_g"], sgo["ln3_b"] = _ln_bwd(t + "ln3_bwd", g, sv["r3"], small["ln3_g"][l],
                                                        after=() if pend_a is None else (tok_a,))
        bm, bn = _tile(s, 1024), _tile(dff, 1024)

        def ep_dh(acc, ex, outs):
            outs[0][...] = (acc * (2.0 * jnp.maximum(ex[0][...].astype(F32), 0.0))).astype(BF16)

        tile = pl.BlockSpec((bm, bn), lambda i, j, k: (i, j))
        dh = _mm(t + "dh", dr3b, full["w_down"], dims=NT, grid=(s // bm, dff // bn, 1),
                 a_spec=pl.BlockSpec((bm, d), lambda i, j, k: (i, 0)),
                 b_spec=pl.BlockSpec((bn, d), lambda i, j, k: (j, 0)),
                 extras=(sv["h"],), extra_specs=(tile,), out_shape=[_sds((s, dff), BF16)], out_specs=[tile],
                 epilogue=ep_dh)[0]
        if pend_a is not None:
            tok_pa = pend_a.exchange(after=(dh,))
            grads[l + 1].update(pend_b.done(after=(dh,)))
        dw["w_down"] = _mm_tn(t + "dw_down", sv["a"], dr3b, after=() if pend_a is None else (tok_pa,))
        dw["w_up"] = _mm_tn(t + "dw_up", sv["x2b"], dh)
        red_c = _Reduce(t + "rs_c", GROUPS_FWD[2], dw, cidx, mcidx)
        bm2, bn2 = _tile(s, 1024), _tile(d, 512)
        tile2 = pl.BlockSpec((bm2, bn2), lambda i, j, k: (i, j))
        dx2 = _mm(t + "dx2", dh, full["w_up"], dims=NT, grid=(s // bm2, d // bn2, 1),
                  a_spec=pl.BlockSpec((bm2, dff), lambda i, j, k: (i, 0)),
                  b_spec=pl.BlockSpec((bn2, dff), lambda i, j, k: (j, 0)),
                  extras=(dr3,), extra_specs=(tile2,), out_shape=[_sds((s, d), F32)], out_specs=[tile2],
                  epilogue=_ep_add_scaled, after=(red_c.token,))[0]
        tok_c = red_c.scatter(after=(dx2,))
        if pend_a is not None:
            grads[l + 1].update(pend_a.done(after=(dx2,)))
            pend_a = None

        dr2, dr2b, sgo["ln2_g"], sgo["ln2_b"] = _ln_bwd(t + "ln2_bwd", dx2, sv["r2"], small["ln2_g"][l],
                                                        after=(tok_c,))
        dx1, dq, dkv = _xattn_bwd(t + "xattn_bwd", dr2b, dr2, sv["q"], sv["kv"], full["w_xo"], full["w_xq"])
        dw["w_xo"] = _mm_tn(t + "dw_xo", sv["o"], dr2b)
        dw["w_xq"] = _mm_tn(t + "dw_xq", sv["x1b"], dq)
        dw["w_xkv"] = _mm_tn(t + "dw_xkv", memb, _cast2d(t + "dkv_cast", dkv))

        dr1, dr1b, sgo["ln1_g"], sgo["ln1_b"] = _ln_bwd(t + "ln1_bwd", dx1, sv["r1"], small["ln1_g"][l])
        dya, dyb, dgate, dba, dbb = _gate_bwd(t + "gate_bwd", dr1b, full["w_o"], sv["proj"], sv["ya"], sv["yb"],
                                              small["b_gate"][l], d)
        sgo["b_gate"] = jnp.concatenate([dba, dbb], axis=-1)
        dw["w_o"] = _mm_tn(t + "dw_o", sv["merged"], dr1b)
        dw["w_br_a"] = _mm_tn(t + "dw_br_a", sv["sg"], dya)
        dw["w_br_b"] = _mm_tn(t + "dw_br_b", sv["attn"], dyb)
        red_b = _Reduce(t + "rs_b", GROUPS_FWD[1], dw, cidx, mcidx)

        def dbranch(name, dyx, w, after):
            return _mm(name, dyx, w, dims=NT, grid=(s // bm, 1, 1),
                       a_spec=pl.BlockSpec((bm, d), lambda i, j, k: (i, 0)),
                       b_spec=pl.BlockSpec((w.shape[0], d), lambda i, j, k: (0, 0)),
                       out_shape=[_sds((s, w.shape[0]), BF16)],
                       out_specs=[pl.BlockSpec((bm, w.shape[0]), lambda i, j, k: (i, 0))],
                       epilogue=_store, after=after)[0]

        dsg = dbranch(t + "dsg", dya, full["w_br_a"], (red_b.token,))
        dattn = dbranch(t + "dattn", dyb, full["w_br_b"], ())
        tok_c = red_c.exchange(after=(dattn, dsg))
        tok_b = red_b.scatter(after=(dattn, dsg))
        duv, sgo["w_s"], dbst, dlg, dlb = _gmlp_bwd(t + "gmlp_bwd", sv["proj"], dsg, small["ln_v_g"][l],
                                                    small["ln_v_b"][l], small["w_s"][l], small["b_st"][l])
        sgo["b_s"] = dbst.T
        sgo["ln_v_g"], sgo["ln_v_b"] = dlg, dlb
        dqkv, sgo["sinks"] = _swa_bwd(t + "swa_bwd", sv["qr"], sv["kr"], sv["proj"], dattn, sv["attn"], sv["lse"],
                                      small["sink_rows"][l], cos4, nsin4, after=(tok_b, tok_c))
        grads[l].update(red_c.done(after=(dqkv,)))
        pieces = (duv, dqkv, (dgate, 0), (dgate, 1))
        widths = (2 * GMLP_W, ATT_W + 2 * KV_W, d, d)
        tok_b = red_b.exchange(after=(dqkv, duv))
        dw["w_in"] = _dw_pieces(t + "dw_in", sv["xb"], pieces, widths, after=(tok_b,))
        red_a = _Reduce(t + "rs_a", GROUPS_FWD[0], dw, cidx, mcidx)
        g = _dx_pieces(t + "dx0", pieces, widths, full["w_in"], dr1, after=(red_a.token,))
        tok_a = red_a.scatter(after=(g,))
        pend_a, pend_b = red_a, red_b
        small_g[l] = sgo
    grad_x = g.reshape(x.shape)

    big_out = {}

    def adam_layer(l, names, after):
        done = []
        for n in names:
            prev = big_out.get(n)
            big_out[n] = _adamw(f"adamw{l}_{n}", wts[n], grads[l][n], mom_m[n], mom_v[n], l, prev, after=after)
            done.append(big_out[n][0])
        return done

    shapes = [wts[n].shape for n in SMALL]
    packed_g = _pack([jnp.stack([small_g[l][n].reshape(wts[n].shape[1:]) for l in range(DEPTH)]) for n in SMALL])
    me8 = jnp.reshape(4 * ax_x + 2 * ax_y + ax_c, (1,)).astype(jnp.int32)
    ar_s, ar_r, ar_land, tok_ar = _split_start("ar_start", [packed_g], [_place_slot("ar_place", packed_g, me8)],
                                               _mk_small, 7, after=(tok_a,))
    fill = []
    for names in GROUPS_FWD:
        fill += adam_layer(1, names, (tok_ar,))
    grads[0].update(pend_b.done(after=tuple(fill)))
    fill += adam_layer(0, GROUPS_FWD[1], (tok_ar,))
    ar_land = _split_wait("ar_wait", [packed_g], ar_land, ar_s, ar_r, _mk_small, after=tuple(fill))
    packed_g = _sum_slots("ar_sum", ar_land[0])
    pw, pm, pv = (_pack([src[n] for n in SMALL]) for src in (wts, mom_m, mom_v))
    small4 = _adamw("adamw_small", pw[None], packed_g, pm[None], pv[None], 0)
    small_out = [dict(zip(SMALL, _unpack(a[0], shapes))) for a in small4]
    tok_a = pend_a.exchange(after=(small4[0],))
    fill = adam_layer(0, GROUPS_FWD[2], (tok_a,))
    grads[0].update(pend_a.done(after=tuple(fill)))
    adam_layer(0, GROUPS_FWD[0], ())

    def pick(kind, n):
        return big_out[n][kind] if n in big_out else small_out[kind][n]

    return (loss, grad_x, *[pick(0, n) for n in WEIGHTS], *[pick(1, n) for n in WEIGHTS],
            *[pick(2, n) for n in WEIGHTS], *[pick(3, n) for n in WEIGHTS])
```

```python
import math

import jax
import jax.numpy as jnp
from jax import lax
from jax.experimental import pallas as pl
from jax.experimental.pallas import tpu as pltpu

F32 = jnp.float32
BF16 = jnp.bfloat16
MESH = pl.DeviceIdType.MESH
ANY = pl.BlockSpec(memory_space=pl.ANY)
HBM = pl.BlockSpec(memory_space=pltpu.HBM)
SEM = pl.BlockSpec(memory_space=pltpu.SEMAPHORE)
VMEM_SPEC = pl.BlockSpec(memory_space=pltpu.VMEM)
EFFECT = pltpu.SideEffectType.DATAFLOW_SIDE_EFFECTING

DEPTH = 2
CHUNK = 128
GMLP_W = 1024
GROUPS = 8
NQ, NKV, HD = 16, 4, 64
ATT_W = NQ * HD
KV_W = NKV * HD
XH, XHD = 4, 128
X_W = XH * XHD
LN_EPS = 1e-5
ALPHA = (2 * DEPTH) ** 0.25
OFF_Q = 2 * GMLP_W
OFF_K = OFF_Q + ATT_W
OFF_VA = OFF_K + KV_W
OFF_GA = OFF_VA + KV_W
NEG = -1e30

ADAM_LR, ADAM_B1, ADAM_B2, ADAM_EPS, ADAM_WD, ADAM_STEP = 0.001, 0.9, 0.999, 1e-08, 0.01, 10

V7X_VMEM_BYTES = 64 * 1024 * 1024
VMEM_LIMIT = V7X_VMEM_BYTES - 4 * 1024 * 1024
LANE = 128

BIG = ("w_in", "w_br_a", "w_br_b", "w_o", "w_xq", "w_xkv", "w_xo", "w_up", "w_down")
SHARD_AXIS = {"w_in": 1, "w_br_a": 1, "w_br_b": 1, "w_o": 0, "w_xq": 0, "w_xkv": 0, "w_xo": 1,
              "w_up": 1, "w_down": 0}
GROUPS_GATHER = (("w_in",), ("w_br_a", "w_br_b", "w_o", "w_xq", "w_xkv", "w_xo"), ("w_up",), ("w_down",))
GROUPS_FWD = (("w_in",), ("w_br_a", "w_br_b", "w_o", "w_xq", "w_xkv", "w_xo"), ("w_up", "w_down"))
SMALL = ("b_gate", "ln_v_g", "ln_v_b", "w_s", "b_s", "sinks", "ln1_g", "ln1_b", "ln2_g", "ln2_b",
         "ln3_g", "ln3_b")
WEIGHTS = ("w_in", "b_gate", "ln_v_g", "ln_v_b", "w_s", "b_s", "sinks", "w_br_a", "w_br_b", "w_o",
           "ln1_g", "ln1_b", "w_xq", "w_xkv", "w_xo", "ln2_g", "ln2_b", "w_up", "w_down", "ln3_g", "ln3_b")


def _pallas(body, after=(), **kw):
    n_after = len(after)
    if not n_after:
        return pl.pallas_call(body, **kw)
    n_in = len(kw["in_specs"])
    kw["in_specs"] = list(kw["in_specs"]) + [ANY] * n_after

    def tied(*refs):
        return body(*refs[:n_in], *refs[n_in + n_after:])

    call = pl.pallas_call(tied, **kw)
    return lambda *ops: call(*ops, *after)


def _params(**kw):
    return pltpu.CompilerParams(vmem_limit_bytes=VMEM_LIMIT, **kw)


def _tile(dim, pref, unit=LANE):
    best = None
    t = unit
    while t <= min(dim, pref):
        if dim % t == 0:
            best = t
        t += unit
    return best if best is not None else dim


def _dot(a, b, dims):
    return lax.dot_general(a, b, (dims, ((), ())), preferred_element_type=F32)


NN = ((1,), (0,))
NT = ((1,), (1,))
TN = ((0,), (0,))


def _bf(x):
    return x if x.dtype == BF16 else x.astype(BF16)


def _sds(shape, dtype):
    return jax.ShapeDtypeStruct(shape, dtype)


def _mm(name, a, b, *, dims, grid, a_spec, b_spec, out_shape, out_specs, epilogue,
        extras=(), extra_specs=(), after=()):
    assert grid[2] == 1
    n_ex, n_out = len(extras), len(out_shape)

    def body(*refs):
        ex = refs[2:2 + n_ex]
        outs = refs[2 + n_ex:2 + n_ex + n_out]
        epilogue(_dot(_bf(refs[0][...]), _bf(refs[1][...]), dims), ex, outs)

    return _pallas(
        body, after=after, name=name, grid=grid, in_specs=[a_spec, b_spec, *extra_specs], out_specs=list(out_specs),
        out_shape=list(out_shape), compiler_params=_params(dimension_semantics=("arbitrary",) * 3),
    )(a, b, *extras)


def _store(acc, ex, outs):
    for o in outs:
        o[...] = acc.astype(o.dtype)


def _ln_rows(r, g, b):
    mu = jnp.mean(r, axis=-1, keepdims=True)
    xc = r - mu
    var = jnp.mean(xc * xc, axis=-1, keepdims=True)
    rstd = lax.rsqrt(var + LN_EPS)
    xhat = xc * rstd
    return xhat * g + b, xhat, rstd


def _ep_add_scaled(acc, ex, outs):
    outs[0][...] = acc + ALPHA * ex[0][...]


def _ln_fwd(name, r, g, b):
    s, d = r.shape
    bm = _tile(s, 256)

    def body(r_ref, g_ref, b_ref, y_ref, yb_ref):
        y, _, _ = _ln_rows(r_ref[...], g_ref[...], b_ref[...])
        y_ref[...] = y
        yb_ref[...] = y.astype(BF16)

    row = pl.BlockSpec((bm, d), lambda i: (i, 0))
    vec = pl.BlockSpec((1, d), lambda i: (0, 0))
    return _pallas(body, name=name, grid=(s // bm,), in_specs=[row, vec, vec], out_specs=[row, row],
                   out_shape=[_sds((s, d), F32), _sds((s, d), BF16)], compiler_params=_params())(r, g, b)


_GC = math.sqrt(2.0 / math.pi)


def _gelu(x):
    t = jnp.tanh(_GC * (x + 0.044715 * (x * x * x)))
    return 0.5 * x * (1.0 + t), t


def _gelu_grad(x, t):
    return 0.5 * (1.0 + t) + 0.5 * x * (1.0 - t * t) * (_GC * (1.0 + 3.0 * 0.044715 * x * x))


def _sigmoid(x):
    return 1.0 / (1.0 + jnp.exp(-x))


GRP = NQ // NKV


def _band_mask(prev_ok, prev_only=False):
    rows = CHUNK if prev_only else 2 * CHUNK
    key = lax.broadcasted_iota(jnp.int32, (rows, GRP * CHUNK), 0)
    qry = jnp.bitwise_and(lax.broadcasted_iota(jnp.int32, (rows, GRP * CHUNK), 1), CHUNK - 1)
    prev = jnp.logical_and(jnp.logical_and(key < CHUNK, key > qry), prev_ok)
    if prev_only:
        return prev
    return jnp.logical_or(prev, jnp.logical_and(key >= CHUNK, key - CHUNK <= qry))


def _pair(x, g):
    return x[:, (g // 2) * LANE:(g // 2 + 1) * LANE]


def _own_head(x, g):
    xp = _pair(x, g)
    lane = lax.broadcasted_iota(jnp.int32, xp.shape, 1)
    lo = (g % 2) * HD
    return jnp.where(jnp.logical_and(lane >= lo, lane < lo + HD), xp, jnp.zeros_like(xp))


def _stack_heads(x, g, dtype=BF16):
    a = x[:, g * GRP * HD:g * GRP * HD + LANE]
    b = x[:, g * GRP * HD + LANE:(g + 1) * GRP * HD]
    ar, br = pltpu.roll(a, HD, 1), pltpu.roll(b, HD, 1)
    parts = [a, ar, b, br] if g % 2 == 0 else [ar, a, br, b]
    return jnp.concatenate(parts, axis=0).astype(dtype)


def _unstack_heads(og, g):
    o = [og[h * CHUNK:(h + 1) * CHUNK] for h in range(GRP)]
    lo = lax.broadcasted_iota(jnp.int32, (CHUNK, LANE), 1) < HD
    if g % 2 == 0:
        x0, x1, x2, x3 = o[0], pltpu.roll(o[1], HD, 1), o[2], pltpu.roll(o[3], HD, 1)
    else:
        x0, x1, x2, x3 = pltpu.roll(o[0], HD, 1), o[1], pltpu.roll(o[2], HD, 1), o[3]
    return [jnp.where(lo, x0, x1), jnp.where(lo, x2, x3)]


def _stack_rows(x, g):
    return jnp.concatenate([x[g * GRP + h:g * GRP + h + 1] for h in range(GRP)], axis=-1)


def _head_lane_sums(x, g):
    lane = lax.broadcasted_iota(jnp.int32, (8, LANE), 1)
    lo_lane = (g % 2) * HD
    sel = jnp.where(jnp.logical_and(lane >= lo_lane, lane < lo_lane + HD), 1.0, 0.0).astype(BF16)
    hi = x.astype(BF16)
    lo = (x - hi.astype(F32)).astype(BF16)
    return (_dot(sel, hi, NT) + _dot(sel, lo, NT))[0:1]


def _rope(x, cos, sin_signed):
    w = x.shape[-1]
    lane = lax.broadcasted_iota(jnp.int32, x.shape, 1)
    first = (lane % HD) < (HD // 2)
    partner = jnp.where(first, pltpu.roll(x, w - HD // 2, 1), pltpu.roll(x, HD // 2, 1))
    reps = w // LANE
    return x * jnp.tile(cos, (1, reps)) + partner * jnp.tile(sin_signed, (1, reps))


def _cast2d(name, x, after=()):
    s, d = x.shape
    bm = _tile(s, 512, 8)

    def body(x_ref, o_ref):
        o_ref[...] = x_ref[...].astype(BF16)

    spec = pl.BlockSpec((bm, d), lambda i: (i, 0))
    return _pallas(body, after=after, name=name, grid=(s // bm,), in_specs=[spec], out_specs=spec,
                   out_shape=_sds(x.shape, BF16), compiler_params=_params())(x)


def _place():
    x, y, c = lax.axis_index("x"), lax.axis_index("y"), lax.axis_index("c")
    chips = [(1 - x, y), (x, 1 - y), (1 - x, 1 - y)]
    return x, y, c, chips


def _cut(ref, axis, chip=None, half=None):
    k, n = ref.shape[-2], ref.shape[-1]
    rows, cols = slice(None), slice(None)
    if chip is not None:
        if axis == 0:
            rows = pl.ds(pl.multiple_of(chip * (k // 4), 8), k // 4)
        else:
            cols = pl.ds(pl.multiple_of(chip * (n // 4), LANE), n // 4)
    if half is not None:
        if axis == 0:
            cols = pl.ds(pl.multiple_of(half * (n // 2), LANE), n // 2)
        else:
            rows = pl.ds(pl.multiple_of(half * (k // 2), 8), k // 2)
    return ref.at[rows, cols]


def _split_start(name, srcs, lands, make, n_sem, after=()):
    ns, nl, na = len(srcs), len(lands), len(after)

    def body(*refs):
        src, land = refs[:ns], refs[ns:ns + nl]
        outs = refs[ns + nl + na:]
        for out_cp, _ in make(src, land, outs[0], outs[1]):
            out_cp.start()
        outs[-1][...] = jnp.zeros_like(outs[-1])

    res = pl.pallas_call(
        body, name=name, in_specs=[HBM] * (ns + nl) + [ANY] * na,
        out_specs=[SEM, SEM] + [HBM] * nl + [VMEM_SPEC],
        out_shape=[pltpu.SemaphoreType.DMA((n_sem,)), pltpu.SemaphoreType.DMA((n_sem,))]
        + [pltpu.HBM(a.shape, a.dtype) for a in lands] + [_sds((8, LANE), F32)],
        input_output_aliases={ns + i: 2 + i for i in range(nl)},
        compiler_params=pltpu.CompilerParams(has_side_effects=EFFECT),
    )(*[pltpu.with_memory_space_constraint(a, pltpu.HBM) for a in (*srcs, *lands)], *after)
    return res[0], res[1], list(res[2:2 + nl]), res[-1]


def _split_wait(name, srcs, lands, ssem, rsem, make, after=()):
    ns, nl, na = len(srcs), len(lands), len(after)

    def body(*refs):
        src, land = refs[:ns], refs[ns:ns + nl]
        s_ref, r_ref = refs[ns + nl], refs[ns + nl + 1]
        pairs = make(src, land, s_ref, r_ref)
        for _, in_cp in pairs:
            in_cp.wait_recv()
        for out_cp, _ in pairs:
            out_cp.wait_send()

    res = pl.pallas_call(
        body, name=name, in_specs=[HBM] * (ns + nl) + [SEM, SEM] + [ANY] * na,
        out_specs=[HBM] * nl, out_shape=[pltpu.HBM(a.shape, a.dtype) for a in lands],
        input_output_aliases={ns + i: i for i in range(nl)},
        compiler_params=pltpu.CompilerParams(has_side_effects=EFFECT),
    )(*srcs, *lands, ssem, rsem, *after)
    return list(res)


def _rcopy(src, dst, ssem, rsem, k, dev):
    return pltpu.make_async_remote_copy(src_ref=src, dst_ref=dst, send_sem=ssem.at[k], recv_sem=rsem.at[k],
                                        device_id=dev, device_id_type=MESH)


def _mk_gather_ici(axes):
    def make(src, land, ssem, rsem):
        x, y, c, chips = _place()
        me = 2 * x + y
        pairs = []
        for w, ax in enumerate(axes):
            mine = _cut(land[w], ax, chip=me, half=c)
            for j, (px, py) in enumerate(chips):
                dev = (px, py, c)
                got = _cut(land[w], ax, chip=2 * px + py, half=c)
                pairs.append((_rcopy(mine, mine, ssem, rsem, 3 * w + j, dev),
                              _rcopy(got, got, ssem, rsem, 3 * w + j, dev)))
        return pairs
    return make


def _mk_gather_d2d(axes):
    def make(src, land, ssem, rsem):
        x, y, c, chips = _place()
        sib = (x, y, 1 - c)
        pairs = []
        for w, ax in enumerate(axes):
            for j, (px, py) in enumerate(chips):
                have = _cut(land[w], ax, chip=2 * px + py, half=c)
                want = _cut(land[w], ax, chip=2 * px + py, half=1 - c)
                pairs.append((_rcopy(have, have, ssem, rsem, 3 * w + j, sib),
                              _rcopy(want, want, ssem, rsem, 3 * w + j, sib)))
        return pairs
    return make


def _mk_swap(src, land, ssem, rsem):
    x, y, c, _ = _place()
    pairs = []
    for w in range(len(src)):
        cp = _rcopy(src[w], land[w], ssem, rsem, w, (x, y, 1 - c))
        pairs.append((cp, cp))
    return pairs


def _mk_scatter(axes):
    def make(src, land, ssem, rsem):
        x, y, c, chips = _place()
        pairs = []
        for w, ax in enumerate(axes):
            for j, (px, py) in enumerate(chips):
                cp = _rcopy(_cut(src[w], ax, chip=2 * px + py), land[w].at[j], ssem, rsem, 3 * w + j, (px, py, c))
                pairs.append((cp, cp))
        return pairs
    return make


def _mk_exchange(axes):
    def make(src, land, ssem, rsem):
        x, y, c, _ = _place()
        sib = (x, y, 1 - c)
        pairs = []
        for w, ax in enumerate(axes):
            have = _cut(land[w], ax, half=c)
            want = _cut(land[w], ax, half=1 - c)
            pairs.append((_rcopy(have, have, ssem, rsem, w, sib), _rcopy(want, want, ssem, rsem, w, sib)))
        return pairs
    return make


def _place_own(name, shard, axis, meidx, after=()):
    _, r, c = shard.shape
    full = (4 * r, c) if axis == 0 else (r, 4 * c)
    br = _tile(r, 512, 8)
    nb = r // br
    if axis == 0:
        ospec = pl.BlockSpec((br, c), lambda i, me: (me[0] * nb + i, 0))
    else:
        ospec = pl.BlockSpec((br, c), lambda i, me: (i, me[0]))
    n_after = len(after)

    def body(me_ref, s_ref, *rest):
        o0_ref, o1_ref = rest[n_after:]
        o0_ref[...] = s_ref[0].astype(BF16)
        o1_ref[...] = s_ref[1].astype(BF16)

    return pl.pallas_call(
        body, name=name,
        grid_spec=pltpu.PrefetchScalarGridSpec(
            num_scalar_prefetch=1, grid=(nb,),
            in_specs=[pl.BlockSpec((2, br, c), lambda i, me: (0, i, 0))] + [ANY] * n_after,
            out_specs=[ospec, ospec]),
        out_shape=[_sds(full, BF16)] * 2, compiler_params=_params(),
    )(meidx, shard, *after)


def _sum_half(name, own, slots, axis, mc):
    _, r, cc = slots.shape
    br = _tile(r, 256, 8)
    nb = r // br
    if axis == 0:
        own_spec = pl.BlockSpec((br, cc), lambda i, mc: (mc[0] * nb + i, 0))
        out_spec = pl.BlockSpec((br, cc), lambda i, mc: (i, mc[1]))
        shape = (r, 2 * cc)
    else:
        own_spec = pl.BlockSpec((br, cc), lambda i, mc: (i, mc[0]))
        out_spec = pl.BlockSpec((br, cc), lambda i, mc: (mc[1] * nb + i, 0))
        shape = (2 * r, cc)

    def body(mc_ref, own_ref, s_ref, o_ref):
        acc = own_ref[...].astype(F32)
        for i in range(3):
            acc = acc + s_ref[i].astype(F32)
        o_ref[...] = acc

    return pl.pallas_call(
        body, name=name,
        grid_spec=pltpu.PrefetchScalarGridSpec(
            num_scalar_prefetch=1, grid=(nb,),
            in_specs=[own_spec, pl.BlockSpec((3, br, cc), lambda i, mc: (0, i, 0))], out_specs=out_spec),
        out_shape=_sds(shape, F32), compiler_params=_params(),
    )(mc, own, slots)


def _mk_small(src, land, ssem, rsem):
    x, y, c, _ = _place()
    me = 4 * x + 2 * y + c
    pairs = []
    for k in range(1, 8):
        peer = (1 - x if k & 4 else x, 1 - y if k & 2 else y, 1 - c if k & 1 else c)
        got = land[0].at[4 * peer[0] + 2 * peer[1] + peer[2]]
        pairs.append((_rcopy(src[0], land[0].at[me], ssem, rsem, k - 1, peer),
                      _rcopy(got, got, ssem, rsem, k - 1, peer)))
    return pairs


def _place_slot(name, packed, me8):
    rows, lanes = packed.shape
    br = _tile(rows, 512, 8)

    def body(me_ref, p_ref, o_ref):
        o_ref[...] = p_ref[...]

    return pl.pallas_call(
        body, name=name,
        grid_spec=pltpu.PrefetchScalarGridSpec(
            num_scalar_prefetch=1, grid=(rows // br,),
            in_specs=[pl.BlockSpec((br, lanes), lambda i, me: (i, 0))],
            out_specs=pl.BlockSpec((None, br, lanes), lambda i, me: (me[0], i, 0))),
        out_shape=_sds((8, rows, lanes), F32), compiler_params=_params(),
    )(me8, packed)


def _sum_slots(name, slots):
    _, rows, lanes = slots.shape
    br = _tile(rows, 512, 8)

    def body(s_ref, o_ref):
        acc = s_ref[0]
        for i in range(1, 8):
            acc = acc + s_ref[i]
        o_ref[...] = acc

    return pl.pallas_call(
        body, name=name, grid=(rows // br,), in_specs=[pl.BlockSpec((8, br, lanes), lambda i: (0, i, 0))],
        out_specs=pl.BlockSpec((br, lanes), lambda i: (i, 0)), out_shape=_sds((rows, lanes), F32),
        compiler_params=_params(),
    )(slots)


def _adamw_math(w, g, m, v):
    m2 = ADAM_B1 * m + (1.0 - ADAM_B1) * g
    v2 = ADAM_B2 * v + (1.0 - ADAM_B2) * (g * g)
    m_hat = m2 / (1.0 - ADAM_B1 ** ADAM_STEP)
    v_hat = v2 / (1.0 - ADAM_B2 ** ADAM_STEP)
    delta = -ADAM_LR * (m_hat / (jnp.sqrt(v_hat) + ADAM_EPS) + ADAM_WD * w)
    return delta, m2, v2


def _adamw(name, w, g, m, v, layer, prev=None, after=()):
    _, r, c = w.shape
    br = _tile(r, 256, 8)
    n_prev = 0 if prev is None else 4

    def body(*refs):
        w_ref, g_ref, m_ref, v_ref = refs[:4]
        go_ref, d_ref, mo_ref, vo_ref = refs[4 + n_prev:]
        gg = g_ref[...]
        delta, m2, v2 = _adamw_math(w_ref[...], gg, m_ref[...], v_ref[...])
        go_ref[...] = gg
        d_ref[...] = delta
        mo_ref[...] = m2
        vo_ref[...] = v2

    spec = pl.BlockSpec((None, br, c), lambda i: (layer, i, 0))
    return _pallas(
        body, after=after, name=name, grid=(r // br,),
        in_specs=[spec, pl.BlockSpec((br, c), lambda i: (i, 0)), spec, spec] + [ANY] * n_prev,
        out_specs=[spec] * 4, out_shape=[_sds(w.shape, F32)] * 4,
        input_output_aliases={4 + i: i for i in range(n_prev)}, compiler_params=_params(),
    )(w, g, m, v, *(prev or ()))


def _gmlp_fwd(name, proj, ln_g, ln_b, w_s, b_st):
    s = proj.shape[0]

    def body(u_ref, v_ref, g_ref, b_ref, ws_ref, bst_ref, sg_ref):
        gu, _ = _gelu(u_ref[...])
        gv, _ = _gelu(v_ref[...])
        vn, _, _ = _ln_rows(gv, g_ref[...], b_ref[...])
        vn = vn.astype(BF16)
        row = lax.broadcasted_iota(jnp.int32, (CHUNK, CHUNK), 0)
        col = lax.broadcasted_iota(jnp.int32, (CHUNK, CHUNK), 1)
        tril = col <= row
        outs = []
        for g in range(GROUPS):
            sl = slice(g * LANE, (g + 1) * LANE)
            w = jnp.where(tril, ws_ref[g], 0.0).astype(BF16)
            mixed = _dot(w, vn[:, sl], NN) + bst_ref[:, g:g + 1]
            outs.append(gu[:, sl] * mixed)
        sg_ref[...] = jnp.concatenate(outs, axis=-1).astype(BF16)

    return _pallas(
        body, name=name, grid=(s // CHUNK,),
        in_specs=[pl.BlockSpec((CHUNK, GMLP_W), lambda n: (n, 0)), pl.BlockSpec((CHUNK, GMLP_W), lambda n: (n, 1)),
                  pl.BlockSpec((1, GMLP_W), lambda n: (0, 0)), pl.BlockSpec((1, GMLP_W), lambda n: (0, 0)),
                  pl.BlockSpec((GROUPS, CHUNK, CHUNK), lambda n: (0, 0, 0)),
                  pl.BlockSpec((CHUNK, GROUPS), lambda n: (0, 0))],
        out_specs=pl.BlockSpec((CHUNK, GMLP_W), lambda n: (n, 0)),
        out_shape=_sds((s, GMLP_W), BF16), compiler_params=_params(),
    )(proj, proj, ln_g, ln_b, w_s, b_st)


def _swa_fwd(name, proj, cos4, sin4, sinks, after=()):
    s = proj.shape[0]
    w = CHUNK
    scale = HD ** -0.5

    def body(q_ref, k_ref, v_ref, cos_ref, sin_ref, sink_ref, o_ref, qr_ref, kr_ref, lse_ref, kprev, vprev):
        n = pl.program_id(0)

        @pl.when(n == 0)
        def _():
            kprev[...] = jnp.zeros_like(kprev)
            vprev[...] = jnp.zeros_like(vprev)

        cos, sin = cos_ref[...], sin_ref[...]
        qr = _rope(q_ref[...], cos, sin)
        kr = _rope(k_ref[...], cos, sin).astype(BF16)
        vb = v_ref[...].astype(BF16)
        kk = jnp.concatenate([kprev[...], kr], axis=0)
        vv = jnp.concatenate([vprev[...], vb], axis=0)
        valid = _band_mask(n > 0)
        outs, lses = [], []
        for g in range(NKV):
            sc = jnp.where(valid, _dot(_own_head(kk, g), _stack_heads(qr, g), NT) * scale, NEG)
            sink = sink_ref[g]
            mx = jnp.maximum(jnp.max(sc, axis=0, keepdims=True), sink)
            p = jnp.exp(sc - mx)
            den = jnp.sum(p, axis=0, keepdims=True) + jnp.exp(sink - mx)
            og = _dot((p * (1.0 / den)).astype(BF16), _pair(vv, g), TN)
            outs.extend(_unstack_heads(og, g))
            lg = mx + jnp.log(den)
            lses.extend([lg[:, h * w:(h + 1) * w] for h in range(GRP)])
        o_ref[...] = jnp.concatenate(outs, axis=-1).astype(BF16)
        lse_ref[...] = jnp.concatenate(lses, axis=0)
        qr_ref[...] = qr.astype(BF16)
        kr_ref[...] = kr
        kprev[...] = kr
        vprev[...] = vb

    return _pallas(
        body, after=after, name=name, grid=(s // w,),
        in_specs=[pl.BlockSpec((w, ATT_W), lambda n: (n, OFF_Q // ATT_W)),
                  pl.BlockSpec((w, KV_W), lambda n: (n, OFF_K // KV_W)),
                  pl.BlockSpec((w, KV_W), lambda n: (n, OFF_VA // KV_W)),
                  pl.BlockSpec((w, LANE), lambda n: (n, 0)), pl.BlockSpec((w, LANE), lambda n: (n, 0)),
                  pl.BlockSpec((NKV, 1, GRP * w), lambda n: (0, 0, 0))],
        out_specs=[pl.BlockSpec((w, ATT_W), lambda n: (n, 0)), pl.BlockSpec((w, ATT_W), lambda n: (n, 0)),
                   pl.BlockSpec((w, KV_W), lambda n: (n, 0)), pl.BlockSpec((None, NQ, w), lambda n: (n, 0, 0))],
        out_shape=[_sds((s, ATT_W), BF16), _sds((s, ATT_W), BF16), _sds((s, KV_W), BF16),
                   _sds((s // w, NQ, w), F32)],
        scratch_shapes=[pltpu.VMEM((w, KV_W), BF16), pltpu.VMEM((w, KV_W), BF16)],
        compiler_params=_params(dimension_semantics=("arbitrary",)),
    )(proj, proj, proj, cos4, sin4, sinks)


def _gate_fwd(name, sg, attn, wa, wb, proj, b_gate, d):
    s = sg.shape[0]
    bm, bn = _tile(s, 1024), _tile(d, 512)
    off_a, off_b = OFF_GA // bn, (OFF_GA + d) // bn

    def body(sg_ref, at_ref, wa_ref, wb_ref, ga_ref, gb_ref, ba_ref, bb_ref, m_ref, ya_ref, yb_ref):
        ya = _dot(sg_ref[...], wa_ref[...], NN)
        yb = _dot(at_ref[...], wb_ref[...], NN)
        sa = _sigmoid(ga_ref[...] + ba_ref[...])
        sb = _sigmoid(gb_ref[...] + bb_ref[...])
        m_ref[...] = (sa * ya + sb * yb).astype(BF16)
        ya_ref[...] = ya.astype(BF16)
        yb_ref[...] = yb.astype(BF16)

    tile = pl.BlockSpec((bm, bn), lambda i, j: (i, j))
    return _pallas(
        body, name=name, grid=(s // bm, d // bn),
        in_specs=[pl.BlockSpec((bm, GMLP_W), lambda i, j: (i, 0)), pl.BlockSpec((bm, ATT_W), lambda i, j: (i, 0)),
                  pl.BlockSpec((GMLP_W, bn), lambda i, j: (0, j)), pl.BlockSpec((ATT_W, bn), lambda i, j: (0, j)),
                  pl.BlockSpec((bm, bn), lambda i, j: (i, off_a + j)),
                  pl.BlockSpec((bm, bn), lambda i, j: (i, off_b + j)),
                  pl.BlockSpec((1, bn), lambda i, j: (0, j)), pl.BlockSpec((1, bn), lambda i, j: (0, d // bn + j))],
        out_specs=[tile, tile, tile], out_shape=[_sds((s, d), BF16)] * 3,
        compiler_params=_params(),
    )(sg, attn, wa, wb, proj, proj, b_gate, b_gate)


def _xattn_fwd(name, xb, xf, wq, kv, wo, ln_g, ln_b, after=()):
    s, d = xf.shape
    mem = kv.shape[0]
    bm = _tile(s, 512)
    scale = XHD ** -0.5

    def body(xb_ref, xf_ref, wq_ref, kv_ref, wo_ref, g_ref, b_ref, q_out, o_out, r_out, y_out, yb_out):
        qb = _dot(xb_ref[...], wq_ref[...], NN).astype(BF16)
        kvv = kv_ref[...]
        outs = []
        for h in range(XH):
            hs = slice(h * XHD, (h + 1) * XHD)
            vs = slice(X_W + h * XHD, X_W + (h + 1) * XHD)
            sc = _dot(qb[:, hs], kvv[:, hs], NT) * scale
            mx = jnp.max(sc, axis=-1, keepdims=True)
            p = jnp.exp(sc - mx)
            p = p / jnp.sum(p, axis=-1, keepdims=True)
            outs.append(_dot(p.astype(BF16), kvv[:, vs], NN))
        ob = jnp.concatenate(outs, axis=-1).astype(BF16)
        yv = _dot(ob, wo_ref[...], NN)
        r = ALPHA * xf_ref[...] + yv
        yn, _, _ = _ln_rows(r, g_ref[...], b_ref[...])
        q_out[...] = qb
        o_out[...] = ob
        r_out[...] = r
        y_out[...] = yn
        yb_out[...] = yn.astype(BF16)

    row = lambda wd: pl.BlockSpec((bm, wd), lambda i: (i, 0))
    return _pallas(
        body, after=after, name=name, grid=(s // bm,),
        in_specs=[row(d), row(d), pl.BlockSpec((d, X_W), lambda i: (0, 0)),
                  pl.BlockSpec((mem, 2 * X_W), lambda i: (0, 0)), pl.BlockSpec((X_W, d), lambda i: (0, 0)),
                  pl.BlockSpec((1, d), lambda i: (0, 0)), pl.BlockSpec((1, d), lambda i: (0, 0))],
        out_specs=[row(X_W), row(X_W), row(d), row(d), row(d)],
        out_shape=[_sds((s, X_W), BF16), _sds((s, X_W), BF16), _sds((s, d), F32), _sds((s, d), F32),
                   _sds((s, d), BF16)],
        compiler_params=_params(),
    )(xb, xf, wq, kv, wo, ln_g, ln_b)


def _loss_grad(name, y, tgt):
    s, d = y.shape
    bm = _tile(s, 512)

    def body(y_ref, t_ref, dy_ref, loss_ref):
        i = pl.program_id(0)
        err = y_ref[...] - t_ref[...]
        dy_ref[...] = err * (1.0 / d)
        part = 0.5 * jnp.sum(jnp.sum(err * err, axis=-1, keepdims=True), axis=0, keepdims=True) * (1.0 / d)

        @pl.when(i == 0)
        def _():
            loss_ref[...] = part

        @pl.when(i > 0)
        def _():
            loss_ref[...] += part

    row = pl.BlockSpec((bm, d), lambda i: (i, 0))
    return _pallas(
        body, name=name, grid=(s // bm,), in_specs=[row, row],
        out_specs=[row, pl.BlockSpec((1, 1), lambda i: (0, 0))],
        out_shape=[_sds((s, d), F32), _sds((1, 1), F32)],
        compiler_params=_params(dimension_semantics=("arbitrary",)),
    )(y, tgt)


def _ln_bwd(name, dy, r, g, after=()):
    s, d = r.shape
    bm = _tile(s, 256)

    def body(dy_ref, r_ref, g_ref, dr_ref, drb_ref, dg_ref, db_ref):
        i = pl.program_id(0)
        dyv = dy_ref[...]
        _, xhat, rstd = _ln_rows(r_ref[...], g_ref[...], 0.0)
        dxh = dyv * g_ref[...]
        m1 = jnp.mean(dxh, axis=-1, keepdims=True)
        m2 = jnp.mean(dxh * xhat, axis=-1, keepdims=True)
        dr = rstd * (dxh - m1 - xhat * m2)
        dr_ref[...] = dr
        drb_ref[...] = dr.astype(BF16)
        dg = jnp.sum(dyv * xhat, axis=0, keepdims=True)
        db = jnp.sum(dyv, axis=0, keepdims=True)

        @pl.when(i == 0)
        def _():
            dg_ref[...] = dg
            db_ref[...] = db

        @pl.when(i > 0)
        def _():
            dg_ref[...] += dg
            db_ref[...] += db

    row = pl.BlockSpec((bm, d), lambda i: (i, 0))
    vec = pl.BlockSpec((1, d), lambda i: (0, 0))
    return _pallas(
        body, after=after, name=name, grid=(s // bm,), in_specs=[row, row, vec], out_specs=[row, row, vec, vec],
        out_shape=[_sds((s, d), F32), _sds((s, d), BF16), _sds((1, d), F32), _sds((1, d), F32)],
        compiler_params=_params(dimension_semantics=("arbitrary",)),
    )(dy, r, g)


def _xattn_bwd(name, dyb, drf, q, kv, wo, wq):
    s, d = drf.shape
    mem = kv.shape[0]
    bm = _tile(s, 512)
    scale = XHD ** -0.5

    def body(dy_ref, dr_ref, q_ref, kv_ref, wo_ref, wq_ref, dx_out, dq_out, dkv_out):
        i = pl.program_id(0)
        dob = _dot(dy_ref[...], wo_ref[...], NT).astype(BF16)
        qb = q_ref[...]
        kvv = kv_ref[...]
        dqs, dks, dvs = [], [], []
        for h in range(XH):
            hs = slice(h * XHD, (h + 1) * XHD)
            vs = slice(X_W + h * XHD, X_W + (h + 1) * XHD)
            sc = _dot(qb[:, hs], kvv[:, hs], NT) * scale
            mx = jnp.max(sc, axis=-1, keepdims=True)
            p = jnp.exp(sc - mx)
            p = p / jnp.sum(p, axis=-1, keepdims=True)
            dp = _dot(dob[:, hs], kvv[:, vs], NT)
            dsum = jnp.sum(p * dp, axis=-1, keepdims=True)
            dsb = (p * (dp - dsum) * scale).astype(BF16)
            dqs.append(_dot(dsb, kvv[:, hs], NN))
            dks.append(_dot(dsb, qb[:, hs], TN))
            dvs.append(_dot(p.astype(BF16), dob[:, hs], TN))
        dqb = jnp.concatenate(dqs, axis=-1).astype(BF16)
        dq_out[...] = dqb
        dx_out[...] = _dot(dqb, wq_ref[...], NT) + ALPHA * dr_ref[...]
        dkv = jnp.concatenate(dks + dvs, axis=-1)

        @pl.when(i == 0)
        def _():
            dkv_out[...] = dkv

        @pl.when(i > 0)
        def _():
            dkv_out[...] += dkv

    row = lambda wd: pl.BlockSpec((bm, wd), lambda i: (i, 0))
    return _pallas(
        body, name=name, grid=(s // bm,),
        in_specs=[row(d), row(d), row(X_W), pl.BlockSpec((mem, 2 * X_W), lambda i: (0, 0)),
                  pl.BlockSpec((X_W, d), lambda i: (0, 0)), pl.BlockSpec((d, X_W), lambda i: (0, 0))],
        out_specs=[row(d), row(X_W), pl.BlockSpec((mem, 2 * X_W), lambda i: (0, 0))],
        out_shape=[_sds((s, d), F32), _sds((s, X_W), BF16), _sds((mem, 2 * X_W), F32)],
        compiler_params=_params(dimension_semantics=("arbitrary",)),
    )(dyb, drf, q, kv, wo, wq)


def _gate_bwd(name, dr1b, w_o, proj, ya, yb, b_gate, d, after=()):
    s = dr1b.shape[0]
    bm, bn = _tile(s, 1024), _tile(d, 512)
    off_a, off_b = OFF_GA // bn, (OFF_GA + d) // bn
    nj = d // bn

    def body(a_ref, w_ref, ga_ref, gb_ref, ya_ref, yb_ref, ba_ref, bb_ref, dya_ref, dyb_ref, dg_ref, dba_ref, dbb_ref):
        i = pl.program_id(1)
        dm = _dot(a_ref[...], w_ref[...], NT)
        sa = _sigmoid(ga_ref[...] + ba_ref[...])
        sb = _sigmoid(gb_ref[...] + bb_ref[...])
        dya_ref[...] = (dm * sa).astype(BF16)
        dyb_ref[...] = (dm * sb).astype(BF16)
        dga = dm * ya_ref[...].astype(F32) * (sa * (1.0 - sa))
        dgb = dm * yb_ref[...].astype(F32) * (sb * (1.0 - sb))
        dg_ref[0] = dga.astype(BF16)
        dg_ref[1] = dgb.astype(BF16)
        sa_sum = jnp.sum(dga, axis=0, keepdims=True)
        sb_sum = jnp.sum(dgb, axis=0, keepdims=True)

        @pl.when(i == 0)
        def _():
            dba_ref[...] = sa_sum
            dbb_ref[...] = sb_sum

        @pl.when(i > 0)
        def _():
            dba_ref[...] += sa_sum
            dbb_ref[...] += sb_sum

    tile = pl.BlockSpec((bm, bn), lambda j, i: (i, j))
    return _pallas(
        body, after=after, name=name, grid=(nj, s // bm),
        in_specs=[pl.BlockSpec((bm, d), lambda j, i: (i, 0)),
                  pl.BlockSpec((bn, d), lambda j, i: (j, 0)),
                  pl.BlockSpec((bm, bn), lambda j, i: (i, off_a + j)),
                  pl.BlockSpec((bm, bn), lambda j, i: (i, off_b + j)),
                  tile, tile,
                  pl.BlockSpec((1, bn), lambda j, i: (0, j)), pl.BlockSpec((1, bn), lambda j, i: (0, nj + j))],
        out_specs=[tile, tile, pl.BlockSpec((2, bm, bn), lambda j, i: (0, i, j)),
                   pl.BlockSpec((1, bn), lambda j, i: (0, j)), pl.BlockSpec((1, bn), lambda j, i: (0, j))],
        out_shape=[_sds((s, d), BF16), _sds((s, d), BF16), _sds((2, s, d), BF16), _sds((1, d), F32),
                   _sds((1, d), F32)],
        compiler_params=_params(dimension_semantics=("arbitrary", "arbitrary")),
    )(dr1b, w_o, proj, proj, ya, yb, b_gate, b_gate)


def _gmlp_bwd(name, proj, dsg, ln_g, ln_b, w_s, b_st):
    s = proj.shape[0]

    def body(u_ref, v_ref, dsg_ref, g_ref, b_ref, ws_ref, bst_ref, duv_ref, dws_ref, dbst_ref, dlg_ref, dlb_ref):
        n = pl.program_id(0)
        u, v = u_ref[...], v_ref[...]
        gu, tu = _gelu(u)
        gv, tv = _gelu(v)
        gam = g_ref[...]
        vn, xhat, rstd = _ln_rows(gv, gam, b_ref[...])
        vnb = vn.astype(BF16)
        dsg = dsg_ref[...].astype(F32)
        row = lax.broadcasted_iota(jnp.int32, (CHUNK, CHUNK), 0)
        col = lax.broadcasted_iota(jnp.int32, (CHUNK, CHUNK), 1)
        tril = col <= row
        dgu, dvn, dws, dbs = [], [], [], []
        for g in range(GROUPS):
            sl = slice(g * LANE, (g + 1) * LANE)
            w = jnp.where(tril, ws_ref[g], 0.0).astype(BF16)
            mixed = _dot(w, vnb[:, sl], NN) + bst_ref[:, g:g + 1]
            dgu.append(dsg[:, sl] * mixed)
            dmx = dsg[:, sl] * gu[:, sl]
            dmxb = dmx.astype(BF16)
            dbs.append(jnp.sum(dmx, axis=-1, keepdims=True))
            dws.append(jnp.where(tril, _dot(dmxb, vnb[:, sl], NT), 0.0))
            dvn.append(_dot(w, dmxb, TN))
        dvn = jnp.concatenate(dvn, axis=-1)
        dgu = jnp.concatenate(dgu, axis=-1)
        dxh = dvn * gam
        m1 = jnp.mean(dxh, axis=-1, keepdims=True)
        m2 = jnp.mean(dxh * xhat, axis=-1, keepdims=True)
        dgv = rstd * (dxh - m1 - xhat * m2)
        du = dgu * _gelu_grad(u, tu)
        dv = dgv * _gelu_grad(v, tv)
        duv_ref[...] = jnp.concatenate([du, dv], axis=-1).astype(BF16)
        dlg = jnp.sum(dvn * xhat, axis=0, keepdims=True)
        dlb = jnp.sum(dvn, axis=0, keepdims=True)
        dbst = jnp.concatenate(dbs, axis=-1)

        @pl.when(n == 0)
        def _():
            for g in range(GROUPS):
                dws_ref[g] = dws[g]
            dbst_ref[...] = dbst
            dlg_ref[...] = dlg
            dlb_ref[...] = dlb

        @pl.when(n > 0)
        def _():
            for g in range(GROUPS):
                dws_ref[g] += dws[g]
            dbst_ref[...] += dbst
            dlg_ref[...] += dlg
            dlb_ref[...] += dlb

    vec = pl.BlockSpec((1, GMLP_W), lambda n: (0, 0))
    return _pallas(
        body, name=name, grid=(s // CHUNK,),
        in_specs=[pl.BlockSpec((CHUNK, GMLP_W), lambda n: (n, 0)), pl.BlockSpec((CHUNK, GMLP_W), lambda n: (n, 1)),
                  pl.BlockSpec((CHUNK, GMLP_W), lambda n: (n, 0)), vec, vec,
                  pl.BlockSpec((GROUPS, CHUNK, CHUNK), lambda n: (0, 0, 0)),
                  pl.BlockSpec((CHUNK, GROUPS), lambda n: (0, 0))],
        out_specs=[pl.BlockSpec((CHUNK, 2 * GMLP_W), lambda n: (n, 0)),
                   pl.BlockSpec((GROUPS, CHUNK, CHUNK), lambda n: (0, 0, 0)),
                   pl.BlockSpec((CHUNK, GROUPS), lambda n: (0, 0)), vec, vec],
        out_shape=[_sds((s, 2 * GMLP_W), BF16), _sds((GROUPS, CHUNK, CHUNK), F32), _sds((CHUNK, GROUPS), F32),
                   _sds((1, GMLP_W), F32), _sds((1, GMLP_W), F32)],
        compiler_params=_params(dimension_semantics=("arbitrary",)),
    )(proj, proj, dsg, ln_g, ln_b, w_s, b_st)


def _swa_bwd(name, qr, kr, proj, do, o, lse, sinks, cos4, nsin4, after=()):
    s = qr.shape[0]
    w = CHUNK
    nblk = s // w
    scale = HD ** -0.5
    grp = NQ // NKV

    def body(qj_ref, qn_ref, kj_ref, kp_ref, vj_ref, vp_ref, doj_ref, don_ref, oj_ref, on_ref, lj_ref, ln_ref,
             sink_ref, cos_ref, sin_ref, out_ref, dsink_ref):
        j = pl.program_id(0)
        qj, qn = qj_ref[...].astype(F32), qn_ref[...].astype(F32)
        doj, don = doj_ref[...].astype(F32), don_ref[...].astype(F32)
        kk = jnp.concatenate([kp_ref[...], kj_ref[...]], axis=0)
        vv = jnp.concatenate([vp_ref[...], vj_ref[...]], axis=0).astype(BF16)
        lj, lnx = lj_ref[...], ln_ref[...]
        prod_j = doj * oj_ref[...].astype(F32)
        prod_n = don * on_ref[...].astype(F32)
        valid_j = _band_mask(j > 0)
        valid_n = _band_mask(j + 1 < nblk, prev_only=True)
        lo = lax.broadcasted_iota(jnp.int32, (w, LANE), 1) < HD
        dqs, dsk, dk_g, dv_g = [], [], [], []
        for g in range(NKV):
            kz, vz = _own_head(kk, g), _own_head(vv, g)
            kz_c, vz_c = kz[w:], vz[w:]
            qg_j, qg_n = _stack_heads(qj, g), _stack_heads(qn, g)
            dog_j, dog_n = _stack_heads(doj, g), _stack_heads(don, g)
            l_j, l_n = _stack_rows(lj, g), _stack_rows(lnx, g)
            d_j = _head_lane_sums(_stack_heads(prod_j, g, F32), g)
            d_n = _head_lane_sums(_stack_heads(prod_n, g, F32), g)
            p = jnp.where(valid_j, jnp.exp(_dot(kz, qg_j, NT) * scale - l_j), 0.0)
            ds = (p * (_dot(vz, dog_j, NT) - d_j) * scale).astype(BF16)
            dqs.extend(_unstack_heads(_dot(ds, kz, TN), g))
            p2 = jnp.where(valid_n, jnp.exp(_dot(kz_c, qg_n, NT) * scale - l_n), 0.0)
            ds2 = (p2 * (_dot(vz_c, dog_n, NT) - d_n) * scale).astype(BF16)
            dk_g.append(_dot(ds[w:], qg_j, NN) + _dot(ds2, qg_n, NN))
            dv_g.append(_dot(p[w:].astype(BF16), dog_j, NN) + _dot(p2.astype(BF16), dog_n, NN))
            t = jnp.exp(sink_ref[g] - l_j) * d_j
            dsk.extend([-jnp.sum(t[:, h * w:(h + 1) * w], axis=-1, keepdims=True) for h in range(GRP)])
        cos, nsin = cos_ref[...], sin_ref[...]
        dq = _rope(jnp.concatenate(dqs, axis=-1), cos, nsin)
        dk = _rope(jnp.concatenate([jnp.where(lo, dk_g[2 * m], dk_g[2 * m + 1]) for m in range(NKV // 2)], axis=-1),
                   cos, nsin)
        dv = jnp.concatenate([jnp.where(lo, dv_g[2 * m], dv_g[2 * m + 1]) for m in range(NKV // 2)], axis=-1)
        out_ref[...] = jnp.concatenate([dq, dk, dv], axis=-1).astype(BF16)
        dsink = jnp.concatenate(dsk, axis=-1)

        @pl.when(j == 0)
        def _():
            dsink_ref[...] = dsink

        @pl.when(j > 0)
        def _():
            dsink_ref[...] += dsink

    nxt = lambda j: jnp.minimum(j + 1, nblk - 1)
    prv = lambda j: jnp.maximum(j - 1, 0)
    va = OFF_VA // KV_W
    return _pallas(
        body, after=after, name=name, grid=(nblk,),
        in_specs=[pl.BlockSpec((w, ATT_W), lambda j: (j, 0)), pl.BlockSpec((w, ATT_W), lambda j: (nxt(j), 0)),
                  pl.BlockSpec((w, KV_W), lambda j: (j, 0)), pl.BlockSpec((w, KV_W), lambda j: (prv(j), 0)),
                  pl.BlockSpec((w, KV_W), lambda j: (j, va)), pl.BlockSpec((w, KV_W), lambda j: (prv(j), va)),
                  pl.BlockSpec((w, ATT_W), lambda j: (j, 0)), pl.BlockSpec((w, ATT_W), lambda j: (nxt(j), 0)),
                  pl.BlockSpec((w, ATT_W), lambda j: (j, 0)), pl.BlockSpec((w, ATT_W), lambda j: (nxt(j), 0)),
                  pl.BlockSpec((None, NQ, w), lambda j: (j, 0, 0)),
                  pl.BlockSpec((None, NQ, w), lambda j: (nxt(j), 0, 0)),
                  pl.BlockSpec((NKV, 1, GRP * w), lambda j: (0, 0, 0)),
                  pl.BlockSpec((w, LANE), lambda j: (j, 0)), pl.BlockSpec((w, LANE), lambda j: (j, 0))],
        out_specs=[pl.BlockSpec((w, ATT_W + 2 * KV_W), lambda j: (j, 0)), pl.BlockSpec((1, NQ), lambda j: (0, 0))],
        out_shape=[_sds((s, ATT_W + 2 * KV_W), BF16), _sds((1, NQ), F32)],
        compiler_params=_params(dimension_semantics=("arbitrary",)),
    )(qr, qr, kr, kr, proj, proj, do, do, o, o, lse, lse, sinks, cos4, nsin4)


def _mm_nn(name, a, w, *, out_dtypes, epilogue=_store, bm_pref=1024, bn_pref=1024, after=()):
    m, k = a.shape
    n = w.shape[-1]
    bm, bn = _tile(m, bm_pref), _tile(n, bn_pref)
    tile = pl.BlockSpec((bm, bn), lambda i, j, kk: (i, j))
    return _mm(name, a, w, dims=NN, grid=(m // bm, n // bn, 1),
               a_spec=pl.BlockSpec((bm, k), lambda i, j, kk: (i, 0)),
               b_spec=pl.BlockSpec((k, bn), lambda i, j, kk: (0, j)),
               out_shape=[_sds((m, n), dt) for dt in out_dtypes], out_specs=[tile] * len(out_dtypes),
               epilogue=epilogue, after=after)


def _dw_half(name, a, b, axis, hidx, got=None, after=()):
    s, m = a.shape
    n = b.shape[-1]
    mh, nh = (m // 2, n) if axis == 1 else (m, n // 2)
    bm, bn = _tile(mh, 1024), _tile(nh, 1024)
    nmb, nnb = mh // bm, nh // bn
    if axis == 1:
        a_spec = pl.BlockSpec((s, bm), lambda i, j, h: (0, h[0] * nmb + i))
        b_spec = pl.BlockSpec((s, bn), lambda i, j, h: (0, j))
    else:
        a_spec = pl.BlockSpec((s, bm), lambda i, j, h: (0, i))
        b_spec = pl.BlockSpec((s, bn), lambda i, j, h: (0, h[0] * nnb + j))
    tile = pl.BlockSpec((bm, bn), lambda i, j, h: (i, j))
    n_got, n_after = (0 if got is None else 1), len(after)

    def body(h_ref, a_ref, b_ref, *rest):
        acc = _dot(a_ref[...], b_ref[...], TN)
        if n_got:
            acc = acc + rest[0][...].astype(F32)
        rest[-1][...] = acc.astype(BF16)

    return pl.pallas_call(
        body, name=name,
        grid_spec=pltpu.PrefetchScalarGridSpec(
            num_scalar_prefetch=1, grid=(nmb, nnb),
            in_specs=[a_spec, b_spec] + [tile] * n_got + [ANY] * n_after, out_specs=tile),
        out_shape=_sds((mh, nh), BF16), compiler_params=_params(dimension_semantics=("arbitrary", "arbitrary")),
    )(hidx, a, b, *(() if got is None else (got,)), *after)


def _mm_residual(name, a, w, x, after=()):
    s, k = a.shape
    d = w.shape[-1]
    bm, bn = _tile(s, 1024), _tile(d, 1024 if k <= 2048 else 512)
    tile = pl.BlockSpec((bm, bn), lambda i, j, kk: (i, j))
    return _mm(name, a, w, dims=NN, grid=(s // bm, d // bn, 1),
               a_spec=pl.BlockSpec((bm, k), lambda i, j, kk: (i, 0)),
               b_spec=pl.BlockSpec((k, bn), lambda i, j, kk: (0, j)),
               extras=(x,), extra_specs=(tile,), out_shape=[_sds((s, d), F32)], out_specs=[tile],
               epilogue=_ep_add_scaled, after=after)[0]


def _dw_pieces_half(name, a, pieces, widths, hidx, got=None, after=()):
    s, m = a.shape
    total = sum(widths)
    mh = m // 2
    bm, bn = _tile(mh, 1024), 512
    nmb = mh // bm
    n_got = 0 if got is None else 1
    out, off = None, 0
    for p, (piece, wd) in enumerate(zip(pieces, widths)):
        if isinstance(piece, tuple):
            arr = piece[0]
            b_spec = pl.BlockSpec((None, s, bn), (lambda ix: lambda i, j, h: (ix, 0, j))(piece[1]))
        else:
            arr, b_spec = piece, pl.BlockSpec((s, bn), lambda i, j, h: (0, j))
        tile = pl.BlockSpec((bm, bn), (lambda c: lambda i, j, h: (i, c + j))(off // bn))
        prev = () if out is None else (out,)
        first_after = after if out is None else ()

        def body(h_ref, a_ref, b_ref, *rest):
            acc = _dot(a_ref[...], b_ref[...], TN)
            if n_got:
                acc = acc + rest[0][...].astype(F32)
            rest[-1][...] = acc.astype(BF16)

        out = pl.pallas_call(
            body, name=f"{name}_{p}",
            grid_spec=pltpu.PrefetchScalarGridSpec(
                num_scalar_prefetch=1, grid=(nmb, wd // bn),
                in_specs=[pl.BlockSpec((s, bm), lambda i, j, h: (0, h[0] * nmb + i)), b_spec] + [tile] * n_got
                + [ANY] * (len(prev) + len(first_after)), out_specs=tile),
            out_shape=_sds((mh, total), BF16), input_output_aliases={3 + n_got: 0} if prev else {},
            compiler_params=_params(dimension_semantics=("arbitrary", "arbitrary")),
        )(hidx, a, arr, *(() if got is None else (got,)), *prev, *first_after)
        off += wd
    return out


def _dx_pieces(name, pieces, widths, w, dr, after=()):
    s, d = dr.shape
    iw = w.shape[-1]
    bm, bn = _tile(s, 1024), _tile(d, 512)
    arrs, specs = [], []
    for piece, wd in zip(pieces, widths):
        if isinstance(piece, tuple):
            arrs.append(piece[0])
            specs.append(pl.BlockSpec((None, bm, wd), (lambda idx: lambda i, j: (idx, i, 0))(piece[1])))
        else:
            arrs.append(piece)
            specs.append(pl.BlockSpec((bm, wd), lambda i, j: (i, 0)))
    n = len(arrs)

    def body(*refs):
        w_ref, dr_ref, o_ref = refs[n], refs[n + 1], refs[n + 2]
        acc = ALPHA * dr_ref[...]
        off = 0
        for p, wd in enumerate(widths):
            acc = acc + _dot(refs[p][...], w_ref[:, off:off + wd], NT)
            off += wd
        o_ref[...] = acc

    tile = pl.BlockSpec((bm, bn), lambda i, j: (i, j))
    return _pallas(
        body, after=after, name=name, grid=(s // bm, d // bn),
        in_specs=specs + [pl.BlockSpec((bn, iw), lambda i, j: (j, 0)), tile], out_specs=tile,
        out_shape=_sds((s, d), F32), compiler_params=_params(dimension_semantics=("arbitrary", "arbitrary")),
    )(*arrs, w, dr)


class _Gather:
    def __init__(self, tag, names, fulls, after):
        self.tag, self.names = tag, names
        self.axes = [SHARD_AXIS[n] for n in names]
        self.srcs = []
        self.mk1 = _mk_gather_ici(self.axes)
        self.mk2 = _mk_gather_d2d(self.axes)
        self.n_sem = 3 * len(names)
        self.s1, self.r1, self.lands, self.token = _split_start(
            tag + "_ici_start", self.srcs, [fulls[n] for n in names], self.mk1, self.n_sem, after)

    def forward(self, after=()):
        lands = _split_wait(self.tag + "_ici_wait", self.srcs, self.lands, self.s1, self.r1, self.mk1, after)
        self.s2, self.r2, self.lands, tok = _split_start(self.tag + "_d2d_start", [], lands, self.mk2, self.n_sem)
        return tok

    def done(self, after=()):
        lands = _split_wait(self.tag + "_d2d_wait", [], self.lands, self.s2, self.r2, self.mk2, after)
        return dict(zip(self.names, lands))


class _Reduce:
    def __init__(self, tag, names, parts, mcidx, after=()):
        self.tag, self.names, self.mcidx = tag, names, mcidx
        self.axes = [SHARD_AXIS[n] for n in names]
        self.parts = [parts[n] for n in names]
        self.mk = _mk_swap
        lands = [lax.empty(p.shape, BF16) for p in self.parts]
        self.s, self.r, self.lands, self.token = _split_start(
            tag + "_swap_start", self.parts, lands, self.mk, len(names), after)

    def scatter(self, own, after=()):
        got = _split_wait(self.tag + "_swap_wait", self.parts, self.lands, self.s, self.r, self.mk, after)
        sums = own(dict(zip(self.names, got)))
        self.sums = [sums[n] for n in self.names]
        self.mk = _mk_scatter(self.axes)
        lands = []
        for q, ax in zip(self.sums, self.axes):
            k, n = q.shape
            lands.append(lax.empty((3, k // 4, n) if ax == 0 else (3, k, n // 4), BF16))
        self.s, self.r, self.lands, tok = _split_start(
            self.tag + "_scatter_start", self.sums, lands, self.mk, 3 * len(self.names))
        return tok

    def exchange(self, after=()):
        slots = _split_wait(self.tag + "_scatter_wait", self.sums, self.lands, self.s, self.r, self.mk, after)
        halves = [_sum_half(f"{self.tag}_sum_{n}", q, sl, ax, self.mcidx)
                  for n, q, sl, ax in zip(self.names, self.sums, slots, self.axes)]
        self.mk = _mk_exchange(self.axes)
        self.s, self.r, self.lands, tok = _split_start(
            self.tag + "_exchange_start", [], halves, self.mk, len(self.names))
        return tok

    def done(self, after=()):
        grads = _split_wait(self.tag + "_exchange_wait", [], self.lands, self.s, self.r, self.mk, after)
        return dict(zip(self.names, grads))


def _pack(arrs):
    flat = jnp.concatenate([a.reshape(-1) for a in arrs])
    n = flat.shape[0]
    pad = (-n) % (8 * LANE)
    return jnp.pad(flat, (0, pad)).reshape(-1, LANE)


def _unpack(packed, shapes):
    flat = packed.reshape(-1)
    out, off = [], 0
    for sh in shapes:
        n = math.prod(sh)
        out.append(flat[off:off + n].reshape(sh))
        off += n
    return out


def kernel(x, mem, w_in, b_gate, ln_v_g, ln_v_b, w_s, b_s, sinks, w_br_a, w_br_b, w_o, ln1_g, ln1_b, w_xq, w_xkv, w_xo, ln2_g, ln2_b, w_up, w_down, ln3_g, ln3_b, loss_target, m_w_in, m_b_gate, m_ln_v_g, m_ln_v_b, m_w_s, m_b_s, m_sinks, m_w_br_a, m_w_br_b, m_w_o, m_ln1_g, m_ln1_b, m_w_xq, m_w_xkv, m_w_xo, m_ln2_g, m_ln2_b, m_w_up, m_w_down, m_ln3_g, m_ln3_b, v_w_in, v_b_gate, v_ln_v_g, v_ln_v_b, v_w_s, v_b_s, v_sinks, v_w_br_a, v_w_br_b, v_w_o, v_ln1_g, v_ln1_b, v_w_xq, v_w_xkv, v_w_xo, v_ln2_g, v_ln2_b, v_w_up, v_w_down, v_ln3_g, v_ln3_b):
    env = dict(locals())
    wts = {n: env[n] for n in WEIGHTS}
    mom_m = {n: env["m_" + n] for n in WEIGHTS}
    mom_v = {n: env["v_" + n] for n in WEIGHTS}
    s, d = x.shape[1], x.shape[2]
    dff = 4 * w_up.shape[-1]
    xf = x.reshape(s, d)
    tgt = loss_target.reshape(s, d)
    memf = mem.reshape(mem.shape[1], d)
    ax_x, ax_y, ax_c = lax.axis_index("x"), lax.axis_index("y"), lax.axis_index("c")
    meidx = jnp.reshape(2 * ax_x + ax_y, (1,)).astype(jnp.int32)
    cidx = jnp.reshape(ax_c, (1,)).astype(jnp.int32)
    sidx = 1 - cidx
    mcidx = jnp.concatenate([meidx, cidx])

    inv = 1.0 / (10000.0 ** (jnp.arange(0, HD, 2, dtype=F32) / HD))
    ang = jnp.arange(s, dtype=F32)[:, None] * inv[None, :]
    cos, sin = jnp.cos(ang), jnp.sin(ang)
    cos4 = jnp.tile(cos, (1, 4))
    sin4 = jnp.concatenate([-sin, sin, -sin, sin], axis=-1)
    nsin4 = -sin4

    small = {}
    for n in SMALL:
        w = wts[n]
        if n == "w_s":
            small[n] = [w[l] for l in range(DEPTH)]
        elif n == "b_s":
            small["b_st"] = [w[l].T for l in range(DEPTH)]
        else:
            small[n] = [w[l][None, :] for l in range(DEPTH)]
    small["sink_rows"] = [jnp.repeat(sinks[l].reshape(NKV, GRP), CHUNK, axis=1)[:, None, :] for l in range(DEPTH)]

    fulls = [{}, {}]
    tok = ()
    gathers = [[None] * len(GROUPS_GATHER) for _ in range(DEPTH)]
    for gi, names in enumerate(GROUPS_GATHER):
        for n in names:
            fulls[0][n], fulls[1][n] = _place_own("place_" + n, wts[n], SHARD_AXIS[n], meidx, after=tok)
        gathers[0][gi] = _Gather(f"ag0_{gi}", names, fulls[0], tok)
        tok = (gathers[0][gi].token,)
    for gi, names in enumerate(GROUPS_GATHER):
        gathers[1][gi] = _Gather(f"ag1_{gi}", names, fulls[1], tok)
        tok = (gathers[1][gi].token,)

    xb = _cast2d("cast_x", xf, after=tok)
    memb = _cast2d("cast_mem", memf, after=tok)

    saved = []
    hf, hb = xf, xb
    nxt_tok = gathers[0][0].forward(after=tok)
    for l in range(DEPTH):
        t = f"l{l}_"
        ga, gb, gc, gd = gathers[l]
        full = ga.done(after=(nxt_tok, hb))
        sv = {"xf": hf, "xb": hb}
        proj = _mm_nn(t + "proj", hb, full["w_in"], out_dtypes=[F32], bn_pref=1280)[0]
        tok_b = gb.forward(after=(proj,))
        sg = _gmlp_fwd(t + "gmlp_fwd", proj, small["ln_v_g"][l], small["ln_v_b"][l], small["w_s"][l],
                       small["b_st"][l])
        attn, qr, kr, lse = _swa_fwd(t + "swa_fwd", proj, cos4, sin4, small["sink_rows"][l], after=(tok_b,))
        full.update(gb.done(after=(attn,)))
        merged, ya, yb = _gate_fwd(t + "gate_fwd", sg, attn, full["w_br_a"], full["w_br_b"], proj,
                                   small["b_gate"][l], d)
        tok_c = gc.forward(after=(merged,))
        r1 = _mm_residual(t + "o", merged, full["w_o"], hf, after=(tok_c,))
        x1, x1b = _ln_fwd(t + "ln1", r1, small["ln1_g"][l], small["ln1_b"][l])
        kv = _mm_nn(t + "xkv", memb, full["w_xkv"], out_dtypes=[BF16])[0]
        q, o, r2, x2, x2b = _xattn_fwd(t + "xattn_fwd", x1b, x1, full["w_xq"], kv, full["w_xo"],
                                       small["ln2_g"][l], small["ln2_b"][l])
        full.update(gc.done(after=(x2b,)))
        tok_d = gd.forward(after=(x2b,))

        def ep_up(acc, ex, outs):
            outs[0][...] = acc.astype(BF16)
            rl = jnp.maximum(acc, 0.0)
            outs[1][...] = (rl * rl).astype(BF16)

        h, a = _mm_nn(t + "up", x2b, full["w_up"], out_dtypes=[BF16, BF16], epilogue=ep_up, after=(tok_d,))
        full.update(gd.done(after=(h,)))
        nxt_tok = gathers[l + 1][0].forward(after=(h,)) if l + 1 < DEPTH else None
        r3 = _mm_residual(t + "down", a, full["w_down"], x2, after=() if nxt_tok is None else (nxt_tok,))
        x3, x3b = _ln_fwd(t + "ln3", r3, small["ln3_g"][l], small["ln3_b"][l])
        sv.update(proj=proj, sg=sg, attn=attn, qr=qr, kr=kr, lse=lse, merged=merged, ya=ya, yb=yb, r1=r1, x1=x1,
                  x1b=x1b, kv=kv, q=q, o=o, r2=r2, x2b=x2b, h=h, a=a, r3=r3, full=full)
        saved.append(sv)
        hf, hb = x3, x3b
    dy, loss11 = _loss_grad("loss", hf, tgt)
    loss = lax.psum(loss11[0, 0], ("x", "y", "c"))

    small_g = [None] * DEPTH
    grads = [{}, {}]
    pend_a = None
    pend_b = None
    g = dy
    for l in reversed(range(DEPTH)):
        t = f"l{l}_"
        sv = saved[l]
        full = sv["full"]
        sgo = {}
        dr3, dr3b, sgo["ln3_g"], sgo["ln3_b"] = _ln_bwd(t + "ln3_bwd", g, sv["r3"], small["ln3_g"][l],
                                                        after=() if pend_a is None else (tok_a,))
        bm, bn = _tile(s, 1024), _tile(dff, 1024)

        def ep_dh(acc, ex, outs):
            outs[0][...] = (acc * (2.0 * jnp.maximum(ex[0][...].astype(F32), 0.0))).astype(BF16)

        tile = pl.BlockSpec((bm, bn), lambda i, j, k: (i, j))
        dh = _mm(t + "dh", dr3b, full["w_down"], dims=NT, grid=(s // bm, dff // bn, 1),
                 a_spec=pl.BlockSpec((bm, d), lambda i, j, k: (i, 0)),
                 b_spec=pl.BlockSpec((bn, d), lambda i, j, k: (j, 0)),
                 extras=(sv["h"],), extra_specs=(tile,), out_shape=[_sds((s, dff), BF16)], out_specs=[tile],
                 epilogue=ep_dh)[0]
        if pend_a is not None:
            tok_pa = pend_a.exchange(after=(dh,))
            grads[l + 1].update(pend_b.done(after=(dh,)))

        def halves(specs, hidx, got=None, after=()):
            out = {}
            for i, (n, a_, b_) in enumerate(specs):
                out[n] = _dw_half(f"{t}d{n}_{'s' if got is None else 'o'}", a_, b_, SHARD_AXIS[n], hidx,
                                  None if got is None else got[n], after if i == 0 else ())
            return out

        specs_c = [("w_down", sv["a"], dr3b), ("w_up", sv["x2b"], dh)]
        red_c = _Reduce(t + "rs_c", GROUPS_FWD[2],
                        halves(specs_c, sidx, after=() if pend_a is None else (tok_pa,)), mcidx)
        bm2, bn2 = _tile(s, 1024), _tile(d, 512)
        tile2 = pl.BlockSpec((bm2, bn2), lambda i, j, k: (i, j))
        dx2 = _mm(t + "dx2", dh, full["w_up"], dims=NT, grid=(s // bm2, d // bn2, 1),
                  a_spec=pl.BlockSpec((bm2, dff), lambda i, j, k: (i, 0)),
                  b_spec=pl.BlockSpec((bn2, dff), lambda i, j, k: (j, 0)),
                  extras=(dr3,), extra_specs=(tile2,), out_shape=[_sds((s, d), F32)], out_specs=[tile2],
                  epilogue=_ep_add_scaled, after=(red_c.token,))[0]
        tok_c = red_c.scatter(lambda got: halves(specs_c, cidx, got), after=(dx2,))
        if pend_a is not None:
            grads[l + 1].update(pend_a.done(after=(dx2,)))
            pend_a = None

        dr2, dr2b, sgo["ln2_g"], sgo["ln2_b"] = _ln_bwd(t + "ln2_bwd", dx2, sv["r2"], small["ln2_g"][l],
                                                        after=(tok_c,))
        dx1, dq, dkv = _xattn_bwd(t + "xattn_bwd", dr2b, dr2, sv["q"], sv["kv"], full["w_xo"], full["w_xq"])

        dr1, dr1b, sgo["ln1_g"], sgo["ln1_b"] = _ln_bwd(t + "ln1_bwd", dx1, sv["r1"], small["ln1_g"][l])
        dya, dyb, dgate, dba, dbb = _gate_bwd(t + "gate_bwd", dr1b, full["w_o"], sv["proj"], sv["ya"], sv["yb"],
                                              small["b_gate"][l], d)
        sgo["b_gate"] = jnp.concatenate([dba, dbb], axis=-1)
        specs_b = [("w_o", sv["merged"], dr1b), ("w_br_a", sv["sg"], dya), ("w_br_b", sv["attn"], dyb),
                   ("w_xo", sv["o"], dr2b), ("w_xq", sv["x1b"], dq),
                   ("w_xkv", memb, _cast2d(t + "dkv_cast", dkv))]
        red_b = _Reduce(t + "rs_b", GROUPS_FWD[1], halves(specs_b, sidx), mcidx)

        def dbranch(name, dyx, w, after):
            return _mm(name, dyx, w, dims=NT, grid=(s // bm, 1, 1),
                       a_spec=pl.BlockSpec((bm, d), lambda i, j, k: (i, 0)),
                       b_spec=pl.BlockSpec((w.shape[0], d), lambda i, j, k: (0, 0)),
                       out_shape=[_sds((s, w.shape[0]), BF16)],
                       out_specs=[pl.BlockSpec((bm, w.shape[0]), lambda i, j, k: (i, 0))],
                       epilogue=_store, after=after)[0]

        dsg = dbranch(t + "dsg", dya, full["w_br_a"], (red_b.token,))
        dattn = dbranch(t + "dattn", dyb, full["w_br_b"], ())
        tok_c = red_c.exchange(after=(dattn, dsg))
        tok_b = red_b.scatter(lambda got: halves(specs_b, cidx, got), after=(dattn, dsg))
        duv, sgo["w_s"], dbst, dlg, dlb = _gmlp_bwd(t + "gmlp_bwd", sv["proj"], dsg, small["ln_v_g"][l],
                                                    small["ln_v_b"][l], small["w_s"][l], small["b_st"][l])
        sgo["b_s"] = dbst.T
        sgo["ln_v_g"], sgo["ln_v_b"] = dlg, dlb
        dqkv, sgo["sinks"] = _swa_bwd(t + "swa_bwd", sv["qr"], sv["kr"], sv["proj"], dattn, sv["attn"], sv["lse"],
                                      small["sink_rows"][l], cos4, nsin4, after=(tok_b, tok_c))
        grads[l].update(red_c.done(after=(dqkv,)))
        pieces = (duv, dqkv, (dgate, 0), (dgate, 1))
        widths = (2 * GMLP_W, ATT_W + 2 * KV_W, d, d)
        tok_b = red_b.exchange(after=(dqkv, duv))
        red_a = _Reduce(t + "rs_a", GROUPS_FWD[0],
                        {"w_in": _dw_pieces_half(t + "dw_in_s", sv["xb"], pieces, widths, sidx, after=(tok_b,))},
                        mcidx)
        g = _dx_pieces(t + "dx0", pieces, widths, full["w_in"], dr1, after=(red_a.token,))
        tok_a = red_a.scatter(
            lambda got: {"w_in": _dw_pieces_half(t + "dw_in_o", sv["xb"], pieces, widths, cidx, got["w_in"])},
            after=(g,))
        pend_a, pend_b = red_a, red_b
        small_g[l] = sgo
    grad_x = g.reshape(x.shape)

    big_out = {}

    def adam_layer(l, names, after):
        done = []
        for n in names:
            prev = big_out.get(n)
            big_out[n] = _adamw(f"adamw{l}_{n}", wts[n], grads[l][n], mom_m[n], mom_v[n], l, prev, after=after)
            done.append(big_out[n][0])
        return done

    shapes = [wts[n].shape for n in SMALL]
    packed_g = _pack([jnp.stack([small_g[l][n].reshape(wts[n].shape[1:]) for l in range(DEPTH)]) for n in SMALL])
    me8 = jnp.reshape(4 * ax_x + 2 * ax_y + ax_c, (1,)).astype(jnp.int32)
    ar_s, ar_r, ar_land, tok_ar = _split_start("ar_start", [packed_g], [_place_slot("ar_place", packed_g, me8)],
                                               _mk_small, 7, after=(tok_a,))
    fill = []
    for names in GROUPS_FWD:
        fill += adam_layer(1, names, (tok_ar,))
    grads[0].update(pend_b.done(after=tuple(fill)))
    fill += adam_layer(0, GROUPS_FWD[1], (tok_ar,))
    ar_land = _split_wait("ar_wait", [packed_g], ar_land, ar_s, ar_r, _mk_small, after=tuple(fill))
    packed_g = _sum_slots("ar_sum", ar_land[0])
    pw, pm, pv = (_pack([src[n] for n in SMALL]) for src in (wts, mom_m, mom_v))
    small4 = _adamw("adamw_small", pw[None], packed_g, pm[None], pv[None], 0)
    small_out = [dict(zip(SMALL, _unpack(a[0], shapes))) for a in small4]
    tok_a = pend_a.exchange(after=(small4[0],))
    fill = adam_layer(0, GROUPS_FWD[2], (tok_a,))
    grads[0].update(pend_a.done(after=tuple(fill)))
    adam_layer(0, GROUPS_FWD[0], ())

    def pick(kind, n):
        return big_out[n][kind] if n in big_out else small_out[kind][n]

    return (loss, grad_x, *[pick(0, n) for n in WEIGHTS], *[pick(1, n) for n in WEIGHTS],
            *[pick(2, n) for n in WEIGHTS], *[pick(3, n) for n in WEIGHTS])
```

```python
import math

import jax
import jax.numpy as jnp
from jax import lax
from jax.experimental import pallas as pl
from jax.experimental.pallas import tpu as pltpu

F32 = jnp.float32
BF16 = jnp.bfloat16
MESH = pl.DeviceIdType.MESH
ANY = pl.BlockSpec(memory_space=pl.ANY)
HBM = pl.BlockSpec(memory_space=pltpu.HBM)
SEM = pl.BlockSpec(memory_space=pltpu.SEMAPHORE)
VMEM_SPEC = pl.BlockSpec(memory_space=pltpu.VMEM)
EFFECT = pltpu.SideEffectType.DATAFLOW_SIDE_EFFECTING

DEPTH = 2
CHUNK = 128
GMLP_W = 1024
GROUPS = 8
NQ, NKV, HD = 16, 4, 64
ATT_W = NQ * HD
KV_W = NKV * HD
XH, XHD = 4, 128
X_W = XH * XHD
LN_EPS = 1e-5
ALPHA = (2 * DEPTH) ** 0.25
OFF_Q = 2 * GMLP_W
OFF_K = OFF_Q + ATT_W
OFF_VA = OFF_K + KV_W
OFF_GA = OFF_VA + KV_W
NEG = -1e30

ADAM_LR, ADAM_B1, ADAM_B2, ADAM_EPS, ADAM_WD, ADAM_STEP = 0.001, 0.9, 0.999, 1e-08, 0.01, 10

V7X_VMEM_BYTES = 64 * 1024 * 1024
VMEM_LIMIT = V7X_VMEM_BYTES - 4 * 1024 * 1024
LANE = 128

BIG = ("w_in", "w_br_a", "w_br_b", "w_o", "w_xq", "w_xkv", "w_xo", "w_up", "w_down")
SHARD_AXIS = {"w_in": 1, "w_br_a": 1, "w_br_b": 1, "w_o": 0, "w_xq": 0, "w_xkv": 0, "w_xo": 1,
              "w_up": 1, "w_down": 0}
GROUPS_GATHER = (("w_in",), ("w_br_a", "w_br_b", "w_o", "w_xq", "w_xkv", "w_xo"), ("w_up",), ("w_down",))
GROUPS_FWD = (("w_in",), ("w_br_a", "w_br_b", "w_o", "w_xq", "w_xkv", "w_xo"), ("w_up", "w_down"))
SMALL = ("b_gate", "ln_v_g", "ln_v_b", "w_s", "b_s", "sinks", "ln1_g", "ln1_b", "ln2_g", "ln2_b",
         "ln3_g", "ln3_b")
WEIGHTS = ("w_in", "b_gate", "ln_v_g", "ln_v_b", "w_s", "b_s", "sinks", "w_br_a", "w_br_b", "w_o",
           "ln1_g", "ln1_b", "w_xq", "w_xkv", "w_xo", "ln2_g", "ln2_b", "w_up", "w_down", "ln3_g", "ln3_b")


def _pallas(body, after=(), **kw):
    n_after = len(after)
    if not n_after:
        return pl.pallas_call(body, **kw)
    n_in = len(kw["in_specs"])
    kw["in_specs"] = list(kw["in_specs"]) + [ANY] * n_after

    def tied(*refs):
        return body(*refs[:n_in], *refs[n_in + n_after:])

    call = pl.pallas_call(tied, **kw)
    return lambda *ops: call(*ops, *after)


def _params(**kw):
    return pltpu.CompilerParams(vmem_limit_bytes=VMEM_LIMIT, **kw)


def _tile(dim, pref, unit=LANE):
    best = None
    t = unit
    while t <= min(dim, pref):
        if dim % t == 0:
            best = t
        t += unit
    return best if best is not None else dim


def _dot(a, b, dims):
    return lax.dot_general(a, b, (dims, ((), ())), preferred_element_type=F32)


NN = ((1,), (0,))
NT = ((1,), (1,))
TN = ((0,), (0,))


def _bf(x):
    return x if x.dtype == BF16 else x.astype(BF16)


def _sds(shape, dtype):
    return jax.ShapeDtypeStruct(shape, dtype)


def _mm(name, a, b, *, dims, grid, a_spec, b_spec, out_shape, out_specs, epilogue,
        extras=(), extra_specs=(), after=()):
    assert grid[2] == 1
    n_ex, n_out = len(extras), len(out_shape)

    def body(*refs):
        ex = refs[2:2 + n_ex]
        outs = refs[2 + n_ex:2 + n_ex + n_out]
        epilogue(_dot(_bf(refs[0][...]), _bf(refs[1][...]), dims), ex, outs)

    return _pallas(
        body, after=after, name=name, grid=grid, in_specs=[a_spec, b_spec, *extra_specs], out_specs=list(out_specs),
        out_shape=list(out_shape), compiler_params=_params(dimension_semantics=("arbitrary",) * 3),
    )(a, b, *extras)


def _store(acc, ex, outs):
    for o in outs:
        o[...] = acc.astype(o.dtype)


def _ln_rows(r, g, b):
    mu = jnp.mean(r, axis=-1, keepdims=True)
    xc = r - mu
    var = jnp.mean(xc * xc, axis=-1, keepdims=True)
    rstd = lax.rsqrt(var + LN_EPS)
    xhat = xc * rstd
    return xhat * g + b, xhat, rstd


def _ep_add_scaled(acc, ex, outs):
    outs[0][...] = acc + ALPHA * ex[0][...]


def _ln_fwd(name, r, g, b):
    s, d = r.shape
    bm = _tile(s, 512)

    def body(r_ref, g_ref, b_ref, y_ref, yb_ref):
        y, _, _ = _ln_rows(r_ref[...], g_ref[...], b_ref[...])
        y_ref[...] = y
        yb_ref[...] = y.astype(BF16)

    row = pl.BlockSpec((bm, d), lambda i: (i, 0))
    vec = pl.BlockSpec((1, d), lambda i: (0, 0))
    return _pallas(body, name=name, grid=(s // bm,), in_specs=[row, vec, vec], out_specs=[row, row],
                   out_shape=[_sds((s, d), F32), _sds((s, d), BF16)], compiler_params=_params())(r, g, b)


_GC = math.sqrt(2.0 / math.pi)


def _gelu(x):
    t = jnp.tanh(_GC * (x + 0.044715 * (x * x * x)))
    return 0.5 * x * (1.0 + t), t


def _gelu_grad(x, t):
    return 0.5 * (1.0 + t) + 0.5 * x * (1.0 - t * t) * (_GC * (1.0 + 3.0 * 0.044715 * x * x))


def _sigmoid(x):
    return 1.0 / (1.0 + jnp.exp(-x))


GRP = NQ // NKV


def _band_mask(prev_ok, prev_only=False):
    rows = CHUNK if prev_only else 2 * CHUNK
    key = lax.broadcasted_iota(jnp.int32, (rows, GRP * CHUNK), 0)
    qry = jnp.bitwise_and(lax.broadcasted_iota(jnp.int32, (rows, GRP * CHUNK), 1), CHUNK - 1)
    prev = jnp.logical_and(jnp.logical_and(key < CHUNK, key > qry), prev_ok)
    if prev_only:
        return prev
    return jnp.logical_or(prev, jnp.logical_and(key >= CHUNK, key - CHUNK <= qry))


def _pair(x, g):
    return x[:, (g // 2) * LANE:(g // 2 + 1) * LANE]


def _own_head(x, g):
    xp = _pair(x, g)
    lane = lax.broadcasted_iota(jnp.int32, xp.shape, 1)
    lo = (g % 2) * HD
    return jnp.where(jnp.logical_and(lane >= lo, lane < lo + HD), xp, jnp.zeros_like(xp))


def _stack_heads(x, g, dtype=BF16):
    a = x[:, g * GRP * HD:g * GRP * HD + LANE]
    b = x[:, g * GRP * HD + LANE:(g + 1) * GRP * HD]
    ar, br = pltpu.roll(a, HD, 1), pltpu.roll(b, HD, 1)
    parts = [a, ar, b, br] if g % 2 == 0 else [ar, a, br, b]
    return jnp.concatenate(parts, axis=0).astype(dtype)


def _unstack_heads(og, g):
    o = [og[h * CHUNK:(h + 1) * CHUNK] for h in range(GRP)]
    lo = lax.broadcasted_iota(jnp.int32, (CHUNK, LANE), 1) < HD
    if g % 2 == 0:
        x0, x1, x2, x3 = o[0], pltpu.roll(o[1], HD, 1), o[2], pltpu.roll(o[3], HD, 1)
    else:
        x0, x1, x2, x3 = pltpu.roll(o[0], HD, 1), o[1], pltpu.roll(o[2], HD, 1), o[3]
    return [jnp.where(lo, x0, x1), jnp.where(lo, x2, x3)]


def _stack_rows(x, g):
    return jnp.concatenate([x[g * GRP + h:g * GRP + h + 1] for h in range(GRP)], axis=-1)


def _head_lane_sums(x, g):
    lane = lax.broadcasted_iota(jnp.int32, (8, LANE), 1)
    lo_lane = (g % 2) * HD
    sel = jnp.where(jnp.logical_and(lane >= lo_lane, lane < lo_lane + HD), 1.0, 0.0).astype(BF16)
    hi = x.astype(BF16)
    lo = (x - hi.astype(F32)).astype(BF16)
    return (_dot(sel, hi, NT) + _dot(sel, lo, NT))[0:1]


def _rope(x, cos, sin_signed):
    w = x.shape[-1]
    lane = lax.broadcasted_iota(jnp.int32, x.shape, 1)
    first = (lane % HD) < (HD // 2)
    partner = jnp.where(first, pltpu.roll(x, w - HD // 2, 1), pltpu.roll(x, HD // 2, 1))
    reps = w // LANE
    return x * jnp.tile(cos, (1, reps)) + partner * jnp.tile(sin_signed, (1, reps))


def _cast2d(name, x, after=()):
    s, d = x.shape
    bm = _tile(s, 512, 8)

    def body(x_ref, o_ref):
        o_ref[...] = x_ref[...].astype(BF16)

    spec = pl.BlockSpec((bm, d), lambda i: (i, 0))
    return _pallas(body, after=after, name=name, grid=(s // bm,), in_specs=[spec], out_specs=spec,
                   out_shape=_sds(x.shape, BF16), compiler_params=_params())(x)


def _place():
    x, y, c = lax.axis_index("x"), lax.axis_index("y"), lax.axis_index("c")
    chips = [(1 - x, y), (x, 1 - y), (1 - x, 1 - y)]
    return x, y, c, chips


def _cut(ref, axis, chip=None, half=None):
    k, n = ref.shape[-2], ref.shape[-1]
    rows, cols = slice(None), slice(None)
    if chip is not None:
        if axis == 0:
            rows = pl.ds(pl.multiple_of(chip * (k // 4), 8), k // 4)
        else:
            cols = pl.ds(pl.multiple_of(chip * (n // 4), LANE), n // 4)
    if half is not None:
        if axis == 0:
            cols = pl.ds(pl.multiple_of(half * (n // 2), LANE), n // 2)
        else:
            rows = pl.ds(pl.multiple_of(half * (k // 2), 8), k // 2)
    return ref.at[rows, cols]


def _split_start(name, srcs, lands, make, n_sem, after=()):
    ns, nl, na = len(srcs), len(lands), len(after)

    def body(*refs):
        src, land = refs[:ns], refs[ns:ns + nl]
        outs = refs[ns + nl + na:]
        for out_cp, _ in make(src, land, outs[0], outs[1]):
            out_cp.start()
        outs[-1][...] = jnp.zeros_like(outs[-1])

    res = pl.pallas_call(
        body, name=name, in_specs=[HBM] * (ns + nl) + [ANY] * na,
        out_specs=[SEM, SEM] + [HBM] * nl + [VMEM_SPEC],
        out_shape=[pltpu.SemaphoreType.DMA((n_sem,)), pltpu.SemaphoreType.DMA((n_sem,))]
        + [pltpu.HBM(a.shape, a.dtype) for a in lands] + [_sds((8, LANE), F32)],
        input_output_aliases={ns + i: 2 + i for i in range(nl)},
        compiler_params=pltpu.CompilerParams(has_side_effects=EFFECT),
    )(*[pltpu.with_memory_space_constraint(a, pltpu.HBM) for a in (*srcs, *lands)], *after)
    return res[0], res[1], list(res[2:2 + nl]), res[-1]


def _split_wait(name, srcs, lands, ssem, rsem, make, after=()):
    ns, nl, na = len(srcs), len(lands), len(after)

    def body(*refs):
        src, land = refs[:ns], refs[ns:ns + nl]
        s_ref, r_ref = refs[ns + nl], refs[ns + nl + 1]
        pairs = make(src, land, s_ref, r_ref)
        for _, in_cp in pairs:
            in_cp.wait_recv()
        for out_cp, _ in pairs:
            out_cp.wait_send()

    res = pl.pallas_call(
        body, name=name, in_specs=[HBM] * (ns + nl) + [SEM, SEM] + [ANY] * na,
        out_specs=[HBM] * nl, out_shape=[pltpu.HBM(a.shape, a.dtype) for a in lands],
        input_output_aliases={ns + i: i for i in range(nl)},
        compiler_params=pltpu.CompilerParams(has_side_effects=EFFECT),
    )(*srcs, *lands, ssem, rsem, *after)
    return list(res)


def _rcopy(src, dst, ssem, rsem, k, dev):
    return pltpu.make_async_remote_copy(src_ref=src, dst_ref=dst, send_sem=ssem.at[k], recv_sem=rsem.at[k],
                                        device_id=dev, device_id_type=MESH)


def _mk_gather_ici(axes):
    def make(src, land, ssem, rsem):
        x, y, c, chips = _place()
        me = 2 * x + y
        pairs = []
        for w, ax in enumerate(axes):
            mine = _cut(land[w], ax, chip=me, half=c)
            for j, (px, py) in enumerate(chips):
                dev = (px, py, c)
                got = _cut(land[w], ax, chip=2 * px + py, half=c)
                pairs.append((_rcopy(mine, mine, ssem, rsem, 3 * w + j, dev),
                              _rcopy(got, got, ssem, rsem, 3 * w + j, dev)))
        return pairs
    return make


def _mk_gather_d2d(axes):
    def make(src, land, ssem, rsem):
        x, y, c, chips = _place()
        sib = (x, y, 1 - c)
        pairs = []
        for w, ax in enumerate(axes):
            for j, (px, py) in enumerate(chips):
                have = _cut(land[w], ax, chip=2 * px + py, half=c)
                want = _cut(land[w], ax, chip=2 * px + py, half=1 - c)
                pairs.append((_rcopy(have, have, ssem, rsem, 3 * w + j, sib),
                              _rcopy(want, want, ssem, rsem, 3 * w + j, sib)))
        return pairs
    return make


def _mk_swap(src, land, ssem, rsem):
    x, y, c, _ = _place()
    pairs = []
    for w in range(len(src)):
        cp = _rcopy(src[w], land[w], ssem, rsem, w, (x, y, 1 - c))
        pairs.append((cp, cp))
    return pairs


def _mk_scatter(axes):
    def make(src, land, ssem, rsem):
        x, y, c, chips = _place()
        pairs = []
        for w, ax in enumerate(axes):
            for j, (px, py) in enumerate(chips):
                cp = _rcopy(_cut(src[w], ax, chip=2 * px + py), land[w].at[j], ssem, rsem, 3 * w + j, (px, py, c))
                pairs.append((cp, cp))
        return pairs
    return make


def _mk_exchange(axes):
    def make(src, land, ssem, rsem):
        x, y, c, _ = _place()
        sib = (x, y, 1 - c)
        pairs = []
        for w, ax in enumerate(axes):
            have = _cut(land[w], ax, half=c)
            want = _cut(land[w], ax, half=1 - c)
            pairs.append((_rcopy(have, have, ssem, rsem, w, sib), _rcopy(want, want, ssem, rsem, w, sib)))
        return pairs
    return make


def _place_own(name, shard, axis, meidx, after=()):
    _, r, c = shard.shape
    full = (4 * r, c) if axis == 0 else (r, 4 * c)
    br = _tile(r, 512, 8)
    nb = r // br
    if axis == 0:
        ospec = pl.BlockSpec((br, c), lambda i, me: (me[0] * nb + i, 0))
    else:
        ospec = pl.BlockSpec((br, c), lambda i, me: (i, me[0]))
    n_after = len(after)

    def body(me_ref, s_ref, *rest):
        o0_ref, o1_ref = rest[n_after:]
        o0_ref[...] = s_ref[0].astype(BF16)
        o1_ref[...] = s_ref[1].astype(BF16)

    return pl.pallas_call(
        body, name=name,
        grid_spec=pltpu.PrefetchScalarGridSpec(
            num_scalar_prefetch=1, grid=(nb,),
            in_specs=[pl.BlockSpec((2, br, c), lambda i, me: (0, i, 0))] + [ANY] * n_after,
            out_specs=[ospec, ospec]),
        out_shape=[_sds(full, BF16)] * 2, compiler_params=_params(),
    )(meidx, shard, *after)


def _sum_half(name, own, slots, axis, mc):
    _, r, cc = slots.shape
    br = _tile(r, 256, 8)
    nb = r // br
    if axis == 0:
        own_spec = pl.BlockSpec((br, cc), lambda i, mc: (mc[0] * nb + i, 0))
        out_spec = pl.BlockSpec((br, cc), lambda i, mc: (i, mc[1]))
        shape = (r, 2 * cc)
    else:
        own_spec = pl.BlockSpec((br, cc), lambda i, mc: (i, mc[0]))
        out_spec = pl.BlockSpec((br, cc), lambda i, mc: (mc[1] * nb + i, 0))
        shape = (2 * r, cc)

    def body(mc_ref, own_ref, s_ref, o_ref):
        acc = own_ref[...].astype(F32)
        for i in range(3):
            acc = acc + s_ref[i].astype(F32)
        o_ref[...] = acc

    return pl.pallas_call(
        body, name=name,
        grid_spec=pltpu.PrefetchScalarGridSpec(
            num_scalar_prefetch=1, grid=(nb,),
            in_specs=[own_spec, pl.BlockSpec((3, br, cc), lambda i, mc: (0, i, 0))], out_specs=out_spec),
        out_shape=_sds(shape, F32), compiler_params=_params(),
    )(mc, own, slots)


def _mk_small(src, land, ssem, rsem):
    x, y, c, _ = _place()
    me = 4 * x + 2 * y + c
    pairs = []
    for k in range(1, 8):
        peer = (1 - x if k & 4 else x, 1 - y if k & 2 else y, 1 - c if k & 1 else c)
        got = land[0].at[4 * peer[0] + 2 * peer[1] + peer[2]]
        pairs.append((_rcopy(src[0], land[0].at[me], ssem, rsem, k - 1, peer),
                      _rcopy(got, got, ssem, rsem, k - 1, peer)))
    return pairs


def _place_slot(name, packed, me8):
    rows, lanes = packed.shape
    br = _tile(rows, 512, 8)

    def body(me_ref, p_ref, o_ref):
        o_ref[...] = p_ref[...]

    return pl.pallas_call(
        body, name=name,
        grid_spec=pltpu.PrefetchScalarGridSpec(
            num_scalar_prefetch=1, grid=(rows // br,),
            in_specs=[pl.BlockSpec((br, lanes), lambda i, me: (i, 0))],
            out_specs=pl.BlockSpec((None, br, lanes), lambda i, me: (me[0], i, 0))),
        out_shape=_sds((8, rows, lanes), F32), compiler_params=_params(),
    )(me8, packed)


def _sum_slots(name, slots):
    _, rows, lanes = slots.shape
    br = _tile(rows, 512, 8)

    def body(s_ref, o_ref):
        acc = s_ref[0]
        for i in range(1, 8):
            acc = acc + s_ref[i]
        o_ref[...] = acc

    return pl.pallas_call(
        body, name=name, grid=(rows // br,), in_specs=[pl.BlockSpec((8, br, lanes), lambda i: (0, i, 0))],
        out_specs=pl.BlockSpec((br, lanes), lambda i: (i, 0)), out_shape=_sds((rows, lanes), F32),
        compiler_params=_params(),
    )(slots)


def _adamw_math(w, g, m, v):
    m2 = ADAM_B1 * m + (1.0 - ADAM_B1) * g
    v2 = ADAM_B2 * v + (1.0 - ADAM_B2) * (g * g)
    m_hat = m2 / (1.0 - ADAM_B1 ** ADAM_STEP)
    v_hat = v2 / (1.0 - ADAM_B2 ** ADAM_STEP)
    delta = -ADAM_LR * (m_hat / (jnp.sqrt(v_hat) + ADAM_EPS) + ADAM_WD * w)
    return delta, m2, v2


def _adamw(name, w, g, m, v, layer, prev=None, after=()):
    _, r, c = w.shape
    br = _tile(r, 256, 8)
    n_prev = 0 if prev is None else 4

    def body(*refs):
        w_ref, g_ref, m_ref, v_ref = refs[:4]
        go_ref, d_ref, mo_ref, vo_ref = refs[4 + n_prev:]
        gg = g_ref[...]
        delta, m2, v2 = _adamw_math(w_ref[...], gg, m_ref[...], v_ref[...])
        go_ref[...] = gg
        d_ref[...] = delta
        mo_ref[...] = m2
        vo_ref[...] = v2

    spec = pl.BlockSpec((None, br, c), lambda i: (layer, i, 0))
    return _pallas(
        body, after=after, name=name, grid=(r // br,),
        in_specs=[spec, pl.BlockSpec((br, c), lambda i: (i, 0)), spec, spec] + [ANY] * n_prev,
        out_specs=[spec] * 4, out_shape=[_sds(w.shape, F32)] * 4,
        input_output_aliases={4 + i: i for i in range(n_prev)}, compiler_params=_params(),
    )(w, g, m, v, *(prev or ()))


def _gmlp_fwd(name, proj, ln_g, ln_b, w_s, b_st):
    s = proj.shape[0]

    def body(u_ref, v_ref, g_ref, b_ref, ws_ref, bst_ref, sg_ref):
        gu, _ = _gelu(u_ref[...])
        gv, _ = _gelu(v_ref[...])
        vn, _, _ = _ln_rows(gv, g_ref[...], b_ref[...])
        vn = vn.astype(BF16)
        row = lax.broadcasted_iota(jnp.int32, (CHUNK, CHUNK), 0)
        col = lax.broadcasted_iota(jnp.int32, (CHUNK, CHUNK), 1)
        tril = col <= row
        outs = []
        for g in range(GROUPS):
            sl = slice(g * LANE, (g + 1) * LANE)
            w = jnp.where(tril, ws_ref[g], 0.0).astype(BF16)
            mixed = _dot(w, vn[:, sl], NN) + bst_ref[:, g:g + 1]
            outs.append(gu[:, sl] * mixed)
        sg_ref[...] = jnp.concatenate(outs, axis=-1).astype(BF16)

    return _pallas(
        body, name=name, grid=(s // CHUNK,),
        in_specs=[pl.BlockSpec((CHUNK, GMLP_W), lambda n: (n, 0)), pl.BlockSpec((CHUNK, GMLP_W), lambda n: (n, 1)),
                  pl.BlockSpec((1, GMLP_W), lambda n: (0, 0)), pl.BlockSpec((1, GMLP_W), lambda n: (0, 0)),
                  pl.BlockSpec((GROUPS, CHUNK, CHUNK), lambda n: (0, 0, 0)),
                  pl.BlockSpec((CHUNK, GROUPS), lambda n: (0, 0))],
        out_specs=pl.BlockSpec((CHUNK, GMLP_W), lambda n: (n, 0)),
        out_shape=_sds((s, GMLP_W), BF16), compiler_params=_params(),
    )(proj, proj, ln_g, ln_b, w_s, b_st)


def _swa_fwd(name, proj, cos4, sin4, sinks, after=()):
    s = proj.shape[0]
    w = CHUNK
    scale = HD ** -0.5

    def body(q_ref, k_ref, v_ref, cos_ref, sin_ref, sink_ref, o_ref, qr_ref, kr_ref, lse_ref, kprev, vprev):
        n = pl.program_id(0)

        @pl.when(n == 0)
        def _():
            kprev[...] = jnp.zeros_like(kprev)
            vprev[...] = jnp.zeros_like(vprev)

        cos, sin = cos_ref[...], sin_ref[...]
        qr = _rope(q_ref[...], cos, sin)
        kr = _rope(k_ref[...], cos, sin).astype(BF16)
        vb = v_ref[...].astype(BF16)
        kk = jnp.concatenate([kprev[...], kr], axis=0)
        vv = jnp.concatenate([vprev[...], vb], axis=0)
        valid = _band_mask(n > 0)
        outs, lses = [], []
        for g in range(NKV):
            sc = jnp.where(valid, _dot(_own_head(kk, g), _stack_heads(qr, g), NT) * scale, NEG)
            sink = sink_ref[g]
            mx = jnp.maximum(jnp.max(sc, axis=0, keepdims=True), sink)
            p = jnp.exp(sc - mx)
            den = jnp.sum(p, axis=0, keepdims=True) + jnp.exp(sink - mx)
            og = _dot((p * (1.0 / den)).astype(BF16), _pair(vv, g), TN)
            outs.extend(_unstack_heads(og, g))
            lg = mx + jnp.log(den)
            lses.extend([lg[:, h * w:(h + 1) * w] for h in range(GRP)])
        o_ref[...] = jnp.concatenate(outs, axis=-1).astype(BF16)
        lse_ref[...] = jnp.concatenate(lses, axis=0)
        qr_ref[...] = qr.astype(BF16)
        kr_ref[...] = kr
        kprev[...] = kr
        vprev[...] = vb

    return _pallas(
        body, after=after, name=name, grid=(s // w,),
        in_specs=[pl.BlockSpec((w, ATT_W), lambda n: (n, OFF_Q // ATT_W)),
                  pl.BlockSpec((w, KV_W), lambda n: (n, OFF_K // KV_W)),
                  pl.BlockSpec((w, KV_W), lambda n: (n, OFF_VA // KV_W)),
                  pl.BlockSpec((w, LANE), lambda n: (n, 0)), pl.BlockSpec((w, LANE), lambda n: (n, 0)),
                  pl.BlockSpec((NKV, 1, GRP * w), lambda n: (0, 0, 0))],
        out_specs=[pl.BlockSpec((w, ATT_W), lambda n: (n, 0)), pl.BlockSpec((w, ATT_W), lambda n: (n, 0)),
                   pl.BlockSpec((w, KV_W), lambda n: (n, 0)), pl.BlockSpec((None, NQ, w), lambda n: (n, 0, 0))],
        out_shape=[_sds((s, ATT_W), BF16), _sds((s, ATT_W), BF16), _sds((s, KV_W), BF16),
                   _sds((s // w, NQ, w), F32)],
        scratch_shapes=[pltpu.VMEM((w, KV_W), BF16), pltpu.VMEM((w, KV_W), BF16)],
        compiler_params=_params(dimension_semantics=("arbitrary",)),
    )(proj, proj, proj, cos4, sin4, sinks)


def _gate_fwd(name, sg, attn, wa, wb, proj, b_gate, d):
    s = sg.shape[0]
    bm, bn = _tile(s, 1024), _tile(d, 512)
    off_a, off_b = OFF_GA // bn, (OFF_GA + d) // bn

    def body(sg_ref, at_ref, wa_ref, wb_ref, ga_ref, gb_ref, ba_ref, bb_ref, m_ref, ya_ref, yb_ref, sa_ref, sb_ref):
        ya = _dot(sg_ref[...], wa_ref[...], NN)
        yb = _dot(at_ref[...], wb_ref[...], NN)
        sa = _sigmoid(ga_ref[...] + ba_ref[...])
        sb = _sigmoid(gb_ref[...] + bb_ref[...])
        m_ref[...] = (sa * ya + sb * yb).astype(BF16)
        ya_ref[...] = ya.astype(BF16)
        yb_ref[...] = yb.astype(BF16)
        sa_ref[...] = sa.astype(BF16)
        sb_ref[...] = sb.astype(BF16)

    tile = pl.BlockSpec((bm, bn), lambda i, j: (i, j))
    return _pallas(
        body, name=name, grid=(s // bm, d // bn),
        in_specs=[pl.BlockSpec((bm, GMLP_W), lambda i, j: (i, 0)), pl.BlockSpec((bm, ATT_W), lambda i, j: (i, 0)),
                  pl.BlockSpec((GMLP_W, bn), lambda i, j: (0, j)), pl.BlockSpec((ATT_W, bn), lambda i, j: (0, j)),
                  pl.BlockSpec((bm, bn), lambda i, j: (i, off_a + j)),
                  pl.BlockSpec((bm, bn), lambda i, j: (i, off_b + j)),
                  pl.BlockSpec((1, bn), lambda i, j: (0, j)), pl.BlockSpec((1, bn), lambda i, j: (0, d // bn + j))],
        out_specs=[tile] * 5, out_shape=[_sds((s, d), BF16)] * 5,
        compiler_params=_params(),
    )(sg, attn, wa, wb, proj, proj, b_gate, b_gate)


def _xattn_fwd(name, xb, xf, wq, kv, wo, ln_g, ln_b, after=()):
    s, d = xf.shape
    mem = kv.shape[0]
    bm = _tile(s, 512)
    scale = XHD ** -0.5

    def body(xb_ref, xf_ref, wq_ref, kv_ref, wo_ref, g_ref, b_ref, q_out, o_out, r_out, y_out, yb_out):
        qb = _dot(xb_ref[...], wq_ref[...], NN).astype(BF16)
        kvv = kv_ref[...]
        outs = []
        for h in range(XH):
            hs = slice(h * XHD, (h + 1) * XHD)
            vs = slice(X_W + h * XHD, X_W + (h + 1) * XHD)
            sc = _dot(qb[:, hs], kvv[:, hs], NT) * scale
            mx = jnp.max(sc, axis=-1, keepdims=True)
            p = jnp.exp(sc - mx)
            p = p / jnp.sum(p, axis=-1, keepdims=True)
            outs.append(_dot(p.astype(BF16), kvv[:, vs], NN))
        ob = jnp.concatenate(outs, axis=-1).astype(BF16)
        yv = _dot(ob, wo_ref[...], NN)
        r = ALPHA * xf_ref[...] + yv
        yn, _, _ = _ln_rows(r, g_ref[...], b_ref[...])
        q_out[...] = qb
        o_out[...] = ob
        r_out[...] = r
        y_out[...] = yn
        yb_out[...] = yn.astype(BF16)

    row = lambda wd: pl.BlockSpec((bm, wd), lambda i: (i, 0))
    return _pallas(
        body, after=after, name=name, grid=(s // bm,),
        in_specs=[row(d), row(d), pl.BlockSpec((d, X_W), lambda i: (0, 0)),
                  pl.BlockSpec((mem, 2 * X_W), lambda i: (0, 0)), pl.BlockSpec((X_W, d), lambda i: (0, 0)),
                  pl.BlockSpec((1, d), lambda i: (0, 0)), pl.BlockSpec((1, d), lambda i: (0, 0))],
        out_specs=[row(X_W), row(X_W), row(d), row(d), row(d)],
        out_shape=[_sds((s, X_W), BF16), _sds((s, X_W), BF16), _sds((s, d), F32), _sds((s, d), F32),
                   _sds((s, d), BF16)],
        compiler_params=_params(),
    )(xb, xf, wq, kv, wo, ln_g, ln_b)


def _accumulate(i, refs, vals):
    @pl.when(i == 0)
    def _():
        for ref, v in zip(refs, vals):
            ref[...] = v

    @pl.when(i > 0)
    def _():
        for ref, v in zip(refs, vals):
            ref[...] += v


def _ln_bwd_rows(dyv, r, g, dr_ref, drb_ref):
    _, xhat, rstd = _ln_rows(r, g, 0.0)
    dxh = dyv * g
    m1 = jnp.mean(dxh, axis=-1, keepdims=True)
    m2 = jnp.mean(dxh * xhat, axis=-1, keepdims=True)
    dr = rstd * (dxh - m1 - xhat * m2)
    dr_ref[...] = dr
    drb_ref[...] = dr.astype(BF16)
    return jnp.sum(dyv * xhat, axis=0, keepdims=True), jnp.sum(dyv, axis=0, keepdims=True)


def _ln_bwd(name, dy, r, g, after=()):
    s, d = r.shape
    bm = _tile(s, 512)

    def body(dy_ref, r_ref, g_ref, dr_ref, drb_ref, dg_ref, db_ref):
        dg, db = _ln_bwd_rows(dy_ref[...], r_ref[...], g_ref[...], dr_ref, drb_ref)
        _accumulate(pl.program_id(0), (dg_ref, db_ref), (dg, db))

    row = pl.BlockSpec((bm, d), lambda i: (i, 0))
    vec = pl.BlockSpec((1, d), lambda i: (0, 0))
    return _pallas(
        body, after=after, name=name, grid=(s // bm,), in_specs=[row, row, vec], out_specs=[row, row, vec, vec],
        out_shape=[_sds((s, d), F32), _sds((s, d), BF16), _sds((1, d), F32), _sds((1, d), F32)],
        compiler_params=_params(dimension_semantics=("arbitrary",)),
    )(dy, r, g)


def _loss_ln_bwd(name, r, g, b, tgt):
    s, d = r.shape
    bm = _tile(s, 512)

    def body(r_ref, g_ref, b_ref, t_ref, dr_ref, drb_ref, dg_ref, db_ref, loss_ref):
        rv, gv = r_ref[...], g_ref[...]
        y, _, _ = _ln_rows(rv, gv, b_ref[...])
        err = y - t_ref[...]
        part = 0.5 * jnp.sum(jnp.sum(err * err, axis=-1, keepdims=True), axis=0, keepdims=True) * (1.0 / d)
        dg, db = _ln_bwd_rows(err * (1.0 / d), rv, gv, dr_ref, drb_ref)
        _accumulate(pl.program_id(0), (dg_ref, db_ref, loss_ref), (dg, db, part))

    row = pl.BlockSpec((bm, d), lambda i: (i, 0))
    vec = pl.BlockSpec((1, d), lambda i: (0, 0))
    return _pallas(
        body, name=name, grid=(s // bm,), in_specs=[row, vec, vec, row],
        out_specs=[row, row, vec, vec, pl.BlockSpec((1, 1), lambda i: (0, 0))],
        out_shape=[_sds((s, d), F32), _sds((s, d), BF16), _sds((1, d), F32), _sds((1, d), F32), _sds((1, 1), F32)],
        compiler_params=_params(dimension_semantics=("arbitrary",)),
    )(r, g, b, tgt)


def _xattn_bwd(name, dyb, drf, q, kv, wo, wq):
    s, d = drf.shape
    mem = kv.shape[0]
    bm = _tile(s, 512)
    scale = XHD ** -0.5

    def body(dy_ref, dr_ref, q_ref, kv_ref, wo_ref, wq_ref, dx_out, dq_out, dkv_out):
        i = pl.program_id(0)
        dob = _dot(dy_ref[...], wo_ref[...], NT).astype(BF16)
        qb = q_ref[...]
        kvv = kv_ref[...]
        dqs, dks, dvs = [], [], []
        for h in range(XH):
            hs = slice(h * XHD, (h + 1) * XHD)
            vs = slice(X_W + h * XHD, X_W + (h + 1) * XHD)
            sc = _dot(qb[:, hs], kvv[:, hs], NT) * scale
            mx = jnp.max(sc, axis=-1, keepdims=True)
            p = jnp.exp(sc - mx)
            p = p / jnp.sum(p, axis=-1, keepdims=True)
            dp = _dot(dob[:, hs], kvv[:, vs], NT)
            dsum = jnp.sum(p * dp, axis=-1, keepdims=True)
            dsb = (p * (dp - dsum) * scale).astype(BF16)
            dqs.append(_dot(dsb, kvv[:, hs], NN))
            dks.append(_dot(dsb, qb[:, hs], TN))
            dvs.append(_dot(p.astype(BF16), dob[:, hs], TN))
        dqb = jnp.concatenate(dqs, axis=-1).astype(BF16)
        dq_out[...] = dqb
        dx_out[...] = _dot(dqb, wq_ref[...], NT) + ALPHA * dr_ref[...]
        dkv = jnp.concatenate(dks + dvs, axis=-1)

        @pl.when(i == 0)
        def _():
            dkv_out[...] = dkv

        @pl.when(i > 0)
        def _():
            dkv_out[...] += dkv

    row = lambda wd: pl.BlockSpec((bm, wd), lambda i: (i, 0))
    return _pallas(
        body, name=name, grid=(s // bm,),
        in_specs=[row(d), row(d), row(X_W), pl.BlockSpec((mem, 2 * X_W), lambda i: (0, 0)),
                  pl.BlockSpec((X_W, d), lambda i: (0, 0)), pl.BlockSpec((d, X_W), lambda i: (0, 0))],
        out_specs=[row(d), row(X_W), pl.BlockSpec((mem, 2 * X_W), lambda i: (0, 0))],
        out_shape=[_sds((s, d), F32), _sds((s, X_W), BF16), _sds((mem, 2 * X_W), F32)],
        compiler_params=_params(dimension_semantics=("arbitrary",)),
    )(dyb, drf, q, kv, wo, wq)


def _gate_bwd(name, dr1b, w_o, sa, sb, ya, yb, d, after=()):
    s = dr1b.shape[0]
    bm, bn = _tile(s, 1024), _tile(d, 512)
    nj = d // bn

    def body(a_ref, w_ref, sa_ref, sb_ref, ya_ref, yb_ref, dya_ref, dyb_ref, dg_ref, dba_ref, dbb_ref):
        i = pl.program_id(1)
        dm = _dot(a_ref[...], w_ref[...], NT)
        sa = sa_ref[...].astype(F32)
        sb = sb_ref[...].astype(F32)
        dya_ref[...] = (dm * sa).astype(BF16)
        dyb_ref[...] = (dm * sb).astype(BF16)
        dga = dm * ya_ref[...].astype(F32) * (sa * (1.0 - sa))
        dgb = dm * yb_ref[...].astype(F32) * (sb * (1.0 - sb))
        dg_ref[0] = dga.astype(BF16)
        dg_ref[1] = dgb.astype(BF16)
        sa_sum = jnp.sum(dga, axis=0, keepdims=True)
        sb_sum = jnp.sum(dgb, axis=0, keepdims=True)

        @pl.when(i == 0)
        def _():
            dba_ref[...] = sa_sum
            dbb_ref[...] = sb_sum

        @pl.when(i > 0)
        def _():
            dba_ref[...] += sa_sum
            dbb_ref[...] += sb_sum

    tile = pl.BlockSpec((bm, bn), lambda j, i: (i, j))
    return _pallas(
        body, after=after, name=name, grid=(nj, s // bm),
        in_specs=[pl.BlockSpec((bm, d), lambda j, i: (i, 0)), pl.BlockSpec((bn, d), lambda j, i: (j, 0)),
                  tile, tile, tile, tile],
        out_specs=[tile, tile, pl.BlockSpec((2, bm, bn), lambda j, i: (0, i, j)),
                   pl.BlockSpec((1, bn), lambda j, i: (0, j)), pl.BlockSpec((1, bn), lambda j, i: (0, j))],
        out_shape=[_sds((s, d), BF16), _sds((s, d), BF16), _sds((2, s, d), BF16), _sds((1, d), F32),
                   _sds((1, d), F32)],
        compiler_params=_params(dimension_semantics=("arbitrary", "arbitrary")),
    )(dr1b, w_o, sa, sb, ya, yb)


def _gmlp_bwd(name, proj, dsg, ln_g, ln_b, w_s, b_st):
    s = proj.shape[0]

    def body(u_ref, v_ref, dsg_ref, g_ref, b_ref, ws_ref, bst_ref, duv_ref, dws_ref, dbst_ref, dlg_ref, dlb_ref):
        n = pl.program_id(0)
        u, v = u_ref[...], v_ref[...]
        gu, tu = _gelu(u)
        gv, tv = _gelu(v)
        gam = g_ref[...]
        vn, xhat, rstd = _ln_rows(gv, gam, b_ref[...])
        vnb = vn.astype(BF16)
        dsg = dsg_ref[...].astype(F32)
        row = lax.broadcasted_iota(jnp.int32, (CHUNK, CHUNK), 0)
        col = lax.broadcasted_iota(jnp.int32, (CHUNK, CHUNK), 1)
        tril = col <= row
        dgu, dvn, dws, dbs = [], [], [], []
        for g in range(GROUPS):
            sl = slice(g * LANE, (g + 1) * LANE)
            w = jnp.where(tril, ws_ref[g], 0.0).astype(BF16)
            mixed = _dot(w, vnb[:, sl], NN) + bst_ref[:, g:g + 1]
            dgu.append(dsg[:, sl] * mixed)
            dmx = dsg[:, sl] * gu[:, sl]
            dmxb = dmx.astype(BF16)
            dbs.append(jnp.sum(dmx, axis=-1, keepdims=True))
            dws.append(jnp.where(tril, _dot(dmxb, vnb[:, sl], NT), 0.0))
            dvn.append(_dot(w, dmxb, TN))
        dvn = jnp.concatenate(dvn, axis=-1)
        dgu = jnp.concatenate(dgu, axis=-1)
        dxh = dvn * gam
        m1 = jnp.mean(dxh, axis=-1, keepdims=True)
        m2 = jnp.mean(dxh * xhat, axis=-1, keepdims=True)
        dgv = rstd * (dxh - m1 - xhat * m2)
        du = dgu * _gelu_grad(u, tu)
        dv = dgv * _gelu_grad(v, tv)
        duv_ref[...] = jnp.concatenate([du, dv], axis=-1).astype(BF16)
        dlg = jnp.sum(dvn * xhat, axis=0, keepdims=True)
        dlb = jnp.sum(dvn, axis=0, keepdims=True)
        dbst = jnp.concatenate(dbs, axis=-1)

        @pl.when(n == 0)
        def _():
            for g in range(GROUPS):
                dws_ref[g] = dws[g]
            dbst_ref[...] = dbst
            dlg_ref[...] = dlg
            dlb_ref[...] = dlb

        @pl.when(n > 0)
        def _():
            for g in range(GROUPS):
                dws_ref[g] += dws[g]
            dbst_ref[...] += dbst
            dlg_ref[...] += dlg
            dlb_ref[...] += dlb

    vec = pl.BlockSpec((1, GMLP_W), lambda n: (0, 0))
    return _pallas(
        body, name=name, grid=(s // CHUNK,),
        in_specs=[pl.BlockSpec((CHUNK, GMLP_W), lambda n: (n, 0)), pl.BlockSpec((CHUNK, GMLP_W), lambda n: (n, 1)),
                  pl.BlockSpec((CHUNK, GMLP_W), lambda n: (n, 0)), vec, vec,
                  pl.BlockSpec((GROUPS, CHUNK, CHUNK), lambda n: (0, 0, 0)),
                  pl.BlockSpec((CHUNK, GROUPS), lambda n: (0, 0))],
        out_specs=[pl.BlockSpec((CHUNK, 2 * GMLP_W), lambda n: (n, 0)),
                   pl.BlockSpec((GROUPS, CHUNK, CHUNK), lambda n: (0, 0, 0)),
                   pl.BlockSpec((CHUNK, GROUPS), lambda n: (0, 0)), vec, vec],
        out_shape=[_sds((s, 2 * GMLP_W), BF16), _sds((GROUPS, CHUNK, CHUNK), F32), _sds((CHUNK, GROUPS), F32),
                   _sds((1, GMLP_W), F32), _sds((1, GMLP_W), F32)],
        compiler_params=_params(dimension_semantics=("arbitrary",)),
    )(proj, proj, dsg, ln_g, ln_b, w_s, b_st)


def _swa_bwd(name, qr, kr, proj, do, o, lse, sinks, cos4, nsin4, after=()):
    s = qr.shape[0]
    w = CHUNK
    nblk = s // w
    scale = HD ** -0.5
    grp = NQ // NKV

    def body(qj_ref, qn_ref, kj_ref, kp_ref, vj_ref, vp_ref, doj_ref, don_ref, oj_ref, on_ref, lj_ref, ln_ref,
             sink_ref, cos_ref, sin_ref, out_ref, dsink_ref):
        j = pl.program_id(0)
        qj, qn = qj_ref[...].astype(F32), qn_ref[...].astype(F32)
        doj, don = doj_ref[...].astype(F32), don_ref[...].astype(F32)
        kk = jnp.concatenate([kp_ref[...], kj_ref[...]], axis=0)
        vv = jnp.concatenate([vp_ref[...], vj_ref[...]], axis=0).astype(BF16)
        lj, lnx = lj_ref[...], ln_ref[...]
        prod_j = doj * oj_ref[...].astype(F32)
        prod_n = don * on_ref[...].astype(F32)
        valid_j = _band_mask(j > 0)
        valid_n = _band_mask(j + 1 < nblk, prev_only=True)
        lo = lax.broadcasted_iota(jnp.int32, (w, LANE), 1) < HD
        dqs, dsk, dk_g, dv_g = [], [], [], []
        for g in range(NKV):
            kz, vz = _own_head(kk, g), _own_head(vv, g)
            kz_c, vz_c = kz[w:], vz[w:]
            qg_j, qg_n = _stack_heads(qj, g), _stack_heads(qn, g)
            dog_j, dog_n = _stack_heads(doj, g), _stack_heads(don, g)
            l_j, l_n = _stack_rows(lj, g), _stack_rows(lnx, g)
            d_j = _head_lane_sums(_stack_heads(prod_j, g, F32), g)
            d_n = _head_lane_sums(_stack_heads(prod_n, g, F32), g)
            p = jnp.where(valid_j, jnp.exp(_dot(kz, qg_j, NT) * scale - l_j), 0.0)
            ds = (p * (_dot(vz, dog_j, NT) - d_j) * scale).astype(BF16)
            dqs.extend(_unstack_heads(_dot(ds, kz, TN), g))
            p2 = jnp.where(valid_n, jnp.exp(_dot(kz_c, qg_n, NT) * scale - l_n), 0.0)
            ds2 = (p2 * (_dot(vz_c, dog_n, NT) - d_n) * scale).astype(BF16)
            dk_g.append(_dot(ds[w:], qg_j, NN) + _dot(ds2, qg_n, NN))
            dv_g.append(_dot(p[w:].astype(BF16), dog_j, NN) + _dot(p2.astype(BF16), dog_n, NN))
            t = jnp.exp(sink_ref[g] - l_j) * d_j
            dsk.extend([-jnp.sum(t[:, h * w:(h + 1) * w], axis=-1, keepdims=True) for h in range(GRP)])
        cos, nsin = cos_ref[...], sin_ref[...]
        dq = _rope(jnp.concatenate(dqs, axis=-1), cos, nsin)
        dk = _rope(jnp.concatenate([jnp.where(lo, dk_g[2 * m], dk_g[2 * m + 1]) for m in range(NKV // 2)], axis=-1),
                   cos, nsin)
        dv = jnp.concatenate([jnp.where(lo, dv_g[2 * m], dv_g[2 * m + 1]) for m in range(NKV // 2)], axis=-1)
        out_ref[...] = jnp.concatenate([dq, dk, dv], axis=-1).astype(BF16)
        dsink = jnp.concatenate(dsk, axis=-1)

        @pl.when(j == 0)
        def _():
            dsink_ref[...] = dsink

        @pl.when(j > 0)
        def _():
            dsink_ref[...] += dsink

    nxt = lambda j: jnp.minimum(j + 1, nblk - 1)
    prv = lambda j: jnp.maximum(j - 1, 0)
    va = OFF_VA // KV_W
    return _pallas(
        body, after=after, name=name, grid=(nblk,),
        in_specs=[pl.BlockSpec((w, ATT_W), lambda j: (j, 0)), pl.BlockSpec((w, ATT_W), lambda j: (nxt(j), 0)),
                  pl.BlockSpec((w, KV_W), lambda j: (j, 0)), pl.BlockSpec((w, KV_W), lambda j: (prv(j), 0)),
                  pl.BlockSpec((w, KV_W), lambda j: (j, va)), pl.BlockSpec((w, KV_W), lambda j: (prv(j), va)),
                  pl.BlockSpec((w, ATT_W), lambda j: (j, 0)), pl.BlockSpec((w, ATT_W), lambda j: (nxt(j), 0)),
                  pl.BlockSpec((w, ATT_W), lambda j: (j, 0)), pl.BlockSpec((w, ATT_W), lambda j: (nxt(j), 0)),
                  pl.BlockSpec((None, NQ, w), lambda j: (j, 0, 0)),
                  pl.BlockSpec((None, NQ, w), lambda j: (nxt(j), 0, 0)),
                  pl.BlockSpec((NKV, 1, GRP * w), lambda j: (0, 0, 0)),
                  pl.BlockSpec((w, LANE), lambda j: (j, 0)), pl.BlockSpec((w, LANE), lambda j: (j, 0))],
        out_specs=[pl.BlockSpec((w, ATT_W + 2 * KV_W), lambda j: (j, 0)), pl.BlockSpec((1, NQ), lambda j: (0, 0))],
        out_shape=[_sds((s, ATT_W + 2 * KV_W), BF16), _sds((1, NQ), F32)],
        compiler_params=_params(dimension_semantics=("arbitrary",)),
    )(qr, qr, kr, kr, proj, proj, do, do, o, o, lse, lse, sinks, cos4, nsin4)


def _mm_nn(name, a, w, *, out_dtypes, epilogue=_store, bm_pref=1024, bn_pref=1024, after=()):
    m, k = a.shape
    n = w.shape[-1]
    bm, bn = _tile(m, bm_pref), _tile(n, bn_pref)
    tile = pl.BlockSpec((bm, bn), lambda i, j, kk: (i, j))
    return _mm(name, a, w, dims=NN, grid=(m // bm, n // bn, 1),
               a_spec=pl.BlockSpec((bm, k), lambda i, j, kk: (i, 0)),
               b_spec=pl.BlockSpec((k, bn), lambda i, j, kk: (0, j)),
               out_shape=[_sds((m, n), dt) for dt in out_dtypes], out_specs=[tile] * len(out_dtypes),
               epilogue=epilogue, after=after)


def _dw_half(name, a, b, axis, hidx, got=None, after=()):
    s, m = a.shape
    n = b.shape[-1]
    mh, nh = (m // 2, n) if axis == 1 else (m, n // 2)
    bm, bn = _tile(mh, 1024), _tile(nh, 1024)
    nmb, nnb = mh // bm, nh // bn
    if axis == 1:
        a_spec = pl.BlockSpec((s, bm), lambda i, j, h: (0, h[0] * nmb + i))
        b_spec = pl.BlockSpec((s, bn), lambda i, j, h: (0, j))
    else:
        a_spec = pl.BlockSpec((s, bm), lambda i, j, h: (0, i))
        b_spec = pl.BlockSpec((s, bn), lambda i, j, h: (0, h[0] * nnb + j))
    tile = pl.BlockSpec((bm, bn), lambda i, j, h: (i, j))
    n_got, n_after = (0 if got is None else 1), len(after)

    def body(h_ref, a_ref, b_ref, *rest):
        acc = _dot(a_ref[...], b_ref[...], TN)
        if n_got:
            acc = acc + rest[0][...].astype(F32)
        rest[-1][...] = acc.astype(BF16)

    return pl.pallas_call(
        body, name=name,
        grid_spec=pltpu.PrefetchScalarGridSpec(
            num_scalar_prefetch=1, grid=(nmb, nnb),
            in_specs=[a_spec, b_spec] + [tile] * n_got + [ANY] * n_after, out_specs=tile),
        out_shape=_sds((mh, nh), BF16), compiler_params=_params(dimension_semantics=("arbitrary", "arbitrary")),
    )(hidx, a, b, *(() if got is None else (got,)), *after)


def _mm_residual(name, a, w, x, after=()):
    s, k = a.shape
    d = w.shape[-1]
    bm, bn = _tile(s, 1024), _tile(d, 1024 if k <= 2048 else 512)
    tile = pl.BlockSpec((bm, bn), lambda i, j, kk: (i, j))
    return _mm(name, a, w, dims=NN, grid=(s // bm, d // bn, 1),
               a_spec=pl.BlockSpec((bm, k), lambda i, j, kk: (i, 0)),
               b_spec=pl.BlockSpec((k, bn), lambda i, j, kk: (0, j)),
               extras=(x,), extra_specs=(tile,), out_shape=[_sds((s, d), F32)], out_specs=[tile],
               epilogue=_ep_add_scaled, after=after)[0]


def _dw_pieces_half(name, a, pieces, widths, hidx, got=None, after=()):
    s, m = a.shape
    total = sum(widths)
    mh = m // 2
    bm, bn = _tile(mh, 1024), 512
    nmb = mh // bm
    n_got = 0 if got is None else 1
    out, off = None, 0
    for p, (piece, wd) in enumerate(zip(pieces, widths)):
        if isinstance(piece, tuple):
            arr = piece[0]
            b_spec = pl.BlockSpec((None, s, bn), (lambda ix: lambda i, j, h: (ix, 0, j))(piece[1]))
        else:
            arr, b_spec = piece, pl.BlockSpec((s, bn), lambda i, j, h: (0, j))
        tile = pl.BlockSpec((bm, bn), (lambda c: lambda i, j, h: (i, c + j))(off // bn))
        prev = () if out is None else (out,)
        first_after = after if out is None else ()

        def body(h_ref, a_ref, b_ref, *rest):
            acc = _dot(a_ref[...], b_ref[...], TN)
            if n_got:
                acc = acc + rest[0][...].astype(F32)
            rest[-1][...] = acc.astype(BF16)

        out = pl.pallas_call(
            body, name=f"{name}_{p}",
            grid_spec=pltpu.PrefetchScalarGridSpec(
                num_scalar_prefetch=1, grid=(nmb, wd // bn),
                in_specs=[pl.BlockSpec((s, bm), lambda i, j, h: (0, h[0] * nmb + i)), b_spec] + [tile] * n_got
                + [ANY] * (len(prev) + len(first_after)), out_specs=tile),
            out_shape=_sds((mh, total), BF16), input_output_aliases={3 + n_got: 0} if prev else {},
            compiler_params=_params(dimension_semantics=("arbitrary", "arbitrary")),
        )(hidx, a, arr, *(() if got is None else (got,)), *prev, *first_after)
        off += wd
    return out


def _dx_pieces(name, pieces, widths, w, dr, after=()):
    s, d = dr.shape
    iw = w.shape[-1]
    bm, bn = _tile(s, 1024), _tile(d, 512)
    arrs, specs = [], []
    for piece, wd in zip(pieces, widths):
        if isinstance(piece, tuple):
            arrs.append(piece[0])
            specs.append(pl.BlockSpec((None, bm, wd), (lambda idx: lambda i, j: (idx, i, 0))(piece[1])))
        else:
            arrs.append(piece)
            specs.append(pl.BlockSpec((bm, wd), lambda i, j: (i, 0)))
    n = len(arrs)

    def body(*refs):
        w_ref, dr_ref, o_ref = refs[n], refs[n + 1], refs[n + 2]
        acc = ALPHA * dr_ref[...]
        off = 0
        for p, wd in enumerate(widths):
            acc = acc + _dot(refs[p][...], w_ref[:, off:off + wd], NT)
            off += wd
        o_ref[...] = acc

    tile = pl.BlockSpec((bm, bn), lambda i, j: (i, j))
    return _pallas(
        body, after=after, name=name, grid=(s // bm, d // bn),
        in_specs=specs + [pl.BlockSpec((bn, iw), lambda i, j: (j, 0)), tile], out_specs=tile,
        out_shape=_sds((s, d), F32), compiler_params=_params(dimension_semantics=("arbitrary", "arbitrary")),
    )(*arrs, w, dr)


class _Gather:
    def __init__(self, tag, names, fulls, after):
        self.tag, self.names = tag, names
        self.axes = [SHARD_AXIS[n] for n in names]
        self.srcs = []
        self.mk1 = _mk_gather_ici(self.axes)
        self.mk2 = _mk_gather_d2d(self.axes)
        self.n_sem = 3 * len(names)
        self.s1, self.r1, self.lands, self.token = _split_start(
            tag + "_ici_start", self.srcs, [fulls[n] for n in names], self.mk1, self.n_sem, after)

    def forward(self, after=()):
        lands = _split_wait(self.tag + "_ici_wait", self.srcs, self.lands, self.s1, self.r1, self.mk1, after)
        self.s2, self.r2, self.lands, tok = _split_start(self.tag + "_d2d_start", [], lands, self.mk2, self.n_sem)
        return tok

    def done(self, after=()):
        lands = _split_wait(self.tag + "_d2d_wait", [], self.lands, self.s2, self.r2, self.mk2, after)
        return dict(zip(self.names, lands))


class _Reduce:
    def __init__(self, tag, names, parts, mcidx, after=()):
        self.tag, self.names, self.mcidx = tag, names, mcidx
        self.axes = [SHARD_AXIS[n] for n in names]
        self.parts = [parts[n] for n in names]
        self.mk = _mk_swap
        lands = [lax.empty(p.shape, BF16) for p in self.parts]
        self.s, self.r, self.lands, self.token = _split_start(
            tag + "_swap_start", self.parts, lands, self.mk, len(names), after)

    def scatter(self, own, after=()):
        got = _split_wait(self.tag + "_swap_wait", self.parts, self.lands, self.s, self.r, self.mk, after)
        sums = own(dict(zip(self.names, got)))
        self.sums = [sums[n] for n in self.names]
        self.mk = _mk_scatter(self.axes)
        lands = []
        for q, ax in zip(self.sums, self.axes):
            k, n = q.shape
            lands.append(lax.empty((3, k // 4, n) if ax == 0 else (3, k, n // 4), BF16))
        self.s, self.r, self.lands, tok = _split_start(
            self.tag + "_scatter_start", self.sums, lands, self.mk, 3 * len(self.names))
        return tok

    def exchange(self, after=()):
        slots = _split_wait(self.tag + "_scatter_wait", self.sums, self.lands, self.s, self.r, self.mk, after)
        halves = [_sum_half(f"{self.tag}_sum_{n}", q, sl, ax, self.mcidx)
                  for n, q, sl, ax in zip(self.names, self.sums, slots, self.axes)]
        self.mk = _mk_exchange(self.axes)
        self.s, self.r, self.lands, tok = _split_start(
            self.tag + "_exchange_start", [], halves, self.mk, len(self.names))
        return tok

    def done(self, after=()):
        grads = _split_wait(self.tag + "_exchange_wait", [], self.lands, self.s, self.r, self.mk, after)
        return dict(zip(self.names, grads))


def _pack(arrs):
    flat = jnp.concatenate([a.reshape(-1) for a in arrs])
    n = flat.shape[0]
    pad = (-n) % (8 * LANE)
    return jnp.pad(flat, (0, pad)).reshape(-1, LANE)


def _unpack(packed, shapes):
    flat = packed.reshape(-1)
    out, off = [], 0
    for sh in shapes:
        n = math.prod(sh)
        out.append(flat[off:off + n].reshape(sh))
        off += n
    return out


def kernel(x, mem, w_in, b_gate, ln_v_g, ln_v_b, w_s, b_s, sinks, w_br_a, w_br_b, w_o, ln1_g, ln1_b, w_xq, w_xkv, w_xo, ln2_g, ln2_b, w_up, w_down, ln3_g, ln3_b, loss_target, m_w_in, m_b_gate, m_ln_v_g, m_ln_v_b, m_w_s, m_b_s, m_sinks, m_w_br_a, m_w_br_b, m_w_o, m_ln1_g, m_ln1_b, m_w_xq, m_w_xkv, m_w_xo, m_ln2_g, m_ln2_b, m_w_up, m_w_down, m_ln3_g, m_ln3_b, v_w_in, v_b_gate, v_ln_v_g, v_ln_v_b, v_w_s, v_b_s, v_sinks, v_w_br_a, v_w_br_b, v_w_o, v_ln1_g, v_ln1_b, v_w_xq, v_w_xkv, v_w_xo, v_ln2_g, v_ln2_b, v_w_up, v_w_down, v_ln3_g, v_ln3_b):
    env = dict(locals())
    wts = {n: env[n] for n in WEIGHTS}
    mom_m = {n: env["m_" + n] for n in WEIGHTS}
    mom_v = {n: env["v_" + n] for n in WEIGHTS}
    s, d = x.shape[1], x.shape[2]
    dff = 4 * w_up.shape[-1]
    xf = x.reshape(s, d)
    tgt = loss_target.reshape(s, d)
    memf = mem.reshape(mem.shape[1], d)
    ax_x, ax_y, ax_c = lax.axis_index("x"), lax.axis_index("y"), lax.axis_index("c")
    meidx = jnp.reshape(2 * ax_x + ax_y, (1,)).astype(jnp.int32)
    cidx = jnp.reshape(ax_c, (1,)).astype(jnp.int32)
    sidx = 1 - cidx
    mcidx = jnp.concatenate([meidx, cidx])

    inv = 1.0 / (10000.0 ** (jnp.arange(0, HD, 2, dtype=F32) / HD))
    ang = jnp.arange(s, dtype=F32)[:, None] * inv[None, :]
    cos, sin = jnp.cos(ang), jnp.sin(ang)
    cos4 = jnp.tile(cos, (1, 4))
    sin4 = jnp.concatenate([-sin, sin, -sin, sin], axis=-1)
    nsin4 = -sin4

    small = {}
    for n in SMALL:
        w = wts[n]
        if n == "w_s":
            small[n] = [w[l] for l in range(DEPTH)]
        elif n == "b_s":
            small["b_st"] = [w[l].T for l in range(DEPTH)]
        else:
            small[n] = [w[l][None, :] for l in range(DEPTH)]
    small["sink_rows"] = [jnp.repeat(sinks[l].reshape(NKV, GRP), CHUNK, axis=1)[:, None, :] for l in range(DEPTH)]

    fulls = [{}, {}]
    tok = ()
    gathers = [[None] * len(GROUPS_GATHER) for _ in range(DEPTH)]
    for gi, names in enumerate(GROUPS_GATHER):
        for n in names:
            fulls[0][n], fulls[1][n] = _place_own("place_" + n, wts[n], SHARD_AXIS[n], meidx, after=tok)
        gathers[0][gi] = _Gather(f"ag0_{gi}", names, fulls[0], tok)
        tok = (gathers[0][gi].token,)
    for gi, names in enumerate(GROUPS_GATHER):
        gathers[1][gi] = _Gather(f"ag1_{gi}", names, fulls[1], tok)
        tok = (gathers[1][gi].token,)

    xb = _cast2d("cast_x", xf, after=tok)
    memb = _cast2d("cast_mem", memf, after=tok)

    saved = []
    hf, hb = xf, xb
    nxt_tok = gathers[0][0].forward(after=tok)
    for l in range(DEPTH):
        t = f"l{l}_"
        ga, gb, gc, gd = gathers[l]
        full = ga.done(after=(nxt_tok, hb))
        sv = {"xf": hf, "xb": hb}
        proj = _mm_nn(t + "proj", hb, full["w_in"], out_dtypes=[F32], bn_pref=1280)[0]
        tok_b = gb.forward(after=(proj,))
        sg = _gmlp_fwd(t + "gmlp_fwd", proj, small["ln_v_g"][l], small["ln_v_b"][l], small["w_s"][l],
                       small["b_st"][l])
        attn, qr, kr, lse = _swa_fwd(t + "swa_fwd", proj, cos4, sin4, small["sink_rows"][l], after=(tok_b,))
        full.update(gb.done(after=(attn,)))
        merged, ya, yb, sa, sb = _gate_fwd(t + "gate_fwd", sg, attn, full["w_br_a"], full["w_br_b"], proj,
                                           small["b_gate"][l], d)
        sv.update(sa=sa, sb=sb)
        tok_c = gc.forward(after=(merged,))
        r1 = _mm_residual(t + "o", merged, full["w_o"], hf, after=(tok_c,))
        x1, x1b = _ln_fwd(t + "ln1", r1, small["ln1_g"][l], small["ln1_b"][l])
        kv = _mm_nn(t + "xkv", memb, full["w_xkv"], out_dtypes=[BF16])[0]
        q, o, r2, x2, x2b = _xattn_fwd(t + "xattn_fwd", x1b, x1, full["w_xq"], kv, full["w_xo"],
                                       small["ln2_g"][l], small["ln2_b"][l])
        full.update(gc.done(after=(x2b,)))
        tok_d = gd.forward(after=(x2b,))

        def ep_up(acc, ex, outs):
            outs[0][...] = acc.astype(BF16)
            rl = jnp.maximum(acc, 0.0)
            outs[1][...] = (rl * rl).astype(BF16)

        h, a = _mm_nn(t + "up", x2b, full["w_up"], out_dtypes=[BF16, BF16], epilogue=ep_up, after=(tok_d,))
        full.update(gd.done(after=(h,)))
        nxt_tok = gathers[l + 1][0].forward(after=(h,)) if l + 1 < DEPTH else None
        r3 = _mm_residual(t + "down", a, full["w_down"], x2, after=() if nxt_tok is None else (nxt_tok,))
        sv.update(proj=proj, sg=sg, attn=attn, qr=qr, kr=kr, lse=lse, merged=merged, ya=ya, yb=yb, r1=r1, x1=x1,
                  x1b=x1b, kv=kv, q=q, o=o, r2=r2, x2b=x2b, h=h, a=a, r3=r3, full=full)
        saved.append(sv)
        if l + 1 < DEPTH:
            hf, hb = _ln_fwd(t + "ln3", r3, small["ln3_g"][l], small["ln3_b"][l])

    small_g = [None] * DEPTH
    grads = [{}, {}]
    pend_a = None
    pend_b = None
    for l in reversed(range(DEPTH)):
        t = f"l{l}_"
        sv = saved[l]
        full = sv["full"]
        sgo = {}
        if l == DEPTH - 1:
            dr3, dr3b, sgo["ln3_g"], sgo["ln3_b"], loss11 = _loss_ln_bwd(
                "loss_ln3_bwd", sv["r3"], small["ln3_g"][l], small["ln3_b"][l], tgt)
            loss = lax.psum(loss11[0, 0], ("x", "y", "c"))
        else:
            dr3, dr3b, sgo["ln3_g"], sgo["ln3_b"] = _ln_bwd(t + "ln3_bwd", g, sv["r3"], small["ln3_g"][l],
                                                            after=(tok_a,))
        bm, bn = _tile(s, 1024), _tile(dff, 1024)

        def ep_dh(acc, ex, outs):
            outs[0][...] = (acc * (2.0 * jnp.maximum(ex[0][...].astype(F32), 0.0))).astype(BF16)

        tile = pl.BlockSpec((bm, bn), lambda i, j, k: (i, j))
        dh = _mm(t + "dh", dr3b, full["w_down"], dims=NT, grid=(s // bm, dff // bn, 1),
                 a_spec=pl.BlockSpec((bm, d), lambda i, j, k: (i, 0)),
                 b_spec=pl.BlockSpec((bn, d), lambda i, j, k: (j, 0)),
                 extras=(sv["h"],), extra_specs=(tile,), out_shape=[_sds((s, dff), BF16)], out_specs=[tile],
                 epilogue=ep_dh)[0]
        if pend_a is not None:
            tok_pa = pend_a.exchange(after=(dh,))
            grads[l + 1].update(pend_b.done(after=(dh,)))

        def halves(specs, hidx, got=None, after=()):
            out = {}
            for i, (n, a_, b_) in enumerate(specs):
                out[n] = _dw_half(f"{t}d{n}_{'s' if got is None else 'o'}", a_, b_, SHARD_AXIS[n], hidx,
                                  None if got is None else got[n], after if i == 0 else ())
            return out

        specs_c = [("w_down", sv["a"], dr3b), ("w_up", sv["x2b"], dh)]
        red_c = _Reduce(t + "rs_c", GROUPS_FWD[2],
                        halves(specs_c, sidx, after=() if pend_a is None else (tok_pa,)), mcidx)
        bm2, bn2 = _tile(s, 1024), _tile(d, 512)
        tile2 = pl.BlockSpec((bm2, bn2), lambda i, j, k: (i, j))
        dx2 = _mm(t + "dx2", dh, full["w_up"], dims=NT, grid=(s // bm2, d // bn2, 1),
                  a_spec=pl.BlockSpec((bm2, dff), lambda i, j, k: (i, 0)),
                  b_spec=pl.BlockSpec((bn2, dff), lambda i, j, k: (j, 0)),
                  extras=(dr3,), extra_specs=(tile2,), out_shape=[_sds((s, d), F32)], out_specs=[tile2],
                  epilogue=_ep_add_scaled, after=(red_c.token,))[0]
        tok_c = red_c.scatter(lambda got: halves(specs_c, cidx, got), after=(dx2,))
        if pend_a is not None:
            grads[l + 1].update(pend_a.done(after=(dx2,)))
            pend_a = None

        dr2, dr2b, sgo["ln2_g"], sgo["ln2_b"] = _ln_bwd(t + "ln2_bwd", dx2, sv["r2"], small["ln2_g"][l],
                                                        after=(tok_c,))
        dx1, dq, dkv = _xattn_bwd(t + "xattn_bwd", dr2b, dr2, sv["q"], sv["kv"], full["w_xo"], full["w_xq"])

        dr1, dr1b, sgo["ln1_g"], sgo["ln1_b"] = _ln_bwd(t + "ln1_bwd", dx1, sv["r1"], small["ln1_g"][l])
        dya, dyb, dgate, dba, dbb = _gate_bwd(t + "gate_bwd", dr1b, full["w_o"], sv["sa"], sv["sb"], sv["ya"],
                                              sv["yb"], d)
        sgo["b_gate"] = jnp.concatenate([dba, dbb], axis=-1)
        specs_b = [("w_o", sv["merged"], dr1b), ("w_br_a", sv["sg"], dya), ("w_br_b", sv["attn"], dyb),
                   ("w_xo", sv["o"], dr2b), ("w_xq", sv["x1b"], dq),
                   ("w_xkv", memb, _cast2d(t + "dkv_cast", dkv))]
        red_b = _Reduce(t + "rs_b", GROUPS_FWD[1], halves(specs_b, sidx), mcidx)

        def dbranch(name, dyx, w, after):
            return _mm(name, dyx, w, dims=NT, grid=(s // bm, 1, 1),
                       a_spec=pl.BlockSpec((bm, d), lambda i, j, k: (i, 0)),
                       b_spec=pl.BlockSpec((w.shape[0], d), lambda i, j, k: (0, 0)),
                       out_shape=[_sds((s, w.shape[0]), BF16)],
                       out_specs=[pl.BlockSpec((bm, w.shape[0]), lambda i, j, k: (i, 0))],
                       epilogue=_store, after=after)[0]

        dsg = dbranch(t + "dsg", dya, full["w_br_a"], (red_b.token,))
        dattn = dbranch(t + "dattn", dyb, full["w_br_b"], ())
        tok_c = red_c.exchange(after=(dattn, dsg))
        tok_b = red_b.scatter(lambda got: halves(specs_b, cidx, got), after=(dattn, dsg))
        duv, sgo["w_s"], dbst, dlg, dlb = _gmlp_bwd(t + "gmlp_bwd", sv["proj"], dsg, small["ln_v_g"][l],
                                                    small["ln_v_b"][l], small["w_s"][l], small["b_st"][l])
        sgo["b_s"] = dbst.T
        sgo["ln_v_g"], sgo["ln_v_b"] = dlg, dlb
        dqkv, sgo["sinks"] = _swa_bwd(t + "swa_bwd", sv["qr"], sv["kr"], sv["proj"], dattn, sv["attn"], sv["lse"],
                                      small["sink_rows"][l], cos4, nsin4, after=(tok_b, tok_c))
        grads[l].update(red_c.done(after=(dqkv,)))
        pieces = (duv, dqkv, (dgate, 0), (dgate, 1))
        widths = (2 * GMLP_W, ATT_W + 2 * KV_W, d, d)
        tok_b = red_b.exchange(after=(dqkv, duv))
        red_a = _Reduce(t + "rs_a", GROUPS_FWD[0],
                        {"w_in": _dw_pieces_half(t + "dw_in_s", sv["xb"], pieces, widths, sidx, after=(tok_b,))},
                        mcidx)
        g = _dx_pieces(t + "dx0", pieces, widths, full["w_in"], dr1, after=(red_a.token,))
        tok_a = red_a.scatter(
            lambda got: {"w_in": _dw_pieces_half(t + "dw_in_o", sv["xb"], pieces, widths, cidx, got["w_in"])},
            after=(g,))
        pend_a, pend_b = red_a, red_b
        small_g[l] = sgo
    grad_x = g.reshape(x.shape)

    big_out = {}

    def adam_layer(l, names, after):
        done = []
        for n in names:
            prev = big_out.get(n)
            big_out[n] = _adamw(f"adamw{l}_{n}", wts[n], grads[l][n], mom_m[n], mom_v[n], l, prev, after=after)
            done.append(big_out[n][0])
        return done

    shapes = [wts[n].shape for n in SMALL]
    packed_g = _pack([jnp.stack([small_g[l][n].reshape(wts[n].shape[1:]) for l in range(DEPTH)]) for n in SMALL])
    me8 = jnp.reshape(4 * ax_x + 2 * ax_y + ax_c, (1,)).astype(jnp.int32)
    ar_s, ar_r, ar_land, tok_ar = _split_start("ar_start", [packed_g], [_place_slot("ar_place", packed_g, me8)],
                                               _mk_small, 7, after=(tok_a,))
    fill = []
    for names in GROUPS_FWD:
        fill += adam_layer(1, names, (tok_ar,))
    grads[0].update(pend_b.done(after=tuple(fill)))
    fill += adam_layer(0, GROUPS_FWD[1], (tok_ar,))
    ar_land = _split_wait("ar_wait", [packed_g], ar_land, ar_s, ar_r, _mk_small, after=tuple(fill))
    packed_g = _sum_slots("ar_sum", ar_land[0])
    pw, pm, pv = (_pack([src[n] for n in SMALL]) for src in (wts, mom_m, mom_v))
    small4 = _adamw("adamw_small", pw[None], packed_g, pm[None], pv[None], 0)
    small_out = [dict(zip(SMALL, _unpack(a[0], shapes))) for a in small4]
    tok_a = pend_a.exchange(after=(small4[0],))
    fill = adam_layer(0, GROUPS_FWD[2], (tok_a,))
    grads[0].update(pend_a.done(after=tuple(fill)))
    adam_layer(0, GROUPS_FWD[0], ())

    def pick(kind, n):
        return big_out[n][kind] if n in big_out else small_out[kind][n]

    return (loss, grad_x, *[pick(0, n) for n in WEIGHTS], *[pick(1, n) for n in WEIGHTS],
            *[pick(2, n) for n in WEIGHTS], *[pick(3, n) for n in WEIGHTS])
```

```python
import math

import jax
import jax.numpy as jnp
from jax import lax
from jax.experimental import pallas as pl
from jax.experimental.pallas import tpu as pltpu

F32 = jnp.float32
BF16 = jnp.bfloat16
MESH = pl.DeviceIdType.MESH
ANY = pl.BlockSpec(memory_space=pl.ANY)
HBM = pl.BlockSpec(memory_space=pltpu.HBM)
SEM = pl.BlockSpec(memory_space=pltpu.SEMAPHORE)
VMEM_SPEC = pl.BlockSpec(memory_space=pltpu.VMEM)
EFFECT = pltpu.SideEffectType.DATAFLOW_SIDE_EFFECTING

DEPTH = 2
CHUNK = 128
GMLP_W = 1024
GROUPS = 8
NQ, NKV, HD = 16, 4, 64
ATT_W = NQ * HD
KV_W = NKV * HD
XH, XHD = 4, 128
X_W = XH * XHD
LN_EPS = 1e-5
ALPHA = (2 * DEPTH) ** 0.25
OFF_Q = 2 * GMLP_W
OFF_K = OFF_Q + ATT_W
OFF_VA = OFF_K + KV_W
OFF_GA = OFF_VA + KV_W
NEG = -1e30

ADAM_LR, ADAM_B1, ADAM_B2, ADAM_EPS, ADAM_WD, ADAM_STEP = 0.001, 0.9, 0.999, 1e-08, 0.01, 10

V7X_VMEM_BYTES = 64 * 1024 * 1024
VMEM_LIMIT = V7X_VMEM_BYTES - 4 * 1024 * 1024
LANE = 128

BIG = ("w_in", "w_br_a", "w_br_b", "w_o", "w_xq", "w_xkv", "w_xo", "w_up", "w_down")
SHARD_AXIS = {"w_in": 1, "w_br_a": 1, "w_br_b": 1, "w_o": 0, "w_xq": 0, "w_xkv": 0, "w_xo": 1,
              "w_up": 1, "w_down": 0}
GROUPS_GATHER = (("w_in",), ("w_br_a", "w_br_b", "w_o", "w_xq", "w_xkv", "w_xo"), ("w_up",), ("w_down",))
GROUPS_FWD = (("w_in",), ("w_br_a", "w_br_b", "w_o", "w_xq", "w_xkv", "w_xo"), ("w_up", "w_down"))
SMALL = ("b_gate", "ln_v_g", "ln_v_b", "w_s", "b_s", "sinks", "ln1_g", "ln1_b", "ln2_g", "ln2_b",
         "ln3_g", "ln3_b")
WEIGHTS = ("w_in", "b_gate", "ln_v_g", "ln_v_b", "w_s", "b_s", "sinks", "w_br_a", "w_br_b", "w_o",
           "ln1_g", "ln1_b", "w_xq", "w_xkv", "w_xo", "ln2_g", "ln2_b", "w_up", "w_down", "ln3_g", "ln3_b")


def _pallas(body, after=(), **kw):
    n_after = len(after)
    if not n_after:
        return pl.pallas_call(body, **kw)
    n_in = len(kw["in_specs"])
    kw["in_specs"] = list(kw["in_specs"]) + [ANY] * n_after

    def tied(*refs):
        return body(*refs[:n_in], *refs[n_in + n_after:])

    call = pl.pallas_call(tied, **kw)
    return lambda *ops: call(*ops, *after)


def _params(**kw):
    return pltpu.CompilerParams(vmem_limit_bytes=VMEM_LIMIT, **kw)


def _tile(dim, pref, unit=LANE):
    best = None
    t = unit
    while t <= min(dim, pref):
        if dim % t == 0:
            best = t
        t += unit
    return best if best is not None else dim


def _dot(a, b, dims):
    return lax.dot_general(a, b, (dims, ((), ())), preferred_element_type=F32)


NN = ((1,), (0,))
NT = ((1,), (1,))
TN = ((0,), (0,))


def _bf(x):
    return x if x.dtype == BF16 else x.astype(BF16)


def _sds(shape, dtype):
    return jax.ShapeDtypeStruct(shape, dtype)


def _mm(name, a, b, *, dims, grid, a_spec, b_spec, out_shape, out_specs, epilogue,
        extras=(), extra_specs=(), after=()):
    assert grid[2] == 1
    n_ex, n_out = len(extras), len(out_shape)

    def body(*refs):
        ex = refs[2:2 + n_ex]
        outs = refs[2 + n_ex:2 + n_ex + n_out]
        epilogue(_dot(_bf(refs[0][...]), _bf(refs[1][...]), dims), ex, outs)

    return _pallas(
        body, after=after, name=name, grid=grid, in_specs=[a_spec, b_spec, *extra_specs], out_specs=list(out_specs),
        out_shape=list(out_shape), compiler_params=_params(dimension_semantics=("arbitrary",) * 3),
    )(a, b, *extras)


def _store(acc, ex, outs):
    for o in outs:
        o[...] = acc.astype(o.dtype)


def _ln_rows(r, g, b):
    mu = jnp.mean(r, axis=-1, keepdims=True)
    xc = r - mu
    var = jnp.mean(xc * xc, axis=-1, keepdims=True)
    rstd = lax.rsqrt(var + LN_EPS)
    xhat = xc * rstd
    return xhat * g + b, xhat, rstd


def _ep_add_scaled(acc, ex, outs):
    outs[0][...] = acc + ALPHA * ex[0][...]


def _ln_fwd(name, r, g, b):
    s, d = r.shape
    bm = _tile(s, 512)

    def body(r_ref, g_ref, b_ref, y_ref, yb_ref):
        y, _, _ = _ln_rows(r_ref[...], g_ref[...], b_ref[...])
        y_ref[...] = y
        yb_ref[...] = y.astype(BF16)

    row = pl.BlockSpec((bm, d), lambda i: (i, 0))
    vec = pl.BlockSpec((1, d), lambda i: (0, 0))
    return _pallas(body, name=name, grid=(s // bm,), in_specs=[row, vec, vec], out_specs=[row, row],
                   out_shape=[_sds((s, d), F32), _sds((s, d), BF16)], compiler_params=_params())(r, g, b)


_GC = math.sqrt(2.0 / math.pi)


def _gelu(x):
    t = jnp.tanh(_GC * (x + 0.044715 * (x * x * x)))
    return 0.5 * x * (1.0 + t), t


def _gelu_grad(x, t):
    return 0.5 * (1.0 + t) + 0.5 * x * (1.0 - t * t) * (_GC * (1.0 + 3.0 * 0.044715 * x * x))


def _sigmoid(x):
    return 1.0 / (1.0 + jnp.exp(-x))


GRP = NQ // NKV


def _band_mask(prev_ok, prev_only=False):
    rows = CHUNK if prev_only else 2 * CHUNK
    key = lax.broadcasted_iota(jnp.int32, (rows, GRP * CHUNK), 0)
    qry = jnp.bitwise_and(lax.broadcasted_iota(jnp.int32, (rows, GRP * CHUNK), 1), CHUNK - 1)
    prev = jnp.logical_and(jnp.logical_and(key < CHUNK, key > qry), prev_ok)
    if prev_only:
        return prev
    return jnp.logical_or(prev, jnp.logical_and(key >= CHUNK, key - CHUNK <= qry))


def _pair(x, g):
    return x[:, (g // 2) * LANE:(g // 2 + 1) * LANE]


def _own_head(x, g):
    xp = _pair(x, g)
    lane = lax.broadcasted_iota(jnp.int32, xp.shape, 1)
    lo = (g % 2) * HD
    return jnp.where(jnp.logical_and(lane >= lo, lane < lo + HD), xp, jnp.zeros_like(xp))


def _stack_heads(x, g, dtype=BF16):
    a = x[:, g * GRP * HD:g * GRP * HD + LANE]
    b = x[:, g * GRP * HD + LANE:(g + 1) * GRP * HD]
    ar, br = pltpu.roll(a, HD, 1), pltpu.roll(b, HD, 1)
    parts = [a, ar, b, br] if g % 2 == 0 else [ar, a, br, b]
    return jnp.concatenate(parts, axis=0).astype(dtype)


def _unstack_heads(og, g):
    o = [og[h * CHUNK:(h + 1) * CHUNK] for h in range(GRP)]
    lo = lax.broadcasted_iota(jnp.int32, (CHUNK, LANE), 1) < HD
    if g % 2 == 0:
        x0, x1, x2, x3 = o[0], pltpu.roll(o[1], HD, 1), o[2], pltpu.roll(o[3], HD, 1)
    else:
        x0, x1, x2, x3 = pltpu.roll(o[0], HD, 1), o[1], pltpu.roll(o[2], HD, 1), o[3]
    return [jnp.where(lo, x0, x1), jnp.where(lo, x2, x3)]


def _stack_rows(x, g):
    return jnp.concatenate([x[g * GRP + h:g * GRP + h + 1] for h in range(GRP)], axis=-1)


def _head_lane_sums(x, g):
    lane = lax.broadcasted_iota(jnp.int32, (8, LANE), 1)
    lo_lane = (g % 2) * HD
    sel = jnp.where(jnp.logical_and(lane >= lo_lane, lane < lo_lane + HD), 1.0, 0.0).astype(BF16)
    hi = x.astype(BF16)
    lo = (x - hi.astype(F32)).astype(BF16)
    return (_dot(sel, hi, NT) + _dot(sel, lo, NT))[0:1]


def _rope(x, cos, sin_signed):
    w = x.shape[-1]
    lane = lax.broadcasted_iota(jnp.int32, x.shape, 1)
    first = (lane % HD) < (HD // 2)
    partner = jnp.where(first, pltpu.roll(x, w - HD // 2, 1), pltpu.roll(x, HD // 2, 1))
    reps = w // LANE
    return x * jnp.tile(cos, (1, reps)) + partner * jnp.tile(sin_signed, (1, reps))


def _cast2d(name, x, after=()):
    s, d = x.shape
    bm = _tile(s, 512, 8)

    def body(x_ref, o_ref):
        o_ref[...] = x_ref[...].astype(BF16)

    spec = pl.BlockSpec((bm, d), lambda i: (i, 0))
    return _pallas(body, after=after, name=name, grid=(s // bm,), in_specs=[spec], out_specs=spec,
                   out_shape=_sds(x.shape, BF16), compiler_params=_params())(x)


def _place():
    x, y, c = lax.axis_index("x"), lax.axis_index("y"), lax.axis_index("c")
    chips = [(1 - x, y), (x, 1 - y), (1 - x, 1 - y)]
    return x, y, c, chips


def _cut(ref, axis, chip=None, half=None):
    k, n = ref.shape[-2], ref.shape[-1]
    rows, cols = slice(None), slice(None)
    if chip is not None:
        if axis == 0:
            rows = pl.ds(pl.multiple_of(chip * (k // 4), 8), k // 4)
        else:
            cols = pl.ds(pl.multiple_of(chip * (n // 4), LANE), n // 4)
    if half is not None:
        if axis == 0:
            cols = pl.ds(pl.multiple_of(half * (n // 2), LANE), n // 2)
        else:
            rows = pl.ds(pl.multiple_of(half * (k // 2), 8), k // 2)
    return ref.at[rows, cols]


def _split_start(name, srcs, lands, make, n_sem, after=()):
    ns, nl, na = len(srcs), len(lands), len(after)

    def body(*refs):
        src, land = refs[:ns], refs[ns:ns + nl]
        outs = refs[ns + nl + na:]
        for out_cp, _ in make(src, land, outs[0], outs[1]):
            out_cp.start()
        outs[-1][...] = jnp.zeros_like(outs[-1])

    res = pl.pallas_call(
        body, name=name, in_specs=[HBM] * (ns + nl) + [ANY] * na,
        out_specs=[SEM, SEM] + [HBM] * nl + [VMEM_SPEC],
        out_shape=[pltpu.SemaphoreType.DMA((n_sem,)), pltpu.SemaphoreType.DMA((n_sem,))]
        + [pltpu.HBM(a.shape, a.dtype) for a in lands] + [_sds((8, LANE), F32)],
        input_output_aliases={ns + i: 2 + i for i in range(nl)},
        compiler_params=pltpu.CompilerParams(has_side_effects=EFFECT),
    )(*[pltpu.with_memory_space_constraint(a, pltpu.HBM) for a in (*srcs, *lands)], *after)
    return res[0], res[1], list(res[2:2 + nl]), res[-1]


def _split_wait(name, srcs, lands, ssem, rsem, make, after=()):
    ns, nl, na = len(srcs), len(lands), len(after)

    def body(*refs):
        src, land = refs[:ns], refs[ns:ns + nl]
        s_ref, r_ref = refs[ns + nl], refs[ns + nl + 1]
        pairs = make(src, land, s_ref, r_ref)
        for _, in_cp in pairs:
            in_cp.wait_recv()
        for out_cp, _ in pairs:
            out_cp.wait_send()

    res = pl.pallas_call(
        body, name=name, in_specs=[HBM] * (ns + nl) + [SEM, SEM] + [ANY] * na,
        out_specs=[HBM] * nl, out_shape=[pltpu.HBM(a.shape, a.dtype) for a in lands],
        input_output_aliases={ns + i: i for i in range(nl)},
        compiler_params=pltpu.CompilerParams(has_side_effects=EFFECT),
    )(*srcs, *lands, ssem, rsem, *after)
    return list(res)


def _rcopy(src, dst, ssem, rsem, k, dev):
    return pltpu.make_async_remote_copy(src_ref=src, dst_ref=dst, send_sem=ssem.at[k], recv_sem=rsem.at[k],
                                        device_id=dev, device_id_type=MESH)


def _mk_gather_ici(axes):
    def make(src, land, ssem, rsem):
        x, y, c, chips = _place()
        me = 2 * x + y
        pairs = []
        for w, ax in enumerate(axes):
            mine = _cut(land[w], ax, chip=me, half=c)
            for j, (px, py) in enumerate(chips):
                dev = (px, py, c)
                got = _cut(land[w], ax, chip=2 * px + py, half=c)
                pairs.append((_rcopy(mine, mine, ssem, rsem, 3 * w + j, dev),
                              _rcopy(got, got, ssem, rsem, 3 * w + j, dev)))
        return pairs
    return make


def _mk_gather_d2d(axes):
    def make(src, land, ssem, rsem):
        x, y, c, chips = _place()
        sib = (x, y, 1 - c)
        pairs = []
        for w, ax in enumerate(axes):
            for j, (px, py) in enumerate(chips):
                have = _cut(land[w], ax, chip=2 * px + py, half=c)
                want = _cut(land[w], ax, chip=2 * px + py, half=1 - c)
                pairs.append((_rcopy(have, have, ssem, rsem, 3 * w + j, sib),
                              _rcopy(want, want, ssem, rsem, 3 * w + j, sib)))
        return pairs
    return make


def _mk_swap(src, land, ssem, rsem):
    x, y, c, _ = _place()
    pairs = []
    for w in range(len(src)):
        cp = _rcopy(src[w], land[w], ssem, rsem, w, (x, y, 1 - c))
        pairs.append((cp, cp))
    return pairs


def _mk_scatter(axes):
    def make(src, land, ssem, rsem):
        x, y, c, chips = _place()
        pairs = []
        for w, ax in enumerate(axes):
            for j, (px, py) in enumerate(chips):
                cp = _rcopy(_cut(src[w], ax, chip=2 * px + py), land[w].at[j], ssem, rsem, 3 * w + j, (px, py, c))
                pairs.append((cp, cp))
        return pairs
    return make


def _mk_exchange(axes):
    def make(src, land, ssem, rsem):
        x, y, c, _ = _place()
        sib = (x, y, 1 - c)
        pairs = []
        for w, ax in enumerate(axes):
            have = _cut(land[w], ax, half=c)
            want = _cut(land[w], ax, half=1 - c)
            pairs.append((_rcopy(have, have, ssem, rsem, w, sib), _rcopy(want, want, ssem, rsem, w, sib)))
        return pairs
    return make


def _place_own(name, shard, axis, meidx, after=()):
    _, r, c = shard.shape
    full = (4 * r, c) if axis == 0 else (r, 4 * c)
    br = _tile(r, 512, 8)
    nb = r // br
    if axis == 0:
        ospec = pl.BlockSpec((br, c), lambda i, me: (me[0] * nb + i, 0))
    else:
        ospec = pl.BlockSpec((br, c), lambda i, me: (i, me[0]))
    n_after = len(after)

    def body(me_ref, s_ref, *rest):
        o0_ref, o1_ref = rest[n_after:]
        o0_ref[...] = s_ref[0].astype(BF16)
        o1_ref[...] = s_ref[1].astype(BF16)

    return pl.pallas_call(
        body, name=name,
        grid_spec=pltpu.PrefetchScalarGridSpec(
            num_scalar_prefetch=1, grid=(nb,),
            in_specs=[pl.BlockSpec((2, br, c), lambda i, me: (0, i, 0))] + [ANY] * n_after,
            out_specs=[ospec, ospec]),
        out_shape=[_sds(full, BF16)] * 2, compiler_params=_params(),
    )(meidx, shard, *after)


def _sum_half(name, own, slots, axis, mc):
    _, r, cc = slots.shape
    br = _tile(r, 256, 8)
    nb = r // br
    if axis == 0:
        own_spec = pl.BlockSpec((br, cc), lambda i, mc: (mc[0] * nb + i, 0))
        out_spec = pl.BlockSpec((br, cc), lambda i, mc: (i, mc[1]))
        shape = (r, 2 * cc)
    else:
        own_spec = pl.BlockSpec((br, cc), lambda i, mc: (i, mc[0]))
        out_spec = pl.BlockSpec((br, cc), lambda i, mc: (mc[1] * nb + i, 0))
        shape = (2 * r, cc)

    def body(mc_ref, own_ref, s_ref, o_ref):
        acc = own_ref[...].astype(F32)
        for i in range(3):
            acc = acc + s_ref[i].astype(F32)
        o_ref[...] = acc

    return pl.pallas_call(
        body, name=name,
        grid_spec=pltpu.PrefetchScalarGridSpec(
            num_scalar_prefetch=1, grid=(nb,),
            in_specs=[own_spec, pl.BlockSpec((3, br, cc), lambda i, mc: (0, i, 0))], out_specs=out_spec),
        out_shape=_sds(shape, F32), compiler_params=_params(),
    )(mc, own, slots)


def _mk_small(src, land, ssem, rsem):
    x, y, c, _ = _place()
    me = 4 * x + 2 * y + c
    pairs = []
    for k in range(1, 8):
        peer = (1 - x if k & 4 else x, 1 - y if k & 2 else y, 1 - c if k & 1 else c)
        got = land[0].at[4 * peer[0] + 2 * peer[1] + peer[2]]
        pairs.append((_rcopy(src[0], land[0].at[me], ssem, rsem, k - 1, peer),
                      _rcopy(got, got, ssem, rsem, k - 1, peer)))
    return pairs


def _place_slot(name, packed, me8):
    rows, lanes = packed.shape
    br = _tile(rows, 512, 8)

    def body(me_ref, p_ref, o_ref):
        o_ref[...] = p_ref[...]

    return pl.pallas_call(
        body, name=name,
        grid_spec=pltpu.PrefetchScalarGridSpec(
            num_scalar_prefetch=1, grid=(rows // br,),
            in_specs=[pl.BlockSpec((br, lanes), lambda i, me: (i, 0))],
            out_specs=pl.BlockSpec((None, br, lanes), lambda i, me: (me[0], i, 0))),
        out_shape=_sds((8, rows, lanes), F32), compiler_params=_params(),
    )(me8, packed)


def _sum_slots(name, slots):
    _, rows, lanes = slots.shape
    br = _tile(rows, 512, 8)

    def body(s_ref, o_ref):
        acc = s_ref[0]
        for i in range(1, 8):
            acc = acc + s_ref[i]
        o_ref[...] = acc

    return pl.pallas_call(
        body, name=name, grid=(rows // br,), in_specs=[pl.BlockSpec((8, br, lanes), lambda i: (0, i, 0))],
        out_specs=pl.BlockSpec((br, lanes), lambda i: (i, 0)), out_shape=_sds((rows, lanes), F32),
        compiler_params=_params(),
    )(slots)


def _adamw_math(w, g, m, v):
    m2 = ADAM_B1 * m + (1.0 - ADAM_B1) * g
    v2 = ADAM_B2 * v + (1.0 - ADAM_B2) * (g * g)
    m_hat = m2 / (1.0 - ADAM_B1 ** ADAM_STEP)
    v_hat = v2 / (1.0 - ADAM_B2 ** ADAM_STEP)
    delta = -ADAM_LR * (m_hat / (jnp.sqrt(v_hat) + ADAM_EPS) + ADAM_WD * w)
    return delta, m2, v2


def _adamw(name, w, g, m, v, layer, prev=None, after=()):
    _, r, c = w.shape
    br = _tile(r, 256, 8)
    n_prev = 0 if prev is None else 4

    def body(*refs):
        w_ref, g_ref, m_ref, v_ref = refs[:4]
        go_ref, d_ref, mo_ref, vo_ref = refs[4 + n_prev:]
        gg = g_ref[...]
        delta, m2, v2 = _adamw_math(w_ref[...], gg, m_ref[...], v_ref[...])
        go_ref[...] = gg
        d_ref[...] = delta
        mo_ref[...] = m2
        vo_ref[...] = v2

    spec = pl.BlockSpec((None, br, c), lambda i: (layer, i, 0))
    return _pallas(
        body, after=after, name=name, grid=(r // br,),
        in_specs=[spec, pl.BlockSpec((br, c), lambda i: (i, 0)), spec, spec] + [ANY] * n_prev,
        out_specs=[spec] * 4, out_shape=[_sds(w.shape, F32)] * 4,
        input_output_aliases={4 + i: i for i in range(n_prev)}, compiler_params=_params(),
    )(w, g, m, v, *(prev or ()))


def _gmlp_fwd(name, proj, ln_g, ln_b, w_s, b_st):
    s = proj.shape[0]

    def body(u_ref, v_ref, g_ref, b_ref, ws_ref, bst_ref, sg_ref):
        gu, _ = _gelu(u_ref[...])
        gv, _ = _gelu(v_ref[...])
        vn, _, _ = _ln_rows(gv, g_ref[...], b_ref[...])
        vn = vn.astype(BF16)
        row = lax.broadcasted_iota(jnp.int32, (CHUNK, CHUNK), 0)
        col = lax.broadcasted_iota(jnp.int32, (CHUNK, CHUNK), 1)
        tril = col <= row
        outs = []
        for g in range(GROUPS):
            sl = slice(g * LANE, (g + 1) * LANE)
            w = jnp.where(tril, ws_ref[g], 0.0).astype(BF16)
            mixed = _dot(w, vn[:, sl], NN) + bst_ref[:, g:g + 1]
            outs.append(gu[:, sl] * mixed)
        sg_ref[...] = jnp.concatenate(outs, axis=-1).astype(BF16)

    return _pallas(
        body, name=name, grid=(s // CHUNK,),
        in_specs=[pl.BlockSpec((CHUNK, GMLP_W), lambda n: (n, 0)), pl.BlockSpec((CHUNK, GMLP_W), lambda n: (n, 1)),
                  pl.BlockSpec((1, GMLP_W), lambda n: (0, 0)), pl.BlockSpec((1, GMLP_W), lambda n: (0, 0)),
                  pl.BlockSpec((GROUPS, CHUNK, CHUNK), lambda n: (0, 0, 0)),
                  pl.BlockSpec((CHUNK, GROUPS), lambda n: (0, 0))],
        out_specs=pl.BlockSpec((CHUNK, GMLP_W), lambda n: (n, 0)),
        out_shape=_sds((s, GMLP_W), BF16), compiler_params=_params(),
    )(proj, proj, ln_g, ln_b, w_s, b_st)


def _swa_fwd(name, proj, cos4, sin4, sinks, after=()):
    s = proj.shape[0]
    w = CHUNK
    scale = HD ** -0.5

    def body(q_ref, k_ref, v_ref, cos_ref, sin_ref, sink_ref, o_ref, qr_ref, kr_ref, lse_ref, kprev, vprev):
        n = pl.program_id(0)

        @pl.when(n == 0)
        def _():
            kprev[...] = jnp.zeros_like(kprev)
            vprev[...] = jnp.zeros_like(vprev)

        cos, sin = cos_ref[...], sin_ref[...]
        qr = _rope(q_ref[...], cos, sin)
        kr = _rope(k_ref[...], cos, sin).astype(BF16)
        vb = v_ref[...].astype(BF16)
        kk = jnp.concatenate([kprev[...], kr], axis=0)
        vv = jnp.concatenate([vprev[...], vb], axis=0)
        valid = _band_mask(n > 0)
        outs, lses = [], []
        for g in range(NKV):
            sc = jnp.where(valid, _dot(_own_head(kk, g), _stack_heads(qr, g), NT) * scale, NEG)
            sink = sink_ref[g]
            mx = jnp.maximum(jnp.max(sc, axis=0, keepdims=True), sink)
            p = jnp.exp(sc - mx)
            den = jnp.sum(p, axis=0, keepdims=True) + jnp.exp(sink - mx)
            og = _dot((p * (1.0 / den)).astype(BF16), _pair(vv, g), TN)
            outs.extend(_unstack_heads(og, g))
            lg = mx + jnp.log(den)
            lses.extend([lg[:, h * w:(h + 1) * w] for h in range(GRP)])
        o_ref[...] = jnp.concatenate(outs, axis=-1).astype(BF16)
        lse_ref[...] = jnp.concatenate(lses, axis=0)
        qr_ref[...] = qr.astype(BF16)
        kr_ref[...] = kr
        kprev[...] = kr
        vprev[...] = vb

    return _pallas(
        body, after=after, name=name, grid=(s // w,),
        in_specs=[pl.BlockSpec((w, ATT_W), lambda n: (n, OFF_Q // ATT_W)),
                  pl.BlockSpec((w, KV_W), lambda n: (n, OFF_K // KV_W)),
                  pl.BlockSpec((w, KV_W), lambda n: (n, OFF_VA // KV_W)),
                  pl.BlockSpec((w, LANE), lambda n: (n, 0)), pl.BlockSpec((w, LANE), lambda n: (n, 0)),
                  pl.BlockSpec((NKV, 1, GRP * w), lambda n: (0, 0, 0))],
        out_specs=[pl.BlockSpec((w, ATT_W), lambda n: (n, 0)), pl.BlockSpec((w, ATT_W), lambda n: (n, 0)),
                   pl.BlockSpec((w, KV_W), lambda n: (n, 0)), pl.BlockSpec((None, NQ, w), lambda n: (n, 0, 0))],
        out_shape=[_sds((s, ATT_W), BF16), _sds((s, ATT_W), BF16), _sds((s, KV_W), BF16),
                   _sds((s // w, NQ, w), F32)],
        scratch_shapes=[pltpu.VMEM((w, KV_W), BF16), pltpu.VMEM((w, KV_W), BF16)],
        compiler_params=_params(dimension_semantics=("arbitrary",)),
    )(proj, proj, proj, cos4, sin4, sinks)


def _gate_fwd(name, sg, attn, wa, wb, proj, b_gate, d):
    s = sg.shape[0]
    bm, bn = _tile(s, 1024), _tile(d, 512)
    off_a, off_b = OFF_GA // bn, (OFF_GA + d) // bn

    def body(sg_ref, at_ref, wa_ref, wb_ref, ga_ref, gb_ref, ba_ref, bb_ref, m_ref, ya_ref, yb_ref, sa_ref, sb_ref):
        ya = _dot(sg_ref[...], wa_ref[...], NN)
        yb = _dot(at_ref[...], wb_ref[...], NN)
        sa = _sigmoid(ga_ref[...] + ba_ref[...])
        sb = _sigmoid(gb_ref[...] + bb_ref[...])
        m_ref[...] = (sa * ya + sb * yb).astype(BF16)
        ya_ref[...] = ya.astype(BF16)
        yb_ref[...] = yb.astype(BF16)
        sa_ref[...] = sa.astype(BF16)
        sb_ref[...] = sb.astype(BF16)

    tile = pl.BlockSpec((bm, bn), lambda i, j: (i, j))
    return _pallas(
        body, name=name, grid=(s // bm, d // bn),
        in_specs=[pl.BlockSpec((bm, GMLP_W), lambda i, j: (i, 0)), pl.BlockSpec((bm, ATT_W), lambda i, j: (i, 0)),
                  pl.BlockSpec((GMLP_W, bn), lambda i, j: (0, j)), pl.BlockSpec((ATT_W, bn), lambda i, j: (0, j)),
                  pl.BlockSpec((bm, bn), lambda i, j: (i, off_a + j)),
                  pl.BlockSpec((bm, bn), lambda i, j: (i, off_b + j)),
                  pl.BlockSpec((1, bn), lambda i, j: (0, j)), pl.BlockSpec((1, bn), lambda i, j: (0, d // bn + j))],
        out_specs=[tile] * 5, out_shape=[_sds((s, d), BF16)] * 5,
        compiler_params=_params(),
    )(sg, attn, wa, wb, proj, proj, b_gate, b_gate)


def _xattn_fwd(name, xb, xf, wq, kv, wo, ln_g, ln_b, after=()):
    s, d = xf.shape
    mem = kv.shape[0]
    bm = _tile(s, 512)
    scale = XHD ** -0.5

    def body(xb_ref, xf_ref, wq_ref, kv_ref, wo_ref, g_ref, b_ref, q_out, o_out, r_out, y_out, yb_out):
        qb = _dot(xb_ref[...], wq_ref[...], NN).astype(BF16)
        kvv = kv_ref[...]
        outs = []
        for h in range(XH):
            hs = slice(h * XHD, (h + 1) * XHD)
            vs = slice(X_W + h * XHD, X_W + (h + 1) * XHD)
            sc = _dot(qb[:, hs], kvv[:, hs], NT) * scale
            mx = jnp.max(sc, axis=-1, keepdims=True)
            p = jnp.exp(sc - mx)
            p = p / jnp.sum(p, axis=-1, keepdims=True)
            outs.append(_dot(p.astype(BF16), kvv[:, vs], NN))
        ob = jnp.concatenate(outs, axis=-1).astype(BF16)
        yv = _dot(ob, wo_ref[...], NN)
        r = ALPHA * xf_ref[...] + yv
        yn, _, _ = _ln_rows(r, g_ref[...], b_ref[...])
        q_out[...] = qb
        o_out[...] = ob
        r_out[...] = r
        y_out[...] = yn
        yb_out[...] = yn.astype(BF16)

    row = lambda wd: pl.BlockSpec((bm, wd), lambda i: (i, 0))
    return _pallas(
        body, after=after, name=name, grid=(s // bm,),
        in_specs=[row(d), row(d), pl.BlockSpec((d, X_W), lambda i: (0, 0)),
                  pl.BlockSpec((mem, 2 * X_W), lambda i: (0, 0)), pl.BlockSpec((X_W, d), lambda i: (0, 0)),
                  pl.BlockSpec((1, d), lambda i: (0, 0)), pl.BlockSpec((1, d), lambda i: (0, 0))],
        out_specs=[row(X_W), row(X_W), row(d), row(d), row(d)],
        out_shape=[_sds((s, X_W), BF16), _sds((s, X_W), BF16), _sds((s, d), F32), _sds((s, d), F32),
                   _sds((s, d), BF16)],
        compiler_params=_params(),
    )(xb, xf, wq, kv, wo, ln_g, ln_b)


def _accumulate(i, refs, vals):
    @pl.when(i == 0)
    def _():
        for ref, v in zip(refs, vals):
            ref[...] = v

    @pl.when(i > 0)
    def _():
        for ref, v in zip(refs, vals):
            ref[...] += v


def _ln_bwd_rows(dyv, r, g, dr_ref, drb_ref):
    _, xhat, rstd = _ln_rows(r, g, 0.0)
    dxh = dyv * g
    m1 = jnp.mean(dxh, axis=-1, keepdims=True)
    m2 = jnp.mean(dxh * xhat, axis=-1, keepdims=True)
    dr = rstd * (dxh - m1 - xhat * m2)
    dr_ref[...] = dr
    drb_ref[...] = dr.astype(BF16)
    return jnp.sum(dyv * xhat, axis=0, keepdims=True), jnp.sum(dyv, axis=0, keepdims=True)


def _ln_bwd(name, dy, r, g, after=()):
    s, d = r.shape
    bm = _tile(s, 512)

    def body(dy_ref, r_ref, g_ref, dr_ref, drb_ref, dg_ref, db_ref):
        dg, db = _ln_bwd_rows(dy_ref[...], r_ref[...], g_ref[...], dr_ref, drb_ref)
        _accumulate(pl.program_id(0), (dg_ref, db_ref), (dg, db))

    row = pl.BlockSpec((bm, d), lambda i: (i, 0))
    vec = pl.BlockSpec((1, d), lambda i: (0, 0))
    return _pallas(
        body, after=after, name=name, grid=(s // bm,), in_specs=[row, row, vec], out_specs=[row, row, vec, vec],
        out_shape=[_sds((s, d), F32), _sds((s, d), BF16), _sds((1, d), F32), _sds((1, d), F32)],
        compiler_params=_params(dimension_semantics=("arbitrary",)),
    )(dy, r, g)


def _loss_ln_bwd(name, r, g, b, tgt):
    s, d = r.shape
    bm = _tile(s, 512)

    def body(r_ref, g_ref, b_ref, t_ref, dr_ref, drb_ref, dg_ref, db_ref, loss_ref):
        rv, gv = r_ref[...], g_ref[...]
        y, _, _ = _ln_rows(rv, gv, b_ref[...])
        err = y - t_ref[...]
        part = 0.5 * jnp.sum(jnp.sum(err * err, axis=-1, keepdims=True), axis=0, keepdims=True) * (1.0 / d)
        dg, db = _ln_bwd_rows(err * (1.0 / d), rv, gv, dr_ref, drb_ref)
        _accumulate(pl.program_id(0), (dg_ref, db_ref, loss_ref), (dg, db, part))

    row = pl.BlockSpec((bm, d), lambda i: (i, 0))
    vec = pl.BlockSpec((1, d), lambda i: (0, 0))
    return _pallas(
        body, name=name, grid=(s // bm,), in_specs=[row, vec, vec, row],
        out_specs=[row, row, vec, vec, pl.BlockSpec((1, 1), lambda i: (0, 0))],
        out_shape=[_sds((s, d), F32), _sds((s, d), BF16), _sds((1, d), F32), _sds((1, d), F32), _sds((1, 1), F32)],
        compiler_params=_params(dimension_semantics=("arbitrary",)),
    )(r, g, b, tgt)


def _xattn_bwd(name, dyb, drf, q, kv, wo, wq):
    s, d = drf.shape
    mem = kv.shape[0]
    bm = _tile(s, 512)
    scale = XHD ** -0.5

    def body(dy_ref, dr_ref, q_ref, kv_ref, wo_ref, wq_ref, dx_out, dq_out, dkv_out):
        i = pl.program_id(0)
        dob = _dot(dy_ref[...], wo_ref[...], NT).astype(BF16)
        qb = q_ref[...]
        kvv = kv_ref[...]
        dqs, dks, dvs = [], [], []
        for h in range(XH):
            hs = slice(h * XHD, (h + 1) * XHD)
            vs = slice(X_W + h * XHD, X_W + (h + 1) * XHD)
            sc = _dot(qb[:, hs], kvv[:, hs], NT) * scale
            mx = jnp.max(sc, axis=-1, keepdims=True)
            p = jnp.exp(sc - mx)
            p = p / jnp.sum(p, axis=-1, keepdims=True)
            dp = _dot(dob[:, hs], kvv[:, vs], NT)
            dsum = jnp.sum(p * dp, axis=-1, keepdims=True)
            dsb = (p * (dp - dsum) * scale).astype(BF16)
            dqs.append(_dot(dsb, kvv[:, hs], NN))
            dks.append(_dot(dsb, qb[:, hs], TN))
            dvs.append(_dot(p.astype(BF16), dob[:, hs], TN))
        dqb = jnp.concatenate(dqs, axis=-1).astype(BF16)
        dq_out[...] = dqb
        dx_out[...] = _dot(dqb, wq_ref[...], NT) + ALPHA * dr_ref[...]
        dkv = jnp.concatenate(dks + dvs, axis=-1)

        @pl.when(i == 0)
        def _():
            dkv_out[...] = dkv

        @pl.when(i > 0)
        def _():
            dkv_out[...] += dkv

    row = lambda wd: pl.BlockSpec((bm, wd), lambda i: (i, 0))
    return _pallas(
        body, name=name, grid=(s // bm,),
        in_specs=[row(d), row(d), row(X_W), pl.BlockSpec((mem, 2 * X_W), lambda i: (0, 0)),
                  pl.BlockSpec((X_W, d), lambda i: (0, 0)), pl.BlockSpec((d, X_W), lambda i: (0, 0))],
        out_specs=[row(d), row(X_W), pl.BlockSpec((mem, 2 * X_W), lambda i: (0, 0))],
        out_shape=[_sds((s, d), F32), _sds((s, X_W), BF16), _sds((mem, 2 * X_W), F32)],
        compiler_params=_params(dimension_semantics=("arbitrary",)),
    )(dyb, drf, q, kv, wo, wq)


def _gate_bwd(name, dr1b, w_o, sa, sb, ya, yb, d, after=()):
    s = dr1b.shape[0]
    bm, bn = _tile(s, 1024), _tile(d, 512)
    nj = d // bn

    def body(a_ref, w_ref, sa_ref, sb_ref, ya_ref, yb_ref, dya_ref, dyb_ref, dg_ref, dba_ref, dbb_ref):
        i = pl.program_id(1)
        dm = _dot(a_ref[...], w_ref[...], NT)
        sa = sa_ref[...].astype(F32)
        sb = sb_ref[...].astype(F32)
        dya_ref[...] = (dm * sa).astype(BF16)
        dyb_ref[...] = (dm * sb).astype(BF16)
        dga = dm * ya_ref[...].astype(F32) * (sa * (1.0 - sa))
        dgb = dm * yb_ref[...].astype(F32) * (sb * (1.0 - sb))
        dg_ref[0] = dga.astype(BF16)
        dg_ref[1] = dgb.astype(BF16)
        sa_sum = jnp.sum(dga, axis=0, keepdims=True)
        sb_sum = jnp.sum(dgb, axis=0, keepdims=True)

        @pl.when(i == 0)
        def _():
            dba_ref[...] = sa_sum
            dbb_ref[...] = sb_sum

        @pl.when(i > 0)
        def _():
            dba_ref[...] += sa_sum
            dbb_ref[...] += sb_sum

    tile = pl.BlockSpec((bm, bn), lambda j, i: (i, j))
    return _pallas(
        body, after=after, name=name, grid=(nj, s // bm),
        in_specs=[pl.BlockSpec((bm, d), lambda j, i: (i, 0)), pl.BlockSpec((bn, d), lambda j, i: (j, 0)),
                  tile, tile, tile, tile],
        out_specs=[tile, tile, pl.BlockSpec((2, bm, bn), lambda j, i: (0, i, j)),
                   pl.BlockSpec((1, bn), lambda j, i: (0, j)), pl.BlockSpec((1, bn), lambda j, i: (0, j))],
        out_shape=[_sds((s, d), BF16), _sds((s, d), BF16), _sds((2, s, d), BF16), _sds((1, d), F32),
                   _sds((1, d), F32)],
        compiler_params=_params(dimension_semantics=("arbitrary", "arbitrary")),
    )(dr1b, w_o, sa, sb, ya, yb)


def _gmlp_bwd(name, proj, dsg, ln_g, ln_b, w_s, b_st):
    s = proj.shape[0]

    def body(u_ref, v_ref, dsg_ref, g_ref, b_ref, ws_ref, bst_ref, duv_ref, dws_ref, dbst_ref, dlg_ref, dlb_ref):
        n = pl.program_id(0)
        u, v = u_ref[...], v_ref[...]
        gu, tu = _gelu(u)
        gv, tv = _gelu(v)
        gam = g_ref[...]
        vn, xhat, rstd = _ln_rows(gv, gam, b_ref[...])
        vnb = vn.astype(BF16)
        dsg = dsg_ref[...].astype(F32)
        row = lax.broadcasted_iota(jnp.int32, (CHUNK, CHUNK), 0)
        col = lax.broadcasted_iota(jnp.int32, (CHUNK, CHUNK), 1)
        tril = col <= row
        dgu, dvn, dws, dbs = [], [], [], []
        for g in range(GROUPS):
            sl = slice(g * LANE, (g + 1) * LANE)
            w = jnp.where(tril, ws_ref[g], 0.0).astype(BF16)
            mixed = _dot(w, vnb[:, sl], NN) + bst_ref[:, g:g + 1]
            dgu.append(dsg[:, sl] * mixed)
            dmx = dsg[:, sl] * gu[:, sl]
            dmxb = dmx.astype(BF16)
            dbs.append(jnp.sum(dmx, axis=-1, keepdims=True))
            dws.append(jnp.where(tril, _dot(dmxb, vnb[:, sl], NT), 0.0))
            dvn.append(_dot(w, dmxb, TN))
        dvn = jnp.concatenate(dvn, axis=-1)
        dgu = jnp.concatenate(dgu, axis=-1)
        dxh = dvn * gam
        m1 = jnp.mean(dxh, axis=-1, keepdims=True)
        m2 = jnp.mean(dxh * xhat, axis=-1, keepdims=True)
        dgv = rstd * (dxh - m1 - xhat * m2)
        du = dgu * _gelu_grad(u, tu)
        dv = dgv * _gelu_grad(v, tv)
        duv_ref[...] = jnp.concatenate([du, dv], axis=-1).astype(BF16)
        dlg = jnp.sum(dvn * xhat, axis=0, keepdims=True)
        dlb = jnp.sum(dvn, axis=0, keepdims=True)
        dbst = jnp.concatenate(dbs, axis=-1)

        @pl.when(n == 0)
        def _():
            for g in range(GROUPS):
                dws_ref[g] = dws[g]
            dbst_ref[...] = dbst
            dlg_ref[...] = dlg
            dlb_ref[...] = dlb

        @pl.when(n > 0)
        def _():
            for g in range(GROUPS):
                dws_ref[g] += dws[g]
            dbst_ref[...] += dbst
            dlg_ref[...] += dlg
            dlb_ref[...] += dlb

    vec = pl.BlockSpec((1, GMLP_W), lambda n: (0, 0))
    return _pallas(
        body, name=name, grid=(s // CHUNK,),
        in_specs=[pl.BlockSpec((CHUNK, GMLP_W), lambda n: (n, 0)), pl.BlockSpec((CHUNK, GMLP_W), lambda n: (n, 1)),
                  pl.BlockSpec((CHUNK, GMLP_W), lambda n: (n, 0)), vec, vec,
                  pl.BlockSpec((GROUPS, CHUNK, CHUNK), lambda n: (0, 0, 0)),
                  pl.BlockSpec((CHUNK, GROUPS), lambda n: (0, 0))],
        out_specs=[pl.BlockSpec((CHUNK, 2 * GMLP_W), lambda n: (n, 0)),
                   pl.BlockSpec((GROUPS, CHUNK, CHUNK), lambda n: (0, 0, 0)),
                   pl.BlockSpec((CHUNK, GROUPS), lambda n: (0, 0)), vec, vec],
        out_shape=[_sds((s, 2 * GMLP_W), BF16), _sds((GROUPS, CHUNK, CHUNK), F32), _sds((CHUNK, GROUPS), F32),
                   _sds((1, GMLP_W), F32), _sds((1, GMLP_W), F32)],
        compiler_params=_params(dimension_semantics=("arbitrary",)),
    )(proj, proj, dsg, ln_g, ln_b, w_s, b_st)


def _swa_bwd(name, qr, kr, proj, do, o, lse, sinks, cos4, nsin4, after=()):
    s = qr.shape[0]
    w = CHUNK
    nblk = s // w
    scale = HD ** -0.5
    grp = NQ // NKV

    def body(qj_ref, qn_ref, kj_ref, kp_ref, vj_ref, vp_ref, doj_ref, don_ref, oj_ref, on_ref, lj_ref, ln_ref,
             sink_ref, cos_ref, sin_ref, out_ref, dsink_ref):
        j = pl.program_id(0)
        qj, qn = qj_ref[...].astype(F32), qn_ref[...].astype(F32)
        doj, don = doj_ref[...].astype(F32), don_ref[...].astype(F32)
        kk = jnp.concatenate([kp_ref[...], kj_ref[...]], axis=0)
        vv = jnp.concatenate([vp_ref[...], vj_ref[...]], axis=0).astype(BF16)
        lj, lnx = lj_ref[...], ln_ref[...]
        prod_j = doj * oj_ref[...].astype(F32)
        prod_n = don * on_ref[...].astype(F32)
        valid_j = _band_mask(j > 0)
        valid_n = _band_mask(j + 1 < nblk, prev_only=True)
        lo = lax.broadcasted_iota(jnp.int32, (w, LANE), 1) < HD
        dqs, dsk, dk_g, dv_g = [], [], [], []
        for g in range(NKV):
            kz, vz = _own_head(kk, g), _own_head(vv, g)
            kz_c, vz_c = kz[w:], vz[w:]
            qg_j, qg_n = _stack_heads(qj, g), _stack_heads(qn, g)
            dog_j, dog_n = _stack_heads(doj, g), _stack_heads(don, g)
            l_j, l_n = _stack_rows(lj, g), _stack_rows(lnx, g)
            d_j = _head_lane_sums(_stack_heads(prod_j, g, F32), g)
            d_n = _head_lane_sums(_stack_heads(prod_n, g, F32), g)
            p = jnp.where(valid_j, jnp.exp(_dot(kz, qg_j, NT) * scale - l_j), 0.0)
            ds = (p * (_dot(vz, dog_j, NT) - d_j) * scale).astype(BF16)
            dqs.extend(_unstack_heads(_dot(ds, kz, TN), g))
            p2 = jnp.where(valid_n, jnp.exp(_dot(kz_c, qg_n, NT) * scale - l_n), 0.0)
            ds2 = (p2 * (_dot(vz_c, dog_n, NT) - d_n) * scale).astype(BF16)
            dk_g.append(_dot(ds[w:], qg_j, NN) + _dot(ds2, qg_n, NN))
            dv_g.append(_dot(p[w:].astype(BF16), dog_j, NN) + _dot(p2.astype(BF16), dog_n, NN))
            t = jnp.exp(sink_ref[g] - l_j) * d_j
            dsk.extend([-jnp.sum(t[:, h * w:(h + 1) * w], axis=-1, keepdims=True) for h in range(GRP)])
        cos, nsin = cos_ref[...], sin_ref[...]
        dq = _rope(jnp.concatenate(dqs, axis=-1), cos, nsin)
        dk = _rope(jnp.concatenate([jnp.where(lo, dk_g[2 * m], dk_g[2 * m + 1]) for m in range(NKV // 2)], axis=-1),
                   cos, nsin)
        dv = jnp.concatenate([jnp.where(lo, dv_g[2 * m], dv_g[2 * m + 1]) for m in range(NKV // 2)], axis=-1)
        out_ref[...] = jnp.concatenate([dq, dk, dv], axis=-1).astype(BF16)
        dsink = jnp.concatenate(dsk, axis=-1)

        @pl.when(j == 0)
        def _():
            dsink_ref[...] = dsink

        @pl.when(j > 0)
        def _():
            dsink_ref[...] += dsink

    nxt = lambda j: jnp.minimum(j + 1, nblk - 1)
    prv = lambda j: jnp.maximum(j - 1, 0)
    va = OFF_VA // KV_W
    return _pallas(
        body, after=after, name=name, grid=(nblk,),
        in_specs=[pl.BlockSpec((w, ATT_W), lambda j: (j, 0)), pl.BlockSpec((w, ATT_W), lambda j: (nxt(j), 0)),
                  pl.BlockSpec((w, KV_W), lambda j: (j, 0)), pl.BlockSpec((w, KV_W), lambda j: (prv(j), 0)),
                  pl.BlockSpec((w, KV_W), lambda j: (j, va)), pl.BlockSpec((w, KV_W), lambda j: (prv(j), va)),
                  pl.BlockSpec((w, ATT_W), lambda j: (j, 0)), pl.BlockSpec((w, ATT_W), lambda j: (nxt(j), 0)),
                  pl.BlockSpec((w, ATT_W), lambda j: (j, 0)), pl.BlockSpec((w, ATT_W), lambda j: (nxt(j), 0)),
                  pl.BlockSpec((None, NQ, w), lambda j: (j, 0, 0)),
                  pl.BlockSpec((None, NQ, w), lambda j: (nxt(j), 0, 0)),
                  pl.BlockSpec((NKV, 1, GRP * w), lambda j: (0, 0, 0)),
                  pl.BlockSpec((w, LANE), lambda j: (j, 0)), pl.BlockSpec((w, LANE), lambda j: (j, 0))],
        out_specs=[pl.BlockSpec((w, ATT_W + 2 * KV_W), lambda j: (j, 0)), pl.BlockSpec((1, NQ), lambda j: (0, 0))],
        out_shape=[_sds((s, ATT_W + 2 * KV_W), BF16), _sds((1, NQ), F32)],
        compiler_params=_params(dimension_semantics=("arbitrary",)),
    )(qr, qr, kr, kr, proj, proj, do, do, o, o, lse, lse, sinks, cos4, nsin4)


def _mm_nn(name, a, w, *, out_dtypes, epilogue=_store, bm_pref=1024, bn_pref=1024, after=()):
    m, k = a.shape
    n = w.shape[-1]
    bm, bn = _tile(m, bm_pref), _tile(n, bn_pref)
    tile = pl.BlockSpec((bm, bn), lambda i, j, kk: (i, j))
    return _mm(name, a, w, dims=NN, grid=(m // bm, n // bn, 1),
               a_spec=pl.BlockSpec((bm, k), lambda i, j, kk: (i, 0)),
               b_spec=pl.BlockSpec((k, bn), lambda i, j, kk: (0, j)),
               out_shape=[_sds((m, n), dt) for dt in out_dtypes], out_specs=[tile] * len(out_dtypes),
               epilogue=epilogue, after=after)


def _dw_half(name, a, b, axis, hidx, got=None, after=()):
    s, m = a.shape
    n = b.shape[-1]
    mh, nh = (m // 2, n) if axis == 1 else (m, n // 2)
    bm, bn = _tile(mh, 1024), _tile(nh, 1024)
    nmb, nnb = mh // bm, nh // bn
    if axis == 1:
        a_spec = pl.BlockSpec((s, bm), lambda i, j, h: (0, h[0] * nmb + i))
        b_spec = pl.BlockSpec((s, bn), lambda i, j, h: (0, j))
    else:
        a_spec = pl.BlockSpec((s, bm), lambda i, j, h: (0, i))
        b_spec = pl.BlockSpec((s, bn), lambda i, j, h: (0, h[0] * nnb + j))
    tile = pl.BlockSpec((bm, bn), lambda i, j, h: (i, j))
    n_got, n_after = (0 if got is None else 1), len(after)

    def body(h_ref, a_ref, b_ref, *rest):
        acc = _dot(a_ref[...], b_ref[...], TN)
        if n_got:
            acc = acc + rest[0][...].astype(F32)
        rest[-1][...] = acc.astype(BF16)

    return pl.pallas_call(
        body, name=name,
        grid_spec=pltpu.PrefetchScalarGridSpec(
            num_scalar_prefetch=1, grid=(nmb, nnb),
            in_specs=[a_spec, b_spec] + [tile] * n_got + [ANY] * n_after, out_specs=tile),
        out_shape=_sds((mh, nh), BF16), compiler_params=_params(dimension_semantics=("arbitrary", "arbitrary")),
    )(hidx, a, b, *(() if got is None else (got,)), *after)


def _mm_residual(name, a, w, x, after=()):
    s, k = a.shape
    d = w.shape[-1]
    bm, bn = _tile(s, 1024), _tile(d, 1024 if k <= 2048 else 512)
    tile = pl.BlockSpec((bm, bn), lambda i, j, kk: (i, j))
    return _mm(name, a, w, dims=NN, grid=(s // bm, d // bn, 1),
               a_spec=pl.BlockSpec((bm, k), lambda i, j, kk: (i, 0)),
               b_spec=pl.BlockSpec((k, bn), lambda i, j, kk: (0, j)),
               extras=(x,), extra_specs=(tile,), out_shape=[_sds((s, d), F32)], out_specs=[tile],
               epilogue=_ep_add_scaled, after=after)[0]


def _dw_pieces_half(name, a, pieces, widths, hidx, got=None, after=()):
    s, m = a.shape
    total = sum(widths)
    mh = m // 2
    bm, bn = _tile(mh, 1024), 512
    nmb = mh // bm
    n_got = 0 if got is None else 1
    out, off = None, 0
    for p, (piece, wd) in enumerate(zip(pieces, widths)):
        if isinstance(piece, tuple):
            arr = piece[0]
            b_spec = pl.BlockSpec((None, s, bn), (lambda ix: lambda i, j, h: (ix, 0, j))(piece[1]))
        else:
            arr, b_spec = piece, pl.BlockSpec((s, bn), lambda i, j, h: (0, j))
        tile = pl.BlockSpec((bm, bn), (lambda c: lambda i, j, h: (i, c + j))(off // bn))
        prev = () if out is None else (out,)
        first_after = after if out is None else ()

        def body(h_ref, a_ref, b_ref, *rest):
            acc = _dot(a_ref[...], b_ref[...], TN)
            if n_got:
                acc = acc + rest[0][...].astype(F32)
            rest[-1][...] = acc.astype(BF16)

        out = pl.pallas_call(
            body, name=f"{name}_{p}",
            grid_spec=pltpu.PrefetchScalarGridSpec(
                num_scalar_prefetch=1, grid=(nmb, wd // bn),
                in_specs=[pl.BlockSpec((s, bm), lambda i, j, h: (0, h[0] * nmb + i)), b_spec] + [tile] * n_got
                + [ANY] * (len(prev) + len(first_after)), out_specs=tile),
            out_shape=_sds((mh, total), BF16), input_output_aliases={3 + n_got: 0} if prev else {},
            compiler_params=_params(dimension_semantics=("arbitrary", "arbitrary")),
        )(hidx, a, arr, *(() if got is None else (got,)), *prev, *first_after)
        off += wd
    return out


def _dx_pieces(name, pieces, widths, w, dr, after=()):
    s, d = dr.shape
    iw = w.shape[-1]
    bm, bn = _tile(s, 1024), _tile(d, 512)
    arrs, specs = [], []
    for piece, wd in zip(pieces, widths):
        if isinstance(piece, tuple):
            arrs.append(piece[0])
            specs.append(pl.BlockSpec((None, bm, wd), (lambda idx: lambda i, j: (idx, i, 0))(piece[1])))
        else:
            arrs.append(piece)
            specs.append(pl.BlockSpec((bm, wd), lambda i, j: (i, 0)))
    n = len(arrs)

    def body(*refs):
        w_ref, dr_ref, o_ref = refs[n], refs[n + 1], refs[n + 2]
        acc = ALPHA * dr_ref[...]
        off = 0
        for p, wd in enumerate(widths):
            acc = acc + _dot(refs[p][...], w_ref[:, off:off + wd], NT)
            off += wd
        o_ref[...] = acc

    tile = pl.BlockSpec((bm, bn), lambda i, j: (i, j))
    return _pallas(
        body, after=after, name=name, grid=(s // bm, d // bn),
        in_specs=specs + [pl.BlockSpec((bn, iw), lambda i, j: (j, 0)), tile], out_specs=tile,
        out_shape=_sds((s, d), F32), compiler_params=_params(dimension_semantics=("arbitrary", "arbitrary")),
    )(*arrs, w, dr)


class _Gather:
    def __init__(self, tag, names, fulls, after):
        self.tag, self.names = tag, names
        self.axes = [SHARD_AXIS[n] for n in names]
        self.srcs = []
        self.mk1 = _mk_gather_ici(self.axes)
        self.mk2 = _mk_gather_d2d(self.axes)
        self.n_sem = 3 * len(names)
        self.s1, self.r1, self.lands, self.token = _split_start(
            tag + "_ici_start", self.srcs, [fulls[n] for n in names], self.mk1, self.n_sem, after)

    def forward(self, after=()):
        lands = _split_wait(self.tag + "_ici_wait", self.srcs, self.lands, self.s1, self.r1, self.mk1, after)
        self.s2, self.r2, self.lands, tok = _split_start(self.tag + "_d2d_start", [], lands, self.mk2, self.n_sem)
        return tok

    def done(self, after=()):
        lands = _split_wait(self.tag + "_d2d_wait", [], self.lands, self.s2, self.r2, self.mk2, after)
        return dict(zip(self.names, lands))


class _Reduce:
    def __init__(self, tag, names, parts, mcidx, after=()):
        self.tag, self.names, self.mcidx = tag, names, mcidx
        self.axes = [SHARD_AXIS[n] for n in names]
        self.parts = [parts[n] for n in names]
        self.mk = _mk_swap
        lands = [lax.empty(p.shape, BF16) for p in self.parts]
        self.s, self.r, self.lands, self.token = _split_start(
            tag + "_swap_start", self.parts, lands, self.mk, len(names), after)

    def scatter(self, own, after=()):
        got = _split_wait(self.tag + "_swap_wait", self.parts, self.lands, self.s, self.r, self.mk, after)
        sums = own(dict(zip(self.names, got)))
        self.sums = [sums[n] for n in self.names]
        self.mk = _mk_scatter(self.axes)
        lands = []
        for q, ax in zip(self.sums, self.axes):
            k, n = q.shape
            lands.append(lax.empty((3, k // 4, n) if ax == 0 else (3, k, n // 4), BF16))
        self.s, self.r, self.lands, tok = _split_start(
            self.tag + "_scatter_start", self.sums, lands, self.mk, 3 * len(self.names))
        return tok

    def exchange(self, after=()):
        slots = _split_wait(self.tag + "_scatter_wait", self.sums, self.lands, self.s, self.r, self.mk, after)
        halves = [_sum_half(f"{self.tag}_sum_{n}", q, sl, ax, self.mcidx)
                  for n, q, sl, ax in zip(self.names, self.sums, slots, self.axes)]
        self.mk = _mk_exchange(self.axes)
        self.s, self.r, self.lands, tok = _split_start(
            self.tag + "_exchange_start", [], halves, self.mk, len(self.names))
        return tok

    def done(self, after=()):
        grads = _split_wait(self.tag + "_exchange_wait", [], self.lands, self.s, self.r, self.mk, after)
        return dict(zip(self.names, grads))


def _pack(arrs):
    flat = jnp.concatenate([a.reshape(-1) for a in arrs])
    n = flat.shape[0]
    pad = (-n) % (8 * LANE)
    return jnp.pad(flat, (0, pad)).reshape(-1, LANE)


def _unpack(packed, shapes):
    flat = packed.reshape(-1)
    out, off = [], 0
    for sh in shapes:
        n = math.prod(sh)
        out.append(flat[off:off + n].reshape(sh))
        off += n
    return out


def kernel(x, mem, w_in, b_gate, ln_v_g, ln_v_b, w_s, b_s, sinks, w_br_a, w_br_b, w_o, ln1_g, ln1_b, w_xq, w_xkv, w_xo, ln2_g, ln2_b, w_up, w_down, ln3_g, ln3_b, loss_target, m_w_in, m_b_gate, m_ln_v_g, m_ln_v_b, m_w_s, m_b_s, m_sinks, m_w_br_a, m_w_br_b, m_w_o, m_ln1_g, m_ln1_b, m_w_xq, m_w_xkv, m_w_xo, m_ln2_g, m_ln2_b, m_w_up, m_w_down, m_ln3_g, m_ln3_b, v_w_in, v_b_gate, v_ln_v_g, v_ln_v_b, v_w_s, v_b_s, v_sinks, v_w_br_a, v_w_br_b, v_w_o, v_ln1_g, v_ln1_b, v_w_xq, v_w_xkv, v_w_xo, v_ln2_g, v_ln2_b, v_w_up, v_w_down, v_ln3_g, v_ln3_b):
    env = dict(locals())
    wts = {n: env[n] for n in WEIGHTS}
    mom_m = {n: env["m_" + n] for n in WEIGHTS}
    mom_v = {n: env["v_" + n] for n in WEIGHTS}
    s, d = x.shape[1], x.shape[2]
    dff = 4 * w_up.shape[-1]
    xf = x.reshape(s, d)
    tgt = loss_target.reshape(s, d)
    memf = mem.reshape(mem.shape[1], d)
    ax_x, ax_y, ax_c = lax.axis_index("x"), lax.axis_index("y"), lax.axis_index("c")
    meidx = jnp.reshape(2 * ax_x + ax_y, (1,)).astype(jnp.int32)
    cidx = jnp.reshape(ax_c, (1,)).astype(jnp.int32)
    sidx = 1 - cidx
    mcidx = jnp.concatenate([meidx, cidx])

    inv = 1.0 / (10000.0 ** (jnp.arange(0, HD, 2, dtype=F32) / HD))
    ang = jnp.arange(s, dtype=F32)[:, None] * inv[None, :]
    cos, sin = jnp.cos(ang), jnp.sin(ang)
    cos4 = jnp.tile(cos, (1, 4))
    sin4 = jnp.concatenate([-sin, sin, -sin, sin], axis=-1)
    nsin4 = -sin4

    small = {}
    for n in SMALL:
        w = wts[n]
        if n == "w_s":
            small[n] = [w[l] for l in range(DEPTH)]
        elif n == "b_s":
            small["b_st"] = [w[l].T for l in range(DEPTH)]
        else:
            small[n] = [w[l][None, :] for l in range(DEPTH)]
    small["sink_rows"] = [jnp.repeat(sinks[l].reshape(NKV, GRP), CHUNK, axis=1)[:, None, :] for l in range(DEPTH)]

    fulls = [{}, {}]
    tok = ()
    gathers = [[None] * len(GROUPS_GATHER) for _ in range(DEPTH)]
    for gi, names in enumerate(GROUPS_GATHER):
        for n in names:
            fulls[0][n], fulls[1][n] = _place_own("place_" + n, wts[n], SHARD_AXIS[n], meidx, after=tok)
        gathers[0][gi] = _Gather(f"ag0_{gi}", names, fulls[0], tok)
        tok = (gathers[0][gi].token,)
    for gi, names in enumerate(GROUPS_GATHER):
        gathers[1][gi] = _Gather(f"ag1_{gi}", names, fulls[1], tok)
        tok = (gathers[1][gi].token,)

    xb = _cast2d("cast_x", xf, after=tok)
    memb = _cast2d("cast_mem", memf, after=tok)

    saved = []
    hf, hb = xf, xb
    nxt_tok = gathers[0][0].forward(after=tok)
    for l in range(DEPTH):
        t = f"l{l}_"
        ga, gb, gc, gd = gathers[l]
        full = ga.done(after=(nxt_tok, hb))
        sv = {"xf": hf, "xb": hb}
        proj = _mm_nn(t + "proj", hb, full["w_in"], out_dtypes=[F32], bn_pref=1280)[0]
        tok_b = gb.forward(after=(proj,))
        sg = _gmlp_fwd(t + "gmlp_fwd", proj, small["ln_v_g"][l], small["ln_v_b"][l], small["w_s"][l],
                       small["b_st"][l])
        attn, qr, kr, lse = _swa_fwd(t + "swa_fwd", proj, cos4, sin4, small["sink_rows"][l], after=(tok_b,))
        full.update(gb.done(after=(attn,)))
        merged, ya, yb, sa, sb = _gate_fwd(t + "gate_fwd", sg, attn, full["w_br_a"], full["w_br_b"], proj,
                                           small["b_gate"][l], d)
        sv.update(sa=sa, sb=sb)
        tok_c = gc.forward(after=(merged,))
        r1 = _mm_residual(t + "o", merged, full["w_o"], hf, after=(tok_c,))
        x1, x1b = _ln_fwd(t + "ln1", r1, small["ln1_g"][l], small["ln1_b"][l])
        kv = _mm_nn(t + "xkv", memb, full["w_xkv"], out_dtypes=[BF16])[0]
        q, o, r2, x2, x2b = _xattn_fwd(t + "xattn_fwd", x1b, x1, full["w_xq"], kv, full["w_xo"],
                                       small["ln2_g"][l], small["ln2_b"][l])
        full.update(gc.done(after=(x2b,)))
        tok_d = gd.forward(after=(x2b,))

        def ep_up(acc, ex, outs):
            outs[0][...] = acc.astype(BF16)
            rl = jnp.maximum(acc, 0.0)
            outs[1][...] = (rl * rl).astype(BF16)

        h, a = _mm_nn(t + "up", x2b, full["w_up"], out_dtypes=[BF16, BF16], epilogue=ep_up, after=(tok_d,))
        full.update(gd.done(after=(h,)))
        nxt_tok = gathers[l + 1][0].forward(after=(h,)) if l + 1 < DEPTH else None
        r3 = _mm_residual(t + "down", a, full["w_down"], x2, after=() if nxt_tok is None else (nxt_tok,))
        sv.update(proj=proj, sg=sg, attn=attn, qr=qr, kr=kr, lse=lse, merged=merged, ya=ya, yb=yb, r1=r1, x1=x1,
                  x1b=x1b, kv=kv, q=q, o=o, r2=r2, x2b=x2b, h=h, a=a, r3=r3, full=full)
        saved.append(sv)
        if l + 1 < DEPTH:
            hf, hb = _ln_fwd(t + "ln3", r3, small["ln3_g"][l], small["ln3_b"][l])

    small_g = [None] * DEPTH
    grads = [{}, {}]
    pend_a = None
    pend_b = None
    for l in reversed(range(DEPTH)):
        t = f"l{l}_"
        sv = saved[l]
        full = sv["full"]
        sgo = {}
        if l == DEPTH - 1:
            dr3, dr3b, sgo["ln3_g"], sgo["ln3_b"], loss11 = _loss_ln_bwd(
                "loss_ln3_bwd", sv["r3"], small["ln3_g"][l], small["ln3_b"][l], tgt)
            loss = lax.psum(loss11[0, 0], ("x", "y", "c"))
        else:
            dr3, dr3b, sgo["ln3_g"], sgo["ln3_b"] = _ln_bwd(t + "ln3_bwd", g, sv["r3"], small["ln3_g"][l],
                                                            after=(tok_a,))
        bm, bn = _tile(s, 1024), _tile(dff, 1024)

        def ep_dh(acc, ex, outs):
            outs[0][...] = (acc * (2.0 * jnp.maximum(ex[0][...].astype(F32), 0.0))).astype(BF16)

        tile = pl.BlockSpec((bm, bn), lambda i, j, k: (i, j))
        dh = _mm(t + "dh", dr3b, full["w_down"], dims=NT, grid=(s // bm, dff // bn, 1),
                 a_spec=pl.BlockSpec((bm, d), lambda i, j, k: (i, 0)),
                 b_spec=pl.BlockSpec((bn, d), lambda i, j, k: (j, 0)),
                 extras=(sv["h"],), extra_specs=(tile,), out_shape=[_sds((s, dff), BF16)], out_specs=[tile],
                 epilogue=ep_dh)[0]
        if pend_a is not None:
            tok_pa = pend_a.exchange(after=(dh,))
            grads[l + 1].update(pend_b.done(after=(dh,)))

        def halves(specs, hidx, got=None, after=()):
            out = {}
            for i, (n, a_, b_) in enumerate(specs):
                out[n] = _dw_half(f"{t}d{n}_{'s' if got is None else 'o'}", a_, b_, SHARD_AXIS[n], hidx,
                                  None if got is None else got[n], after if i == 0 else ())
            return out

        specs_c = [("w_down", sv["a"], dr3b), ("w_up", sv["x2b"], dh)]
        red_c = _Reduce(t + "rs_c", GROUPS_FWD[2],
                        halves(specs_c, sidx, after=() if pend_a is None else (tok_pa,)), mcidx)
        bm2, bn2 = _tile(s, 1024), _tile(d, 512)
        tile2 = pl.BlockSpec((bm2, bn2), lambda i, j, k: (i, j))
        dx2 = _mm(t + "dx2", dh, full["w_up"], dims=NT, grid=(s // bm2, d // bn2, 1),
                  a_spec=pl.BlockSpec((bm2, dff), lambda i, j, k: (i, 0)),
                  b_spec=pl.BlockSpec((bn2, dff), lambda i, j, k: (j, 0)),
                  extras=(dr3,), extra_specs=(tile2,), out_shape=[_sds((s, d), F32)], out_specs=[tile2],
                  epilogue=_ep_add_scaled, after=(red_c.token,))[0]
        tok_c = red_c.scatter(lambda got: halves(specs_c, cidx, got), after=(dx2,))
        if pend_a is not None:
            grads[l + 1].update(pend_a.done(after=(dx2,)))
            pend_a = None

        dr2, dr2b, sgo["ln2_g"], sgo["ln2_b"] = _ln_bwd(t + "ln2_bwd", dx2, sv["r2"], small["ln2_g"][l],
                                                        after=(tok_c,))
        dx1, dq, dkv = _xattn_bwd(t + "xattn_bwd", dr2b, dr2, sv["q"], sv["kv"], full["w_xo"], full["w_xq"])

        dr1, dr1b, sgo["ln1_g"], sgo["ln1_b"] = _ln_bwd(t + "ln1_bwd", dx1, sv["r1"], small["ln1_g"][l])
        dya, dyb, dgate, dba, dbb = _gate_bwd(t + "gate_bwd", dr1b, full["w_o"], sv["sa"], sv["sb"], sv["ya"],
                                              sv["yb"], d)
        sgo["b_gate"] = jnp.concatenate([dba, dbb], axis=-1)
        specs_b = [("w_o", sv["merged"], dr1b), ("w_br_a", sv["sg"], dya), ("w_br_b", sv["attn"], dyb),
                   ("w_xo", sv["o"], dr2b), ("w_xq", sv["x1b"], dq),
                   ("w_xkv", memb, _cast2d(t + "dkv_cast", dkv))]
        red_b = _Reduce(t + "rs_b", GROUPS_FWD[1], halves(specs_b, sidx), mcidx)

        def dbranch(name, dyx, w, after):
            return _mm(name, dyx, w, dims=NT, grid=(s // bm, 1, 1),
                       a_spec=pl.BlockSpec((bm, d), lambda i, j, k: (i, 0)),
                       b_spec=pl.BlockSpec((w.shape[0], d), lambda i, j, k: (0, 0)),
                       out_shape=[_sds((s, w.shape[0]), BF16)],
                       out_specs=[pl.BlockSpec((bm, w.shape[0]), lambda i, j, k: (i, 0))],
                       epilogue=_store, after=after)[0]

        dsg = dbranch(t + "dsg", dya, full["w_br_a"], (red_b.token,))
        dattn = dbranch(t + "dattn", dyb, full["w_br_b"], ())
        tok_c = red_c.exchange(after=(dattn, dsg))
        tok_b = red_b.scatter(lambda got: halves(specs_b, cidx, got), after=(dattn, dsg))
        duv, sgo["w_s"], dbst, dlg, dlb = _gmlp_bwd(t + "gmlp_bwd", sv["proj"], dsg, small["ln_v_g"][l],
                                                    small["ln_v_b"][l], small["w_s"][l], small["b_st"][l])
        sgo["b_s"] = dbst.T
        sgo["ln_v_g"], sgo["ln_v_b"] = dlg, dlb
        dqkv, sgo["sinks"] = _swa_bwd(t + "swa_bwd", sv["qr"], sv["kr"], sv["proj"], dattn, sv["attn"], sv["lse"],
                                      small["sink_rows"][l], cos4, nsin4, after=(tok_b, tok_c))
        grads[l].update(red_c.done(after=(dqkv,)))
        pieces = (duv, dqkv, (dgate, 0), (dgate, 1))
        widths = (2 * GMLP_W, ATT_W + 2 * KV_W, d, d)
        tok_b = red_b.exchange(after=(dqkv, duv))
        red_a = _Reduce(t + "rs_a", GROUPS_FWD[0],
                        {"w_in": _dw_pieces_half(t + "dw_in_s", sv["xb"], pieces, widths, sidx, after=(tok_b,))},
                        mcidx)
        g = _dx_pieces(t + "dx0", pieces, widths, full["w_in"], dr1, after=(red_a.token,))
        tok_a = red_a.scatter(
            lambda got: {"w_in": _dw_pieces_half(t + "dw_in_o", sv["xb"], pieces, widths, cidx, got["w_in"])},
            after=(g,))
        pend_a, pend_b = red_a, red_b
        small_g[l] = sgo
    grad_x = g.reshape(x.shape)

    big_out = {}

    def adam_layer(l, names, after):
        done = []
        for n in names:
            prev = big_out.get(n)
            big_out[n] = _adamw(f"adamw{l}_{n}", wts[n], grads[l][n], mom_m[n], mom_v[n], l, prev, after=after)
            done.append(big_out[n][0])
        return done

    shapes = [wts[n].shape for n in SMALL]
    packed_g = _pack([jnp.stack([small_g[l][n].reshape(wts[n].shape[1:]) for l in range(DEPTH)]) for n in SMALL])
    me8 = jnp.reshape(4 * ax_x + 2 * ax_y + ax_c, (1,)).astype(jnp.int32)
    ar_s, ar_r, ar_land, tok_ar = _split_start("ar_start", [packed_g], [_place_slot("ar_place", packed_g, me8)],
                                               _mk_small, 7, after=(tok_a,))
    fill = []
    for names in GROUPS_FWD:
        fill += adam_layer(1, names, (tok_ar,))
    grads[0].update(pend_b.done(after=tuple(fill)))
    fill += adam_layer(0, GROUPS_FWD[1], (tok_ar,))
    ar_land = _split_wait("ar_wait", [packed_g], ar_land, ar_s, ar_r, _mk_small, after=tuple(fill))
    packed_g = _sum_slots("ar_sum", ar_land[0])
    pw, pm, pv = (_pack([src[n] for n in SMALL]) for src in (wts, mom_m, mom_v))
    small4 = _adamw("adamw_small", pw[None], packed_g, pm[None], pv[None], 0)
    small_out = [dict(zip(SMALL, _unpack(a[0], shapes))) for a in small4]
    ws_row0 = sum(math.prod(wts[n].shape) for n in SMALL[:SMALL.index("w_s")]) // LANE
    ws_rows = math.prod(w_s.shape) // LANE
    as_rows = lambda a: a.reshape(1, ws_rows, LANE)
    ws4 = _adamw("adamw_w_s", as_rows(w_s), packed_g[ws_row0:ws_row0 + ws_rows], as_rows(m_w_s), as_rows(v_w_s), 0)
    for out, a in zip(small_out, ws4):
        out["w_s"] = a.reshape(w_s.shape)
    tok_a = pend_a.exchange(after=(small4[0], ws4[0]))
    fill = adam_layer(0, GROUPS_FWD[2], (tok_a,))
    grads[0].update(pend_a.done(after=tuple(fill)))
    adam_layer(0, GROUPS_FWD[0], ())

    def pick(kind, n):
        return big_out[n][kind] if n in big_out else small_out[kind][n]

    return (loss, grad_x, *[pick(0, n) for n in WEIGHTS], *[pick(1, n) for n in WEIGHTS],
            *[pick(2, n) for n in WEIGHTS], *[pick(3, n) for n in WEIGHTS])
```

```python
import math

import jax
import jax.numpy as jnp
from jax import lax
from jax.experimental import pallas as pl
from jax.experimental.pallas import tpu as pltpu

F32 = jnp.float32
BF16 = jnp.bfloat16
MESH = pl.DeviceIdType.MESH
ANY = pl.BlockSpec(memory_space=pl.ANY)
HBM = pl.BlockSpec(memory_space=pltpu.HBM)
SEM = pl.BlockSpec(memory_space=pltpu.SEMAPHORE)
VMEM_SPEC = pl.BlockSpec(memory_space=pltpu.VMEM)
EFFECT = pltpu.SideEffectType.DATAFLOW_SIDE_EFFECTING

DEPTH = 2
CHUNK = 128
GMLP_W = 1024
GROUPS = 8
NQ, NKV, HD = 16, 4, 64
ATT_W = NQ * HD
KV_W = NKV * HD
XH, XHD = 4, 128
X_W = XH * XHD
LN_EPS = 1e-5
ALPHA = (2 * DEPTH) ** 0.25
OFF_Q = 2 * GMLP_W
OFF_K = OFF_Q + ATT_W
OFF_VA = OFF_K + KV_W
OFF_GA = OFF_VA + KV_W
NEG = -1e30

ADAM_LR, ADAM_B1, ADAM_B2, ADAM_EPS, ADAM_WD, ADAM_STEP = 0.001, 0.9, 0.999, 1e-08, 0.01, 10

V7X_VMEM_BYTES = 64 * 1024 * 1024
VMEM_LIMIT = V7X_VMEM_BYTES - 4 * 1024 * 1024
LANE = 128

BIG = ("w_in", "w_br_a", "w_br_b", "w_o", "w_xq", "w_xkv", "w_xo", "w_up", "w_down")
SHARD_AXIS = {"w_in": 1, "w_br_a": 1, "w_br_b": 1, "w_o": 0, "w_xq": 0, "w_xkv": 0, "w_xo": 1,
              "w_up": 1, "w_down": 0}
GROUPS_GATHER = (("w_in",), ("w_br_a", "w_br_b", "w_o", "w_xq", "w_xkv", "w_xo"), ("w_up",), ("w_down",))
GROUPS_FWD = (("w_in",), ("w_br_a", "w_br_b", "w_o", "w_xq", "w_xkv", "w_xo"), ("w_up", "w_down"))
SMALL = ("b_gate", "ln_v_g", "ln_v_b", "w_s", "b_s", "sinks", "ln1_g", "ln1_b", "ln2_g", "ln2_b",
         "ln3_g", "ln3_b")
WEIGHTS = ("w_in", "b_gate", "ln_v_g", "ln_v_b", "w_s", "b_s", "sinks", "w_br_a", "w_br_b", "w_o",
           "ln1_g", "ln1_b", "w_xq", "w_xkv", "w_xo", "ln2_g", "ln2_b", "w_up", "w_down", "ln3_g", "ln3_b")


def _pallas(body, after=(), **kw):
    n_after = len(after)
    if not n_after:
        return pl.pallas_call(body, **kw)
    n_in = len(kw["in_specs"])
    kw["in_specs"] = list(kw["in_specs"]) + [ANY] * n_after

    def tied(*refs):
        return body(*refs[:n_in], *refs[n_in + n_after:])

    call = pl.pallas_call(tied, **kw)
    return lambda *ops: call(*ops, *after)


def _params(**kw):
    return pltpu.CompilerParams(vmem_limit_bytes=VMEM_LIMIT, **kw)


def _tile(dim, pref, unit=LANE):
    best = None
    t = unit
    while t <= min(dim, pref):
        if dim % t == 0:
            best = t
        t += unit
    return best if best is not None else dim


def _dot(a, b, dims):
    return lax.dot_general(a, b, (dims, ((), ())), preferred_element_type=F32)


NN = ((1,), (0,))
NT = ((1,), (1,))
TN = ((0,), (0,))


def _bf(x):
    return x if x.dtype == BF16 else x.astype(BF16)


def _sds(shape, dtype):
    return jax.ShapeDtypeStruct(shape, dtype)


def _mm(name, a, b, *, dims, grid, a_spec, b_spec, out_shape, out_specs, epilogue,
        extras=(), extra_specs=(), after=()):
    assert grid[2] == 1
    n_ex, n_out = len(extras), len(out_shape)

    def body(*refs):
        ex = refs[2:2 + n_ex]
        outs = refs[2 + n_ex:2 + n_ex + n_out]
        epilogue(_dot(_bf(refs[0][...]), _bf(refs[1][...]), dims), ex, outs)

    return _pallas(
        body, after=after, name=name, grid=grid, in_specs=[a_spec, b_spec, *extra_specs], out_specs=list(out_specs),
        out_shape=list(out_shape), compiler_params=_params(dimension_semantics=("arbitrary",) * 3),
    )(a, b, *extras)


def _store(acc, ex, outs):
    for o in outs:
        o[...] = acc.astype(o.dtype)


def _ln_rows(r, g, b):
    mu = jnp.mean(r, axis=-1, keepdims=True)
    xc = r - mu
    var = jnp.mean(xc * xc, axis=-1, keepdims=True)
    rstd = lax.rsqrt(var + LN_EPS)
    xhat = xc * rstd
    return xhat * g + b, xhat, rstd


def _ep_add_scaled(acc, ex, outs):
    outs[0][...] = acc + ALPHA * ex[0][...]


def _ln_fwd(name, r, g, b):
    s, d = r.shape
    bm = _tile(s, 512)

    def body(r_ref, g_ref, b_ref, y_ref, yb_ref):
        y, _, _ = _ln_rows(r_ref[...], g_ref[...], b_ref[...])
        y_ref[...] = y
        yb_ref[...] = y.astype(BF16)

    row = pl.BlockSpec((bm, d), lambda i: (i, 0))
    vec = pl.BlockSpec((1, d), lambda i: (0, 0))
    return _pallas(body, name=name, grid=(s // bm,), in_specs=[row, vec, vec], out_specs=[row, row],
                   out_shape=[_sds((s, d), F32), _sds((s, d), BF16)], compiler_params=_params())(r, g, b)


_GC = math.sqrt(2.0 / math.pi)


def _gelu(x):
    t = jnp.tanh(_GC * (x + 0.044715 * (x * x * x)))
    return 0.5 * x * (1.0 + t), t


def _gelu_grad(x, t):
    return 0.5 * (1.0 + t) + 0.5 * x * (1.0 - t * t) * (_GC * (1.0 + 3.0 * 0.044715 * x * x))


def _sigmoid(x):
    return 1.0 / (1.0 + jnp.exp(-x))


GRP = NQ // NKV


def _band_mask(prev_ok, prev_only=False):
    rows = CHUNK if prev_only else 2 * CHUNK
    key = lax.broadcasted_iota(jnp.int32, (rows, GRP * CHUNK), 0)
    qry = jnp.bitwise_and(lax.broadcasted_iota(jnp.int32, (rows, GRP * CHUNK), 1), CHUNK - 1)
    prev = jnp.logical_and(jnp.logical_and(key < CHUNK, key > qry), prev_ok)
    if prev_only:
        return prev
    return jnp.logical_or(prev, jnp.logical_and(key >= CHUNK, key - CHUNK <= qry))


def _pair(x, g):
    return x[:, (g // 2) * LANE:(g // 2 + 1) * LANE]


def _own_head(x, g):
    xp = _pair(x, g)
    lane = lax.broadcasted_iota(jnp.int32, xp.shape, 1)
    lo = (g % 2) * HD
    return jnp.where(jnp.logical_and(lane >= lo, lane < lo + HD), xp, jnp.zeros_like(xp))


def _stack_heads(x, g, dtype=BF16):
    a = x[:, g * GRP * HD:g * GRP * HD + LANE]
    b = x[:, g * GRP * HD + LANE:(g + 1) * GRP * HD]
    ar, br = pltpu.roll(a, HD, 1), pltpu.roll(b, HD, 1)
    parts = [a, ar, b, br] if g % 2 == 0 else [ar, a, br, b]
    return jnp.concatenate(parts, axis=0).astype(dtype)


def _unstack_heads(og, g):
    o = [og[h * CHUNK:(h + 1) * CHUNK] for h in range(GRP)]
    lo = lax.broadcasted_iota(jnp.int32, (CHUNK, LANE), 1) < HD
    if g % 2 == 0:
        x0, x1, x2, x3 = o[0], pltpu.roll(o[1], HD, 1), o[2], pltpu.roll(o[3], HD, 1)
    else:
        x0, x1, x2, x3 = pltpu.roll(o[0], HD, 1), o[1], pltpu.roll(o[2], HD, 1), o[3]
    return [jnp.where(lo, x0, x1), jnp.where(lo, x2, x3)]


def _stack_rows(x, g):
    return jnp.concatenate([x[g * GRP + h:g * GRP + h + 1] for h in range(GRP)], axis=-1)


def _head_lane_sums(x, g):
    lane = lax.broadcasted_iota(jnp.int32, (8, LANE), 1)
    lo_lane = (g % 2) * HD
    sel = jnp.where(jnp.logical_and(lane >= lo_lane, lane < lo_lane + HD), 1.0, 0.0).astype(BF16)
    hi = x.astype(BF16)
    lo = (x - hi.astype(F32)).astype(BF16)
    return (_dot(sel, hi, NT) + _dot(sel, lo, NT))[0:1]


def _rope(x, cos, sin_signed):
    w = x.shape[-1]
    lane = lax.broadcasted_iota(jnp.int32, x.shape, 1)
    first = (lane % HD) < (HD // 2)
    partner = jnp.where(first, pltpu.roll(x, w - HD // 2, 1), pltpu.roll(x, HD // 2, 1))
    reps = w // LANE
    return x * jnp.tile(cos, (1, reps)) + partner * jnp.tile(sin_signed, (1, reps))


def _cast2d(name, x, after=()):
    s, d = x.shape
    bm = _tile(s, 512, 8)

    def body(x_ref, o_ref):
        o_ref[...] = x_ref[...].astype(BF16)

    spec = pl.BlockSpec((bm, d), lambda i: (i, 0))
    return _pallas(body, after=after, name=name, grid=(s // bm,), in_specs=[spec], out_specs=spec,
                   out_shape=_sds(x.shape, BF16), compiler_params=_params())(x)


def _place():
    x, y, c = lax.axis_index("x"), lax.axis_index("y"), lax.axis_index("c")
    chips = [(1 - x, y), (x, 1 - y), (1 - x, 1 - y)]
    return x, y, c, chips


def _cut(ref, axis, chip=None, half=None):
    k, n = ref.shape[-2], ref.shape[-1]
    rows, cols = slice(None), slice(None)
    if chip is not None:
        if axis == 0:
            rows = pl.ds(pl.multiple_of(chip * (k // 4), 8), k // 4)
        else:
            cols = pl.ds(pl.multiple_of(chip * (n // 4), LANE), n // 4)
    if half is not None:
        if axis == 0:
            cols = pl.ds(pl.multiple_of(half * (n // 2), LANE), n // 2)
        else:
            rows = pl.ds(pl.multiple_of(half * (k // 2), 8), k // 2)
    return ref.at[rows, cols]


def _split_start(name, srcs, lands, make, n_sem, after=()):
    ns, nl, na = len(srcs), len(lands), len(after)

    def body(*refs):
        src, land = refs[:ns], refs[ns:ns + nl]
        outs = refs[ns + nl + na:]
        for out_cp, _ in make(src, land, outs[0], outs[1]):
            out_cp.start()
        outs[-1][...] = jnp.zeros_like(outs[-1])

    res = pl.pallas_call(
        body, name=name, in_specs=[HBM] * (ns + nl) + [ANY] * na,
        out_specs=[SEM, SEM] + [HBM] * nl + [VMEM_SPEC],
        out_shape=[pltpu.SemaphoreType.DMA((n_sem,)), pltpu.SemaphoreType.DMA((n_sem,))]
        + [pltpu.HBM(a.shape, a.dtype) for a in lands] + [_sds((8, LANE), F32)],
        input_output_aliases={ns + i: 2 + i for i in range(nl)},
        compiler_params=pltpu.CompilerParams(has_side_effects=EFFECT),
    )(*[pltpu.with_memory_space_constraint(a, pltpu.HBM) for a in (*srcs, *lands)], *after)
    return res[0], res[1], list(res[2:2 + nl]), res[-1]


def _split_wait(name, srcs, lands, ssem, rsem, make, after=()):
    ns, nl, na = len(srcs), len(lands), len(after)

    def body(*refs):
        src, land = refs[:ns], refs[ns:ns + nl]
        s_ref, r_ref = refs[ns + nl], refs[ns + nl + 1]
        pairs = make(src, land, s_ref, r_ref)
        for _, in_cp in pairs:
            in_cp.wait_recv()
        for out_cp, _ in pairs:
            out_cp.wait_send()

    res = pl.pallas_call(
        body, name=name, in_specs=[HBM] * (ns + nl) + [SEM, SEM] + [ANY] * na,
        out_specs=[HBM] * nl, out_shape=[pltpu.HBM(a.shape, a.dtype) for a in lands],
        input_output_aliases={ns + i: i for i in range(nl)},
        compiler_params=pltpu.CompilerParams(has_side_effects=EFFECT),
    )(*srcs, *lands, ssem, rsem, *after)
    return list(res)


def _rcopy(src, dst, ssem, rsem, k, dev):
    return pltpu.make_async_remote_copy(src_ref=src, dst_ref=dst, send_sem=ssem.at[k], recv_sem=rsem.at[k],
                                        device_id=dev, device_id_type=MESH)


def _mk_gather_ici(axes):
    def make(src, land, ssem, rsem):
        x, y, c, chips = _place()
        me = 2 * x + y
        pairs = []
        for w, ax in enumerate(axes):
            mine = _cut(land[w], ax, chip=me, half=c)
            for j, (px, py) in enumerate(chips):
                dev = (px, py, c)
                got = _cut(land[w], ax, chip=2 * px + py, half=c)
                pairs.append((_rcopy(mine, mine, ssem, rsem, 3 * w + j, dev),
                              _rcopy(got, got, ssem, rsem, 3 * w + j, dev)))
        return pairs
    return make


def _mk_gather_d2d(axes):
    def make(src, land, ssem, rsem):
        x, y, c, chips = _place()
        sib = (x, y, 1 - c)
        pairs = []
        for w, ax in enumerate(axes):
            for j, (px, py) in enumerate(chips):
                have = _cut(land[w], ax, chip=2 * px + py, half=c)
                want = _cut(land[w], ax, chip=2 * px + py, half=1 - c)
                pairs.append((_rcopy(have, have, ssem, rsem, 3 * w + j, sib),
                              _rcopy(want, want, ssem, rsem, 3 * w + j, sib)))
        return pairs
    return make


def _mk_swap(src, land, ssem, rsem):
    x, y, c, _ = _place()
    pairs = []
    for w in range(len(src)):
        cp = _rcopy(src[w], land[w], ssem, rsem, w, (x, y, 1 - c))
        pairs.append((cp, cp))
    return pairs


def _mk_scatter(axes):
    def make(src, land, ssem, rsem):
        x, y, c, chips = _place()
        pairs = []
        for w, ax in enumerate(axes):
            for j, (px, py) in enumerate(chips):
                cp = _rcopy(_cut(src[w], ax, chip=2 * px + py), land[w].at[j], ssem, rsem, 3 * w + j, (px, py, c))
                pairs.append((cp, cp))
        return pairs
    return make


def _mk_exchange(axes):
    def make(src, land, ssem, rsem):
        x, y, c, _ = _place()
        sib = (x, y, 1 - c)
        pairs = []
        for w, ax in enumerate(axes):
            have = _cut(land[w], ax, half=c)
            want = _cut(land[w], ax, half=1 - c)
            pairs.append((_rcopy(have, have, ssem, rsem, w, sib), _rcopy(want, want, ssem, rsem, w, sib)))
        return pairs
    return make


def _place_own(name, shard, axis, meidx, after=()):
    _, r, c = shard.shape
    full = (4 * r, c) if axis == 0 else (r, 4 * c)
    br = _tile(r, 512, 8)
    nb = r // br
    if axis == 0:
        ospec = pl.BlockSpec((br, c), lambda i, me: (me[0] * nb + i, 0))
    else:
        ospec = pl.BlockSpec((br, c), lambda i, me: (i, me[0]))
    n_after = len(after)

    def body(me_ref, s_ref, *rest):
        o0_ref, o1_ref = rest[n_after:]
        o0_ref[...] = s_ref[0].astype(BF16)
        o1_ref[...] = s_ref[1].astype(BF16)

    return pl.pallas_call(
        body, name=name,
        grid_spec=pltpu.PrefetchScalarGridSpec(
            num_scalar_prefetch=1, grid=(nb,),
            in_specs=[pl.BlockSpec((2, br, c), lambda i, me: (0, i, 0))] + [ANY] * n_after,
            out_specs=[ospec, ospec]),
        out_shape=[_sds(full, BF16)] * 2, compiler_params=_params(),
    )(meidx, shard, *after)


def _sum_half(name, own, slots, axis, mc):
    _, r, cc = slots.shape
    br = _tile(r, 256, 8)
    nb = r // br
    if axis == 0:
        own_spec = pl.BlockSpec((br, cc), lambda i, mc: (mc[0] * nb + i, 0))
        out_spec = pl.BlockSpec((br, cc), lambda i, mc: (i, mc[1]))
        shape = (r, 2 * cc)
    else:
        own_spec = pl.BlockSpec((br, cc), lambda i, mc: (i, mc[0]))
        out_spec = pl.BlockSpec((br, cc), lambda i, mc: (mc[1] * nb + i, 0))
        shape = (2 * r, cc)

    def body(mc_ref, own_ref, s_ref, o_ref):
        acc = own_ref[...].astype(F32)
        for i in range(3):
            acc = acc + s_ref[i].astype(F32)
        o_ref[...] = acc

    return pl.pallas_call(
        body, name=name,
        grid_spec=pltpu.PrefetchScalarGridSpec(
            num_scalar_prefetch=1, grid=(nb,),
            in_specs=[own_spec, pl.BlockSpec((3, br, cc), lambda i, mc: (0, i, 0))], out_specs=out_spec),
        out_shape=_sds(shape, F32), compiler_params=_params(),
    )(mc, own, slots)


def _mk_small(src, land, ssem, rsem):
    x, y, c, _ = _place()
    me = 4 * x + 2 * y + c
    pairs = []
    for k in range(1, 8):
        peer = (1 - x if k & 4 else x, 1 - y if k & 2 else y, 1 - c if k & 1 else c)
        got = land[0].at[4 * peer[0] + 2 * peer[1] + peer[2]]
        pairs.append((_rcopy(src[0], land[0].at[me], ssem, rsem, k - 1, peer),
                      _rcopy(got, got, ssem, rsem, k - 1, peer)))
    return pairs


def _place_slot(name, packed, me8):
    rows, lanes = packed.shape
    br = _tile(rows, 512, 8)

    def body(me_ref, p_ref, o_ref):
        o_ref[...] = p_ref[...]

    return pl.pallas_call(
        body, name=name,
        grid_spec=pltpu.PrefetchScalarGridSpec(
            num_scalar_prefetch=1, grid=(rows // br,),
            in_specs=[pl.BlockSpec((br, lanes), lambda i, me: (i, 0))],
            out_specs=pl.BlockSpec((None, br, lanes), lambda i, me: (me[0], i, 0))),
        out_shape=_sds((8, rows, lanes), F32), compiler_params=_params(),
    )(me8, packed)


def _sum_slots(name, slots):
    _, rows, lanes = slots.shape
    br = _tile(rows, 512, 8)

    def body(s_ref, o_ref):
        acc = s_ref[0]
        for i in range(1, 8):
            acc = acc + s_ref[i]
        o_ref[...] = acc

    return pl.pallas_call(
        body, name=name, grid=(rows // br,), in_specs=[pl.BlockSpec((8, br, lanes), lambda i: (0, i, 0))],
        out_specs=pl.BlockSpec((br, lanes), lambda i: (i, 0)), out_shape=_sds((rows, lanes), F32),
        compiler_params=_params(),
    )(slots)


def _adamw_math(w, g, m, v):
    m2 = ADAM_B1 * m + (1.0 - ADAM_B1) * g
    v2 = ADAM_B2 * v + (1.0 - ADAM_B2) * (g * g)
    m_hat = m2 / (1.0 - ADAM_B1 ** ADAM_STEP)
    v_hat = v2 / (1.0 - ADAM_B2 ** ADAM_STEP)
    delta = -ADAM_LR * (m_hat / (jnp.sqrt(v_hat) + ADAM_EPS) + ADAM_WD * w)
    return delta, m2, v2


def _adamw(name, w, g, m, v, layer, prev=None, after=()):
    _, r, c = w.shape
    br = _tile(r, 256, 8)
    n_prev = 0 if prev is None else 4

    def body(*refs):
        w_ref, g_ref, m_ref, v_ref = refs[:4]
        go_ref, d_ref, mo_ref, vo_ref = refs[4 + n_prev:]
        gg = g_ref[...]
        delta, m2, v2 = _adamw_math(w_ref[...], gg, m_ref[...], v_ref[...])
        go_ref[...] = gg
        d_ref[...] = delta
        mo_ref[...] = m2
        vo_ref[...] = v2

    spec = pl.BlockSpec((None, br, c), lambda i: (layer, i, 0))
    return _pallas(
        body, after=after, name=name, grid=(r // br,),
        in_specs=[spec, pl.BlockSpec((br, c), lambda i: (i, 0)), spec, spec] + [ANY] * n_prev,
        out_specs=[spec] * 4, out_shape=[_sds(w.shape, F32)] * 4,
        input_output_aliases={4 + i: i for i in range(n_prev)}, compiler_params=_params(),
    )(w, g, m, v, *(prev or ()))


def _gmlp_fwd(name, proj, ln_g, ln_b, w_s, b_st):
    s = proj.shape[0]

    def body(u_ref, v_ref, g_ref, b_ref, ws_ref, bst_ref, sg_ref):
        gu, _ = _gelu(u_ref[...])
        gv, _ = _gelu(v_ref[...])
        vn, _, _ = _ln_rows(gv, g_ref[...], b_ref[...])
        vn = vn.astype(BF16)
        row = lax.broadcasted_iota(jnp.int32, (CHUNK, CHUNK), 0)
        col = lax.broadcasted_iota(jnp.int32, (CHUNK, CHUNK), 1)
        tril = col <= row
        outs = []
        for g in range(GROUPS):
            sl = slice(g * LANE, (g + 1) * LANE)
            w = jnp.where(tril, ws_ref[g], 0.0).astype(BF16)
            mixed = _dot(w, vn[:, sl], NN) + bst_ref[:, g:g + 1]
            outs.append(gu[:, sl] * mixed)
        sg_ref[...] = jnp.concatenate(outs, axis=-1).astype(BF16)

    return _pallas(
        body, name=name, grid=(s // CHUNK,),
        in_specs=[pl.BlockSpec((CHUNK, GMLP_W), lambda n: (n, 0)), pl.BlockSpec((CHUNK, GMLP_W), lambda n: (n, 1)),
                  pl.BlockSpec((1, GMLP_W), lambda n: (0, 0)), pl.BlockSpec((1, GMLP_W), lambda n: (0, 0)),
                  pl.BlockSpec((GROUPS, CHUNK, CHUNK), lambda n: (0, 0, 0)),
                  pl.BlockSpec((CHUNK, GROUPS), lambda n: (0, 0))],
        out_specs=pl.BlockSpec((CHUNK, GMLP_W), lambda n: (n, 0)),
        out_shape=_sds((s, GMLP_W), BF16), compiler_params=_params(),
    )(proj, proj, ln_g, ln_b, w_s, b_st)


def _swa_fwd(name, proj, cos4, sin4, sinks, after=()):
    s = proj.shape[0]
    w = CHUNK
    scale = HD ** -0.5

    def body(q_ref, k_ref, v_ref, cos_ref, sin_ref, sink_ref, o_ref, qr_ref, kr_ref, lse_ref, kprev, vprev):
        n = pl.program_id(0)

        @pl.when(n == 0)
        def _():
            kprev[...] = jnp.zeros_like(kprev)
            vprev[...] = jnp.zeros_like(vprev)

        cos, sin = cos_ref[...], sin_ref[...]
        qr = _rope(q_ref[...], cos, sin)
        kr = _rope(k_ref[...], cos, sin).astype(BF16)
        vb = v_ref[...].astype(BF16)
        kk = jnp.concatenate([kprev[...], kr], axis=0)
        vv = jnp.concatenate([vprev[...], vb], axis=0)
        valid = _band_mask(n > 0)
        outs, lses = [], []
        for g in range(NKV):
            sc = jnp.where(valid, _dot(_own_head(kk, g), _stack_heads(qr, g), NT) * scale, NEG)
            sink = sink_ref[g]
            mx = jnp.maximum(jnp.max(sc, axis=0, keepdims=True), sink)
            p = jnp.exp(sc - mx)
            den = jnp.sum(p, axis=0, keepdims=True) + jnp.exp(sink - mx)
            og = _dot((p * (1.0 / den)).astype(BF16), _pair(vv, g), TN)
            outs.extend(_unstack_heads(og, g))
            lg = mx + jnp.log(den)
            lses.extend([lg[:, h * w:(h + 1) * w] for h in range(GRP)])
        o_ref[...] = jnp.concatenate(outs, axis=-1).astype(BF16)
        lse_ref[...] = jnp.concatenate(lses, axis=0)
        qr_ref[...] = qr.astype(BF16)
        kr_ref[...] = kr
        kprev[...] = kr
        vprev[...] = vb

    return _pallas(
        body, after=after, name=name, grid=(s // w,),
        in_specs=[pl.BlockSpec((w, ATT_W), lambda n: (n, OFF_Q // ATT_W)),
                  pl.BlockSpec((w, KV_W), lambda n: (n, OFF_K // KV_W)),
                  pl.BlockSpec((w, KV_W), lambda n: (n, OFF_VA // KV_W)),
                  pl.BlockSpec((w, LANE), lambda n: (n, 0)), pl.BlockSpec((w, LANE), lambda n: (n, 0)),
                  pl.BlockSpec((NKV, 1, GRP * w), lambda n: (0, 0, 0))],
        out_specs=[pl.BlockSpec((w, ATT_W), lambda n: (n, 0)), pl.BlockSpec((w, ATT_W), lambda n: (n, 0)),
                   pl.BlockSpec((w, KV_W), lambda n: (n, 0)), pl.BlockSpec((None, NQ, w), lambda n: (n, 0, 0))],
        out_shape=[_sds((s, ATT_W), BF16), _sds((s, ATT_W), BF16), _sds((s, KV_W), BF16),
                   _sds((s // w, NQ, w), F32)],
        scratch_shapes=[pltpu.VMEM((w, KV_W), BF16), pltpu.VMEM((w, KV_W), BF16)],
        compiler_params=_params(dimension_semantics=("arbitrary",)),
    )(proj, proj, proj, cos4, sin4, sinks)


def _gate_fwd(name, sg, attn, wa, wb, proj, b_gate, d):
    s = sg.shape[0]
    bm, bn = _tile(s, 1024), _tile(d, 512)
    off_a, off_b = OFF_GA // bn, (OFF_GA + d) // bn

    def body(sg_ref, at_ref, wa_ref, wb_ref, ga_ref, gb_ref, ba_ref, bb_ref, m_ref, ya_ref, yb_ref, sa_ref, sb_ref):
        ya = _dot(sg_ref[...], wa_ref[...], NN)
        yb = _dot(at_ref[...], wb_ref[...], NN)
        sa = _sigmoid(ga_ref[...] + ba_ref[...])
        sb = _sigmoid(gb_ref[...] + bb_ref[...])
        m_ref[...] = (sa * ya + sb * yb).astype(BF16)
        ya_ref[...] = ya.astype(BF16)
        yb_ref[...] = yb.astype(BF16)
        sa_ref[...] = sa.astype(BF16)
        sb_ref[...] = sb.astype(BF16)

    tile = pl.BlockSpec((bm, bn), lambda i, j: (i, j))
    return _pallas(
        body, name=name, grid=(s // bm, d // bn),
        in_specs=[pl.BlockSpec((bm, GMLP_W), lambda i, j: (i, 0)), pl.BlockSpec((bm, ATT_W), lambda i, j: (i, 0)),
                  pl.BlockSpec((GMLP_W, bn), lambda i, j: (0, j)), pl.BlockSpec((ATT_W, bn), lambda i, j: (0, j)),
                  pl.BlockSpec((bm, bn), lambda i, j: (i, off_a + j)),
                  pl.BlockSpec((bm, bn), lambda i, j: (i, off_b + j)),
                  pl.BlockSpec((1, bn), lambda i, j: (0, j)), pl.BlockSpec((1, bn), lambda i, j: (0, d // bn + j))],
        out_specs=[tile] * 5, out_shape=[_sds((s, d), BF16)] * 5,
        compiler_params=_params(),
    )(sg, attn, wa, wb, proj, proj, b_gate, b_gate)


def _xattn_fwd(name, xb, xf, wq, kv, wo, ln_g, ln_b, after=()):
    s, d = xf.shape
    mem = kv.shape[0]
    bm = _tile(s, 512)
    scale = XHD ** -0.5

    def body(xb_ref, xf_ref, wq_ref, kv_ref, wo_ref, g_ref, b_ref, q_out, o_out, r_out, y_out, yb_out):
        qb = _dot(xb_ref[...], wq_ref[...], NN).astype(BF16)
        kvv = kv_ref[...]
        outs = []
        for h in range(XH):
            hs = slice(h * XHD, (h + 1) * XHD)
            vs = slice(X_W + h * XHD, X_W + (h + 1) * XHD)
            sc = _dot(qb[:, hs], kvv[:, hs], NT) * scale
            mx = jnp.max(sc, axis=-1, keepdims=True)
            p = jnp.exp(sc - mx)
            p = p / jnp.sum(p, axis=-1, keepdims=True)
            outs.append(_dot(p.astype(BF16), kvv[:, vs], NN))
        ob = jnp.concatenate(outs, axis=-1).astype(BF16)
        yv = _dot(ob, wo_ref[...], NN)
        r = ALPHA * xf_ref[...] + yv
        yn, _, _ = _ln_rows(r, g_ref[...], b_ref[...])
        q_out[...] = qb
        o_out[...] = ob
        r_out[...] = r
        y_out[...] = yn
        yb_out[...] = yn.astype(BF16)

    row = lambda wd: pl.BlockSpec((bm, wd), lambda i: (i, 0))
    return _pallas(
        body, after=after, name=name, grid=(s // bm,),
        in_specs=[row(d), row(d), pl.BlockSpec((d, X_W), lambda i: (0, 0)),
                  pl.BlockSpec((mem, 2 * X_W), lambda i: (0, 0)), pl.BlockSpec((X_W, d), lambda i: (0, 0)),
                  pl.BlockSpec((1, d), lambda i: (0, 0)), pl.BlockSpec((1, d), lambda i: (0, 0))],
        out_specs=[row(X_W), row(X_W), row(d), row(d), row(d)],
        out_shape=[_sds((s, X_W), BF16), _sds((s, X_W), BF16), _sds((s, d), F32), _sds((s, d), F32),
                   _sds((s, d), BF16)],
        compiler_params=_params(),
    )(xb, xf, wq, kv, wo, ln_g, ln_b)


def _accumulate(i, refs, vals):
    @pl.when(i == 0)
    def _():
        for ref, v in zip(refs, vals):
            ref[...] = v

    @pl.when(i > 0)
    def _():
        for ref, v in zip(refs, vals):
            ref[...] += v


def _ln_bwd_rows(dyv, r, g, dr_ref, drb_ref):
    _, xhat, rstd = _ln_rows(r, g, 0.0)
    dxh = dyv * g
    m1 = jnp.mean(dxh, axis=-1, keepdims=True)
    m2 = jnp.mean(dxh * xhat, axis=-1, keepdims=True)
    dr = rstd * (dxh - m1 - xhat * m2)
    dr_ref[...] = dr
    drb_ref[...] = dr.astype(BF16)
    return jnp.sum(dyv * xhat, axis=0, keepdims=True), jnp.sum(dyv, axis=0, keepdims=True)


def _ln_bwd(name, dy, r, g, after=()):
    s, d = r.shape
    bm = _tile(s, 512)

    def body(dy_ref, r_ref, g_ref, dr_ref, drb_ref, dg_ref, db_ref):
        dg, db = _ln_bwd_rows(dy_ref[...], r_ref[...], g_ref[...], dr_ref, drb_ref)
        _accumulate(pl.program_id(0), (dg_ref, db_ref), (dg, db))

    row = pl.BlockSpec((bm, d), lambda i: (i, 0))
    vec = pl.BlockSpec((1, d), lambda i: (0, 0))
    return _pallas(
        body, after=after, name=name, grid=(s // bm,), in_specs=[row, row, vec], out_specs=[row, row, vec, vec],
        out_shape=[_sds((s, d), F32), _sds((s, d), BF16), _sds((1, d), F32), _sds((1, d), F32)],
        compiler_params=_params(dimension_semantics=("arbitrary",)),
    )(dy, r, g)


def _loss_ln_bwd(name, r, g, b, tgt):
    s, d = r.shape
    bm = _tile(s, 512)

    def body(r_ref, g_ref, b_ref, t_ref, dr_ref, drb_ref, dg_ref, db_ref, loss_ref):
        rv, gv = r_ref[...], g_ref[...]
        y, _, _ = _ln_rows(rv, gv, b_ref[...])
        err = y - t_ref[...]
        part = 0.5 * jnp.sum(jnp.sum(err * err, axis=-1, keepdims=True), axis=0, keepdims=True) * (1.0 / d)
        dg, db = _ln_bwd_rows(err * (1.0 / d), rv, gv, dr_ref, drb_ref)
        _accumulate(pl.program_id(0), (dg_ref, db_ref, loss_ref), (dg, db, part))

    row = pl.BlockSpec((bm, d), lambda i: (i, 0))
    vec = pl.BlockSpec((1, d), lambda i: (0, 0))
    return _pallas(
        body, name=name, grid=(s // bm,), in_specs=[row, vec, vec, row],
        out_specs=[row, row, vec, vec, pl.BlockSpec((1, 1), lambda i: (0, 0))],
        out_shape=[_sds((s, d), F32), _sds((s, d), BF16), _sds((1, d), F32), _sds((1, d), F32), _sds((1, 1), F32)],
        compiler_params=_params(dimension_semantics=("arbitrary",)),
    )(r, g, b, tgt)


def _xattn_bwd(name, dyb, drf, q, kv, wo, wq):
    s, d = drf.shape
    mem = kv.shape[0]
    bm = _tile(s, 512)
    scale = XHD ** -0.5

    def body(dy_ref, dr_ref, q_ref, kv_ref, wo_ref, wq_ref, dx_out, dq_out, dkv_out):
        i = pl.program_id(0)
        dob = _dot(dy_ref[...], wo_ref[...], NT).astype(BF16)
        qb = q_ref[...]
        kvv = kv_ref[...]
        dqs, dks, dvs = [], [], []
        for h in range(XH):
            hs = slice(h * XHD, (h + 1) * XHD)
            vs = slice(X_W + h * XHD, X_W + (h + 1) * XHD)
            sc = _dot(qb[:, hs], kvv[:, hs], NT) * scale
            mx = jnp.max(sc, axis=-1, keepdims=True)
            p = jnp.exp(sc - mx)
            p = p / jnp.sum(p, axis=-1, keepdims=True)
            dp = _dot(dob[:, hs], kvv[:, vs], NT)
            dsum = jnp.sum(p * dp, axis=-1, keepdims=True)
            dsb = (p * (dp - dsum) * scale).astype(BF16)
            dqs.append(_dot(dsb, kvv[:, hs], NN))
            dks.append(_dot(dsb, qb[:, hs], TN))
            dvs.append(_dot(p.astype(BF16), dob[:, hs], TN))
        dqb = jnp.concatenate(dqs, axis=-1).astype(BF16)
        dq_out[...] = dqb
        dx_out[...] = _dot(dqb, wq_ref[...], NT) + ALPHA * dr_ref[...]
        dkv = jnp.concatenate(dks + dvs, axis=-1)

        @pl.when(i == 0)
        def _():
            dkv_out[...] = dkv

        @pl.when(i > 0)
        def _():
            dkv_out[...] += dkv

    row = lambda wd: pl.BlockSpec((bm, wd), lambda i: (i, 0))
    return _pallas(
        body, name=name, grid=(s // bm,),
        in_specs=[row(d), row(d), row(X_W), pl.BlockSpec((mem, 2 * X_W), lambda i: (0, 0)),
                  pl.BlockSpec((X_W, d), lambda i: (0, 0)), pl.BlockSpec((d, X_W), lambda i: (0, 0))],
        out_specs=[row(d), row(X_W), pl.BlockSpec((mem, 2 * X_W), lambda i: (0, 0))],
        out_shape=[_sds((s, d), F32), _sds((s, X_W), BF16), _sds((mem, 2 * X_W), F32)],
        compiler_params=_params(dimension_semantics=("arbitrary",)),
    )(dyb, drf, q, kv, wo, wq)


def _gate_bwd(name, dr1b, w_o, sa, sb, ya, yb, d, after=()):
    s = dr1b.shape[0]
    bm, bn = _tile(s, 1024), _tile(d, 512)
    nj = d // bn

    def body(a_ref, w_ref, sa_ref, sb_ref, ya_ref, yb_ref, dya_ref, dyb_ref, dg_ref, dba_ref, dbb_ref):
        i = pl.program_id(1)
        dm = _dot(a_ref[...], w_ref[...], NT)
        sa = sa_ref[...].astype(F32)
        sb = sb_ref[...].astype(F32)
        dya_ref[...] = (dm * sa).astype(BF16)
        dyb_ref[...] = (dm * sb).astype(BF16)
        dga = dm * ya_ref[...].astype(F32) * (sa * (1.0 - sa))
        dgb = dm * yb_ref[...].astype(F32) * (sb * (1.0 - sb))
        dg_ref[0] = dga.astype(BF16)
        dg_ref[1] = dgb.astype(BF16)
        sa_sum = jnp.sum(dga, axis=0, keepdims=True)
        sb_sum = jnp.sum(dgb, axis=0, keepdims=True)

        @pl.when(i == 0)
        def _():
            dba_ref[...] = sa_sum
            dbb_ref[...] = sb_sum

        @pl.when(i > 0)
        def _():
            dba_ref[...] += sa_sum
            dbb_ref[...] += sb_sum

    tile = pl.BlockSpec((bm, bn), lambda j, i: (i, j))
    return _pallas(
        body, after=after, name=name, grid=(nj, s // bm),
        in_specs=[pl.BlockSpec((bm, d), lambda j, i: (i, 0)), pl.BlockSpec((bn, d), lambda j, i: (j, 0)),
                  tile, tile, tile, tile],
        out_specs=[tile, tile, pl.BlockSpec((2, bm, bn), lambda j, i: (0, i, j)),
                   pl.BlockSpec((1, bn), lambda j, i: (0, j)), pl.BlockSpec((1, bn), lambda j, i: (0, j))],
        out_shape=[_sds((s, d), BF16), _sds((s, d), BF16), _sds((2, s, d), BF16), _sds((1, d), F32),
                   _sds((1, d), F32)],
        compiler_params=_params(dimension_semantics=("arbitrary", "arbitrary")),
    )(dr1b, w_o, sa, sb, ya, yb)


def _gmlp_bwd(name, proj, dsg, ln_g, ln_b, w_s, b_st):
    s = proj.shape[0]

    def body(u_ref, v_ref, dsg_ref, g_ref, b_ref, ws_ref, bst_ref, duv_ref, dws_ref, dbst_ref, dlg_ref, dlb_ref):
        n = pl.program_id(0)
        u, v = u_ref[...], v_ref[...]
        gu, tu = _gelu(u)
        gv, tv = _gelu(v)
        gam = g_ref[...]
        vn, xhat, rstd = _ln_rows(gv, gam, b_ref[...])
        vnb = vn.astype(BF16)
        dsg = dsg_ref[...].astype(F32)
        row = lax.broadcasted_iota(jnp.int32, (CHUNK, CHUNK), 0)
        col = lax.broadcasted_iota(jnp.int32, (CHUNK, CHUNK), 1)
        tril = col <= row
        dgu, dvn, dws, dbs = [], [], [], []
        for g in range(GROUPS):
            sl = slice(g * LANE, (g + 1) * LANE)
            w = jnp.where(tril, ws_ref[g], 0.0).astype(BF16)
            mixed = _dot(w, vnb[:, sl], NN) + bst_ref[:, g:g + 1]
            dgu.append(dsg[:, sl] * mixed)
            dmx = dsg[:, sl] * gu[:, sl]
            dmxb = dmx.astype(BF16)
            dbs.append(jnp.sum(dmx, axis=-1, keepdims=True))
            dws.append(jnp.where(tril, _dot(dmxb, vnb[:, sl], NT), 0.0))
            dvn.append(_dot(w, dmxb, TN))
        dvn = jnp.concatenate(dvn, axis=-1)
        dgu = jnp.concatenate(dgu, axis=-1)
        dxh = dvn * gam
        m1 = jnp.mean(dxh, axis=-1, keepdims=True)
        m2 = jnp.mean(dxh * xhat, axis=-1, keepdims=True)
        dgv = rstd * (dxh - m1 - xhat * m2)
        du = dgu * _gelu_grad(u, tu)
        dv = dgv * _gelu_grad(v, tv)
        duv_ref[...] = jnp.concatenate([du, dv], axis=-1).astype(BF16)
        dlg = jnp.sum(dvn * xhat, axis=0, keepdims=True)
        dlb = jnp.sum(dvn, axis=0, keepdims=True)
        dbst = jnp.concatenate(dbs, axis=-1)

        @pl.when(n == 0)
        def _():
            for g in range(GROUPS):
                dws_ref[g] = dws[g]
            dbst_ref[...] = dbst
            dlg_ref[...] = dlg
            dlb_ref[...] = dlb

        @pl.when(n > 0)
        def _():
            for g in range(GROUPS):
                dws_ref[g] += dws[g]
            dbst_ref[...] += dbst
            dlg_ref[...] += dlg
            dlb_ref[...] += dlb

    vec = pl.BlockSpec((1, GMLP_W), lambda n: (0, 0))
    return _pallas(
        body, name=name, grid=(s // CHUNK,),
        in_specs=[pl.BlockSpec((CHUNK, GMLP_W), lambda n: (n, 0)), pl.BlockSpec((CHUNK, GMLP_W), lambda n: (n, 1)),
                  pl.BlockSpec((CHUNK, GMLP_W), lambda n: (n, 0)), vec, vec,
                  pl.BlockSpec((GROUPS, CHUNK, CHUNK), lambda n: (0, 0, 0)),
                  pl.BlockSpec((CHUNK, GROUPS), lambda n: (0, 0))],
        out_specs=[pl.BlockSpec((CHUNK, 2 * GMLP_W), lambda n: (n, 0)),
                   pl.BlockSpec((GROUPS, CHUNK, CHUNK), lambda n: (0, 0, 0)),
                   pl.BlockSpec((CHUNK, GROUPS), lambda n: (0, 0)), vec, vec],
        out_shape=[_sds((s, 2 * GMLP_W), BF16), _sds((GROUPS, CHUNK, CHUNK), F32), _sds((CHUNK, GROUPS), F32),
                   _sds((1, GMLP_W), F32), _sds((1, GMLP_W), F32)],
        compiler_params=_params(dimension_semantics=("arbitrary",)),
    )(proj, proj, dsg, ln_g, ln_b, w_s, b_st)


def _swa_bwd(name, qr, kr, proj, do, o, lse, sinks, cos4, nsin4, after=()):
    s = qr.shape[0]
    w = CHUNK
    nblk = s // w
    scale = HD ** -0.5
    grp = NQ // NKV

    def body(qj_ref, qn_ref, kj_ref, kp_ref, vj_ref, vp_ref, doj_ref, don_ref, oj_ref, on_ref, lj_ref, ln_ref,
             sink_ref, cos_ref, sin_ref, out_ref, dsink_ref):
        j = pl.program_id(0)
        qj, qn = qj_ref[...].astype(F32), qn_ref[...].astype(F32)
        doj, don = doj_ref[...].astype(F32), don_ref[...].astype(F32)
        kk = jnp.concatenate([kp_ref[...], kj_ref[...]], axis=0)
        vv = jnp.concatenate([vp_ref[...], vj_ref[...]], axis=0).astype(BF16)
        lj, lnx = lj_ref[...], ln_ref[...]
        prod_j = doj * oj_ref[...].astype(F32)
        prod_n = don * on_ref[...].astype(F32)
        valid_j = _band_mask(j > 0)
        valid_n = _band_mask(j + 1 < nblk, prev_only=True)
        lo = lax.broadcasted_iota(jnp.int32, (w, LANE), 1) < HD
        dqs, dsk, dk_g, dv_g = [], [], [], []
        for g in range(NKV):
            kz, vz = _own_head(kk, g), _own_head(vv, g)
            kz_c, vz_c = kz[w:], vz[w:]
            qg_j, qg_n = _stack_heads(qj, g), _stack_heads(qn, g)
            dog_j, dog_n = _stack_heads(doj, g), _stack_heads(don, g)
            l_j, l_n = _stack_rows(lj, g), _stack_rows(lnx, g)
            d_j = _head_lane_sums(_stack_heads(prod_j, g, F32), g)
            d_n = _head_lane_sums(_stack_heads(prod_n, g, F32), g)
            p = jnp.where(valid_j, jnp.exp(_dot(kz, qg_j, NT) * scale - l_j), 0.0)
            ds = (p * (_dot(vz, dog_j, NT) - d_j) * scale).astype(BF16)
            dqs.extend(_unstack_heads(_dot(ds, kz, TN), g))
            p2 = jnp.where(valid_n, jnp.exp(_dot(kz_c, qg_n, NT) * scale - l_n), 0.0)
            ds2 = (p2 * (_dot(vz_c, dog_n, NT) - d_n) * scale).astype(BF16)
            dk_g.append(_dot(ds[w:], qg_j, NN) + _dot(ds2, qg_n, NN))
            dv_g.append(_dot(p[w:].astype(BF16), dog_j, NN) + _dot(p2.astype(BF16), dog_n, NN))
            t = jnp.exp(sink_ref[g] - l_j) * d_j
            dsk.extend([-jnp.sum(t[:, h * w:(h + 1) * w], axis=-1, keepdims=True) for h in range(GRP)])
        cos, nsin = cos_ref[...], sin_ref[...]
        dq = _rope(jnp.concatenate(dqs, axis=-1), cos, nsin)
        dk = _rope(jnp.concatenate([jnp.where(lo, dk_g[2 * m], dk_g[2 * m + 1]) for m in range(NKV // 2)], axis=-1),
                   cos, nsin)
        dv = jnp.concatenate([jnp.where(lo, dv_g[2 * m], dv_g[2 * m + 1]) for m in range(NKV // 2)], axis=-1)
        out_ref[...] = jnp.concatenate([dq, dk, dv], axis=-1).astype(BF16)
        dsink = jnp.concatenate(dsk, axis=-1)

        @pl.when(j == 0)
        def _():
            dsink_ref[...] = dsink

        @pl.when(j > 0)
        def _():
            dsink_ref[...] += dsink

    nxt = lambda j: jnp.minimum(j + 1, nblk - 1)
    prv = lambda j: jnp.maximum(j - 1, 0)
    va = OFF_VA // KV_W
    return _pallas(
        body, after=after, name=name, grid=(nblk,),
        in_specs=[pl.BlockSpec((w, ATT_W), lambda j: (j, 0)), pl.BlockSpec((w, ATT_W), lambda j: (nxt(j), 0)),
                  pl.BlockSpec((w, KV_W), lambda j: (j, 0)), pl.BlockSpec((w, KV_W), lambda j: (prv(j), 0)),
                  pl.BlockSpec((w, KV_W), lambda j: (j, va)), pl.BlockSpec((w, KV_W), lambda j: (prv(j), va)),
                  pl.BlockSpec((w, ATT_W), lambda j: (j, 0)), pl.BlockSpec((w, ATT_W), lambda j: (nxt(j), 0)),
                  pl.BlockSpec((w, ATT_W), lambda j: (j, 0)), pl.BlockSpec((w, ATT_W), lambda j: (nxt(j), 0)),
                  pl.BlockSpec((None, NQ, w), lambda j: (j, 0, 0)),
                  pl.BlockSpec((None, NQ, w), lambda j: (nxt(j), 0, 0)),
                  pl.BlockSpec((NKV, 1, GRP * w), lambda j: (0, 0, 0)),
                  pl.BlockSpec((w, LANE), lambda j: (j, 0)), pl.BlockSpec((w, LANE), lambda j: (j, 0))],
        out_specs=[pl.BlockSpec((w, ATT_W + 2 * KV_W), lambda j: (j, 0)), pl.BlockSpec((1, NQ), lambda j: (0, 0))],
        out_shape=[_sds((s, ATT_W + 2 * KV_W), BF16), _sds((1, NQ), F32)],
        compiler_params=_params(dimension_semantics=("arbitrary",)),
    )(qr, qr, kr, kr, proj, proj, do, do, o, o, lse, lse, sinks, cos4, nsin4)


def _mm_nn(name, a, w, *, out_dtypes, epilogue=_store, bm_pref=2048, bn_pref=1024, after=()):
    m, k = a.shape
    n = w.shape[-1]
    bm, bn = _tile(m, bm_pref), _tile(n, bn_pref)
    tile = pl.BlockSpec((bm, bn), lambda i, j, kk: (i, j))
    return _mm(name, a, w, dims=NN, grid=(m // bm, n // bn, 1),
               a_spec=pl.BlockSpec((bm, k), lambda i, j, kk: (i, 0)),
               b_spec=pl.BlockSpec((k, bn), lambda i, j, kk: (0, j)),
               out_shape=[_sds((m, n), dt) for dt in out_dtypes], out_specs=[tile] * len(out_dtypes),
               epilogue=epilogue, after=after)


def _dw_half(name, a, b, axis, hidx, got=None, after=()):
    s, m = a.shape
    n = b.shape[-1]
    mh, nh = (m // 2, n) if axis == 1 else (m, n // 2)
    bm, bn = _tile(mh, 1024), _tile(nh, 1024)
    nmb, nnb = mh // bm, nh // bn
    if axis == 1:
        a_spec = pl.BlockSpec((s, bm), lambda i, j, h: (0, h[0] * nmb + i))
        b_spec = pl.BlockSpec((s, bn), lambda i, j, h: (0, j))
    else:
        a_spec = pl.BlockSpec((s, bm), lambda i, j, h: (0, i))
        b_spec = pl.BlockSpec((s, bn), lambda i, j, h: (0, h[0] * nnb + j))
    tile = pl.BlockSpec((bm, bn), lambda i, j, h: (i, j))
    n_got, n_after = (0 if got is None else 1), len(after)

    def body(h_ref, a_ref, b_ref, *rest):
        acc = _dot(a_ref[...], b_ref[...], TN)
        if n_got:
            acc = acc + rest[0][...].astype(F32)
        rest[-1][...] = acc.astype(BF16)

    return pl.pallas_call(
        body, name=name,
        grid_spec=pltpu.PrefetchScalarGridSpec(
            num_scalar_prefetch=1, grid=(nmb, nnb),
            in_specs=[a_spec, b_spec] + [tile] * n_got + [ANY] * n_after, out_specs=tile),
        out_shape=_sds((mh, nh), BF16), compiler_params=_params(dimension_semantics=("arbitrary", "arbitrary")),
    )(hidx, a, b, *(() if got is None else (got,)), *after)


def _mm_residual(name, a, w, x, after=()):
    s, k = a.shape
    d = w.shape[-1]
    bm, bn = _tile(s, 1024), _tile(d, 1024 if k <= 2048 else 512)
    tile = pl.BlockSpec((bm, bn), lambda i, j, kk: (i, j))
    return _mm(name, a, w, dims=NN, grid=(s // bm, d // bn, 1),
               a_spec=pl.BlockSpec((bm, k), lambda i, j, kk: (i, 0)),
               b_spec=pl.BlockSpec((k, bn), lambda i, j, kk: (0, j)),
               extras=(x,), extra_specs=(tile,), out_shape=[_sds((s, d), F32)], out_specs=[tile],
               epilogue=_ep_add_scaled, after=after)[0]


def _dw_pieces_half(name, a, pieces, widths, hidx, got=None, after=()):
    s, m = a.shape
    total = sum(widths)
    mh = m // 2
    bm, bn = _tile(mh, 1024), 512
    nmb = mh // bm
    n_got = 0 if got is None else 1
    out, off = None, 0
    for p, (piece, wd) in enumerate(zip(pieces, widths)):
        if isinstance(piece, tuple):
            arr = piece[0]
            b_spec = pl.BlockSpec((None, s, bn), (lambda ix: lambda i, j, h: (ix, 0, j))(piece[1]))
        else:
            arr, b_spec = piece, pl.BlockSpec((s, bn), lambda i, j, h: (0, j))
        tile = pl.BlockSpec((bm, bn), (lambda c: lambda i, j, h: (i, c + j))(off // bn))
        prev = () if out is None else (out,)
        first_after = after if out is None else ()

        def body(h_ref, a_ref, b_ref, *rest):
            acc = _dot(a_ref[...], b_ref[...], TN)
            if n_got:
                acc = acc + rest[0][...].astype(F32)
            rest[-1][...] = acc.astype(BF16)

        out = pl.pallas_call(
            body, name=f"{name}_{p}",
            grid_spec=pltpu.PrefetchScalarGridSpec(
                num_scalar_prefetch=1, grid=(nmb, wd // bn),
                in_specs=[pl.BlockSpec((s, bm), lambda i, j, h: (0, h[0] * nmb + i)), b_spec] + [tile] * n_got
                + [ANY] * (len(prev) + len(first_after)), out_specs=tile),
            out_shape=_sds((mh, total), BF16), input_output_aliases={3 + n_got: 0} if prev else {},
            compiler_params=_params(dimension_semantics=("arbitrary", "arbitrary")),
        )(hidx, a, arr, *(() if got is None else (got,)), *prev, *first_after)
        off += wd
    return out


def _dx_pieces(name, pieces, widths, w, dr, after=()):
    s, d = dr.shape
    iw = w.shape[-1]
    bm, bn = _tile(s, 1024), _tile(d, 512)
    arrs, specs = [], []
    for piece, wd in zip(pieces, widths):
        if isinstance(piece, tuple):
            arrs.append(piece[0])
            specs.append(pl.BlockSpec((None, bm, wd), (lambda idx: lambda i, j: (idx, i, 0))(piece[1])))
        else:
            arrs.append(piece)
            specs.append(pl.BlockSpec((bm, wd), lambda i, j: (i, 0)))
    n = len(arrs)

    def body(*refs):
        w_ref, dr_ref, o_ref = refs[n], refs[n + 1], refs[n + 2]
        acc = ALPHA * dr_ref[...]
        off = 0
        for p, wd in enumerate(widths):
            acc = acc + _dot(refs[p][...], w_ref[:, off:off + wd], NT)
            off += wd
        o_ref[...] = acc

    tile = pl.BlockSpec((bm, bn), lambda i, j: (i, j))
    return _pallas(
        body, after=after, name=name, grid=(s // bm, d // bn),
        in_specs=specs + [pl.BlockSpec((bn, iw), lambda i, j: (j, 0)), tile], out_specs=tile,
        out_shape=_sds((s, d), F32), compiler_params=_params(dimension_semantics=("arbitrary", "arbitrary")),
    )(*arrs, w, dr)


class _Gather:
    def __init__(self, tag, names, fulls, after):
        self.tag, self.names = tag, names
        self.axes = [SHARD_AXIS[n] for n in names]
        self.srcs = []
        self.mk1 = _mk_gather_ici(self.axes)
        self.mk2 = _mk_gather_d2d(self.axes)
        self.n_sem = 3 * len(names)
        self.s1, self.r1, self.lands, self.token = _split_start(
            tag + "_ici_start", self.srcs, [fulls[n] for n in names], self.mk1, self.n_sem, after)

    def forward(self, after=()):
        lands = _split_wait(self.tag + "_ici_wait", self.srcs, self.lands, self.s1, self.r1, self.mk1, after)
        self.s2, self.r2, self.lands, tok = _split_start(self.tag + "_d2d_start", [], lands, self.mk2, self.n_sem)
        return tok

    def done(self, after=()):
        lands = _split_wait(self.tag + "_d2d_wait", [], self.lands, self.s2, self.r2, self.mk2, after)
        return dict(zip(self.names, lands))


class _Reduce:
    def __init__(self, tag, names, parts, mcidx, after=()):
        self.tag, self.names, self.mcidx = tag, names, mcidx
        self.axes = [SHARD_AXIS[n] for n in names]
        self.parts = [parts[n] for n in names]
        self.mk = _mk_swap
        lands = [lax.empty(p.shape, BF16) for p in self.parts]
        self.s, self.r, self.lands, self.token = _split_start(
            tag + "_swap_start", self.parts, lands, self.mk, len(names), after)

    def scatter(self, own, after=()):
        got = _split_wait(self.tag + "_swap_wait", self.parts, self.lands, self.s, self.r, self.mk, after)
        sums = own(dict(zip(self.names, got)))
        self.sums = [sums[n] for n in self.names]
        self.mk = _mk_scatter(self.axes)
        lands = []
        for q, ax in zip(self.sums, self.axes):
            k, n = q.shape
            lands.append(lax.empty((3, k // 4, n) if ax == 0 else (3, k, n // 4), BF16))
        self.s, self.r, self.lands, tok = _split_start(
            self.tag + "_scatter_start", self.sums, lands, self.mk, 3 * len(self.names))
        return tok

    def exchange(self, after=()):
        slots = _split_wait(self.tag + "_scatter_wait", self.sums, self.lands, self.s, self.r, self.mk, after)
        halves = [_sum_half(f"{self.tag}_sum_{n}", q, sl, ax, self.mcidx)
                  for n, q, sl, ax in zip(self.names, self.sums, slots, self.axes)]
        self.mk = _mk_exchange(self.axes)
        self.s, self.r, self.lands, tok = _split_start(
            self.tag + "_exchange_start", [], halves, self.mk, len(self.names))
        return tok

    def done(self, after=()):
        grads = _split_wait(self.tag + "_exchange_wait", [], self.lands, self.s, self.r, self.mk, after)
        return dict(zip(self.names, grads))


def _pack(arrs):
    flat = jnp.concatenate([a.reshape(-1) for a in arrs])
    n = flat.shape[0]
    pad = (-n) % (8 * LANE)
    return jnp.pad(flat, (0, pad)).reshape(-1, LANE)


def _unpack(packed, shapes):
    flat = packed.reshape(-1)
    out, off = [], 0
    for sh in shapes:
        n = math.prod(sh)
        out.append(flat[off:off + n].reshape(sh))
        off += n
    return out


def kernel(x, mem, w_in, b_gate, ln_v_g, ln_v_b, w_s, b_s, sinks, w_br_a, w_br_b, w_o, ln1_g, ln1_b, w_xq, w_xkv, w_xo, ln2_g, ln2_b, w_up, w_down, ln3_g, ln3_b, loss_target, m_w_in, m_b_gate, m_ln_v_g, m_ln_v_b, m_w_s, m_b_s, m_sinks, m_w_br_a, m_w_br_b, m_w_o, m_ln1_g, m_ln1_b, m_w_xq, m_w_xkv, m_w_xo, m_ln2_g, m_ln2_b, m_w_up, m_w_down, m_ln3_g, m_ln3_b, v_w_in, v_b_gate, v_ln_v_g, v_ln_v_b, v_w_s, v_b_s, v_sinks, v_w_br_a, v_w_br_b, v_w_o, v_ln1_g, v_ln1_b, v_w_xq, v_w_xkv, v_w_xo, v_ln2_g, v_ln2_b, v_w_up, v_w_down, v_ln3_g, v_ln3_b):
    env = dict(locals())
    wts = {n: env[n] for n in WEIGHTS}
    mom_m = {n: env["m_" + n] for n in WEIGHTS}
    mom_v = {n: env["v_" + n] for n in WEIGHTS}
    s, d = x.shape[1], x.shape[2]
    dff = 4 * w_up.shape[-1]
    xf = x.reshape(s, d)
    tgt = loss_target.reshape(s, d)
    memf = mem.reshape(mem.shape[1], d)
    ax_x, ax_y, ax_c = lax.axis_index("x"), lax.axis_index("y"), lax.axis_index("c")
    meidx = jnp.reshape(2 * ax_x + ax_y, (1,)).astype(jnp.int32)
    cidx = jnp.reshape(ax_c, (1,)).astype(jnp.int32)
    sidx = 1 - cidx
    mcidx = jnp.concatenate([meidx, cidx])

    inv = 1.0 / (10000.0 ** (jnp.arange(0, HD, 2, dtype=F32) / HD))
    ang = jnp.arange(s, dtype=F32)[:, None] * inv[None, :]
    cos, sin = jnp.cos(ang), jnp.sin(ang)
    cos4 = jnp.tile(cos, (1, 4))
    sin4 = jnp.concatenate([-sin, sin, -sin, sin], axis=-1)
    nsin4 = -sin4

    small = {}
    for n in SMALL:
        w = wts[n]
        if n == "w_s":
            small[n] = [w[l] for l in range(DEPTH)]
        elif n == "b_s":
            small["b_st"] = [w[l].T for l in range(DEPTH)]
        else:
            small[n] = [w[l][None, :] for l in range(DEPTH)]
    small["sink_rows"] = [jnp.repeat(sinks[l].reshape(NKV, GRP), CHUNK, axis=1)[:, None, :] for l in range(DEPTH)]

    fulls = [{}, {}]
    tok = ()
    gathers = [[None] * len(GROUPS_GATHER) for _ in range(DEPTH)]
    for gi, names in enumerate(GROUPS_GATHER):
        for n in names:
            fulls[0][n], fulls[1][n] = _place_own("place_" + n, wts[n], SHARD_AXIS[n], meidx, after=tok)
        gathers[0][gi] = _Gather(f"ag0_{gi}", names, fulls[0], tok)
        tok = (gathers[0][gi].token,)
    for gi, names in enumerate(GROUPS_GATHER):
        gathers[1][gi] = _Gather(f"ag1_{gi}", names, fulls[1], tok)
        tok = (gathers[1][gi].token,)

    xb = _cast2d("cast_x", xf, after=tok)
    memb = _cast2d("cast_mem", memf, after=tok)

    saved = []
    hf, hb = xf, xb
    nxt_tok = gathers[0][0].forward(after=tok)
    for l in range(DEPTH):
        t = f"l{l}_"
        ga, gb, gc, gd = gathers[l]
        full = ga.done(after=(nxt_tok, hb))
        sv = {"xf": hf, "xb": hb}
        proj = _mm_nn(t + "proj", hb, full["w_in"], out_dtypes=[F32], bn_pref=1280)[0]
        tok_b = gb.forward(after=(proj,))
        sg = _gmlp_fwd(t + "gmlp_fwd", proj, small["ln_v_g"][l], small["ln_v_b"][l], small["w_s"][l],
                       small["b_st"][l])
        attn, qr, kr, lse = _swa_fwd(t + "swa_fwd", proj, cos4, sin4, small["sink_rows"][l], after=(tok_b,))
        full.update(gb.done(after=(attn,)))
        merged, ya, yb, sa, sb = _gate_fwd(t + "gate_fwd", sg, attn, full["w_br_a"], full["w_br_b"], proj,
                                           small["b_gate"][l], d)
        sv.update(sa=sa, sb=sb)
        tok_c = gc.forward(after=(merged,))
        r1 = _mm_residual(t + "o", merged, full["w_o"], hf, after=(tok_c,))
        x1, x1b = _ln_fwd(t + "ln1", r1, small["ln1_g"][l], small["ln1_b"][l])
        kv = _mm_nn(t + "xkv", memb, full["w_xkv"], out_dtypes=[BF16])[0]
        q, o, r2, x2, x2b = _xattn_fwd(t + "xattn_fwd", x1b, x1, full["w_xq"], kv, full["w_xo"],
                                       small["ln2_g"][l], small["ln2_b"][l])
        full.update(gc.done(after=(x2b,)))
        tok_d = gd.forward(after=(x2b,))

        def ep_up(acc, ex, outs):
            outs[0][...] = acc.astype(BF16)
            rl = jnp.maximum(acc, 0.0)
            outs[1][...] = (rl * rl).astype(BF16)

        h, a = _mm_nn(t + "up", x2b, full["w_up"], out_dtypes=[BF16, BF16], epilogue=ep_up, after=(tok_d,))
        full.update(gd.done(after=(h,)))
        nxt_tok = gathers[l + 1][0].forward(after=(h,)) if l + 1 < DEPTH else None
        r3 = _mm_residual(t + "down", a, full["w_down"], x2, after=() if nxt_tok is None else (nxt_tok,))
        sv.update(proj=proj, sg=sg, attn=attn, qr=qr, kr=kr, lse=lse, merged=merged, ya=ya, yb=yb, r1=r1, x1=x1,
                  x1b=x1b, kv=kv, q=q, o=o, r2=r2, x2b=x2b, h=h, a=a, r3=r3, full=full)
        saved.append(sv)
        if l + 1 < DEPTH:
            hf, hb = _ln_fwd(t + "ln3", r3, small["ln3_g"][l], small["ln3_b"][l])

    small_g = [None] * DEPTH
    grads = [{}, {}]
    pend_a = None
    pend_b = None
    for l in reversed(range(DEPTH)):
        t = f"l{l}_"
        sv = saved[l]
        full = sv["full"]
        sgo = {}
        if l == DEPTH - 1:
            dr3, dr3b, sgo["ln3_g"], sgo["ln3_b"], loss11 = _loss_ln_bwd(
                "loss_ln3_bwd", sv["r3"], small["ln3_g"][l], small["ln3_b"][l], tgt)
            loss = lax.psum(loss11[0, 0], ("x", "y", "c"))
        else:
            dr3, dr3b, sgo["ln3_g"], sgo["ln3_b"] = _ln_bwd(t + "ln3_bwd", g, sv["r3"], small["ln3_g"][l],
                                                            after=(tok_a,))
        bm, bn = _tile(s, 2048), _tile(dff, 1024)

        def ep_dh(acc, ex, outs):
            outs[0][...] = (acc * (2.0 * jnp.maximum(ex[0][...].astype(F32), 0.0))).astype(BF16)

        tile = pl.BlockSpec((bm, bn), lambda i, j, k: (i, j))
        dh = _mm(t + "dh", dr3b, full["w_down"], dims=NT, grid=(s // bm, dff // bn, 1),
                 a_spec=pl.BlockSpec((bm, d), lambda i, j, k: (i, 0)),
                 b_spec=pl.BlockSpec((bn, d), lambda i, j, k: (j, 0)),
                 extras=(sv["h"],), extra_specs=(tile,), out_shape=[_sds((s, dff), BF16)], out_specs=[tile],
                 epilogue=ep_dh)[0]
        if pend_a is not None:
            tok_pa = pend_a.exchange(after=(dh,))
            grads[l + 1].update(pend_b.done(after=(dh,)))

        def halves(specs, hidx, got=None, after=()):
            out = {}
            for i, (n, a_, b_) in enumerate(specs):
                out[n] = _dw_half(f"{t}d{n}_{'s' if got is None else 'o'}", a_, b_, SHARD_AXIS[n], hidx,
                                  None if got is None else got[n], after if i == 0 else ())
            return out

        specs_c = [("w_down", sv["a"], dr3b), ("w_up", sv["x2b"], dh)]
        red_c = _Reduce(t + "rs_c", GROUPS_FWD[2],
                        halves(specs_c, sidx, after=() if pend_a is None else (tok_pa,)), mcidx)
        bm2, bn2 = _tile(s, 1024), _tile(d, 512)
        tile2 = pl.BlockSpec((bm2, bn2), lambda i, j, k: (i, j))
        dx2 = _mm(t + "dx2", dh, full["w_up"], dims=NT, grid=(s // bm2, d // bn2, 1),
                  a_spec=pl.BlockSpec((bm2, dff), lambda i, j, k: (i, 0)),
                  b_spec=pl.BlockSpec((bn2, dff), lambda i, j, k: (j, 0)),
                  extras=(dr3,), extra_specs=(tile2,), out_shape=[_sds((s, d), F32)], out_specs=[tile2],
                  epilogue=_ep_add_scaled, after=(red_c.token,))[0]
        tok_c = red_c.scatter(lambda got: halves(specs_c, cidx, got), after=(dx2,))
        if pend_a is not None:
            grads[l + 1].update(pend_a.done(after=(dx2,)))
            pend_a = None

        dr2, dr2b, sgo["ln2_g"], sgo["ln2_b"] = _ln_bwd(t + "ln2_bwd", dx2, sv["r2"], small["ln2_g"][l],
                                                        after=(tok_c,))
        dx1, dq, dkv = _xattn_bwd(t + "xattn_bwd", dr2b, dr2, sv["q"], sv["kv"], full["w_xo"], full["w_xq"])

        dr1, dr1b, sgo["ln1_g"], sgo["ln1_b"] = _ln_bwd(t + "ln1_bwd", dx1, sv["r1"], small["ln1_g"][l])
        dya, dyb, dgate, dba, dbb = _gate_bwd(t + "gate_bwd", dr1b, full["w_o"], sv["sa"], sv["sb"], sv["ya"],
                                              sv["yb"], d)
        sgo["b_gate"] = jnp.concatenate([dba, dbb], axis=-1)
        specs_b = [("w_o", sv["merged"], dr1b), ("w_br_a", sv["sg"], dya), ("w_br_b", sv["attn"], dyb),
                   ("w_xo", sv["o"], dr2b), ("w_xq", sv["x1b"], dq),
                   ("w_xkv", memb, _cast2d(t + "dkv_cast", dkv))]
        red_b = _Reduce(t + "rs_b", GROUPS_FWD[1], halves(specs_b, sidx), mcidx)

        def dbranch(name, dyx, w, after):
            return _mm(name, dyx, w, dims=NT, grid=(s // bm, 1, 1),
                       a_spec=pl.BlockSpec((bm, d), lambda i, j, k: (i, 0)),
                       b_spec=pl.BlockSpec((w.shape[0], d), lambda i, j, k: (0, 0)),
                       out_shape=[_sds((s, w.shape[0]), BF16)],
                       out_specs=[pl.BlockSpec((bm, w.shape[0]), lambda i, j, k: (i, 0))],
                       epilogue=_store, after=after)[0]

        dsg = dbranch(t + "dsg", dya, full["w_br_a"], (red_b.token,))
        dattn = dbranch(t + "dattn", dyb, full["w_br_b"], ())
        tok_c = red_c.exchange(after=(dattn, dsg))
        tok_b = red_b.scatter(lambda got: halves(specs_b, cidx, got), after=(dattn, dsg))
        duv, sgo["w_s"], dbst, dlg, dlb = _gmlp_bwd(t + "gmlp_bwd", sv["proj"], dsg, small["ln_v_g"][l],
                                                    small["ln_v_b"][l], small["w_s"][l], small["b_st"][l])
        sgo["b_s"] = dbst.T
        sgo["ln_v_g"], sgo["ln_v_b"] = dlg, dlb
        dqkv, sgo["sinks"] = _swa_bwd(t + "swa_bwd", sv["qr"], sv["kr"], sv["proj"], dattn, sv["attn"], sv["lse"],
                                      small["sink_rows"][l], cos4, nsin4, after=(tok_b, tok_c))
        grads[l].update(red_c.done(after=(dqkv,)))
        pieces = (duv, dqkv, (dgate, 0), (dgate, 1))
        widths = (2 * GMLP_W, ATT_W + 2 * KV_W, d, d)
        tok_b = red_b.exchange(after=(dqkv, duv))
        red_a = _Reduce(t + "rs_a", GROUPS_FWD[0],
                        {"w_in": _dw_pieces_half(t + "dw_in_s", sv["xb"], pieces, widths, sidx, after=(tok_b,))},
                        mcidx)
        g = _dx_pieces(t + "dx0", pieces, widths, full["w_in"], dr1, after=(red_a.token,))
        tok_a = red_a.scatter(
            lambda got: {"w_in": _dw_pieces_half(t + "dw_in_o", sv["xb"], pieces, widths, cidx, got["w_in"])},
            after=(g,))
        pend_a, pend_b = red_a, red_b
        small_g[l] = sgo
    grad_x = g.reshape(x.shape)

    big_out = {}

    def adam_layer(l, names, after):
        done = []
        for n in names:
            prev = big_out.get(n)
            big_out[n] = _adamw(f"adamw{l}_{n}", wts[n], grads[l][n], mom_m[n], mom_v[n], l, prev, after=after)
            done.append(big_out[n][0])
        return done

    shapes = [wts[n].shape for n in SMALL]
    packed_g = _pack([jnp.stack([small_g[l][n].reshape(wts[n].shape[1:]) for l in range(DEPTH)]) for n in SMALL])
    me8 = jnp.reshape(4 * ax_x + 2 * ax_y + ax_c, (1,)).astype(jnp.int32)
    ar_s, ar_r, ar_land, tok_ar = _split_start("ar_start", [packed_g], [_place_slot("ar_place", packed_g, me8)],
                                               _mk_small, 7, after=(tok_a,))
    fill = []
    for names in GROUPS_FWD:
        fill += adam_layer(1, names, (tok_ar,))
    grads[0].update(pend_b.done(after=tuple(fill)))
    fill += adam_layer(0, GROUPS_FWD[1], (tok_ar,))
    ar_land = _split_wait("ar_wait", [packed_g], ar_land, ar_s, ar_r, _mk_small, after=tuple(fill))
    packed_g = _sum_slots("ar_sum", ar_land[0])
    pw, pm, pv = (_pack([src[n] for n in SMALL]) for src in (wts, mom_m, mom_v))
    small4 = _adamw("adamw_small", pw[None], packed_g, pm[None], pv[None], 0)
    small_out = [dict(zip(SMALL, _unpack(a[0], shapes))) for a in small4]
    tok_a = pend_a.exchange(after=(small4[0],))
    fill = adam_layer(0, GROUPS_FWD[2], (tok_a,))
    grads[0].update(pend_a.done(after=tuple(fill)))
    adam_layer(0, GROUPS_FWD[0], ())

    def pick(kind, n):
        return big_out[n][kind] if n in big_out else small_out[kind][n]

    return (loss, grad_x, *[pick(0, n) for n in WEIGHTS], *[pick(1, n) for n in WEIGHTS],
            *[pick(2, n) for n in WEIGHTS], *[pick(3, n) for n in WEIGHTS])
```

```python
import math

import jax
import jax.numpy as jnp
from jax import lax
from jax.experimental import pallas as pl
from jax.experimental.pallas import tpu as pltpu

F32 = jnp.float32
BF16 = jnp.bfloat16
MESH = pl.DeviceIdType.MESH
ANY = pl.BlockSpec(memory_space=pl.ANY)
HBM = pl.BlockSpec(memory_space=pltpu.HBM)
SEM = pl.BlockSpec(memory_space=pltpu.SEMAPHORE)
VMEM_SPEC = pl.BlockSpec(memory_space=pltpu.VMEM)
EFFECT = pltpu.SideEffectType.DATAFLOW_SIDE_EFFECTING

DEPTH = 2
CHUNK = 128
GMLP_W = 1024
GROUPS = 8
NQ, NKV, HD = 16, 4, 64
ATT_W = NQ * HD
KV_W = NKV * HD
XH, XHD = 4, 128
X_W = XH * XHD
LN_EPS = 1e-5
ALPHA = (2 * DEPTH) ** 0.25
OFF_Q = 2 * GMLP_W
OFF_K = OFF_Q + ATT_W
OFF_VA = OFF_K + KV_W
OFF_GA = OFF_VA + KV_W
NEG = -1e30

ADAM_LR, ADAM_B1, ADAM_B2, ADAM_EPS, ADAM_WD, ADAM_STEP = 0.001, 0.9, 0.999, 1e-08, 0.01, 10

V7X_VMEM_BYTES = 64 * 1024 * 1024
VMEM_LIMIT = V7X_VMEM_BYTES - 4 * 1024 * 1024
LANE = 128

BIG = ("w_in", "w_br_a", "w_br_b", "w_o", "w_xq", "w_xkv", "w_xo", "w_up", "w_down")
SHARD_AXIS = {"w_in": 1, "w_br_a": 1, "w_br_b": 1, "w_o": 0, "w_xq": 0, "w_xkv": 0, "w_xo": 1,
              "w_up": 1, "w_down": 0}
GROUPS_GATHER = (("w_in",), ("w_br_a", "w_br_b", "w_o", "w_xq", "w_xkv", "w_xo"), ("w_up",), ("w_down",))
GROUPS_FWD = (("w_in",), ("w_br_a", "w_br_b", "w_o", "w_xq", "w_xkv", "w_xo"), ("w_up", "w_down"))
SMALL = ("b_gate", "ln_v_g", "ln_v_b", "w_s", "b_s", "sinks", "ln1_g", "ln1_b", "ln2_g", "ln2_b",
         "ln3_g", "ln3_b")
WEIGHTS = ("w_in", "b_gate", "ln_v_g", "ln_v_b", "w_s", "b_s", "sinks", "w_br_a", "w_br_b", "w_o",
           "ln1_g", "ln1_b", "w_xq", "w_xkv", "w_xo", "ln2_g", "ln2_b", "w_up", "w_down", "ln3_g", "ln3_b")


def _pallas(body, after=(), **kw):
    n_after = len(after)
    if not n_after:
        return pl.pallas_call(body, **kw)
    n_in = len(kw["in_specs"])
    kw["in_specs"] = list(kw["in_specs"]) + [ANY] * n_after

    def tied(*refs):
        return body(*refs[:n_in], *refs[n_in + n_after:])

    call = pl.pallas_call(tied, **kw)
    return lambda *ops: call(*ops, *after)


def _params(**kw):
    return pltpu.CompilerParams(vmem_limit_bytes=VMEM_LIMIT, **kw)


def _tile(dim, pref, unit=LANE):
    best = None
    t = unit
    while t <= min(dim, pref):
        if dim % t == 0:
            best = t
        t += unit
    return best if best is not None else dim


def _dot(a, b, dims):
    return lax.dot_general(a, b, (dims, ((), ())), preferred_element_type=F32)


NN = ((1,), (0,))
NT = ((1,), (1,))
TN = ((0,), (0,))


def _bf(x):
    return x if x.dtype == BF16 else x.astype(BF16)


def _sds(shape, dtype):
    return jax.ShapeDtypeStruct(shape, dtype)


def _mm(name, a, b, *, dims, grid, a_spec, b_spec, out_shape, out_specs, epilogue,
        extras=(), extra_specs=(), after=()):
    assert grid[2] == 1
    n_ex, n_out = len(extras), len(out_shape)

    def body(*refs):
        ex = refs[2:2 + n_ex]
        outs = refs[2 + n_ex:2 + n_ex + n_out]
        epilogue(_dot(_bf(refs[0][...]), _bf(refs[1][...]), dims), ex, outs)

    return _pallas(
        body, after=after, name=name, grid=grid, in_specs=[a_spec, b_spec, *extra_specs], out_specs=list(out_specs),
        out_shape=list(out_shape), compiler_params=_params(dimension_semantics=("arbitrary",) * 3),
    )(a, b, *extras)


def _store(acc, ex, outs):
    for o in outs:
        o[...] = acc.astype(o.dtype)


def _ln_rows(r, g, b):
    mu = jnp.mean(r, axis=-1, keepdims=True)
    xc = r - mu
    var = jnp.mean(xc * xc, axis=-1, keepdims=True)
    rstd = lax.rsqrt(var + LN_EPS)
    xhat = xc * rstd
    return xhat * g + b, xhat, rstd


def _ep_add_scaled(acc, ex, outs):
    outs[0][...] = acc + ALPHA * ex[0][...]


def _ln_fwd(name, r, g, b):
    s, d = r.shape
    bm = _tile(s, 512)

    def body(r_ref, g_ref, b_ref, y_ref, yb_ref):
        y, _, _ = _ln_rows(r_ref[...], g_ref[...], b_ref[...])
        y_ref[...] = y
        yb_ref[...] = y.astype(BF16)

    row = pl.BlockSpec((bm, d), lambda i: (i, 0))
    vec = pl.BlockSpec((1, d), lambda i: (0, 0))
    return _pallas(body, name=name, grid=(s // bm,), in_specs=[row, vec, vec], out_specs=[row, row],
                   out_shape=[_sds((s, d), F32), _sds((s, d), BF16)], compiler_params=_params())(r, g, b)


_GC = math.sqrt(2.0 / math.pi)


def _gelu(x):
    t = jnp.tanh(_GC * (x + 0.044715 * (x * x * x)))
    return 0.5 * x * (1.0 + t), t


def _gelu_grad(x, t):
    return 0.5 * (1.0 + t) + 0.5 * x * (1.0 - t * t) * (_GC * (1.0 + 3.0 * 0.044715 * x * x))


def _sigmoid(x):
    return 0.5 * jnp.tanh(0.5 * x) + 0.5


GRP = NQ // NKV


def _band_mask(prev_ok, prev_only=False):
    rows = CHUNK if prev_only else 2 * CHUNK
    key = lax.broadcasted_iota(jnp.int32, (rows, GRP * CHUNK), 0)
    qry = jnp.bitwise_and(lax.broadcasted_iota(jnp.int32, (rows, GRP * CHUNK), 1), CHUNK - 1)
    prev = jnp.logical_and(jnp.logical_and(key < CHUNK, key > qry), prev_ok)
    if prev_only:
        return prev
    return jnp.logical_or(prev, jnp.logical_and(key >= CHUNK, key - CHUNK <= qry))


def _pair(x, g):
    return x[:, (g // 2) * LANE:(g // 2 + 1) * LANE]


def _own_head(x, g):
    xp = _pair(x, g)
    lane = lax.broadcasted_iota(jnp.int32, xp.shape, 1)
    lo = (g % 2) * HD
    return jnp.where(jnp.logical_and(lane >= lo, lane < lo + HD), xp, jnp.zeros_like(xp))


def _stack_heads(x, g, dtype=BF16):
    a = x[:, g * GRP * HD:g * GRP * HD + LANE]
    b = x[:, g * GRP * HD + LANE:(g + 1) * GRP * HD]
    ar, br = pltpu.roll(a, HD, 1), pltpu.roll(b, HD, 1)
    parts = [a, ar, b, br] if g % 2 == 0 else [ar, a, br, b]
    return jnp.concatenate(parts, axis=0).astype(dtype)


def _unstack_heads(og, g):
    o = [og[h * CHUNK:(h + 1) * CHUNK] for h in range(GRP)]
    lo = lax.broadcasted_iota(jnp.int32, (CHUNK, LANE), 1) < HD
    if g % 2 == 0:
        x0, x1, x2, x3 = o[0], pltpu.roll(o[1], HD, 1), o[2], pltpu.roll(o[3], HD, 1)
    else:
        x0, x1, x2, x3 = pltpu.roll(o[0], HD, 1), o[1], pltpu.roll(o[2], HD, 1), o[3]
    return [jnp.where(lo, x0, x1), jnp.where(lo, x2, x3)]


def _stack_rows(x, g):
    return jnp.concatenate([x[g * GRP + h:g * GRP + h + 1] for h in range(GRP)], axis=-1)


def _head_lane_sums(x, g):
    lane = lax.broadcasted_iota(jnp.int32, (8, LANE), 1)
    lo_lane = (g % 2) * HD
    sel = jnp.where(jnp.logical_and(lane >= lo_lane, lane < lo_lane + HD), 1.0, 0.0).astype(BF16)
    hi = x.astype(BF16)
    lo = (x - hi.astype(F32)).astype(BF16)
    return (_dot(sel, hi, NT) + _dot(sel, lo, NT))[0:1]


def _rope(x, cos, sin_signed):
    w = x.shape[-1]
    lane = lax.broadcasted_iota(jnp.int32, x.shape, 1)
    first = (lane % HD) < (HD // 2)
    partner = jnp.where(first, pltpu.roll(x, w - HD // 2, 1), pltpu.roll(x, HD // 2, 1))
    reps = w // LANE
    return x * jnp.tile(cos, (1, reps)) + partner * jnp.tile(sin_signed, (1, reps))


def _cast2d(name, x, after=()):
    s, d = x.shape
    bm = _tile(s, 512, 8)

    def body(x_ref, o_ref):
        o_ref[...] = x_ref[...].astype(BF16)

    spec = pl.BlockSpec((bm, d), lambda i: (i, 0))
    return _pallas(body, after=after, name=name, grid=(s // bm,), in_specs=[spec], out_specs=spec,
                   out_shape=_sds(x.shape, BF16), compiler_params=_params())(x)


def _place():
    x, y, c = lax.axis_index("x"), lax.axis_index("y"), lax.axis_index("c")
    chips = [(1 - x, y), (x, 1 - y), (1 - x, 1 - y)]
    return x, y, c, chips


def _cut(ref, axis, chip=None, half=None):
    k, n = ref.shape[-2], ref.shape[-1]
    rows, cols = slice(None), slice(None)
    if chip is not None:
        if axis == 0:
            rows = pl.ds(pl.multiple_of(chip * (k // 4), 8), k // 4)
        else:
            cols = pl.ds(pl.multiple_of(chip * (n // 4), LANE), n // 4)
    if half is not None:
        if axis == 0:
            cols = pl.ds(pl.multiple_of(half * (n // 2), LANE), n // 2)
        else:
            rows = pl.ds(pl.multiple_of(half * (k // 2), 8), k // 2)
    return ref.at[rows, cols]


def _split_start(name, srcs, lands, make, n_sem, after=()):
    ns, nl, na = len(srcs), len(lands), len(after)

    def body(*refs):
        src, land = refs[:ns], refs[ns:ns + nl]
        outs = refs[ns + nl + na:]
        for out_cp, _ in make(src, land, outs[0], outs[1]):
            out_cp.start()
        outs[-1][...] = jnp.zeros_like(outs[-1])

    res = pl.pallas_call(
        body, name=name, in_specs=[HBM] * (ns + nl) + [ANY] * na,
        out_specs=[SEM, SEM] + [HBM] * nl + [VMEM_SPEC],
        out_shape=[pltpu.SemaphoreType.DMA((n_sem,)), pltpu.SemaphoreType.DMA((n_sem,))]
        + [pltpu.HBM(a.shape, a.dtype) for a in lands] + [_sds((8, LANE), F32)],
        input_output_aliases={ns + i: 2 + i for i in range(nl)},
        compiler_params=pltpu.CompilerParams(has_side_effects=EFFECT),
    )(*[pltpu.with_memory_space_constraint(a, pltpu.HBM) for a in (*srcs, *lands)], *after)
    return res[0], res[1], list(res[2:2 + nl]), res[-1]


def _split_wait(name, srcs, lands, ssem, rsem, make, after=()):
    ns, nl, na = len(srcs), len(lands), len(after)

    def body(*refs):
        src, land = refs[:ns], refs[ns:ns + nl]
        s_ref, r_ref = refs[ns + nl], refs[ns + nl + 1]
        pairs = make(src, land, s_ref, r_ref)
        for _, in_cp in pairs:
            in_cp.wait_recv()
        for out_cp, _ in pairs:
            out_cp.wait_send()

    res = pl.pallas_call(
        body, name=name, in_specs=[HBM] * (ns + nl) + [SEM, SEM] + [ANY] * na,
        out_specs=[HBM] * nl, out_shape=[pltpu.HBM(a.shape, a.dtype) for a in lands],
        input_output_aliases={ns + i: i for i in range(nl)},
        compiler_params=pltpu.CompilerParams(has_side_effects=EFFECT),
    )(*srcs, *lands, ssem, rsem, *after)
    return list(res)


def _rcopy(src, dst, ssem, rsem, k, dev):
    return pltpu.make_async_remote_copy(src_ref=src, dst_ref=dst, send_sem=ssem.at[k], recv_sem=rsem.at[k],
                                        device_id=dev, device_id_type=MESH)


def _mk_gather_ici(axes):
    def make(src, land, ssem, rsem):
        x, y, c, chips = _place()
        me = 2 * x + y
        pairs = []
        for w, ax in enumerate(axes):
            mine = _cut(land[w], ax, chip=me, half=c)
            for j, (px, py) in enumerate(chips):
                dev = (px, py, c)
                got = _cut(land[w], ax, chip=2 * px + py, half=c)
                pairs.append((_rcopy(mine, mine, ssem, rsem, 3 * w + j, dev),
                              _rcopy(got, got, ssem, rsem, 3 * w + j, dev)))
        return pairs
    return make


def _mk_gather_d2d(axes):
    def make(src, land, ssem, rsem):
        x, y, c, chips = _place()
        sib = (x, y, 1 - c)
        pairs = []
        for w, ax in enumerate(axes):
            for j, (px, py) in enumerate(chips):
                have = _cut(land[w], ax, chip=2 * px + py, half=c)
                want = _cut(land[w], ax, chip=2 * px + py, half=1 - c)
                pairs.append((_rcopy(have, have, ssem, rsem, 3 * w + j, sib),
                              _rcopy(want, want, ssem, rsem, 3 * w + j, sib)))
        return pairs
    return make


def _mk_swap(src, land, ssem, rsem):
    x, y, c, _ = _place()
    pairs = []
    for w in range(len(src)):
        cp = _rcopy(src[w], land[w], ssem, rsem, w, (x, y, 1 - c))
        pairs.append((cp, cp))
    return pairs


def _mk_scatter(axes):
    def make(src, land, ssem, rsem):
        x, y, c, chips = _place()
        pairs = []
        for w, ax in enumerate(axes):
            for j, (px, py) in enumerate(chips):
                cp = _rcopy(_cut(src[w], ax, chip=2 * px + py), land[w].at[j], ssem, rsem, 3 * w + j, (px, py, c))
                pairs.append((cp, cp))
        return pairs
    return make


def _mk_exchange(axes):
    def make(src, land, ssem, rsem):
        x, y, c, _ = _place()
        sib = (x, y, 1 - c)
        pairs = []
        for w, ax in enumerate(axes):
            have = _cut(land[w], ax, half=c)
            want = _cut(land[w], ax, half=1 - c)
            pairs.append((_rcopy(have, have, ssem, rsem, w, sib), _rcopy(want, want, ssem, rsem, w, sib)))
        return pairs
    return make


def _place_own(name, shard, axis, meidx, after=()):
    _, r, c = shard.shape
    full = (4 * r, c) if axis == 0 else (r, 4 * c)
    br = _tile(r, 512, 8)
    nb = r // br
    if axis == 0:
        ospec = pl.BlockSpec((br, c), lambda i, me: (me[0] * nb + i, 0))
    else:
        ospec = pl.BlockSpec((br, c), lambda i, me: (i, me[0]))
    n_after = len(after)

    def body(me_ref, s_ref, *rest):
        o0_ref, o1_ref = rest[n_after:]
        o0_ref[...] = s_ref[0].astype(BF16)
        o1_ref[...] = s_ref[1].astype(BF16)

    return pl.pallas_call(
        body, name=name,
        grid_spec=pltpu.PrefetchScalarGridSpec(
            num_scalar_prefetch=1, grid=(nb,),
            in_specs=[pl.BlockSpec((2, br, c), lambda i, me: (0, i, 0))] + [ANY] * n_after,
            out_specs=[ospec, ospec]),
        out_shape=[_sds(full, BF16)] * 2, compiler_params=_params(),
    )(meidx, shard, *after)


def _sum_half(name, own, slots, axis, mc):
    _, r, cc = slots.shape
    br = _tile(r, 256, 8)
    nb = r // br
    if axis == 0:
        own_spec = pl.BlockSpec((br, cc), lambda i, mc: (mc[0] * nb + i, 0))
        out_spec = pl.BlockSpec((br, cc), lambda i, mc: (i, mc[1]))
        shape = (r, 2 * cc)
    else:
        own_spec = pl.BlockSpec((br, cc), lambda i, mc: (i, mc[0]))
        out_spec = pl.BlockSpec((br, cc), lambda i, mc: (mc[1] * nb + i, 0))
        shape = (2 * r, cc)

    def body(mc_ref, own_ref, s_ref, o_ref):
        acc = own_ref[...].astype(F32)
        for i in range(3):
            acc = acc + s_ref[i].astype(F32)
        o_ref[...] = acc

    return pl.pallas_call(
        body, name=name,
        grid_spec=pltpu.PrefetchScalarGridSpec(
            num_scalar_prefetch=1, grid=(nb,),
            in_specs=[own_spec, pl.BlockSpec((3, br, cc), lambda i, mc: (0, i, 0))], out_specs=out_spec),
        out_shape=_sds(shape, F32), compiler_params=_params(),
    )(mc, own, slots)


def _mk_small(src, land, ssem, rsem):
    x, y, c, _ = _place()
    me = 4 * x + 2 * y + c
    pairs = []
    for k in range(1, 8):
        peer = (1 - x if k & 4 else x, 1 - y if k & 2 else y, 1 - c if k & 1 else c)
        got = land[0].at[4 * peer[0] + 2 * peer[1] + peer[2]]
        pairs.append((_rcopy(src[0], land[0].at[me], ssem, rsem, k - 1, peer),
                      _rcopy(got, got, ssem, rsem, k - 1, peer)))
    return pairs


def _place_slot(name, packed, me8):
    rows, lanes = packed.shape
    br = _tile(rows, 512, 8)

    def body(me_ref, p_ref, o_ref):
        o_ref[...] = p_ref[...]

    return pl.pallas_call(
        body, name=name,
        grid_spec=pltpu.PrefetchScalarGridSpec(
            num_scalar_prefetch=1, grid=(rows // br,),
            in_specs=[pl.BlockSpec((br, lanes), lambda i, me: (i, 0))],
            out_specs=pl.BlockSpec((None, br, lanes), lambda i, me: (me[0], i, 0))),
        out_shape=_sds((8, rows, lanes), F32), compiler_params=_params(),
    )(me8, packed)


def _sum_slots(name, slots):
    _, rows, lanes = slots.shape
    br = _tile(rows, 512, 8)

    def body(s_ref, o_ref):
        acc = s_ref[0]
        for i in range(1, 8):
            acc = acc + s_ref[i]
        o_ref[...] = acc

    return pl.pallas_call(
        body, name=name, grid=(rows // br,), in_specs=[pl.BlockSpec((8, br, lanes), lambda i: (0, i, 0))],
        out_specs=pl.BlockSpec((br, lanes), lambda i: (i, 0)), out_shape=_sds((rows, lanes), F32),
        compiler_params=_params(),
    )(slots)


def _adamw_math(w, g, m, v):
    m2 = ADAM_B1 * m + (1.0 - ADAM_B1) * g
    v2 = ADAM_B2 * v + (1.0 - ADAM_B2) * (g * g)
    m_hat = m2 / (1.0 - ADAM_B1 ** ADAM_STEP)
    v_hat = v2 / (1.0 - ADAM_B2 ** ADAM_STEP)
    delta = -ADAM_LR * (m_hat / (jnp.sqrt(v_hat) + ADAM_EPS) + ADAM_WD * w)
    return delta, m2, v2


def _adamw(name, w, g, m, v, layer, prev=None, after=()):
    _, r, c = w.shape
    br = _tile(r, 256, 8)
    n_prev = 0 if prev is None else 4

    def body(*refs):
        w_ref, g_ref, m_ref, v_ref = refs[:4]
        go_ref, d_ref, mo_ref, vo_ref = refs[4 + n_prev:]
        gg = g_ref[...]
        delta, m2, v2 = _adamw_math(w_ref[...], gg, m_ref[...], v_ref[...])
        go_ref[...] = gg
        d_ref[...] = delta
        mo_ref[...] = m2
        vo_ref[...] = v2

    spec = pl.BlockSpec((None, br, c), lambda i: (layer, i, 0))
    return _pallas(
        body, after=after, name=name, grid=(r // br,),
        in_specs=[spec, pl.BlockSpec((br, c), lambda i: (i, 0)), spec, spec] + [ANY] * n_prev,
        out_specs=[spec] * 4, out_shape=[_sds(w.shape, F32)] * 4,
        input_output_aliases={4 + i: i for i in range(n_prev)}, compiler_params=_params(),
    )(w, g, m, v, *(prev or ()))


def _gmlp_fwd(name, proj, ln_g, ln_b, w_s, b_st):
    s = proj.shape[0]

    def body(u_ref, v_ref, g_ref, b_ref, ws_ref, bst_ref, sg_ref):
        gu, _ = _gelu(u_ref[...])
        gv, _ = _gelu(v_ref[...])
        vn, _, _ = _ln_rows(gv, g_ref[...], b_ref[...])
        vn = vn.astype(BF16)
        row = lax.broadcasted_iota(jnp.int32, (CHUNK, CHUNK), 0)
        col = lax.broadcasted_iota(jnp.int32, (CHUNK, CHUNK), 1)
        tril = col <= row
        outs = []
        for g in range(GROUPS):
            sl = slice(g * LANE, (g + 1) * LANE)
            w = jnp.where(tril, ws_ref[g], 0.0).astype(BF16)
            mixed = _dot(w, vn[:, sl], NN) + bst_ref[:, g:g + 1]
            outs.append(gu[:, sl] * mixed)
        sg_ref[...] = jnp.concatenate(outs, axis=-1).astype(BF16)

    return _pallas(
        body, name=name, grid=(s // CHUNK,),
        in_specs=[pl.BlockSpec((CHUNK, GMLP_W), lambda n: (n, 0)), pl.BlockSpec((CHUNK, GMLP_W), lambda n: (n, 1)),
                  pl.BlockSpec((1, GMLP_W), lambda n: (0, 0)), pl.BlockSpec((1, GMLP_W), lambda n: (0, 0)),
                  pl.BlockSpec((GROUPS, CHUNK, CHUNK), lambda n: (0, 0, 0)),
                  pl.BlockSpec((CHUNK, GROUPS), lambda n: (0, 0))],
        out_specs=pl.BlockSpec((CHUNK, GMLP_W), lambda n: (n, 0)),
        out_shape=_sds((s, GMLP_W), BF16), compiler_params=_params(),
    )(proj, proj, ln_g, ln_b, w_s, b_st)


def _swa_fwd(name, proj, cos4, sin4, sinks, after=()):
    s = proj.shape[0]
    w = CHUNK
    scale = HD ** -0.5

    def body(q_ref, k_ref, v_ref, cos_ref, sin_ref, sink_ref, o_ref, qr_ref, kr_ref, lse_ref, kprev, vprev):
        n = pl.program_id(0)

        @pl.when(n == 0)
        def _():
            kprev[...] = jnp.zeros_like(kprev)
            vprev[...] = jnp.zeros_like(vprev)

        cos, sin = cos_ref[...], sin_ref[...]
        qr = _rope(q_ref[...], cos, sin)
        kr = _rope(k_ref[...], cos, sin).astype(BF16)
        vb = v_ref[...].astype(BF16)
        kk = jnp.concatenate([kprev[...], kr], axis=0)
        vv = jnp.concatenate([vprev[...], vb], axis=0)
        valid = _band_mask(n > 0)
        outs, lses = [], []
        for g in range(NKV):
            sc = jnp.where(valid, _dot(_own_head(kk, g), _stack_heads(qr, g), NT) * scale, NEG)
            sink = sink_ref[g]
            mx = jnp.maximum(jnp.max(sc, axis=0, keepdims=True), sink)
            p = jnp.exp(sc - mx)
            den = jnp.sum(p, axis=0, keepdims=True) + jnp.exp(sink - mx)
            og = _dot((p * (1.0 / den)).astype(BF16), _pair(vv, g), TN)
            outs.extend(_unstack_heads(og, g))
            lg = mx + jnp.log(den)
            lses.extend([lg[:, h * w:(h + 1) * w] for h in range(GRP)])
        o_ref[...] = jnp.concatenate(outs, axis=-1).astype(BF16)
        lse_ref[...] = jnp.concatenate(lses, axis=0)
        qr_ref[...] = qr.astype(BF16)
        kr_ref[...] = kr
        kprev[...] = kr
        vprev[...] = vb

    return _pallas(
        body, after=after, name=name, grid=(s // w,),
        in_specs=[pl.BlockSpec((w, ATT_W), lambda n: (n, OFF_Q // ATT_W)),
                  pl.BlockSpec((w, KV_W), lambda n: (n, OFF_K // KV_W)),
                  pl.BlockSpec((w, KV_W), lambda n: (n, OFF_VA // KV_W)),
                  pl.BlockSpec((w, LANE), lambda n: (n, 0)), pl.BlockSpec((w, LANE), lambda n: (n, 0)),
                  pl.BlockSpec((NKV, 1, GRP * w), lambda n: (0, 0, 0))],
        out_specs=[pl.BlockSpec((w, ATT_W), lambda n: (n, 0)), pl.BlockSpec((w, ATT_W), lambda n: (n, 0)),
                   pl.BlockSpec((w, KV_W), lambda n: (n, 0)), pl.BlockSpec((None, NQ, w), lambda n: (n, 0, 0))],
        out_shape=[_sds((s, ATT_W), BF16), _sds((s, ATT_W), BF16), _sds((s, KV_W), BF16),
                   _sds((s // w, NQ, w), F32)],
        scratch_shapes=[pltpu.VMEM((w, KV_W), BF16), pltpu.VMEM((w, KV_W), BF16)],
        compiler_params=_params(dimension_semantics=("arbitrary",)),
    )(proj, proj, proj, cos4, sin4, sinks)


def _gate_fwd(name, sg, attn, wa, wb, proj, b_gate, d):
    s = sg.shape[0]
    bm, bn = _tile(s, 1024), _tile(d, 512)
    off_a, off_b = OFF_GA // bn, (OFF_GA + d) // bn

    def body(sg_ref, at_ref, wa_ref, wb_ref, ga_ref, gb_ref, ba_ref, bb_ref, m_ref, ya_ref, yb_ref, sa_ref, sb_ref):
        ya = _dot(sg_ref[...], wa_ref[...], NN)
        yb = _dot(at_ref[...], wb_ref[...], NN)
        sa = _sigmoid(ga_ref[...] + ba_ref[...])
        sb = _sigmoid(gb_ref[...] + bb_ref[...])
        m_ref[...] = (sa * ya + sb * yb).astype(BF16)
        ya_ref[...] = ya.astype(BF16)
        yb_ref[...] = yb.astype(BF16)
        sa_ref[...] = sa.astype(BF16)
        sb_ref[...] = sb.astype(BF16)

    tile = pl.BlockSpec((bm, bn), lambda i, j: (i, j))
    return _pallas(
        body, name=name, grid=(s // bm, d // bn),
        in_specs=[pl.BlockSpec((bm, GMLP_W), lambda i, j: (i, 0)), pl.BlockSpec((bm, ATT_W), lambda i, j: (i, 0)),
                  pl.BlockSpec((GMLP_W, bn), lambda i, j: (0, j)), pl.BlockSpec((ATT_W, bn), lambda i, j: (0, j)),
                  pl.BlockSpec((bm, bn), lambda i, j: (i, off_a + j)),
                  pl.BlockSpec((bm, bn), lambda i, j: (i, off_b + j)),
                  pl.BlockSpec((1, bn), lambda i, j: (0, j)), pl.BlockSpec((1, bn), lambda i, j: (0, d // bn + j))],
        out_specs=[tile] * 5, out_shape=[_sds((s, d), BF16)] * 5,
        compiler_params=_params(),
    )(sg, attn, wa, wb, proj, proj, b_gate, b_gate)


def _xattn_fwd(name, xb, xf, wq, kv, wo, ln_g, ln_b, after=()):
    s, d = xf.shape
    mem = kv.shape[0]
    bm = _tile(s, 512)
    scale = XHD ** -0.5

    def body(xb_ref, xf_ref, wq_ref, kv_ref, wo_ref, g_ref, b_ref, q_out, o_out, r_out, y_out, yb_out):
        qb = _dot(xb_ref[...], wq_ref[...], NN).astype(BF16)
        kvv = kv_ref[...]
        outs = []
        for h in range(XH):
            hs = slice(h * XHD, (h + 1) * XHD)
            vs = slice(X_W + h * XHD, X_W + (h + 1) * XHD)
            sc = _dot(qb[:, hs], kvv[:, hs], NT) * scale
            mx = jnp.max(sc, axis=-1, keepdims=True)
            p = jnp.exp(sc - mx)
            p = p / jnp.sum(p, axis=-1, keepdims=True)
            outs.append(_dot(p.astype(BF16), kvv[:, vs], NN))
        ob = jnp.concatenate(outs, axis=-1).astype(BF16)
        yv = _dot(ob, wo_ref[...], NN)
        r = ALPHA * xf_ref[...] + yv
        yn, _, _ = _ln_rows(r, g_ref[...], b_ref[...])
        q_out[...] = qb
        o_out[...] = ob
        r_out[...] = r
        y_out[...] = yn
        yb_out[...] = yn.astype(BF16)

    row = lambda wd: pl.BlockSpec((bm, wd), lambda i: (i, 0))
    return _pallas(
        body, after=after, name=name, grid=(s // bm,),
        in_specs=[row(d), row(d), pl.BlockSpec((d, X_W), lambda i: (0, 0)),
                  pl.BlockSpec((mem, 2 * X_W), lambda i: (0, 0)), pl.BlockSpec((X_W, d), lambda i: (0, 0)),
                  pl.BlockSpec((1, d), lambda i: (0, 0)), pl.BlockSpec((1, d), lambda i: (0, 0))],
        out_specs=[row(X_W), row(X_W), row(d), row(d), row(d)],
        out_shape=[_sds((s, X_W), BF16), _sds((s, X_W), BF16), _sds((s, d), F32), _sds((s, d), F32),
                   _sds((s, d), BF16)],
        compiler_params=_params(),
    )(xb, xf, wq, kv, wo, ln_g, ln_b)


def _accumulate(i, refs, vals):
    @pl.when(i == 0)
    def _():
        for ref, v in zip(refs, vals):
            ref[...] = v

    @pl.when(i > 0)
    def _():
        for ref, v in zip(refs, vals):
            ref[...] += v


def _ln_bwd_rows(dyv, r, g, dr_ref, drb_ref):
    _, xhat, rstd = _ln_rows(r, g, 0.0)
    dxh = dyv * g
    m1 = jnp.mean(dxh, axis=-1, keepdims=True)
    m2 = jnp.mean(dxh * xhat, axis=-1, keepdims=True)
    dr = rstd * (dxh - m1 - xhat * m2)
    dr_ref[...] = dr
    drb_ref[...] = dr.astype(BF16)
    return jnp.sum(dyv * xhat, axis=0, keepdims=True), jnp.sum(dyv, axis=0, keepdims=True)


def _ln_bwd(name, dy, r, g, after=()):
    s, d = r.shape
    bm = _tile(s, 512)

    def body(dy_ref, r_ref, g_ref, dr_ref, drb_ref, dg_ref, db_ref):
        dg, db = _ln_bwd_rows(dy_ref[...], r_ref[...], g_ref[...], dr_ref, drb_ref)
        _accumulate(pl.program_id(0), (dg_ref, db_ref), (dg, db))

    row = pl.BlockSpec((bm, d), lambda i: (i, 0))
    vec = pl.BlockSpec((1, d), lambda i: (0, 0))
    return _pallas(
        body, after=after, name=name, grid=(s // bm,), in_specs=[row, row, vec], out_specs=[row, row, vec, vec],
        out_shape=[_sds((s, d), F32), _sds((s, d), BF16), _sds((1, d), F32), _sds((1, d), F32)],
        compiler_params=_params(dimension_semantics=("arbitrary",)),
    )(dy, r, g)


def _loss_ln_bwd(name, r, g, b, tgt):
    s, d = r.shape
    bm = _tile(s, 512)

    def body(r_ref, g_ref, b_ref, t_ref, dr_ref, drb_ref, dg_ref, db_ref, loss_ref):
        rv, gv = r_ref[...], g_ref[...]
        y, _, _ = _ln_rows(rv, gv, b_ref[...])
        err = y - t_ref[...]
        part = 0.5 * jnp.sum(jnp.sum(err * err, axis=-1, keepdims=True), axis=0, keepdims=True) * (1.0 / d)
        dg, db = _ln_bwd_rows(err * (1.0 / d), rv, gv, dr_ref, drb_ref)
        _accumulate(pl.program_id(0), (dg_ref, db_ref, loss_ref), (dg, db, part))

    row = pl.BlockSpec((bm, d), lambda i: (i, 0))
    vec = pl.BlockSpec((1, d), lambda i: (0, 0))
    return _pallas(
        body, name=name, grid=(s // bm,), in_specs=[row, vec, vec, row],
        out_specs=[row, row, vec, vec, pl.BlockSpec((1, 1), lambda i: (0, 0))],
        out_shape=[_sds((s, d), F32), _sds((s, d), BF16), _sds((1, d), F32), _sds((1, d), F32), _sds((1, 1), F32)],
        compiler_params=_params(dimension_semantics=("arbitrary",)),
    )(r, g, b, tgt)


def _xattn_bwd(name, dyb, drf, q, kv, wo, wq):
    s, d = drf.shape
    mem = kv.shape[0]
    bm = _tile(s, 512)
    scale = XHD ** -0.5

    def body(dy_ref, dr_ref, q_ref, kv_ref, wo_ref, wq_ref, dx_out, dq_out, dkv_out):
        i = pl.program_id(0)
        dob = _dot(dy_ref[...], wo_ref[...], NT).astype(BF16)
        qb = q_ref[...]
        kvv = kv_ref[...]
        dqs, dks, dvs = [], [], []
        for h in range(XH):
            hs = slice(h * XHD, (h + 1) * XHD)
            vs = slice(X_W + h * XHD, X_W + (h + 1) * XHD)
            sc = _dot(qb[:, hs], kvv[:, hs], NT) * scale
            mx = jnp.max(sc, axis=-1, keepdims=True)
            p = jnp.exp(sc - mx)
            p = p / jnp.sum(p, axis=-1, keepdims=True)
            dp = _dot(dob[:, hs], kvv[:, vs], NT)
            dsum = jnp.sum(p * dp, axis=-1, keepdims=True)
            dsb = (p * (dp - dsum) * scale).astype(BF16)
            dqs.append(_dot(dsb, kvv[:, hs], NN))
            dks.append(_dot(dsb, qb[:, hs], TN))
            dvs.append(_dot(p.astype(BF16), dob[:, hs], TN))
        dqb = jnp.concatenate(dqs, axis=-1).astype(BF16)
        dq_out[...] = dqb
        dx_out[...] = _dot(dqb, wq_ref[...], NT) + ALPHA * dr_ref[...]
        dkv = jnp.concatenate(dks + dvs, axis=-1)

        @pl.when(i == 0)
        def _():
            dkv_out[...] = dkv

        @pl.when(i > 0)
        def _():
            dkv_out[...] += dkv

    row = lambda wd: pl.BlockSpec((bm, wd), lambda i: (i, 0))
    return _pallas(
        body, name=name, grid=(s // bm,),
        in_specs=[row(d), row(d), row(X_W), pl.BlockSpec((mem, 2 * X_W), lambda i: (0, 0)),
                  pl.BlockSpec((X_W, d), lambda i: (0, 0)), pl.BlockSpec((d, X_W), lambda i: (0, 0))],
        out_specs=[row(d), row(X_W), pl.BlockSpec((mem, 2 * X_W), lambda i: (0, 0))],
        out_shape=[_sds((s, d), F32), _sds((s, X_W), BF16), _sds((mem, 2 * X_W), F32)],
        compiler_params=_params(dimension_semantics=("arbitrary",)),
    )(dyb, drf, q, kv, wo, wq)


def _gate_bwd(name, dr1b, w_o, sa, sb, ya, yb, d, after=()):
    s = dr1b.shape[0]
    bm, bn = _tile(s, 1024), _tile(d, 512)
    nj = d // bn

    def body(a_ref, w_ref, sa_ref, sb_ref, ya_ref, yb_ref, dya_ref, dyb_ref, dg_ref, dba_ref, dbb_ref):
        i = pl.program_id(1)
        dm = _dot(a_ref[...], w_ref[...], NT)
        sa = sa_ref[...].astype(F32)
        sb = sb_ref[...].astype(F32)
        dya_ref[...] = (dm * sa).astype(BF16)
        dyb_ref[...] = (dm * sb).astype(BF16)
        dga = dm * ya_ref[...].astype(F32) * (sa * (1.0 - sa))
        dgb = dm * yb_ref[...].astype(F32) * (sb * (1.0 - sb))
        dg_ref[0] = dga.astype(BF16)
        dg_ref[1] = dgb.astype(BF16)
        sa_sum = jnp.sum(dga, axis=0, keepdims=True)
        sb_sum = jnp.sum(dgb, axis=0, keepdims=True)

        @pl.when(i == 0)
        def _():
            dba_ref[...] = sa_sum
            dbb_ref[...] = sb_sum

        @pl.when(i > 0)
        def _():
            dba_ref[...] += sa_sum
            dbb_ref[...] += sb_sum

    tile = pl.BlockSpec((bm, bn), lambda j, i: (i, j))
    return _pallas(
        body, after=after, name=name, grid=(nj, s // bm),
        in_specs=[pl.BlockSpec((bm, d), lambda j, i: (i, 0)), pl.BlockSpec((bn, d), lambda j, i: (j, 0)),
                  tile, tile, tile, tile],
        out_specs=[tile, tile, pl.BlockSpec((2, bm, bn), lambda j, i: (0, i, j)),
                   pl.BlockSpec((1, bn), lambda j, i: (0, j)), pl.BlockSpec((1, bn), lambda j, i: (0, j))],
        out_shape=[_sds((s, d), BF16), _sds((s, d), BF16), _sds((2, s, d), BF16), _sds((1, d), F32),
                   _sds((1, d), F32)],
        compiler_params=_params(dimension_semantics=("arbitrary", "arbitrary")),
    )(dr1b, w_o, sa, sb, ya, yb)


def _gmlp_bwd(name, proj, dsg, ln_g, ln_b, w_s, b_st):
    s = proj.shape[0]

    def body(u_ref, v_ref, dsg_ref, g_ref, b_ref, ws_ref, bst_ref, duv_ref, dws_ref, dbst_ref, dlg_ref, dlb_ref):
        n = pl.program_id(0)
        u, v = u_ref[...], v_ref[...]
        gu, tu = _gelu(u)
        gv, tv = _gelu(v)
        gam = g_ref[...]
        vn, xhat, rstd = _ln_rows(gv, gam, b_ref[...])
        vnb = vn.astype(BF16)
        dsg = dsg_ref[...].astype(F32)
        row = lax.broadcasted_iota(jnp.int32, (CHUNK, CHUNK), 0)
        col = lax.broadcasted_iota(jnp.int32, (CHUNK, CHUNK), 1)
        tril = col <= row
        dgu, dvn, dws, dbs = [], [], [], []
        for g in range(GROUPS):
            sl = slice(g * LANE, (g + 1) * LANE)
            w = jnp.where(tril, ws_ref[g], 0.0).astype(BF16)
            mixed = _dot(w, vnb[:, sl], NN) + bst_ref[:, g:g + 1]
            dgu.append(dsg[:, sl] * mixed)
            dmx = dsg[:, sl] * gu[:, sl]
            dmxb = dmx.astype(BF16)
            dbs.append(jnp.sum(dmx, axis=-1, keepdims=True))
            dws.append(jnp.where(tril, _dot(dmxb, vnb[:, sl], NT), 0.0))
            dvn.append(_dot(w, dmxb, TN))
        dvn = jnp.concatenate(dvn, axis=-1)
        dgu = jnp.concatenate(dgu, axis=-1)
        dxh = dvn * gam
        m1 = jnp.mean(dxh, axis=-1, keepdims=True)
        m2 = jnp.mean(dxh * xhat, axis=-1, keepdims=True)
        dgv = rstd * (dxh - m1 - xhat * m2)
        du = dgu * _gelu_grad(u, tu)
        dv = dgv * _gelu_grad(v, tv)
        duv_ref[...] = jnp.concatenate([du, dv], axis=-1).astype(BF16)
        dlg = jnp.sum(dvn * xhat, axis=0, keepdims=True)
        dlb = jnp.sum(dvn, axis=0, keepdims=True)
        dbst = jnp.concatenate(dbs, axis=-1)

        @pl.when(n == 0)
        def _():
            for g in range(GROUPS):
                dws_ref[g] = dws[g]
            dbst_ref[...] = dbst
            dlg_ref[...] = dlg
            dlb_ref[...] = dlb

        @pl.when(n > 0)
        def _():
            for g in range(GROUPS):
                dws_ref[g] += dws[g]
            dbst_ref[...] += dbst
            dlg_ref[...] += dlg
            dlb_ref[...] += dlb

    vec = pl.BlockSpec((1, GMLP_W), lambda n: (0, 0))
    return _pallas(
        body, name=name, grid=(s // CHUNK,),
        in_specs=[pl.BlockSpec((CHUNK, GMLP_W), lambda n: (n, 0)), pl.BlockSpec((CHUNK, GMLP_W), lambda n: (n, 1)),
                  pl.BlockSpec((CHUNK, GMLP_W), lambda n: (n, 0)), vec, vec,
                  pl.BlockSpec((GROUPS, CHUNK, CHUNK), lambda n: (0, 0, 0)),
                  pl.BlockSpec((CHUNK, GROUPS), lambda n: (0, 0))],
        out_specs=[pl.BlockSpec((CHUNK, 2 * GMLP_W), lambda n: (n, 0)),
                   pl.BlockSpec((GROUPS, CHUNK, CHUNK), lambda n: (0, 0, 0)),
                   pl.BlockSpec((CHUNK, GROUPS), lambda n: (0, 0)), vec, vec],
        out_shape=[_sds((s, 2 * GMLP_W), BF16), _sds((GROUPS, CHUNK, CHUNK), F32), _sds((CHUNK, GROUPS), F32),
                   _sds((1, GMLP_W), F32), _sds((1, GMLP_W), F32)],
        compiler_params=_params(dimension_semantics=("arbitrary",)),
    )(proj, proj, dsg, ln_g, ln_b, w_s, b_st)


def _swa_bwd(name, qr, kr, proj, do, o, lse, sinks, cos4, nsin4, after=()):
    s = qr.shape[0]
    w = CHUNK
    nblk = s // w
    scale = HD ** -0.5
    grp = NQ // NKV

    def body(qj_ref, qn_ref, kj_ref, kp_ref, vj_ref, vp_ref, doj_ref, don_ref, oj_ref, on_ref, lj_ref, ln_ref,
             sink_ref, cos_ref, sin_ref, out_ref, dsink_ref):
        j = pl.program_id(0)
        qj, qn = qj_ref[...].astype(F32), qn_ref[...].astype(F32)
        doj, don = doj_ref[...].astype(F32), don_ref[...].astype(F32)
        kk = jnp.concatenate([kp_ref[...], kj_ref[...]], axis=0)
        vv = jnp.concatenate([vp_ref[...], vj_ref[...]], axis=0).astype(BF16)
        lj, lnx = lj_ref[...], ln_ref[...]
        prod_j = doj * oj_ref[...].astype(F32)
        prod_n = don * on_ref[...].astype(F32)
        valid_j = _band_mask(j > 0)
        valid_n = _band_mask(j + 1 < nblk, prev_only=True)
        lo = lax.broadcasted_iota(jnp.int32, (w, LANE), 1) < HD
        dqs, dsk, dk_g, dv_g = [], [], [], []
        for g in range(NKV):
            kz, vz = _own_head(kk, g), _own_head(vv, g)
            kz_c, vz_c = kz[w:], vz[w:]
            qg_j, qg_n = _stack_heads(qj, g), _stack_heads(qn, g)
            dog_j, dog_n = _stack_heads(doj, g), _stack_heads(don, g)
            l_j, l_n = _stack_rows(lj, g), _stack_rows(lnx, g)
            d_j = _head_lane_sums(_stack_heads(prod_j, g, F32), g)
            d_n = _head_lane_sums(_stack_heads(prod_n, g, F32), g)
            p = jnp.where(valid_j, jnp.exp(_dot(kz, qg_j, NT) * scale - l_j), 0.0)
            ds = (p * (_dot(vz, dog_j, NT) - d_j) * scale).astype(BF16)
            dqs.extend(_unstack_heads(_dot(ds, kz, TN), g))
            p2 = jnp.where(valid_n, jnp.exp(_dot(kz_c, qg_n, NT) * scale - l_n), 0.0)
            ds2 = (p2 * (_dot(vz_c, dog_n, NT) - d_n) * scale).astype(BF16)
            dk_g.append(_dot(ds[w:], qg_j, NN) + _dot(ds2, qg_n, NN))
            dv_g.append(_dot(p[w:].astype(BF16), dog_j, NN) + _dot(p2.astype(BF16), dog_n, NN))
            t = jnp.exp(sink_ref[g] - l_j) * d_j
            dsk.extend([-jnp.sum(t[:, h * w:(h + 1) * w], axis=-1, keepdims=True) for h in range(GRP)])
        cos, nsin = cos_ref[...], sin_ref[...]
        dq = _rope(jnp.concatenate(dqs, axis=-1), cos, nsin)
        dk = _rope(jnp.concatenate([jnp.where(lo, dk_g[2 * m], dk_g[2 * m + 1]) for m in range(NKV // 2)], axis=-1),
                   cos, nsin)
        dv = jnp.concatenate([jnp.where(lo, dv_g[2 * m], dv_g[2 * m + 1]) for m in range(NKV // 2)], axis=-1)
        out_ref[...] = jnp.concatenate([dq, dk, dv], axis=-1).astype(BF16)
        dsink = jnp.concatenate(dsk, axis=-1)

        @pl.when(j == 0)
        def _():
            dsink_ref[...] = dsink

        @pl.when(j > 0)
        def _():
            dsink_ref[...] += dsink

    nxt = lambda j: jnp.minimum(j + 1, nblk - 1)
    prv = lambda j: jnp.maximum(j - 1, 0)
    va = OFF_VA // KV_W
    return _pallas(
        body, after=after, name=name, grid=(nblk,),
        in_specs=[pl.BlockSpec((w, ATT_W), lambda j: (j, 0)), pl.BlockSpec((w, ATT_W), lambda j: (nxt(j), 0)),
                  pl.BlockSpec((w, KV_W), lambda j: (j, 0)), pl.BlockSpec((w, KV_W), lambda j: (prv(j), 0)),
                  pl.BlockSpec((w, KV_W), lambda j: (j, va)), pl.BlockSpec((w, KV_W), lambda j: (prv(j), va)),
                  pl.BlockSpec((w, ATT_W), lambda j: (j, 0)), pl.BlockSpec((w, ATT_W), lambda j: (nxt(j), 0)),
                  pl.BlockSpec((w, ATT_W), lambda j: (j, 0)), pl.BlockSpec((w, ATT_W), lambda j: (nxt(j), 0)),
                  pl.BlockSpec((None, NQ, w), lambda j: (j, 0, 0)),
                  pl.BlockSpec((None, NQ, w), lambda j: (nxt(j), 0, 0)),
                  pl.BlockSpec((NKV, 1, GRP * w), lambda j: (0, 0, 0)),
                  pl.BlockSpec((w, LANE), lambda j: (j, 0)), pl.BlockSpec((w, LANE), lambda j: (j, 0))],
        out_specs=[pl.BlockSpec((w, ATT_W + 2 * KV_W), lambda j: (j, 0)), pl.BlockSpec((1, NQ), lambda j: (0, 0))],
        out_shape=[_sds((s, ATT_W + 2 * KV_W), BF16), _sds((1, NQ), F32)],
        compiler_params=_params(dimension_semantics=("arbitrary",)),
    )(qr, qr, kr, kr, proj, proj, do, do, o, o, lse, lse, sinks, cos4, nsin4)


def _mm_nn(name, a, w, *, out_dtypes, epilogue=_store, bm_pref=1024, bn_pref=1024, after=()):
    m, k = a.shape
    n = w.shape[-1]
    bm, bn = _tile(m, bm_pref), _tile(n, bn_pref)
    tile = pl.BlockSpec((bm, bn), lambda i, j, kk: (i, j))
    return _mm(name, a, w, dims=NN, grid=(m // bm, n // bn, 1),
               a_spec=pl.BlockSpec((bm, k), lambda i, j, kk: (i, 0)),
               b_spec=pl.BlockSpec((k, bn), lambda i, j, kk: (0, j)),
               out_shape=[_sds((m, n), dt) for dt in out_dtypes], out_specs=[tile] * len(out_dtypes),
               epilogue=epilogue, after=after)


def _dw_half(name, a, b, axis, hidx, got=None, after=()):
    s, m = a.shape
    n = b.shape[-1]
    mh, nh = (m // 2, n) if axis == 1 else (m, n // 2)
    bm, bn = _tile(mh, 1024), _tile(nh, 1024)
    nmb, nnb = mh // bm, nh // bn
    if axis == 1:
        a_spec = pl.BlockSpec((s, bm), lambda i, j, h: (0, h[0] * nmb + i))
        b_spec = pl.BlockSpec((s, bn), lambda i, j, h: (0, j))
    else:
        a_spec = pl.BlockSpec((s, bm), lambda i, j, h: (0, i))
        b_spec = pl.BlockSpec((s, bn), lambda i, j, h: (0, h[0] * nnb + j))
    tile = pl.BlockSpec((bm, bn), lambda i, j, h: (i, j))
    n_got, n_after = (0 if got is None else 1), len(after)

    def body(h_ref, a_ref, b_ref, *rest):
        acc = _dot(a_ref[...], b_ref[...], TN)
        if n_got:
            acc = acc + rest[0][...].astype(F32)
        rest[-1][...] = acc.astype(BF16)

    return pl.pallas_call(
        body, name=name,
        grid_spec=pltpu.PrefetchScalarGridSpec(
            num_scalar_prefetch=1, grid=(nmb, nnb),
            in_specs=[a_spec, b_spec] + [tile] * n_got + [ANY] * n_after, out_specs=tile),
        out_shape=_sds((mh, nh), BF16), compiler_params=_params(dimension_semantics=("arbitrary", "arbitrary")),
    )(hidx, a, b, *(() if got is None else (got,)), *after)


def _mm_residual(name, a, w, x, after=()):
    s, k = a.shape
    d = w.shape[-1]
    bm, bn = _tile(s, 1024), _tile(d, 1024 if k <= 2048 else 512)
    tile = pl.BlockSpec((bm, bn), lambda i, j, kk: (i, j))
    return _mm(name, a, w, dims=NN, grid=(s // bm, d // bn, 1),
               a_spec=pl.BlockSpec((bm, k), lambda i, j, kk: (i, 0)),
               b_spec=pl.BlockSpec((k, bn), lambda i, j, kk: (0, j)),
               extras=(x,), extra_specs=(tile,), out_shape=[_sds((s, d), F32)], out_specs=[tile],
               epilogue=_ep_add_scaled, after=after)[0]


def _dw_pieces_half(name, a, pieces, widths, hidx, got=None, after=()):
    s, m = a.shape
    total = sum(widths)
    mh = m // 2
    bm, bn = _tile(mh, 1024), 512
    nmb = mh // bm
    n_got = 0 if got is None else 1
    out, off = None, 0
    for p, (piece, wd) in enumerate(zip(pieces, widths)):
        if isinstance(piece, tuple):
            arr = piece[0]
            b_spec = pl.BlockSpec((None, s, bn), (lambda ix: lambda i, j, h: (ix, 0, j))(piece[1]))
        else:
            arr, b_spec = piece, pl.BlockSpec((s, bn), lambda i, j, h: (0, j))
        tile = pl.BlockSpec((bm, bn), (lambda c: lambda i, j, h: (i, c + j))(off // bn))
        prev = () if out is None else (out,)
        first_after = after if out is None else ()

        def body(h_ref, a_ref, b_ref, *rest):
            acc = _dot(a_ref[...], b_ref[...], TN)
            if n_got:
                acc = acc + rest[0][...].astype(F32)
            rest[-1][...] = acc.astype(BF16)

        out = pl.pallas_call(
            body, name=f"{name}_{p}",
            grid_spec=pltpu.PrefetchScalarGridSpec(
                num_scalar_prefetch=1, grid=(nmb, wd // bn),
                in_specs=[pl.BlockSpec((s, bm), lambda i, j, h: (0, h[0] * nmb + i)), b_spec] + [tile] * n_got
                + [ANY] * (len(prev) + len(first_after)), out_specs=tile),
            out_shape=_sds((mh, total), BF16), input_output_aliases={3 + n_got: 0} if prev else {},
            compiler_params=_params(dimension_semantics=("arbitrary", "arbitrary")),
        )(hidx, a, arr, *(() if got is None else (got,)), *prev, *first_after)
        off += wd
    return out


def _dx_pieces(name, pieces, widths, w, dr, after=()):
    s, d = dr.shape
    iw = w.shape[-1]
    bm, bn = _tile(s, 1024), _tile(d, 512)
    arrs, specs = [], []
    for piece, wd in zip(pieces, widths):
        if isinstance(piece, tuple):
            arrs.append(piece[0])
            specs.append(pl.BlockSpec((None, bm, wd), (lambda idx: lambda i, j: (idx, i, 0))(piece[1])))
        else:
            arrs.append(piece)
            specs.append(pl.BlockSpec((bm, wd), lambda i, j: (i, 0)))
    n = len(arrs)

    def body(*refs):
        w_ref, dr_ref, o_ref = refs[n], refs[n + 1], refs[n + 2]
        acc = ALPHA * dr_ref[...]
        off = 0
        for p, wd in enumerate(widths):
            acc = acc + _dot(refs[p][...], w_ref[:, off:off + wd], NT)
            off += wd
        o_ref[...] = acc

    tile = pl.BlockSpec((bm, bn), lambda i, j: (i, j))
    return _pallas(
        body, after=after, name=name, grid=(s // bm, d // bn),
        in_specs=specs + [pl.BlockSpec((bn, iw), lambda i, j: (j, 0)), tile], out_specs=tile,
        out_shape=_sds((s, d), F32), compiler_params=_params(dimension_semantics=("arbitrary", "arbitrary")),
    )(*arrs, w, dr)


class _Gather:
    def __init__(self, tag, names, fulls, after):
        self.tag, self.names = tag, names
        self.axes = [SHARD_AXIS[n] for n in names]
        self.srcs = []
        self.mk1 = _mk_gather_ici(self.axes)
        self.mk2 = _mk_gather_d2d(self.axes)
        self.n_sem = 3 * len(names)
        self.s1, self.r1, self.lands, self.token = _split_start(
            tag + "_ici_start", self.srcs, [fulls[n] for n in names], self.mk1, self.n_sem, after)

    def forward(self, after=()):
        lands = _split_wait(self.tag + "_ici_wait", self.srcs, self.lands, self.s1, self.r1, self.mk1, after)
        self.s2, self.r2, self.lands, tok = _split_start(self.tag + "_d2d_start", [], lands, self.mk2, self.n_sem)
        return tok

    def done(self, after=()):
        lands = _split_wait(self.tag + "_d2d_wait", [], self.lands, self.s2, self.r2, self.mk2, after)
        return dict(zip(self.names, lands))


class _Reduce:
    def __init__(self, tag, names, parts, mcidx, after=()):
        self.tag, self.names, self.mcidx = tag, names, mcidx
        self.axes = [SHARD_AXIS[n] for n in names]
        self.parts = [parts[n] for n in names]
        self.mk = _mk_swap
        lands = [lax.empty(p.shape, BF16) for p in self.parts]
        self.s, self.r, self.lands, self.token = _split_start(
            tag + "_swap_start", self.parts, lands, self.mk, len(names), after)

    def scatter(self, own, after=()):
        got = _split_wait(self.tag + "_swap_wait", self.parts, self.lands, self.s, self.r, self.mk, after)
        sums = own(dict(zip(self.names, got)))
        self.sums = [sums[n] for n in self.names]
        self.mk = _mk_scatter(self.axes)
        lands = []
        for q, ax in zip(self.sums, self.axes):
            k, n = q.shape
            lands.append(lax.empty((3, k // 4, n) if ax == 0 else (3, k, n // 4), BF16))
        self.s, self.r, self.lands, tok = _split_start(
            self.tag + "_scatter_start", self.sums, lands, self.mk, 3 * len(self.names))
        return tok

    def exchange(self, after=()):
        slots = _split_wait(self.tag + "_scatter_wait", self.sums, self.lands, self.s, self.r, self.mk, after)
        halves = [_sum_half(f"{self.tag}_sum_{n}", q, sl, ax, self.mcidx)
                  for n, q, sl, ax in zip(self.names, self.sums, slots, self.axes)]
        self.mk = _mk_exchange(self.axes)
        self.s, self.r, self.lands, tok = _split_start(
            self.tag + "_exchange_start", [], halves, self.mk, len(self.names))
        return tok

    def done(self, after=()):
        grads = _split_wait(self.tag + "_exchange_wait", [], self.lands, self.s, self.r, self.mk, after)
        return dict(zip(self.names, grads))


def _pack(arrs):
    flat = jnp.concatenate([a.reshape(-1) for a in arrs])
    n = flat.shape[0]
    pad = (-n) % (8 * LANE)
    return jnp.pad(flat, (0, pad)).reshape(-1, LANE)


def _unpack(packed, shapes):
    flat = packed.reshape(-1)
    out, off = [], 0
    for sh in shapes:
        n = math.prod(sh)
        out.append(flat[off:off + n].reshape(sh))
        off += n
    return out


def kernel(x, mem, w_in, b_gate, ln_v_g, ln_v_b, w_s, b_s, sinks, w_br_a, w_br_b, w_o, ln1_g, ln1_b, w_xq, w_xkv, w_xo, ln2_g, ln2_b, w_up, w_down, ln3_g, ln3_b, loss_target, m_w_in, m_b_gate, m_ln_v_g, m_ln_v_b, m_w_s, m_b_s, m_sinks, m_w_br_a, m_w_br_b, m_w_o, m_ln1_g, m_ln1_b, m_w_xq, m_w_xkv, m_w_xo, m_ln2_g, m_ln2_b, m_w_up, m_w_down, m_ln3_g, m_ln3_b, v_w_in, v_b_gate, v_ln_v_g, v_ln_v_b, v_w_s, v_b_s, v_sinks, v_w_br_a, v_w_br_b, v_w_o, v_ln1_g, v_ln1_b, v_w_xq, v_w_xkv, v_w_xo, v_ln2_g, v_ln2_b, v_w_up, v_w_down, v_ln3_g, v_ln3_b):
    env = dict(locals())
    wts = {n: env[n] for n in WEIGHTS}
    mom_m = {n: env["m_" + n] for n in WEIGHTS}
    mom_v = {n: env["v_" + n] for n in WEIGHTS}
    s, d = x.shape[1], x.shape[2]
    dff = 4 * w_up.shape[-1]
    xf = x.reshape(s, d)
    tgt = loss_target.reshape(s, d)
    memf = mem.reshape(mem.shape[1], d)
    ax_x, ax_y, ax_c = lax.axis_index("x"), lax.axis_index("y"), lax.axis_index("c")
    meidx = jnp.reshape(2 * ax_x + ax_y, (1,)).astype(jnp.int32)
    cidx = jnp.reshape(ax_c, (1,)).astype(jnp.int32)
    sidx = 1 - cidx
    mcidx = jnp.concatenate([meidx, cidx])

    inv = 1.0 / (10000.0 ** (jnp.arange(0, HD, 2, dtype=F32) / HD))
    ang = jnp.arange(s, dtype=F32)[:, None] * inv[None, :]
    cos, sin = jnp.cos(ang), jnp.sin(ang)
    cos4 = jnp.tile(cos, (1, 4))
    sin4 = jnp.concatenate([-sin, sin, -sin, sin], axis=-1)
    nsin4 = -sin4

    small = {}
    for n in SMALL:
        w = wts[n]
        if n == "w_s":
            small[n] = [w[l] for l in range(DEPTH)]
        elif n == "b_s":
            small["b_st"] = [w[l].T for l in range(DEPTH)]
        else:
            small[n] = [w[l][None, :] for l in range(DEPTH)]
    small["sink_rows"] = [jnp.repeat(sinks[l].reshape(NKV, GRP), CHUNK, axis=1)[:, None, :] for l in range(DEPTH)]

    fulls = [{}, {}]
    tok = ()
    gathers = [[None] * len(GROUPS_GATHER) for _ in range(DEPTH)]
    for gi, names in enumerate(GROUPS_GATHER):
        for n in names:
            fulls[0][n], fulls[1][n] = _place_own("place_" + n, wts[n], SHARD_AXIS[n], meidx, after=tok)
        gathers[0][gi] = _Gather(f"ag0_{gi}", names, fulls[0], tok)
        tok = (gathers[0][gi].token,)
    for gi, names in enumerate(GROUPS_GATHER):
        gathers[1][gi] = _Gather(f"ag1_{gi}", names, fulls[1], tok)
        tok = (gathers[1][gi].token,)

    xb = _cast2d("cast_x", xf, after=tok)
    memb = _cast2d("cast_mem", memf, after=tok)

    saved = []
    hf, hb = xf, xb
    nxt_tok = gathers[0][0].forward(after=tok)
    for l in range(DEPTH):
        t = f"l{l}_"
        ga, gb, gc, gd = gathers[l]
        full = ga.done(after=(nxt_tok, hb))
        sv = {"xf": hf, "xb": hb}
        proj = _mm_nn(t + "proj", hb, full["w_in"], out_dtypes=[F32], bn_pref=1280)[0]
        tok_b = gb.forward(after=(proj,))
        sg = _gmlp_fwd(t + "gmlp_fwd", proj, small["ln_v_g"][l], small["ln_v_b"][l], small["w_s"][l],
                       small["b_st"][l])
        attn, qr, kr, lse = _swa_fwd(t + "swa_fwd", proj, cos4, sin4, small["sink_rows"][l], after=(tok_b,))
        full.update(gb.done(after=(attn,)))
        merged, ya, yb, sa, sb = _gate_fwd(t + "gate_fwd", sg, attn, full["w_br_a"], full["w_br_b"], proj,
                                           small["b_gate"][l], d)
        sv.update(sa=sa, sb=sb)
        tok_c = gc.forward(after=(merged,))
        r1 = _mm_residual(t + "o", merged, full["w_o"], hf, after=(tok_c,))
        x1, x1b = _ln_fwd(t + "ln1", r1, small["ln1_g"][l], small["ln1_b"][l])
        kv = _mm_nn(t + "xkv", memb, full["w_xkv"], out_dtypes=[BF16])[0]
        q, o, r2, x2, x2b = _xattn_fwd(t + "xattn_fwd", x1b, x1, full["w_xq"], kv, full["w_xo"],
                                       small["ln2_g"][l], small["ln2_b"][l])
        full.update(gc.done(after=(x2b,)))
        tok_d = gd.forward(after=(x2b,))

        def ep_up(acc, ex, outs):
            outs[0][...] = acc.astype(BF16)
            rl = jnp.maximum(acc, 0.0)
            outs[1][...] = (rl * rl).astype(BF16)

        h, a = _mm_nn(t + "up", x2b, full["w_up"], out_dtypes=[BF16, BF16], epilogue=ep_up, after=(tok_d,))
        full.update(gd.done(after=(h,)))
        nxt_tok = gathers[l + 1][0].forward(after=(h,)) if l + 1 < DEPTH else None
        r3 = _mm_residual(t + "down", a, full["w_down"], x2, after=() if nxt_tok is None else (nxt_tok,))
        sv.update(proj=proj, sg=sg, attn=attn, qr=qr, kr=kr, lse=lse, merged=merged, ya=ya, yb=yb, r1=r1, x1=x1,
                  x1b=x1b, kv=kv, q=q, o=o, r2=r2, x2b=x2b, h=h, a=a, r3=r3, full=full)
        saved.append(sv)
        if l + 1 < DEPTH:
            hf, hb = _ln_fwd(t + "ln3", r3, small["ln3_g"][l], small["ln3_b"][l])

    small_g = [None] * DEPTH
    grads = [{}, {}]
    pend_a = None
    pend_b = None
    for l in reversed(range(DEPTH)):
        t = f"l{l}_"
        sv = saved[l]
        full = sv["full"]
        sgo = {}
        if l == DEPTH - 1:
            dr3, dr3b, sgo["ln3_g"], sgo["ln3_b"], loss11 = _loss_ln_bwd(
                "loss_ln3_bwd", sv["r3"], small["ln3_g"][l], small["ln3_b"][l], tgt)
            loss = lax.psum(loss11[0, 0], ("x", "y", "c"))
        else:
            dr3, dr3b, sgo["ln3_g"], sgo["ln3_b"] = _ln_bwd(t + "ln3_bwd", g, sv["r3"], small["ln3_g"][l],
                                                            after=(tok_a,))
        bm, bn = _tile(s, 1024), _tile(dff, 1024)

        def ep_dh(acc, ex, outs):
            outs[0][...] = (acc * (2.0 * jnp.maximum(ex[0][...].astype(F32), 0.0))).astype(BF16)

        tile = pl.BlockSpec((bm, bn), lambda i, j, k: (i, j))
        dh = _mm(t + "dh", dr3b, full["w_down"], dims=NT, grid=(s // bm, dff // bn, 1),
                 a_spec=pl.BlockSpec((bm, d), lambda i, j, k: (i, 0)),
                 b_spec=pl.BlockSpec((bn, d), lambda i, j, k: (j, 0)),
                 extras=(sv["h"],), extra_specs=(tile,), out_shape=[_sds((s, dff), BF16)], out_specs=[tile],
                 epilogue=ep_dh)[0]
        if pend_a is not None:
            tok_pa = pend_a.exchange(after=(dh,))
            grads[l + 1].update(pend_b.done(after=(dh,)))

        def halves(specs, hidx, got=None, after=()):
            out = {}
            for i, (n, a_, b_) in enumerate(specs):
                out[n] = _dw_half(f"{t}d{n}_{'s' if got is None else 'o'}", a_, b_, SHARD_AXIS[n], hidx,
                                  None if got is None else got[n], after if i == 0 else ())
            return out

        specs_c = [("w_down", sv["a"], dr3b), ("w_up", sv["x2b"], dh)]
        red_c = _Reduce(t + "rs_c", GROUPS_FWD[2],
                        halves(specs_c, sidx, after=() if pend_a is None else (tok_pa,)), mcidx)
        bm2, bn2 = _tile(s, 1024), _tile(d, 512)
        tile2 = pl.BlockSpec((bm2, bn2), lambda i, j, k: (i, j))
        dx2 = _mm(t + "dx2", dh, full["w_up"], dims=NT, grid=(s // bm2, d // bn2, 1),
                  a_spec=pl.BlockSpec((bm2, dff), lambda i, j, k: (i, 0)),
                  b_spec=pl.BlockSpec((bn2, dff), lambda i, j, k: (j, 0)),
                  extras=(dr3,), extra_specs=(tile2,), out_shape=[_sds((s, d), F32)], out_specs=[tile2],
                  epilogue=_ep_add_scaled, after=(red_c.token,))[0]
        tok_c = red_c.scatter(lambda got: halves(specs_c, cidx, got), after=(dx2,))
        if pend_a is not None:
            grads[l + 1].update(pend_a.done(after=(dx2,)))
            pend_a = None

        dr2, dr2b, sgo["ln2_g"], sgo["ln2_b"] = _ln_bwd(t + "ln2_bwd", dx2, sv["r2"], small["ln2_g"][l],
                                                        after=(tok_c,))
        dx1, dq, dkv = _xattn_bwd(t + "xattn_bwd", dr2b, dr2, sv["q"], sv["kv"], full["w_xo"], full["w_xq"])

        dr1, dr1b, sgo["ln1_g"], sgo["ln1_b"] = _ln_bwd(t + "ln1_bwd", dx1, sv["r1"], small["ln1_g"][l])
        dya, dyb, dgate, dba, dbb = _gate_bwd(t + "gate_bwd", dr1b, full["w_o"], sv["sa"], sv["sb"], sv["ya"],
                                              sv["yb"], d)
        sgo["b_gate"] = jnp.concatenate([dba, dbb], axis=-1)
        specs_b = [("w_o", sv["merged"], dr1b), ("w_br_a", sv["sg"], dya), ("w_br_b", sv["attn"], dyb),
                   ("w_xo", sv["o"], dr2b), ("w_xq", sv["x1b"], dq),
                   ("w_xkv", memb, _cast2d(t + "dkv_cast", dkv))]
        red_b = _Reduce(t + "rs_b", GROUPS_FWD[1], halves(specs_b, sidx), mcidx)

        def dbranch(name, dyx, w, after):
            return _mm(name, dyx, w, dims=NT, grid=(s // bm, 1, 1),
                       a_spec=pl.BlockSpec((bm, d), lambda i, j, k: (i, 0)),
                       b_spec=pl.BlockSpec((w.shape[0], d), lambda i, j, k: (0, 0)),
                       out_shape=[_sds((s, w.shape[0]), BF16)],
                       out_specs=[pl.BlockSpec((bm, w.shape[0]), lambda i, j, k: (i, 0))],
                       epilogue=_store, after=after)[0]

        dsg = dbranch(t + "dsg", dya, full["w_br_a"], (red_b.token,))
        dattn = dbranch(t + "dattn", dyb, full["w_br_b"], ())
        tok_c = red_c.exchange(after=(dattn, dsg))
        tok_b = red_b.scatter(lambda got: halves(specs_b, cidx, got), after=(dattn, dsg))
        duv, sgo["w_s"], dbst, dlg, dlb = _gmlp_bwd(t + "gmlp_bwd", sv["proj"], dsg, small["ln_v_g"][l],
                                                    small["ln_v_b"][l], small["w_s"][l], small["b_st"][l])
        sgo["b_s"] = dbst.T
        sgo["ln_v_g"], sgo["ln_v_b"] = dlg, dlb
        dqkv, sgo["sinks"] = _swa_bwd(t + "swa_bwd", sv["qr"], sv["kr"], sv["proj"], dattn, sv["attn"], sv["lse"],
                                      small["sink_rows"][l], cos4, nsin4, after=(tok_b, tok_c))
        grads[l].update(red_c.done(after=(dqkv,)))
        pieces = (duv, dqkv, (dgate, 0), (dgate, 1))
        widths = (2 * GMLP_W, ATT_W + 2 * KV_W, d, d)
        tok_b = red_b.exchange(after=(dqkv, duv))
        red_a = _Reduce(t + "rs_a", GROUPS_FWD[0],
                        {"w_in": _dw_pieces_half(t + "dw_in_s", sv["xb"], pieces, widths, sidx, after=(tok_b,))},
                        mcidx)
        g = _dx_pieces(t + "dx0", pieces, widths, full["w_in"], dr1, after=(red_a.token,))
        tok_a = red_a.scatter(
            lambda got: {"w_in": _dw_pieces_half(t + "dw_in_o", sv["xb"], pieces, widths, cidx, got["w_in"])},
            after=(g,))
        pend_a, pend_b = red_a, red_b
        small_g[l] = sgo
    grad_x = g.reshape(x.shape)

    big_out = {}

    def adam_layer(l, names, after):
        done = []
        for n in names:
            prev = big_out.get(n)
            big_out[n] = _adamw(f"adamw{l}_{n}", wts[n], grads[l][n], mom_m[n], mom_v[n], l, prev, after=after)
            done.append(big_out[n][0])
        return done

    shapes = [wts[n].shape for n in SMALL]
    packed_g = _pack([jnp.stack([small_g[l][n].reshape(wts[n].shape[1:]) for l in range(DEPTH)]) for n in SMALL])
    me8 = jnp.reshape(4 * ax_x + 2 * ax_y + ax_c, (1,)).astype(jnp.int32)
    ar_s, ar_r, ar_land, tok_ar = _split_start("ar_start", [packed_g], [_place_slot("ar_place", packed_g, me8)],
                                               _mk_small, 7, after=(tok_a,))
    fill = []
    for names in GROUPS_FWD:
        fill += adam_layer(1, names, (tok_ar,))
    grads[0].update(pend_b.done(after=tuple(fill)))
    fill += adam_layer(0, GROUPS_FWD[1], (tok_ar,))
    ar_land = _split_wait("ar_wait", [packed_g], ar_land, ar_s, ar_r, _mk_small, after=tuple(fill))
    packed_g = _sum_slots("ar_sum", ar_land[0])
    pw, pm, pv = (_pack([src[n] for n in SMALL]) for src in (wts, mom_m, mom_v))
    small4 = _adamw("adamw_small", pw[None], packed_g, pm[None], pv[None], 0)
    small_out = [dict(zip(SMALL, _unpack(a[0], shapes))) for a in small4]
    tok_a = pend_a.exchange(after=(small4[0],))
    fill = adam_layer(0, GROUPS_FWD[2], (tok_a,))
    grads[0].update(pend_a.done(after=tuple(fill)))
    adam_layer(0, GROUPS_FWD[0], ())

    def pick(kind, n):
        return big_out[n][kind] if n in big_out else small_out[kind][n]

    return (loss, grad_x, *[pick(0, n) for n in WEIGHTS], *[pick(1, n) for n in WEIGHTS],
            *[pick(2, n) for n in WEIGHTS], *[pick(3, n) for n in WEIGHTS])
```
